```python
import math
import jax
import jax.numpy as jnp
from jax import lax
import numpy as np

D_MODEL = 1024
BATCH = 8
SEQ = 16384
DEPTH = 2

MEM_LEN = 256
EPS = 1e-6
MASK_VALUE = -1e30
TINY = 1e-30

A_HEADS = 8
A_NOPE = 64
A_ROPE = 32
A_V = 64
A_Q_RANK = 384
A_KV_RANK = 256
A_QBLOCK = 128
ROPE_THETA = 10000.0

B_HEADS = 8
B_DK = 128
B_DV = 64
B_CHUNK = 16

C_HEADS = 8
C_KV_HEADS = 2
C_DH = 64
C_WINDOW = 128
C_BLOCK = 128

REL_BUCKETS = 32
REL_MAX_DIST = 128

X_HEADS = 4
X_DH = 256

D_FF = -(-(8 * D_MODEL) // (3 * 256)) * 256

IN_SPLITS = (
    A_Q_RANK, A_KV_RANK, A_ROPE,
    B_HEADS * B_DK, B_HEADS * B_DK, B_HEADS * B_DK,
    B_HEADS * B_DV, B_HEADS * B_DV,
    C_HEADS * C_DH, C_KV_HEADS * C_DH, C_KV_HEADS * C_DH,
    D_MODEL, D_MODEL, D_MODEL,
)
IN_WIDTH = sum(IN_SPLITS)

kernel_name = 'hybrid_mla_hgrn2_swa_encoder'


def _rmsnorm(x, g):
    x32 = x.astype(jnp.float32)
    y = x32 * lax.rsqrt(jnp.mean(x32 * x32, axis=-1, keepdims=True) + EPS)
    return (y * g.astype(jnp.float32)).astype(x.dtype)


def _split_cols(z, sizes):
    out, start = [], 0
    for s in sizes:
        out.append(z[..., start:start + s])
        start += s
    return out


def _rope(x, pos):
    half = x.shape[-1] // 2
    inv = ROPE_THETA ** (-jnp.arange(half, dtype=jnp.float32) / half)
    ang = pos.astype(jnp.float32)[:, None] * inv[None, :]
    cos = jnp.cos(ang)[None, :, None, :]
    sin = jnp.sin(ang)[None, :, None, :]
    x32 = x.astype(jnp.float32)
    x1, x2 = x32[..., :half], x32[..., half:]
    return jnp.concatenate([x1 * cos - x2 * sin, x1 * sin + x2 * cos], axis=-1).astype(x.dtype)


def _t5_bucket(rel):
    nb = REL_BUCKETS // 2
    max_exact = nb // 2
    ret = (rel > 0).astype(jnp.int32) * nb
    n = jnp.abs(rel)
    large = max_exact + (jnp.log(jnp.maximum(n, 1).astype(jnp.float32) / max_exact)
                         / math.log(REL_MAX_DIST / max_exact) * (nb - max_exact)).astype(jnp.int32)
    large = jnp.minimum(large, nb - 1)
    return ret + jnp.where(n < max_exact, n, large)


def _mla(cq, ckv, kr, gq, gkv, wuq, wukv, pos):
    Bsz, S, _ = cq.shape
    q = (_rmsnorm(cq, gq) @ wuq).reshape(Bsz, S, A_HEADS, A_NOPE + A_ROPE)
    q = jnp.concatenate([q[..., :A_NOPE], _rope(q[..., A_NOPE:], pos)], axis=-1)
    kv = (_rmsnorm(ckv, gkv) @ wukv).reshape(Bsz, S, A_HEADS, A_NOPE + A_V)
    k_rope = jnp.broadcast_to(_rope(kr[:, :, None, :], pos), (Bsz, S, A_HEADS, A_ROPE))
    k = jnp.concatenate([kv[..., :A_NOPE], k_rope], axis=-1)
    v = kv[..., A_NOPE:]
    scale = (A_NOPE + A_ROPE) ** -0.5
    nb = S // A_QBLOCK
    qb = jnp.moveaxis(q.reshape(Bsz, nb, A_QBLOCK, A_HEADS, A_NOPE + A_ROPE), 1, 0)

    def attend(q_blk):
        s = jnp.einsum('bqhd,bkhd->bhqk', q_blk, k).astype(jnp.float32) * scale
        p = jax.nn.softmax(s, axis=-1).astype(v.dtype)
        return jnp.einsum('bhqk,bkhd->bqhd', p, v)

    o = lax.map(attend, qb)
    return jnp.moveaxis(o, 0, 1).reshape(Bsz, S, A_HEADS * A_V)


def _gated_scan(q, k, v, log_f):
    Bsz, S, H, DK = q.shape
    DV = v.shape[-1]
    nc = S // B_CHUNK
    q, k, log_f = [t.reshape(Bsz, nc, B_CHUNK, H, DK) for t in (q, k, log_f)]
    v = v.reshape(Bsz, nc, B_CHUNK, H, DV)
    b = jnp.cumsum(log_f, axis=2)
    b_last = b[:, :, -1:]
    q_dec = q * jnp.exp(b)
    k_inv = k * jnp.exp(-b)
    k_end = k * jnp.exp(b_last - b)
    scores = jnp.einsum('bnthk,bnshk->bnhts', q_dec, k_inv)
    tri = jnp.tril(jnp.ones((B_CHUNK, B_CHUNK), dtype=bool))
    scores = jnp.where(tri, scores, 0.0)
    o_intra = jnp.einsum('bnhts,bnshv->bnthv', scores, v)

    def step(state, inp):
        q_c, k_c, v_c, dec_c = inp
        o_c = jnp.einsum('bthk,bhkv->bthv', q_c, state)
        state = state * dec_c[:, 0, :, :, None] + jnp.einsum('bshk,bshv->bhkv', k_c, v_c)
        return state, o_c

    xs = tuple(jnp.moveaxis(t, 1, 0) for t in (q_dec, k_end, v, jnp.exp(b_last)))
    s0 = jnp.zeros((Bsz, H, DK, DV), jnp.float32)
    _, o_inter = lax.scan(step, s0, xs)
    o = o_intra + jnp.moveaxis(o_inter, 0, 1)
    return o.reshape(Bsz, S, H, DV)


def _hgrn2(q, f_fwd, f_bwd, i, g, lb_fwd, lb_bwd, g_out):
    Bsz, S, _ = q.shape
    dt = q.dtype

    def heads(t, d):
        return t.astype(jnp.float32).reshape(Bsz, S, B_HEADS, d)

    def gates(z, lb):
        lb = lb.astype(jnp.float32).reshape(B_HEADS, B_DK)
        zh = heads(z, B_DK)
        f = lb + (1.0 - lb) * jax.nn.sigmoid(zh)
        log_f = jnp.log(jnp.maximum(f, TINY))
        key = (1.0 - lb) * jax.nn.sigmoid(-zh)
        return log_f, key

    qh = heads(q, B_DK)
    vh = heads(i, B_DV)
    lf_f, k_f = gates(f_fwd, lb_fwd)
    lf_b, k_b = gates(f_bwd, lb_bwd)
    o_f = _gated_scan(qh, k_f, vh, lf_f)
    flip = lambda t: jnp.flip(t, axis=1)
    o_b = flip(_gated_scan(flip(qh), flip(k_b), flip(vh), flip(lf_b)))
    o = _rmsnorm(o_f + o_b, g_out) * jax.nn.silu(heads(g, B_DV))
    return o.reshape(Bsz, S, B_HEADS * B_DV).astype(dt)


def _window_gqa(q, k, v, rel_bias, sink):
    Bsz, S, _ = q.shape
    nb = S // C_BLOCK
    G = C_HEADS // C_KV_HEADS
    span = 3 * C_BLOCK
    q = q.reshape(Bsz, nb, C_BLOCK, C_KV_HEADS, G, C_DH)

    def band(t):
        t = t.reshape(Bsz, S, C_KV_HEADS, C_DH)
        t = jnp.pad(t, ((0, 0), (C_BLOCK, C_BLOCK), (0, 0), (0, 0)))
        t = t.reshape(Bsz, nb + 2, C_BLOCK, C_KV_HEADS, C_DH)
        return jnp.concatenate([t[:, :-2], t[:, 1:-1], t[:, 2:]], axis=2)

    kb, vb = band(k), band(v)
    rel = jnp.arange(span)[None, :] - C_BLOCK - jnp.arange(C_BLOCK)[:, None]
    bias = rel_bias.astype(jnp.float32)[_t5_bucket(rel)]
    bias = jnp.transpose(bias, (2, 0, 1)).reshape(C_KV_HEADS, G, C_BLOCK, span)
    key_pos = (jnp.arange(nb)[:, None] - 1) * C_BLOCK + jnp.arange(span)[None, :]
    valid = (jnp.abs(rel) <= C_WINDOW)[None] & ((key_pos >= 0) & (key_pos < S))[:, None, :]
    s = jnp.einsum('bnqkgd,bnskd->bnkgqs', q, kb).astype(jnp.float32) * (C_DH ** -0.5) + bias
    s = jnp.where(valid[None, :, None, None], s, MASK_VALUE)
    sink_l = sink.astype(jnp.float32).reshape(C_KV_HEADS, G)[:, :, None, None]
    m = jnp.maximum(jnp.max(s, axis=-1, keepdims=True), sink_l)
    p = jnp.exp(s - m)
    p = p / (jnp.sum(p, axis=-1, keepdims=True) + jnp.exp(sink_l - m))
    o = jnp.einsum('bnkgqs,bnskd->bnqkgd', p.astype(v.dtype), vb)
    return o.reshape(Bsz, S, C_HEADS * C_DH)


def _cross(h, mem_n, wq, wkv, wo):
    Bsz, S, _ = h.shape
    q = (h @ wq).reshape(Bsz, S, X_HEADS, X_DH)
    kv = (mem_n @ wkv).reshape(Bsz, mem_n.shape[1], 2, X_HEADS, X_DH)
    s = jnp.einsum('bqhd,bkhd->bhqk', q, kv[:, :, 0]).astype(jnp.float32) * (X_DH ** -0.5)
    p = jax.nn.softmax(s, axis=-1).astype(h.dtype)
    o = jnp.einsum('bhqk,bkhd->bqhd', p, kv[:, :, 1]).reshape(Bsz, S, X_HEADS * X_DH)
    return o @ wo


def _swiglu(h, w1, w3, w2):
    return (jax.nn.silu(h @ w1) * (h @ w3)) @ w2


def _fwd_setup_inputs(seed: int = 0) -> dict:
    key = jax.random.key(seed)
    ks = iter(jax.random.split(key, 32))
    f32 = jnp.float32

    def nrm(shape, fan_in):
        return jax.random.normal(next(ks), shape, f32) * (fan_in ** -0.5)

    def gain(shape):
        return 1.0 + 0.02 * jax.random.normal(next(ks), shape, f32)

    L, D = DEPTH, D_MODEL
    return {
        'x': jax.random.normal(next(ks), (BATCH, SEQ, D), f32),
        'mem': jax.random.normal(next(ks), (BATCH, MEM_LEN, D), f32),
        'w_in': nrm((L, D, IN_WIDTH), D),
        'g_mix': gain((L, D)),
        'a_gq': gain((L, A_Q_RANK)),
        'a_gkv': gain((L, A_KV_RANK)),
        'a_wuq': nrm((L, A_Q_RANK, A_HEADS * (A_NOPE + A_ROPE)), A_Q_RANK),
        'a_wukv': nrm((L, A_KV_RANK, A_HEADS * (A_NOPE + A_V)), A_KV_RANK),
        'b_lb': jax.random.normal(next(ks), (2, L, B_HEADS * B_DK), f32),
        'b_gout': gain((L, B_DV)),
        'c_sink': 0.5 * jax.random.normal(next(ks), (L, C_HEADS), f32),
        'rel_bias': 0.5 * jax.random.normal(next(ks), (REL_BUCKETS, C_HEADS), f32),
        'w_br_a': nrm((L, A_HEADS * A_V, D), A_HEADS * A_V),
        'w_br_b': nrm((L, B_HEADS * B_DV, D), B_HEADS * B_DV),
        'w_br_c': nrm((L, C_HEADS * C_DH, D), C_HEADS * C_DH),
        'w_out': nrm((L, D, D), D),
        'g_x': gain((L, D)),
        'g_mem': gain((L, D)),
        'x_wq': nrm((L, D, X_HEADS * X_DH), D),
        'x_wkv': nrm((L, D, 2 * X_HEADS * X_DH), D),
        'x_wo': nrm((L, X_HEADS * X_DH, D), X_HEADS * X_DH),
        'g_ffn': gain((L, D)),
        'f_w1': nrm((L, D, D_FF), D),
        'f_w3': nrm((L, D, D_FF), D),
        'f_w2': nrm((L, D_FF, D), D_FF),
        'g_final': gain((D,)),
    }


def _fwd_reference(x, mem, w_in, g_mix, a_gq, a_gkv, a_wuq, a_wukv, b_lb, b_gout, c_sink, rel_bias,
              w_br_a, w_br_b, w_br_c, w_out, g_x, g_mem, x_wq, x_wkv, x_wo, g_ffn,
              f_w1, f_w3, f_w2, g_final):
    S = x.shape[1]
    pos = jnp.arange(S, dtype=jnp.int32)
    sm = jax.nn.softmax(b_lb.astype(jnp.float32), axis=1)
    lower_bounds = jnp.cumsum(sm, axis=1) - sm[:, :1]
    for l in range(DEPTH):
        h = _rmsnorm(x, g_mix[l])
        (a_cq, a_ckv, a_kr, b_q, b_ff, b_fb, b_i, b_g,
         c_q, c_k, c_v, gate_a, gate_b, gate_c) = _split_cols(h @ w_in[l], IN_SPLITS)
        y_a = _mla(a_cq, a_ckv, a_kr, a_gq[l], a_gkv[l], a_wuq[l], a_wukv[l], pos)
        y_b = _hgrn2(b_q, b_ff, b_fb, b_i, b_g, lower_bounds[0, l], lower_bounds[1, l], b_gout[l])
        y_c = _window_gqa(c_q, c_k, c_v, rel_bias, c_sink[l])
        merged = (jax.nn.sigmoid(gate_a) * (y_a @ w_br_a[l])
                  + jax.nn.sigmoid(gate_b) * (y_b @ w_br_b[l])
                  + jax.nn.sigmoid(gate_c) * (y_c @ w_br_c[l]))
        x = x + merged @ w_out[l]
        h = _rmsnorm(x, g_x[l])
        x = x + _cross(h, _rmsnorm(mem, g_mem[l]), x_wq[l], x_wkv[l], x_wo[l])
        h = _rmsnorm(x, g_ffn[l])
        x = x + _swiglu(h, f_w1[l], f_w3[l], f_w2[l])
    return _rmsnorm(x, g_final)


import jax as _jax
import jax.numpy as _jnp

TWIN_FORMAT = 'train_step'
FWD_PARAMS = ['x', 'mem', 'w_in', 'g_mix', 'a_gq', 'a_gkv', 'a_wuq', 'a_wukv', 'b_lb', 'b_gout', 'c_sink', 'rel_bias', 'w_br_a', 'w_br_b', 'w_br_c', 'w_out', 'g_x', 'g_mem', 'x_wq', 'x_wkv', 'x_wo', 'g_ffn', 'f_w1', 'f_w3', 'f_w2', 'g_final']
TWIN_WEIGHTS = ['w_in', 'g_mix', 'a_gq', 'a_gkv', 'a_wuq', 'a_wukv', 'b_lb', 'b_gout', 'c_sink', 'rel_bias', 'w_br_a', 'w_br_b', 'w_br_c', 'w_out', 'g_x', 'g_mem', 'x_wq', 'x_wkv', 'x_wo', 'g_ffn', 'f_w1', 'f_w3', 'f_w2', 'g_final']
TWIN_DIFF_INPUT = 'x'
TWIN_INPUTS = ['x', 'mem', 'w_in', 'g_mix', 'a_gq', 'a_gkv', 'a_wuq', 'a_wukv', 'b_lb', 'b_gout', 'c_sink', 'rel_bias', 'w_br_a', 'w_br_b', 'w_br_c', 'w_out', 'g_x', 'g_mem', 'x_wq', 'x_wkv', 'x_wo', 'g_ffn', 'f_w1', 'f_w3', 'f_w2', 'g_final', 'loss_target', 'm_w_in', 'm_g_mix', 'm_a_gq', 'm_a_gkv', 'm_a_wuq', 'm_a_wukv', 'm_b_lb', 'm_b_gout', 'm_c_sink', 'm_rel_bias', 'm_w_br_a', 'm_w_br_b', 'm_w_br_c', 'm_w_out', 'm_g_x', 'm_g_mem', 'm_x_wq', 'm_x_wkv', 'm_x_wo', 'm_g_ffn', 'm_f_w1', 'm_f_w3', 'm_f_w2', 'm_g_final', 'v_w_in', 'v_g_mix', 'v_a_gq', 'v_a_gkv', 'v_a_wuq', 'v_a_wukv', 'v_b_lb', 'v_b_gout', 'v_c_sink', 'v_rel_bias', 'v_w_br_a', 'v_w_br_b', 'v_w_br_c', 'v_w_out', 'v_g_x', 'v_g_mem', 'v_x_wq', 'v_x_wkv', 'v_x_wo', 'v_g_ffn', 'v_f_w1', 'v_f_w3', 'v_f_w2', 'v_g_final']
TWIN_OUTPUTS = ['loss', 'grad_x', 'grad_w_in', 'grad_g_mix', 'grad_a_gq', 'grad_a_gkv', 'grad_a_wuq', 'grad_a_wukv', 'grad_b_lb', 'grad_b_gout', 'grad_c_sink', 'grad_rel_bias', 'grad_w_br_a', 'grad_w_br_b', 'grad_w_br_c', 'grad_w_out', 'grad_g_x', 'grad_g_mem', 'grad_x_wq', 'grad_x_wkv', 'grad_x_wo', 'grad_g_ffn', 'grad_f_w1', 'grad_f_w3', 'grad_f_w2', 'grad_g_final', 'delta_w_in', 'delta_g_mix', 'delta_a_gq', 'delta_a_gkv', 'delta_a_wuq', 'delta_a_wukv', 'delta_b_lb', 'delta_b_gout', 'delta_c_sink', 'delta_rel_bias', 'delta_w_br_a', 'delta_w_br_b', 'delta_w_br_c', 'delta_w_out', 'delta_g_x', 'delta_g_mem', 'delta_x_wq', 'delta_x_wkv', 'delta_x_wo', 'delta_g_ffn', 'delta_f_w1', 'delta_f_w3', 'delta_f_w2', 'delta_g_final', 'new_m_w_in', 'new_m_g_mix', 'new_m_a_gq', 'new_m_a_gkv', 'new_m_a_wuq', 'new_m_a_wukv', 'new_m_b_lb', 'new_m_b_gout', 'new_m_c_sink', 'new_m_rel_bias', 'new_m_w_br_a', 'new_m_w_br_b', 'new_m_w_br_c', 'new_m_w_out', 'new_m_g_x', 'new_m_g_mem', 'new_m_x_wq', 'new_m_x_wkv', 'new_m_x_wo', 'new_m_g_ffn', 'new_m_f_w1', 'new_m_f_w3', 'new_m_f_w2', 'new_m_g_final', 'new_v_w_in', 'new_v_g_mix', 'new_v_a_gq', 'new_v_a_gkv', 'new_v_a_wuq', 'new_v_a_wukv', 'new_v_b_lb', 'new_v_b_gout', 'new_v_c_sink', 'new_v_rel_bias', 'new_v_w_br_a', 'new_v_w_br_b', 'new_v_w_br_c', 'new_v_w_out', 'new_v_g_x', 'new_v_g_mem', 'new_v_x_wq', 'new_v_x_wkv', 'new_v_x_wo', 'new_v_g_ffn', 'new_v_f_w1', 'new_v_f_w3', 'new_v_f_w2', 'new_v_g_final']
TWIN_LEAF_KINDS = {'loss': 'loss', 'grad_x': 'grad_x', 'grad_w_in': 'grad_w', 'grad_g_mix': 'grad_w', 'grad_a_gq': 'grad_w', 'grad_a_gkv': 'grad_w', 'grad_a_wuq': 'grad_w', 'grad_a_wukv': 'grad_w', 'grad_b_lb': 'grad_w', 'grad_b_gout': 'grad_w', 'grad_c_sink': 'grad_w', 'grad_rel_bias': 'grad_w', 'grad_w_br_a': 'grad_w', 'grad_w_br_b': 'grad_w', 'grad_w_br_c': 'grad_w', 'grad_w_out': 'grad_w', 'grad_g_x': 'grad_w', 'grad_g_mem': 'grad_w', 'grad_x_wq': 'grad_w', 'grad_x_wkv': 'grad_w', 'grad_x_wo': 'grad_w', 'grad_g_ffn': 'grad_w', 'grad_f_w1': 'grad_w', 'grad_f_w3': 'grad_w', 'grad_f_w2': 'grad_w', 'grad_g_final': 'grad_w', 'delta_w_in': 'delta_w', 'delta_g_mix': 'delta_w', 'delta_a_gq': 'delta_w', 'delta_a_gkv': 'delta_w', 'delta_a_wuq': 'delta_w', 'delta_a_wukv': 'delta_w', 'delta_b_lb': 'delta_w', 'delta_b_gout': 'delta_w', 'delta_c_sink': 'delta_w', 'delta_rel_bias': 'delta_w', 'delta_w_br_a': 'delta_w', 'delta_w_br_b': 'delta_w', 'delta_w_br_c': 'delta_w', 'delta_w_out': 'delta_w', 'delta_g_x': 'delta_w', 'delta_g_mem': 'delta_w', 'delta_x_wq': 'delta_w', 'delta_x_wkv': 'delta_w', 'delta_x_wo': 'delta_w', 'delta_g_ffn': 'delta_w', 'delta_f_w1': 'delta_w', 'delta_f_w3': 'delta_w', 'delta_f_w2': 'delta_w', 'delta_g_final': 'delta_w', 'new_m_w_in': 'new_m', 'new_m_g_mix': 'new_m', 'new_m_a_gq': 'new_m', 'new_m_a_gkv': 'new_m', 'new_m_a_wuq': 'new_m', 'new_m_a_wukv': 'new_m', 'new_m_b_lb': 'new_m', 'new_m_b_gout': 'new_m', 'new_m_c_sink': 'new_m', 'new_m_rel_bias': 'new_m', 'new_m_w_br_a': 'new_m', 'new_m_w_br_b': 'new_m', 'new_m_w_br_c': 'new_m', 'new_m_w_out': 'new_m', 'new_m_g_x': 'new_m', 'new_m_g_mem': 'new_m', 'new_m_x_wq': 'new_m', 'new_m_x_wkv': 'new_m', 'new_m_x_wo': 'new_m', 'new_m_g_ffn': 'new_m', 'new_m_f_w1': 'new_m', 'new_m_f_w3': 'new_m', 'new_m_f_w2': 'new_m', 'new_m_g_final': 'new_m', 'new_v_w_in': 'new_v', 'new_v_g_mix': 'new_v', 'new_v_a_gq': 'new_v', 'new_v_a_gkv': 'new_v', 'new_v_a_wuq': 'new_v', 'new_v_a_wukv': 'new_v', 'new_v_b_lb': 'new_v', 'new_v_b_gout': 'new_v', 'new_v_c_sink': 'new_v', 'new_v_rel_bias': 'new_v', 'new_v_w_br_a': 'new_v', 'new_v_w_br_b': 'new_v', 'new_v_w_br_c': 'new_v', 'new_v_w_out': 'new_v', 'new_v_g_x': 'new_v', 'new_v_g_mem': 'new_v', 'new_v_x_wq': 'new_v', 'new_v_x_wkv': 'new_v', 'new_v_x_wo': 'new_v', 'new_v_g_ffn': 'new_v', 'new_v_f_w1': 'new_v', 'new_v_f_w3': 'new_v', 'new_v_f_w2': 'new_v', 'new_v_g_final': 'new_v'}


def _forward(args):
    return _fwd_reference(*[args[k] for k in FWD_PARAMS])


def _output_shape():
    def fwd():
        inp = _fwd_setup_inputs(0)
        return _fwd_reference(*[inp[k] for k in FWD_PARAMS])
    out = _jax.eval_shape(fwd)
    return out.shape, out.dtype

N_MICROBATCH = 1
ADAM_LR = 0.001
ADAM_B1 = 0.9
ADAM_B2 = 0.999
ADAM_EPS = 1e-08
ADAM_WD = 0.01
ADAM_STEP = 10
PER_EXAMPLE_BATCH_AXIS = {'x': 0, 'mem': 0, 'loss_target': 0}
SHARED_INPUTS = []
_WEIGHT_DTYPES = {'w_in': _jnp.float32, 'g_mix': _jnp.float32, 'a_gq': _jnp.float32, 'a_gkv': _jnp.float32, 'a_wuq': _jnp.float32, 'a_wukv': _jnp.float32, 'b_lb': _jnp.float32, 'b_gout': _jnp.float32, 'c_sink': _jnp.float32, 'rel_bias': _jnp.float32, 'w_br_a': _jnp.float32, 'w_br_b': _jnp.float32, 'w_br_c': _jnp.float32, 'w_out': _jnp.float32, 'g_x': _jnp.float32, 'g_mem': _jnp.float32, 'x_wq': _jnp.float32, 'x_wkv': _jnp.float32, 'x_wo': _jnp.float32, 'g_ffn': _jnp.float32, 'f_w1': _jnp.float32, 'f_w3': _jnp.float32, 'f_w2': _jnp.float32, 'g_final': _jnp.float32}
MOMENT_SCALE = {'w_in': 1.035920e-01, 'g_mix': 3.020536e-01, 'a_gq': 3.804104e-02, 'a_gkv': 7.561289e-02, 'a_wuq': 2.789296e-02, 'a_wukv': 3.369209e-02, 'b_lb': 3.176688e-02, 'b_gout': 5.440172e-01, 'c_sink': 1.579585e-03, 'rel_bias': 7.690048e-02, 'w_br_a': 2.669552e-02, 'w_br_b': 1.268692e-01, 'w_br_c': 3.011614e-02, 'w_out': 1.326697e-01, 'g_x': 4.073645e-02, 'g_mem': 6.214063e-02, 'x_wq': 4.088692e-02, 'x_wkv': 4.205648e-02, 'x_wo': 4.357836e-02, 'g_ffn': 2.699511e-01, 'f_w1': 1.150806e-01, 'f_w3': 1.117576e-01, 'f_w2': 1.848037e-01, 'g_final': 1.278816e+02}


def _to_microbatches(a, axis):
    t = _jnp.moveaxis(a, axis, 0)
    t = t.reshape((N_MICROBATCH, t.shape[0] // N_MICROBATCH) + t.shape[1:])
    return _jnp.moveaxis(t, 1, axis + 1)


def setup_inputs(seed: int = 0) -> dict:
    inp = _fwd_setup_inputs(seed)
    key = _jax.random.fold_in(_jax.random.key(seed), 7919)
    shape, _ = _output_shape()
    out = dict(inp)
    out["loss_target"] = _jax.random.normal(_jax.random.fold_in(key, 0), shape, _jnp.float32)
    for i, name in enumerate(TWIN_WEIGHTS):
        w = inp[name].astype(_jnp.float32)
        if MOMENT_SCALE is None:
            s = _jnp.sqrt(_jnp.mean(_jnp.square(w)) + 1e-30)
        else:
            s = MOMENT_SCALE[name]
        km, kv = _jax.random.split(_jax.random.fold_in(key, i + 1))
        out[name] = w
        out["m_" + name] = s * _jax.random.normal(km, w.shape, _jnp.float32)
        out["v_" + name] = (s * s) * _jax.random.uniform(kv, w.shape, _jnp.float32, 0.5, 1.5)
    if N_MICROBATCH > 1:
        for name, axis in PER_EXAMPLE_BATCH_AXIS.items():
            out[name] = _to_microbatches(out[name], axis)
    return {'x': out['x'], 'mem': out['mem'], 'w_in': out['w_in'], 'g_mix': out['g_mix'], 'a_gq': out['a_gq'], 'a_gkv': out['a_gkv'], 'a_wuq': out['a_wuq'], 'a_wukv': out['a_wukv'], 'b_lb': out['b_lb'], 'b_gout': out['b_gout'], 'c_sink': out['c_sink'], 'rel_bias': out['rel_bias'], 'w_br_a': out['w_br_a'], 'w_br_b': out['w_br_b'], 'w_br_c': out['w_br_c'], 'w_out': out['w_out'], 'g_x': out['g_x'], 'g_mem': out['g_mem'], 'x_wq': out['x_wq'], 'x_wkv': out['x_wkv'], 'x_wo': out['x_wo'], 'g_ffn': out['g_ffn'], 'f_w1': out['f_w1'], 'f_w3': out['f_w3'], 'f_w2': out['f_w2'], 'g_final': out['g_final'], 'loss_target': out['loss_target'], 'm_w_in': out['m_w_in'], 'm_g_mix': out['m_g_mix'], 'm_a_gq': out['m_a_gq'], 'm_a_gkv': out['m_a_gkv'], 'm_a_wuq': out['m_a_wuq'], 'm_a_wukv': out['m_a_wukv'], 'm_b_lb': out['m_b_lb'], 'm_b_gout': out['m_b_gout'], 'm_c_sink': out['m_c_sink'], 'm_rel_bias': out['m_rel_bias'], 'm_w_br_a': out['m_w_br_a'], 'm_w_br_b': out['m_w_br_b'], 'm_w_br_c': out['m_w_br_c'], 'm_w_out': out['m_w_out'], 'm_g_x': out['m_g_x'], 'm_g_mem': out['m_g_mem'], 'm_x_wq': out['m_x_wq'], 'm_x_wkv': out['m_x_wkv'], 'm_x_wo': out['m_x_wo'], 'm_g_ffn': out['m_g_ffn'], 'm_f_w1': out['m_f_w1'], 'm_f_w3': out['m_f_w3'], 'm_f_w2': out['m_f_w2'], 'm_g_final': out['m_g_final'], 'v_w_in': out['v_w_in'], 'v_g_mix': out['v_g_mix'], 'v_a_gq': out['v_a_gq'], 'v_a_gkv': out['v_a_gkv'], 'v_a_wuq': out['v_a_wuq'], 'v_a_wukv': out['v_a_wukv'], 'v_b_lb': out['v_b_lb'], 'v_b_gout': out['v_b_gout'], 'v_c_sink': out['v_c_sink'], 'v_rel_bias': out['v_rel_bias'], 'v_w_br_a': out['v_w_br_a'], 'v_w_br_b': out['v_w_br_b'], 'v_w_br_c': out['v_w_br_c'], 'v_w_out': out['v_w_out'], 'v_g_x': out['v_g_x'], 'v_g_mem': out['v_g_mem'], 'v_x_wq': out['v_x_wq'], 'v_x_wkv': out['v_x_wkv'], 'v_x_wo': out['v_x_wo'], 'v_g_ffn': out['v_g_ffn'], 'v_f_w1': out['v_f_w1'], 'v_f_w3': out['v_f_w3'], 'v_f_w2': out['v_f_w2'], 'v_g_final': out['v_g_final']}


def _loss(weights, diff, rest, loss_target):
    with _jax.named_scope("forward"):
        args = {**rest, TWIN_DIFF_INPUT: diff, **{k: w.astype(_WEIGHT_DTYPES[k]) for k, w in weights.items()}}
        y = _forward(args)
    with _jax.named_scope("loss_head"):
        err = _jnp.square(y.astype(_jnp.float32) - loss_target)
        return 0.5 * _jnp.sum(_jnp.mean(err, axis=-1)) if err.ndim else 0.5 * err


def _adamw(w, g, m, v):
    m = ADAM_B1 * m + (1.0 - ADAM_B1) * g
    v = ADAM_B2 * v + (1.0 - ADAM_B2) * _jnp.square(g)
    m_hat = m / (1.0 - ADAM_B1 ** ADAM_STEP)
    v_hat = v / (1.0 - ADAM_B2 ** ADAM_STEP)
    delta = -ADAM_LR * (m_hat / (_jnp.sqrt(v_hat) + ADAM_EPS) + ADAM_WD * w)
    return delta, m, v


def reference(x, mem, w_in, g_mix, a_gq, a_gkv, a_wuq, a_wukv, b_lb, b_gout, c_sink, rel_bias, w_br_a, w_br_b, w_br_c, w_out, g_x, g_mem, x_wq, x_wkv, x_wo, g_ffn, f_w1, f_w3, f_w2, g_final, loss_target, m_w_in, m_g_mix, m_a_gq, m_a_gkv, m_a_wuq, m_a_wukv, m_b_lb, m_b_gout, m_c_sink, m_rel_bias, m_w_br_a, m_w_br_b, m_w_br_c, m_w_out, m_g_x, m_g_mem, m_x_wq, m_x_wkv, m_x_wo, m_g_ffn, m_f_w1, m_f_w3, m_f_w2, m_g_final, v_w_in, v_g_mix, v_a_gq, v_a_gkv, v_a_wuq, v_a_wukv, v_b_lb, v_b_gout, v_c_sink, v_rel_bias, v_w_br_a, v_w_br_b, v_w_br_c, v_w_out, v_g_x, v_g_mem, v_x_wq, v_x_wkv, v_x_wo, v_g_ffn, v_f_w1, v_f_w3, v_f_w2, v_g_final):
    given = dict(x=x, mem=mem, w_in=w_in, g_mix=g_mix, a_gq=a_gq, a_gkv=a_gkv, a_wuq=a_wuq, a_wukv=a_wukv, b_lb=b_lb, b_gout=b_gout, c_sink=c_sink, rel_bias=rel_bias, w_br_a=w_br_a, w_br_b=w_br_b, w_br_c=w_br_c, w_out=w_out, g_x=g_x, g_mem=g_mem, x_wq=x_wq, x_wkv=x_wkv, x_wo=x_wo, g_ffn=g_ffn, f_w1=f_w1, f_w3=f_w3, f_w2=f_w2, g_final=g_final, loss_target=loss_target, m_w_in=m_w_in, m_g_mix=m_g_mix, m_a_gq=m_a_gq, m_a_gkv=m_a_gkv, m_a_wuq=m_a_wuq, m_a_wukv=m_a_wukv, m_b_lb=m_b_lb, m_b_gout=m_b_gout, m_c_sink=m_c_sink, m_rel_bias=m_rel_bias, m_w_br_a=m_w_br_a, m_w_br_b=m_w_br_b, m_w_br_c=m_w_br_c, m_w_out=m_w_out, m_g_x=m_g_x, m_g_mem=m_g_mem, m_x_wq=m_x_wq, m_x_wkv=m_x_wkv, m_x_wo=m_x_wo, m_g_ffn=m_g_ffn, m_f_w1=m_f_w1, m_f_w3=m_f_w3, m_f_w2=m_f_w2, m_g_final=m_g_final, v_w_in=v_w_in, v_g_mix=v_g_mix, v_a_gq=v_a_gq, v_a_gkv=v_a_gkv, v_a_wuq=v_a_wuq, v_a_wukv=v_a_wukv, v_b_lb=v_b_lb, v_b_gout=v_b_gout, v_c_sink=v_c_sink, v_rel_bias=v_rel_bias, v_w_br_a=v_w_br_a, v_w_br_b=v_w_br_b, v_w_br_c=v_w_br_c, v_w_out=v_w_out, v_g_x=v_g_x, v_g_mem=v_g_mem, v_x_wq=v_x_wq, v_x_wkv=v_x_wkv, v_x_wo=v_x_wo, v_g_ffn=v_g_ffn, v_f_w1=v_f_w1, v_f_w3=v_f_w3, v_f_w2=v_f_w2, v_g_final=v_g_final)
    weights = {n: given[n] for n in TWIN_WEIGHTS}
    shared = {n: given[n] for n in SHARED_INPUTS}
    per_example = {n: given[n] for n in ['x', 'mem']}
    grad_fn = _jax.value_and_grad(_loss, argnums=(0, 1))

    def one_microbatch(ex, loss_target):
        ex = dict(ex)
        diff = ex.pop(TWIN_DIFF_INPUT)
        return grad_fn(weights, diff, {**shared, **ex}, loss_target)

    if N_MICROBATCH == 1:
        loss, (grad_w, grad_x) = one_microbatch(per_example, given["loss_target"])
    else:
        def body(carry, xs):
            loss_sum, grad_sum = carry
            l_k, (gw_k, gx_k) = one_microbatch(xs[0], xs[1])
            with _jax.named_scope("update"):
                return (loss_sum + l_k, _jax.tree.map(_jnp.add, grad_sum, gw_k)), gx_k

        init = (_jnp.zeros((), _jnp.float32), _jax.tree.map(_jnp.zeros_like, weights))
        (loss, grad_w), grad_x = _jax.lax.scan(body, init, (per_example, given["loss_target"]))
    with _jax.named_scope("update"):
        delta_w, new_m, new_v = {}, {}, {}
        for n in TWIN_WEIGHTS:
            delta_w[n], new_m[n], new_v[n] = _adamw(weights[n], grad_w[n], given["m_" + n], given["v_" + n])
    return (loss, grad_x, *[grad_w[n] for n in TWIN_WEIGHTS], *[delta_w[n] for n in TWIN_WEIGHTS],
            *[new_m[n] for n in TWIN_WEIGHTS], *[new_v[n] for n in TWIN_WEIGHTS])
```

```python
import functools
import math

import jax
import jax.numpy as jnp
from jax import lax
from jax.experimental import pallas as pl
from jax.experimental.pallas import tpu as pltpu

F32 = jnp.float32
BF16 = jnp.bfloat16

D_MODEL = 1024
DEPTH = 2
EPS = 1e-6
MASK_VALUE = -1e30
TINY = 1e-30
A_HEADS, A_NOPE, A_ROPE, A_V = 8, 64, 32, 64
A_QK = A_NOPE + A_ROPE
A_Q_RANK, A_KV_RANK = 384, 256
ROPE_THETA = 10000.0
B_HEADS, B_DK, B_DV, B_CHUNK = 8, 128, 64, 16
C_HEADS, C_KV_HEADS, C_DH, C_WINDOW, C_BLOCK = 8, 2, 64, 128, 128
REL_BUCKETS, REL_MAX_DIST = 32, 128
X_HEADS, X_DH = 4, 256
D_FF = 2816
ADAM_LR, ADAM_B1, ADAM_B2, ADAM_EPS, ADAM_WD, ADAM_STEP = 0.001, 0.9, 0.999, 1e-08, 0.01, 10

LANES = 128
VMEM_LIMIT = 56 * 1024 * 1024


def _pallas(body, **kw):
    return pl.pallas_call(body, **kw)


def _params(*sem):
    return pltpu.CompilerParams(dimension_semantics=sem, vmem_limit_bytes=VMEM_LIMIT)


def _tile(n, pref, unit=LANES):
    if n <= pref:
        return n
    t = (pref // unit) * unit
    while t > unit and n % t:
        t -= unit
    assert n % t == 0, (n, pref, unit)
    return t


def _dot(a, b, dims):
    return lax.dot_general(a, b, (dims, ((), ())), preferred_element_type=F32)


_NN = ((1,), (0,))
_NT = ((1,), (1,))
_TN = ((0,), (0,))


def _mm(a, b, *, mode="nn", add=None, out_dtype=F32, name):
    if mode == "nn":
        (M, K), (K2, N) = a.shape, b.shape
    elif mode == "nt":
        (M, K), (N, K2) = a.shape, b.shape
    else:
        (K, M), (K2, N) = a.shape, b.shape
    assert K == K2, (a.shape, b.shape, mode)
    tm = _tile(M, 512, LANES if mode == "tn" else 8)
    tn = _tile(N, 1024, 256) if N % 256 == 0 else _tile(N, 1024)
    tk = _tile(K, 1024)
    nk = K // tk
    dims = {"nn": _NN, "nt": _NT, "tn": _TN}[mode]
    a_spec = pl.BlockSpec((tk, tm), lambda i, j, k: (k, i)) if mode == "tn" else pl.BlockSpec((tm, tk), lambda i, j, k: (i, k))
    b_spec = pl.BlockSpec((tn, tk), lambda i, j, k: (j, k)) if mode == "nt" else pl.BlockSpec((tk, tn), lambda i, j, k: (k, j))
    o_spec = pl.BlockSpec((tm, tn), lambda i, j, k: (i, j))
    has_add = add is not None

    def body(*refs):
        if has_add:
            a_ref, b_ref, add_ref, o_ref, acc_ref = refs
        else:
            a_ref, b_ref, o_ref, acc_ref = refs
        k = pl.program_id(2)
        part = _dot(a_ref[...].astype(BF16), b_ref[...].astype(BF16), dims)

        @pl.when(k == 0)
        def _():
            acc_ref[...] = part

        @pl.when(k > 0)
        def _():
            acc_ref[...] += part

        @pl.when(k == nk - 1)
        def _():
            r = acc_ref[...]
            if has_add:
                r = r + add_ref[...].astype(F32)
            o_ref[...] = r.astype(out_dtype)

    ins = [a, b] + ([add] if has_add else [])
    in_specs = [a_spec, b_spec] + ([o_spec] if has_add else [])
    return _pallas(
        body, name=name, grid=(M // tm, N // tn, nk), in_specs=in_specs, out_specs=o_spec,
        out_shape=jax.ShapeDtypeStruct((M, N), out_dtype), scratch_shapes=[pltpu.VMEM((tm, tn), F32)],
        compiler_params=_params("parallel", "parallel", "arbitrary"),
    )(*ins)


def _rms(x, g, *, col=0, width=None, out_dtype=BF16, name):
    T = x.shape[0]
    width = x.shape[1] if width is None else width
    assert col % width == 0
    tm = _tile(T, 512, 8)
    cb = col // width

    def body(x_ref, g_ref, o_ref):
        xv = x_ref[...].astype(F32)
        r = lax.rsqrt(jnp.mean(xv * xv, axis=-1, keepdims=True) + EPS)
        o_ref[...] = (xv * r * g_ref[...]).astype(out_dtype)

    return _pallas(
        body, name=name, grid=(T // tm,),
        in_specs=[pl.BlockSpec((tm, width), lambda i: (i, cb)), pl.BlockSpec((1, width), lambda i: (0, 0))],
        out_specs=pl.BlockSpec((tm, width), lambda i: (i, 0)),
        out_shape=jax.ShapeDtypeStruct((T, width), out_dtype), compiler_params=_params("parallel"),
    )(x, g.reshape(1, width))


def _rms_bwd(x, g, dy, *, res=None, col=0, width=None, out_dtype=F32, name):
    T = x.shape[0]
    width = x.shape[1] if width is None else width
    assert col % width == 0
    tm = _tile(T, 512, 8)
    cb = col // width
    has_res = res is not None

    def body(*refs):
        if has_res:
            x_ref, g_ref, dy_ref, res_ref, dx_ref, dg_ref = refs
        else:
            x_ref, g_ref, dy_ref, dx_ref, dg_ref = refs
        xv = x_ref[...].astype(F32)
        r = lax.rsqrt(jnp.mean(xv * xv, axis=-1, keepdims=True) + EPS)
        xh = xv * r
        dyv = dy_ref[...].astype(F32)
        dxh = dyv * g_ref[...]
        dx = r * (dxh - xh * jnp.mean(dxh * xh, axis=-1, keepdims=True))
        if has_res:
            dx = dx + res_ref[...].astype(F32)
        dx_ref[...] = dx.astype(out_dtype)
        part = jnp.sum(dyv * xh, axis=0, keepdims=True)

        @pl.when(pl.program_id(0) == 0)
        def _():
            dg_ref[...] = part

        @pl.when(pl.program_id(0) > 0)
        def _():
            dg_ref[...] += part

    row = pl.BlockSpec((tm, width), lambda i: (i, 0))
    ins = [x, g.reshape(1, width), dy] + ([res] if has_res else [])
    in_specs = [pl.BlockSpec((tm, width), lambda i: (i, cb)), pl.BlockSpec((1, width), lambda i: (0, 0)), row] + ([row] if has_res else [])
    return _pallas(
        body, name=name, grid=(T // tm,), in_specs=in_specs,
        out_specs=[row, pl.BlockSpec((1, width), lambda i: (0, 0))],
        out_shape=[jax.ShapeDtypeStruct((T, width), out_dtype), jax.ShapeDtypeStruct((1, width), F32)],
        compiler_params=_params("arbitrary"),
    )(*ins)


def _rope_tables(T):
    half = A_ROPE // 2
    inv = ROPE_THETA ** (-jnp.arange(half, dtype=F32) / half)
    ang = jnp.arange(T, dtype=jnp.int32).astype(F32)[:, None] * inv[None, :]
    c32 = jnp.concatenate([jnp.cos(ang), jnp.cos(ang)], axis=-1)
    s32 = jnp.concatenate([jnp.sin(ang), jnp.sin(ang)], axis=-1)
    cq = jnp.tile(jnp.concatenate([jnp.ones((T, A_NOPE), F32), c32], axis=-1), (1, A_HEADS))
    sq = jnp.tile(jnp.concatenate([jnp.zeros((T, A_NOPE), F32), s32], axis=-1), (1, A_HEADS))
    ck = jnp.concatenate([c32, s32, jnp.zeros((T, LANES - 2 * A_ROPE), F32)], axis=-1)
    return cq, sq, ck


def _rope_swap_cols(w):
    half = A_ROPE // 2
    return jnp.concatenate([-w[..., half:], w[..., :half]], axis=-1)


def _rope_unswap_cols(g):
    half = A_ROPE // 2
    return jnp.concatenate([g[..., half:], -g[..., :half]], axis=-1)


def _qrope(q2, cq, sq, *, name):
    T = q2.shape[0]
    W = A_HEADS * A_QK
    tm = _tile(T, 512, 8)

    def body(a_ref, b_ref, c_ref, s_ref, o_ref):
        o_ref[...] = (a_ref[...] * c_ref[...] + b_ref[...] * s_ref[...]).astype(BF16)

    blk = lambda j: pl.BlockSpec((tm, W), lambda i: (i, j))
    return _pallas(body, name=name, grid=(T // tm,), in_specs=[blk(0), blk(1), blk(0), blk(0)], out_specs=blk(0),
                   out_shape=jax.ShapeDtypeStruct((T, W), BF16), compiler_params=_params("parallel"))(q2, q2, cq, sq)


def _qrope_bwd(dq, cq, sq, *, name):
    T = dq.shape[0]
    W = A_HEADS * A_QK
    tm = _tile(T, 512, 8)

    def body(d_ref, c_ref, s_ref, o_ref):
        d = d_ref[...]
        o_ref[:, 0:W] = (d * c_ref[...]).astype(BF16)
        o_ref[:, W:2 * W] = (d * s_ref[...]).astype(BF16)

    blk = pl.BlockSpec((tm, W), lambda i: (i, 0))
    return _pallas(body, name=name, grid=(T // tm,), in_specs=[blk, blk, blk],
                   out_specs=pl.BlockSpec((tm, 2 * W), lambda i: (i, 0)),
                   out_shape=jax.ShapeDtypeStruct((T, 2 * W), BF16), compiler_params=_params("parallel"))(dq, cq, sq)


def _kprep(kv, za, ck, *, name):
    T = kv.shape[0]
    tm = _tile(T, 512, 8)

    def body(kv_ref, kr_ref, ck_ref, o_ref):
        t = kr_ref[...] * ck_ref[...]
        krope = (t[:, 0:A_ROPE] + t[:, A_ROPE:2 * A_ROPE]).astype(BF16)
        for h in range(A_HEADS):
            o_ref[:, A_QK * h:A_QK * h + A_NOPE] = kv_ref[:, A_NOPE * h:A_NOPE * (h + 1)]
            o_ref[:, A_QK * h + A_NOPE:A_QK * (h + 1)] = krope

    return _pallas(
        body, name=name, grid=(T // tm,),
        in_specs=[pl.BlockSpec((tm, 512), lambda i: (i, 0)), pl.BlockSpec((tm, LANES), lambda i: (i, 3)),
                  pl.BlockSpec((tm, LANES), lambda i: (i, 0))],
        out_specs=pl.BlockSpec((tm, A_HEADS * A_QK), lambda i: (i, 0)),
        out_shape=jax.ShapeDtypeStruct((T, A_HEADS * A_QK), BF16), compiler_params=_params("parallel"))(kv, za, ck)


def _kprep_bwd(dk, ck, *, name):
    T = dk.shape[0]
    tm = _tile(T, 512, 8)

    def body(dk_ref, ck_ref, dn_ref, dr_ref):
        acc = jnp.zeros((tm, A_ROPE), F32)
        for h in range(A_HEADS):
            dn_ref[:, A_NOPE * h:A_NOPE * (h + 1)] = dk_ref[:, A_QK * h:A_QK * h + A_NOPE].astype(BF16)
            acc = acc + dk_ref[:, A_QK * h + A_NOPE:A_QK * (h + 1)].astype(F32)
        dr_ref[...] = jnp.zeros((tm, LANES), BF16)
        dr_ref[:, 0:A_ROPE] = (acc * ck_ref[:, 0:A_ROPE]).astype(BF16)
        dr_ref[:, A_ROPE:2 * A_ROPE] = (acc * ck_ref[:, A_ROPE:2 * A_ROPE]).astype(BF16)

    return _pallas(
        body, name=name, grid=(T // tm,),
        in_specs=[pl.BlockSpec((tm, A_HEADS * A_QK), lambda i: (i, 0)), pl.BlockSpec((tm, LANES), lambda i: (i, 0))],
        out_specs=[pl.BlockSpec((tm, 512), lambda i: (i, 0)), pl.BlockSpec((tm, LANES), lambda i: (i, 0))],
        out_shape=[jax.ShapeDtypeStruct((T, 512), BF16), jax.ShapeDtypeStruct((T, LANES), BF16)],
        compiler_params=_params("parallel"))(dk, ck)


def _flash_tiles(T):
    return _tile(T, 512, 8), _tile(T, 1024)


def _flash_fwd(q, k, kv, *, name):
    T = q.shape[0]
    tq, tk = _flash_tiles(T)
    nk = T // tk
    scale = A_QK ** -0.5
    H, DQ, DV = A_HEADS, A_QK, A_V

    def body(q_ref, k_ref, v_ref, o_ref, lse_ref, m_sc, l_sc, acc_sc):
        j = pl.program_id(1)

        @pl.when(j == 0)
        def _():
            m_sc[...] = jnp.full(m_sc.shape, -jnp.inf, F32)
            l_sc[...] = jnp.zeros(l_sc.shape, F32)
            acc_sc[...] = jnp.zeros(acc_sc.shape, F32)

        for h in range(H):
            s = _dot(q_ref[:, DQ * h:DQ * (h + 1)], k_ref[:, DQ * h:DQ * (h + 1)], _NT) * scale
            m_prev = m_sc[h]
            m_new = jnp.maximum(m_prev, jnp.max(s, axis=-1, keepdims=True))
            p = jnp.exp(s - m_new)
            alpha = jnp.exp(m_prev - m_new)
            l_sc[h] = alpha * l_sc[h] + jnp.sum(p, axis=-1, keepdims=True)
            acc_sc[h] = alpha * acc_sc[h] + _dot(p.astype(BF16), v_ref[:, DV * h:DV * (h + 1)], _NN)
            m_sc[h] = m_new

        @pl.when(j == nk - 1)
        def _():
            for h in range(H):
                l = l_sc[h]
                o_ref[:, DV * h:DV * (h + 1)] = (acc_sc[h] / l).astype(o_ref.dtype)
                lse_ref[h] = m_sc[h] + jnp.log(l)

    return _pallas(
        body, name=name, grid=(T // tq, nk),
        in_specs=[pl.BlockSpec((tq, H * DQ), lambda i, j: (i, 0)), pl.BlockSpec((tk, H * DQ), lambda i, j: (j, 0)),
                  pl.BlockSpec((tk, H * DV), lambda i, j: (j, 1))],
        out_specs=[pl.BlockSpec((tq, H * DV), lambda i, j: (i, 0)), pl.BlockSpec((H, tq, 1), lambda i, j: (0, i, 0))],
        out_shape=[jax.ShapeDtypeStruct((T, H * DV), F32), jax.ShapeDtypeStruct((H, T, 1), F32)],
        scratch_shapes=[pltpu.VMEM((H, tq, 1), F32), pltpu.VMEM((H, tq, 1), F32), pltpu.VMEM((H, tq, DV), F32)],
        compiler_params=_params("parallel", "arbitrary"))(q, k, kv)


def _flash_bwd(q, k, kv, o, do, lse, *, name):
    T = q.shape[0]
    tq, tk = _flash_tiles(T)
    nq, nk = T // tq, T // tk
    scale = A_QK ** -0.5
    H, DQ, DV = A_HEADS, A_QK, A_V

    def probs(q_ref, k_ref, v_ref, o_ref, do_ref, lse_ref, h):
        qh = q_ref[:, DQ * h:DQ * (h + 1)]
        kh = k_ref[:, DQ * h:DQ * (h + 1)]
        doh = do_ref[:, DV * h:DV * (h + 1)]
        s = _dot(qh, kh, _NT) * scale
        p = jnp.exp(s - lse_ref[h])
        delta = jnp.sum(o_ref[:, DV * h:DV * (h + 1)] * doh.astype(F32), axis=-1, keepdims=True)
        dp = _dot(doh, v_ref[:, DV * h:DV * (h + 1)], _NT)
        ds = (p * (dp - delta) * scale).astype(BF16)
        return qh, kh, doh, p, ds

    def dkv_body(q_ref, k_ref, v_ref, o_ref, do_ref, lse_ref, dk_ref, dv_ref, dk_sc, dv_sc):
        i = pl.program_id(1)

        @pl.when(i == 0)
        def _():
            dk_sc[...] = jnp.zeros(dk_sc.shape, F32)
            dv_sc[...] = jnp.zeros(dv_sc.shape, F32)

        for h in range(H):
            qh, kh, doh, p, ds = probs(q_ref, k_ref, v_ref, o_ref, do_ref, lse_ref, h)
            dv_sc[h] += _dot(p.astype(BF16), doh, _TN)
            dk_sc[h] += _dot(ds, qh, _TN)

        @pl.when(i == nq - 1)
        def _():
            for h in range(H):
                dk_ref[:, DQ * h:DQ * (h + 1)] = dk_sc[h]
                dv_ref[:, DV * h:DV * (h + 1)] = dv_sc[h].astype(BF16)

    def dq_body(q_ref, k_ref, v_ref, o_ref, do_ref, lse_ref, dq_ref, dq_sc):
        j = pl.program_id(1)

        @pl.when(j == 0)
        def _():
            dq_sc[...] = jnp.zeros(dq_sc.shape, F32)

        for h in range(H):
            qh, kh, doh, p, ds = probs(q_ref, k_ref, v_ref, o_ref, do_ref, lse_ref, h)
            dq_sc[h] += _dot(ds, kh, _NN)

        @pl.when(j == nk - 1)
        def _():
            for h in range(H):
                dq_ref[:, DQ * h:DQ * (h + 1)] = dq_sc[h]

    def specs(qi, kj):
        return [pl.BlockSpec((tq, H * DQ), lambda a, b: (qi(a, b), 0)), pl.BlockSpec((tk, H * DQ), lambda a, b: (kj(a, b), 0)),
                pl.BlockSpec((tk, H * DV), lambda a, b: (kj(a, b), 1)), pl.BlockSpec((tq, H * DV), lambda a, b: (qi(a, b), 0)),
                pl.BlockSpec((tq, H * DV), lambda a, b: (qi(a, b), 0)), pl.BlockSpec((H, tq, 1), lambda a, b: (0, qi(a, b), 0))]

    dk, dv = _pallas(
        dkv_body, name=name + "_dkv", grid=(nk, nq), in_specs=specs(lambda a, b: b, lambda a, b: a),
        out_specs=[pl.BlockSpec((tk, H * DQ), lambda a, b: (a, 0)), pl.BlockSpec((tk, H * DV), lambda a, b: (a, 0))],
        out_shape=[jax.ShapeDtypeStruct((T, H * DQ), F32), jax.ShapeDtypeStruct((T, H * DV), BF16)],
        scratch_shapes=[pltpu.VMEM((H, tk, DQ), F32), pltpu.VMEM((H, tk, DV), F32)],
        compiler_params=_params("parallel", "arbitrary"))(q, k, kv, o, do, lse)
    dq = _pallas(
        dq_body, name=name + "_dq", grid=(nq, nk), in_specs=specs(lambda a, b: a, lambda a, b: b),
        out_specs=pl.BlockSpec((tq, H * DQ), lambda a, b: (a, 0)),
        out_shape=jax.ShapeDtypeStruct((T, H * DQ), F32),
        scratch_shapes=[pltpu.VMEM((H, tq, DQ), F32)],
        compiler_params=_params("parallel", "arbitrary"))(q, k, kv, o, do, lse)
    return dq, dk, dv


HB = 8 * B_CHUNK


def _chunk_masks(reverse):
    r = lax.broadcasted_iota(jnp.int32, (HB, HB), 0)
    c = lax.broadcasted_iota(jnp.int32, (HB, HB), 1)
    same = (r // B_CHUNK) == (c // B_CHUNK)
    incl = same & ((c >= r) if reverse else (c <= r))
    return same, incl


def _split3(x):
    hi = x.astype(BF16)
    r1 = x - hi.astype(F32)
    mid = r1.astype(BF16)
    lo = (r1 - mid.astype(F32)).astype(BF16)
    return hi, mid, lo


def _mask_mm(mask, x):
    hi, mid, lo = _split3(x)
    return _dot(mask, hi, _NN) + _dot(mask, mid, _NN) + _dot(mask, lo, _NN)


def _hgrn_gates(q, z, lb, reverse):
    same, incl = _chunk_masks(reverse)
    sg = jax.nn.sigmoid(z)
    f = lb + (1.0 - lb) * sg
    lf = jnp.log(jnp.maximum(f, TINY))
    kk = (1.0 - lb) * jax.nn.sigmoid(-z)
    b = _mask_mm(incl.astype(BF16), lf)
    btot = _mask_mm(same.astype(BF16), lf)
    eb, enb, er, dec = jnp.exp(b), jnp.exp(-b), jnp.exp(btot - b), jnp.exp(btot)
    return dict(same=same, incl=incl, sg=sg, f=f, kk=kk, eb=eb, enb=enb, er=er, dec=dec,
                qd=q * eb, ki=kk * enb, ke=kk * er)


def _hgrn_specs(T, reverse):
    nb = T // HB
    blk = (lambda i: nb - 1 - i) if reverse else (lambda i: i)
    gate0 = 16 if reverse else 8
    return nb, blk, [
        pl.BlockSpec((HB, B_DK), lambda h, i: (blk(i), h)),
        pl.BlockSpec((HB, B_DK), lambda h, i: (blk(i), gate0 + h)),
        pl.BlockSpec((HB, 2 * B_DV), lambda h, i: (blk(i), 24 + h // 2)),
        pl.BlockSpec((1, B_DK), lambda h, i: (0, h)),
    ]


def _hgrn_fwd(zb, lb, *, reverse, name):
    T = zb.shape[0]
    nb, blk, in_specs = _hgrn_specs(T, reverse)
    order = range(7, -1, -1) if reverse else range(8)

    def body(q_ref, z_ref, v_ref, lb_ref, o_ref, st_ref, s_sc):
        h, i = pl.program_id(0), pl.program_id(1)

        @pl.when(i == 0)
        def _():
            s_sc[...] = jnp.zeros(s_sc.shape, F32)

        g = _hgrn_gates(q_ref[...], z_ref[...], lb_ref[...], reverse)
        vb = v_ref[...]
        v = jnp.where(h % 2 == 0, vb[:, :B_DV], vb[:, B_DV:]).astype(BF16)
        qd, ki, ke = g["qd"].astype(BF16), g["ki"].astype(BF16), g["ke"].astype(BF16)
        a = jnp.where(g["incl"], _dot(qd, ki, _NT), 0.0)
        o = _dot(a.astype(BF16), v, _NN)
        st = s_sc[...]
        pieces = [None] * 8
        for c in order:
            rows = slice(B_CHUNK * c, B_CHUNK * (c + 1))
            st_ref[0, c] = st
            pieces[c] = _dot(qd[rows], st.astype(BF16), _NT)
            st = st * g["dec"][B_CHUNK * c:B_CHUNK * c + 1, :] + _dot(v[rows], ke[rows], _TN)
        s_sc[...] = st
        o_ref[0] = o + jnp.concatenate(pieces, axis=0)

    return _pallas(
        body, name=name, grid=(B_HEADS, nb), in_specs=in_specs,
        out_specs=[pl.BlockSpec((1, HB, B_DV), lambda h, i: (h, blk(i), 0)),
                   pl.BlockSpec((1, 8, B_DV, B_DK), lambda h, i: (h, blk(i), 0, 0))],
        out_shape=[jax.ShapeDtypeStruct((B_HEADS, T, B_DV), F32),
                   jax.ShapeDtypeStruct((B_HEADS, T // B_CHUNK, B_DV, B_DK), F32)],
        scratch_shapes=[pltpu.VMEM((B_DV, B_DK), F32)],
        compiler_params=_params("parallel", "arbitrary"))(zb, zb, zb, lb)


def _hgrn_bwd(zb, lb, do, states, *, reverse, name):
    T = zb.shape[0]
    nb, blk, in_specs = _hgrn_specs(T, not reverse)
    in_specs[1] = pl.BlockSpec((HB, B_DK), lambda h, i: (blk(i), (16 if reverse else 8) + h))
    order = range(8) if reverse else range(7, -1, -1)

    def body(q_ref, z_ref, v_ref, lb_ref, do_ref, st_ref, dq_ref, dz_ref, dv_ref, dlb_ref, ds_sc):
        h, i = pl.program_id(0), pl.program_id(1)

        @pl.when(i == 0)
        def _():
            ds_sc[...] = jnp.zeros(ds_sc.shape, F32)
            dlb_ref[...] = jnp.zeros(dlb_ref.shape, F32)

        lb = lb_ref[...]
        g = _hgrn_gates(q_ref[...], z_ref[...], lb, reverse)
        vb = v_ref[...]
        v = jnp.where(h % 2 == 0, vb[:, :B_DV], vb[:, B_DV:]).astype(BF16)
        dout = do_ref[0].astype(BF16)
        qd, ki, ke = g["qd"].astype(BF16), g["ki"].astype(BF16), g["ke"].astype(BF16)
        a = jnp.where(g["incl"], _dot(qd, ki, _NT), 0.0).astype(BF16)
        da = jnp.where(g["incl"], _dot(dout, v, _NT), 0.0).astype(BF16)
        dv = _dot(a, dout, _TN)
        dqd = _dot(da, ki, _NN)
        dki = _dot(da, qd, _TN)
        dst = ds_sc[...]
        dv_p, dqd_p, dke_p, ddec_p = [None] * 8, [None] * 8, [None] * 8, [None] * 8
        for c in order:
            rows = slice(B_CHUNK * c, B_CHUNK * (c + 1))
            st = st_ref[0, c]
            dst_b = dst.astype(BF16)
            dv_p[c] = _dot(ke[rows], dst_b, _NT)
            dqd_p[c] = _dot(dout[rows], st.astype(BF16), _NN)
            dke_p[c] = _dot(v[rows], dst_b, _NN)
            dec_c = g["dec"][B_CHUNK * c:B_CHUNK * c + 1, :]
            ddec_p[c] = jnp.broadcast_to(jnp.sum(dst * st, axis=0, keepdims=True) * dec_c, (B_CHUNK, B_DK))
            dst = dst * dec_c + _dot(dout[rows], qd[rows], _TN)
        ds_sc[...] = dst
        dv = dv + jnp.concatenate(dv_p, axis=0)
        dqd = dqd + jnp.concatenate(dqd_p, axis=0)
        dke = jnp.concatenate(dke_p, axis=0)
        db = dqd * g["qd"] - dki * g["ki"] - dke * g["ke"]
        _, incl_t = _chunk_masks(not reverse)
        dlf = (_mask_mm(incl_t.astype(BF16), db) + _mask_mm(g["same"].astype(BF16), dke * g["ke"])
               + jnp.concatenate(ddec_p, axis=0))
        dk = dki * g["enb"] + dke * g["er"]
        u = jnp.where(g["f"] > TINY, dlf / g["f"], 0.0) - dk
        sg = g["sg"]
        dq_ref[...] = dqd * g["eb"]
        dz_ref[...] = u * (1.0 - lb) * sg * (1.0 - sg)
        dv_ref[0] = dv
        dlb_ref[...] += jnp.sum(u * (1.0 - sg), axis=0, keepdims=True)

    col = pl.BlockSpec((HB, B_DK), lambda h, i: (blk(i), h))
    hm = pl.BlockSpec((1, HB, B_DV), lambda h, i: (h, blk(i), 0))
    return _pallas(
        body, name=name, grid=(B_HEADS, nb),
        in_specs=in_specs + [hm, pl.BlockSpec((1, 8, B_DV, B_DK), lambda h, i: (h, blk(i), 0, 0))],
        out_specs=[col, col, hm, pl.BlockSpec((1, B_DK), lambda h, i: (0, h))],
        out_shape=[jax.ShapeDtypeStruct((T, B_HEADS * B_DK), F32), jax.ShapeDtypeStruct((T, B_HEADS * B_DK), F32),
                   jax.ShapeDtypeStruct((B_HEADS, T, B_DV), F32), jax.ShapeDtypeStruct((1, B_HEADS * B_DK), F32)],
        scratch_shapes=[pltpu.VMEM((B_DV, B_DK), F32)],
        compiler_params=_params("parallel", "arbitrary"))(zb, zb, zb, lb, do, states)


def _hgrn_out(of, ob, zb, gout, *, name):
    T = zb.shape[0]
    tm = _tile(T, 512, 8)

    def body(of_ref, ob_ref, g_ref, gout_ref, y_ref):
        for h in range(B_HEADS):
            o = of_ref[h] + ob_ref[h]
            r = lax.rsqrt(jnp.mean(o * o, axis=-1, keepdims=True) + EPS)
            gh = g_ref[:, B_DV * h:B_DV * (h + 1)]
            y_ref[:, B_DV * h:B_DV * (h + 1)] = (o * r * gout_ref[...] * (gh * jax.nn.sigmoid(gh))).astype(BF16)

    hm = pl.BlockSpec((B_HEADS, tm, B_DV), lambda i: (0, i, 0))
    return _pallas(
        body, name=name, grid=(T // tm,),
        in_specs=[hm, hm, pl.BlockSpec((tm, 512), lambda i: (i, 7)), pl.BlockSpec((1, B_DV), lambda i: (0, 0))],
        out_specs=pl.BlockSpec((tm, 512), lambda i: (i, 0)),
        out_shape=jax.ShapeDtypeStruct((T, 512), BF16), compiler_params=_params("parallel"))(of, ob, zb, gout.reshape(1, B_DV))


def _hgrn_out_bwd(of, ob, zb, gout, dy, *, name):
    T = zb.shape[0]
    tm = _tile(T, 512, 8)

    def body(of_ref, ob_ref, g_ref, gout_ref, dy_ref, do_ref, dg_ref, dgo_ref):
        gout_v = gout_ref[...]
        acc = jnp.zeros((1, B_DV), F32)
        for h in range(B_HEADS):
            o = of_ref[h] + ob_ref[h]
            r = lax.rsqrt(jnp.mean(o * o, axis=-1, keepdims=True) + EPS)
            oh = o * r
            gh = g_ref[:, B_DV * h:B_DV * (h + 1)]
            sg = jax.nn.sigmoid(gh)
            dyh = dy_ref[:, B_DV * h:B_DV * (h + 1)].astype(F32)
            dn = dyh * (gh * sg)
            dg_ref[:, B_DV * h:B_DV * (h + 1)] = dyh * (oh * gout_v) * (sg * (1.0 + gh * (1.0 - sg)))
            dxh = dn * gout_v
            do_ref[h] = r * (dxh - oh * jnp.mean(dxh * oh, axis=-1, keepdims=True))
            acc = acc + jnp.sum(dn * oh, axis=0, keepdims=True)

        @pl.when(pl.program_id(0) == 0)
        def _():
            dgo_ref[...] = acc

        @pl.when(pl.program_id(0) > 0)
        def _():
            dgo_ref[...] += acc

    hm = pl.BlockSpec((B_HEADS, tm, B_DV), lambda i: (0, i, 0))
    row = pl.BlockSpec((tm, 512), lambda i: (i, 0))
    return _pallas(
        body, name=name, grid=(T // tm,),
        in_specs=[hm, hm, pl.BlockSpec((tm, 512), lambda i: (i, 7)), pl.BlockSpec((1, B_DV), lambda i: (0, 0)), row],
        out_specs=[hm, row, pl.BlockSpec((1, B_DV), lambda i: (0, 0))],
        out_shape=[jax.ShapeDtypeStruct((B_HEADS, T, B_DV), F32), jax.ShapeDtypeStruct((T, 512), F32),
                   jax.ShapeDtypeStruct((1, B_DV), F32)],
        compiler_params=_params("arbitrary"))(of, ob, zb, gout.reshape(1, B_DV), dy)


def _dzb_assemble(dq_f, dq_b, dzf, dzb_, dv_f, dv_b, dgate, *, name):
    T = dq_f.shape[0]
    tm = _tile(T, 256, 8)

    def body(qf, qb, zf, zr, vf, vr, dg, o_ref):
        o_ref[:, 0:1024] = (qf[...] + qb[...]).astype(BF16)
        o_ref[:, 1024:2048] = zf[...].astype(BF16)
        o_ref[:, 2048:3072] = zr[...].astype(BF16)
        for h in range(B_HEADS):
            o_ref[:, 3072 + B_DV * h:3072 + B_DV * (h + 1)] = (vf[h] + vr[h]).astype(BF16)
        o_ref[:, 3584:4096] = dg[...].astype(BF16)

    wide = pl.BlockSpec((tm, 1024), lambda i: (i, 0))
    hm = pl.BlockSpec((B_HEADS, tm, B_DV), lambda i: (0, i, 0))
    return _pallas(
        body, name=name, grid=(T // tm,),
        in_specs=[wide, wide, wide, wide, hm, hm, pl.BlockSpec((tm, 512), lambda i: (i, 0))],
        out_specs=pl.BlockSpec((tm, 4096), lambda i: (i, 0)),
        out_shape=jax.ShapeDtypeStruct((T, 4096), BF16), compiler_params=_params("parallel"))(dq_f, dq_b, dzf, dzb_, dv_f, dv_b, dgate)


C_SPAN = 3 * C_BLOCK
C_G = C_HEADS // C_KV_HEADS


def _t5_bucket(rel):
    nb = REL_BUCKETS // 2
    max_exact = nb // 2
    ret = (rel > 0).astype(jnp.int32) * nb
    n = jnp.abs(rel)
    large = max_exact + (jnp.log(jnp.maximum(n, 1).astype(F32) / max_exact)
                         / math.log(REL_MAX_DIST / max_exact) * (nb - max_exact)).astype(jnp.int32)
    large = jnp.minimum(large, nb - 1)
    return ret + jnp.where(n < max_exact, n, large)


def _swa_buckets():
    rel = jnp.arange(C_SPAN)[None, :] - C_BLOCK - jnp.arange(C_BLOCK)[:, None]
    return _t5_bucket(rel)


def _swa_specs(T):
    nb = T // C_BLOCK
    return nb, [
        pl.BlockSpec((C_BLOCK, 512), lambda n: (n, 0)),
        pl.BlockSpec((C_BLOCK, LANES), lambda n: (jnp.maximum(n - 1, 0), 4)),
        pl.BlockSpec((C_BLOCK, LANES), lambda n: (n, 4)),
        pl.BlockSpec((C_BLOCK, LANES), lambda n: (jnp.minimum(n + 1, nb - 1), 4)),
        pl.BlockSpec((C_BLOCK, LANES), lambda n: (jnp.maximum(n - 1, 0), 5)),
        pl.BlockSpec((C_BLOCK, LANES), lambda n: (n, 5)),
        pl.BlockSpec((C_BLOCK, LANES), lambda n: (jnp.minimum(n + 1, nb - 1), 5)),
        pl.BlockSpec((C_HEADS, C_BLOCK, C_SPAN), lambda n: (0, 0, 0)),
        pl.BlockSpec(memory_space=pltpu.SMEM),
    ]


def _swa_valid(n, T):
    qi = lax.broadcasted_iota(jnp.int32, (C_BLOCK, C_SPAN), 0)
    si = lax.broadcasted_iota(jnp.int32, (C_BLOCK, C_SPAN), 1)
    rel = si - C_BLOCK - qi
    kpos = (n - 1) * C_BLOCK + si
    return (jnp.abs(rel) <= C_WINDOW) & (kpos >= 0) & (kpos < T)


def _swa_probs(qh, kh, bias, valid, sink):
    s = _dot(qh, kh, _NT) * (C_DH ** -0.5) + bias
    s = jnp.where(valid, s, MASK_VALUE)
    m = jnp.maximum(jnp.max(s, axis=-1, keepdims=True), sink)
    e = jnp.exp(s - m)
    den = jnp.sum(e, axis=-1, keepdims=True) + jnp.exp(sink - m)
    return e / den, jnp.exp(sink - m) / den


def _swa_fwd(zc, bias, sink, *, name):
    T = zc.shape[0]
    nb, in_specs = _swa_specs(T)

    def body(q_ref, kp, kc, kn, vp, vc, vn, bias_ref, sink_ref, y_ref):
        n = pl.program_id(0)
        kcat = jnp.concatenate([kp[...], kc[...], kn[...]], axis=0)
        vcat = jnp.concatenate([vp[...], vc[...], vn[...]], axis=0)
        valid = _swa_valid(n, T)
        for h in range(C_HEADS):
            kv = h // C_G
            p, _ = _swa_probs(q_ref[:, C_DH * h:C_DH * (h + 1)], kcat[:, C_DH * kv:C_DH * (kv + 1)], bias_ref[h], valid, sink_ref[h])
            y_ref[:, C_DH * h:C_DH * (h + 1)] = _dot(p.astype(BF16), vcat[:, C_DH * kv:C_DH * (kv + 1)], _NN).astype(BF16)

    return _pallas(
        body, name=name, grid=(nb,), in_specs=in_specs, out_specs=pl.BlockSpec((C_BLOCK, 512), lambda n: (n, 0)),
        out_shape=jax.ShapeDtypeStruct((T, 512), BF16), compiler_params=_params("parallel"))(zc, zc, zc, zc, zc, zc, zc, bias, sink)


def _swa_bwd(zc, bias, sink, dy, *, name):
    T = zc.shape[0]
    nb, in_specs = _swa_specs(T)
    scale = C_DH ** -0.5

    def body(q_ref, kp, kc, kn, vp, vc, vn, bias_ref, sink_ref, dy_ref, dq_ref, dkc_ref, dvc_ref, dbias_ref, dsink_ref):
        n = pl.program_id(0)

        @pl.when(n == 0)
        def _():
            dbias_ref[...] = jnp.zeros(dbias_ref.shape, F32)
            dsink_ref[...] = jnp.zeros(dsink_ref.shape, F32)

        kcat = jnp.concatenate([kp[...], kc[...], kn[...]], axis=0)
        vcat = jnp.concatenate([vp[...], vc[...], vn[...]], axis=0)
        valid = _swa_valid(n, T)
        for kv in range(C_KV_HEADS):
            kh = kcat[:, C_DH * kv:C_DH * (kv + 1)]
            vh = vcat[:, C_DH * kv:C_DH * (kv + 1)]
            dk_acc = jnp.zeros((C_SPAN, C_DH), F32)
            dv_acc = jnp.zeros((C_SPAN, C_DH), F32)
            for g in range(C_G):
                h = kv * C_G + g
                qh = q_ref[:, C_DH * h:C_DH * (h + 1)]
                doh = dy_ref[:, C_DH * h:C_DH * (h + 1)].astype(BF16)
                p, p_sink = _swa_probs(qh, kh, bias_ref[h], valid, sink_ref[h])
                dp = _dot(doh, vh, _NT)
                rowdot = jnp.sum(p * dp, axis=-1, keepdims=True)
                ds = p * (dp - rowdot)
                dbias_ref[h] += ds
                tot = jnp.sum(jnp.sum(-p_sink * rowdot, axis=0, keepdims=True), axis=1, keepdims=True)
                dsink_ref[h:h + 1, :] += jnp.broadcast_to(tot, (1, LANES))
                dsb = (ds * scale).astype(BF16)
                dq_ref[:, C_DH * h:C_DH * (h + 1)] = _dot(dsb, kh, _NN).astype(BF16)
                dk_acc = dk_acc + _dot(dsb, qh, _TN)
                dv_acc = dv_acc + _dot(p.astype(BF16), doh, _TN)
            dkc_ref[0, :, C_DH * kv:C_DH * (kv + 1)] = dk_acc
            dvc_ref[0, :, C_DH * kv:C_DH * (kv + 1)] = dv_acc

    part = pl.BlockSpec((1, C_SPAN, LANES), lambda n: (n, 0, 0))
    dq, dkc, dvc, dbias, dsink = _pallas(
        body, name=name, grid=(nb,), in_specs=in_specs + [pl.BlockSpec((C_BLOCK, 512), lambda n: (n, 0))],
        out_specs=[pl.BlockSpec((C_BLOCK, 512), lambda n: (n, 0)), part, part,
                   pl.BlockSpec((C_HEADS, C_BLOCK, C_SPAN), lambda n: (0, 0, 0)), pl.BlockSpec((C_HEADS, LANES), lambda n: (0, 0))],
        out_shape=[jax.ShapeDtypeStruct((T, 512), BF16), jax.ShapeDtypeStruct((nb, C_SPAN, LANES), F32),
                   jax.ShapeDtypeStruct((nb, C_SPAN, LANES), F32), jax.ShapeDtypeStruct((C_HEADS, C_BLOCK, C_SPAN), F32),
                   jax.ShapeDtypeStruct((C_HEADS, LANES), F32)],
        compiler_params=_params("arbitrary"))(zc, zc, zc, zc, zc, zc, zc, bias, sink, dy)

    def combine(dq_ref, kp, kc, kn, vp, vc, vn, o_ref):
        n = pl.program_id(0)
        lo = (n > 0).astype(F32)
        hi = (n < nb - 1).astype(F32)
        o_ref[:, 0:512] = dq_ref[...]
        o_ref[:, 512:640] = (kp[0] * lo + kc[0] + kn[0] * hi).astype(BF16)
        o_ref[:, 640:768] = (vp[0] * lo + vc[0] + vn[0] * hi).astype(BF16)

    prev = pl.BlockSpec((1, C_BLOCK, LANES), lambda n: (jnp.maximum(n - 1, 0), 2, 0))
    cur = pl.BlockSpec((1, C_BLOCK, LANES), lambda n: (n, 1, 0))
    nxt = pl.BlockSpec((1, C_BLOCK, LANES), lambda n: (jnp.minimum(n + 1, nb - 1), 0, 0))
    dzc = _pallas(
        combine, name=name + "_combine", grid=(nb,),
        in_specs=[pl.BlockSpec((C_BLOCK, 512), lambda n: (n, 0)), prev, cur, nxt, prev, cur, nxt],
        out_specs=pl.BlockSpec((C_BLOCK, 768), lambda n: (n, 0)),
        out_shape=jax.ShapeDtypeStruct((T, 768), BF16), compiler_params=_params("parallel"))(dq, dkc, dkc, dkc, dvc, dvc, dvc)
    return dzc, dbias, dsink


def _merge_tiles(T):
    return _tile(T, 512, 8), 512


def _merge_fwd(ya, yb, yc, wa, wb, wc, zg, *, name):
    T = ya.shape[0]
    tm, tn = _merge_tiles(T)
    nd = D_MODEL // tn

    def body(ya_ref, yb_ref, yc_ref, wa_ref, wb_ref, wc_ref, ga_ref, gb_ref, gc_ref, o_ref):
        acc = jax.nn.sigmoid(ga_ref[...]) * _dot(ya_ref[...].astype(BF16), wa_ref[...], _NN)
        acc += jax.nn.sigmoid(gb_ref[...]) * _dot(yb_ref[...].astype(BF16), wb_ref[...], _NN)
        acc += jax.nn.sigmoid(gc_ref[...]) * _dot(yc_ref[...].astype(BF16), wc_ref[...], _NN)
        o_ref[...] = acc.astype(BF16)

    y = pl.BlockSpec((tm, 512), lambda i, j: (i, 0))
    w = pl.BlockSpec((512, tn), lambda i, j: (0, j))
    gate = lambda b: pl.BlockSpec((tm, tn), lambda i, j: (i, b * nd + j))
    return _pallas(
        body, name=name, grid=(T // tm, nd), in_specs=[y, y, y, w, w, w, gate(0), gate(1), gate(2)],
        out_specs=pl.BlockSpec((tm, tn), lambda i, j: (i, j)),
        out_shape=jax.ShapeDtypeStruct((T, D_MODEL), BF16),
        compiler_params=_params("parallel", "parallel"))(ya, yb, yc, wa, wb, wc, zg, zg, zg)


def _merge_bwd(ya, yb, yc, wa, wb, wc, zg, dm, *, name):
    T = ya.shape[0]
    tm, tn = _merge_tiles(T)
    nd = D_MODEL // tn

    def body(ya_ref, yb_ref, yc_ref, wa_ref, wb_ref, wc_ref, ga_ref, gb_ref, gc_ref, dm_ref, *outs):
        dmv = dm_ref[...].astype(F32)
        for y_ref, w_ref, g_ref, du_ref, dg_ref in zip((ya_ref, yb_ref, yc_ref), (wa_ref, wb_ref, wc_ref),
                                                       (ga_ref, gb_ref, gc_ref), outs[:3], outs[3:]):
            u = _dot(y_ref[...].astype(BF16), w_ref[...], _NN)
            sg = jax.nn.sigmoid(g_ref[...])
            du_ref[...] = (dmv * sg).astype(BF16)
            dg_ref[...] = (dmv * u * sg * (1.0 - sg)).astype(BF16)

    y = pl.BlockSpec((tm, 512), lambda i, j: (i, 0))
    w = pl.BlockSpec((512, tn), lambda i, j: (0, j))
    gate = lambda b: pl.BlockSpec((tm, tn), lambda i, j: (i, b * nd + j))
    t = pl.BlockSpec((tm, tn), lambda i, j: (i, j))
    return _pallas(
        body, name=name, grid=(T // tm, nd), in_specs=[y, y, y, w, w, w, gate(0), gate(1), gate(2), t],
        out_specs=[t] * 6, out_shape=[jax.ShapeDtypeStruct((T, D_MODEL), BF16)] * 6,
        compiler_params=_params("parallel", "parallel"))(ya, yb, yc, wa, wb, wc, zg, zg, zg, dm)


def _cross_fwd(q, kvm, *, name):
    T = q.shape[0]
    M = kvm.shape[0]
    tm = _tile(T, 512, 8)
    scale = X_DH ** -0.5

    def body(q_ref, k_ref, v_ref, o_ref):
        for h in range(X_HEADS):
            cs = slice(X_DH * h, X_DH * (h + 1))
            s = _dot(q_ref[:, cs], k_ref[:, cs], _NT) * scale
            e = jnp.exp(s - jnp.max(s, axis=-1, keepdims=True))
            p = e / jnp.sum(e, axis=-1, keepdims=True)
            o_ref[:, cs] = _dot(p.astype(BF16), v_ref[:, cs], _NN).astype(BF16)

    row = pl.BlockSpec((tm, D_MODEL), lambda i: (i, 0))
    return _pallas(
        body, name=name, grid=(T // tm,),
        in_specs=[row, pl.BlockSpec((M, D_MODEL), lambda i: (0, 0)), pl.BlockSpec((M, D_MODEL), lambda i: (0, 1))],
        out_specs=row, out_shape=jax.ShapeDtypeStruct((T, D_MODEL), BF16), compiler_params=_params("parallel"))(q, kvm, kvm)


def _cross_bwd(q, kvm, do, *, name):
    T = q.shape[0]
    M = kvm.shape[0]
    tm = _tile(T, 512, 8)
    scale = X_DH ** -0.5

    def body(q_ref, k_ref, v_ref, do_ref, dq_ref, dkv_ref):
        @pl.when(pl.program_id(0) == 0)
        def _():
            dkv_ref[...] = jnp.zeros(dkv_ref.shape, F32)

        for h in range(X_HEADS):
            cs = slice(X_DH * h, X_DH * (h + 1))
            vs = slice(D_MODEL + X_DH * h, D_MODEL + X_DH * (h + 1))
            qh, kh, doh = q_ref[:, cs], k_ref[:, cs], do_ref[:, cs]
            s = _dot(qh, kh, _NT) * scale
            e = jnp.exp(s - jnp.max(s, axis=-1, keepdims=True))
            p = e / jnp.sum(e, axis=-1, keepdims=True)
            dp = _dot(doh, v_ref[:, cs], _NT)
            ds = (p * (dp - jnp.sum(p * dp, axis=-1, keepdims=True)) * scale).astype(BF16)
            dq_ref[:, cs] = _dot(ds, kh, _NN).astype(BF16)
            dkv_ref[:, cs] += _dot(ds, qh, _TN)
            dkv_ref[:, vs] += _dot(p.astype(BF16), doh, _TN)

    row = pl.BlockSpec((tm, D_MODEL), lambda i: (i, 0))
    return _pallas(
        body, name=name, grid=(T // tm,),
        in_specs=[row, pl.BlockSpec((M, D_MODEL), lambda i: (0, 0)), pl.BlockSpec((M, D_MODEL), lambda i: (0, 1)), row],
        out_specs=[row, pl.BlockSpec((M, 2 * D_MODEL), lambda i: (0, 0))],
        out_shape=[jax.ShapeDtypeStruct((T, D_MODEL), BF16), jax.ShapeDtypeStruct((M, 2 * D_MODEL), F32)],
        compiler_params=_params("arbitrary"))(q, kvm, kvm, do)


def _ffn_up(h, w1, w3, *, name):
    T = h.shape[0]
    tm = _tile(T, 512, 8)
    tn = 256

    def body(h_ref, w1_ref, w3_ref, a_ref, b_ref, act_ref):
        hv = h_ref[...]
        a = _dot(hv, w1_ref[...], _NN)
        b = _dot(hv, w3_ref[...], _NN)
        a_ref[...] = a
        b_ref[...] = b
        act_ref[...] = (a * jax.nn.sigmoid(a) * b).astype(BF16)

    w = pl.BlockSpec((D_MODEL, tn), lambda i, j: (0, j))
    t = pl.BlockSpec((tm, tn), lambda i, j: (i, j))
    return _pallas(
        body, name=name, grid=(T // tm, D_FF // tn), in_specs=[pl.BlockSpec((tm, D_MODEL), lambda i, j: (i, 0)), w, w],
        out_specs=[t, t, t],
        out_shape=[jax.ShapeDtypeStruct((T, D_FF), F32), jax.ShapeDtypeStruct((T, D_FF), F32), jax.ShapeDtypeStruct((T, D_FF), BF16)],
        compiler_params=_params("parallel", "parallel"))(h, w1, w3)


def _ffn_dact(dx, w2, a, b, *, name):
    T = dx.shape[0]
    tm = _tile(T, 512, 8)
    tn = 256

    def body(dx_ref, w2_ref, a_ref, b_ref, da_ref, db_ref):
        dact = _dot(dx_ref[...].astype(BF16), w2_ref[...], _NT)
        av = a_ref[...]
        sg = jax.nn.sigmoid(av)
        da_ref[...] = (dact * b_ref[...] * (sg * (1.0 + av * (1.0 - sg)))).astype(BF16)
        db_ref[...] = (dact * (av * sg)).astype(BF16)

    t = pl.BlockSpec((tm, tn), lambda i, j: (i, j))
    return _pallas(
        body, name=name, grid=(T // tm, D_FF // tn),
        in_specs=[pl.BlockSpec((tm, D_MODEL), lambda i, j: (i, 0)), pl.BlockSpec((tn, D_MODEL), lambda i, j: (j, 0)), t, t],
        out_specs=[t, t], out_shape=[jax.ShapeDtypeStruct((T, D_FF), BF16)] * 2,
        compiler_params=_params("parallel", "parallel"))(dx, w2, a, b)


def _loss_head(x, g, target, *, name):
    T, D = x.shape
    tm = _tile(T, 512, 8)

    def body(x_ref, g_ref, t_ref, loss_ref, dx_ref, dg_ref):
        xv = x_ref[...]
        r = lax.rsqrt(jnp.mean(xv * xv, axis=-1, keepdims=True) + EPS)
        xh = xv * r
        gv = g_ref[...]
        err = xh * gv - t_ref[...]
        dy = err * (1.0 / D)
        dxh = dy * gv
        dx_ref[...] = r * (dxh - xh * jnp.mean(dxh * xh, axis=-1, keepdims=True))
        lpart = 0.5 * jnp.sum(jnp.mean(err * err, axis=-1, keepdims=True), axis=0, keepdims=True)
        gpart = jnp.sum(dy * xh, axis=0, keepdims=True)

        @pl.when(pl.program_id(0) == 0)
        def _():
            loss_ref[...] = jnp.broadcast_to(lpart, (1, LANES))
            dg_ref[...] = gpart

        @pl.when(pl.program_id(0) > 0)
        def _():
            loss_ref[...] += jnp.broadcast_to(lpart, (1, LANES))
            dg_ref[...] += gpart

    row = pl.BlockSpec((tm, D), lambda i: (i, 0))
    vec = pl.BlockSpec((1, D), lambda i: (0, 0))
    return _pallas(
        body, name=name, grid=(T // tm,), in_specs=[row, vec, row],
        out_specs=[pl.BlockSpec((1, LANES), lambda i: (0, 0)), row, vec],
        out_shape=[jax.ShapeDtypeStruct((1, LANES), F32), jax.ShapeDtypeStruct((T, D), F32), jax.ShapeDtypeStruct((1, D), F32)],
        compiler_params=_params("arbitrary"))(x, g.reshape(1, D), target)


IN_CQ, IN_CKV, IN_KR, IN_B, IN_C, IN_G, IN_END = 0, 384, 640, 672, 4768, 5536, 8608
WEIGHT_NAMES = ("w_in", "g_mix", "a_gq", "a_gkv", "a_wuq", "a_wukv", "b_lb", "b_gout", "c_sink", "rel_bias",
                "w_br_a", "w_br_b", "w_br_c", "w_out", "g_x", "g_mem", "x_wq", "x_wkv", "x_wo", "g_ffn",
                "f_w1", "f_w3", "f_w2", "g_final")


def _lower_bounds(b_lb):
    sm = jax.nn.softmax(b_lb.astype(F32), axis=1)
    return jnp.cumsum(sm, axis=1) - sm[:, :1]


def _layer_weights(w, l):
    bf = lambda a: a.astype(BF16)
    w_in = bf(w["w_in"][l])
    kr = w_in[:, IN_KR:IN_B]
    wa = jnp.concatenate([w_in[:, IN_CQ:IN_CKV], kr, _rope_swap_cols(kr), jnp.zeros((D_MODEL, 64), BF16),
                          w_in[:, IN_CKV:IN_KR]], axis=1)
    wuq = bf(w["a_wuq"][l]).reshape(A_Q_RANK, A_HEADS, A_QK)
    wuq_sw = jnp.concatenate([jnp.zeros((A_Q_RANK, A_HEADS, A_NOPE), BF16), _rope_swap_cols(wuq[..., A_NOPE:])], axis=-1)
    wq2 = jnp.concatenate([wuq.reshape(A_Q_RANK, -1), wuq_sw.reshape(A_Q_RANK, -1)], axis=1)
    wukv = bf(w["a_wukv"][l]).reshape(A_KV_RANK, A_HEADS, A_NOPE + A_V)
    wkv = jnp.concatenate([wukv[..., :A_NOPE].reshape(A_KV_RANK, -1), wukv[..., A_NOPE:].reshape(A_KV_RANK, -1)], axis=1)
    return dict(wa=wa, wb=w_in[:, IN_B:IN_C], wc=w_in[:, IN_C:IN_G], wg=w_in[:, IN_G:IN_END], wq2=wq2, wkv=wkv,
                w_br_a=bf(w["w_br_a"][l]), w_br_b=bf(w["w_br_b"][l]), w_br_c=bf(w["w_br_c"][l]), w_out=bf(w["w_out"][l]),
                x_wq=bf(w["x_wq"][l]), x_wkv=bf(w["x_wkv"][l]), x_wo=bf(w["x_wo"][l]),
                f_w1=bf(w["f_w1"][l]), f_w3=bf(w["f_w3"][l]), f_w2=bf(w["f_w2"][l]))


def _layer_fwd(l, x, mem, w, lw, lower, bias, tabs):
    n = lambda s: f"l{l}_{s}"
    cq_t, sq_t, ck_t = tabs
    s = dict(x=x)
    s["h0"] = h0 = _rms(x, w["g_mix"][l], name=n("rms_mix"))
    s["za"] = za = _mm(h0, lw["wa"], name=n("in_a"))
    s["zb"] = zb = _mm(h0, lw["wb"], name=n("in_b"))
    s["zc"] = zc = _mm(h0, lw["wc"], out_dtype=BF16, name=n("in_c"))
    s["zg"] = zg = _mm(h0, lw["wg"], name=n("in_g"))
    s["cqn"] = cqn = _rms(za, w["a_gq"][l], col=0, width=A_Q_RANK, name=n("rms_cq"))
    s["ckvn"] = ckvn = _rms(za, w["a_gkv"][l], col=512, width=A_KV_RANK, name=n("rms_ckv"))
    s["q"] = q = _qrope(_mm(cqn, lw["wq2"], name=n("uq")), cq_t, sq_t, name=n("qrope"))
    s["kv"] = kv = _mm(ckvn, lw["wkv"], out_dtype=BF16, name=n("ukv"))
    s["k"] = k = _kprep(kv, za, ck_t, name=n("kprep"))
    s["ya"], s["lse"] = ya, _ = _flash_fwd(q, k, kv, name=n("mla"))
    lb_f, lb_b = lower[0, l].reshape(1, -1), lower[1, l].reshape(1, -1)
    s["of"], s["stf"] = of, _ = _hgrn_fwd(zb, lb_f, reverse=False, name=n("hgrn_f"))
    s["ob"], s["stb"] = ob, _ = _hgrn_fwd(zb, lb_b, reverse=True, name=n("hgrn_b"))
    s["yb"] = yb = _hgrn_out(of, ob, zb, w["b_gout"][l], name=n("hgrn_out"))
    s["yc"] = yc = _swa_fwd(zc, bias, w["c_sink"][l], name=n("swa"))
    s["merged"] = merged = _merge_fwd(ya, yb, yc, lw["w_br_a"], lw["w_br_b"], lw["w_br_c"], zg, name=n("merge"))
    s["x1"] = x1 = _mm(merged, lw["w_out"], add=x, name=n("out"))
    s["h1"] = h1 = _rms(x1, w["g_x"][l], name=n("rms_x"))
    s["qx"] = qx = _mm(h1, lw["x_wq"], out_dtype=BF16, name=n("xq"))
    s["memn"] = memn = _rms(mem, w["g_mem"][l], name=n("rms_mem"))
    s["kvm"] = kvm = _mm(memn, lw["x_wkv"], out_dtype=BF16, name=n("xkv"))
    s["ox"] = ox = _cross_fwd(qx, kvm, name=n("cross"))
    s["x2"] = x2 = _mm(ox, lw["x_wo"], add=x1, name=n("xo"))
    s["h2"] = h2 = _rms(x2, w["g_ffn"][l], name=n("rms_ffn"))
    s["a"], s["b"], s["act"] = a, b, act = _ffn_up(h2, lw["f_w1"], lw["f_w3"], name=n("ffn_up"))
    x3 = _mm(act, lw["f_w2"], add=x2, name=n("ffn_down"))
    return x3, s


def _layer_bwd(l, dx3, mem, w, lw, lower, bias, tabs, s):
    n = lambda t: f"l{l}_b_{t}"
    cq_t, sq_t, ck_t = tabs
    g = {}
    da, db = _ffn_dact(dx3, lw["f_w2"], s["a"], s["b"], name=n("ffn_dact"))
    g["f_w2"] = _mm(s["act"], dx3, mode="tn", name=n("dw2"))
    dh2 = _mm(db, lw["f_w3"], mode="nt", add=_mm(da, lw["f_w1"], mode="nt", name=n("dh2a")), name=n("dh2b"))
    g["f_w1"] = _mm(s["h2"], da, mode="tn", name=n("dw1"))
    g["f_w3"] = _mm(s["h2"], db, mode="tn", name=n("dw3"))
    dx2, g["g_ffn"] = _rms_bwd(s["x2"], w["g_ffn"][l], dh2, res=dx3, name=n("rms_ffn"))
    dox = _mm(dx2, lw["x_wo"], mode="nt", out_dtype=BF16, name=n("dox"))
    g["x_wo"] = _mm(s["ox"], dx2, mode="tn", name=n("dwo"))
    dqx, dkvm = _cross_bwd(s["qx"], s["kvm"], dox, name=n("cross"))
    g["x_wq"] = _mm(s["h1"], dqx, mode="tn", name=n("dwq"))
    dh1 = _mm(dqx, lw["x_wq"], mode="nt", name=n("dh1"))
    g["x_wkv"] = _mm(s["memn"], dkvm, mode="tn", name=n("dwkv"))
    dmemn = _mm(dkvm, lw["x_wkv"], mode="nt", name=n("dmemn"))
    _, g["g_mem"] = _rms_bwd(mem, w["g_mem"][l], dmemn, name=n("rms_mem"))
    dx1, g["g_x"] = _rms_bwd(s["x1"], w["g_x"][l], dh1, res=dx2, name=n("rms_x"))
    dmerged = _mm(dx1, lw["w_out"], mode="nt", name=n("dmerged"))
    g["w_out"] = _mm(s["merged"], dx1, mode="tn", name=n("dwout"))
    dua, dub, duc, dga, dgb, dgc = _merge_bwd(s["ya"], s["yb"], s["yc"], lw["w_br_a"], lw["w_br_b"], lw["w_br_c"],
                                              s["zg"], dmerged, name=n("merge"))
    dya = _mm(dua, lw["w_br_a"], mode="nt", out_dtype=BF16, name=n("dya"))
    dyb = _mm(dub, lw["w_br_b"], mode="nt", name=n("dyb"))
    dyc = _mm(duc, lw["w_br_c"], mode="nt", out_dtype=BF16, name=n("dyc"))
    g["w_br_a"] = _mm(s["ya"], dua, mode="tn", name=n("dwbra"))
    g["w_br_b"] = _mm(s["yb"], dub, mode="tn", name=n("dwbrb"))
    g["w_br_c"] = _mm(s["yc"], duc, mode="tn", name=n("dwbrc"))
    dzc, dbias, dsink = _swa_bwd(s["zc"], bias, w["c_sink"][l], dyc, name=n("swa"))
    g["c_sink"] = dsink[:, 0]
    g["bias"] = dbias
    lb_f, lb_b = lower[0, l].reshape(1, -1), lower[1, l].reshape(1, -1)
    do_, dgate, dgout = _hgrn_out_bwd(s["of"], s["ob"], s["zb"], w["b_gout"][l], dyb, name=n("hgrn_out"))
    g["b_gout"] = dgout[0]
    dq_f, dzf, dv_f, dlb_f = _hgrn_bwd(s["zb"], lb_f, do_, s["stf"], reverse=False, name=n("hgrn_f"))
    dq_b, dzr, dv_b, dlb_b = _hgrn_bwd(s["zb"], lb_b, do_, s["stb"], reverse=True, name=n("hgrn_b"))
    g["lower"] = jnp.concatenate([dlb_f, dlb_b], axis=0)
    dzb = _dzb_assemble(dq_f, dq_b, dzf, dzr, dv_f, dv_b, dgate, name=n("dzb"))
    dq, dk, dv = _flash_bwd(s["q"], s["k"], s["kv"], s["ya"], dya, s["lse"], name=n("mla"))
    dq2 = _qrope_bwd(dq, cq_t, sq_t, name=n("qrope"))
    dcqn = _mm(dq2, lw["wq2"], mode="nt", name=n("dcqn"))
    dwq2 = _mm(s["cqn"], dq2, mode="tn", name=n("dwq2"))
    dkn, dzkr = _kprep_bwd(dk, ck_t, name=n("kprep"))
    dckvn = _mm(dv, lw["wkv"][:, 512:], mode="nt", add=_mm(dkn, lw["wkv"][:, :512], mode="nt", name=n("dckvn_k")), name=n("dckvn_v"))
    dwkn = _mm(s["ckvn"], dkn, mode="tn", name=n("dwkn"))
    dwv = _mm(s["ckvn"], dv, mode="tn", name=n("dwv"))
    dzcq, dgq = _rms_bwd(s["za"], w["a_gq"][l], dcqn, col=0, width=A_Q_RANK, out_dtype=BF16, name=n("rms_cq"))
    dzckv, dgkv = _rms_bwd(s["za"], w["a_gkv"][l], dckvn, col=512, width=A_KV_RANK, out_dtype=BF16, name=n("rms_ckv"))
    g["a_gq"], g["a_gkv"] = dgq[0], dgkv[0]
    sw = dwq2[:, A_HEADS * A_QK:].reshape(A_Q_RANK, A_HEADS, A_QK)
    sw = jnp.concatenate([jnp.zeros((A_Q_RANK, A_HEADS, A_NOPE), F32), _rope_unswap_cols(sw[..., A_NOPE:])], axis=-1)
    g["a_wuq"] = dwq2[:, :A_HEADS * A_QK] + sw.reshape(A_Q_RANK, -1)
    g["a_wukv"] = jnp.concatenate([dwkn.reshape(A_KV_RANK, A_HEADS, A_NOPE), dwv.reshape(A_KV_RANK, A_HEADS, A_V)], axis=-1).reshape(A_KV_RANK, -1)
    wa = lw["wa"]
    pieces = [(dzcq, wa[:, 0:384]), (dzkr, wa[:, 384:512]), (dzckv, wa[:, 512:768]), (dzb, lw["wb"]), (dzc, lw["wc"]),
              (dga, lw["wg"][:, 0:1024]), (dgb, lw["wg"][:, 1024:2048]), (dgc, lw["wg"][:, 2048:3072])]
    dh0 = None
    dws = []
    for i, (dz, wp) in enumerate(pieces):
        dh0 = _mm(dz, wp, mode="nt", add=dh0, name=n(f"dh0_{i}"))
        dws.append(_mm(s["h0"], dz, mode="tn", name=n(f"dwin_{i}")))
    dwkr = dws[1][:, 0:A_ROPE] + _rope_unswap_cols(dws[1][:, A_ROPE:2 * A_ROPE])
    g["w_in"] = jnp.concatenate([dws[0], dws[2], dwkr, dws[3], dws[4], dws[5], dws[6], dws[7]], axis=1)
    dx, g["g_mix"] = _rms_bwd(s["x"], w["g_mix"][l], dh0, res=dx1, name=n("rms_mix"))
    return dx, g


def _local_step(x, mem, target, w):
    T = x.shape[0]
    tabs = _rope_tables(T)
    lower, lower_vjp = jax.vjp(_lower_bounds, w["b_lb"])
    buckets = _swa_buckets()
    bias = jnp.transpose(w["rel_bias"].astype(F32)[buckets], (2, 0, 1))
    lws, saved = [], []
    for l in range(DEPTH):
        lws.append(_layer_weights(w, l))
        x, s = _layer_fwd(l, x, mem, w, lws[l], lower, bias, tabs)
        saved.append(s)
    loss, dx, dg_final = _loss_head(x, w["g_final"], target, name="loss_head")
    layer_grads = [None] * DEPTH
    for l in reversed(range(DEPTH)):
        dx, layer_grads[l] = _layer_bwd(l, dx, mem, w, lws[l], lower, bias, tabs, saved[l])
        saved[l] = None
    grads = {}
    for name in WEIGHT_NAMES:
        if name in layer_grads[0]:
            grads[name] = jnp.stack([layer_grads[l][name].reshape(w[name].shape[1:]) for l in range(DEPTH)])
    grads["g_final"] = dg_final[0]
    dlower = jnp.stack([layer_grads[l]["lower"] for l in range(DEPTH)], axis=1)
    grads["b_lb"] = lower_vjp(dlower)[0]
    dbias = layer_grads[0]["bias"] + layer_grads[1]["bias"]
    onehot = (buckets.reshape(-1)[:, None] == jnp.arange(REL_BUCKETS)[None, :]).astype(F32)
    grads["rel_bias"] = jnp.dot(onehot.T, dbias.reshape(C_HEADS, -1).T, precision=lax.Precision.HIGHEST)
    return loss, dx, grads


N_CHIPS = 4
PACK_COLS = 1024
PACK_ALIGN = 32 * PACK_COLS
SHARDED = (("w_in", 2), ("a_wuq", 2), ("a_wukv", 2), ("b_lb", 2), ("w_br_a", 2), ("w_br_b", 2), ("w_br_c", 2), ("w_out", 1),
           ("x_wq", 1), ("x_wkv", 2), ("x_wo", 1), ("f_w1", 2), ("f_w3", 2), ("f_w2", 1))
REPLICATED = ("g_mix", "a_gq", "a_gkv", "b_gout", "c_sink", "rel_bias", "g_x", "g_mem", "g_ffn", "g_final")
MESH_IDS = pl.DeviceIdType.MESH
ANY_SPEC = pl.BlockSpec(memory_space=pl.ANY)


def _pack(arrs, cols, align):
    flat = jnp.concatenate([a.reshape(-1) for a in arrs])
    pad = (-flat.shape[0]) % align
    return jnp.pad(flat, (0, pad)).reshape(-1, cols)


def _unpack(buf, shapes):
    flat = buf.reshape(-1)
    out, start = [], 0
    for shp in shapes:
        size = math.prod(shp)
        out.append(flat[start:start + size].reshape(shp))
        start += size
    return out


def _chip_peers():
    x, y, c = lax.axis_index("x"), lax.axis_index("y"), lax.axis_index("c")
    return x, y, c, [(1 - x, y), (x, 1 - y), (1 - x, 1 - y)]


def _chip_exchange(src, *, segmented, name):
    R, C = src.shape[-2:]

    def body(src_ref, out_ref, send_sems, recv_sems, local_sem):
        x, y, c, chips = _chip_peers()
        me = 2 * x + y
        piece = (lambda k: src_ref.at[k]) if segmented else (lambda k: src_ref)
        local = pltpu.make_async_copy(piece(me), out_ref.at[me], local_sem)
        local.start()

        def copy(j, seg, slot):
            px, py = chips[j]
            return pltpu.make_async_remote_copy(src_ref=piece(seg), dst_ref=out_ref.at[slot], send_sem=send_sems.at[j],
                                                recv_sem=recv_sems.at[j], device_id=(px, py, c), device_id_type=MESH_IDS)

        sends = [copy(j, 2 * px + py, me) for j, (px, py) in enumerate(chips)]
        for cp in sends:
            cp.start()
        for j, (px, py) in enumerate(chips):
            copy(j, me, 2 * px + py).wait_recv()
        for cp in sends:
            cp.wait_send()
        local.wait()

    return _pallas(
        body, name=name, in_specs=[ANY_SPEC], out_specs=ANY_SPEC, out_shape=jax.ShapeDtypeStruct((N_CHIPS, R, C), src.dtype),
        scratch_shapes=[pltpu.SemaphoreType.DMA((3,)), pltpu.SemaphoreType.DMA((3,)), pltpu.SemaphoreType.DMA],
        compiler_params=pltpu.CompilerParams(has_side_effects=True))(src)


def _pair_split(g, *, name):
    _, _, R, C = g.shape

    def body(g_ref, kept_ref, got_ref, send_sems, recv_sems, local_sems):
        x, y, c = lax.axis_index("x"), lax.axis_index("y"), lax.axis_index("c")
        local = [pltpu.make_async_copy(g_ref.at[k, c], kept_ref.at[k], local_sems.at[k]) for k in range(N_CHIPS)]
        remote = [pltpu.make_async_remote_copy(src_ref=g_ref.at[k, 1 - c], dst_ref=got_ref.at[k], send_sem=send_sems.at[k],
                                               recv_sem=recv_sems.at[k], device_id=(x, y, 1 - c), device_id_type=MESH_IDS)
                  for k in range(N_CHIPS)]
        for cp in local + remote:
            cp.start()
        for cp in remote:
            cp.wait_recv()
        for cp in remote:
            cp.wait_send()
        for cp in local:
            cp.wait()

    shape = jax.ShapeDtypeStruct((N_CHIPS, R, C), g.dtype)
    return _pallas(
        body, name=name, in_specs=[ANY_SPEC], out_specs=[ANY_SPEC, ANY_SPEC], out_shape=[shape, shape],
        scratch_shapes=[pltpu.SemaphoreType.DMA((N_CHIPS,)), pltpu.SemaphoreType.DMA((N_CHIPS,)), pltpu.SemaphoreType.DMA((N_CHIPS,))],
        compiler_params=pltpu.CompilerParams(has_side_effects=True))(g)


def _pair_join(f, *, name):
    R, C = f.shape

    def body(f_ref, out_ref, send_sem, recv_sem, local_sem):
        x, y, c = lax.axis_index("x"), lax.axis_index("y"), lax.axis_index("c")
        local = pltpu.make_async_copy(f_ref, out_ref.at[c], local_sem)
        local.start()
        send = pltpu.make_async_remote_copy(src_ref=f_ref, dst_ref=out_ref.at[c], send_sem=send_sem, recv_sem=recv_sem,
                                            device_id=(x, y, 1 - c), device_id_type=MESH_IDS)
        send.start()
        pltpu.make_async_remote_copy(src_ref=f_ref, dst_ref=out_ref.at[1 - c], send_sem=send_sem, recv_sem=recv_sem,
                                     device_id=(x, y, 1 - c), device_id_type=MESH_IDS).wait_recv()
        send.wait_send()
        local.wait()

    return _pallas(
        body, name=name, in_specs=[ANY_SPEC], out_specs=ANY_SPEC, out_shape=jax.ShapeDtypeStruct((2, R, C), f.dtype),
        scratch_shapes=[pltpu.SemaphoreType.DMA, pltpu.SemaphoreType.DMA, pltpu.SemaphoreType.DMA],
        compiler_params=pltpu.CompilerParams(has_side_effects=True))(f)


def _gather8(s, *, name):
    R, C = s.shape

    def body(s_ref, out_ref, send_sems, recv_sems):
        x, y, c = lax.axis_index("x"), lax.axis_index("y"), lax.axis_index("c")
        me = 4 * x + 2 * y + c
        flips = [(dx, dy, dc) for dx in (0, 1) for dy in (0, 1) for dc in (0, 1)][1:]
        out_ref[me] = s_ref[...]

        def copy(j, slot):
            dx, dy, dc = flips[j]
            return pltpu.make_async_remote_copy(src_ref=s_ref, dst_ref=out_ref.at[slot], send_sem=send_sems.at[j],
                                                recv_sem=recv_sems.at[j], device_id=(x ^ dx, y ^ dy, c ^ dc), device_id_type=MESH_IDS)

        sends = [copy(j, me) for j in range(7)]
        for cp in sends:
            cp.start()
        for j, (dx, dy, dc) in enumerate(flips):
            copy(j, 4 * (x ^ dx) + 2 * (y ^ dy) + (c ^ dc)).wait_recv()
        for cp in sends:
            cp.wait_send()

    vmem = pl.BlockSpec(memory_space=pltpu.VMEM)
    return _pallas(
        body, name=name, in_specs=[vmem], out_specs=vmem, out_shape=jax.ShapeDtypeStruct((8, R, C), s.dtype),
        scratch_shapes=[pltpu.SemaphoreType.DMA((7,)), pltpu.SemaphoreType.DMA((7,))],
        compiler_params=pltpu.CompilerParams(has_side_effects=True))(s)


def _sum_slots(a, *, name):
    n, R, C = a.shape
    tr = _tile(R, 512, 8)

    def body(a_ref, o_ref):
        acc = a_ref[0]
        for k in range(1, n):
            acc = acc + a_ref[k]
        o_ref[...] = acc

    return _pallas(body, name=name, grid=(R // tr,), in_specs=[pl.BlockSpec((n, tr, C), lambda i: (0, i, 0))],
                   out_specs=pl.BlockSpec((tr, C), lambda i: (i, 0)), out_shape=jax.ShapeDtypeStruct((R, C), a.dtype),
                   compiler_params=_params("parallel"))(a)


def _add(a, b, *, name):
    R, C = a.shape[-2:]
    lead = a.shape[:-2]
    a2, b2 = a.reshape(-1, C), b.reshape(-1, C)
    tr = _tile(a2.shape[0], 1024, 8)

    def body(a_ref, b_ref, o_ref):
        o_ref[...] = a_ref[...] + b_ref[...]

    blk = pl.BlockSpec((tr, C), lambda i: (i, 0))
    out = _pallas(body, name=name, grid=(a2.shape[0] // tr,), in_specs=[blk, blk], out_specs=blk,
                  out_shape=jax.ShapeDtypeStruct(a2.shape, a.dtype), compiler_params=_params("parallel"))(a2, b2)
    return out.reshape(*lead, R, C)


def _adamw(w, g, m, v, *, name):
    R, C = w.shape
    tr = _tile(R, 512, 8)
    c1 = 1.0 / (1.0 - ADAM_B1 ** ADAM_STEP)
    c2 = 1.0 / (1.0 - ADAM_B2 ** ADAM_STEP)

    def body(w_ref, g_ref, m_ref, v_ref, d_ref, nm_ref, nv_ref):
        gv = g_ref[...]
        nm = ADAM_B1 * m_ref[...] + (1.0 - ADAM_B1) * gv
        nv = ADAM_B2 * v_ref[...] + (1.0 - ADAM_B2) * (gv * gv)
        d_ref[...] = -ADAM_LR * ((nm * c1) / (jnp.sqrt(nv * c2) + ADAM_EPS) + ADAM_WD * w_ref[...])
        nm_ref[...] = nm
        nv_ref[...] = nv

    blk = pl.BlockSpec((tr, C), lambda i: (i, 0))
    shape = jax.ShapeDtypeStruct((R, C), F32)
    return _pallas(body, name=name, grid=(R // tr,), in_specs=[blk] * 4, out_specs=[blk] * 3, out_shape=[shape] * 3,
                   compiler_params=_params("parallel"))(w, g, m, v)


def kernel(x, mem, w_in, g_mix, a_gq, a_gkv, a_wuq, a_wukv, b_lb, b_gout, c_sink, rel_bias, w_br_a, w_br_b, w_br_c, w_out, g_x, g_mem, x_wq, x_wkv, x_wo, g_ffn, f_w1, f_w3, f_w2, g_final, loss_target, m_w_in, m_g_mix, m_a_gq, m_a_gkv, m_a_wuq, m_a_wukv, m_b_lb, m_b_gout, m_c_sink, m_rel_bias, m_w_br_a, m_w_br_b, m_w_br_c, m_w_out, m_g_x, m_g_mem, m_x_wq, m_x_wkv, m_x_wo, m_g_ffn, m_f_w1, m_f_w3, m_f_w2, m_g_final, v_w_in, v_g_mix, v_a_gq, v_a_gkv, v_a_wuq, v_a_wukv, v_b_lb, v_b_gout, v_c_sink, v_rel_bias, v_w_br_a, v_w_br_b, v_w_br_c, v_w_out, v_g_x, v_g_mem, v_x_wq, v_x_wkv, v_x_wo, v_g_ffn, v_f_w1, v_f_w3, v_f_w2, v_g_final):
    ws = dict(zip(WEIGHT_NAMES, (w_in, g_mix, a_gq, a_gkv, a_wuq, a_wukv, b_lb, b_gout, c_sink, rel_bias, w_br_a, w_br_b, w_br_c,
                                 w_out, g_x, g_mem, x_wq, x_wkv, x_wo, g_ffn, f_w1, f_w3, f_w2, g_final)))
    ms = dict(zip(WEIGHT_NAMES, (m_w_in, m_g_mix, m_a_gq, m_a_gkv, m_a_wuq, m_a_wukv, m_b_lb, m_b_gout, m_c_sink, m_rel_bias,
                                 m_w_br_a, m_w_br_b, m_w_br_c, m_w_out, m_g_x, m_g_mem, m_x_wq, m_x_wkv, m_x_wo, m_g_ffn,
                                 m_f_w1, m_f_w3, m_f_w2, m_g_final)))
    vs = dict(zip(WEIGHT_NAMES, (v_w_in, v_g_mix, v_a_gq, v_a_gkv, v_a_wuq, v_a_wukv, v_b_lb, v_b_gout, v_c_sink, v_rel_bias,
                                 v_w_br_a, v_w_br_b, v_w_br_c, v_w_out, v_g_x, v_g_mem, v_x_wq, v_x_wkv, v_x_wo, v_g_ffn,
                                 v_f_w1, v_f_w3, v_f_w2, v_g_final)))
    sharded = [n for n, _ in SHARDED]
    axis_of = dict(SHARDED)

    def wire(n):
        return lax.bitcast_convert_type(ws[n], BF16) if n == "b_lb" else ws[n].astype(BF16)

    wire_shapes = [wire(n).shape for n in sharded]
    gathered = _chip_exchange(_pack([wire(n) for n in sharded], PACK_COLS, PACK_ALIGN), segmented=False, name="gather_weights")
    per_chip = [_unpack(gathered[k], wire_shapes) for k in range(N_CHIPS)]
    full = dict(ws)
    for i, n in enumerate(sharded):
        parts = [per_chip[k][i] for k in range(N_CHIPS)]
        if n == "b_lb":
            parts = [lax.bitcast_convert_type(p, F32) for p in parts]
        full[n] = jnp.concatenate(parts, axis=axis_of[n])

    loss, grad_x, grads = _local_step(x[0], mem[0], loss_target[0], full)

    segs = []
    for k in range(N_CHIPS):
        pieces = [jnp.split(grads[n], N_CHIPS, axis=axis_of[n])[k] for n in sharded]
        segs.append(_pack(pieces, PACK_COLS, PACK_ALIGN))
    rows = segs[0].shape[0]
    g4 = jnp.stack(segs).reshape(N_CHIPS, 2, rows // 2, PACK_COLS)
    kept, got = _pair_split(g4, name="reduce_pair_split")
    pair_sum = _add(kept, got, name="reduce_pair_add")
    landed = _chip_exchange(pair_sum, segmented=True, name="reduce_chip_scatter")
    mine = _sum_slots(landed, name="reduce_chip_sum")
    g_shard = _pair_join(mine, name="reduce_pair_join").reshape(rows, PACK_COLS)

    small = _pack([grads[n] for n in REPLICATED] + [loss[0, 0:1]], LANES, 8 * LANES)
    small_sum = _sum_slots(_gather8(small, name="gather_small"), name="sum_small")
    small_grads = _unpack(small_sum, [ws[n].shape for n in REPLICATED] + [(1,)])
    loss_total = small_grads.pop()[0]

    shard_shapes = [ws[n].shape for n in sharded]
    pk = lambda d: _pack([d[n] for n in sharded], PACK_COLS, PACK_ALIGN)
    d_flat, m_flat, v_flat = _adamw(pk(ws), g_shard, pk(ms), pk(vs), name="adamw_sharded")
    out = {}
    for i, (n, gr, d, nm, nv) in enumerate(zip(sharded, _unpack(g_shard, shard_shapes), _unpack(d_flat, shard_shapes),
                                               _unpack(m_flat, shard_shapes), _unpack(v_flat, shard_shapes))):
        out[n] = (gr, d, nm, nv)
    pk_s = lambda d: _pack([d[n] for n in REPLICATED], LANES, 8 * LANES)
    rep_shapes = [ws[n].shape for n in REPLICATED]
    gs_flat = _pack(small_grads, LANES, 8 * LANES)
    ds_flat, ms_flat, vs_flat = _adamw(pk_s(ws), gs_flat, pk_s(ms), pk_s(vs), name="adamw_replicated")
    for n, gr, d, nm, nv in zip(REPLICATED, small_grads, _unpack(ds_flat, rep_shapes), _unpack(ms_flat, rep_shapes),
                                _unpack(vs_flat, rep_shapes)):
        out[n] = (gr, d, nm, nv)
    return (loss_total, grad_x[None], *[out[n][0] for n in WEIGHT_NAMES], *[out[n][1] for n in WEIGHT_NAMES],
            *[out[n][2] for n in WEIGHT_NAMES], *[out[n][3] for n in WEIGHT_NAMES])
```

```python
import functools
import math

import jax
import jax.numpy as jnp
from jax import lax
from jax.experimental import pallas as pl
from jax.experimental.pallas import tpu as pltpu

F32 = jnp.float32
BF16 = jnp.bfloat16

D_MODEL = 1024
DEPTH = 2
EPS = 1e-6
MASK_VALUE = -1e30
TINY = 1e-30
A_HEADS, A_NOPE, A_ROPE, A_V = 8, 64, 32, 64
A_QK = A_NOPE + A_ROPE
A_Q_RANK, A_KV_RANK = 384, 256
ROPE_THETA = 10000.0
B_HEADS, B_DK, B_DV, B_CHUNK = 8, 128, 64, 16
C_HEADS, C_KV_HEADS, C_DH, C_WINDOW, C_BLOCK = 8, 2, 64, 128, 128
REL_BUCKETS, REL_MAX_DIST = 32, 128
X_HEADS, X_DH = 4, 256
D_FF = 2816
ADAM_LR, ADAM_B1, ADAM_B2, ADAM_EPS, ADAM_WD, ADAM_STEP = 0.001, 0.9, 0.999, 1e-08, 0.01, 10

LANES = 128
VMEM_LIMIT = 56 * 1024 * 1024


def _pallas(body, **kw):
    return pl.pallas_call(body, **kw)


def _params(*sem):
    return pltpu.CompilerParams(dimension_semantics=sem, vmem_limit_bytes=VMEM_LIMIT)


def _tile(n, pref, unit=LANES):
    if n <= pref:
        return n
    t = (pref // unit) * unit
    while t > unit and n % t:
        t -= unit
    assert n % t == 0, (n, pref, unit)
    return t


def _dot(a, b, dims):
    return lax.dot_general(a, b, (dims, ((), ())), preferred_element_type=F32)


_NN = ((1,), (0,))
_NT = ((1,), (1,))
_TN = ((0,), (0,))


def _mm(a, b, *, mode="nn", add=None, out_dtype=F32, name):
    if mode == "nn":
        (M, K), (K2, N) = a.shape, b.shape
    elif mode == "nt":
        (M, K), (N, K2) = a.shape, b.shape
    else:
        (K, M), (K2, N) = a.shape, b.shape
    assert K == K2, (a.shape, b.shape, mode)
    tm = _tile(M, 512, LANES if mode == "tn" else 8)
    tn = _tile(N, 1024, 256) if N % 256 == 0 else _tile(N, 1024)
    tk = _tile(K, 1024)
    nk = K // tk
    dims = {"nn": _NN, "nt": _NT, "tn": _TN}[mode]
    a_spec = pl.BlockSpec((tk, tm), lambda i, j, k: (k, i)) if mode == "tn" else pl.BlockSpec((tm, tk), lambda i, j, k: (i, k))
    b_spec = pl.BlockSpec((tn, tk), lambda i, j, k: (j, k)) if mode == "nt" else pl.BlockSpec((tk, tn), lambda i, j, k: (k, j))
    o_spec = pl.BlockSpec((tm, tn), lambda i, j, k: (i, j))
    has_add = add is not None

    def body(*refs):
        if has_add:
            a_ref, b_ref, add_ref, o_ref, acc_ref = refs
        else:
            a_ref, b_ref, o_ref, acc_ref = refs
        k = pl.program_id(2)
        part = _dot(a_ref[...].astype(BF16), b_ref[...].astype(BF16), dims)

        @pl.when(k == 0)
        def _():
            acc_ref[...] = part

        @pl.when(k > 0)
        def _():
            acc_ref[...] += part

        @pl.when(k == nk - 1)
        def _():
            r = acc_ref[...]
            if has_add:
                r = r + add_ref[...].astype(F32)
            o_ref[...] = r.astype(out_dtype)

    ins = [a, b] + ([add] if has_add else [])
    in_specs = [a_spec, b_spec] + ([o_spec] if has_add else [])
    return _pallas(
        body, name=name, grid=(M // tm, N // tn, nk), in_specs=in_specs, out_specs=o_spec,
        out_shape=jax.ShapeDtypeStruct((M, N), out_dtype), scratch_shapes=[pltpu.VMEM((tm, tn), F32)],
        compiler_params=_params("parallel", "parallel", "arbitrary"),
    )(*ins)


def _rms(x, g, *, col=0, width=None, out_dtype=BF16, name):
    T = x.shape[0]
    width = x.shape[1] if width is None else width
    assert col % width == 0
    tm = _tile(T, 512, 8)
    cb = col // width

    def body(x_ref, g_ref, o_ref):
        xv = x_ref[...].astype(F32)
        r = lax.rsqrt(jnp.mean(xv * xv, axis=-1, keepdims=True) + EPS)
        o_ref[...] = (xv * r * g_ref[...]).astype(out_dtype)

    return _pallas(
        body, name=name, grid=(T // tm,),
        in_specs=[pl.BlockSpec((tm, width), lambda i: (i, cb)), pl.BlockSpec((1, width), lambda i: (0, 0))],
        out_specs=pl.BlockSpec((tm, width), lambda i: (i, 0)),
        out_shape=jax.ShapeDtypeStruct((T, width), out_dtype), compiler_params=_params("parallel"),
    )(x, g.reshape(1, width))


def _rms_bwd(x, g, dy, *, res=None, col=0, width=None, out_dtype=F32, name):
    T = x.shape[0]
    width = x.shape[1] if width is None else width
    assert col % width == 0
    tm = _tile(T, 512, 8)
    cb = col // width
    has_res = res is not None

    def body(*refs):
        if has_res:
            x_ref, g_ref, dy_ref, res_ref, dx_ref, dg_ref = refs
        else:
            x_ref, g_ref, dy_ref, dx_ref, dg_ref = refs
        xv = x_ref[...].astype(F32)
        r = lax.rsqrt(jnp.mean(xv * xv, axis=-1, keepdims=True) + EPS)
        xh = xv * r
        dyv = dy_ref[...].astype(F32)
        dxh = dyv * g_ref[...]
        dx = r * (dxh - xh * jnp.mean(dxh * xh, axis=-1, keepdims=True))
        if has_res:
            dx = dx + res_ref[...].astype(F32)
        dx_ref[...] = dx.astype(out_dtype)
        part = jnp.sum(dyv * xh, axis=0, keepdims=True)

        @pl.when(pl.program_id(0) == 0)
        def _():
            dg_ref[...] = part

        @pl.when(pl.program_id(0) > 0)
        def _():
            dg_ref[...] += part

    row = pl.BlockSpec((tm, width), lambda i: (i, 0))
    ins = [x, g.reshape(1, width), dy] + ([res] if has_res else [])
    in_specs = [pl.BlockSpec((tm, width), lambda i: (i, cb)), pl.BlockSpec((1, width), lambda i: (0, 0)), row] + ([row] if has_res else [])
    return _pallas(
        body, name=name, grid=(T // tm,), in_specs=in_specs,
        out_specs=[row, pl.BlockSpec((1, width), lambda i: (0, 0))],
        out_shape=[jax.ShapeDtypeStruct((T, width), out_dtype), jax.ShapeDtypeStruct((1, width), F32)],
        compiler_params=_params("arbitrary"),
    )(*ins)


def _rope_tables(T):
    half = A_ROPE // 2
    inv = ROPE_THETA ** (-jnp.arange(half, dtype=F32) / half)
    ang = jnp.arange(T, dtype=jnp.int32).astype(F32)[:, None] * inv[None, :]
    c32 = jnp.concatenate([jnp.cos(ang), jnp.cos(ang)], axis=-1)
    s32 = jnp.concatenate([jnp.sin(ang), jnp.sin(ang)], axis=-1)
    pad = A_PAD - A_QK
    cq = jnp.tile(jnp.concatenate([jnp.ones((T, A_NOPE), F32), c32, jnp.ones((T, pad), F32)], axis=-1), (1, A_HEADS))
    sq = jnp.tile(jnp.concatenate([jnp.zeros((T, A_NOPE), F32), s32, jnp.zeros((T, pad), F32)], axis=-1), (1, A_HEADS))
    ck = jnp.concatenate([c32, s32, jnp.zeros((T, LANES - 2 * A_ROPE), F32)], axis=-1)
    ck_t = jnp.concatenate([c32, s32], axis=-1).T
    return cq, sq, ck, ck_t


def _rope_swap_cols(w):
    half = A_ROPE // 2
    return jnp.concatenate([-w[..., half:], w[..., :half]], axis=-1)


def _rope_unswap_cols(g):
    half = A_ROPE // 2
    return jnp.concatenate([g[..., half:], -g[..., :half]], axis=-1)


A_PAD = LANES
A_W = A_HEADS * A_PAD
LOG2E = 1.4426950408889634
LN2 = 0.6931471805599453
Q_SCALE = A_QK ** -0.5 * LOG2E


def _qrope(q2, cq, sq, *, name):
    T = q2.shape[0]
    W = A_W
    tm = _tile(T, 512, 8)

    def body(a_ref, b_ref, c_ref, s_ref, o_ref):
        o_ref[...] = ((a_ref[...] * c_ref[...] + b_ref[...] * s_ref[...]) * Q_SCALE).astype(BF16)

    blk = lambda j: pl.BlockSpec((tm, W), lambda i: (i, j))
    return _pallas(body, name=name, grid=(T // tm,), in_specs=[blk(0), blk(1), blk(0), blk(0)], out_specs=blk(0),
                   out_shape=jax.ShapeDtypeStruct((T, W), BF16), compiler_params=_params("parallel"))(q2, q2, cq, sq)


def _qrope_bwd(dq, cq, sq, *, name):
    T = dq.shape[0]
    W = A_W
    tm = _tile(T, 512, 8)

    def body(d_ref, c_ref, s_ref, o_ref):
        d = d_ref[...]
        o_ref[:, 0:W] = (d * c_ref[...]).astype(BF16)
        o_ref[:, W:2 * W] = (d * s_ref[...]).astype(BF16)

    blk = pl.BlockSpec((tm, W), lambda i: (i, 0))
    return _pallas(body, name=name, grid=(T // tm,), in_specs=[blk, blk, blk],
                   out_specs=pl.BlockSpec((tm, 2 * W), lambda i: (i, 0)),
                   out_shape=jax.ShapeDtypeStruct((T, 2 * W), BF16), compiler_params=_params("parallel"))(dq, cq, sq)


def _kprep(kv, za, ck, *, name):
    T = kv.shape[0]
    tm = _tile(T, 512, 8)

    def body(kv_ref, kr_ref, ck_ref, k_ref, vx_ref):
        t = kr_ref[...] * ck_ref[...]
        krope = (t[:, 0:A_ROPE] + t[:, A_ROPE:2 * A_ROPE]).astype(BF16)
        one = (lax.broadcasted_iota(jnp.int32, (tm, A_PAD - A_V), 1) == 0).astype(BF16)
        for h in range(A_HEADS):
            k_ref[:, A_PAD * h:A_PAD * h + A_NOPE] = kv_ref[:, A_NOPE * h:A_NOPE * (h + 1)]
            k_ref[:, A_PAD * h + A_NOPE:A_PAD * h + A_QK] = krope
            k_ref[:, A_PAD * h + A_QK:A_PAD * (h + 1)] = jnp.zeros((tm, A_PAD - A_QK), BF16)
            vx_ref[:, A_PAD * h:A_PAD * h + A_V] = kv_ref[:, 512 + A_V * h:512 + A_V * (h + 1)]
            vx_ref[:, A_PAD * h + A_V:A_PAD * (h + 1)] = one

    wide = pl.BlockSpec((tm, A_W), lambda i: (i, 0))
    return _pallas(
        body, name=name, grid=(T // tm,),
        in_specs=[wide, pl.BlockSpec((tm, LANES), lambda i: (i, 3)), pl.BlockSpec((tm, LANES), lambda i: (i, 0))],
        out_specs=[wide, wide], out_shape=[jax.ShapeDtypeStruct((T, A_W), BF16)] * 2,
        compiler_params=_params("parallel"))(kv, za, ck)


def _kprep_bwd(dkt, ck_t, *, name):
    T = dkt.shape[1]
    tc = _tile(T, 512)

    def body(dk_ref, ck_ref, dn_ref, dr_ref):
        acc = jnp.zeros((A_ROPE, tc), F32)
        for h in range(A_HEADS):
            dn_ref[A_NOPE * h:A_NOPE * (h + 1), :] = dk_ref[A_PAD * h:A_PAD * h + A_NOPE, :].astype(BF16)
            acc = acc + dk_ref[A_PAD * h + A_NOPE:A_PAD * h + A_QK, :]
        dr_ref[0:A_ROPE, :] = (acc * ck_ref[0:A_ROPE, :]).astype(BF16)
        dr_ref[A_ROPE:2 * A_ROPE, :] = (acc * ck_ref[A_ROPE:2 * A_ROPE, :]).astype(BF16)
        dr_ref[2 * A_ROPE:LANES, :] = jnp.zeros((LANES - 2 * A_ROPE, tc), BF16)

    col = lambda r: pl.BlockSpec((r, tc), lambda i: (0, i))
    return _pallas(
        body, name=name, grid=(T // tc,), in_specs=[col(A_W), col(2 * A_ROPE)], out_specs=[col(512), col(LANES)],
        out_shape=[jax.ShapeDtypeStruct((512, T), BF16), jax.ShapeDtypeStruct((LANES, T), BF16)],
        compiler_params=_params("parallel"))(dkt, ck_t)


def _flash_fwd(qs, k, vxt, *, name):
    T = qs.shape[0]
    tq, tk = _tile(T, 512), _tile(T, 2048)
    nk = T // tk
    H, P, DV = A_HEADS, A_PAD, A_V

    def body(q_ref, k_ref, v_ref, o_ref, lse_ref, m_sc, acc_sc):
        j = pl.program_id(1)

        @pl.when(j == 0)
        def _():
            m_sc[...] = jnp.full(m_sc.shape, -jnp.inf, F32)
            acc_sc[...] = jnp.zeros(acc_sc.shape, F32)

        def scores(h):
            return _dot(k_ref[:, P * h:P * (h + 1)], q_ref[:, P * h:P * (h + 1)], _NT)

        st_next = scores(0)
        for h in range(H):
            st = st_next
            if h + 1 < H:
                st_next = scores(h + 1)
            m_prev = m_sc[h]
            m_new = jnp.maximum(m_prev, jnp.max(st, axis=0, keepdims=True))
            pt = jnp.exp2(st - m_new).astype(BF16)
            acc_sc[h] = jnp.exp2(m_prev - m_new) * acc_sc[h] + _dot(v_ref[P * h:P * (h + 1), :], pt, _NN)
            m_sc[h] = m_new

        @pl.when(j == nk - 1)
        def _():
            for h in range(H):
                acc = acc_sc[h]
                l = acc[DV:DV + 1, :]
                o_ref[:, DV * h:DV * (h + 1)] = (acc[0:DV, :] / l).T
                lse_ref[h] = m_sc[h] + jnp.log2(l)

    return _pallas(
        body, name=name, grid=(T // tq, nk),
        in_specs=[pl.BlockSpec((tq, A_W), lambda i, j: (i, 0)), pl.BlockSpec((tk, A_W), lambda i, j: (j, 0)),
                  pl.BlockSpec((A_W, tk), lambda i, j: (0, j))],
        out_specs=[pl.BlockSpec((tq, H * DV), lambda i, j: (i, 0)), pl.BlockSpec((H, 1, tq), lambda i, j: (0, 0, i))],
        out_shape=[jax.ShapeDtypeStruct((T, H * DV), F32), jax.ShapeDtypeStruct((H, 1, T), F32)],
        scratch_shapes=[pltpu.VMEM((H, 1, tq), F32), pltpu.VMEM((H, P, tq), F32)],
        compiler_params=_params("parallel", "arbitrary"))(qs, k, vxt)


def _attn_delta(o, do, *, name):
    T = o.shape[0]
    tm = _tile(T, 512, 8)

    def body(o_ref, do_ref, d_ref):
        prod = o_ref[...] * do_ref[...].astype(F32)
        for h in range(A_HEADS):
            d_ref[h] = jnp.sum(prod[:, A_V * h:A_V * (h + 1)], axis=-1, keepdims=True)

    row = pl.BlockSpec((tm, A_HEADS * A_V), lambda i: (i, 0))
    return _pallas(body, name=name, grid=(T // tm,), in_specs=[row, row],
                   out_specs=pl.BlockSpec((A_HEADS, tm, 1), lambda i: (0, i, 0)),
                   out_shape=jax.ShapeDtypeStruct((A_HEADS, T, 1), F32), compiler_params=_params("parallel"))(o, do)


def _flash_bwd(qs, qst, k, kv, do, dot_, lse2, delta, *, name):
    T = qs.shape[0]
    tq, tk = _tile(T, 512), _tile(T, 1024)
    nq, nk = T // tq, T // tk
    H, P, DV = A_HEADS, A_PAD, A_V

    def body(q_ref, qt_ref, k_ref, v_ref, do_ref, dot_ref, lse_ref, delta_ref, dq_ref, dkt_ref, dvt_ref, dkt_sc, dvt_sc):
        i = pl.program_id(1)

        @pl.when(i == 0)
        def _():
            dkt_sc[...] = jnp.zeros(dkt_sc.shape, F32)
            dvt_sc[...] = jnp.zeros(dvt_sc.shape, F32)

        def products(h):
            s = _dot(q_ref[:, P * h:P * (h + 1)], k_ref[:, P * h:P * (h + 1)], _NT)
            dp = _dot(do_ref[:, DV * h:DV * (h + 1)], v_ref[:, DV * h:DV * (h + 1)], _NT)
            return s, dp

        nxt = products(0)
        for h in range(H):
            s, dp = nxt
            if h + 1 < H:
                nxt = products(h + 1)
            p = jnp.exp2(s - lse_ref[h])
            ds = (p * (dp - delta_ref[h])).astype(BF16)
            pb = p.astype(BF16)
            dq_ref[0, :, P * h:P * (h + 1)] = _dot(ds, k_ref[:, P * h:P * (h + 1)], _NN)
            dkt_sc[h] += _dot(qt_ref[P * h:P * (h + 1), :], ds, _NN)
            dvt_sc[h] += _dot(dot_ref[DV * h:DV * (h + 1), :], pb, _NN)

        @pl.when(i == nq - 1)
        def _():
            for h in range(H):
                dkt_ref[P * h:P * (h + 1), :] = dkt_sc[h] * LN2
                dvt_ref[DV * h:DV * (h + 1), :] = dvt_sc[h].astype(BF16)

    qrow = lambda w: pl.BlockSpec((tq, w), lambda j, i: (i, 0))
    qcol = lambda r: pl.BlockSpec((r, tq), lambda j, i: (0, i))
    stat = pl.BlockSpec((H, tq, 1), lambda j, i: (0, i, 0))
    return _pallas(
        body, name=name, grid=(nk, nq),
        in_specs=[qrow(A_W), qcol(A_W), pl.BlockSpec((tk, A_W), lambda j, i: (j, 0)), pl.BlockSpec((tk, H * DV), lambda j, i: (j, 1)),
                  qrow(H * DV), qcol(H * DV), stat, stat],
        out_specs=[pl.BlockSpec((1, tq, A_W), lambda j, i: (j, i, 0)), pl.BlockSpec((A_W, tk), lambda j, i: (0, j)),
                   pl.BlockSpec((H * DV, tk), lambda j, i: (0, j))],
        out_shape=[jax.ShapeDtypeStruct((nk, T, A_W), F32), jax.ShapeDtypeStruct((A_W, T), F32),
                   jax.ShapeDtypeStruct((H * DV, T), BF16)],
        scratch_shapes=[pltpu.VMEM((H, P, tk), F32), pltpu.VMEM((H, DV, tk), F32)],
        compiler_params=_params("parallel", "arbitrary"))(qs, qst, k, kv, do, dot_, lse2, delta)


def _dq_sum(dq_part, *, name):
    n, T, W = dq_part.shape
    tm = _tile(T, 128, 8)

    def body(p_ref, o_ref):
        acc = p_ref[0]
        for j in range(1, n):
            acc = acc + p_ref[j]
        o_ref[...] = acc * (A_QK ** -0.5)

    return _pallas(body, name=name, grid=(T // tm,), in_specs=[pl.BlockSpec((n, tm, W), lambda i: (0, i, 0))],
                   out_specs=pl.BlockSpec((tm, W), lambda i: (i, 0)), out_shape=jax.ShapeDtypeStruct((T, W), F32),
                   compiler_params=_params("parallel"))(dq_part)


HB = 8 * B_CHUNK


def _chunk_masks(reverse):
    r = lax.broadcasted_iota(jnp.int32, (HB, HB), 0)
    c = lax.broadcasted_iota(jnp.int32, (HB, HB), 1)
    same = (r // B_CHUNK) == (c // B_CHUNK)
    incl = same & ((c >= r) if reverse else (c <= r))
    return same, incl


def _split3(x):
    hi = x.astype(BF16)
    r1 = x - hi.astype(F32)
    mid = r1.astype(BF16)
    lo = (r1 - mid.astype(F32)).astype(BF16)
    return hi, mid, lo


def _mask_mm(mask, x):
    hi, mid, lo = _split3(x)
    return _dot(mask, hi, _NN) + _dot(mask, mid, _NN) + _dot(mask, lo, _NN)


def _hgrn_gates(q, z, lb, reverse):
    same, incl = _chunk_masks(reverse)
    sg = jax.nn.sigmoid(z)
    f = lb + (1.0 - lb) * sg
    lf = jnp.log(jnp.maximum(f, TINY))
    kk = (1.0 - lb) * jax.nn.sigmoid(-z)
    b = _mask_mm(incl.astype(BF16), lf)
    btot = _mask_mm(same.astype(BF16), lf)
    eb, enb, er, dec = jnp.exp(b), jnp.exp(-b), jnp.exp(btot - b), jnp.exp(btot)
    return dict(same=same, incl=incl, sg=sg, f=f, kk=kk, eb=eb, enb=enb, er=er, dec=dec,
                qd=q * eb, ki=kk * enb, ke=kk * er)


def _hgrn_specs(T, reverse, gate_reverse):
    nb = T // HB
    blk = (lambda i: nb - 1 - i) if reverse else (lambda i: i)
    wide = B_HEADS * B_DK
    return nb, blk, [
        pl.BlockSpec((HB, wide), lambda i: (blk(i), 0)),
        pl.BlockSpec((HB, wide), lambda i: (blk(i), 2 if gate_reverse else 1)),
        pl.BlockSpec((HB, B_HEADS * B_DV), lambda i: (blk(i), 6)),
        pl.BlockSpec((1, wide), lambda i: (0, 0)),
    ]


def _hk(h):
    return slice(B_DK * h, B_DK * (h + 1))


def _hv(h):
    return slice(B_DV * h, B_DV * (h + 1))


def _crows(c):
    return slice(B_CHUNK * c, B_CHUNK * (c + 1))


def _hgrn_fwd(zb, lb, *, reverse, name):
    T = zb.shape[0]
    nb, blk, in_specs = _hgrn_specs(T, reverse, reverse)
    order = range(7, -1, -1) if reverse else range(8)
    heads = range(B_HEADS)

    def body(q_ref, z_ref, v_ref, lb_ref, o_ref, st_ref, s_sc):
        @pl.when(pl.program_id(0) == 0)
        def _():
            s_sc[...] = jnp.zeros(s_sc.shape, F32)

        g = _hgrn_gates(q_ref[...], z_ref[...], lb_ref[...], reverse)
        v = v_ref[...].astype(BF16)
        qd, ki, ke = g["qd"].astype(BF16), g["ki"].astype(BF16), g["ke"].astype(BF16)
        dec = g["dec"]
        o_intra = []
        for h in heads:
            a = jnp.where(g["incl"], _dot(qd[:, _hk(h)], ki[:, _hk(h)], _NT), 0.0)
            o_intra.append(_dot(a.astype(BF16), v[:, _hv(h)], _NN))
        upd = [[_dot(v[_crows(c), _hv(h)], ke[_crows(c), _hk(h)], _TN) for c in range(8)] for h in heads]
        st = [s_sc[h] for h in heads]
        snap = [[None] * 8 for _ in heads]
        for c in order:
            for h in heads:
                snap[h][c] = st[h]
                st[h] = st[h] * dec[B_CHUNK * c:B_CHUNK * c + 1, _hk(h)] + upd[h][c]
        for h in heads:
            s_sc[h] = st[h]
            for c in range(8):
                st_ref[h, c] = snap[h][c]
            inter = [_dot(qd[_crows(c), _hk(h)], snap[h][c].astype(BF16), _NT) for c in range(8)]
            o_ref[h] = o_intra[h] + jnp.concatenate(inter, axis=0)

    return _pallas(
        body, name=name, grid=(nb,), in_specs=in_specs,
        out_specs=[pl.BlockSpec((B_HEADS, HB, B_DV), lambda i: (0, blk(i), 0)),
                   pl.BlockSpec((B_HEADS, 8, B_DV, B_DK), lambda i: (0, blk(i), 0, 0))],
        out_shape=[jax.ShapeDtypeStruct((B_HEADS, T, B_DV), F32),
                   jax.ShapeDtypeStruct((B_HEADS, T // B_CHUNK, B_DV, B_DK), F32)],
        scratch_shapes=[pltpu.VMEM((B_HEADS, B_DV, B_DK), F32)],
        compiler_params=_params("arbitrary"))(zb, zb, zb, lb)


def _hgrn_bwd(zb, lb, do, states, *, reverse, name):
    T = zb.shape[0]
    nb, blk, in_specs = _hgrn_specs(T, not reverse, reverse)
    order = range(8) if reverse else range(7, -1, -1)
    heads = range(B_HEADS)

    def body(q_ref, z_ref, v_ref, lb_ref, do_ref, st_ref, dq_ref, dz_ref, dv_ref, dlb_ref, ds_sc):
        @pl.when(pl.program_id(0) == 0)
        def _():
            ds_sc[...] = jnp.zeros(ds_sc.shape, F32)
            dlb_ref[...] = jnp.zeros(dlb_ref.shape, F32)

        lb = lb_ref[...]
        g = _hgrn_gates(q_ref[...], z_ref[...], lb, reverse)
        v = v_ref[...].astype(BF16)
        qd, ki, ke = g["qd"].astype(BF16), g["ki"].astype(BF16), g["ke"].astype(BF16)
        dec = g["dec"]
        dout = [do_ref[h].astype(BF16) for h in heads]
        dv_i, dqd_i, dki = [], [], []
        for h in heads:
            a = jnp.where(g["incl"], _dot(qd[:, _hk(h)], ki[:, _hk(h)], _NT), 0.0).astype(BF16)
            da = jnp.where(g["incl"], _dot(dout[h], v[:, _hv(h)], _NT), 0.0).astype(BF16)
            dv_i.append(_dot(a, dout[h], _TN))
            dqd_i.append(_dot(da, ki[:, _hk(h)], _NN))
            dki.append(_dot(da, qd[:, _hk(h)], _TN))
        upd = [[_dot(dout[h][_crows(c)], qd[_crows(c), _hk(h)], _TN) for c in range(8)] for h in heads]
        dqd_p = [[_dot(dout[h][_crows(c)], st_ref[h, c].astype(BF16), _NN) for c in range(8)] for h in heads]
        dst = [ds_sc[h] for h in heads]
        used = [[None] * 8 for _ in heads]
        for c in order:
            for h in heads:
                used[h][c] = dst[h]
                dst[h] = dst[h] * dec[B_CHUNK * c:B_CHUNK * c + 1, _hk(h)] + upd[h][c]
        dqd_h, dke_h, ddec_h = [], [], []
        for h in heads:
            ds_sc[h] = dst[h]
            dv_p, dke_p, ddec_p = [], [], []
            for c in range(8):
                d = used[h][c]
                db16 = d.astype(BF16)
                dv_p.append(_dot(ke[_crows(c), _hk(h)], db16, _NT))
                dke_p.append(_dot(v[_crows(c), _hv(h)], db16, _NN))
                tot = jnp.sum(d * st_ref[h, c], axis=0, keepdims=True) * dec[B_CHUNK * c:B_CHUNK * c + 1, _hk(h)]
                ddec_p.append(jnp.broadcast_to(tot, (B_CHUNK, B_DK)))
            dv_ref[h] = dv_i[h] + jnp.concatenate(dv_p, axis=0)
            dqd_h.append(dqd_i[h] + jnp.concatenate(dqd_p[h], axis=0))
            dke_h.append(jnp.concatenate(dke_p, axis=0))
            ddec_h.append(jnp.concatenate(ddec_p, axis=0))
        dqd = jnp.concatenate(dqd_h, axis=1)
        dki = jnp.concatenate(dki, axis=1)
        dke = jnp.concatenate(dke_h, axis=1)
        db = dqd * g["qd"] - dki * g["ki"] - dke * g["ke"]
        _, incl_t = _chunk_masks(not reverse)
        dlf = (_mask_mm(incl_t.astype(BF16), db) + _mask_mm(g["same"].astype(BF16), dke * g["ke"])
               + jnp.concatenate(ddec_h, axis=1))
        dk = dki * g["enb"] + dke * g["er"]
        u = jnp.where(g["f"] > TINY, dlf / g["f"], 0.0) - dk
        sg = g["sg"]
        dq_ref[...] = dqd * g["eb"]
        dz_ref[...] = u * (1.0 - lb) * sg * (1.0 - sg)
        dlb_ref[...] += jnp.sum(u * (1.0 - sg), axis=0, keepdims=True)

    wide = pl.BlockSpec((HB, B_HEADS * B_DK), lambda i: (blk(i), 0))
    hm = pl.BlockSpec((B_HEADS, HB, B_DV), lambda i: (0, blk(i), 0))
    return _pallas(
        body, name=name, grid=(nb,),
        in_specs=in_specs + [hm, pl.BlockSpec((B_HEADS, 8, B_DV, B_DK), lambda i: (0, blk(i), 0, 0))],
        out_specs=[wide, wide, hm, pl.BlockSpec((1, B_HEADS * B_DK), lambda i: (0, 0))],
        out_shape=[jax.ShapeDtypeStruct((T, B_HEADS * B_DK), F32), jax.ShapeDtypeStruct((T, B_HEADS * B_DK), F32),
                   jax.ShapeDtypeStruct((B_HEADS, T, B_DV), F32), jax.ShapeDtypeStruct((1, B_HEADS * B_DK), F32)],
        scratch_shapes=[pltpu.VMEM((B_HEADS, B_DV, B_DK), F32)],
        compiler_params=_params("arbitrary"))(zb, zb, zb, lb, do, states)


def _hgrn_out(of, ob, zb, gout, *, name):
    T = zb.shape[0]
    tm = _tile(T, 512, 8)

    def body(of_ref, ob_ref, g_ref, gout_ref, y_ref):
        for h in range(B_HEADS):
            o = of_ref[h] + ob_ref[h]
            r = lax.rsqrt(jnp.mean(o * o, axis=-1, keepdims=True) + EPS)
            gh = g_ref[:, B_DV * h:B_DV * (h + 1)]
            y_ref[:, B_DV * h:B_DV * (h + 1)] = (o * r * gout_ref[...] * (gh * jax.nn.sigmoid(gh))).astype(BF16)

    hm = pl.BlockSpec((B_HEADS, tm, B_DV), lambda i: (0, i, 0))
    return _pallas(
        body, name=name, grid=(T // tm,),
        in_specs=[hm, hm, pl.BlockSpec((tm, 512), lambda i: (i, 7)), pl.BlockSpec((1, B_DV), lambda i: (0, 0))],
        out_specs=pl.BlockSpec((tm, 512), lambda i: (i, 0)),
        out_shape=jax.ShapeDtypeStruct((T, 512), BF16), compiler_params=_params("parallel"))(of, ob, zb, gout.reshape(1, B_DV))


def _hgrn_out_bwd(of, ob, zb, gout, dy, *, name):
    T = zb.shape[0]
    tm = _tile(T, 512, 8)

    def body(of_ref, ob_ref, g_ref, gout_ref, dy_ref, do_ref, dg_ref, dgo_ref):
        gout_v = gout_ref[...]
        acc = jnp.zeros((1, B_DV), F32)
        for h in range(B_HEADS):
            o = of_ref[h] + ob_ref[h]
            r = lax.rsqrt(jnp.mean(o * o, axis=-1, keepdims=True) + EPS)
            oh = o * r
            gh = g_ref[:, B_DV * h:B_DV * (h + 1)]
            sg = jax.nn.sigmoid(gh)
            dyh = dy_ref[:, B_DV * h:B_DV * (h + 1)].astype(F32)
            dn = dyh * (gh * sg)
            dg_ref[:, B_DV * h:B_DV * (h + 1)] = dyh * (oh * gout_v) * (sg * (1.0 + gh * (1.0 - sg)))
            dxh = dn * gout_v
            do_ref[h] = r * (dxh - oh * jnp.mean(dxh * oh, axis=-1, keepdims=True))
            acc = acc + jnp.sum(dn * oh, axis=0, keepdims=True)

        @pl.when(pl.program_id(0) == 0)
        def _():
            dgo_ref[...] = acc

        @pl.when(pl.program_id(0) > 0)
        def _():
            dgo_ref[...] += acc

    hm = pl.BlockSpec((B_HEADS, tm, B_DV), lambda i: (0, i, 0))
    row = pl.BlockSpec((tm, 512), lambda i: (i, 0))
    return _pallas(
        body, name=name, grid=(T // tm,),
        in_specs=[hm, hm, pl.BlockSpec((tm, 512), lambda i: (i, 7)), pl.BlockSpec((1, B_DV), lambda i: (0, 0)), row],
        out_specs=[hm, row, pl.BlockSpec((1, B_DV), lambda i: (0, 0))],
        out_shape=[jax.ShapeDtypeStruct((B_HEADS, T, B_DV), F32), jax.ShapeDtypeStruct((T, 512), F32),
                   jax.ShapeDtypeStruct((1, B_DV), F32)],
        compiler_params=_params("arbitrary"))(of, ob, zb, gout.reshape(1, B_DV), dy)


def _dzb_assemble(dq_f, dq_b, dzf, dzb_, dv_f, dv_b, dgate, *, name):
    T = dq_f.shape[0]
    tm = _tile(T, 256, 8)

    def body(qf, qb, zf, zr, vf, vr, dg, o_ref):
        o_ref[:, 0:1024] = (qf[...] + qb[...]).astype(BF16)
        o_ref[:, 1024:2048] = zf[...].astype(BF16)
        o_ref[:, 2048:3072] = zr[...].astype(BF16)
        for h in range(B_HEADS):
            o_ref[:, 3072 + B_DV * h:3072 + B_DV * (h + 1)] = (vf[h] + vr[h]).astype(BF16)
        o_ref[:, 3584:4096] = dg[...].astype(BF16)

    wide = pl.BlockSpec((tm, 1024), lambda i: (i, 0))
    hm = pl.BlockSpec((B_HEADS, tm, B_DV), lambda i: (0, i, 0))
    return _pallas(
        body, name=name, grid=(T // tm,),
        in_specs=[wide, wide, wide, wide, hm, hm, pl.BlockSpec((tm, 512), lambda i: (i, 0))],
        out_specs=pl.BlockSpec((tm, 4096), lambda i: (i, 0)),
        out_shape=jax.ShapeDtypeStruct((T, 4096), BF16), compiler_params=_params("parallel"))(dq_f, dq_b, dzf, dzb_, dv_f, dv_b, dgate)


C_SPAN = 3 * C_BLOCK
C_G = C_HEADS // C_KV_HEADS


def _t5_bucket(rel):
    nb = REL_BUCKETS // 2
    max_exact = nb // 2
    ret = (rel > 0).astype(jnp.int32) * nb
    n = jnp.abs(rel)
    large = max_exact + (jnp.log(jnp.maximum(n, 1).astype(F32) / max_exact)
                         / math.log(REL_MAX_DIST / max_exact) * (nb - max_exact)).astype(jnp.int32)
    large = jnp.minimum(large, nb - 1)
    return ret + jnp.where(n < max_exact, n, large)


def _swa_buckets():
    rel = jnp.arange(C_SPAN)[None, :] - C_BLOCK - jnp.arange(C_BLOCK)[:, None]
    return _t5_bucket(rel)


def _swa_specs(T):
    nb = T // C_BLOCK
    return nb, [
        pl.BlockSpec((C_BLOCK, 512), lambda n: (n, 0)),
        pl.BlockSpec((C_BLOCK, LANES), lambda n: (jnp.maximum(n - 1, 0), 4)),
        pl.BlockSpec((C_BLOCK, LANES), lambda n: (n, 4)),
        pl.BlockSpec((C_BLOCK, LANES), lambda n: (jnp.minimum(n + 1, nb - 1), 4)),
        pl.BlockSpec((C_BLOCK, LANES), lambda n: (jnp.maximum(n - 1, 0), 5)),
        pl.BlockSpec((C_BLOCK, LANES), lambda n: (n, 5)),
        pl.BlockSpec((C_BLOCK, LANES), lambda n: (jnp.minimum(n + 1, nb - 1), 5)),
        pl.BlockSpec((C_HEADS, C_BLOCK, C_SPAN), lambda n: (0, 0, 0)),
        pl.BlockSpec(memory_space=pltpu.SMEM),
    ]


def _swa_valid(n, T):
    qi = lax.broadcasted_iota(jnp.int32, (C_BLOCK, C_SPAN), 0)
    si = lax.broadcasted_iota(jnp.int32, (C_BLOCK, C_SPAN), 1)
    rel = si - C_BLOCK - qi
    kpos = (n - 1) * C_BLOCK + si
    return (jnp.abs(rel) <= C_WINDOW) & (kpos >= 0) & (kpos < T)


def _swa_probs(qh, kh, bias, valid, sink):
    s = _dot(qh, kh, _NT) * (C_DH ** -0.5) + bias
    s = jnp.where(valid, s, MASK_VALUE)
    m = jnp.maximum(jnp.max(s, axis=-1, keepdims=True), sink)
    e = jnp.exp(s - m)
    den = jnp.sum(e, axis=-1, keepdims=True) + jnp.exp(sink - m)
    return e / den, jnp.exp(sink - m) / den


def _swa_fwd(zc, bias, sink, *, name):
    T = zc.shape[0]
    nb, in_specs = _swa_specs(T)

    def body(q_ref, kp, kc, kn, vp, vc, vn, bias_ref, sink_ref, y_ref):
        n = pl.program_id(0)
        kcat = jnp.concatenate([kp[...], kc[...], kn[...]], axis=0)
        vcat = jnp.concatenate([vp[...], vc[...], vn[...]], axis=0)
        valid = _swa_valid(n, T)
        for h in range(C_HEADS):
            kv = h // C_G
            p, _ = _swa_probs(q_ref[:, C_DH * h:C_DH * (h + 1)], kcat[:, C_DH * kv:C_DH * (kv + 1)], bias_ref[h], valid, sink_ref[h])
            y_ref[:, C_DH * h:C_DH * (h + 1)] = _dot(p.astype(BF16), vcat[:, C_DH * kv:C_DH * (kv + 1)], _NN).astype(BF16)

    return _pallas(
        body, name=name, grid=(nb,), in_specs=in_specs, out_specs=pl.BlockSpec((C_BLOCK, 512), lambda n: (n, 0)),
        out_shape=jax.ShapeDtypeStruct((T, 512), BF16), compiler_params=_params("parallel"))(zc, zc, zc, zc, zc, zc, zc, bias, sink)


def _swa_bwd(zc, bias, sink, dy, *, name):
    T = zc.shape[0]
    nb, in_specs = _swa_specs(T)
    scale = C_DH ** -0.5

    def body(q_ref, kp, kc, kn, vp, vc, vn, bias_ref, sink_ref, dy_ref, dq_ref, dkc_ref, dvc_ref, dbias_ref, dsink_ref):
        n = pl.program_id(0)

        @pl.when(n == 0)
        def _():
            dbias_ref[...] = jnp.zeros(dbias_ref.shape, F32)
            dsink_ref[...] = jnp.zeros(dsink_ref.shape, F32)

        kcat = jnp.concatenate([kp[...], kc[...], kn[...]], axis=0)
        vcat = jnp.concatenate([vp[...], vc[...], vn[...]], axis=0)
        valid = _swa_valid(n, T)
        for kv in range(C_KV_HEADS):
            kh = kcat[:, C_DH * kv:C_DH * (kv + 1)]
            vh = vcat[:, C_DH * kv:C_DH * (kv + 1)]
            dk_acc = jnp.zeros((C_SPAN, C_DH), F32)
            dv_acc = jnp.zeros((C_SPAN, C_DH), F32)
            for g in range(C_G):
                h = kv * C_G + g
                qh = q_ref[:, C_DH * h:C_DH * (h + 1)]
                doh = dy_ref[:, C_DH * h:C_DH * (h + 1)].astype(BF16)
                p, p_sink = _swa_probs(qh, kh, bias_ref[h], valid, sink_ref[h])
                dp = _dot(doh, vh, _NT)
                rowdot = jnp.sum(p * dp, axis=-1, keepdims=True)
                ds = p * (dp - rowdot)
                dbias_ref[h] += ds
                tot = jnp.sum(jnp.sum(-p_sink * rowdot, axis=0, keepdims=True), axis=1, keepdims=True)
                dsink_ref[h:h + 1, :] += jnp.broadcast_to(tot, (1, LANES))
                dsb = (ds * scale).astype(BF16)
                dq_ref[:, C_DH * h:C_DH * (h + 1)] = _dot(dsb, kh, _NN).astype(BF16)
                dk_acc = dk_acc + _dot(dsb, qh, _TN)
                dv_acc = dv_acc + _dot(p.astype(BF16), doh, _TN)
            dkc_ref[0, :, C_DH * kv:C_DH * (kv + 1)] = dk_acc
            dvc_ref[0, :, C_DH * kv:C_DH * (kv + 1)] = dv_acc

    part = pl.BlockSpec((1, C_SPAN, LANES), lambda n: (n, 0, 0))
    dq, dkc, dvc, dbias, dsink = _pallas(
        body, name=name, grid=(nb,), in_specs=in_specs + [pl.BlockSpec((C_BLOCK, 512), lambda n: (n, 0))],
        out_specs=[pl.BlockSpec((C_BLOCK, 512), lambda n: (n, 0)), part, part,
                   pl.BlockSpec((C_HEADS, C_BLOCK, C_SPAN), lambda n: (0, 0, 0)), pl.BlockSpec((C_HEADS, LANES), lambda n: (0, 0))],
        out_shape=[jax.ShapeDtypeStruct((T, 512), BF16), jax.ShapeDtypeStruct((nb, C_SPAN, LANES), F32),
                   jax.ShapeDtypeStruct((nb, C_SPAN, LANES), F32), jax.ShapeDtypeStruct((C_HEADS, C_BLOCK, C_SPAN), F32),
                   jax.ShapeDtypeStruct((C_HEADS, LANES), F32)],
        compiler_params=_params("arbitrary"))(zc, zc, zc, zc, zc, zc, zc, bias, sink, dy)

    def combine(dq_ref, kp, kc, kn, vp, vc, vn, o_ref):
        n = pl.program_id(0)
        lo = (n > 0).astype(F32)
        hi = (n < nb - 1).astype(F32)
        o_ref[:, 0:512] = dq_ref[...]
        o_ref[:, 512:640] = (kp[0] * lo + kc[0] + kn[0] * hi).astype(BF16)
        o_ref[:, 640:768] = (vp[0] * lo + vc[0] + vn[0] * hi).astype(BF16)

    prev = pl.BlockSpec((1, C_BLOCK, LANES), lambda n: (jnp.maximum(n - 1, 0), 2, 0))
    cur = pl.BlockSpec((1, C_BLOCK, LANES), lambda n: (n, 1, 0))
    nxt = pl.BlockSpec((1, C_BLOCK, LANES), lambda n: (jnp.minimum(n + 1, nb - 1), 0, 0))
    dzc = _pallas(
        combine, name=name + "_combine", grid=(nb,),
        in_specs=[pl.BlockSpec((C_BLOCK, 512), lambda n: (n, 0)), prev, cur, nxt, prev, cur, nxt],
        out_specs=pl.BlockSpec((C_BLOCK, 768), lambda n: (n, 0)),
        out_shape=jax.ShapeDtypeStruct((T, 768), BF16), compiler_params=_params("parallel"))(dq, dkc, dkc, dkc, dvc, dvc, dvc)
    return dzc, dbias, dsink


def _merge_tiles(T):
    return _tile(T, 512, 8), 512


def _merge_fwd(ya, yb, yc, wa, wb, wc, zg, *, name):
    T = ya.shape[0]
    tm, tn = _merge_tiles(T)
    nd = D_MODEL // tn

    def body(ya_ref, yb_ref, yc_ref, wa_ref, wb_ref, wc_ref, ga_ref, gb_ref, gc_ref, o_ref):
        acc = jax.nn.sigmoid(ga_ref[...]) * _dot(ya_ref[...].astype(BF16), wa_ref[...], _NN)
        acc += jax.nn.sigmoid(gb_ref[...]) * _dot(yb_ref[...].astype(BF16), wb_ref[...], _NN)
        acc += jax.nn.sigmoid(gc_ref[...]) * _dot(yc_ref[...].astype(BF16), wc_ref[...], _NN)
        o_ref[...] = acc.astype(BF16)

    y = pl.BlockSpec((tm, 512), lambda i, j: (i, 0))
    w = pl.BlockSpec((512, tn), lambda i, j: (0, j))
    gate = lambda b: pl.BlockSpec((tm, tn), lambda i, j: (i, b * nd + j))
    return _pallas(
        body, name=name, grid=(T // tm, nd), in_specs=[y, y, y, w, w, w, gate(0), gate(1), gate(2)],
        out_specs=pl.BlockSpec((tm, tn), lambda i, j: (i, j)),
        out_shape=jax.ShapeDtypeStruct((T, D_MODEL), BF16),
        compiler_params=_params("parallel", "parallel"))(ya, yb, yc, wa, wb, wc, zg, zg, zg)


def _merge_bwd(ya, yb, yc, wa, wb, wc, zg, dm, *, name):
    T = ya.shape[0]
    tm, tn = _merge_tiles(T)
    nd = D_MODEL // tn

    def body(ya_ref, yb_ref, yc_ref, wa_ref, wb_ref, wc_ref, ga_ref, gb_ref, gc_ref, dm_ref, *outs):
        dmv = dm_ref[...].astype(F32)
        for y_ref, w_ref, g_ref, du_ref, dg_ref in zip((ya_ref, yb_ref, yc_ref), (wa_ref, wb_ref, wc_ref),
                                                       (ga_ref, gb_ref, gc_ref), outs[:3], outs[3:]):
            u = _dot(y_ref[...].astype(BF16), w_ref[...], _NN)
            sg = jax.nn.sigmoid(g_ref[...])
            du_ref[...] = (dmv * sg).astype(BF16)
            dg_ref[...] = (dmv * u * sg * (1.0 - sg)).astype(BF16)

    y = pl.BlockSpec((tm, 512), lambda i, j: (i, 0))
    w = pl.BlockSpec((512, tn), lambda i, j: (0, j))
    gate = lambda b: pl.BlockSpec((tm, tn), lambda i, j: (i, b * nd + j))
    t = pl.BlockSpec((tm, tn), lambda i, j: (i, j))
    return _pallas(
        body, name=name, grid=(T // tm, nd), in_specs=[y, y, y, w, w, w, gate(0), gate(1), gate(2), t],
        out_specs=[t] * 6, out_shape=[jax.ShapeDtypeStruct((T, D_MODEL), BF16)] * 6,
        compiler_params=_params("parallel", "parallel"))(ya, yb, yc, wa, wb, wc, zg, zg, zg, dm)


def _cross_fwd(q, kvm, *, name):
    T = q.shape[0]
    M = kvm.shape[0]
    tm = _tile(T, 512, 8)
    scale = X_DH ** -0.5

    def body(q_ref, k_ref, v_ref, o_ref):
        for h in range(X_HEADS):
            cs = slice(X_DH * h, X_DH * (h + 1))
            s = _dot(q_ref[:, cs], k_ref[:, cs], _NT) * scale
            e = jnp.exp(s - jnp.max(s, axis=-1, keepdims=True))
            p = e / jnp.sum(e, axis=-1, keepdims=True)
            o_ref[:, cs] = _dot(p.astype(BF16), v_ref[:, cs], _NN).astype(BF16)

    row = pl.BlockSpec((tm, D_MODEL), lambda i: (i, 0))
    return _pallas(
        body, name=name, grid=(T // tm,),
        in_specs=[row, pl.BlockSpec((M, D_MODEL), lambda i: (0, 0)), pl.BlockSpec((M, D_MODEL), lambda i: (0, 1))],
        out_specs=row, out_shape=jax.ShapeDtypeStruct((T, D_MODEL), BF16), compiler_params=_params("parallel"))(q, kvm, kvm)


def _cross_bwd(q, kvm, do, *, name):
    T = q.shape[0]
    M = kvm.shape[0]
    tm = _tile(T, 512, 8)
    scale = X_DH ** -0.5

    def body(q_ref, k_ref, v_ref, do_ref, dq_ref, dkv_ref):
        @pl.when(pl.program_id(0) == 0)
        def _():
            dkv_ref[...] = jnp.zeros(dkv_ref.shape, F32)

        for h in range(X_HEADS):
            cs = slice(X_DH * h, X_DH * (h + 1))
            vs = slice(D_MODEL + X_DH * h, D_MODEL + X_DH * (h + 1))
            qh, kh, doh = q_ref[:, cs], k_ref[:, cs], do_ref[:, cs]
            s = _dot(qh, kh, _NT) * scale
            e = jnp.exp(s - jnp.max(s, axis=-1, keepdims=True))
            p = e / jnp.sum(e, axis=-1, keepdims=True)
            dp = _dot(doh, v_ref[:, cs], _NT)
            ds = (p * (dp - jnp.sum(p * dp, axis=-1, keepdims=True)) * scale).astype(BF16)
            dq_ref[:, cs] = _dot(ds, kh, _NN).astype(BF16)
            dkv_ref[:, cs] += _dot(ds, qh, _TN)
            dkv_ref[:, vs] += _dot(p.astype(BF16), doh, _TN)

    row = pl.BlockSpec((tm, D_MODEL), lambda i: (i, 0))
    return _pallas(
        body, name=name, grid=(T // tm,),
        in_specs=[row, pl.BlockSpec((M, D_MODEL), lambda i: (0, 0)), pl.BlockSpec((M, D_MODEL), lambda i: (0, 1)), row],
        out_specs=[row, pl.BlockSpec((M, 2 * D_MODEL), lambda i: (0, 0))],
        out_shape=[jax.ShapeDtypeStruct((T, D_MODEL), BF16), jax.ShapeDtypeStruct((M, 2 * D_MODEL), F32)],
        compiler_params=_params("arbitrary"))(q, kvm, kvm, do)


def _ffn_up(h, w1, w3, *, name):
    T = h.shape[0]
    tm = _tile(T, 512, 8)
    tn = 256

    def body(h_ref, w1_ref, w3_ref, a_ref, b_ref, act_ref):
        hv = h_ref[...]
        a = _dot(hv, w1_ref[...], _NN)
        b = _dot(hv, w3_ref[...], _NN)
        a_ref[...] = a
        b_ref[...] = b
        act_ref[...] = (a * jax.nn.sigmoid(a) * b).astype(BF16)

    w = pl.BlockSpec((D_MODEL, tn), lambda i, j: (0, j))
    t = pl.BlockSpec((tm, tn), lambda i, j: (i, j))
    return _pallas(
        body, name=name, grid=(T // tm, D_FF // tn), in_specs=[pl.BlockSpec((tm, D_MODEL), lambda i, j: (i, 0)), w, w],
        out_specs=[t, t, t],
        out_shape=[jax.ShapeDtypeStruct((T, D_FF), F32), jax.ShapeDtypeStruct((T, D_FF), F32), jax.ShapeDtypeStruct((T, D_FF), BF16)],
        compiler_params=_params("parallel", "parallel"))(h, w1, w3)


def _ffn_dact(dx, w2, a, b, *, name):
    T = dx.shape[0]
    tm = _tile(T, 512, 8)
    tn = 256

    def body(dx_ref, w2_ref, a_ref, b_ref, da_ref, db_ref):
        dact = _dot(dx_ref[...].astype(BF16), w2_ref[...], _NT)
        av = a_ref[...]
        sg = jax.nn.sigmoid(av)
        da_ref[...] = (dact * b_ref[...] * (sg * (1.0 + av * (1.0 - sg)))).astype(BF16)
        db_ref[...] = (dact * (av * sg)).astype(BF16)

    t = pl.BlockSpec((tm, tn), lambda i, j: (i, j))
    return _pallas(
        body, name=name, grid=(T // tm, D_FF // tn),
        in_specs=[pl.BlockSpec((tm, D_MODEL), lambda i, j: (i, 0)), pl.BlockSpec((tn, D_MODEL), lambda i, j: (j, 0)), t, t],
        out_specs=[t, t], out_shape=[jax.ShapeDtypeStruct((T, D_FF), BF16)] * 2,
        compiler_params=_params("parallel", "parallel"))(dx, w2, a, b)


def _loss_head(x, g, target, *, name):
    T, D = x.shape
    tm = _tile(T, 512, 8)

    def body(x_ref, g_ref, t_ref, loss_ref, dx_ref, dg_ref):
        xv = x_ref[...]
        r = lax.rsqrt(jnp.mean(xv * xv, axis=-1, keepdims=True) + EPS)
        xh = xv * r
        gv = g_ref[...]
        err = xh * gv - t_ref[...]
        dy = err * (1.0 / D)
        dxh = dy * gv
        dx_ref[...] = r * (dxh - xh * jnp.mean(dxh * xh, axis=-1, keepdims=True))
        lpart = 0.5 * jnp.sum(jnp.mean(err * err, axis=-1, keepdims=True), axis=0, keepdims=True)
        gpart = jnp.sum(dy * xh, axis=0, keepdims=True)

        @pl.when(pl.program_id(0) == 0)
        def _():
            loss_ref[...] = jnp.broadcast_to(lpart, (1, LANES))
            dg_ref[...] = gpart

        @pl.when(pl.program_id(0) > 0)
        def _():
            loss_ref[...] += jnp.broadcast_to(lpart, (1, LANES))
            dg_ref[...] += gpart

    row = pl.BlockSpec((tm, D), lambda i: (i, 0))
    vec = pl.BlockSpec((1, D), lambda i: (0, 0))
    return _pallas(
        body, name=name, grid=(T // tm,), in_specs=[row, vec, row],
        out_specs=[pl.BlockSpec((1, LANES), lambda i: (0, 0)), row, vec],
        out_shape=[jax.ShapeDtypeStruct((1, LANES), F32), jax.ShapeDtypeStruct((T, D), F32), jax.ShapeDtypeStruct((1, D), F32)],
        compiler_params=_params("arbitrary"))(x, g.reshape(1, D), target)


IN_CQ, IN_CKV, IN_KR, IN_B, IN_C, IN_G, IN_END = 0, 384, 640, 672, 4768, 5536, 8608
WEIGHT_NAMES = ("w_in", "g_mix", "a_gq", "a_gkv", "a_wuq", "a_wukv", "b_lb", "b_gout", "c_sink", "rel_bias",
                "w_br_a", "w_br_b", "w_br_c", "w_out", "g_x", "g_mem", "x_wq", "x_wkv", "x_wo", "g_ffn",
                "f_w1", "f_w3", "f_w2", "g_final")


def _lower_bounds(b_lb):
    sm = jax.nn.softmax(b_lb.astype(F32), axis=1)
    return jnp.cumsum(sm, axis=1) - sm[:, :1]


def _layer_weights(w, l):
    bf = lambda a: a.astype(BF16)
    w_in = bf(w["w_in"][l])
    kr = w_in[:, IN_KR:IN_B]
    wa = jnp.concatenate([w_in[:, IN_CQ:IN_CKV], kr, _rope_swap_cols(kr), jnp.zeros((D_MODEL, 64), BF16),
                          w_in[:, IN_CKV:IN_KR]], axis=1)
    wuq = bf(w["a_wuq"][l]).reshape(A_Q_RANK, A_HEADS, A_QK)
    zeros = lambda n: jnp.zeros((A_Q_RANK, A_HEADS, n), BF16)
    wuq_pad = jnp.concatenate([wuq, zeros(A_PAD - A_QK)], axis=-1)
    wuq_sw = jnp.concatenate([zeros(A_NOPE), _rope_swap_cols(wuq[..., A_NOPE:]), zeros(A_PAD - A_QK)], axis=-1)
    wq2 = jnp.concatenate([wuq_pad.reshape(A_Q_RANK, -1), wuq_sw.reshape(A_Q_RANK, -1)], axis=1)
    wukv = bf(w["a_wukv"][l]).reshape(A_KV_RANK, A_HEADS, A_NOPE + A_V)
    wkv = jnp.concatenate([wukv[..., :A_NOPE].reshape(A_KV_RANK, -1), wukv[..., A_NOPE:].reshape(A_KV_RANK, -1)], axis=1)
    return dict(wa=wa, wb=w_in[:, IN_B:IN_C], wc=w_in[:, IN_C:IN_G], wg=w_in[:, IN_G:IN_END], wq2=wq2, wkv=wkv,
                w_br_a=bf(w["w_br_a"][l]), w_br_b=bf(w["w_br_b"][l]), w_br_c=bf(w["w_br_c"][l]), w_out=bf(w["w_out"][l]),
                x_wq=bf(w["x_wq"][l]), x_wkv=bf(w["x_wkv"][l]), x_wo=bf(w["x_wo"][l]),
                f_w1=bf(w["f_w1"][l]), f_w3=bf(w["f_w3"][l]), f_w2=bf(w["f_w2"][l]))


def _layer_fwd(l, x, mem, w, lw, lower, bias, tabs):
    n = lambda s: f"l{l}_{s}"
    cq_t, sq_t, ck, _ = tabs
    s = dict(x=x)
    s["h0"] = h0 = _rms(x, w["g_mix"][l], name=n("rms_mix"))
    s["za"] = za = _mm(h0, lw["wa"], name=n("in_a"))
    s["zb"] = zb = _mm(h0, lw["wb"], name=n("in_b"))
    s["zc"] = zc = _mm(h0, lw["wc"], out_dtype=BF16, name=n("in_c"))
    s["zg"] = zg = _mm(h0, lw["wg"], name=n("in_g"))
    s["cqn"] = cqn = _rms(za, w["a_gq"][l], col=0, width=A_Q_RANK, name=n("rms_cq"))
    s["ckvn"] = ckvn = _rms(za, w["a_gkv"][l], col=512, width=A_KV_RANK, name=n("rms_ckv"))
    s["q"] = q = _qrope(_mm(cqn, lw["wq2"], name=n("uq")), cq_t, sq_t, name=n("qrope"))
    s["kv"] = kv = _mm(ckvn, lw["wkv"], out_dtype=BF16, name=n("ukv"))
    s["k"], vx = k, _ = _kprep(kv, za, ck, name=n("kprep"))
    s["ya"], s["lse"] = ya, _ = _flash_fwd(q, k, vx.T, name=n("mla"))
    lb_f, lb_b = lower[0, l].reshape(1, -1), lower[1, l].reshape(1, -1)
    s["of"], s["stf"] = of, _ = _hgrn_fwd(zb, lb_f, reverse=False, name=n("hgrn_f"))
    s["ob"], s["stb"] = ob, _ = _hgrn_fwd(zb, lb_b, reverse=True, name=n("hgrn_b"))
    s["yb"] = yb = _hgrn_out(of, ob, zb, w["b_gout"][l], name=n("hgrn_out"))
    s["yc"] = yc = _swa_fwd(zc, bias, w["c_sink"][l], name=n("swa"))
    s["merged"] = merged = _merge_fwd(ya, yb, yc, lw["w_br_a"], lw["w_br_b"], lw["w_br_c"], zg, name=n("merge"))
    s["x1"] = x1 = _mm(merged, lw["w_out"], add=x, name=n("out"))
    s["h1"] = h1 = _rms(x1, w["g_x"][l], name=n("rms_x"))
    s["qx"] = qx = _mm(h1, lw["x_wq"], out_dtype=BF16, name=n("xq"))
    s["memn"] = memn = _rms(mem, w["g_mem"][l], name=n("rms_mem"))
    s["kvm"] = kvm = _mm(memn, lw["x_wkv"], out_dtype=BF16, name=n("xkv"))
    s["ox"] = ox = _cross_fwd(qx, kvm, name=n("cross"))
    s["x2"] = x2 = _mm(ox, lw["x_wo"], add=x1, name=n("xo"))
    s["h2"] = h2 = _rms(x2, w["g_ffn"][l], name=n("rms_ffn"))
    s["a"], s["b"], s["act"] = a, b, act = _ffn_up(h2, lw["f_w1"], lw["f_w3"], name=n("ffn_up"))
    x3 = _mm(act, lw["f_w2"], add=x2, name=n("ffn_down"))
    return x3, s


def _layer_bwd(l, dx3, mem, w, lw, lower, bias, tabs, s):
    n = lambda t: f"l{l}_b_{t}"
    cq_t, sq_t, _, ck_t = tabs
    g = {}
    da, db = _ffn_dact(dx3, lw["f_w2"], s["a"], s["b"], name=n("ffn_dact"))
    g["f_w2"] = _mm(s["act"], dx3, mode="tn", name=n("dw2"))
    dh2 = _mm(db, lw["f_w3"], mode="nt", add=_mm(da, lw["f_w1"], mode="nt", name=n("dh2a")), name=n("dh2b"))
    g["f_w1"] = _mm(s["h2"], da, mode="tn", name=n("dw1"))
    g["f_w3"] = _mm(s["h2"], db, mode="tn", name=n("dw3"))
    dx2, g["g_ffn"] = _rms_bwd(s["x2"], w["g_ffn"][l], dh2, res=dx3, name=n("rms_ffn"))
    dox = _mm(dx2, lw["x_wo"], mode="nt", out_dtype=BF16, name=n("dox"))
    g["x_wo"] = _mm(s["ox"], dx2, mode="tn", name=n("dwo"))
    dqx, dkvm = _cross_bwd(s["qx"], s["kvm"], dox, name=n("cross"))
    g["x_wq"] = _mm(s["h1"], dqx, mode="tn", name=n("dwq"))
    dh1 = _mm(dqx, lw["x_wq"], mode="nt", name=n("dh1"))
    g["x_wkv"] = _mm(s["memn"], dkvm, mode="tn", name=n("dwkv"))
    dmemn = _mm(dkvm, lw["x_wkv"], mode="nt", name=n("dmemn"))
    _, g["g_mem"] = _rms_bwd(mem, w["g_mem"][l], dmemn, name=n("rms_mem"))
    dx1, g["g_x"] = _rms_bwd(s["x1"], w["g_x"][l], dh1, res=dx2, name=n("rms_x"))
    dmerged = _mm(dx1, lw["w_out"], mode="nt", name=n("dmerged"))
    g["w_out"] = _mm(s["merged"], dx1, mode="tn", name=n("dwout"))
    dua, dub, duc, dga, dgb, dgc = _merge_bwd(s["ya"], s["yb"], s["yc"], lw["w_br_a"], lw["w_br_b"], lw["w_br_c"],
                                              s["zg"], dmerged, name=n("merge"))
    dya = _mm(dua, lw["w_br_a"], mode="nt", out_dtype=BF16, name=n("dya"))
    dyb = _mm(dub, lw["w_br_b"], mode="nt", name=n("dyb"))
    dyc = _mm(duc, lw["w_br_c"], mode="nt", out_dtype=BF16, name=n("dyc"))
    g["w_br_a"] = _mm(s["ya"], dua, mode="tn", name=n("dwbra"))
    g["w_br_b"] = _mm(s["yb"], dub, mode="tn", name=n("dwbrb"))
    g["w_br_c"] = _mm(s["yc"], duc, mode="tn", name=n("dwbrc"))
    dzc, dbias, dsink = _swa_bwd(s["zc"], bias, w["c_sink"][l], dyc, name=n("swa"))
    g["c_sink"] = dsink[:, 0]
    g["bias"] = dbias
    lb_f, lb_b = lower[0, l].reshape(1, -1), lower[1, l].reshape(1, -1)
    do_, dgate, dgout = _hgrn_out_bwd(s["of"], s["ob"], s["zb"], w["b_gout"][l], dyb, name=n("hgrn_out"))
    g["b_gout"] = dgout[0]
    dq_f, dzf, dv_f, dlb_f = _hgrn_bwd(s["zb"], lb_f, do_, s["stf"], reverse=False, name=n("hgrn_f"))
    dq_b, dzr, dv_b, dlb_b = _hgrn_bwd(s["zb"], lb_b, do_, s["stb"], reverse=True, name=n("hgrn_b"))
    g["lower"] = jnp.concatenate([dlb_f, dlb_b], axis=0)
    dzb = _dzb_assemble(dq_f, dq_b, dzf, dzr, dv_f, dv_b, dgate, name=n("dzb"))
    delta = _attn_delta(s["ya"], dya, name=n("mla_delta"))
    dq_part, dkt, dvt = _flash_bwd(s["q"], s["q"].T, s["k"], s["kv"], dya, dya.T, s["lse"].reshape(A_HEADS, -1, 1), delta, name=n("mla"))
    dq2 = _qrope_bwd(_dq_sum(dq_part, name=n("mla_dq")), cq_t, sq_t, name=n("qrope"))
    dcqn = _mm(dq2, lw["wq2"], mode="nt", name=n("dcqn"))
    dwq2 = _mm(s["cqn"], dq2, mode="tn", name=n("dwq2")).reshape(A_Q_RANK, 2, A_HEADS, A_PAD)
    dknt, dzkrt = _kprep_bwd(dkt, ck_t, name=n("kprep"))
    wkv = lw["wkv"]
    dckvn = _mm(dvt, wkv[:, 512:].T, mode="tn", add=_mm(dknt, wkv[:, :512].T, mode="tn", name=n("dckvn_k")), name=n("dckvn_v"))
    dwkn = _mm(dknt, s["ckvn"], name=n("dwkn")).T
    dwv = _mm(dvt, s["ckvn"], name=n("dwv")).T
    dzcq, dgq = _rms_bwd(s["za"], w["a_gq"][l], dcqn, col=0, width=A_Q_RANK, out_dtype=BF16, name=n("rms_cq"))
    dzckv, dgkv = _rms_bwd(s["za"], w["a_gkv"][l], dckvn, col=512, width=A_KV_RANK, out_dtype=BF16, name=n("rms_ckv"))
    g["a_gq"], g["a_gkv"] = dgq[0], dgkv[0]
    sw = jnp.concatenate([jnp.zeros((A_Q_RANK, A_HEADS, A_NOPE), F32), _rope_unswap_cols(dwq2[:, 1, :, A_NOPE:A_QK])], axis=-1)
    g["a_wuq"] = (dwq2[:, 0, :, :A_QK] + sw).reshape(A_Q_RANK, -1)
    g["a_wukv"] = jnp.concatenate([dwkn.reshape(A_KV_RANK, A_HEADS, A_NOPE), dwv.reshape(A_KV_RANK, A_HEADS, A_V)], axis=-1).reshape(A_KV_RANK, -1)
    wa = lw["wa"]
    pieces = [(dzcq, wa[:, 0:384]), (dzckv, wa[:, 512:768]), (dzb, lw["wb"]), (dzc, lw["wc"]),
              (dga, lw["wg"][:, 0:1024]), (dgb, lw["wg"][:, 1024:2048]), (dgc, lw["wg"][:, 2048:3072])]
    dh0 = _mm(dzkrt, wa[:, 384:512].T, mode="tn", name=n("dh0_kr"))
    dwkr = _mm(dzkrt, s["h0"], name=n("dwin_kr")).T
    dwkr = dwkr[:, 0:A_ROPE] + _rope_unswap_cols(dwkr[:, A_ROPE:2 * A_ROPE])
    dws = []
    for i, (dz, wp) in enumerate(pieces):
        dh0 = _mm(dz, wp, mode="nt", add=dh0, name=n(f"dh0_{i}"))
        dws.append(_mm(s["h0"], dz, mode="tn", name=n(f"dwin_{i}")))
    g["w_in"] = jnp.concatenate([dws[0], dws[1], dwkr] + dws[2:], axis=1)
    dx, g["g_mix"] = _rms_bwd(s["x"], w["g_mix"][l], dh0, res=dx1, name=n("rms_mix"))
    return dx, g


def _local_step(x, mem, target, w):
    T = x.shape[0]
    tabs = _rope_tables(T)
    lower, lower_vjp = jax.vjp(_lower_bounds, w["b_lb"])
    buckets = _swa_buckets()
    onehot = (buckets.reshape(-1)[:, None] == jnp.arange(REL_BUCKETS)[None, :]).astype(F32)
    bias = jnp.dot(w["rel_bias"].astype(F32).T, onehot.T, precision=lax.Precision.HIGHEST).reshape(C_HEADS, C_BLOCK, C_SPAN)
    lws, saved = [], []
    for l in range(DEPTH):
        lws.append(_layer_weights(w, l))
        x, s = _layer_fwd(l, x, mem, w, lws[l], lower, bias, tabs)
        saved.append(s)
    loss, dx, dg_final = _loss_head(x, w["g_final"], target, name="loss_head")
    layer_grads = [None] * DEPTH
    for l in reversed(range(DEPTH)):
        dx, layer_grads[l] = _layer_bwd(l, dx, mem, w, lws[l], lower, bias, tabs, saved[l])
        saved[l] = None
    grads = {}
    for name in WEIGHT_NAMES:
        if name in layer_grads[0]:
            grads[name] = jnp.stack([layer_grads[l][name].reshape(w[name].shape[1:]) for l in range(DEPTH)])
    grads["g_final"] = dg_final[0]
    dlower = jnp.stack([layer_grads[l]["lower"] for l in range(DEPTH)], axis=1)
    grads["b_lb"] = lower_vjp(dlower)[0]
    dbias = layer_grads[0]["bias"] + layer_grads[1]["bias"]
    grads["rel_bias"] = jnp.dot(onehot.T, dbias.reshape(C_HEADS, -1).T, precision=lax.Precision.HIGHEST)
    return loss, dx, grads


N_CHIPS = 4
PACK_COLS = 1024
PACK_ALIGN = 32 * PACK_COLS
SHARDED = (("w_in", 2), ("a_wuq", 2), ("a_wukv", 2), ("b_lb", 2), ("w_br_a", 2), ("w_br_b", 2), ("w_br_c", 2), ("w_out", 1),
           ("x_wq", 1), ("x_wkv", 2), ("x_wo", 1), ("f_w1", 2), ("f_w3", 2), ("f_w2", 1))
REPLICATED = ("g_mix", "a_gq", "a_gkv", "b_gout", "c_sink", "rel_bias", "g_x", "g_mem", "g_ffn", "g_final")
MESH_IDS = pl.DeviceIdType.MESH
ANY_SPEC = pl.BlockSpec(memory_space=pl.ANY)


def _pack(arrs, cols, align):
    flat = jnp.concatenate([a.reshape(-1) for a in arrs])
    pad = (-flat.shape[0]) % align
    return jnp.pad(flat, (0, pad)).reshape(-1, cols)


def _unpack(buf, shapes):
    flat = buf.reshape(-1)
    out, start = [], 0
    for shp in shapes:
        size = math.prod(shp)
        out.append(flat[start:start + size].reshape(shp))
        start += size
    return out


def _chip_peers():
    x, y, c = lax.axis_index("x"), lax.axis_index("y"), lax.axis_index("c")
    return x, y, c, [(1 - x, y), (x, 1 - y), (1 - x, 1 - y)]


def _chip_gather(src, *, name):
    R, C = src.shape

    def body(src_ref, out_ref, send_sems, recv_sems, local_sem):
        x, y, c, chips = _chip_peers()
        me = 2 * x + y
        local = pltpu.make_async_copy(src_ref, out_ref.at[me], local_sem)
        local.start()

        def copy(j, slot):
            px, py = chips[j]
            return pltpu.make_async_remote_copy(src_ref=src_ref, dst_ref=out_ref.at[slot], send_sem=send_sems.at[j],
                                                recv_sem=recv_sems.at[j], device_id=(px, py, c), device_id_type=MESH_IDS)

        sends = [copy(j, me) for j in range(3)]
        for cp in sends:
            cp.start()
        for j, (px, py) in enumerate(chips):
            copy(j, 2 * px + py).wait_recv()
        for cp in sends:
            cp.wait_send()
        local.wait()

    return _pallas(
        body, name=name, in_specs=[ANY_SPEC], out_specs=ANY_SPEC, out_shape=jax.ShapeDtypeStruct((N_CHIPS, R, C), src.dtype),
        scratch_shapes=[pltpu.SemaphoreType.DMA((3,)), pltpu.SemaphoreType.DMA((3,)), pltpu.SemaphoreType.DMA],
        compiler_params=pltpu.CompilerParams(has_side_effects=True))(src)


def _chip_scatter(src, *, name):
    _, R, C = src.shape

    def body(src_ref, out_ref, send_sems, recv_sems):
        x, y, c, chips = _chip_peers()
        me = 2 * x + y

        def copy(j, seg):
            px, py = chips[j]
            return pltpu.make_async_remote_copy(src_ref=src_ref.at[seg], dst_ref=out_ref.at[j], send_sem=send_sems.at[j],
                                                recv_sem=recv_sems.at[j], device_id=(px, py, c), device_id_type=MESH_IDS)

        sends = [copy(j, 2 * px + py) for j, (px, py) in enumerate(chips)]
        for cp in sends:
            cp.start()
        for j in range(3):
            copy(j, me).wait_recv()
        for cp in sends:
            cp.wait_send()

    return _pallas(
        body, name=name, in_specs=[ANY_SPEC], out_specs=ANY_SPEC, out_shape=jax.ShapeDtypeStruct((3, R, C), src.dtype),
        scratch_shapes=[pltpu.SemaphoreType.DMA((3,)), pltpu.SemaphoreType.DMA((3,))],
        compiler_params=pltpu.CompilerParams(has_side_effects=True))(src)


PAIR_CHUNKS = 4


def _pair_swap(src, *, halves, name):
    R, C = src.shape[-2:]
    n = N_CHIPS if halves else 1
    rc = R // PAIR_CHUNKS
    assert rc * PAIR_CHUNKS == R and rc % 16 == 0, R

    def body(src_ref, out_ref, send_sems, recv_sems):
        x, y, c = lax.axis_index("x"), lax.axis_index("y"), lax.axis_index("c")
        copies = []
        for k in range(n):
            for r in range(PAIR_CHUNKS):
                rows = pl.ds(r * rc, rc)
                s = src_ref.at[k, 1 - c, rows] if halves else src_ref.at[rows]
                d = out_ref.at[k, rows] if halves else out_ref.at[rows]
                i = k * PAIR_CHUNKS + r
                copies.append(pltpu.make_async_remote_copy(src_ref=s, dst_ref=d, send_sem=send_sems.at[i], recv_sem=recv_sems.at[i],
                                                           device_id=(x, y, 1 - c), device_id_type=MESH_IDS))
        for cp in copies:
            cp.start()
        for cp in copies:
            cp.wait_recv()
        for cp in copies:
            cp.wait_send()

    shape = (N_CHIPS, R, C) if halves else (R, C)
    return _pallas(
        body, name=name, in_specs=[ANY_SPEC], out_specs=ANY_SPEC, out_shape=jax.ShapeDtypeStruct(shape, src.dtype),
        scratch_shapes=[pltpu.SemaphoreType.DMA((n * PAIR_CHUNKS,)), pltpu.SemaphoreType.DMA((n * PAIR_CHUNKS,))],
        compiler_params=pltpu.CompilerParams(has_side_effects=True))(src)


def _pair_add(g4, got, c, *, name):
    _, _, R, C = g4.shape
    tr = _tile(R, 512, 16)

    def body(c_ref, mine_ref, got_ref, o_ref, ob_ref):
        s = mine_ref[0, 0] + got_ref[0].astype(F32)
        o_ref[0] = s
        ob_ref[0] = s.astype(BF16)

    blk = pl.BlockSpec((1, tr, C), lambda k, i, c_ref: (k, i, 0))
    grid_spec = pltpu.PrefetchScalarGridSpec(
        num_scalar_prefetch=1, grid=(N_CHIPS, R // tr),
        in_specs=[pl.BlockSpec((1, 1, tr, C), lambda k, i, c_ref: (k, c_ref[0], i, 0)), blk], out_specs=[blk, blk])
    return _pallas(body, name=name, grid_spec=grid_spec,
                   out_shape=[jax.ShapeDtypeStruct((N_CHIPS, R, C), F32), jax.ShapeDtypeStruct((N_CHIPS, R, C), BF16)],
                   compiler_params=_params("parallel", "parallel"))(c, g4, got)


def _chip_sum(pair_sum, landed, me, *, name):
    _, R, C = pair_sum.shape
    tr = _tile(R, 512, 16)

    def body(me_ref, own_ref, landed_ref, o_ref):
        acc = own_ref[0]
        for j in range(3):
            acc = acc + landed_ref[j].astype(F32)
        o_ref[...] = acc

    grid_spec = pltpu.PrefetchScalarGridSpec(
        num_scalar_prefetch=1, grid=(R // tr,),
        in_specs=[pl.BlockSpec((1, tr, C), lambda i, me_ref: (me_ref[0], i, 0)), pl.BlockSpec((3, tr, C), lambda i, me_ref: (0, i, 0))],
        out_specs=pl.BlockSpec((tr, C), lambda i, me_ref: (i, 0)))
    return _pallas(body, name=name, grid_spec=grid_spec, out_shape=jax.ShapeDtypeStruct((R, C), F32),
                   compiler_params=_params("parallel"))(me, pair_sum, landed)


def _join_halves(mine, got, c, *, name):
    R, C = mine.shape
    tr = _tile(R, 512, 16)

    def body(c_ref, mine_ref, got_ref, o_ref):
        use_mine = pl.program_id(0) == c_ref[0]
        o_ref[0] = jnp.where(use_mine, mine_ref[...], got_ref[...])

    blk = pl.BlockSpec((tr, C), lambda h, i, c_ref: (i, 0))
    grid_spec = pltpu.PrefetchScalarGridSpec(num_scalar_prefetch=1, grid=(2, R // tr), in_specs=[blk, blk],
                                             out_specs=pl.BlockSpec((1, tr, C), lambda h, i, c_ref: (h, i, 0)))
    return _pallas(body, name=name, grid_spec=grid_spec, out_shape=jax.ShapeDtypeStruct((2, R, C), mine.dtype),
                   compiler_params=_params("parallel", "parallel"))(c, mine, got).reshape(2 * R, C)


def _gather8(s, *, name):
    R, C = s.shape

    def body(s_ref, out_ref, send_sems, recv_sems):
        x, y, c = lax.axis_index("x"), lax.axis_index("y"), lax.axis_index("c")
        me = 4 * x + 2 * y + c
        flips = [(dx, dy, dc) for dx in (0, 1) for dy in (0, 1) for dc in (0, 1)][1:]
        out_ref[me] = s_ref[...]

        def copy(j, slot):
            dx, dy, dc = flips[j]
            return pltpu.make_async_remote_copy(src_ref=s_ref, dst_ref=out_ref.at[slot], send_sem=send_sems.at[j],
                                                recv_sem=recv_sems.at[j], device_id=(x ^ dx, y ^ dy, c ^ dc), device_id_type=MESH_IDS)

        sends = [copy(j, me) for j in range(7)]
        for cp in sends:
            cp.start()
        for j, (dx, dy, dc) in enumerate(flips):
            copy(j, 4 * (x ^ dx) + 2 * (y ^ dy) + (c ^ dc)).wait_recv()
        for cp in sends:
            cp.wait_send()

    vmem = pl.BlockSpec(memory_space=pltpu.VMEM)
    return _pallas(
        body, name=name, in_specs=[vmem], out_specs=vmem, out_shape=jax.ShapeDtypeStruct((8, R, C), s.dtype),
        scratch_shapes=[pltpu.SemaphoreType.DMA((7,)), pltpu.SemaphoreType.DMA((7,))],
        compiler_params=pltpu.CompilerParams(has_side_effects=True))(s)


def _sum_slots(a, *, name):
    n, R, C = a.shape
    tr = _tile(R, 512, 8)

    def body(a_ref, o_ref):
        acc = a_ref[0]
        for k in range(1, n):
            acc = acc + a_ref[k]
        o_ref[...] = acc

    return _pallas(body, name=name, grid=(R // tr,), in_specs=[pl.BlockSpec((n, tr, C), lambda i: (0, i, 0))],
                   out_specs=pl.BlockSpec((tr, C), lambda i: (i, 0)), out_shape=jax.ShapeDtypeStruct((R, C), a.dtype),
                   compiler_params=_params("parallel"))(a)


def _adamw(w, g, m, v, *, name):
    R, C = w.shape
    tr = _tile(R, 512, 8)
    c1 = 1.0 / (1.0 - ADAM_B1 ** ADAM_STEP)
    c2 = 1.0 / (1.0 - ADAM_B2 ** ADAM_STEP)

    def body(w_ref, g_ref, m_ref, v_ref, d_ref, nm_ref, nv_ref):
        gv = g_ref[...]
        nm = ADAM_B1 * m_ref[...] + (1.0 - ADAM_B1) * gv
        nv = ADAM_B2 * v_ref[...] + (1.0 - ADAM_B2) * (gv * gv)
        d_ref[...] = -ADAM_LR * ((nm * c1) / (jnp.sqrt(nv * c2) + ADAM_EPS) + ADAM_WD * w_ref[...])
        nm_ref[...] = nm
        nv_ref[...] = nv

    blk = pl.BlockSpec((tr, C), lambda i: (i, 0))
    shape = jax.ShapeDtypeStruct((R, C), F32)
    return _pallas(body, name=name, grid=(R // tr,), in_specs=[blk] * 4, out_specs=[blk] * 3, out_shape=[shape] * 3,
                   compiler_params=_params("parallel"))(w, g, m, v)


def kernel(x, mem, w_in, g_mix, a_gq, a_gkv, a_wuq, a_wukv, b_lb, b_gout, c_sink, rel_bias, w_br_a, w_br_b, w_br_c, w_out, g_x, g_mem, x_wq, x_wkv, x_wo, g_ffn, f_w1, f_w3, f_w2, g_final, loss_target, m_w_in, m_g_mix, m_a_gq, m_a_gkv, m_a_wuq, m_a_wukv, m_b_lb, m_b_gout, m_c_sink, m_rel_bias, m_w_br_a, m_w_br_b, m_w_br_c, m_w_out, m_g_x, m_g_mem, m_x_wq, m_x_wkv, m_x_wo, m_g_ffn, m_f_w1, m_f_w3, m_f_w2, m_g_final, v_w_in, v_g_mix, v_a_gq, v_a_gkv, v_a_wuq, v_a_wukv, v_b_lb, v_b_gout, v_c_sink, v_rel_bias, v_w_br_a, v_w_br_b, v_w_br_c, v_w_out, v_g_x, v_g_mem, v_x_wq, v_x_wkv, v_x_wo, v_g_ffn, v_f_w1, v_f_w3, v_f_w2, v_g_final):
    ws = dict(zip(WEIGHT_NAMES, (w_in, g_mix, a_gq, a_gkv, a_wuq, a_wukv, b_lb, b_gout, c_sink, rel_bias, w_br_a, w_br_b, w_br_c,
                                 w_out, g_x, g_mem, x_wq, x_wkv, x_wo, g_ffn, f_w1, f_w3, f_w2, g_final)))
    ms = dict(zip(WEIGHT_NAMES, (m_w_in, m_g_mix, m_a_gq, m_a_gkv, m_a_wuq, m_a_wukv, m_b_lb, m_b_gout, m_c_sink, m_rel_bias,
                                 m_w_br_a, m_w_br_b, m_w_br_c, m_w_out, m_g_x, m_g_mem, m_x_wq, m_x_wkv, m_x_wo, m_g_ffn,
                                 m_f_w1, m_f_w3, m_f_w2, m_g_final)))
    vs = dict(zip(WEIGHT_NAMES, (v_w_in, v_g_mix, v_a_gq, v_a_gkv, v_a_wuq, v_a_wukv, v_b_lb, v_b_gout, v_c_sink, v_rel_bias,
                                 v_w_br_a, v_w_br_b, v_w_br_c, v_w_out, v_g_x, v_g_mem, v_x_wq, v_x_wkv, v_x_wo, v_g_ffn,
                                 v_f_w1, v_f_w3, v_f_w2, v_g_final)))
    sharded = [n for n, _ in SHARDED]
    axis_of = dict(SHARDED)

    def wire(n):
        return lax.bitcast_convert_type(ws[n], BF16) if n == "b_lb" else ws[n].astype(BF16)

    wire_shapes = [wire(n).shape for n in sharded]
    gathered = _chip_gather(_pack([wire(n) for n in sharded], PACK_COLS, PACK_ALIGN), name="gather_weights")
    per_chip = [_unpack(gathered[k], wire_shapes) for k in range(N_CHIPS)]
    full = dict(ws)
    for i, n in enumerate(sharded):
        parts = [per_chip[k][i] for k in range(N_CHIPS)]
        if n == "b_lb":
            parts = [lax.bitcast_convert_type(p, F32) for p in parts]
        full[n] = jnp.concatenate(parts, axis=axis_of[n])

    loss, grad_x, grads = _local_step(x[0], mem[0], loss_target[0], full)

    segs = []
    for k in range(N_CHIPS):
        pieces = [jnp.split(grads[n], N_CHIPS, axis=axis_of[n])[k] for n in sharded]
        segs.append(_pack(pieces, PACK_COLS, PACK_ALIGN))
    rows = segs[0].shape[0]
    g4 = jnp.stack(segs).reshape(N_CHIPS, 2, rows // 2, PACK_COLS)
    core = lax.axis_index("c").astype(jnp.int32).reshape(1)
    chip = (2 * lax.axis_index("x") + lax.axis_index("y")).astype(jnp.int32).reshape(1)
    got = _pair_swap(g4.astype(BF16), halves=True, name="reduce_pair_swap")
    pair_sum, pair_sum_wire = _pair_add(g4, got, core, name="reduce_pair_add")
    landed = _chip_scatter(pair_sum_wire, name="reduce_chip_scatter")
    mine = _chip_sum(pair_sum, landed, chip, name="reduce_chip_sum")
    g_shard = _join_halves(mine, _pair_swap(mine, halves=False, name="reduce_pair_join"), core, name="reduce_join_halves")

    small = _pack([grads[n] for n in REPLICATED] + [loss[0, 0:1]], LANES, 8 * LANES)
    small_sum = _sum_slots(_gather8(small, name="gather_small"), name="sum_small")
    small_grads = _unpack(small_sum, [ws[n].shape for n in REPLICATED] + [(1,)])
    loss_total = small_grads.pop()[0]

    shard_shapes = [ws[n].shape for n in sharded]
    pk = lambda d: _pack([d[n] for n in sharded], PACK_COLS, PACK_ALIGN)
    d_flat, m_flat, v_flat = _adamw(pk(ws), g_shard, pk(ms), pk(vs), name="adamw_sharded")
    out = {}
    for i, (n, gr, d, nm, nv) in enumerate(zip(sharded, _unpack(g_shard, shard_shapes), _unpack(d_flat, shard_shapes),
                                               _unpack(m_flat, shard_shapes), _unpack(v_flat, shard_shapes))):
        out[n] = (gr, d, nm, nv)
    pk_s = lambda d: _pack([d[n] for n in REPLICATED], LANES, 8 * LANES)
    rep_shapes = [ws[n].shape for n in REPLICATED]
    gs_flat = _pack(small_grads, LANES, 8 * LANES)
    ds_flat, ms_flat, vs_flat = _adamw(pk_s(ws), gs_flat, pk_s(ms), pk_s(vs), name="adamw_replicated")
    for n, gr, d, nm, nv in zip(REPLICATED, small_grads, _unpack(ds_flat, rep_shapes), _unpack(ms_flat, rep_shapes),
                                _unpack(vs_flat, rep_shapes)):
        out[n] = (gr, d, nm, nv)
    return (loss_total, grad_x[None], *[out[n][0] for n in WEIGHT_NAMES], *[out[n][1] for n in WEIGHT_NAMES],
            *[out[n][2] for n in WEIGHT_NAMES], *[out[n][3] for n in WEIGHT_NAMES])
```

```python
import functools
import math

import jax
import jax.numpy as jnp
from jax import lax
from jax.experimental import pallas as pl
from jax.experimental.pallas import tpu as pltpu

F32 = jnp.float32
BF16 = jnp.bfloat16

D_MODEL = 1024
DEPTH = 2
EPS = 1e-6
MASK_VALUE = -1e30
TINY = 1e-30
A_HEADS, A_NOPE, A_ROPE, A_V = 8, 64, 32, 64
A_QK = A_NOPE + A_ROPE
A_Q_RANK, A_KV_RANK = 384, 256
ROPE_THETA = 10000.0
B_HEADS, B_DK, B_DV, B_CHUNK = 8, 128, 64, 16
C_HEADS, C_KV_HEADS, C_DH, C_WINDOW, C_BLOCK = 8, 2, 64, 128, 128
REL_BUCKETS, REL_MAX_DIST = 32, 128
X_HEADS, X_DH = 4, 256
D_FF = 2816
ADAM_LR, ADAM_B1, ADAM_B2, ADAM_EPS, ADAM_WD, ADAM_STEP = 0.001, 0.9, 0.999, 1e-08, 0.01, 10

LANES = 128
VMEM_LIMIT = 56 * 1024 * 1024


def _pallas(body, **kw):
    return pl.pallas_call(body, **kw)


def _params(*sem):
    return pltpu.CompilerParams(dimension_semantics=sem, vmem_limit_bytes=VMEM_LIMIT)


def _tile(n, pref, unit=LANES):
    if n <= pref:
        return n
    t = (pref // unit) * unit
    while t > unit and n % t:
        t -= unit
    assert n % t == 0, (n, pref, unit)
    return t


def _dot(a, b, dims):
    return lax.dot_general(a, b, (dims, ((), ())), preferred_element_type=F32)


_NN = ((1,), (0,))
_NT = ((1,), (1,))
_TN = ((0,), (0,))


def _mm_tiles(M, N, K, mode):
    tm = _tile(M, 1408 if M % 1408 == 0 else 512, LANES if mode == "tn" else 8)
    tn = _tile(N, 1408 if N % 1408 == 0 else 1024, 256 if N % 256 == 0 and N % 1408 else LANES)
    tk = K if K <= 2816 else _tile(K, 1024)
    return tm, tn, tk


def _mm(a, b, *, mode="nn", add=None, out_dtype=F32, tiles=None, name):
    if mode == "nn":
        (M, K), (K2, N) = a.shape, b.shape
    elif mode == "nt":
        (M, K), (N, K2) = a.shape, b.shape
    else:
        (K, M), (K2, N) = a.shape, b.shape
    assert K == K2, (a.shape, b.shape, mode)
    tm, tn, tk = tiles or _mm_tiles(M, N, K, mode)
    nk = K // tk
    dims = {"nn": _NN, "nt": _NT, "tn": _TN}[mode]
    a_spec = pl.BlockSpec((tk, tm), lambda i, j, k: (k, i)) if mode == "tn" else pl.BlockSpec((tm, tk), lambda i, j, k: (i, k))
    b_spec = pl.BlockSpec((tn, tk), lambda i, j, k: (j, k)) if mode == "nt" else pl.BlockSpec((tk, tn), lambda i, j, k: (k, j))
    o_spec = pl.BlockSpec((tm, tn), lambda i, j, k: (i, j))
    has_add = add is not None

    def body(*refs):
        if has_add:
            a_ref, b_ref, add_ref, o_ref, acc_ref = refs
        else:
            a_ref, b_ref, o_ref, acc_ref = refs
        k = pl.program_id(2)
        part = _dot(a_ref[...].astype(BF16), b_ref[...].astype(BF16), dims)

        @pl.when(k == 0)
        def _():
            acc_ref[...] = part

        @pl.when(k > 0)
        def _():
            acc_ref[...] += part

        @pl.when(k == nk - 1)
        def _():
            r = acc_ref[...]
            if has_add:
                r = r + add_ref[...].astype(F32)
            o_ref[...] = r.astype(out_dtype)

    ins = [a, b] + ([add] if has_add else [])
    in_specs = [a_spec, b_spec] + ([o_spec] if has_add else [])
    return _pallas(
        body, name=name, grid=(M // tm, N // tn, nk), in_specs=in_specs, out_specs=o_spec,
        out_shape=jax.ShapeDtypeStruct((M, N), out_dtype), scratch_shapes=[pltpu.VMEM((tm, tn), F32)],
        compiler_params=_params("parallel", "parallel", "arbitrary"),
    )(*ins)


def _rms(x, g, *, col=0, width=None, out_dtype=BF16, name):
    T = x.shape[0]
    width = x.shape[1] if width is None else width
    assert col % width == 0
    tm = _tile(T, 512, 8)
    cb = col // width

    def body(x_ref, g_ref, o_ref):
        xv = x_ref[...].astype(F32)
        r = lax.rsqrt(jnp.mean(xv * xv, axis=-1, keepdims=True) + EPS)
        o_ref[...] = (xv * r * g_ref[...]).astype(out_dtype)

    return _pallas(
        body, name=name, grid=(T // tm,),
        in_specs=[pl.BlockSpec((tm, width), lambda i: (i, cb)), pl.BlockSpec((1, width), lambda i: (0, 0))],
        out_specs=pl.BlockSpec((tm, width), lambda i: (i, 0)),
        out_shape=jax.ShapeDtypeStruct((T, width), out_dtype), compiler_params=_params("parallel"),
    )(x, g.reshape(1, width))


def _rms_bwd(x, g, dy, *, res=None, col=0, width=None, out_dtype=F32, name):
    T = x.shape[0]
    width = x.shape[1] if width is None else width
    assert col % width == 0
    tm = _tile(T, 512, 8)
    cb = col // width
    has_res = res is not None

    def body(*refs):
        if has_res:
            x_ref, g_ref, dy_ref, res_ref, dx_ref, dg_ref = refs
        else:
            x_ref, g_ref, dy_ref, dx_ref, dg_ref = refs
        xv = x_ref[...].astype(F32)
        r = lax.rsqrt(jnp.mean(xv * xv, axis=-1, keepdims=True) + EPS)
        xh = xv * r
        dyv = dy_ref[...].astype(F32)
        dxh = dyv * g_ref[...]
        dx = r * (dxh - xh * jnp.mean(dxh * xh, axis=-1, keepdims=True))
        if has_res:
            dx = dx + res_ref[...].astype(F32)
        dx_ref[...] = dx.astype(out_dtype)
        part = jnp.sum(dyv * xh, axis=0, keepdims=True)

        @pl.when(pl.program_id(0) == 0)
        def _():
            dg_ref[...] = part

        @pl.when(pl.program_id(0) > 0)
        def _():
            dg_ref[...] += part

    row = pl.BlockSpec((tm, width), lambda i: (i, 0))
    ins = [x, g.reshape(1, width), dy] + ([res] if has_res else [])
    in_specs = [pl.BlockSpec((tm, width), lambda i: (i, cb)), pl.BlockSpec((1, width), lambda i: (0, 0)), row] + ([row] if has_res else [])
    return _pallas(
        body, name=name, grid=(T // tm,), in_specs=in_specs,
        out_specs=[row, pl.BlockSpec((1, width), lambda i: (0, 0))],
        out_shape=[jax.ShapeDtypeStruct((T, width), out_dtype), jax.ShapeDtypeStruct((1, width), F32)],
        compiler_params=_params("arbitrary"),
    )(*ins)


def _rope_tables(T):
    half = A_ROPE // 2
    inv = ROPE_THETA ** (-jnp.arange(half, dtype=F32) / half)
    ang = jnp.arange(T, dtype=jnp.int32).astype(F32)[:, None] * inv[None, :]
    c32 = jnp.concatenate([jnp.cos(ang), jnp.cos(ang)], axis=-1)
    s32 = jnp.concatenate([jnp.sin(ang), jnp.sin(ang)], axis=-1)
    pad = A_PAD - A_QK
    cq = jnp.tile(jnp.concatenate([jnp.ones((T, A_NOPE), F32), c32, jnp.ones((T, pad), F32)], axis=-1), (1, A_HEADS))
    sq = jnp.tile(jnp.concatenate([jnp.zeros((T, A_NOPE), F32), s32, jnp.zeros((T, pad), F32)], axis=-1), (1, A_HEADS))
    ck = jnp.concatenate([c32, s32, jnp.zeros((T, LANES - 2 * A_ROPE), F32)], axis=-1)
    ck_t = jnp.concatenate([c32, s32], axis=-1).T
    return cq, sq, ck, ck_t


def _rope_swap_cols(w):
    half = A_ROPE // 2
    return jnp.concatenate([-w[..., half:], w[..., :half]], axis=-1)


def _rope_unswap_cols(g):
    half = A_ROPE // 2
    return jnp.concatenate([g[..., half:], -g[..., :half]], axis=-1)


A_PAD = LANES
A_W = A_HEADS * A_PAD
LOG2E = 1.4426950408889634
LN2 = 0.6931471805599453
Q_SCALE = A_QK ** -0.5 * LOG2E


def _qrope(q2, cq, sq, *, name):
    T = q2.shape[0]
    W = A_W
    tm = _tile(T, 512, 8)

    def body(a_ref, b_ref, c_ref, s_ref, o_ref):
        o_ref[...] = ((a_ref[...] * c_ref[...] + b_ref[...] * s_ref[...]) * Q_SCALE).astype(BF16)

    blk = lambda j: pl.BlockSpec((tm, W), lambda i: (i, j))
    return _pallas(body, name=name, grid=(T // tm,), in_specs=[blk(0), blk(1), blk(0), blk(0)], out_specs=blk(0),
                   out_shape=jax.ShapeDtypeStruct((T, W), BF16), compiler_params=_params("parallel"))(q2, q2, cq, sq)


def _qrope_bwd(dq, cq, sq, *, name):
    T = dq.shape[0]
    W = A_W
    tm = _tile(T, 512, 8)

    def body(d_ref, c_ref, s_ref, o_ref):
        d = d_ref[...]
        o_ref[:, 0:W] = (d * c_ref[...]).astype(BF16)
        o_ref[:, W:2 * W] = (d * s_ref[...]).astype(BF16)

    blk = pl.BlockSpec((tm, W), lambda i: (i, 0))
    return _pallas(body, name=name, grid=(T // tm,), in_specs=[blk, blk, blk],
                   out_specs=pl.BlockSpec((tm, 2 * W), lambda i: (i, 0)),
                   out_shape=jax.ShapeDtypeStruct((T, 2 * W), BF16), compiler_params=_params("parallel"))(dq, cq, sq)


def _kprep(kv, za, ck, *, name):
    T = kv.shape[0]
    tm = _tile(T, 512, 8)

    def body(kv_ref, kr_ref, ck_ref, k_ref, vx_ref):
        t = kr_ref[...] * ck_ref[...]
        krope = (t[:, 0:A_ROPE] + t[:, A_ROPE:2 * A_ROPE]).astype(BF16)
        one = (lax.broadcasted_iota(jnp.int32, (tm, A_PAD - A_V), 1) == 0).astype(BF16)
        for h in range(A_HEADS):
            k_ref[:, A_PAD * h:A_PAD * h + A_NOPE] = kv_ref[:, A_NOPE * h:A_NOPE * (h + 1)]
            k_ref[:, A_PAD * h + A_NOPE:A_PAD * h + A_QK] = krope
            k_ref[:, A_PAD * h + A_QK:A_PAD * (h + 1)] = jnp.zeros((tm, A_PAD - A_QK), BF16)
            vx_ref[:, A_PAD * h:A_PAD * h + A_V] = kv_ref[:, 512 + A_V * h:512 + A_V * (h + 1)]
            vx_ref[:, A_PAD * h + A_V:A_PAD * (h + 1)] = one

    wide = pl.BlockSpec((tm, A_W), lambda i: (i, 0))
    return _pallas(
        body, name=name, grid=(T // tm,),
        in_specs=[wide, pl.BlockSpec((tm, LANES), lambda i: (i, 3)), pl.BlockSpec((tm, LANES), lambda i: (i, 0))],
        out_specs=[wide, wide], out_shape=[jax.ShapeDtypeStruct((T, A_W), BF16)] * 2,
        compiler_params=_params("parallel"))(kv, za, ck)


def _kprep_bwd(dkt, ck_t, *, name):
    T = dkt.shape[1]
    tc = _tile(T, 512)

    def body(dk_ref, ck_ref, dn_ref, dr_ref):
        acc = jnp.zeros((A_ROPE, tc), F32)
        for h in range(A_HEADS):
            dn_ref[A_NOPE * h:A_NOPE * (h + 1), :] = dk_ref[A_PAD * h:A_PAD * h + A_NOPE, :].astype(BF16)
            acc = acc + dk_ref[A_PAD * h + A_NOPE:A_PAD * h + A_QK, :]
        dr_ref[0:A_ROPE, :] = (acc * ck_ref[0:A_ROPE, :]).astype(BF16)
        dr_ref[A_ROPE:2 * A_ROPE, :] = (acc * ck_ref[A_ROPE:2 * A_ROPE, :]).astype(BF16)
        dr_ref[2 * A_ROPE:LANES, :] = jnp.zeros((LANES - 2 * A_ROPE, tc), BF16)

    col = lambda r: pl.BlockSpec((r, tc), lambda i: (0, i))
    return _pallas(
        body, name=name, grid=(T // tc,), in_specs=[col(A_W), col(2 * A_ROPE)], out_specs=[col(512), col(LANES)],
        out_shape=[jax.ShapeDtypeStruct((512, T), BF16), jax.ShapeDtypeStruct((LANES, T), BF16)],
        compiler_params=_params("parallel"))(dkt, ck_t)


def _flash_fwd(qs, k, vxt, *, name):
    T = qs.shape[0]
    tq, tk = _tile(T, 512), _tile(T, 2048)
    nk = T // tk
    H, P, DV = A_HEADS, A_PAD, A_V

    def body(q_ref, k_ref, v_ref, o_ref, lse_ref, m_sc, acc_sc):
        j = pl.program_id(1)

        @pl.when(j == 0)
        def _():
            m_sc[...] = jnp.full(m_sc.shape, -jnp.inf, F32)
            acc_sc[...] = jnp.zeros(acc_sc.shape, F32)

        def scores(h):
            return _dot(k_ref[:, P * h:P * (h + 1)], q_ref[:, P * h:P * (h + 1)], _NT)

        st_next = scores(0)
        for h in range(H):
            st = st_next
            if h + 1 < H:
                st_next = scores(h + 1)
            m_prev = m_sc[h]
            m_new = jnp.maximum(m_prev, jnp.max(st, axis=0, keepdims=True))
            pt = jnp.exp2(st - m_new).astype(BF16)
            acc_sc[h] = jnp.exp2(m_prev - m_new) * acc_sc[h] + _dot(v_ref[P * h:P * (h + 1), :], pt, _NN)
            m_sc[h] = m_new

        @pl.when(j == nk - 1)
        def _():
            for h in range(H):
                acc = acc_sc[h]
                l = acc[DV:DV + 1, :]
                o_ref[:, DV * h:DV * (h + 1)] = (acc[0:DV, :] / l).T
                lse_ref[h] = m_sc[h] + jnp.log2(l)

    return _pallas(
        body, name=name, grid=(T // tq, nk),
        in_specs=[pl.BlockSpec((tq, A_W), lambda i, j: (i, 0)), pl.BlockSpec((tk, A_W), lambda i, j: (j, 0)),
                  pl.BlockSpec((A_W, tk), lambda i, j: (0, j))],
        out_specs=[pl.BlockSpec((tq, H * DV), lambda i, j: (i, 0)), pl.BlockSpec((H, 1, tq), lambda i, j: (0, 0, i))],
        out_shape=[jax.ShapeDtypeStruct((T, H * DV), F32), jax.ShapeDtypeStruct((H, 1, T), F32)],
        scratch_shapes=[pltpu.VMEM((H, 1, tq), F32), pltpu.VMEM((H, P, tq), F32)],
        compiler_params=_params("parallel", "arbitrary"))(qs, k, vxt)


def _attn_delta(o, do, *, name):
    T = o.shape[0]
    tm = _tile(T, 512, 8)

    def body(o_ref, do_ref, d_ref):
        prod = o_ref[...] * do_ref[...].astype(F32)
        for h in range(A_HEADS):
            d_ref[h] = jnp.sum(prod[:, A_V * h:A_V * (h + 1)], axis=-1, keepdims=True)

    row = pl.BlockSpec((tm, A_HEADS * A_V), lambda i: (i, 0))
    return _pallas(body, name=name, grid=(T // tm,), in_specs=[row, row],
                   out_specs=pl.BlockSpec((A_HEADS, tm, 1), lambda i: (0, i, 0)),
                   out_shape=jax.ShapeDtypeStruct((A_HEADS, T, 1), F32), compiler_params=_params("parallel"))(o, do)


def _flash_bwd(qs, qst, k, kv, do, dot_, lse2, delta, *, tiles=None, name):
    T = qs.shape[0]
    tq, tk = tiles or (_tile(T, 512), _tile(T, 1024))
    nq, nk = T // tq, T // tk
    H, P, DV = A_HEADS, A_PAD, A_V

    def body(q_ref, qt_ref, k_ref, v_ref, do_ref, dot_ref, lse_ref, delta_ref, dq_ref, dkt_ref, dvt_ref, dkt_sc, dvt_sc):
        i = pl.program_id(1)

        @pl.when(i == 0)
        def _():
            dkt_sc[...] = jnp.zeros(dkt_sc.shape, F32)
            dvt_sc[...] = jnp.zeros(dvt_sc.shape, F32)

        def products(h):
            s = _dot(q_ref[:, P * h:P * (h + 1)], k_ref[:, P * h:P * (h + 1)], _NT)
            dp = _dot(do_ref[:, DV * h:DV * (h + 1)], v_ref[:, DV * h:DV * (h + 1)], _NT)
            return s, dp

        nxt = products(0)
        for h in range(H):
            s, dp = nxt
            if h + 1 < H:
                nxt = products(h + 1)
            p = jnp.exp2(s - lse_ref[h])
            ds = (p * (dp - delta_ref[h])).astype(BF16)
            pb = p.astype(BF16)
            dq_ref[0, :, P * h:P * (h + 1)] = _dot(ds, k_ref[:, P * h:P * (h + 1)], _NN)
            dkt_sc[h] += _dot(qt_ref[P * h:P * (h + 1), :], ds, _NN)
            dvt_sc[h] += _dot(dot_ref[DV * h:DV * (h + 1), :], pb, _NN)

        @pl.when(i == nq - 1)
        def _():
            for h in range(H):
                dkt_ref[P * h:P * (h + 1), :] = dkt_sc[h] * LN2
                dvt_ref[DV * h:DV * (h + 1), :] = dvt_sc[h].astype(BF16)

    qrow = lambda w: pl.BlockSpec((tq, w), lambda j, i: (i, 0))
    qcol = lambda r: pl.BlockSpec((r, tq), lambda j, i: (0, i))
    stat = pl.BlockSpec((H, tq, 1), lambda j, i: (0, i, 0))
    return _pallas(
        body, name=name, grid=(nk, nq),
        in_specs=[qrow(A_W), qcol(A_W), pl.BlockSpec((tk, A_W), lambda j, i: (j, 0)), pl.BlockSpec((tk, H * DV), lambda j, i: (j, 1)),
                  qrow(H * DV), qcol(H * DV), stat, stat],
        out_specs=[pl.BlockSpec((1, tq, A_W), lambda j, i: (j, i, 0)), pl.BlockSpec((A_W, tk), lambda j, i: (0, j)),
                   pl.BlockSpec((H * DV, tk), lambda j, i: (0, j))],
        out_shape=[jax.ShapeDtypeStruct((nk, T, A_W), F32), jax.ShapeDtypeStruct((A_W, T), F32),
                   jax.ShapeDtypeStruct((H * DV, T), BF16)],
        scratch_shapes=[pltpu.VMEM((H, P, tk), F32), pltpu.VMEM((H, DV, tk), F32)],
        compiler_params=_params("parallel", "arbitrary"))(qs, qst, k, kv, do, dot_, lse2, delta)


def _dq_sum(dq_part, *, name):
    n, T, W = dq_part.shape
    tm = _tile(T, 128, 8)

    def body(p_ref, o_ref):
        acc = p_ref[0]
        for j in range(1, n):
            acc = acc + p_ref[j]
        o_ref[...] = acc * (A_QK ** -0.5)

    return _pallas(body, name=name, grid=(T // tm,), in_specs=[pl.BlockSpec((n, tm, W), lambda i: (0, i, 0))],
                   out_specs=pl.BlockSpec((tm, W), lambda i: (i, 0)), out_shape=jax.ShapeDtypeStruct((T, W), F32),
                   compiler_params=_params("parallel"))(dq_part)


HB = 8 * B_CHUNK


def _chunk_masks(reverse):
    r = lax.broadcasted_iota(jnp.int32, (HB, HB), 0)
    c = lax.broadcasted_iota(jnp.int32, (HB, HB), 1)
    same = (r // B_CHUNK) == (c // B_CHUNK)
    incl = same & ((c >= r) if reverse else (c <= r))
    return same, incl


def _split3(x):
    hi = x.astype(BF16)
    r1 = x - hi.astype(F32)
    mid = r1.astype(BF16)
    lo = (r1 - mid.astype(F32)).astype(BF16)
    return hi, mid, lo


def _mask_mm(mask, x):
    hi, mid, lo = _split3(x)
    return _dot(mask, hi, _NN) + _dot(mask, mid, _NN) + _dot(mask, lo, _NN)


def _hgrn_gates(q, z, lb, reverse):
    same, incl = _chunk_masks(reverse)
    sg = jax.nn.sigmoid(z)
    f = lb + (1.0 - lb) * sg
    lf = jnp.log(jnp.maximum(f, TINY))
    kk = (1.0 - lb) * jax.nn.sigmoid(-z)
    b = _mask_mm(incl.astype(BF16), lf)
    btot = _mask_mm(same.astype(BF16), lf)
    eb, enb, er, dec = jnp.exp(b), jnp.exp(-b), jnp.exp(btot - b), jnp.exp(btot)
    return dict(same=same, incl=incl, sg=sg, f=f, kk=kk, eb=eb, enb=enb, er=er, dec=dec,
                qd=q * eb, ki=kk * enb, ke=kk * er)


def _hgrn_specs(T, reverse, gate_reverse):
    nb = T // HB
    blk = (lambda i: nb - 1 - i) if reverse else (lambda i: i)
    wide = B_HEADS * B_DK
    return nb, blk, [
        pl.BlockSpec((HB, wide), lambda i: (blk(i), 0)),
        pl.BlockSpec((HB, wide), lambda i: (blk(i), 2 if gate_reverse else 1)),
        pl.BlockSpec((HB, B_HEADS * B_DV), lambda i: (blk(i), 6)),
        pl.BlockSpec((1, wide), lambda i: (0, 0)),
    ]


def _hk(h):
    return slice(B_DK * h, B_DK * (h + 1))


def _hv(h):
    return slice(B_DV * h, B_DV * (h + 1))


def _crows(c):
    return slice(B_CHUNK * c, B_CHUNK * (c + 1))


def _hgrn_fwd(zb, lb, *, reverse, name):
    T = zb.shape[0]
    nb, blk, in_specs = _hgrn_specs(T, reverse, reverse)
    order = range(7, -1, -1) if reverse else range(8)
    heads = range(B_HEADS)

    def body(q_ref, z_ref, v_ref, lb_ref, o_ref, st_ref, s_sc):
        @pl.when(pl.program_id(0) == 0)
        def _():
            s_sc[...] = jnp.zeros(s_sc.shape, F32)

        g = _hgrn_gates(q_ref[...], z_ref[...], lb_ref[...], reverse)
        v = v_ref[...].astype(BF16)
        qd, ki, ke = g["qd"].astype(BF16), g["ki"].astype(BF16), g["ke"].astype(BF16)
        dec = g["dec"]
        o_intra = []
        for h in heads:
            a = jnp.where(g["incl"], _dot(qd[:, _hk(h)], ki[:, _hk(h)], _NT), 0.0)
            o_intra.append(_dot(a.astype(BF16), v[:, _hv(h)], _NN))
        upd = [[_dot(v[_crows(c), _hv(h)], ke[_crows(c), _hk(h)], _TN) for c in range(8)] for h in heads]
        st = [s_sc[h] for h in heads]
        snap = [[None] * 8 for _ in heads]
        for c in order:
            for h in heads:
                snap[h][c] = st[h]
                st[h] = st[h] * dec[B_CHUNK * c:B_CHUNK * c + 1, _hk(h)] + upd[h][c]
        for h in heads:
            s_sc[h] = st[h]
            for c in range(8):
                st_ref[h, c] = snap[h][c]
            inter = [_dot(qd[_crows(c), _hk(h)], snap[h][c].astype(BF16), _NT) for c in range(8)]
            o_ref[h] = o_intra[h] + jnp.concatenate(inter, axis=0)

    return _pallas(
        body, name=name, grid=(nb,), in_specs=in_specs,
        out_specs=[pl.BlockSpec((B_HEADS, HB, B_DV), lambda i: (0, blk(i), 0)),
                   pl.BlockSpec((B_HEADS, 8, B_DV, B_DK), lambda i: (0, blk(i), 0, 0))],
        out_shape=[jax.ShapeDtypeStruct((B_HEADS, T, B_DV), F32),
                   jax.ShapeDtypeStruct((B_HEADS, T // B_CHUNK, B_DV, B_DK), F32)],
        scratch_shapes=[pltpu.VMEM((B_HEADS, B_DV, B_DK), F32)],
        compiler_params=_params("arbitrary"))(zb, zb, zb, lb)


def _hgrn_bwd(zb, lb, do, states, *, reverse, name):
    T = zb.shape[0]
    nb, blk, in_specs = _hgrn_specs(T, not reverse, reverse)
    order = range(8) if reverse else range(7, -1, -1)
    heads = range(B_HEADS)

    def body(q_ref, z_ref, v_ref, lb_ref, do_ref, st_ref, dq_ref, dz_ref, dv_ref, dlb_ref, ds_sc):
        @pl.when(pl.program_id(0) == 0)
        def _():
            ds_sc[...] = jnp.zeros(ds_sc.shape, F32)
            dlb_ref[...] = jnp.zeros(dlb_ref.shape, F32)

        lb = lb_ref[...]
        g = _hgrn_gates(q_ref[...], z_ref[...], lb, reverse)
        v = v_ref[...].astype(BF16)
        qd, ki, ke = g["qd"].astype(BF16), g["ki"].astype(BF16), g["ke"].astype(BF16)
        dec = g["dec"]
        dout = [do_ref[h].astype(BF16) for h in heads]
        dv_i, dqd_i, dki = [], [], []
        for h in heads:
            a = jnp.where(g["incl"], _dot(qd[:, _hk(h)], ki[:, _hk(h)], _NT), 0.0).astype(BF16)
            da = jnp.where(g["incl"], _dot(dout[h], v[:, _hv(h)], _NT), 0.0).astype(BF16)
            dv_i.append(_dot(a, dout[h], _TN))
            dqd_i.append(_dot(da, ki[:, _hk(h)], _NN))
            dki.append(_dot(da, qd[:, _hk(h)], _TN))
        upd = [[_dot(dout[h][_crows(c)], qd[_crows(c), _hk(h)], _TN) for c in range(8)] for h in heads]
        dqd_p = [[_dot(dout[h][_crows(c)], st_ref[h, c].astype(BF16), _NN) for c in range(8)] for h in heads]
        dst = [ds_sc[h] for h in heads]
        used = [[None] * 8 for _ in heads]
        for c in order:
            for h in heads:
                used[h][c] = dst[h]
                dst[h] = dst[h] * dec[B_CHUNK * c:B_CHUNK * c + 1, _hk(h)] + upd[h][c]
        dqd_h, dke_h, ddec_h = [], [], []
        for h in heads:
            ds_sc[h] = dst[h]
            dv_p, dke_p, ddec_p = [], [], []
            for c in range(8):
                d = used[h][c]
                db16 = d.astype(BF16)
                dv_p.append(_dot(ke[_crows(c), _hk(h)], db16, _NT))
                dke_p.append(_dot(v[_crows(c), _hv(h)], db16, _NN))
                tot = jnp.sum(d * st_ref[h, c], axis=0, keepdims=True) * dec[B_CHUNK * c:B_CHUNK * c + 1, _hk(h)]
                ddec_p.append(jnp.broadcast_to(tot, (B_CHUNK, B_DK)))
            dv_ref[h] = dv_i[h] + jnp.concatenate(dv_p, axis=0)
            dqd_h.append(dqd_i[h] + jnp.concatenate(dqd_p[h], axis=0))
            dke_h.append(jnp.concatenate(dke_p, axis=0))
            ddec_h.append(jnp.concatenate(ddec_p, axis=0))
        dqd = jnp.concatenate(dqd_h, axis=1)
        dki = jnp.concatenate(dki, axis=1)
        dke = jnp.concatenate(dke_h, axis=1)
        db = dqd * g["qd"] - dki * g["ki"] - dke * g["ke"]
        _, incl_t = _chunk_masks(not reverse)
        dlf = (_mask_mm(incl_t.astype(BF16), db) + _mask_mm(g["same"].astype(BF16), dke * g["ke"])
               + jnp.concatenate(ddec_h, axis=1))
        dk = dki * g["enb"] + dke * g["er"]
        u = jnp.where(g["f"] > TINY, dlf / g["f"], 0.0) - dk
        sg = g["sg"]
        dq_ref[...] = dqd * g["eb"]
        dz_ref[...] = u * (1.0 - lb) * sg * (1.0 - sg)
        dlb_ref[...] += jnp.sum(u * (1.0 - sg), axis=0, keepdims=True)

    wide = pl.BlockSpec((HB, B_HEADS * B_DK), lambda i: (blk(i), 0))
    hm = pl.BlockSpec((B_HEADS, HB, B_DV), lambda i: (0, blk(i), 0))
    return _pallas(
        body, name=name, grid=(nb,),
        in_specs=in_specs + [hm, pl.BlockSpec((B_HEADS, 8, B_DV, B_DK), lambda i: (0, blk(i), 0, 0))],
        out_specs=[wide, wide, hm, pl.BlockSpec((1, B_HEADS * B_DK), lambda i: (0, 0))],
        out_shape=[jax.ShapeDtypeStruct((T, B_HEADS * B_DK), F32), jax.ShapeDtypeStruct((T, B_HEADS * B_DK), F32),
                   jax.ShapeDtypeStruct((B_HEADS, T, B_DV), F32), jax.ShapeDtypeStruct((1, B_HEADS * B_DK), F32)],
        scratch_shapes=[pltpu.VMEM((B_HEADS, B_DV, B_DK), F32)],
        compiler_params=_params("arbitrary"))(zb, zb, zb, lb, do, states)


def _hgrn_out(of, ob, zb, gout, *, name):
    T = zb.shape[0]
    tm = _tile(T, 512, 8)

    def body(of_ref, ob_ref, g_ref, gout_ref, y_ref):
        for h in range(B_HEADS):
            o = of_ref[h] + ob_ref[h]
            r = lax.rsqrt(jnp.mean(o * o, axis=-1, keepdims=True) + EPS)
            gh = g_ref[:, B_DV * h:B_DV * (h + 1)]
            y_ref[:, B_DV * h:B_DV * (h + 1)] = (o * r * gout_ref[...] * (gh * jax.nn.sigmoid(gh))).astype(BF16)

    hm = pl.BlockSpec((B_HEADS, tm, B_DV), lambda i: (0, i, 0))
    return _pallas(
        body, name=name, grid=(T // tm,),
        in_specs=[hm, hm, pl.BlockSpec((tm, 512), lambda i: (i, 7)), pl.BlockSpec((1, B_DV), lambda i: (0, 0))],
        out_specs=pl.BlockSpec((tm, 512), lambda i: (i, 0)),
        out_shape=jax.ShapeDtypeStruct((T, 512), BF16), compiler_params=_params("parallel"))(of, ob, zb, gout.reshape(1, B_DV))


def _hgrn_out_bwd(of, ob, zb, gout, dy, *, name):
    T = zb.shape[0]
    tm = _tile(T, 512, 8)

    def body(of_ref, ob_ref, g_ref, gout_ref, dy_ref, do_ref, dg_ref, dgo_ref):
        gout_v = gout_ref[...]
        acc = jnp.zeros((1, B_DV), F32)
        for h in range(B_HEADS):
            o = of_ref[h] + ob_ref[h]
            r = lax.rsqrt(jnp.mean(o * o, axis=-1, keepdims=True) + EPS)
            oh = o * r
            gh = g_ref[:, B_DV * h:B_DV * (h + 1)]
            sg = jax.nn.sigmoid(gh)
            dyh = dy_ref[:, B_DV * h:B_DV * (h + 1)].astype(F32)
            dn = dyh * (gh * sg)
            dg_ref[:, B_DV * h:B_DV * (h + 1)] = dyh * (oh * gout_v) * (sg * (1.0 + gh * (1.0 - sg)))
            dxh = dn * gout_v
            do_ref[h] = r * (dxh - oh * jnp.mean(dxh * oh, axis=-1, keepdims=True))
            acc = acc + jnp.sum(dn * oh, axis=0, keepdims=True)

        @pl.when(pl.program_id(0) == 0)
        def _():
            dgo_ref[...] = acc

        @pl.when(pl.program_id(0) > 0)
        def _():
            dgo_ref[...] += acc

    hm = pl.BlockSpec((B_HEADS, tm, B_DV), lambda i: (0, i, 0))
    row = pl.BlockSpec((tm, 512), lambda i: (i, 0))
    return _pallas(
        body, name=name, grid=(T // tm,),
        in_specs=[hm, hm, pl.BlockSpec((tm, 512), lambda i: (i, 7)), pl.BlockSpec((1, B_DV), lambda i: (0, 0)), row],
        out_specs=[hm, row, pl.BlockSpec((1, B_DV), lambda i: (0, 0))],
        out_shape=[jax.ShapeDtypeStruct((B_HEADS, T, B_DV), F32), jax.ShapeDtypeStruct((T, 512), F32),
                   jax.ShapeDtypeStruct((1, B_DV), F32)],
        compiler_params=_params("arbitrary"))(of, ob, zb, gout.reshape(1, B_DV), dy)


def _dzb_assemble(dq_f, dq_b, dzf, dzb_, dv_f, dv_b, dgate, *, name):
    T = dq_f.shape[0]
    tm = _tile(T, 256, 8)

    def body(qf, qb, zf, zr, vf, vr, dg, o_ref):
        o_ref[:, 0:1024] = (qf[...] + qb[...]).astype(BF16)
        o_ref[:, 1024:2048] = zf[...].astype(BF16)
        o_ref[:, 2048:3072] = zr[...].astype(BF16)
        for h in range(B_HEADS):
            o_ref[:, 3072 + B_DV * h:3072 + B_DV * (h + 1)] = (vf[h] + vr[h]).astype(BF16)
        o_ref[:, 3584:4096] = dg[...].astype(BF16)

    wide = pl.BlockSpec((tm, 1024), lambda i: (i, 0))
    hm = pl.BlockSpec((B_HEADS, tm, B_DV), lambda i: (0, i, 0))
    return _pallas(
        body, name=name, grid=(T // tm,),
        in_specs=[wide, wide, wide, wide, hm, hm, pl.BlockSpec((tm, 512), lambda i: (i, 0))],
        out_specs=pl.BlockSpec((tm, 4096), lambda i: (i, 0)),
        out_shape=jax.ShapeDtypeStruct((T, 4096), BF16), compiler_params=_params("parallel"))(dq_f, dq_b, dzf, dzb_, dv_f, dv_b, dgate)


C_SPAN = 3 * C_BLOCK
C_G = C_HEADS // C_KV_HEADS


def _t5_bucket(rel):
    nb = REL_BUCKETS // 2
    max_exact = nb // 2
    ret = (rel > 0).astype(jnp.int32) * nb
    n = jnp.abs(rel)
    large = max_exact + (jnp.log(jnp.maximum(n, 1).astype(F32) / max_exact)
                         / math.log(REL_MAX_DIST / max_exact) * (nb - max_exact)).astype(jnp.int32)
    large = jnp.minimum(large, nb - 1)
    return ret + jnp.where(n < max_exact, n, large)


def _swa_buckets():
    rel = jnp.arange(C_SPAN)[None, :] - C_BLOCK - jnp.arange(C_BLOCK)[:, None]
    return _t5_bucket(rel)


def _swa_specs(T):
    nb = T // C_BLOCK
    return nb, [
        pl.BlockSpec((C_BLOCK, 512), lambda n: (n, 0)),
        pl.BlockSpec((C_BLOCK, LANES), lambda n: (jnp.maximum(n - 1, 0), 4)),
        pl.BlockSpec((C_BLOCK, LANES), lambda n: (n, 4)),
        pl.BlockSpec((C_BLOCK, LANES), lambda n: (jnp.minimum(n + 1, nb - 1), 4)),
        pl.BlockSpec((C_BLOCK, LANES), lambda n: (jnp.maximum(n - 1, 0), 5)),
        pl.BlockSpec((C_BLOCK, LANES), lambda n: (n, 5)),
        pl.BlockSpec((C_BLOCK, LANES), lambda n: (jnp.minimum(n + 1, nb - 1), 5)),
        pl.BlockSpec((C_HEADS, C_BLOCK, C_SPAN), lambda n: (0, 0, 0)),
        pl.BlockSpec(memory_space=pltpu.SMEM),
    ]


def _swa_valid(n, T):
    qi = lax.broadcasted_iota(jnp.int32, (C_BLOCK, C_SPAN), 0)
    si = lax.broadcasted_iota(jnp.int32, (C_BLOCK, C_SPAN), 1)
    rel = si - C_BLOCK - qi
    kpos = (n - 1) * C_BLOCK + si
    return (jnp.abs(rel) <= C_WINDOW) & (kpos >= 0) & (kpos < T)


def _swa_softmax(raw, bias, valid, sink):
    s = raw * (C_DH ** -0.5) + bias
    s = jnp.where(valid, s, MASK_VALUE)
    m = jnp.maximum(jnp.max(s, axis=-1, keepdims=True), sink)
    e = jnp.exp(s - m)
    den = jnp.sum(e, axis=-1, keepdims=True) + jnp.exp(sink - m)
    return e / den, jnp.exp(sink - m) / den


def _swa_fwd(zc, bias, sink, *, name):
    T = zc.shape[0]
    nb, in_specs = _swa_specs(T)

    def body(q_ref, kp, kc, kn, vp, vc, vn, bias_ref, sink_ref, y_ref):
        n = pl.program_id(0)
        kcat = jnp.concatenate([kp[...], kc[...], kn[...]], axis=0)
        vcat = jnp.concatenate([vp[...], vc[...], vn[...]], axis=0)
        valid = _swa_valid(n, T)
        heads = range(C_HEADS)
        kvs = [slice(C_DH * (h // C_G), C_DH * (h // C_G + 1)) for h in heads]
        scores = [_dot(q_ref[:, C_DH * h:C_DH * (h + 1)], kcat[:, kvs[h]], _NT) for h in heads]
        probs = [_swa_softmax(scores[h], bias_ref[h], valid, sink_ref[h])[0].astype(BF16) for h in heads]
        for h in heads:
            y_ref[:, C_DH * h:C_DH * (h + 1)] = _dot(probs[h], vcat[:, kvs[h]], _NN).astype(BF16)

    return _pallas(
        body, name=name, grid=(nb,), in_specs=in_specs, out_specs=pl.BlockSpec((C_BLOCK, 512), lambda n: (n, 0)),
        out_shape=jax.ShapeDtypeStruct((T, 512), BF16), compiler_params=_params("parallel"))(zc, zc, zc, zc, zc, zc, zc, bias, sink)


def _swa_bwd(zc, bias, sink, dy, *, name):
    T = zc.shape[0]
    nb, in_specs = _swa_specs(T)
    scale = C_DH ** -0.5

    def body(q_ref, kp, kc, kn, vp, vc, vn, bias_ref, sink_ref, dy_ref, dq_ref, dkc_ref, dvc_ref, dbias_ref, dsink_ref):
        n = pl.program_id(0)

        @pl.when(n == 0)
        def _():
            dbias_ref[...] = jnp.zeros(dbias_ref.shape, F32)
            dsink_ref[...] = jnp.zeros(dsink_ref.shape, F32)

        kcat = jnp.concatenate([kp[...], kc[...], kn[...]], axis=0)
        vcat = jnp.concatenate([vp[...], vc[...], vn[...]], axis=0)
        valid = _swa_valid(n, T)
        heads = range(C_HEADS)
        kvs = [slice(C_DH * (h // C_G), C_DH * (h // C_G + 1)) for h in heads]
        qs = [q_ref[:, C_DH * h:C_DH * (h + 1)] for h in heads]
        dos = [dy_ref[:, C_DH * h:C_DH * (h + 1)].astype(BF16) for h in heads]
        scores = [_dot(qs[h], kcat[:, kvs[h]], _NT) for h in heads]
        dps = [_dot(dos[h], vcat[:, kvs[h]], _NT) for h in heads]
        pbs, dsbs = [], []
        for h in heads:
            p, p_sink = _swa_softmax(scores[h], bias_ref[h], valid, sink_ref[h])
            rowdot = jnp.sum(p * dps[h], axis=-1, keepdims=True)
            ds = p * (dps[h] - rowdot)
            dbias_ref[h] += ds
            tot = jnp.sum(jnp.sum(-p_sink * rowdot, axis=0, keepdims=True), axis=1, keepdims=True)
            dsink_ref[h:h + 1, :] += jnp.broadcast_to(tot, (1, LANES))
            pbs.append(p.astype(BF16))
            dsbs.append((ds * scale).astype(BF16))
        for h in heads:
            dq_ref[:, C_DH * h:C_DH * (h + 1)] = _dot(dsbs[h], kcat[:, kvs[h]], _NN).astype(BF16)
        dks = [_dot(dsbs[h], qs[h], _TN) for h in heads]
        dvs = [_dot(pbs[h], dos[h], _TN) for h in heads]
        for kv in range(C_KV_HEADS):
            group = range(kv * C_G, (kv + 1) * C_G)
            dkc_ref[0, :, C_DH * kv:C_DH * (kv + 1)] = sum(dks[h] for h in group)
            dvc_ref[0, :, C_DH * kv:C_DH * (kv + 1)] = sum(dvs[h] for h in group)

    part = pl.BlockSpec((1, C_SPAN, LANES), lambda n: (n, 0, 0))
    dq, dkc, dvc, dbias, dsink = _pallas(
        body, name=name, grid=(nb,), in_specs=in_specs + [pl.BlockSpec((C_BLOCK, 512), lambda n: (n, 0))],
        out_specs=[pl.BlockSpec((C_BLOCK, 512), lambda n: (n, 0)), part, part,
                   pl.BlockSpec((C_HEADS, C_BLOCK, C_SPAN), lambda n: (0, 0, 0)), pl.BlockSpec((C_HEADS, LANES), lambda n: (0, 0))],
        out_shape=[jax.ShapeDtypeStruct((T, 512), BF16), jax.ShapeDtypeStruct((nb, C_SPAN, LANES), F32),
                   jax.ShapeDtypeStruct((nb, C_SPAN, LANES), F32), jax.ShapeDtypeStruct((C_HEADS, C_BLOCK, C_SPAN), F32),
                   jax.ShapeDtypeStruct((C_HEADS, LANES), F32)],
        compiler_params=_params("arbitrary"))(zc, zc, zc, zc, zc, zc, zc, bias, sink, dy)

    def combine(dq_ref, kp, kc, kn, vp, vc, vn, o_ref):
        n = pl.program_id(0)
        lo = (n > 0).astype(F32)
        hi = (n < nb - 1).astype(F32)
        o_ref[:, 0:512] = dq_ref[...]
        o_ref[:, 512:640] = (kp[0] * lo + kc[0] + kn[0] * hi).astype(BF16)
        o_ref[:, 640:768] = (vp[0] * lo + vc[0] + vn[0] * hi).astype(BF16)

    prev = pl.BlockSpec((1, C_BLOCK, LANES), lambda n: (jnp.maximum(n - 1, 0), 2, 0))
    cur = pl.BlockSpec((1, C_BLOCK, LANES), lambda n: (n, 1, 0))
    nxt = pl.BlockSpec((1, C_BLOCK, LANES), lambda n: (jnp.minimum(n + 1, nb - 1), 0, 0))
    dzc = _pallas(
        combine, name=name + "_combine", grid=(nb,),
        in_specs=[pl.BlockSpec((C_BLOCK, 512), lambda n: (n, 0)), prev, cur, nxt, prev, cur, nxt],
        out_specs=pl.BlockSpec((C_BLOCK, 768), lambda n: (n, 0)),
        out_shape=jax.ShapeDtypeStruct((T, 768), BF16), compiler_params=_params("parallel"))(dq, dkc, dkc, dkc, dvc, dvc, dvc)
    return dzc, dbias, dsink


def _merge_tiles(T):
    return _tile(T, 512, 8), 512


def _merge_fwd(ya, yb, yc, wa, wb, wc, zg, *, name):
    T = ya.shape[0]
    tm, tn = _merge_tiles(T)
    nd = D_MODEL // tn

    def body(ya_ref, yb_ref, yc_ref, wa_ref, wb_ref, wc_ref, ga_ref, gb_ref, gc_ref, o_ref):
        acc = jax.nn.sigmoid(ga_ref[...]) * _dot(ya_ref[...].astype(BF16), wa_ref[...], _NN)
        acc += jax.nn.sigmoid(gb_ref[...]) * _dot(yb_ref[...].astype(BF16), wb_ref[...], _NN)
        acc += jax.nn.sigmoid(gc_ref[...]) * _dot(yc_ref[...].astype(BF16), wc_ref[...], _NN)
        o_ref[...] = acc.astype(BF16)

    y = pl.BlockSpec((tm, 512), lambda i, j: (i, 0))
    w = pl.BlockSpec((512, tn), lambda i, j: (0, j))
    gate = lambda b: pl.BlockSpec((tm, tn), lambda i, j: (i, b * nd + j))
    return _pallas(
        body, name=name, grid=(T // tm, nd), in_specs=[y, y, y, w, w, w, gate(0), gate(1), gate(2)],
        out_specs=pl.BlockSpec((tm, tn), lambda i, j: (i, j)),
        out_shape=jax.ShapeDtypeStruct((T, D_MODEL), BF16),
        compiler_params=_params("parallel", "parallel"))(ya, yb, yc, wa, wb, wc, zg, zg, zg)


def _merge_bwd(ya, yb, yc, wa, wb, wc, zg, dm, *, name):
    T = ya.shape[0]
    tm, tn = _merge_tiles(T)
    nd = D_MODEL // tn

    def body(ya_ref, yb_ref, yc_ref, wa_ref, wb_ref, wc_ref, ga_ref, gb_ref, gc_ref, dm_ref, *outs):
        dmv = dm_ref[...].astype(F32)
        for y_ref, w_ref, g_ref, du_ref, dg_ref in zip((ya_ref, yb_ref, yc_ref), (wa_ref, wb_ref, wc_ref),
                                                       (ga_ref, gb_ref, gc_ref), outs[:3], outs[3:]):
            u = _dot(y_ref[...].astype(BF16), w_ref[...], _NN)
            sg = jax.nn.sigmoid(g_ref[...])
            du_ref[...] = (dmv * sg).astype(BF16)
            dg_ref[...] = (dmv * u * sg * (1.0 - sg)).astype(BF16)

    y = pl.BlockSpec((tm, 512), lambda i, j: (i, 0))
    w = pl.BlockSpec((512, tn), lambda i, j: (0, j))
    gate = lambda b: pl.BlockSpec((tm, tn), lambda i, j: (i, b * nd + j))
    t = pl.BlockSpec((tm, tn), lambda i, j: (i, j))
    return _pallas(
        body, name=name, grid=(T // tm, nd), in_specs=[y, y, y, w, w, w, gate(0), gate(1), gate(2), t],
        out_specs=[t] * 6, out_shape=[jax.ShapeDtypeStruct((T, D_MODEL), BF16)] * 6,
        compiler_params=_params("parallel", "parallel"))(ya, yb, yc, wa, wb, wc, zg, zg, zg, dm)


def _cross_fwd(q, kvm, *, name):
    T = q.shape[0]
    M = kvm.shape[0]
    tm = _tile(T, 512, 8)
    scale = X_DH ** -0.5

    def body(q_ref, k_ref, v_ref, o_ref):
        for h in range(X_HEADS):
            cs = slice(X_DH * h, X_DH * (h + 1))
            s = _dot(q_ref[:, cs], k_ref[:, cs], _NT) * scale
            e = jnp.exp(s - jnp.max(s, axis=-1, keepdims=True))
            p = e / jnp.sum(e, axis=-1, keepdims=True)
            o_ref[:, cs] = _dot(p.astype(BF16), v_ref[:, cs], _NN).astype(BF16)

    row = pl.BlockSpec((tm, D_MODEL), lambda i: (i, 0))
    return _pallas(
        body, name=name, grid=(T // tm,),
        in_specs=[row, pl.BlockSpec((M, D_MODEL), lambda i: (0, 0)), pl.BlockSpec((M, D_MODEL), lambda i: (0, 1))],
        out_specs=row, out_shape=jax.ShapeDtypeStruct((T, D_MODEL), BF16), compiler_params=_params("parallel"))(q, kvm, kvm)


def _cross_bwd(q, kvm, do, *, name):
    T = q.shape[0]
    M = kvm.shape[0]
    tm = _tile(T, 512, 8)
    scale = X_DH ** -0.5

    def body(q_ref, k_ref, v_ref, do_ref, dq_ref, dkv_ref):
        @pl.when(pl.program_id(0) == 0)
        def _():
            dkv_ref[...] = jnp.zeros(dkv_ref.shape, F32)

        for h in range(X_HEADS):
            cs = slice(X_DH * h, X_DH * (h + 1))
            vs = slice(D_MODEL + X_DH * h, D_MODEL + X_DH * (h + 1))
            qh, kh, doh = q_ref[:, cs], k_ref[:, cs], do_ref[:, cs]
            s = _dot(qh, kh, _NT) * scale
            e = jnp.exp(s - jnp.max(s, axis=-1, keepdims=True))
            p = e / jnp.sum(e, axis=-1, keepdims=True)
            dp = _dot(doh, v_ref[:, cs], _NT)
            ds = (p * (dp - jnp.sum(p * dp, axis=-1, keepdims=True)) * scale).astype(BF16)
            dq_ref[:, cs] = _dot(ds, kh, _NN).astype(BF16)
            dkv_ref[:, cs] += _dot(ds, qh, _TN)
            dkv_ref[:, vs] += _dot(p.astype(BF16), doh, _TN)

    row = pl.BlockSpec((tm, D_MODEL), lambda i: (i, 0))
    return _pallas(
        body, name=name, grid=(T // tm,),
        in_specs=[row, pl.BlockSpec((M, D_MODEL), lambda i: (0, 0)), pl.BlockSpec((M, D_MODEL), lambda i: (0, 1)), row],
        out_specs=[row, pl.BlockSpec((M, 2 * D_MODEL), lambda i: (0, 0))],
        out_shape=[jax.ShapeDtypeStruct((T, D_MODEL), BF16), jax.ShapeDtypeStruct((M, 2 * D_MODEL), F32)],
        compiler_params=_params("arbitrary"))(q, kvm, kvm, do)


def _ffn_up(h, w1, w3, *, name):
    T = h.shape[0]
    tm = _tile(T, 512, 8)
    tn = D_FF // 2

    def body(h_ref, w1_ref, w3_ref, a_ref, b_ref, act_ref):
        hv = h_ref[...]
        a = _dot(hv, w1_ref[...], _NN)
        b = _dot(hv, w3_ref[...], _NN)
        a_ref[...] = a
        b_ref[...] = b
        act_ref[...] = (a * jax.nn.sigmoid(a) * b).astype(BF16)

    w = pl.BlockSpec((D_MODEL, tn), lambda i, j: (0, j))
    t = pl.BlockSpec((tm, tn), lambda i, j: (i, j))
    return _pallas(
        body, name=name, grid=(T // tm, D_FF // tn), in_specs=[pl.BlockSpec((tm, D_MODEL), lambda i, j: (i, 0)), w, w],
        out_specs=[t, t, t],
        out_shape=[jax.ShapeDtypeStruct((T, D_FF), F32), jax.ShapeDtypeStruct((T, D_FF), F32), jax.ShapeDtypeStruct((T, D_FF), BF16)],
        compiler_params=_params("parallel", "parallel"))(h, w1, w3)


def _ffn_dact(dx, w2, a, b, *, name):
    T = dx.shape[0]
    tm = _tile(T, 512, 8)
    tn = D_FF // 2

    def body(dx_ref, w2_ref, a_ref, b_ref, da_ref, db_ref):
        dact = _dot(dx_ref[...].astype(BF16), w2_ref[...], _NT)
        av = a_ref[...]
        sg = jax.nn.sigmoid(av)
        da_ref[...] = (dact * b_ref[...] * (sg * (1.0 + av * (1.0 - sg)))).astype(BF16)
        db_ref[...] = (dact * (av * sg)).astype(BF16)

    t = pl.BlockSpec((tm, tn), lambda i, j: (i, j))
    return _pallas(
        body, name=name, grid=(T // tm, D_FF // tn),
        in_specs=[pl.BlockSpec((tm, D_MODEL), lambda i, j: (i, 0)), pl.BlockSpec((tn, D_MODEL), lambda i, j: (j, 0)), t, t],
        out_specs=[t, t], out_shape=[jax.ShapeDtypeStruct((T, D_FF), BF16)] * 2,
        compiler_params=_params("parallel", "parallel"))(dx, w2, a, b)


def _loss_head(x, g, target, *, name):
    T, D = x.shape
    tm = _tile(T, 512, 8)

    def body(x_ref, g_ref, t_ref, loss_ref, dx_ref, dg_ref):
        xv = x_ref[...]
        r = lax.rsqrt(jnp.mean(xv * xv, axis=-1, keepdims=True) + EPS)
        xh = xv * r
        gv = g_ref[...]
        err = xh * gv - t_ref[...]
        dy = err * (1.0 / D)
        dxh = dy * gv
        dx_ref[...] = r * (dxh - xh * jnp.mean(dxh * xh, axis=-1, keepdims=True))
        lpart = 0.5 * jnp.sum(jnp.mean(err * err, axis=-1, keepdims=True), axis=0, keepdims=True)
        gpart = jnp.sum(dy * xh, axis=0, keepdims=True)

        @pl.when(pl.program_id(0) == 0)
        def _():
            loss_ref[...] = jnp.broadcast_to(lpart, (1, LANES))
            dg_ref[...] = gpart

        @pl.when(pl.program_id(0) > 0)
        def _():
            loss_ref[...] += jnp.broadcast_to(lpart, (1, LANES))
            dg_ref[...] += gpart

    row = pl.BlockSpec((tm, D), lambda i: (i, 0))
    vec = pl.BlockSpec((1, D), lambda i: (0, 0))
    return _pallas(
        body, name=name, grid=(T // tm,), in_specs=[row, vec, row],
        out_specs=[pl.BlockSpec((1, LANES), lambda i: (0, 0)), row, vec],
        out_shape=[jax.ShapeDtypeStruct((1, LANES), F32), jax.ShapeDtypeStruct((T, D), F32), jax.ShapeDtypeStruct((1, D), F32)],
        compiler_params=_params("arbitrary"))(x, g.reshape(1, D), target)


IN_CQ, IN_CKV, IN_KR, IN_B, IN_C, IN_G, IN_END = 0, 384, 640, 672, 4768, 5536, 8608
WEIGHT_NAMES = ("w_in", "g_mix", "a_gq", "a_gkv", "a_wuq", "a_wukv", "b_lb", "b_gout", "c_sink", "rel_bias",
                "w_br_a", "w_br_b", "w_br_c", "w_out", "g_x", "g_mem", "x_wq", "x_wkv", "x_wo", "g_ffn",
                "f_w1", "f_w3", "f_w2", "g_final")


def _lower_bounds(b_lb):
    sm = jax.nn.softmax(b_lb.astype(F32), axis=1)
    return jnp.cumsum(sm, axis=1) - sm[:, :1]


def _layer_weights(w, l):
    bf = lambda a: a.astype(BF16)
    w_in = bf(w["w_in"][l])
    kr = w_in[:, IN_KR:IN_B]
    wa = jnp.concatenate([w_in[:, IN_CQ:IN_CKV], kr, _rope_swap_cols(kr), jnp.zeros((D_MODEL, 64), BF16),
                          w_in[:, IN_CKV:IN_KR]], axis=1)
    wuq = bf(w["a_wuq"][l]).reshape(A_Q_RANK, A_HEADS, A_QK)
    zeros = lambda n: jnp.zeros((A_Q_RANK, A_HEADS, n), BF16)
    wuq_pad = jnp.concatenate([wuq, zeros(A_PAD - A_QK)], axis=-1)
    wuq_sw = jnp.concatenate([zeros(A_NOPE), _rope_swap_cols(wuq[..., A_NOPE:]), zeros(A_PAD - A_QK)], axis=-1)
    wq2 = jnp.concatenate([wuq_pad.reshape(A_Q_RANK, -1), wuq_sw.reshape(A_Q_RANK, -1)], axis=1)
    wukv = bf(w["a_wukv"][l]).reshape(A_KV_RANK, A_HEADS, A_NOPE + A_V)
    wkv = jnp.concatenate([wukv[..., :A_NOPE].reshape(A_KV_RANK, -1), wukv[..., A_NOPE:].reshape(A_KV_RANK, -1)], axis=1)
    return dict(wa=wa, wb=w_in[:, IN_B:IN_C], wc=w_in[:, IN_C:IN_G], wg=w_in[:, IN_G:IN_END], wq2=wq2, wkv=wkv,
                w_br_a=bf(w["w_br_a"][l]), w_br_b=bf(w["w_br_b"][l]), w_br_c=bf(w["w_br_c"][l]), w_out=bf(w["w_out"][l]),
                x_wq=bf(w["x_wq"][l]), x_wkv=bf(w["x_wkv"][l]), x_wo=bf(w["x_wo"][l]),
                f_w1=bf(w["f_w1"][l]), f_w3=bf(w["f_w3"][l]), f_w2=bf(w["f_w2"][l]))


def _layer_fwd(l, x, mem, w, lw, lower, bias, tabs):
    n = lambda s: f"l{l}_{s}"
    cq_t, sq_t, ck, _ = tabs
    s = dict(x=x)
    s["h0"] = h0 = _rms(x, w["g_mix"][l], name=n("rms_mix"))
    s["za"] = za = _mm(h0, lw["wa"], name=n("in_a"))
    s["zb"] = zb = _mm(h0, lw["wb"], name=n("in_b"))
    s["zc"] = zc = _mm(h0, lw["wc"], out_dtype=BF16, name=n("in_c"))
    s["zg"] = zg = _mm(h0, lw["wg"], name=n("in_g"))
    s["cqn"] = cqn = _rms(za, w["a_gq"][l], col=0, width=A_Q_RANK, name=n("rms_cq"))
    s["ckvn"] = ckvn = _rms(za, w["a_gkv"][l], col=512, width=A_KV_RANK, name=n("rms_ckv"))
    s["q"] = q = _qrope(_mm(cqn, lw["wq2"], name=n("uq")), cq_t, sq_t, name=n("qrope"))
    s["kv"] = kv = _mm(ckvn, lw["wkv"], out_dtype=BF16, name=n("ukv"))
    s["k"], vx = k, _ = _kprep(kv, za, ck, name=n("kprep"))
    s["ya"], s["lse"] = ya, _ = _flash_fwd(q, k, vx.T, name=n("mla"))
    lb_f, lb_b = lower[0, l].reshape(1, -1), lower[1, l].reshape(1, -1)
    s["of"], s["stf"] = of, _ = _hgrn_fwd(zb, lb_f, reverse=False, name=n("hgrn_f"))
    s["ob"], s["stb"] = ob, _ = _hgrn_fwd(zb, lb_b, reverse=True, name=n("hgrn_b"))
    s["yb"] = yb = _hgrn_out(of, ob, zb, w["b_gout"][l], name=n("hgrn_out"))
    s["yc"] = yc = _swa_fwd(zc, bias, w["c_sink"][l], name=n("swa"))
    s["merged"] = merged = _merge_fwd(ya, yb, yc, lw["w_br_a"], lw["w_br_b"], lw["w_br_c"], zg, name=n("merge"))
    s["x1"] = x1 = _mm(merged, lw["w_out"], add=x, name=n("out"))
    s["h1"] = h1 = _rms(x1, w["g_x"][l], name=n("rms_x"))
    s["qx"] = qx = _mm(h1, lw["x_wq"], out_dtype=BF16, name=n("xq"))
    s["memn"] = memn = _rms(mem, w["g_mem"][l], name=n("rms_mem"))
    s["kvm"] = kvm = _mm(memn, lw["x_wkv"], out_dtype=BF16, name=n("xkv"))
    s["ox"] = ox = _cross_fwd(qx, kvm, name=n("cross"))
    s["x2"] = x2 = _mm(ox, lw["x_wo"], add=x1, name=n("xo"))
    s["h2"] = h2 = _rms(x2, w["g_ffn"][l], name=n("rms_ffn"))
    s["a"], s["b"], s["act"] = a, b, act = _ffn_up(h2, lw["f_w1"], lw["f_w3"], name=n("ffn_up"))
    x3 = _mm(act, lw["f_w2"], add=x2, name=n("ffn_down"))
    return x3, s


def _layer_bwd(l, dx3, mem, w, lw, lower, bias, tabs, s):
    n = lambda t: f"l{l}_b_{t}"
    cq_t, sq_t, _, ck_t = tabs
    g = {}
    da, db = _ffn_dact(dx3, lw["f_w2"], s["a"], s["b"], name=n("ffn_dact"))
    g["f_w2"] = _mm(s["act"], dx3, mode="tn", name=n("dw2"))
    dh2 = _mm(db, lw["f_w3"], mode="nt", add=_mm(da, lw["f_w1"], mode="nt", name=n("dh2a")), name=n("dh2b"))
    g["f_w1"] = _mm(s["h2"], da, mode="tn", name=n("dw1"))
    g["f_w3"] = _mm(s["h2"], db, mode="tn", name=n("dw3"))
    dx2, g["g_ffn"] = _rms_bwd(s["x2"], w["g_ffn"][l], dh2, res=dx3, name=n("rms_ffn"))
    dox = _mm(dx2, lw["x_wo"], mode="nt", out_dtype=BF16, name=n("dox"))
    g["x_wo"] = _mm(s["ox"], dx2, mode="tn", name=n("dwo"))
    dqx, dkvm = _cross_bwd(s["qx"], s["kvm"], dox, name=n("cross"))
    g["x_wq"] = _mm(s["h1"], dqx, mode="tn", name=n("dwq"))
    dh1 = _mm(dqx, lw["x_wq"], mode="nt", name=n("dh1"))
    g["x_wkv"] = _mm(s["memn"], dkvm, mode="tn", name=n("dwkv"))
    dmemn = _mm(dkvm, lw["x_wkv"], mode="nt", name=n("dmemn"))
    _, g["g_mem"] = _rms_bwd(mem, w["g_mem"][l], dmemn, name=n("rms_mem"))
    dx1, g["g_x"] = _rms_bwd(s["x1"], w["g_x"][l], dh1, res=dx2, name=n("rms_x"))
    dmerged = _mm(dx1, lw["w_out"], mode="nt", name=n("dmerged"))
    g["w_out"] = _mm(s["merged"], dx1, mode="tn", name=n("dwout"))
    dua, dub, duc, dga, dgb, dgc = _merge_bwd(s["ya"], s["yb"], s["yc"], lw["w_br_a"], lw["w_br_b"], lw["w_br_c"],
                                              s["zg"], dmerged, name=n("merge"))
    dya = _mm(dua, lw["w_br_a"], mode="nt", out_dtype=BF16, name=n("dya"))
    dyb = _mm(dub, lw["w_br_b"], mode="nt", name=n("dyb"))
    dyc = _mm(duc, lw["w_br_c"], mode="nt", out_dtype=BF16, name=n("dyc"))
    g["w_br_a"] = _mm(s["ya"], dua, mode="tn", name=n("dwbra"))
    g["w_br_b"] = _mm(s["yb"], dub, mode="tn", name=n("dwbrb"))
    g["w_br_c"] = _mm(s["yc"], duc, mode="tn", name=n("dwbrc"))
    dzc, dbias, dsink = _swa_bwd(s["zc"], bias, w["c_sink"][l], dyc, name=n("swa"))
    g["c_sink"] = dsink[:, 0]
    g["bias"] = dbias
    lb_f, lb_b = lower[0, l].reshape(1, -1), lower[1, l].reshape(1, -1)
    do_, dgate, dgout = _hgrn_out_bwd(s["of"], s["ob"], s["zb"], w["b_gout"][l], dyb, name=n("hgrn_out"))
    g["b_gout"] = dgout[0]
    dq_f, dzf, dv_f, dlb_f = _hgrn_bwd(s["zb"], lb_f, do_, s["stf"], reverse=False, name=n("hgrn_f"))
    dq_b, dzr, dv_b, dlb_b = _hgrn_bwd(s["zb"], lb_b, do_, s["stb"], reverse=True, name=n("hgrn_b"))
    g["lower"] = jnp.concatenate([dlb_f, dlb_b], axis=0)
    dzb = _dzb_assemble(dq_f, dq_b, dzf, dzr, dv_f, dv_b, dgate, name=n("dzb"))
    delta = _attn_delta(s["ya"], dya, name=n("mla_delta"))
    dq_part, dkt, dvt = _flash_bwd(s["q"], s["q"].T, s["k"], s["kv"], dya, dya.T, s["lse"].reshape(A_HEADS, -1, 1), delta, name=n("mla"))
    dq2 = _qrope_bwd(_dq_sum(dq_part, name=n("mla_dq")), cq_t, sq_t, name=n("qrope"))
    dcqn = _mm(dq2, lw["wq2"], mode="nt", name=n("dcqn"))
    dwq2 = _mm(s["cqn"], dq2, mode="tn", name=n("dwq2")).reshape(A_Q_RANK, 2, A_HEADS, A_PAD)
    dknt, dzkrt = _kprep_bwd(dkt, ck_t, name=n("kprep"))
    wkv = lw["wkv"]
    dckvn = _mm(dvt, wkv[:, 512:].T, mode="tn", add=_mm(dknt, wkv[:, :512].T, mode="tn", name=n("dckvn_k")), name=n("dckvn_v"))
    dwkn = _mm(dknt, s["ckvn"], name=n("dwkn")).T
    dwv = _mm(dvt, s["ckvn"], name=n("dwv")).T
    dzcq, dgq = _rms_bwd(s["za"], w["a_gq"][l], dcqn, col=0, width=A_Q_RANK, out_dtype=BF16, name=n("rms_cq"))
    dzckv, dgkv = _rms_bwd(s["za"], w["a_gkv"][l], dckvn, col=512, width=A_KV_RANK, out_dtype=BF16, name=n("rms_ckv"))
    g["a_gq"], g["a_gkv"] = dgq[0], dgkv[0]
    sw = jnp.concatenate([jnp.zeros((A_Q_RANK, A_HEADS, A_NOPE), F32), _rope_unswap_cols(dwq2[:, 1, :, A_NOPE:A_QK])], axis=-1)
    g["a_wuq"] = (dwq2[:, 0, :, :A_QK] + sw).reshape(A_Q_RANK, -1)
    g["a_wukv"] = jnp.concatenate([dwkn.reshape(A_KV_RANK, A_HEADS, A_NOPE), dwv.reshape(A_KV_RANK, A_HEADS, A_V)], axis=-1).reshape(A_KV_RANK, -1)
    wa = lw["wa"]
    pieces = [(dzcq, wa[:, 0:384]), (dzckv, wa[:, 512:768]), (dzb, lw["wb"]), (dzc, lw["wc"]),
              (dga, lw["wg"][:, 0:1024]), (dgb, lw["wg"][:, 1024:2048]), (dgc, lw["wg"][:, 2048:3072])]
    dh0 = _mm(dzkrt, wa[:, 384:512].T, mode="tn", name=n("dh0_kr"))
    dwkr = _mm(dzkrt, s["h0"], name=n("dwin_kr")).T
    dwkr = dwkr[:, 0:A_ROPE] + _rope_unswap_cols(dwkr[:, A_ROPE:2 * A_ROPE])
    dws = []
    for i, (dz, wp) in enumerate(pieces):
        dh0 = _mm(dz, wp, mode="nt", add=dh0, name=n(f"dh0_{i}"))
        dws.append(_mm(s["h0"], dz, mode="tn", name=n(f"dwin_{i}")))
    g["w_in"] = jnp.concatenate([dws[0], dws[1], dwkr] + dws[2:], axis=1)
    dx, g["g_mix"] = _rms_bwd(s["x"], w["g_mix"][l], dh0, res=dx1, name=n("rms_mix"))
    return dx, g


def _local_step(x, mem, target, w):
    T = x.shape[0]
    tabs = _rope_tables(T)
    lower, lower_vjp = jax.vjp(_lower_bounds, w["b_lb"])
    buckets = _swa_buckets()
    onehot = (buckets.reshape(-1)[:, None] == jnp.arange(REL_BUCKETS)[None, :]).astype(F32)
    bias = jnp.dot(w["rel_bias"].astype(F32).T, onehot.T, precision=lax.Precision.HIGHEST).reshape(C_HEADS, C_BLOCK, C_SPAN)
    lws, saved = [], []
    for l in range(DEPTH):
        lws.append(_layer_weights(w, l))
        x, s = _layer_fwd(l, x, mem, w, lws[l], lower, bias, tabs)
        saved.append(s)
    loss, dx, dg_final = _loss_head(x, w["g_final"], target, name="loss_head")
    layer_grads = [None] * DEPTH
    for l in reversed(range(DEPTH)):
        dx, layer_grads[l] = _layer_bwd(l, dx, mem, w, lws[l], lower, bias, tabs, saved[l])
        saved[l] = None
    grads = {}
    for name in WEIGHT_NAMES:
        if name in layer_grads[0]:
            grads[name] = jnp.stack([layer_grads[l][name].reshape(w[name].shape[1:]) for l in range(DEPTH)])
    grads["g_final"] = dg_final[0]
    dlower = jnp.stack([layer_grads[l]["lower"] for l in range(DEPTH)], axis=1)
    grads["b_lb"] = lower_vjp(dlower)[0]
    dbias = layer_grads[0]["bias"] + layer_grads[1]["bias"]
    grads["rel_bias"] = jnp.dot(onehot.T, dbias.reshape(C_HEADS, -1).T, precision=lax.Precision.HIGHEST)
    return loss, dx, grads


N_CHIPS = 4
PACK_COLS = 1024
PACK_ALIGN = 32 * PACK_COLS
SHARDED = (("w_in", 2), ("a_wuq", 2), ("a_wukv", 2), ("b_lb", 2), ("w_br_a", 2), ("w_br_b", 2), ("w_br_c", 2), ("w_out", 1),
           ("x_wq", 1), ("x_wkv", 2), ("x_wo", 1), ("f_w1", 2), ("f_w3", 2), ("f_w2", 1))
REPLICATED = ("g_mix", "a_gq", "a_gkv", "b_gout", "c_sink", "rel_bias", "g_x", "g_mem", "g_ffn", "g_final")
MESH_IDS = pl.DeviceIdType.MESH
ANY_SPEC = pl.BlockSpec(memory_space=pl.ANY)


def _pack(arrs, cols, align):
    flat = jnp.concatenate([a.reshape(-1) for a in arrs])
    pad = (-flat.shape[0]) % align
    return jnp.pad(flat, (0, pad)).reshape(-1, cols)


def _unpack(buf, shapes):
    flat = buf.reshape(-1)
    out, start = [], 0
    for shp in shapes:
        size = math.prod(shp)
        out.append(flat[start:start + size].reshape(shp))
        start += size
    return out


def _chip_peers():
    x, y, c = lax.axis_index("x"), lax.axis_index("y"), lax.axis_index("c")
    return x, y, c, [(1 - x, y), (x, 1 - y), (1 - x, 1 - y)]


def _chip_gather(src, chip, *, name):
    _, R, C = src.shape

    def body(src_ref, out_ref, send_sems, recv_sems, pass_send_sems, pass_recv_sems):
        x, y, c, chips = _chip_peers()
        me = 2 * x + y
        sibling = (x, y, 1 - c)

        def over_ici(j, slot):
            px, py = chips[j]
            return pltpu.make_async_remote_copy(src_ref=src_ref.at[c], dst_ref=out_ref.at[slot, c], send_sem=send_sems.at[j],
                                                recv_sem=recv_sems.at[j], device_id=(px, py, c), device_id_type=MESH_IDS)

        def pass_on(j, half):
            px, py = chips[j]
            piece = out_ref.at[2 * px + py, half]
            return pltpu.make_async_remote_copy(src_ref=piece, dst_ref=piece, send_sem=pass_send_sems.at[j],
                                                recv_sem=pass_recv_sems.at[j], device_id=sibling, device_id_type=MESH_IDS)

        sends = [over_ici(j, me) for j in range(3)]
        for cp in sends:
            cp.start()
        passed = []
        for j, (px, py) in enumerate(chips):
            over_ici(j, 2 * px + py).wait_recv()
            passed.append(pass_on(j, c))
            passed[j].start()
        for j in range(3):
            pass_on(j, 1 - c).wait_recv()
        for cp in sends + passed:
            cp.wait_send()

    gathered = _pallas(
        body, name=name, in_specs=[ANY_SPEC], out_specs=ANY_SPEC, out_shape=jax.ShapeDtypeStruct((N_CHIPS, 2, R, C), src.dtype),
        scratch_shapes=[pltpu.SemaphoreType.DMA((3,))] * 4,
        compiler_params=pltpu.CompilerParams(has_side_effects=True))(src)

    tr = _tile(R, 512, 16)

    def place(chip_ref, gathered_ref, own_ref, out_ref):
        out_ref[0, 0] = own_ref[0]

    grid_spec = pltpu.PrefetchScalarGridSpec(
        num_scalar_prefetch=1, grid=(2, R // tr),
        in_specs=[ANY_SPEC, pl.BlockSpec((1, tr, C), lambda h, i, chip_ref: (h, i, 0))],
        out_specs=pl.BlockSpec((1, 1, tr, C), lambda h, i, chip_ref: (chip_ref[0], h, i, 0)))
    return _pallas(place, name=name + "_own", grid_spec=grid_spec, out_shape=jax.ShapeDtypeStruct(gathered.shape, gathered.dtype),
                   input_output_aliases={1: 0}, compiler_params=_params("arbitrary", "arbitrary"))(chip, gathered, src)


def _chip_scatter(src, *, name):
    _, R, C = src.shape

    def body(src_ref, out_ref, send_sems, recv_sems):
        x, y, c, chips = _chip_peers()
        me = 2 * x + y

        def copy(j, seg):
            px, py = chips[j]
            return pltpu.make_async_remote_copy(src_ref=src_ref.at[seg], dst_ref=out_ref.at[j], send_sem=send_sems.at[j],
                                                recv_sem=recv_sems.at[j], device_id=(px, py, c), device_id_type=MESH_IDS)

        sends = [copy(j, 2 * px + py) for j, (px, py) in enumerate(chips)]
        for cp in sends:
            cp.start()
        for j in range(3):
            copy(j, me).wait_recv()
        for cp in sends:
            cp.wait_send()

    return _pallas(
        body, name=name, in_specs=[ANY_SPEC], out_specs=ANY_SPEC, out_shape=jax.ShapeDtypeStruct((3, R, C), src.dtype),
        scratch_shapes=[pltpu.SemaphoreType.DMA((3,)), pltpu.SemaphoreType.DMA((3,))],
        compiler_params=pltpu.CompilerParams(has_side_effects=True))(src)


PAIR_CHUNKS = 4


def _pair_swap(src, *, halves, name):
    R, C = src.shape[-2:]
    n = N_CHIPS if halves else 1
    rc = R // PAIR_CHUNKS
    assert rc * PAIR_CHUNKS == R and rc % 16 == 0, R

    def body(src_ref, out_ref, send_sems, recv_sems):
        x, y, c = lax.axis_index("x"), lax.axis_index("y"), lax.axis_index("c")
        copies = []
        for k in range(n):
            for r in range(PAIR_CHUNKS):
                rows = pl.ds(r * rc, rc)
                s = src_ref.at[k, 1 - c, rows] if halves else src_ref.at[rows]
                d = out_ref.at[k, rows] if halves else out_ref.at[rows]
                i = k * PAIR_CHUNKS + r
                copies.append(pltpu.make_async_remote_copy(src_ref=s, dst_ref=d, send_sem=send_sems.at[i], recv_sem=recv_sems.at[i],
                                                           device_id=(x, y, 1 - c), device_id_type=MESH_IDS))
        for cp in copies:
            cp.start()
        for cp in copies:
            cp.wait_recv()
        for cp in copies:
            cp.wait_send()

    shape = (N_CHIPS, R, C) if halves else (R, C)
    return _pallas(
        body, name=name, in_specs=[ANY_SPEC], out_specs=ANY_SPEC, out_shape=jax.ShapeDtypeStruct(shape, src.dtype),
        scratch_shapes=[pltpu.SemaphoreType.DMA((n * PAIR_CHUNKS,)), pltpu.SemaphoreType.DMA((n * PAIR_CHUNKS,))],
        compiler_params=pltpu.CompilerParams(has_side_effects=True))(src)


def _pair_add(g4, got, c, *, name):
    _, _, R, C = g4.shape
    tr = _tile(R, 512, 16)

    def body(c_ref, mine_ref, got_ref, o_ref, ob_ref):
        s = mine_ref[0, 0] + got_ref[0].astype(F32)
        o_ref[0] = s
        ob_ref[0] = s.astype(BF16)

    blk = pl.BlockSpec((1, tr, C), lambda k, i, c_ref: (k, i, 0))
    grid_spec = pltpu.PrefetchScalarGridSpec(
        num_scalar_prefetch=1, grid=(N_CHIPS, R // tr),
        in_specs=[pl.BlockSpec((1, 1, tr, C), lambda k, i, c_ref: (k, c_ref[0], i, 0)), blk], out_specs=[blk, blk])
    return _pallas(body, name=name, grid_spec=grid_spec,
                   out_shape=[jax.ShapeDtypeStruct((N_CHIPS, R, C), F32), jax.ShapeDtypeStruct((N_CHIPS, R, C), BF16)],
                   compiler_params=_params("parallel", "parallel"))(c, g4, got)


def _chip_sum(pair_sum, landed, me, *, name):
    _, R, C = pair_sum.shape
    tr = _tile(R, 512, 16)

    def body(me_ref, own_ref, landed_ref, o_ref):
        acc = own_ref[0]
        for j in range(3):
            acc = acc + landed_ref[j].astype(F32)
        o_ref[...] = acc

    grid_spec = pltpu.PrefetchScalarGridSpec(
        num_scalar_prefetch=1, grid=(R // tr,),
        in_specs=[pl.BlockSpec((1, tr, C), lambda i, me_ref: (me_ref[0], i, 0)), pl.BlockSpec((3, tr, C), lambda i, me_ref: (0, i, 0))],
        out_specs=pl.BlockSpec((tr, C), lambda i, me_ref: (i, 0)))
    return _pallas(body, name=name, grid_spec=grid_spec, out_shape=jax.ShapeDtypeStruct((R, C), F32),
                   compiler_params=_params("parallel"))(me, pair_sum, landed)


def _join_halves(mine, got, c, *, name):
    R, C = mine.shape
    tr = _tile(R, 512, 16)

    def body(c_ref, mine_ref, got_ref, o_ref):
        use_mine = pl.program_id(0) == c_ref[0]
        o_ref[0] = jnp.where(use_mine, mine_ref[...], got_ref[...])

    blk = pl.BlockSpec((tr, C), lambda h, i, c_ref: (i, 0))
    grid_spec = pltpu.PrefetchScalarGridSpec(num_scalar_prefetch=1, grid=(2, R // tr), in_specs=[blk, blk],
                                             out_specs=pl.BlockSpec((1, tr, C), lambda h, i, c_ref: (h, i, 0)))
    return _pallas(body, name=name, grid_spec=grid_spec, out_shape=jax.ShapeDtypeStruct((2, R, C), mine.dtype),
                   compiler_params=_params("parallel", "parallel"))(c, mine, got).reshape(2 * R, C)


def _gather8(s, *, name):
    R, C = s.shape

    def body(s_ref, out_ref, send_sems, recv_sems):
        x, y, c = lax.axis_index("x"), lax.axis_index("y"), lax.axis_index("c")
        me = 4 * x + 2 * y + c
        flips = [(dx, dy, dc) for dx in (0, 1) for dy in (0, 1) for dc in (0, 1)][1:]
        out_ref[me] = s_ref[...]

        def copy(j, slot):
            dx, dy, dc = flips[j]
            return pltpu.make_async_remote_copy(src_ref=s_ref, dst_ref=out_ref.at[slot], send_sem=send_sems.at[j],
                                                recv_sem=recv_sems.at[j], device_id=(x ^ dx, y ^ dy, c ^ dc), device_id_type=MESH_IDS)

        sends = [copy(j, me) for j in range(7)]
        for cp in sends:
            cp.start()
        for j, (dx, dy, dc) in enumerate(flips):
            copy(j, 4 * (x ^ dx) + 2 * (y ^ dy) + (c ^ dc)).wait_recv()
        for cp in sends:
            cp.wait_send()

    vmem = pl.BlockSpec(memory_space=pltpu.VMEM)
    return _pallas(
        body, name=name, in_specs=[vmem], out_specs=vmem, out_shape=jax.ShapeDtypeStruct((8, R, C), s.dtype),
        scratch_shapes=[pltpu.SemaphoreType.DMA((7,)), pltpu.SemaphoreType.DMA((7,))],
        compiler_params=pltpu.CompilerParams(has_side_effects=True))(s)


def _sum_slots(a, *, name):
    n, R, C = a.shape
    tr = _tile(R, 512, 8)

    def body(a_ref, o_ref):
        acc = a_ref[0]
        for k in range(1, n):
            acc = acc + a_ref[k]
        o_ref[...] = acc

    return _pallas(body, name=name, grid=(R // tr,), in_specs=[pl.BlockSpec((n, tr, C), lambda i: (0, i, 0))],
                   out_specs=pl.BlockSpec((tr, C), lambda i: (i, 0)), out_shape=jax.ShapeDtypeStruct((R, C), a.dtype),
                   compiler_params=_params("parallel"))(a)


def _adamw(w, g, m, v, *, name):
    R, C = w.shape
    tr = _tile(R, max(8, (1 << 18) // C // 8 * 8), 8)
    c1 = 1.0 / (1.0 - ADAM_B1 ** ADAM_STEP)
    c2 = 1.0 / (1.0 - ADAM_B2 ** ADAM_STEP)

    def body(w_ref, g_ref, m_ref, v_ref, d_ref, nm_ref, nv_ref):
        gv = g_ref[...]
        nm = ADAM_B1 * m_ref[...] + (1.0 - ADAM_B1) * gv
        nv = ADAM_B2 * v_ref[...] + (1.0 - ADAM_B2) * (gv * gv)
        d_ref[...] = -ADAM_LR * ((nm * c1) / (jnp.sqrt(nv * c2) + ADAM_EPS) + ADAM_WD * w_ref[...])
        nm_ref[...] = nm
        nv_ref[...] = nv

    blk = pl.BlockSpec((tr, C), lambda i: (i, 0))
    shape = jax.ShapeDtypeStruct((R, C), F32)
    return _pallas(body, name=name, grid=(R // tr,), in_specs=[blk] * 4, out_specs=[blk] * 3, out_shape=[shape] * 3,
                   compiler_params=_params("parallel"))(w, g, m, v)


def kernel(x, mem, w_in, g_mix, a_gq, a_gkv, a_wuq, a_wukv, b_lb, b_gout, c_sink, rel_bias, w_br_a, w_br_b, w_br_c, w_out, g_x, g_mem, x_wq, x_wkv, x_wo, g_ffn, f_w1, f_w3, f_w2, g_final, loss_target, m_w_in, m_g_mix, m_a_gq, m_a_gkv, m_a_wuq, m_a_wukv, m_b_lb, m_b_gout, m_c_sink, m_rel_bias, m_w_br_a, m_w_br_b, m_w_br_c, m_w_out, m_g_x, m_g_mem, m_x_wq, m_x_wkv, m_x_wo, m_g_ffn, m_f_w1, m_f_w3, m_f_w2, m_g_final, v_w_in, v_g_mix, v_a_gq, v_a_gkv, v_a_wuq, v_a_wukv, v_b_lb, v_b_gout, v_c_sink, v_rel_bias, v_w_br_a, v_w_br_b, v_w_br_c, v_w_out, v_g_x, v_g_mem, v_x_wq, v_x_wkv, v_x_wo, v_g_ffn, v_f_w1, v_f_w3, v_f_w2, v_g_final):
    ws = dict(zip(WEIGHT_NAMES, (w_in, g_mix, a_gq, a_gkv, a_wuq, a_wukv, b_lb, b_gout, c_sink, rel_bias, w_br_a, w_br_b, w_br_c,
                                 w_out, g_x, g_mem, x_wq, x_wkv, x_wo, g_ffn, f_w1, f_w3, f_w2, g_final)))
    ms = dict(zip(WEIGHT_NAMES, (m_w_in, m_g_mix, m_a_gq, m_a_gkv, m_a_wuq, m_a_wukv, m_b_lb, m_b_gout, m_c_sink, m_rel_bias,
                                 m_w_br_a, m_w_br_b, m_w_br_c, m_w_out, m_g_x, m_g_mem, m_x_wq, m_x_wkv, m_x_wo, m_g_ffn,
                                 m_f_w1, m_f_w3, m_f_w2, m_g_final)))
    vs = dict(zip(WEIGHT_NAMES, (v_w_in, v_g_mix, v_a_gq, v_a_gkv, v_a_wuq, v_a_wukv, v_b_lb, v_b_gout, v_c_sink, v_rel_bias,
                                 v_w_br_a, v_w_br_b, v_w_br_c, v_w_out, v_g_x, v_g_mem, v_x_wq, v_x_wkv, v_x_wo, v_g_ffn,
                                 v_f_w1, v_f_w3, v_f_w2, v_g_final)))
    sharded = [n for n, _ in SHARDED]
    axis_of = dict(SHARDED)

    def wire(n):
        return lax.bitcast_convert_type(ws[n], BF16) if n == "b_lb" else ws[n].astype(BF16)

    core = lax.axis_index("c").astype(jnp.int32).reshape(1)
    chip = (2 * lax.axis_index("x") + lax.axis_index("y")).astype(jnp.int32).reshape(1)
    wire_shapes = [wire(n).shape for n in sharded]
    packed = _pack([wire(n) for n in sharded], PACK_COLS, PACK_ALIGN)
    gathered = _chip_gather(packed.reshape(2, packed.shape[0] // 2, PACK_COLS), chip, name="gather_weights")
    per_chip = [_unpack(gathered[k], wire_shapes) for k in range(N_CHIPS)]
    full = dict(ws)
    for i, n in enumerate(sharded):
        parts = [per_chip[k][i] for k in range(N_CHIPS)]
        if n == "b_lb":
            parts = [lax.bitcast_convert_type(p, F32) for p in parts]
        full[n] = jnp.concatenate(parts, axis=axis_of[n])

    loss, grad_x, grads = _local_step(x[0], mem[0], loss_target[0], full)

    segs = []
    for k in range(N_CHIPS):
        pieces = [jnp.split(grads[n], N_CHIPS, axis=axis_of[n])[k] for n in sharded]
        segs.append(_pack(pieces, PACK_COLS, PACK_ALIGN))
    rows = segs[0].shape[0]
    g4 = jnp.stack(segs).reshape(N_CHIPS, 2, rows // 2, PACK_COLS)
    got = _pair_swap(g4.astype(BF16), halves=True, name="reduce_pair_swap")
    pair_sum, pair_sum_wire = _pair_add(g4, got, core, name="reduce_pair_add")
    landed = _chip_scatter(pair_sum_wire, name="reduce_chip_scatter")
    mine = _chip_sum(pair_sum, landed, chip, name="reduce_chip_sum")
    g_shard = _join_halves(mine, _pair_swap(mine, halves=False, name="reduce_pair_join"), core, name="reduce_join_halves")

    small = _pack([grads[n] for n in REPLICATED] + [loss[0, 0:1]], LANES, 8 * LANES)
    small_sum = _sum_slots(_gather8(small, name="gather_small"), name="sum_small")
    small_grads = _unpack(small_sum, [ws[n].shape for n in REPLICATED] + [(1,)])
    loss_total = small_grads.pop()[0]

    shard_shapes = [ws[n].shape for n in sharded]
    out = {}
    for n, gr in zip(sharded, _unpack(g_shard, shard_shapes)):
        flat2 = lambda a: a.reshape(-1, a.shape[-1])
        d, nm, nv = _adamw(flat2(ws[n]), flat2(gr), flat2(ms[n]), flat2(vs[n]), name="adamw_" + n)
        out[n] = (gr, d.reshape(gr.shape), nm.reshape(gr.shape), nv.reshape(gr.shape))
    pk_s = lambda d: _pack([d[n] for n in REPLICATED], LANES, 8 * LANES)
    rep_shapes = [ws[n].shape for n in REPLICATED]
    gs_flat = _pack(small_grads, LANES, 8 * LANES)
    ds_flat, ms_flat, vs_flat = _adamw(pk_s(ws), gs_flat, pk_s(ms), pk_s(vs), name="adamw_replicated")
    for n, gr, d, nm, nv in zip(REPLICATED, small_grads, _unpack(ds_flat, rep_shapes), _unpack(ms_flat, rep_shapes),
                                _unpack(vs_flat, rep_shapes)):
        out[n] = (gr, d, nm, nv)
    return (loss_total, grad_x[None], *[out[n][0] for n in WEIGHT_NAMES], *[out[n][1] for n in WEIGHT_NAMES],
            *[out[n][2] for n in WEIGHT_NAMES], *[out[n][3] for n in WEIGHT_NAMES])
```

```python
import functools
import math

import jax
import jax.numpy as jnp
from jax import lax
from jax.experimental import pallas as pl
from jax.experimental.pallas import tpu as pltpu

F32 = jnp.float32
BF16 = jnp.bfloat16

D_MODEL = 1024
DEPTH = 2
EPS = 1e-6
MASK_VALUE = -1e30
TINY = 1e-30
A_HEADS, A_NOPE, A_ROPE, A_V = 8, 64, 32, 64
A_QK = A_NOPE + A_ROPE
A_Q_RANK, A_KV_RANK = 384, 256
ROPE_THETA = 10000.0
B_HEADS, B_DK, B_DV, B_CHUNK = 8, 128, 64, 16
C_HEADS, C_KV_HEADS, C_DH, C_WINDOW, C_BLOCK = 8, 2, 64, 128, 128
REL_BUCKETS, REL_MAX_DIST = 32, 128
X_HEADS, X_DH = 4, 256
D_FF = 2816
ADAM_LR, ADAM_B1, ADAM_B2, ADAM_EPS, ADAM_WD, ADAM_STEP = 0.001, 0.9, 0.999, 1e-08, 0.01, 10

LANES = 128
VMEM_LIMIT = 56 * 1024 * 1024


def _pallas(body, **kw):
    return pl.pallas_call(body, **kw)


def _params(*sem):
    return pltpu.CompilerParams(dimension_semantics=sem, vmem_limit_bytes=VMEM_LIMIT)


def _tile(n, pref, unit=LANES):
    if n <= pref:
        return n
    t = (pref // unit) * unit
    while t > unit and n % t:
        t -= unit
    assert n % t == 0, (n, pref, unit)
    return t


def _dot(a, b, dims):
    return lax.dot_general(a, b, (dims, ((), ())), preferred_element_type=F32)


_NN = ((1,), (0,))
_NT = ((1,), (1,))
_TN = ((0,), (0,))


def _mm_tiles(M, N, K, mode):
    tm = _tile(M, 1408 if M % 1408 == 0 else 512, LANES if mode == "tn" else 8)
    tn = _tile(N, 1408 if N % 1408 == 0 else 1024, 256 if N % 256 == 0 and N % 1408 else LANES)
    tk = K if K <= 2816 else _tile(K, 1024)
    return tm, tn, tk


def _mm(a, b, *, mode="nn", add=None, out_dtype=F32, tiles=None, name):
    if mode == "nn":
        (M, K), (K2, N) = a.shape, b.shape
    elif mode == "nt":
        (M, K), (N, K2) = a.shape, b.shape
    else:
        (K, M), (K2, N) = a.shape, b.shape
    assert K == K2, (a.shape, b.shape, mode)
    tm, tn, tk = tiles or _mm_tiles(M, N, K, mode)
    nk = K // tk
    dims = {"nn": _NN, "nt": _NT, "tn": _TN}[mode]
    a_spec = pl.BlockSpec((tk, tm), lambda i, j, k: (k, i)) if mode == "tn" else pl.BlockSpec((tm, tk), lambda i, j, k: (i, k))
    b_spec = pl.BlockSpec((tn, tk), lambda i, j, k: (j, k)) if mode == "nt" else pl.BlockSpec((tk, tn), lambda i, j, k: (k, j))
    o_spec = pl.BlockSpec((tm, tn), lambda i, j, k: (i, j))
    has_add = add is not None

    def body(*refs):
        if has_add:
            a_ref, b_ref, add_ref, o_ref, acc_ref = refs
        else:
            a_ref, b_ref, o_ref, acc_ref = refs
        k = pl.program_id(2)
        part = _dot(a_ref[...].astype(BF16), b_ref[...].astype(BF16), dims)

        @pl.when(k == 0)
        def _():
            acc_ref[...] = part

        @pl.when(k > 0)
        def _():
            acc_ref[...] += part

        @pl.when(k == nk - 1)
        def _():
            r = acc_ref[...]
            if has_add:
                r = r + add_ref[...].astype(F32)
            o_ref[...] = r.astype(out_dtype)

    ins = [a, b] + ([add] if has_add else [])
    in_specs = [a_spec, b_spec] + ([o_spec] if has_add else [])
    return _pallas(
        body, name=name, grid=(M // tm, N // tn, nk), in_specs=in_specs, out_specs=o_spec,
        out_shape=jax.ShapeDtypeStruct((M, N), out_dtype), scratch_shapes=[pltpu.VMEM((tm, tn), F32)],
        compiler_params=_params("parallel", "parallel", "arbitrary"),
    )(*ins)


def _rms(x, g, *, col=0, width=None, out_dtype=BF16, name):
    T = x.shape[0]
    width = x.shape[1] if width is None else width
    assert col % width == 0
    tm = _tile(T, 512, 8)
    cb = col // width

    def body(x_ref, g_ref, o_ref):
        xv = x_ref[...].astype(F32)
        r = lax.rsqrt(jnp.mean(xv * xv, axis=-1, keepdims=True) + EPS)
        o_ref[...] = (xv * r * g_ref[...]).astype(out_dtype)

    return _pallas(
        body, name=name, grid=(T // tm,),
        in_specs=[pl.BlockSpec((tm, width), lambda i: (i, cb)), pl.BlockSpec((1, width), lambda i: (0, 0))],
        out_specs=pl.BlockSpec((tm, width), lambda i: (i, 0)),
        out_shape=jax.ShapeDtypeStruct((T, width), out_dtype), compiler_params=_params("parallel"),
    )(x, g.reshape(1, width))


def _rms_bwd(x, g, dy, *, res=None, col=0, width=None, out_dtype=F32, name):
    T = x.shape[0]
    width = x.shape[1] if width is None else width
    assert col % width == 0
    tm = _tile(T, 512, 8)
    cb = col // width
    has_res = res is not None

    def body(*refs):
        if has_res:
            x_ref, g_ref, dy_ref, res_ref, dx_ref, dg_ref = refs
        else:
            x_ref, g_ref, dy_ref, dx_ref, dg_ref = refs
        xv = x_ref[...].astype(F32)
        r = lax.rsqrt(jnp.mean(xv * xv, axis=-1, keepdims=True) + EPS)
        xh = xv * r
        dyv = dy_ref[...].astype(F32)
        dxh = dyv * g_ref[...]
        dx = r * (dxh - xh * jnp.mean(dxh * xh, axis=-1, keepdims=True))
        if has_res:
            dx = dx + res_ref[...].astype(F32)
        dx_ref[...] = dx.astype(out_dtype)
        part = jnp.sum(dyv * xh, axis=0, keepdims=True)

        @pl.when(pl.program_id(0) == 0)
        def _():
            dg_ref[...] = part

        @pl.when(pl.program_id(0) > 0)
        def _():
            dg_ref[...] += part

    row = pl.BlockSpec((tm, width), lambda i: (i, 0))
    ins = [x, g.reshape(1, width), dy] + ([res] if has_res else [])
    in_specs = [pl.BlockSpec((tm, width), lambda i: (i, cb)), pl.BlockSpec((1, width), lambda i: (0, 0)), row] + ([row] if has_res else [])
    return _pallas(
        body, name=name, grid=(T // tm,), in_specs=in_specs,
        out_specs=[row, pl.BlockSpec((1, width), lambda i: (0, 0))],
        out_shape=[jax.ShapeDtypeStruct((T, width), out_dtype), jax.ShapeDtypeStruct((1, width), F32)],
        compiler_params=_params("arbitrary"),
    )(*ins)


def _rope_tables(T):
    half = A_ROPE // 2
    inv = ROPE_THETA ** (-jnp.arange(half, dtype=F32) / half)
    ang = jnp.arange(T, dtype=jnp.int32).astype(F32)[:, None] * inv[None, :]
    c32 = jnp.concatenate([jnp.cos(ang), jnp.cos(ang)], axis=-1)
    s32 = jnp.concatenate([jnp.sin(ang), jnp.sin(ang)], axis=-1)
    pad = A_PAD - A_QK
    cq = jnp.tile(jnp.concatenate([jnp.ones((T, A_NOPE), F32), c32, jnp.ones((T, pad), F32)], axis=-1), (1, A_HEADS))
    sq = jnp.tile(jnp.concatenate([jnp.zeros((T, A_NOPE), F32), s32, jnp.zeros((T, pad), F32)], axis=-1), (1, A_HEADS))
    ck = jnp.concatenate([c32, s32, jnp.zeros((T, LANES - 2 * A_ROPE), F32)], axis=-1)
    ck_t = jnp.concatenate([c32, s32], axis=-1).T
    return cq, sq, ck, ck_t


def _rope_swap_cols(w):
    half = A_ROPE // 2
    return jnp.concatenate([-w[..., half:], w[..., :half]], axis=-1)


def _rope_unswap_cols(g):
    half = A_ROPE // 2
    return jnp.concatenate([g[..., half:], -g[..., :half]], axis=-1)


A_PAD = LANES
A_W = A_HEADS * A_PAD
LOG2E = 1.4426950408889634
LN2 = 0.6931471805599453
Q_SCALE = A_QK ** -0.5 * LOG2E


def _qrope(q2, cq, sq, *, name):
    T = q2.shape[0]
    W = A_W
    tm = _tile(T, 512, 8)

    def body(a_ref, b_ref, c_ref, s_ref, o_ref):
        o_ref[...] = ((a_ref[...] * c_ref[...] + b_ref[...] * s_ref[...]) * Q_SCALE).astype(BF16)

    blk = lambda j: pl.BlockSpec((tm, W), lambda i: (i, j))
    return _pallas(body, name=name, grid=(T // tm,), in_specs=[blk(0), blk(1), blk(0), blk(0)], out_specs=blk(0),
                   out_shape=jax.ShapeDtypeStruct((T, W), BF16), compiler_params=_params("parallel"))(q2, q2, cq, sq)


def _kprep(kv, za, ck, *, name):
    T = kv.shape[0]
    tm = _tile(T, 512, 8)

    def body(kv_ref, kr_ref, ck_ref, k_ref, vx_ref):
        t = kr_ref[...] * ck_ref[...]
        krope = (t[:, 0:A_ROPE] + t[:, A_ROPE:2 * A_ROPE]).astype(BF16)
        one = (lax.broadcasted_iota(jnp.int32, (tm, A_PAD - A_V), 1) == 0).astype(BF16)
        for h in range(A_HEADS):
            k_ref[:, A_PAD * h:A_PAD * h + A_NOPE] = kv_ref[:, A_NOPE * h:A_NOPE * (h + 1)]
            k_ref[:, A_PAD * h + A_NOPE:A_PAD * h + A_QK] = krope
            k_ref[:, A_PAD * h + A_QK:A_PAD * (h + 1)] = jnp.zeros((tm, A_PAD - A_QK), BF16)
            vx_ref[:, A_PAD * h:A_PAD * h + A_V] = kv_ref[:, 512 + A_V * h:512 + A_V * (h + 1)]
            vx_ref[:, A_PAD * h + A_V:A_PAD * (h + 1)] = one

    wide = pl.BlockSpec((tm, A_W), lambda i: (i, 0))
    return _pallas(
        body, name=name, grid=(T // tm,),
        in_specs=[wide, pl.BlockSpec((tm, LANES), lambda i: (i, 3)), pl.BlockSpec((tm, LANES), lambda i: (i, 0))],
        out_specs=[wide, wide], out_shape=[jax.ShapeDtypeStruct((T, A_W), BF16)] * 2,
        compiler_params=_params("parallel"))(kv, za, ck)


def _kprep_bwd(dkt, ck_t, *, name):
    T = dkt.shape[1]
    tc = _tile(T, 512)

    def body(dk_ref, ck_ref, dn_ref, dr_ref):
        acc = jnp.zeros((A_ROPE, tc), F32)
        for h in range(A_HEADS):
            dn_ref[A_NOPE * h:A_NOPE * (h + 1), :] = dk_ref[A_PAD * h:A_PAD * h + A_NOPE, :].astype(BF16)
            acc = acc + dk_ref[A_PAD * h + A_NOPE:A_PAD * h + A_QK, :]
        dr_ref[0:A_ROPE, :] = (acc * ck_ref[0:A_ROPE, :]).astype(BF16)
        dr_ref[A_ROPE:2 * A_ROPE, :] = (acc * ck_ref[A_ROPE:2 * A_ROPE, :]).astype(BF16)
        dr_ref[2 * A_ROPE:LANES, :] = jnp.zeros((LANES - 2 * A_ROPE, tc), BF16)

    col = lambda r: pl.BlockSpec((r, tc), lambda i: (0, i))
    return _pallas(
        body, name=name, grid=(T // tc,), in_specs=[col(A_W), col(2 * A_ROPE)], out_specs=[col(512), col(LANES)],
        out_shape=[jax.ShapeDtypeStruct((512, T), BF16), jax.ShapeDtypeStruct((LANES, T), BF16)],
        compiler_params=_params("parallel"))(dkt, ck_t)


def _flash_fwd(qs, k, vxt, *, name):
    T = qs.shape[0]
    tq, tk = _tile(T, 512), _tile(T, 2048)
    nk = T // tk
    H, P, DV = A_HEADS, A_PAD, A_V

    def body(q_ref, k_ref, v_ref, o_ref, lse_ref, m_sc, acc_sc):
        j = pl.program_id(1)

        @pl.when(j == 0)
        def _():
            m_sc[...] = jnp.full(m_sc.shape, -jnp.inf, F32)
            acc_sc[...] = jnp.zeros(acc_sc.shape, F32)

        def scores(h):
            return _dot(k_ref[:, P * h:P * (h + 1)], q_ref[:, P * h:P * (h + 1)], _NT)

        st_next = scores(0)
        for h in range(H):
            st = st_next
            if h + 1 < H:
                st_next = scores(h + 1)
            m_prev = m_sc[h]
            m_new = jnp.maximum(m_prev, jnp.max(st, axis=0, keepdims=True))
            pt = jnp.exp2(st - m_new).astype(BF16)
            acc_sc[h] = jnp.exp2(m_prev - m_new) * acc_sc[h] + _dot(v_ref[P * h:P * (h + 1), :], pt, _NN)
            m_sc[h] = m_new

        @pl.when(j == nk - 1)
        def _():
            for h in range(H):
                acc = acc_sc[h]
                l = acc[DV:DV + 1, :]
                o_ref[:, DV * h:DV * (h + 1)] = (acc[0:DV, :] / l).T
                lse_ref[h] = m_sc[h] + jnp.log2(l)

    return _pallas(
        body, name=name, grid=(T // tq, nk),
        in_specs=[pl.BlockSpec((tq, A_W), lambda i, j: (i, 0)), pl.BlockSpec((tk, A_W), lambda i, j: (j, 0)),
                  pl.BlockSpec((A_W, tk), lambda i, j: (0, j))],
        out_specs=[pl.BlockSpec((tq, H * DV), lambda i, j: (i, 0)), pl.BlockSpec((H, 1, tq), lambda i, j: (0, 0, i))],
        out_shape=[jax.ShapeDtypeStruct((T, H * DV), F32), jax.ShapeDtypeStruct((H, 1, T), F32)],
        scratch_shapes=[pltpu.VMEM((H, 1, tq), F32), pltpu.VMEM((H, P, tq), F32)],
        compiler_params=_params("parallel", "arbitrary"))(qs, k, vxt)


def _attn_delta(o, do, *, name):
    T = o.shape[0]
    tm = _tile(T, 512, 8)

    def body(o_ref, do_ref, d_ref):
        prod = o_ref[...] * do_ref[...].astype(F32)
        for h in range(A_HEADS):
            d_ref[h] = jnp.sum(prod[:, A_V * h:A_V * (h + 1)], axis=-1, keepdims=True)

    row = pl.BlockSpec((tm, A_HEADS * A_V), lambda i: (i, 0))
    return _pallas(body, name=name, grid=(T // tm,), in_specs=[row, row],
                   out_specs=pl.BlockSpec((A_HEADS, tm, 1), lambda i: (0, i, 0)),
                   out_shape=jax.ShapeDtypeStruct((A_HEADS, T, 1), F32), compiler_params=_params("parallel"))(o, do)


def _flash_bwd(qs, qst, k, kv, do, dot_, lse2, delta, *, tiles=None, name):
    T = qs.shape[0]
    tq, tk = tiles or (_tile(T, 512), _tile(T, 1024))
    nq, nk = T // tq, T // tk
    H, P, DV = A_HEADS, A_PAD, A_V

    def body(q_ref, qt_ref, k_ref, v_ref, do_ref, dot_ref, lse_ref, delta_ref, dq_ref, dkt_ref, dvt_ref, dkt_sc, dvt_sc):
        i = pl.program_id(1)

        @pl.when(i == 0)
        def _():
            dkt_sc[...] = jnp.zeros(dkt_sc.shape, F32)
            dvt_sc[...] = jnp.zeros(dvt_sc.shape, F32)

        for h in range(H):
            s = _dot(q_ref[:, P * h:P * (h + 1)], k_ref[:, P * h:P * (h + 1)], _NT)
            dp = _dot(do_ref[:, DV * h:DV * (h + 1)], v_ref[:, DV * h:DV * (h + 1)], _NT)
            p = jnp.exp2(s - lse_ref[h])
            ds = (p * (dp - delta_ref[h])).astype(BF16)
            pb = p.astype(BF16)
            dq_ref[0, :, P * h:P * (h + 1)] = _dot(ds, k_ref[:, P * h:P * (h + 1)], _NN)
            dkt_sc[h] += _dot(qt_ref[P * h:P * (h + 1), :], ds, _NN)
            dvt_sc[h] += _dot(dot_ref[DV * h:DV * (h + 1), :], pb, _NN)

        @pl.when(i == nq - 1)
        def _():
            for h in range(H):
                dkt_ref[P * h:P * (h + 1), :] = dkt_sc[h] * LN2
                dvt_ref[DV * h:DV * (h + 1), :] = dvt_sc[h].astype(BF16)

    qrow = lambda w: pl.BlockSpec((tq, w), lambda j, i: (i, 0))
    qcol = lambda r: pl.BlockSpec((r, tq), lambda j, i: (0, i))
    stat = pl.BlockSpec((H, tq, 1), lambda j, i: (0, i, 0))
    return _pallas(
        body, name=name, grid=(nk, nq),
        in_specs=[qrow(A_W), qcol(A_W), pl.BlockSpec((tk, A_W), lambda j, i: (j, 0)), pl.BlockSpec((tk, H * DV), lambda j, i: (j, 1)),
                  qrow(H * DV), qcol(H * DV), stat, stat],
        out_specs=[pl.BlockSpec((1, tq, A_W), lambda j, i: (j, i, 0)), pl.BlockSpec((A_W, tk), lambda j, i: (0, j)),
                   pl.BlockSpec((H * DV, tk), lambda j, i: (0, j))],
        out_shape=[jax.ShapeDtypeStruct((nk, T, A_W), F32), jax.ShapeDtypeStruct((A_W, T), F32),
                   jax.ShapeDtypeStruct((H * DV, T), BF16)],
        scratch_shapes=[pltpu.VMEM((H, P, tk), F32), pltpu.VMEM((H, DV, tk), F32)],
        compiler_params=_params("parallel", "arbitrary"))(qs, qst, k, kv, do, dot_, lse2, delta)


def _dq_sum(dq_part, cq, sq, *, name):
    n, T, W = dq_part.shape
    tm = _tile(T, 128, 8)

    def body(p_ref, c_ref, s_ref, o_ref):
        acc = p_ref[0]
        for j in range(1, n):
            acc = acc + p_ref[j]
        acc = acc * (A_QK ** -0.5)
        o_ref[:, 0:W] = (acc * c_ref[...]).astype(BF16)
        o_ref[:, W:2 * W] = (acc * s_ref[...]).astype(BF16)

    row = pl.BlockSpec((tm, W), lambda i: (i, 0))
    return _pallas(body, name=name, grid=(T // tm,), in_specs=[pl.BlockSpec((n, tm, W), lambda i: (0, i, 0)), row, row],
                   out_specs=pl.BlockSpec((tm, 2 * W), lambda i: (i, 0)), out_shape=jax.ShapeDtypeStruct((T, 2 * W), BF16),
                   compiler_params=_params("parallel"))(dq_part, cq, sq)


HB = 8 * B_CHUNK


def _chunk_masks(reverse):
    r = lax.broadcasted_iota(jnp.int32, (HB, HB), 0)
    c = lax.broadcasted_iota(jnp.int32, (HB, HB), 1)
    same = (r // B_CHUNK) == (c // B_CHUNK)
    incl = same & ((c >= r) if reverse else (c <= r))
    return same, incl


def _mask_mm(mask, x):
    hi = x.astype(BF16)
    lo = (x - hi.astype(F32)).astype(BF16)
    return _dot(mask, hi, _NN) + _dot(mask, lo, _NN)


def _hgrn_gates(q, z, lb, reverse):
    same, incl = _chunk_masks(reverse)
    sg = jax.nn.sigmoid(z)
    f = lb + (1.0 - lb) * sg
    lf = jnp.log(jnp.maximum(f, TINY))
    kk = (1.0 - lb) * jax.nn.sigmoid(-z)
    b = _mask_mm(incl.astype(BF16), lf)
    btot = _mask_mm(same.astype(BF16), lf)
    eb, enb, er, dec = jnp.exp(b), jnp.exp(-b), jnp.exp(btot - b), jnp.exp(btot)
    return dict(same=same, incl=incl, sg=sg, f=f, kk=kk, eb=eb, enb=enb, er=er, dec=dec,
                qd=q * eb, ki=kk * enb, ke=kk * er)


def _hgrn_specs(T, reverse, gate_reverse):
    nb = T // HB
    blk = (lambda i: nb - 1 - i) if reverse else (lambda i: i)
    wide = B_HEADS * B_DK
    return nb, blk, [
        pl.BlockSpec((HB, wide), lambda i: (blk(i), 0)),
        pl.BlockSpec((HB, wide), lambda i: (blk(i), 2 if gate_reverse else 1)),
        pl.BlockSpec((HB, B_HEADS * B_DV), lambda i: (blk(i), 6)),
        pl.BlockSpec((1, wide), lambda i: (0, 0)),
    ]


def _hk(h):
    return slice(B_DK * h, B_DK * (h + 1))


def _hv(h):
    return slice(B_DV * h, B_DV * (h + 1))


def _crows(c):
    return slice(B_CHUNK * c, B_CHUNK * (c + 1))


def _chunk_selectors():
    r = lax.broadcasted_iota(jnp.int32, (HB, 1), 0) // B_CHUNK
    l = lax.broadcasted_iota(jnp.int32, (1, HB), 1) // B_CHUNK
    return [r == c for c in range(8)], [l == c for c in range(8)]


def _hgrn_fwd(zb, lb, *, reverse, name):
    T = zb.shape[0]
    nb, blk, in_specs = _hgrn_specs(T, reverse, reverse)
    order = range(7, -1, -1) if reverse else range(8)
    heads = range(B_HEADS)

    def body(q_ref, z_ref, v_ref, lb_ref, o_ref, st_ref, s_sc):
        @pl.when(pl.program_id(0) == 0)
        def _():
            s_sc[...] = jnp.zeros(s_sc.shape, F32)

        g = _hgrn_gates(q_ref[...], z_ref[...], lb_ref[...], reverse)
        v = v_ref[...].astype(BF16)
        qd, ki, ke = g["qd"].astype(BF16), g["ki"].astype(BF16), g["ke"].astype(BF16)
        dec = g["dec"]
        in_chunk_rows, in_chunk_lanes = _chunk_selectors()
        o_intra, upd = [], []
        for h in heads:
            a = jnp.where(g["incl"], _dot(qd[:, _hk(h)], ki[:, _hk(h)], _NT), 0.0)
            o_intra.append(_dot(a.astype(BF16), v[:, _hv(h)], _NN))
            vt = v[:, _hv(h)].T
            lhs = jnp.concatenate([jnp.where(in_chunk_lanes[c], vt, 0) for c in range(8)], axis=0)
            upd.append(_dot(lhs, ke[:, _hk(h)], _NN))
        st = [s_sc[h] for h in heads]
        snap = [[None] * 8 for _ in heads]
        for c in order:
            for h in heads:
                snap[h][c] = st[h]
                st[h] = st[h] * dec[B_CHUNK * c:B_CHUNK * c + 1, _hk(h)] + upd[h][B_DV * c:B_DV * (c + 1), :]
        for h in heads:
            s_sc[h] = st[h]
            for c in range(8):
                st_ref[h, c] = snap[h][c]
            qd_big = jnp.concatenate([jnp.where(in_chunk_rows[c], qd[:, _hk(h)], 0) for c in range(8)], axis=1)
            states = jnp.concatenate([snap[h][c].astype(BF16) for c in range(8)], axis=1)
            o_ref[h] = o_intra[h] + _dot(qd_big, states, _NT)

    return _pallas(
        body, name=name, grid=(nb,), in_specs=in_specs,
        out_specs=[pl.BlockSpec((B_HEADS, HB, B_DV), lambda i: (0, blk(i), 0)),
                   pl.BlockSpec((B_HEADS, 8, B_DV, B_DK), lambda i: (0, blk(i), 0, 0))],
        out_shape=[jax.ShapeDtypeStruct((B_HEADS, T, B_DV), F32),
                   jax.ShapeDtypeStruct((B_HEADS, T // B_CHUNK, B_DV, B_DK), F32)],
        scratch_shapes=[pltpu.VMEM((B_HEADS, B_DV, B_DK), F32)],
        compiler_params=_params("arbitrary"))(zb, zb, zb, lb)


def _hgrn_bwd(zb, lb, do, states, *, reverse, name):
    T = zb.shape[0]
    nb, blk, in_specs = _hgrn_specs(T, not reverse, reverse)
    order = range(8) if reverse else range(7, -1, -1)
    heads = range(B_HEADS)

    def body(q_ref, z_ref, v_ref, lb_ref, do_ref, st_ref, dq_ref, dz_ref, dv_ref, dlb_ref, ds_sc):
        @pl.when(pl.program_id(0) == 0)
        def _():
            ds_sc[...] = jnp.zeros(ds_sc.shape, F32)
            dlb_ref[...] = jnp.zeros(dlb_ref.shape, F32)

        lb = lb_ref[...]
        g = _hgrn_gates(q_ref[...], z_ref[...], lb, reverse)
        v = v_ref[...].astype(BF16)
        qd, ki, ke = g["qd"].astype(BF16), g["ki"].astype(BF16), g["ke"].astype(BF16)
        dec = g["dec"]
        dout = [do_ref[h].astype(BF16) for h in heads]
        in_chunk_rows, in_chunk_lanes = _chunk_selectors()
        _, incl_t = _chunk_masks(not reverse)
        rows_of = lambda x: jnp.concatenate([jnp.where(in_chunk_rows[c], x, 0) for c in range(8)], axis=1)
        dv_i, dqd_h, dki, upd = [], [], [], []
        for h in heads:
            qd_h, ki_h, v_h = qd[:, _hk(h)], ki[:, _hk(h)], v[:, _hv(h)]
            da = jnp.where(g["incl"], _dot(dout[h], v_h, _NT), 0.0).astype(BF16)
            at = jnp.where(incl_t, _dot(ki_h, qd_h, _NT), 0.0).astype(BF16)
            dat = jnp.where(incl_t, _dot(v_h, dout[h], _NT), 0.0).astype(BF16)
            dv_i.append(_dot(at, dout[h], _NN))
            dki.append(_dot(dat, qd_h, _NN))
            dot_t = dout[h].T
            lhs = jnp.concatenate([jnp.where(in_chunk_lanes[c], dot_t, 0) for c in range(8)], axis=0)
            upd.append(_dot(lhs, qd_h, _NN))
            saved = jnp.concatenate([st_ref[h, c].astype(BF16) for c in range(8)], axis=0)
            dqd_h.append(_dot(da, ki_h, _NN) + _dot(rows_of(dout[h]), saved, _NN))
        dst = [ds_sc[h] for h in heads]
        used = [[None] * 8 for _ in heads]
        for c in order:
            for h in heads:
                used[h][c] = dst[h]
                dst[h] = dst[h] * dec[B_CHUNK * c:B_CHUNK * c + 1, _hk(h)] + upd[h][B_DV * c:B_DV * (c + 1), :]
        dke_h, ddec_h = [], []
        for h in heads:
            ds_sc[h] = dst[h]
            used16 = [used[h][c].astype(BF16) for c in range(8)]
            dv_ref[h] = dv_i[h] + _dot(rows_of(ke[:, _hk(h)]), jnp.concatenate(used16, axis=1), _NT)
            dke_h.append(_dot(rows_of(v[:, _hv(h)]), jnp.concatenate(used16, axis=0), _NN))
            ddec_p = []
            for c in range(8):
                tot = jnp.sum(used[h][c] * st_ref[h, c], axis=0, keepdims=True) * dec[B_CHUNK * c:B_CHUNK * c + 1, _hk(h)]
                ddec_p.append(jnp.broadcast_to(tot, (B_CHUNK, B_DK)))
            ddec_h.append(jnp.concatenate(ddec_p, axis=0))
        dqd = jnp.concatenate(dqd_h, axis=1)
        dki = jnp.concatenate(dki, axis=1)
        dke = jnp.concatenate(dke_h, axis=1)
        db = dqd * g["qd"] - dki * g["ki"] - dke * g["ke"]
        _, incl_t = _chunk_masks(not reverse)
        dlf = (_mask_mm(incl_t.astype(BF16), db) + _mask_mm(g["same"].astype(BF16), dke * g["ke"])
               + jnp.concatenate(ddec_h, axis=1))
        dk = dki * g["enb"] + dke * g["er"]
        u = jnp.where(g["f"] > TINY, dlf / g["f"], 0.0) - dk
        sg = g["sg"]
        dq_ref[...] = dqd * g["eb"]
        dz_ref[...] = u * (1.0 - lb) * sg * (1.0 - sg)
        dlb_ref[...] += jnp.sum(u * (1.0 - sg), axis=0, keepdims=True)

    wide = pl.BlockSpec((HB, B_HEADS * B_DK), lambda i: (blk(i), 0))
    hm = pl.BlockSpec((B_HEADS, HB, B_DV), lambda i: (0, blk(i), 0))
    return _pallas(
        body, name=name, grid=(nb,),
        in_specs=in_specs + [hm, pl.BlockSpec((B_HEADS, 8, B_DV, B_DK), lambda i: (0, blk(i), 0, 0))],
        out_specs=[wide, wide, hm, pl.BlockSpec((1, B_HEADS * B_DK), lambda i: (0, 0))],
        out_shape=[jax.ShapeDtypeStruct((T, B_HEADS * B_DK), F32), jax.ShapeDtypeStruct((T, B_HEADS * B_DK), F32),
                   jax.ShapeDtypeStruct((B_HEADS, T, B_DV), F32), jax.ShapeDtypeStruct((1, B_HEADS * B_DK), F32)],
        scratch_shapes=[pltpu.VMEM((B_HEADS, B_DV, B_DK), F32)],
        compiler_params=_params("arbitrary"))(zb, zb, zb, lb, do, states)


def _hgrn_out(of, ob, zb, gout, *, name):
    T = zb.shape[0]
    tm = _tile(T, 512, 8)

    def body(of_ref, ob_ref, g_ref, gout_ref, y_ref):
        for h in range(B_HEADS):
            o = of_ref[h] + ob_ref[h]
            r = lax.rsqrt(jnp.mean(o * o, axis=-1, keepdims=True) + EPS)
            gh = g_ref[:, B_DV * h:B_DV * (h + 1)]
            y_ref[:, B_DV * h:B_DV * (h + 1)] = (o * r * gout_ref[...] * (gh * jax.nn.sigmoid(gh))).astype(BF16)

    hm = pl.BlockSpec((B_HEADS, tm, B_DV), lambda i: (0, i, 0))
    return _pallas(
        body, name=name, grid=(T // tm,),
        in_specs=[hm, hm, pl.BlockSpec((tm, 512), lambda i: (i, 7)), pl.BlockSpec((1, B_DV), lambda i: (0, 0))],
        out_specs=pl.BlockSpec((tm, 512), lambda i: (i, 0)),
        out_shape=jax.ShapeDtypeStruct((T, 512), BF16), compiler_params=_params("parallel"))(of, ob, zb, gout.reshape(1, B_DV))


def _hgrn_out_bwd(of, ob, zb, gout, dy, *, name):
    T = zb.shape[0]
    tm = _tile(T, 512, 8)

    def body(of_ref, ob_ref, g_ref, gout_ref, dy_ref, do_ref, dg_ref, dgo_ref):
        gout_v = gout_ref[...]
        acc = jnp.zeros((1, B_DV), F32)
        for h in range(B_HEADS):
            o = of_ref[h] + ob_ref[h]
            r = lax.rsqrt(jnp.mean(o * o, axis=-1, keepdims=True) + EPS)
            oh = o * r
            gh = g_ref[:, B_DV * h:B_DV * (h + 1)]
            sg = jax.nn.sigmoid(gh)
            dyh = dy_ref[:, B_DV * h:B_DV * (h + 1)].astype(F32)
            dn = dyh * (gh * sg)
            dg_ref[:, B_DV * h:B_DV * (h + 1)] = dyh * (oh * gout_v) * (sg * (1.0 + gh * (1.0 - sg)))
            dxh = dn * gout_v
            do_ref[h] = r * (dxh - oh * jnp.mean(dxh * oh, axis=-1, keepdims=True))
            acc = acc + jnp.sum(dn * oh, axis=0, keepdims=True)

        @pl.when(pl.program_id(0) == 0)
        def _():
            dgo_ref[...] = acc

        @pl.when(pl.program_id(0) > 0)
        def _():
            dgo_ref[...] += acc

    hm = pl.BlockSpec((B_HEADS, tm, B_DV), lambda i: (0, i, 0))
    row = pl.BlockSpec((tm, 512), lambda i: (i, 0))
    return _pallas(
        body, name=name, grid=(T // tm,),
        in_specs=[hm, hm, pl.BlockSpec((tm, 512), lambda i: (i, 7)), pl.BlockSpec((1, B_DV), lambda i: (0, 0)), row],
        out_specs=[hm, row, pl.BlockSpec((1, B_DV), lambda i: (0, 0))],
        out_shape=[jax.ShapeDtypeStruct((B_HEADS, T, B_DV), F32), jax.ShapeDtypeStruct((T, 512), F32),
                   jax.ShapeDtypeStruct((1, B_DV), F32)],
        compiler_params=_params("arbitrary"))(of, ob, zb, gout.reshape(1, B_DV), dy)


def _dzb_assemble(dq_f, dq_b, dzf, dzb_, dv_f, dv_b, dgate, *, name):
    T = dq_f.shape[0]
    tm = _tile(T, 256, 8)

    def body(qf, qb, zf, zr, vf, vr, dg, o_ref):
        o_ref[:, 0:1024] = (qf[...] + qb[...]).astype(BF16)
        o_ref[:, 1024:2048] = zf[...].astype(BF16)
        o_ref[:, 2048:3072] = zr[...].astype(BF16)
        for h in range(B_HEADS):
            o_ref[:, 3072 + B_DV * h:3072 + B_DV * (h + 1)] = (vf[h] + vr[h]).astype(BF16)
        o_ref[:, 3584:4096] = dg[...].astype(BF16)

    wide = pl.BlockSpec((tm, 1024), lambda i: (i, 0))
    hm = pl.BlockSpec((B_HEADS, tm, B_DV), lambda i: (0, i, 0))
    return _pallas(
        body, name=name, grid=(T // tm,),
        in_specs=[wide, wide, wide, wide, hm, hm, pl.BlockSpec((tm, 512), lambda i: (i, 0))],
        out_specs=pl.BlockSpec((tm, 4096), lambda i: (i, 0)),
        out_shape=jax.ShapeDtypeStruct((T, 4096), BF16), compiler_params=_params("parallel"))(dq_f, dq_b, dzf, dzb_, dv_f, dv_b, dgate)


C_SPAN = 3 * C_BLOCK
C_G = C_HEADS // C_KV_HEADS


def _t5_bucket(rel):
    nb = REL_BUCKETS // 2
    max_exact = nb // 2
    ret = (rel > 0).astype(jnp.int32) * nb
    n = jnp.abs(rel)
    large = max_exact + (jnp.log(jnp.maximum(n, 1).astype(F32) / max_exact)
                         / math.log(REL_MAX_DIST / max_exact) * (nb - max_exact)).astype(jnp.int32)
    large = jnp.minimum(large, nb - 1)
    return ret + jnp.where(n < max_exact, n, large)


def _swa_buckets():
    rel = jnp.arange(C_SPAN)[None, :] - C_BLOCK - jnp.arange(C_BLOCK)[:, None]
    return _t5_bucket(rel)


def _swa_specs(T):
    nb = T // C_BLOCK
    return nb, [
        pl.BlockSpec((C_BLOCK, 512), lambda n: (n, 0)),
        pl.BlockSpec((C_BLOCK, LANES), lambda n: (jnp.maximum(n - 1, 0), 4)),
        pl.BlockSpec((C_BLOCK, LANES), lambda n: (n, 4)),
        pl.BlockSpec((C_BLOCK, LANES), lambda n: (jnp.minimum(n + 1, nb - 1), 4)),
        pl.BlockSpec((C_BLOCK, LANES), lambda n: (jnp.maximum(n - 1, 0), 5)),
        pl.BlockSpec((C_BLOCK, LANES), lambda n: (n, 5)),
        pl.BlockSpec((C_BLOCK, LANES), lambda n: (jnp.minimum(n + 1, nb - 1), 5)),
        pl.BlockSpec((C_HEADS, C_BLOCK, C_SPAN), lambda n: (0, 0, 0)),
        pl.BlockSpec(memory_space=pltpu.SMEM),
    ]


def _swa_valid(n, T):
    qi = lax.broadcasted_iota(jnp.int32, (C_BLOCK, C_SPAN), 0)
    si = lax.broadcasted_iota(jnp.int32, (C_BLOCK, C_SPAN), 1)
    rel = si - C_BLOCK - qi
    kpos = (n - 1) * C_BLOCK + si
    return (jnp.abs(rel) <= C_WINDOW) & (kpos >= 0) & (kpos < T)


def _swa_softmax(raw, bias, valid, sink):
    s = raw * (C_DH ** -0.5) + bias
    s = jnp.where(valid, s, MASK_VALUE)
    m = jnp.maximum(jnp.max(s, axis=-1, keepdims=True), sink)
    e = jnp.exp(s - m)
    den = jnp.sum(e, axis=-1, keepdims=True) + jnp.exp(sink - m)
    return e / den, jnp.exp(sink - m) / den


def _swa_fwd(zc, bias, sink, *, name):
    T = zc.shape[0]
    nb, in_specs = _swa_specs(T)

    def body(q_ref, kp, kc, kn, vp, vc, vn, bias_ref, sink_ref, y_ref):
        n = pl.program_id(0)
        kcat = jnp.concatenate([kp[...], kc[...], kn[...]], axis=0)
        vcat = jnp.concatenate([vp[...], vc[...], vn[...]], axis=0)
        valid = _swa_valid(n, T)
        heads = range(C_HEADS)
        kvs = [slice(C_DH * (h // C_G), C_DH * (h // C_G + 1)) for h in heads]
        scores = [_dot(q_ref[:, C_DH * h:C_DH * (h + 1)], kcat[:, kvs[h]], _NT) for h in heads]
        probs = [_swa_softmax(scores[h], bias_ref[h], valid, sink_ref[h])[0].astype(BF16) for h in heads]
        for h in heads:
            y_ref[:, C_DH * h:C_DH * (h + 1)] = _dot(probs[h], vcat[:, kvs[h]], _NN).astype(BF16)

    return _pallas(
        body, name=name, grid=(nb,), in_specs=in_specs, out_specs=pl.BlockSpec((C_BLOCK, 512), lambda n: (n, 0)),
        out_shape=jax.ShapeDtypeStruct((T, 512), BF16), compiler_params=_params("parallel"))(zc, zc, zc, zc, zc, zc, zc, bias, sink)


def _swa_bwd(zc, bias, sink, dy, *, name):
    T = zc.shape[0]
    nb, in_specs = _swa_specs(T)
    scale = C_DH ** -0.5

    def body(q_ref, kp, kc, kn, vp, vc, vn, bias_ref, sink_ref, dy_ref, dq_ref, dkc_ref, dvc_ref, dbias_ref, dsink_ref):
        n = pl.program_id(0)

        @pl.when(n == 0)
        def _():
            dbias_ref[...] = jnp.zeros(dbias_ref.shape, F32)
            dsink_ref[...] = jnp.zeros(dsink_ref.shape, F32)

        kcat = jnp.concatenate([kp[...], kc[...], kn[...]], axis=0)
        vcat = jnp.concatenate([vp[...], vc[...], vn[...]], axis=0)
        valid = _swa_valid(n, T)
        heads = range(C_HEADS)
        kvs = [slice(C_DH * (h // C_G), C_DH * (h // C_G + 1)) for h in heads]
        qs = [q_ref[:, C_DH * h:C_DH * (h + 1)] for h in heads]
        dos = [dy_ref[:, C_DH * h:C_DH * (h + 1)].astype(BF16) for h in heads]
        scores = [_dot(qs[h], kcat[:, kvs[h]], _NT) for h in heads]
        dps = [_dot(dos[h], vcat[:, kvs[h]], _NT) for h in heads]
        pbs, dsbs = [], []
        for h in heads:
            p, p_sink = _swa_softmax(scores[h], bias_ref[h], valid, sink_ref[h])
            rowdot = jnp.sum(p * dps[h], axis=-1, keepdims=True)
            ds = p * (dps[h] - rowdot)
            dbias_ref[h] += ds
            tot = jnp.sum(jnp.sum(-p_sink * rowdot, axis=0, keepdims=True), axis=1, keepdims=True)
            dsink_ref[h:h + 1, :] += jnp.broadcast_to(tot, (1, LANES))
            pbs.append(p.astype(BF16))
            dsbs.append((ds * scale).astype(BF16))
        for h in heads:
            dq_ref[:, C_DH * h:C_DH * (h + 1)] = _dot(dsbs[h], kcat[:, kvs[h]], _NN).astype(BF16)
        dks = [_dot(dsbs[h], qs[h], _TN) for h in heads]
        dvs = [_dot(pbs[h], dos[h], _TN) for h in heads]
        for kv in range(C_KV_HEADS):
            group = range(kv * C_G, (kv + 1) * C_G)
            dkc_ref[0, :, C_DH * kv:C_DH * (kv + 1)] = sum(dks[h] for h in group)
            dvc_ref[0, :, C_DH * kv:C_DH * (kv + 1)] = sum(dvs[h] for h in group)

    part = pl.BlockSpec((1, C_SPAN, LANES), lambda n: (n, 0, 0))
    dq, dkc, dvc, dbias, dsink = _pallas(
        body, name=name, grid=(nb,), in_specs=in_specs + [pl.BlockSpec((C_BLOCK, 512), lambda n: (n, 0))],
        out_specs=[pl.BlockSpec((C_BLOCK, 512), lambda n: (n, 0)), part, part,
                   pl.BlockSpec((C_HEADS, C_BLOCK, C_SPAN), lambda n: (0, 0, 0)), pl.BlockSpec((C_HEADS, LANES), lambda n: (0, 0))],
        out_shape=[jax.ShapeDtypeStruct((T, 512), BF16), jax.ShapeDtypeStruct((nb, C_SPAN, LANES), F32),
                   jax.ShapeDtypeStruct((nb, C_SPAN, LANES), F32), jax.ShapeDtypeStruct((C_HEADS, C_BLOCK, C_SPAN), F32),
                   jax.ShapeDtypeStruct((C_HEADS, LANES), F32)],
        compiler_params=_params("arbitrary"))(zc, zc, zc, zc, zc, zc, zc, bias, sink, dy)

    def combine(dq_ref, kp, kc, kn, vp, vc, vn, o_ref):
        n = pl.program_id(0)
        lo = (n > 0).astype(F32)
        hi = (n < nb - 1).astype(F32)
        o_ref[:, 0:512] = dq_ref[...]
        o_ref[:, 512:640] = (kp[0] * lo + kc[0] + kn[0] * hi).astype(BF16)
        o_ref[:, 640:768] = (vp[0] * lo + vc[0] + vn[0] * hi).astype(BF16)

    prev = pl.BlockSpec((1, C_BLOCK, LANES), lambda n: (jnp.maximum(n - 1, 0), 2, 0))
    cur = pl.BlockSpec((1, C_BLOCK, LANES), lambda n: (n, 1, 0))
    nxt = pl.BlockSpec((1, C_BLOCK, LANES), lambda n: (jnp.minimum(n + 1, nb - 1), 0, 0))
    dzc = _pallas(
        combine, name=name + "_combine", grid=(nb,),
        in_specs=[pl.BlockSpec((C_BLOCK, 512), lambda n: (n, 0)), prev, cur, nxt, prev, cur, nxt],
        out_specs=pl.BlockSpec((C_BLOCK, 768), lambda n: (n, 0)),
        out_shape=jax.ShapeDtypeStruct((T, 768), BF16), compiler_params=_params("parallel"))(dq, dkc, dkc, dkc, dvc, dvc, dvc)
    return dzc, dbias, dsink


def _merge_tiles(T):
    return _tile(T, 512, 8), 512


def _merge_fwd(ya, yb, yc, wa, wb, wc, zg, *, name):
    T = ya.shape[0]
    tm, tn = _merge_tiles(T)
    nd = D_MODEL // tn

    def body(ya_ref, yb_ref, yc_ref, wa_ref, wb_ref, wc_ref, ga_ref, gb_ref, gc_ref, o_ref):
        acc = jax.nn.sigmoid(ga_ref[...]) * _dot(ya_ref[...].astype(BF16), wa_ref[...], _NN)
        acc += jax.nn.sigmoid(gb_ref[...]) * _dot(yb_ref[...].astype(BF16), wb_ref[...], _NN)
        acc += jax.nn.sigmoid(gc_ref[...]) * _dot(yc_ref[...].astype(BF16), wc_ref[...], _NN)
        o_ref[...] = acc.astype(BF16)

    y = pl.BlockSpec((tm, 512), lambda i, j: (i, 0))
    w = pl.BlockSpec((512, tn), lambda i, j: (0, j))
    gate = lambda b: pl.BlockSpec((tm, tn), lambda i, j: (i, b * nd + j))
    return _pallas(
        body, name=name, grid=(T // tm, nd), in_specs=[y, y, y, w, w, w, gate(0), gate(1), gate(2)],
        out_specs=pl.BlockSpec((tm, tn), lambda i, j: (i, j)),
        out_shape=jax.ShapeDtypeStruct((T, D_MODEL), BF16),
        compiler_params=_params("parallel", "parallel"))(ya, yb, yc, wa, wb, wc, zg, zg, zg)


def _merge_bwd(ya, yb, yc, wa, wb, wc, zg, dm, *, name):
    T = ya.shape[0]
    tm, tn = _merge_tiles(T)
    nd = D_MODEL // tn

    def body(ya_ref, yb_ref, yc_ref, wa_ref, wb_ref, wc_ref, ga_ref, gb_ref, gc_ref, dm_ref, *outs):
        dmv = dm_ref[...].astype(F32)
        for y_ref, w_ref, g_ref, du_ref, dg_ref in zip((ya_ref, yb_ref, yc_ref), (wa_ref, wb_ref, wc_ref),
                                                       (ga_ref, gb_ref, gc_ref), outs[:3], outs[3:]):
            u = _dot(y_ref[...].astype(BF16), w_ref[...], _NN)
            sg = jax.nn.sigmoid(g_ref[...])
            du_ref[...] = (dmv * sg).astype(BF16)
            dg_ref[...] = (dmv * u * sg * (1.0 - sg)).astype(BF16)

    y = pl.BlockSpec((tm, 512), lambda i, j: (i, 0))
    w = pl.BlockSpec((512, tn), lambda i, j: (0, j))
    gate = lambda b: pl.BlockSpec((tm, tn), lambda i, j: (i, b * nd + j))
    t = pl.BlockSpec((tm, tn), lambda i, j: (i, j))
    return _pallas(
        body, name=name, grid=(T // tm, nd), in_specs=[y, y, y, w, w, w, gate(0), gate(1), gate(2), t],
        out_specs=[t] * 6, out_shape=[jax.ShapeDtypeStruct((T, D_MODEL), BF16)] * 6,
        compiler_params=_params("parallel", "parallel"))(ya, yb, yc, wa, wb, wc, zg, zg, zg, dm)


def _cross_fwd(q, kvm, *, name):
    T = q.shape[0]
    M = kvm.shape[0]
    tm = _tile(T, 512, 8)
    scale = X_DH ** -0.5

    def body(q_ref, k_ref, v_ref, o_ref):
        for h in range(X_HEADS):
            cs = slice(X_DH * h, X_DH * (h + 1))
            s = _dot(q_ref[:, cs], k_ref[:, cs], _NT) * scale
            e = jnp.exp(s - jnp.max(s, axis=-1, keepdims=True))
            p = e / jnp.sum(e, axis=-1, keepdims=True)
            o_ref[:, cs] = _dot(p.astype(BF16), v_ref[:, cs], _NN).astype(BF16)

    row = pl.BlockSpec((tm, D_MODEL), lambda i: (i, 0))
    return _pallas(
        body, name=name, grid=(T // tm,),
        in_specs=[row, pl.BlockSpec((M, D_MODEL), lambda i: (0, 0)), pl.BlockSpec((M, D_MODEL), lambda i: (0, 1))],
        out_specs=row, out_shape=jax.ShapeDtypeStruct((T, D_MODEL), BF16), compiler_params=_params("parallel"))(q, kvm, kvm)


def _cross_bwd(q, kvm, do, *, name):
    T = q.shape[0]
    M = kvm.shape[0]
    tm = _tile(T, 512, 8)
    scale = X_DH ** -0.5

    def body(q_ref, k_ref, v_ref, do_ref, dq_ref, dkv_ref):
        @pl.when(pl.program_id(0) == 0)
        def _():
            dkv_ref[...] = jnp.zeros(dkv_ref.shape, F32)

        for h in range(X_HEADS):
            cs = slice(X_DH * h, X_DH * (h + 1))
            vs = slice(D_MODEL + X_DH * h, D_MODEL + X_DH * (h + 1))
            qh, kh, doh = q_ref[:, cs], k_ref[:, cs], do_ref[:, cs]
            s = _dot(qh, kh, _NT) * scale
            e = jnp.exp(s - jnp.max(s, axis=-1, keepdims=True))
            p = e / jnp.sum(e, axis=-1, keepdims=True)
            dp = _dot(doh, v_ref[:, cs], _NT)
            ds = (p * (dp - jnp.sum(p * dp, axis=-1, keepdims=True)) * scale).astype(BF16)
            dq_ref[:, cs] = _dot(ds, kh, _NN).astype(BF16)
            dkv_ref[:, cs] += _dot(ds, qh, _TN)
            dkv_ref[:, vs] += _dot(p.astype(BF16), doh, _TN)

    row = pl.BlockSpec((tm, D_MODEL), lambda i: (i, 0))
    return _pallas(
        body, name=name, grid=(T // tm,),
        in_specs=[row, pl.BlockSpec((M, D_MODEL), lambda i: (0, 0)), pl.BlockSpec((M, D_MODEL), lambda i: (0, 1)), row],
        out_specs=[row, pl.BlockSpec((M, 2 * D_MODEL), lambda i: (0, 0))],
        out_shape=[jax.ShapeDtypeStruct((T, D_MODEL), BF16), jax.ShapeDtypeStruct((M, 2 * D_MODEL), F32)],
        compiler_params=_params("arbitrary"))(q, kvm, kvm, do)


def _ffn_up(h, w1, w3, *, name):
    T = h.shape[0]
    tm = _tile(T, 512, 8)
    tn = D_FF // 2

    def body(h_ref, w1_ref, w3_ref, a_ref, b_ref, act_ref):
        hv = h_ref[...]
        a = _dot(hv, w1_ref[...], _NN)
        b = _dot(hv, w3_ref[...], _NN)
        a_ref[...] = a
        b_ref[...] = b
        act_ref[...] = (a * jax.nn.sigmoid(a) * b).astype(BF16)

    w = pl.BlockSpec((D_MODEL, tn), lambda i, j: (0, j))
    t = pl.BlockSpec((tm, tn), lambda i, j: (i, j))
    return _pallas(
        body, name=name, grid=(T // tm, D_FF // tn), in_specs=[pl.BlockSpec((tm, D_MODEL), lambda i, j: (i, 0)), w, w],
        out_specs=[t, t, t],
        out_shape=[jax.ShapeDtypeStruct((T, D_FF), F32), jax.ShapeDtypeStruct((T, D_FF), F32), jax.ShapeDtypeStruct((T, D_FF), BF16)],
        compiler_params=_params("parallel", "parallel"))(h, w1, w3)


def _ffn_dact(dx, w2, a, b, *, name):
    T = dx.shape[0]
    tm = _tile(T, 512, 8)
    tn = D_FF // 2

    def body(dx_ref, w2_ref, a_ref, b_ref, da_ref, db_ref):
        dact = _dot(dx_ref[...].astype(BF16), w2_ref[...], _NT)
        av = a_ref[...]
        sg = jax.nn.sigmoid(av)
        da_ref[...] = (dact * b_ref[...] * (sg * (1.0 + av * (1.0 - sg)))).astype(BF16)
        db_ref[...] = (dact * (av * sg)).astype(BF16)

    t = pl.BlockSpec((tm, tn), lambda i, j: (i, j))
    return _pallas(
        body, name=name, grid=(T // tm, D_FF // tn),
        in_specs=[pl.BlockSpec((tm, D_MODEL), lambda i, j: (i, 0)), pl.BlockSpec((tn, D_MODEL), lambda i, j: (j, 0)), t, t],
        out_specs=[t, t], out_shape=[jax.ShapeDtypeStruct((T, D_FF), BF16)] * 2,
        compiler_params=_params("parallel", "parallel"))(dx, w2, a, b)


def _loss_head(x, g, target, *, name):
    T, D = x.shape
    tm = _tile(T, 512, 8)

    def body(x_ref, g_ref, t_ref, loss_ref, dx_ref, dg_ref):
        xv = x_ref[...]
        r = lax.rsqrt(jnp.mean(xv * xv, axis=-1, keepdims=True) + EPS)
        xh = xv * r
        gv = g_ref[...]
        err = xh * gv - t_ref[...]
        dy = err * (1.0 / D)
        dxh = dy * gv
        dx_ref[...] = r * (dxh - xh * jnp.mean(dxh * xh, axis=-1, keepdims=True))
        lpart = 0.5 * jnp.sum(jnp.mean(err * err, axis=-1, keepdims=True), axis=0, keepdims=True)
        gpart = jnp.sum(dy * xh, axis=0, keepdims=True)

        @pl.when(pl.program_id(0) == 0)
        def _():
            loss_ref[...] = jnp.broadcast_to(lpart, (1, LANES))
            dg_ref[...] = gpart

        @pl.when(pl.program_id(0) > 0)
        def _():
            loss_ref[...] += jnp.broadcast_to(lpart, (1, LANES))
            dg_ref[...] += gpart

    row = pl.BlockSpec((tm, D), lambda i: (i, 0))
    vec = pl.BlockSpec((1, D), lambda i: (0, 0))
    return _pallas(
        body, name=name, grid=(T // tm,), in_specs=[row, vec, row],
        out_specs=[pl.BlockSpec((1, LANES), lambda i: (0, 0)), row, vec],
        out_shape=[jax.ShapeDtypeStruct((1, LANES), F32), jax.ShapeDtypeStruct((T, D), F32), jax.ShapeDtypeStruct((1, D), F32)],
        compiler_params=_params("arbitrary"))(x, g.reshape(1, D), target)


IN_CQ, IN_CKV, IN_KR, IN_B, IN_C, IN_G, IN_END = 0, 384, 640, 672, 4768, 5536, 8608
WEIGHT_NAMES = ("w_in", "g_mix", "a_gq", "a_gkv", "a_wuq", "a_wukv", "b_lb", "b_gout", "c_sink", "rel_bias",
                "w_br_a", "w_br_b", "w_br_c", "w_out", "g_x", "g_mem", "x_wq", "x_wkv", "x_wo", "g_ffn",
                "f_w1", "f_w3", "f_w2", "g_final")


def _lower_bounds(b_lb):
    sm = jax.nn.softmax(b_lb.astype(F32), axis=1)
    return jnp.cumsum(sm, axis=1) - sm[:, :1]


def _layer_weights(w, l):
    bf = lambda a: a.astype(BF16)
    w_in = bf(w["w_in"][l])
    kr = w_in[:, IN_KR:IN_B]
    wa = jnp.concatenate([w_in[:, IN_CQ:IN_CKV], kr, _rope_swap_cols(kr), jnp.zeros((D_MODEL, 64), BF16),
                          w_in[:, IN_CKV:IN_KR]], axis=1)
    wuq = bf(w["a_wuq"][l]).reshape(A_Q_RANK, A_HEADS, A_QK)
    zeros = lambda n: jnp.zeros((A_Q_RANK, A_HEADS, n), BF16)
    wuq_pad = jnp.concatenate([wuq, zeros(A_PAD - A_QK)], axis=-1)
    wuq_sw = jnp.concatenate([zeros(A_NOPE), _rope_swap_cols(wuq[..., A_NOPE:]), zeros(A_PAD - A_QK)], axis=-1)
    wq2 = jnp.concatenate([wuq_pad.reshape(A_Q_RANK, -1), wuq_sw.reshape(A_Q_RANK, -1)], axis=1)
    wukv = bf(w["a_wukv"][l]).reshape(A_KV_RANK, A_HEADS, A_NOPE + A_V)
    wkv = jnp.concatenate([wukv[..., :A_NOPE].reshape(A_KV_RANK, -1), wukv[..., A_NOPE:].reshape(A_KV_RANK, -1)], axis=1)
    return dict(wa=wa, wb=w_in[:, IN_B:IN_C], wc=w_in[:, IN_C:IN_G], wg=w_in[:, IN_G:IN_END], wq2=wq2, wkv=wkv,
                w_br_a=bf(w["w_br_a"][l]), w_br_b=bf(w["w_br_b"][l]), w_br_c=bf(w["w_br_c"][l]), w_out=bf(w["w_out"][l]),
                x_wq=bf(w["x_wq"][l]), x_wkv=bf(w["x_wkv"][l]), x_wo=bf(w["x_wo"][l]),
                f_w1=bf(w["f_w1"][l]), f_w3=bf(w["f_w3"][l]), f_w2=bf(w["f_w2"][l]))


def _layer_fwd(l, x, mem, w, lw, lower, bias, tabs):
    n = lambda s: f"l{l}_{s}"
    cq_t, sq_t, ck, _ = tabs
    s = dict(x=x)
    s["h0"] = h0 = _rms(x, w["g_mix"][l], name=n("rms_mix"))
    s["za"] = za = _mm(h0, lw["wa"], name=n("in_a"))
    s["zb"] = zb = _mm(h0, lw["wb"], name=n("in_b"))
    s["zc"] = zc = _mm(h0, lw["wc"], out_dtype=BF16, name=n("in_c"))
    s["zg"] = zg = _mm(h0, lw["wg"], name=n("in_g"))
    s["cqn"] = cqn = _rms(za, w["a_gq"][l], col=0, width=A_Q_RANK, name=n("rms_cq"))
    s["ckvn"] = ckvn = _rms(za, w["a_gkv"][l], col=512, width=A_KV_RANK, name=n("rms_ckv"))
    s["q"] = q = _qrope(_mm(cqn, lw["wq2"], name=n("uq")), cq_t, sq_t, name=n("qrope"))
    s["kv"] = kv = _mm(ckvn, lw["wkv"], out_dtype=BF16, name=n("ukv"))
    s["k"], vx = k, _ = _kprep(kv, za, ck, name=n("kprep"))
    s["ya"], s["lse"] = ya, _ = _flash_fwd(q, k, vx.T, name=n("mla"))
    lb_f, lb_b = lower[0, l].reshape(1, -1), lower[1, l].reshape(1, -1)
    s["of"], s["stf"] = of, _ = _hgrn_fwd(zb, lb_f, reverse=False, name=n("hgrn_f"))
    s["ob"], s["stb"] = ob, _ = _hgrn_fwd(zb, lb_b, reverse=True, name=n("hgrn_b"))
    s["yb"] = yb = _hgrn_out(of, ob, zb, w["b_gout"][l], name=n("hgrn_out"))
    s["yc"] = yc = _swa_fwd(zc, bias, w["c_sink"][l], name=n("swa"))
    s["merged"] = merged = _merge_fwd(ya, yb, yc, lw["w_br_a"], lw["w_br_b"], lw["w_br_c"], zg, name=n("merge"))
    s["x1"] = x1 = _mm(merged, lw["w_out"], add=x, name=n("out"))
    s["h1"] = h1 = _rms(x1, w["g_x"][l], name=n("rms_x"))
    s["qx"] = qx = _mm(h1, lw["x_wq"], out_dtype=BF16, name=n("xq"))
    s["memn"] = memn = _rms(mem, w["g_mem"][l], name=n("rms_mem"))
    s["kvm"] = kvm = _mm(memn, lw["x_wkv"], out_dtype=BF16, name=n("xkv"))
    s["ox"] = ox = _cross_fwd(qx, kvm, name=n("cross"))
    s["x2"] = x2 = _mm(ox, lw["x_wo"], add=x1, name=n("xo"))
    s["h2"] = h2 = _rms(x2, w["g_ffn"][l], name=n("rms_ffn"))
    s["a"], s["b"], s["act"] = a, b, act = _ffn_up(h2, lw["f_w1"], lw["f_w3"], name=n("ffn_up"))
    x3 = _mm(act, lw["f_w2"], add=x2, name=n("ffn_down"))
    return x3, s


def _layer_bwd(l, dx3, mem, w, lw, lower, bias, tabs, s):
    n = lambda t: f"l{l}_b_{t}"
    cq_t, sq_t, _, ck_t = tabs
    g = {}
    da, db = _ffn_dact(dx3, lw["f_w2"], s["a"], s["b"], name=n("ffn_dact"))
    g["f_w2"] = _mm(s["act"], dx3, mode="tn", name=n("dw2"))
    dh2 = _mm(db, lw["f_w3"], mode="nt", add=_mm(da, lw["f_w1"], mode="nt", name=n("dh2a")), name=n("dh2b"))
    g["f_w1"] = _mm(s["h2"], da, mode="tn", name=n("dw1"))
    g["f_w3"] = _mm(s["h2"], db, mode="tn", name=n("dw3"))
    dx2, g["g_ffn"] = _rms_bwd(s["x2"], w["g_ffn"][l], dh2, res=dx3, name=n("rms_ffn"))
    dox = _mm(dx2, lw["x_wo"], mode="nt", out_dtype=BF16, name=n("dox"))
    g["x_wo"] = _mm(s["ox"], dx2, mode="tn", name=n("dwo"))
    dqx, dkvm = _cross_bwd(s["qx"], s["kvm"], dox, name=n("cross"))
    g["x_wq"] = _mm(s["h1"], dqx, mode="tn", name=n("dwq"))
    dh1 = _mm(dqx, lw["x_wq"], mode="nt", name=n("dh1"))
    g["x_wkv"] = _mm(s["memn"], dkvm, mode="tn", name=n("dwkv"))
    dmemn = _mm(dkvm, lw["x_wkv"], mode="nt", name=n("dmemn"))
    _, g["g_mem"] = _rms_bwd(mem, w["g_mem"][l], dmemn, name=n("rms_mem"))
    dx1, g["g_x"] = _rms_bwd(s["x1"], w["g_x"][l], dh1, res=dx2, name=n("rms_x"))
    dmerged = _mm(dx1, lw["w_out"], mode="nt", name=n("dmerged"))
    g["w_out"] = _mm(s["merged"], dx1, mode="tn", name=n("dwout"))
    dua, dub, duc, dga, dgb, dgc = _merge_bwd(s["ya"], s["yb"], s["yc"], lw["w_br_a"], lw["w_br_b"], lw["w_br_c"],
                                              s["zg"], dmerged, name=n("merge"))
    dya = _mm(dua, lw["w_br_a"], mode="nt", out_dtype=BF16, name=n("dya"))
    dyb = _mm(dub, lw["w_br_b"], mode="nt", name=n("dyb"))
    dyc = _mm(duc, lw["w_br_c"], mode="nt", out_dtype=BF16, name=n("dyc"))
    g["w_br_a"] = _mm(s["ya"], dua, mode="tn", name=n("dwbra"))
    g["w_br_b"] = _mm(s["yb"], dub, mode="tn", name=n("dwbrb"))
    g["w_br_c"] = _mm(s["yc"], duc, mode="tn", name=n("dwbrc"))
    dzc, dbias, dsink = _swa_bwd(s["zc"], bias, w["c_sink"][l], dyc, name=n("swa"))
    g["c_sink"] = dsink[:, 0]
    g["bias"] = dbias
    lb_f, lb_b = lower[0, l].reshape(1, -1), lower[1, l].reshape(1, -1)
    do_, dgate, dgout = _hgrn_out_bwd(s["of"], s["ob"], s["zb"], w["b_gout"][l], dyb, name=n("hgrn_out"))
    g["b_gout"] = dgout[0]
    dq_f, dzf, dv_f, dlb_f = _hgrn_bwd(s["zb"], lb_f, do_, s["stf"], reverse=False, name=n("hgrn_f"))
    dq_b, dzr, dv_b, dlb_b = _hgrn_bwd(s["zb"], lb_b, do_, s["stb"], reverse=True, name=n("hgrn_b"))
    g["lower"] = jnp.concatenate([dlb_f, dlb_b], axis=0)
    dzb = _dzb_assemble(dq_f, dq_b, dzf, dzr, dv_f, dv_b, dgate, name=n("dzb"))
    delta = _attn_delta(s["ya"], dya, name=n("mla_delta"))
    dq_part, dkt, dvt = _flash_bwd(s["q"], s["q"].T, s["k"], s["kv"], dya, dya.T, s["lse"].reshape(A_HEADS, -1, 1), delta, name=n("mla"))
    dq2 = _dq_sum(dq_part, cq_t, sq_t, name=n("mla_dq"))
    dcqn = _mm(dq2, lw["wq2"], mode="nt", name=n("dcqn"))
    dwq2 = _mm(s["cqn"], dq2, mode="tn", name=n("dwq2")).reshape(A_Q_RANK, 2, A_HEADS, A_PAD)
    dknt, dzkrt = _kprep_bwd(dkt, ck_t, name=n("kprep"))
    wkv = lw["wkv"]
    dckvn = _mm(dvt, wkv[:, 512:].T, mode="tn", add=_mm(dknt, wkv[:, :512].T, mode="tn", name=n("dckvn_k")), name=n("dckvn_v"))
    dwkn = _mm(dknt, s["ckvn"], name=n("dwkn")).T
    dwv = _mm(dvt, s["ckvn"], name=n("dwv")).T
    dzcq, dgq = _rms_bwd(s["za"], w["a_gq"][l], dcqn, col=0, width=A_Q_RANK, out_dtype=BF16, name=n("rms_cq"))
    dzckv, dgkv = _rms_bwd(s["za"], w["a_gkv"][l], dckvn, col=512, width=A_KV_RANK, out_dtype=BF16, name=n("rms_ckv"))
    g["a_gq"], g["a_gkv"] = dgq[0], dgkv[0]
    sw = jnp.concatenate([jnp.zeros((A_Q_RANK, A_HEADS, A_NOPE), F32), _rope_unswap_cols(dwq2[:, 1, :, A_NOPE:A_QK])], axis=-1)
    g["a_wuq"] = (dwq2[:, 0, :, :A_QK] + sw).reshape(A_Q_RANK, -1)
    g["a_wukv"] = jnp.concatenate([dwkn.reshape(A_KV_RANK, A_HEADS, A_NOPE), dwv.reshape(A_KV_RANK, A_HEADS, A_V)], axis=-1).reshape(A_KV_RANK, -1)
    wa = lw["wa"]
    pieces = [(dzcq, wa[:, 0:384]), (dzckv, wa[:, 512:768]), (dzb, lw["wb"]), (dzc, lw["wc"]),
              (dga, lw["wg"][:, 0:1024]), (dgb, lw["wg"][:, 1024:2048]), (dgc, lw["wg"][:, 2048:3072])]
    dh0 = _mm(dzkrt, wa[:, 384:512].T, mode="tn", name=n("dh0_kr"))
    dwkr = _mm(dzkrt, s["h0"], name=n("dwin_kr")).T
    dwkr = dwkr[:, 0:A_ROPE] + _rope_unswap_cols(dwkr[:, A_ROPE:2 * A_ROPE])
    dws = []
    for i, (dz, wp) in enumerate(pieces):
        dh0 = _mm(dz, wp, mode="nt", add=dh0, name=n(f"dh0_{i}"))
        dws.append(_mm(s["h0"], dz, mode="tn", name=n(f"dwin_{i}")))
    g["w_in"] = jnp.concatenate([dws[0], dws[1], dwkr] + dws[2:], axis=1)
    dx, g["g_mix"] = _rms_bwd(s["x"], w["g_mix"][l], dh0, res=dx1, name=n("rms_mix"))
    return dx, g


def _local_step(x, mem, target, w):
    T = x.shape[0]
    tabs = _rope_tables(T)
    lower, lower_vjp = jax.vjp(_lower_bounds, w["b_lb"])
    buckets = _swa_buckets()
    onehot = (buckets.reshape(-1)[:, None] == jnp.arange(REL_BUCKETS)[None, :]).astype(F32)
    bias = jnp.dot(w["rel_bias"].astype(F32).T, onehot.T, precision=lax.Precision.HIGHEST).reshape(C_HEADS, C_BLOCK, C_SPAN)
    lws, saved = [], []
    for l in range(DEPTH):
        lws.append(_layer_weights(w, l))
        x, s = _layer_fwd(l, x, mem, w, lws[l], lower, bias, tabs)
        saved.append(s)
    loss, dx, dg_final = _loss_head(x, w["g_final"], target, name="loss_head")
    layer_grads = [None] * DEPTH
    for l in reversed(range(DEPTH)):
        dx, layer_grads[l] = _layer_bwd(l, dx, mem, w, lws[l], lower, bias, tabs, saved[l])
        saved[l] = None
    grads = {}
    for name in WEIGHT_NAMES:
        if name in layer_grads[0]:
            grads[name] = jnp.stack([layer_grads[l][name].reshape(w[name].shape[1:]) for l in range(DEPTH)])
    grads["g_final"] = dg_final[0]
    dlower = jnp.stack([layer_grads[l]["lower"] for l in range(DEPTH)], axis=1)
    grads["b_lb"] = lower_vjp(dlower)[0]
    dbias = layer_grads[0]["bias"] + layer_grads[1]["bias"]
    grads["rel_bias"] = jnp.dot(onehot.T, dbias.reshape(C_HEADS, -1).T, precision=lax.Precision.HIGHEST)
    return loss, dx, grads


N_CHIPS = 4
PACK_COLS = 1024
PACK_ALIGN = 32 * PACK_COLS
SHARDED = (("w_in", 2), ("a_wuq", 2), ("a_wukv", 2), ("b_lb", 2), ("w_br_a", 2), ("w_br_b", 2), ("w_br_c", 2), ("w_out", 1),
           ("x_wq", 1), ("x_wkv", 2), ("x_wo", 1), ("f_w1", 2), ("f_w3", 2), ("f_w2", 1))
REPLICATED = ("g_mix", "a_gq", "a_gkv", "b_gout", "c_sink", "rel_bias", "g_x", "g_mem", "g_ffn", "g_final")
MESH_IDS = pl.DeviceIdType.MESH
ANY_SPEC = pl.BlockSpec(memory_space=pl.ANY)


def _pack_pieces(arrs, align):
    pieces = [a.reshape(-1) for a in arrs]
    pad = (-sum(p.shape[0] for p in pieces)) % align
    return pieces + ([jnp.zeros((pad,), pieces[0].dtype)] if pad else [])


def _pack(arrs, cols, align):
    return jnp.concatenate(_pack_pieces(arrs, align)).reshape(-1, cols)


def _unpack(buf, shapes):
    flat = buf.reshape(-1)
    out, start = [], 0
    for shp in shapes:
        size = math.prod(shp)
        out.append(flat[start:start + size].reshape(shp))
        start += size
    return out


def _chip_peers():
    x, y, c = lax.axis_index("x"), lax.axis_index("y"), lax.axis_index("c")
    return x, y, c, [(1 - x, y), (x, 1 - y), (1 - x, 1 - y)]


def _chip_gather(src, chip, *, name):
    _, R, C = src.shape

    def body(src_ref, out_ref, send_sems, recv_sems, pass_send_sems, pass_recv_sems):
        x, y, c, chips = _chip_peers()
        me = 2 * x + y
        sibling = (x, y, 1 - c)

        def over_ici(j, slot):
            px, py = chips[j]
            return pltpu.make_async_remote_copy(src_ref=src_ref.at[c], dst_ref=out_ref.at[slot, c], send_sem=send_sems.at[j],
                                                recv_sem=recv_sems.at[j], device_id=(px, py, c), device_id_type=MESH_IDS)

        def pass_on(j, half):
            px, py = chips[j]
            piece = out_ref.at[2 * px + py, half]
            return pltpu.make_async_remote_copy(src_ref=piece, dst_ref=piece, send_sem=pass_send_sems.at[j],
                                                recv_sem=pass_recv_sems.at[j], device_id=sibling, device_id_type=MESH_IDS)

        sends = [over_ici(j, me) for j in range(3)]
        for cp in sends:
            cp.start()
        passed = []
        for j, (px, py) in enumerate(chips):
            over_ici(j, 2 * px + py).wait_recv()
            passed.append(pass_on(j, c))
            passed[j].start()
        for j in range(3):
            pass_on(j, 1 - c).wait_recv()
        for cp in sends + passed:
            cp.wait_send()

    gathered = _pallas(
        body, name=name, in_specs=[ANY_SPEC], out_specs=ANY_SPEC, out_shape=jax.ShapeDtypeStruct((N_CHIPS, 2, R, C), src.dtype),
        scratch_shapes=[pltpu.SemaphoreType.DMA((3,))] * 4,
        compiler_params=pltpu.CompilerParams(has_side_effects=True))(src)

    tr = _tile(R, 512, 16)

    def place(chip_ref, gathered_ref, own_ref, out_ref):
        out_ref[0, 0] = own_ref[0]

    grid_spec = pltpu.PrefetchScalarGridSpec(
        num_scalar_prefetch=1, grid=(2, R // tr),
        in_specs=[ANY_SPEC, pl.BlockSpec((1, tr, C), lambda h, i, chip_ref: (h, i, 0))],
        out_specs=pl.BlockSpec((1, 1, tr, C), lambda h, i, chip_ref: (chip_ref[0], h, i, 0)))
    return _pallas(place, name=name + "_own", grid_spec=grid_spec, out_shape=jax.ShapeDtypeStruct(gathered.shape, gathered.dtype),
                   input_output_aliases={1: 0}, compiler_params=_params("arbitrary", "arbitrary"))(chip, gathered, src)


def _chip_scatter(src, *, name):
    _, R, C = src.shape

    def body(src_ref, out_ref, send_sems, recv_sems):
        x, y, c, chips = _chip_peers()
        me = 2 * x + y

        def copy(j, seg):
            px, py = chips[j]
            return pltpu.make_async_remote_copy(src_ref=src_ref.at[seg], dst_ref=out_ref.at[j], send_sem=send_sems.at[j],
                                                recv_sem=recv_sems.at[j], device_id=(px, py, c), device_id_type=MESH_IDS)

        sends = [copy(j, 2 * px + py) for j, (px, py) in enumerate(chips)]
        for cp in sends:
            cp.start()
        for j in range(3):
            copy(j, me).wait_recv()
        for cp in sends:
            cp.wait_send()

    return _pallas(
        body, name=name, in_specs=[ANY_SPEC], out_specs=ANY_SPEC, out_shape=jax.ShapeDtypeStruct((3, R, C), src.dtype),
        scratch_shapes=[pltpu.SemaphoreType.DMA((3,)), pltpu.SemaphoreType.DMA((3,))],
        compiler_params=pltpu.CompilerParams(has_side_effects=True))(src)


PAIR_CHUNKS = 4


def _pair_swap(src, *, halves, name):
    R, C = src.shape[-2:]
    n = N_CHIPS if halves else 1
    rc = R // PAIR_CHUNKS
    assert rc * PAIR_CHUNKS == R and rc % 16 == 0, R

    def body(src_ref, out_ref, send_sems, recv_sems):
        x, y, c = lax.axis_index("x"), lax.axis_index("y"), lax.axis_index("c")
        copies = []
        for k in range(n):
            for r in range(PAIR_CHUNKS):
                rows = pl.ds(r * rc, rc)
                s = src_ref.at[k, 1 - c, rows] if halves else src_ref.at[rows]
                d = out_ref.at[k, rows] if halves else out_ref.at[rows]
                i = k * PAIR_CHUNKS + r
                copies.append(pltpu.make_async_remote_copy(src_ref=s, dst_ref=d, send_sem=send_sems.at[i], recv_sem=recv_sems.at[i],
                                                           device_id=(x, y, 1 - c), device_id_type=MESH_IDS))
        for cp in copies:
            cp.start()
        for cp in copies:
            cp.wait_recv()
        for cp in copies:
            cp.wait_send()

    shape = (N_CHIPS, R, C) if halves else (R, C)
    return _pallas(
        body, name=name, in_specs=[ANY_SPEC], out_specs=ANY_SPEC, out_shape=jax.ShapeDtypeStruct(shape, src.dtype),
        scratch_shapes=[pltpu.SemaphoreType.DMA((n * PAIR_CHUNKS,)), pltpu.SemaphoreType.DMA((n * PAIR_CHUNKS,))],
        compiler_params=pltpu.CompilerParams(has_side_effects=True))(src)


def _pair_add(g4, got, c, *, name):
    _, _, R, C = g4.shape
    tr = _tile(R, 512, 16)

    def body(c_ref, mine_ref, got_ref, o_ref, ob_ref):
        s = mine_ref[0, 0] + got_ref[0].astype(F32)
        o_ref[0] = s
        ob_ref[0] = s.astype(BF16)

    blk = pl.BlockSpec((1, tr, C), lambda k, i, c_ref: (k, i, 0))
    grid_spec = pltpu.PrefetchScalarGridSpec(
        num_scalar_prefetch=1, grid=(N_CHIPS, R // tr),
        in_specs=[pl.BlockSpec((1, 1, tr, C), lambda k, i, c_ref: (k, c_ref[0], i, 0)), blk], out_specs=[blk, blk])
    return _pallas(body, name=name, grid_spec=grid_spec,
                   out_shape=[jax.ShapeDtypeStruct((N_CHIPS, R, C), F32), jax.ShapeDtypeStruct((N_CHIPS, R, C), BF16)],
                   compiler_params=_params("parallel", "parallel"))(c, g4, got)


def _chip_sum(pair_sum, landed, me, *, name):
    _, R, C = pair_sum.shape
    tr = _tile(R, 512, 16)

    def body(me_ref, own_ref, landed_ref, o_ref):
        acc = own_ref[0]
        for j in range(3):
            acc = acc + landed_ref[j].astype(F32)
        o_ref[...] = acc

    grid_spec = pltpu.PrefetchScalarGridSpec(
        num_scalar_prefetch=1, grid=(R // tr,),
        in_specs=[pl.BlockSpec((1, tr, C), lambda i, me_ref: (me_ref[0], i, 0)), pl.BlockSpec((3, tr, C), lambda i, me_ref: (0, i, 0))],
        out_specs=pl.BlockSpec((tr, C), lambda i, me_ref: (i, 0)))
    return _pallas(body, name=name, grid_spec=grid_spec, out_shape=jax.ShapeDtypeStruct((R, C), F32),
                   compiler_params=_params("parallel"))(me, pair_sum, landed)


def _join_halves(mine, got, c, *, name):
    R, C = mine.shape
    tr = _tile(R, 512, 16)

    def body(c_ref, mine_ref, got_ref, o_ref):
        use_mine = pl.program_id(0) == c_ref[0]
        o_ref[0] = jnp.where(use_mine, mine_ref[...], got_ref[...])

    blk = pl.BlockSpec((tr, C), lambda h, i, c_ref: (i, 0))
    grid_spec = pltpu.PrefetchScalarGridSpec(num_scalar_prefetch=1, grid=(2, R // tr), in_specs=[blk, blk],
                                             out_specs=pl.BlockSpec((1, tr, C), lambda h, i, c_ref: (h, i, 0)))
    return _pallas(body, name=name, grid_spec=grid_spec, out_shape=jax.ShapeDtypeStruct((2, R, C), mine.dtype),
                   compiler_params=_params("parallel", "parallel"))(c, mine, got).reshape(2 * R, C)


def _gather8(s, *, name):
    R, C = s.shape

    def body(s_ref, out_ref, send_sems, recv_sems):
        x, y, c = lax.axis_index("x"), lax.axis_index("y"), lax.axis_index("c")
        me = 4 * x + 2 * y + c
        flips = [(dx, dy, dc) for dx in (0, 1) for dy in (0, 1) for dc in (0, 1)][1:]
        out_ref[me] = s_ref[...]

        def copy(j, slot):
            dx, dy, dc = flips[j]
            return pltpu.make_async_remote_copy(src_ref=s_ref, dst_ref=out_ref.at[slot], send_sem=send_sems.at[j],
                                                recv_sem=recv_sems.at[j], device_id=(x ^ dx, y ^ dy, c ^ dc), device_id_type=MESH_IDS)

        sends = [copy(j, me) for j in range(7)]
        for cp in sends:
            cp.start()
        for j, (dx, dy, dc) in enumerate(flips):
            copy(j, 4 * (x ^ dx) + 2 * (y ^ dy) + (c ^ dc)).wait_recv()
        for cp in sends:
            cp.wait_send()

    vmem = pl.BlockSpec(memory_space=pltpu.VMEM)
    return _pallas(
        body, name=name, in_specs=[vmem], out_specs=vmem, out_shape=jax.ShapeDtypeStruct((8, R, C), s.dtype),
        scratch_shapes=[pltpu.SemaphoreType.DMA((7,)), pltpu.SemaphoreType.DMA((7,))],
        compiler_params=pltpu.CompilerParams(has_side_effects=True))(s)


def _sum_slots(a, *, name):
    n, R, C = a.shape
    tr = _tile(R, 512, 8)

    def body(a_ref, o_ref):
        acc = a_ref[0]
        for k in range(1, n):
            acc = acc + a_ref[k]
        o_ref[...] = acc

    return _pallas(body, name=name, grid=(R // tr,), in_specs=[pl.BlockSpec((n, tr, C), lambda i: (0, i, 0))],
                   out_specs=pl.BlockSpec((tr, C), lambda i: (i, 0)), out_shape=jax.ShapeDtypeStruct((R, C), a.dtype),
                   compiler_params=_params("parallel"))(a)


def _adamw(w, g, m, v, *, name):
    R, C = w.shape
    tr = _tile(R, max(8, (1 << 18) // C // 8 * 8), 8)
    c1 = 1.0 / (1.0 - ADAM_B1 ** ADAM_STEP)
    c2 = 1.0 / (1.0 - ADAM_B2 ** ADAM_STEP)

    def body(w_ref, g_ref, m_ref, v_ref, d_ref, nm_ref, nv_ref):
        gv = g_ref[...]
        nm = ADAM_B1 * m_ref[...] + (1.0 - ADAM_B1) * gv
        nv = ADAM_B2 * v_ref[...] + (1.0 - ADAM_B2) * (gv * gv)
        d_ref[...] = -ADAM_LR * ((nm * c1) / (jnp.sqrt(nv * c2) + ADAM_EPS) + ADAM_WD * w_ref[...])
        nm_ref[...] = nm
        nv_ref[...] = nv

    blk = pl.BlockSpec((tr, C), lambda i: (i, 0))
    shape = jax.ShapeDtypeStruct((R, C), F32)
    return _pallas(body, name=name, grid=(R // tr,), in_specs=[blk] * 4, out_specs=[blk] * 3, out_shape=[shape] * 3,
                   compiler_params=_params("parallel"))(w, g, m, v)


def kernel(x, mem, w_in, g_mix, a_gq, a_gkv, a_wuq, a_wukv, b_lb, b_gout, c_sink, rel_bias, w_br_a, w_br_b, w_br_c, w_out, g_x, g_mem, x_wq, x_wkv, x_wo, g_ffn, f_w1, f_w3, f_w2, g_final, loss_target, m_w_in, m_g_mix, m_a_gq, m_a_gkv, m_a_wuq, m_a_wukv, m_b_lb, m_b_gout, m_c_sink, m_rel_bias, m_w_br_a, m_w_br_b, m_w_br_c, m_w_out, m_g_x, m_g_mem, m_x_wq, m_x_wkv, m_x_wo, m_g_ffn, m_f_w1, m_f_w3, m_f_w2, m_g_final, v_w_in, v_g_mix, v_a_gq, v_a_gkv, v_a_wuq, v_a_wukv, v_b_lb, v_b_gout, v_c_sink, v_rel_bias, v_w_br_a, v_w_br_b, v_w_br_c, v_w_out, v_g_x, v_g_mem, v_x_wq, v_x_wkv, v_x_wo, v_g_ffn, v_f_w1, v_f_w3, v_f_w2, v_g_final):
    ws = dict(zip(WEIGHT_NAMES, (w_in, g_mix, a_gq, a_gkv, a_wuq, a_wukv, b_lb, b_gout, c_sink, rel_bias, w_br_a, w_br_b, w_br_c,
                                 w_out, g_x, g_mem, x_wq, x_wkv, x_wo, g_ffn, f_w1, f_w3, f_w2, g_final)))
    ms = dict(zip(WEIGHT_NAMES, (m_w_in, m_g_mix, m_a_gq, m_a_gkv, m_a_wuq, m_a_wukv, m_b_lb, m_b_gout, m_c_sink, m_rel_bias,
                                 m_w_br_a, m_w_br_b, m_w_br_c, m_w_out, m_g_x, m_g_mem, m_x_wq, m_x_wkv, m_x_wo, m_g_ffn,
                                 m_f_w1, m_f_w3, m_f_w2, m_g_final)))
    vs = dict(zip(WEIGHT_NAMES, (v_w_in, v_g_mix, v_a_gq, v_a_gkv, v_a_wuq, v_a_wukv, v_b_lb, v_b_gout, v_c_sink, v_rel_bias,
                                 v_w_br_a, v_w_br_b, v_w_br_c, v_w_out, v_g_x, v_g_mem, v_x_wq, v_x_wkv, v_x_wo, v_g_ffn,
                                 v_f_w1, v_f_w3, v_f_w2, v_g_final)))
    sharded = [n for n, _ in SHARDED]
    axis_of = dict(SHARDED)

    def wire(n):
        return lax.bitcast_convert_type(ws[n], BF16) if n == "b_lb" else ws[n].astype(BF16)

    core = lax.axis_index("c").astype(jnp.int32).reshape(1)
    chip = (2 * lax.axis_index("x") + lax.axis_index("y")).astype(jnp.int32).reshape(1)
    wire_shapes = [wire(n).shape for n in sharded]
    packed = _pack([wire(n) for n in sharded], PACK_COLS, PACK_ALIGN)
    gathered = _chip_gather(packed.reshape(2, packed.shape[0] // 2, PACK_COLS), chip, name="gather_weights")
    per_chip = [_unpack(gathered[k], wire_shapes) for k in range(N_CHIPS)]
    full = dict(ws)
    for i, n in enumerate(sharded):
        parts = [per_chip[k][i] for k in range(N_CHIPS)]
        if n == "b_lb":
            parts = [lax.bitcast_convert_type(p, F32) for p in parts]
        full[n] = jnp.concatenate(parts, axis=axis_of[n])

    loss, grad_x, grads = _local_step(x[0], mem[0], loss_target[0], full)

    pieces = []
    for k in range(N_CHIPS):
        pieces += _pack_pieces([jnp.split(grads[n], N_CHIPS, axis=axis_of[n])[k] for n in sharded], PACK_ALIGN)
    g4 = jnp.concatenate(pieces).reshape(N_CHIPS, 2, -1, PACK_COLS)
    rows = 2 * g4.shape[2]
    got = _pair_swap(g4, halves=True, name="reduce_pair_swap")
    pair_sum, pair_sum_wire = _pair_add(g4, got, core, name="reduce_pair_add")
    landed = _chip_scatter(pair_sum_wire, name="reduce_chip_scatter")
    mine = _chip_sum(pair_sum, landed, chip, name="reduce_chip_sum")
    g_shard = _join_halves(mine, _pair_swap(mine, halves=False, name="reduce_pair_join"), core, name="reduce_join_halves")

    small = _pack([grads[n] for n in REPLICATED] + [loss[0, 0:1]], LANES, 8 * LANES)
    small_sum = _sum_slots(_gather8(small, name="gather_small"), name="sum_small")
    small_grads = _unpack(small_sum, [ws[n].shape for n in REPLICATED] + [(1,)])
    loss_total = small_grads.pop()[0]

    shard_shapes = [ws[n].shape for n in sharded]
    out = {}
    for n, gr in zip(sharded, _unpack(g_shard, shard_shapes)):
        flat2 = lambda a: a.reshape(-1, a.shape[-1])
        d, nm, nv = _adamw(flat2(ws[n]), flat2(gr), flat2(ms[n]), flat2(vs[n]), name="adamw_" + n)
        out[n] = (gr, d.reshape(gr.shape), nm.reshape(gr.shape), nv.reshape(gr.shape))
    pk_s = lambda d: _pack([d[n] for n in REPLICATED], LANES, 8 * LANES)
    rep_shapes = [ws[n].shape for n in REPLICATED]
    gs_flat = _pack(small_grads, LANES, 8 * LANES)
    ds_flat, ms_flat, vs_flat = _adamw(pk_s(ws), gs_flat, pk_s(ms), pk_s(vs), name="adamw_replicated")
    for n, gr, d, nm, nv in zip(REPLICATED, small_grads, _unpack(ds_flat, rep_shapes), _unpack(ms_flat, rep_shapes),
                                _unpack(vs_flat, rep_shapes)):
        out[n] = (gr, d, nm, nv)
    return (loss_total, grad_x[None], *[out[n][0] for n in WEIGHT_NAMES], *[out[n][1] for n in WEIGHT_NAMES],
            *[out[n][2] for n in WEIGHT_NAMES], *[out[n][3] for n in WEIGHT_NAMES])
```

```python
import functools
import math

import jax
import jax.numpy as jnp
from jax import lax
from jax.experimental import pallas as pl
from jax.experimental.pallas import tpu as pltpu

F32 = jnp.float32
BF16 = jnp.bfloat16

D_MODEL = 1024
DEPTH = 2
EPS = 1e-6
MASK_VALUE = -1e30
TINY = 1e-30
A_HEADS, A_NOPE, A_ROPE, A_V = 8, 64, 32, 64
A_QK = A_NOPE + A_ROPE
A_Q_RANK, A_KV_RANK = 384, 256
ROPE_THETA = 10000.0
B_HEADS, B_DK, B_DV, B_CHUNK = 8, 128, 64, 16
C_HEADS, C_KV_HEADS, C_DH, C_WINDOW, C_BLOCK = 8, 2, 64, 128, 128
REL_BUCKETS, REL_MAX_DIST = 32, 128
X_HEADS, X_DH = 4, 256
D_FF = 2816
ADAM_LR, ADAM_B1, ADAM_B2, ADAM_EPS, ADAM_WD, ADAM_STEP = 0.001, 0.9, 0.999, 1e-08, 0.01, 10

LANES = 128
VMEM_LIMIT = 56 * 1024 * 1024


def _pallas(body, **kw):
    return pl.pallas_call(body, **kw)


def _params(*sem):
    return pltpu.CompilerParams(dimension_semantics=sem, vmem_limit_bytes=VMEM_LIMIT)


def _tile(n, pref, unit=LANES):
    if n <= pref:
        return n
    t = (pref // unit) * unit
    while t > unit and n % t:
        t -= unit
    assert n % t == 0, (n, pref, unit)
    return t


def _dot(a, b, dims):
    return lax.dot_general(a, b, (dims, ((), ())), preferred_element_type=F32)


_NN = ((1,), (0,))
_NT = ((1,), (1,))
_TN = ((0,), (0,))


def _mm_tiles(M, N, K, mode):
    tm = _tile(M, 1408 if M % 1408 == 0 else 512, LANES if mode == "tn" else 8)
    tn = _tile(N, 1408 if N % 1408 == 0 else 1024, 256 if N % 256 == 0 and N % 1408 else LANES)
    tk = K if K <= 2816 else _tile(K, 1024)
    return tm, tn, tk


def _mm(a, b, *, mode="nn", add=None, out_dtype=F32, tiles=None, name):
    if mode == "nn":
        (M, K), (K2, N) = a.shape, b.shape
    elif mode == "nt":
        (M, K), (N, K2) = a.shape, b.shape
    else:
        (K, M), (K2, N) = a.shape, b.shape
    assert K == K2, (a.shape, b.shape, mode)
    tm, tn, tk = tiles or _mm_tiles(M, N, K, mode)
    nk = K // tk
    dims = {"nn": _NN, "nt": _NT, "tn": _TN}[mode]
    a_spec = pl.BlockSpec((tk, tm), lambda i, j, k: (k, i)) if mode == "tn" else pl.BlockSpec((tm, tk), lambda i, j, k: (i, k))
    b_spec = pl.BlockSpec((tn, tk), lambda i, j, k: (j, k)) if mode == "nt" else pl.BlockSpec((tk, tn), lambda i, j, k: (k, j))
    o_spec = pl.BlockSpec((tm, tn), lambda i, j, k: (i, j))
    has_add = add is not None

    def body(*refs):
        if has_add:
            a_ref, b_ref, add_ref, o_ref, acc_ref = refs
        else:
            a_ref, b_ref, o_ref, acc_ref = refs
        k = pl.program_id(2)
        part = _dot(a_ref[...].astype(BF16), b_ref[...].astype(BF16), dims)

        @pl.when(k == 0)
        def _():
            acc_ref[...] = part

        @pl.when(k > 0)
        def _():
            acc_ref[...] += part

        @pl.when(k == nk - 1)
        def _():
            r = acc_ref[...]
            if has_add:
                r = r + add_ref[...].astype(F32)
            o_ref[...] = r.astype(out_dtype)

    ins = [a, b] + ([add] if has_add else [])
    in_specs = [a_spec, b_spec] + ([o_spec] if has_add else [])
    return _pallas(
        body, name=name, grid=(M // tm, N // tn, nk), in_specs=in_specs, out_specs=o_spec,
        out_shape=jax.ShapeDtypeStruct((M, N), out_dtype), scratch_shapes=[pltpu.VMEM((tm, tn), F32)],
        compiler_params=_params("parallel", "parallel", "arbitrary"),
    )(*ins)


def _rms(x, g, *, col=0, width=None, out_dtype=BF16, name):
    T = x.shape[0]
    width = x.shape[1] if width is None else width
    assert col % width == 0
    tm = _tile(T, 512, 8)
    cb = col // width

    def body(x_ref, g_ref, o_ref):
        xv = x_ref[...].astype(F32)
        r = lax.rsqrt(jnp.mean(xv * xv, axis=-1, keepdims=True) + EPS)
        o_ref[...] = (xv * r * g_ref[...]).astype(out_dtype)

    return _pallas(
        body, name=name, grid=(T // tm,),
        in_specs=[pl.BlockSpec((tm, width), lambda i: (i, cb)), pl.BlockSpec((1, width), lambda i: (0, 0))],
        out_specs=pl.BlockSpec((tm, width), lambda i: (i, 0)),
        out_shape=jax.ShapeDtypeStruct((T, width), out_dtype), compiler_params=_params("parallel"),
    )(x, g.reshape(1, width))


def _rms_bwd(x, g, dy, *, res=None, col=0, width=None, out_dtype=F32, name):
    T = x.shape[0]
    width = x.shape[1] if width is None else width
    assert col % width == 0
    tm = _tile(T, 512, 8)
    cb = col // width
    has_res = res is not None

    def body(*refs):
        if has_res:
            x_ref, g_ref, dy_ref, res_ref, dx_ref, dg_ref = refs
        else:
            x_ref, g_ref, dy_ref, dx_ref, dg_ref = refs
        xv = x_ref[...].astype(F32)
        r = lax.rsqrt(jnp.mean(xv * xv, axis=-1, keepdims=True) + EPS)
        xh = xv * r
        dyv = dy_ref[...].astype(F32)
        dxh = dyv * g_ref[...]
        dx = r * (dxh - xh * jnp.mean(dxh * xh, axis=-1, keepdims=True))
        if has_res:
            dx = dx + res_ref[...].astype(F32)
        dx_ref[...] = dx.astype(out_dtype)
        part = jnp.sum(dyv * xh, axis=0, keepdims=True)

        @pl.when(pl.program_id(0) == 0)
        def _():
            dg_ref[...] = part

        @pl.when(pl.program_id(0) > 0)
        def _():
            dg_ref[...] += part

    row = pl.BlockSpec((tm, width), lambda i: (i, 0))
    ins = [x, g.reshape(1, width), dy] + ([res] if has_res else [])
    in_specs = [pl.BlockSpec((tm, width), lambda i: (i, cb)), pl.BlockSpec((1, width), lambda i: (0, 0)), row] + ([row] if has_res else [])
    return _pallas(
        body, name=name, grid=(T // tm,), in_specs=in_specs,
        out_specs=[row, pl.BlockSpec((1, width), lambda i: (0, 0))],
        out_shape=[jax.ShapeDtypeStruct((T, width), out_dtype), jax.ShapeDtypeStruct((1, width), F32)],
        compiler_params=_params("arbitrary"),
    )(*ins)


def _rope_tables(T):
    half = A_ROPE // 2
    inv = ROPE_THETA ** (-jnp.arange(half, dtype=F32) / half)
    ang = jnp.arange(T, dtype=jnp.int32).astype(F32)[:, None] * inv[None, :]
    c32 = jnp.concatenate([jnp.cos(ang), jnp.cos(ang)], axis=-1)
    s32 = jnp.concatenate([jnp.sin(ang), jnp.sin(ang)], axis=-1)
    pad = A_PAD - A_QK
    cq = jnp.concatenate([jnp.ones((T, A_NOPE), F32), c32, jnp.ones((T, pad), F32)], axis=-1)
    sq = jnp.concatenate([jnp.zeros((T, A_NOPE), F32), s32, jnp.zeros((T, pad), F32)], axis=-1)
    ck = jnp.concatenate([c32, s32, jnp.zeros((T, LANES - 2 * A_ROPE), F32)], axis=-1)
    ck_t = jnp.concatenate([c32, s32], axis=-1).T
    return cq, sq, ck, ck_t


def _rope_swap_cols(w):
    half = A_ROPE // 2
    return jnp.concatenate([-w[..., half:], w[..., :half]], axis=-1)


def _rope_unswap_cols(g):
    half = A_ROPE // 2
    return jnp.concatenate([g[..., half:], -g[..., :half]], axis=-1)


A_PAD = LANES
A_W = A_HEADS * A_PAD
LOG2E = 1.4426950408889634
LN2 = 0.6931471805599453
Q_SCALE = A_QK ** -0.5 * LOG2E


def _qrope(q2, cq, sq, *, name):
    T = q2.shape[0]
    W = A_W
    tm = _tile(T, 512)

    def body(a_ref, b_ref, c_ref, s_ref, o_ref, ot_ref):
        c = jnp.concatenate([c_ref[...]] * A_HEADS, axis=1)
        s = jnp.concatenate([s_ref[...]] * A_HEADS, axis=1)
        q = (a_ref[...] * c + b_ref[...] * s) * Q_SCALE
        o_ref[...] = q.astype(BF16)
        ot_ref[...] = q.T.astype(BF16)

    blk = lambda j: pl.BlockSpec((tm, W), lambda i: (i, j))
    tab = pl.BlockSpec((tm, A_PAD), lambda i: (i, 0))
    return _pallas(body, name=name, grid=(T // tm,), in_specs=[blk(0), blk(1), tab, tab],
                   out_specs=[blk(0), pl.BlockSpec((W, tm), lambda i: (0, i))],
                   out_shape=[jax.ShapeDtypeStruct((T, W), BF16), jax.ShapeDtypeStruct((W, T), BF16)],
                   compiler_params=_params("parallel"))(q2, q2, cq, sq)


def _kprep(kv, za, ck, *, name):
    T = kv.shape[0]
    tm = _tile(T, 512)

    def body(kv_ref, kr_ref, ck_ref, k_ref, vxt_ref):
        t = kr_ref[...] * ck_ref[...]
        krope = (t[:, 0:A_ROPE] + t[:, A_ROPE:2 * A_ROPE]).astype(BF16)
        one = (lax.broadcasted_iota(jnp.int32, (A_PAD - A_V, tm), 0) == 0).astype(BF16)
        for h in range(A_HEADS):
            k_ref[:, A_PAD * h:A_PAD * h + A_NOPE] = kv_ref[:, A_NOPE * h:A_NOPE * (h + 1)]
            k_ref[:, A_PAD * h + A_NOPE:A_PAD * h + A_QK] = krope
            k_ref[:, A_PAD * h + A_QK:A_PAD * (h + 1)] = jnp.zeros((tm, A_PAD - A_QK), BF16)
            vxt_ref[A_PAD * h + A_V:A_PAD * (h + 1), :] = one
        vt = kv_ref[:, 512:1024].astype(F32).T.astype(BF16)
        for h in range(A_HEADS):
            vxt_ref[A_PAD * h:A_PAD * h + A_V, :] = vt[A_V * h:A_V * (h + 1), :]

    wide = pl.BlockSpec((tm, A_W), lambda i: (i, 0))
    return _pallas(
        body, name=name, grid=(T // tm,),
        in_specs=[wide, pl.BlockSpec((tm, LANES), lambda i: (i, 3)), pl.BlockSpec((tm, LANES), lambda i: (i, 0))],
        out_specs=[wide, pl.BlockSpec((A_W, tm), lambda i: (0, i))],
        out_shape=[jax.ShapeDtypeStruct((T, A_W), BF16), jax.ShapeDtypeStruct((A_W, T), BF16)],
        compiler_params=_params("parallel"))(kv, za, ck)


def _kprep_bwd(dkt, ck_t, *, name):
    T = dkt.shape[1]
    tc = _tile(T, 512)

    def body(dk_ref, ck_ref, dn_ref, dr_ref):
        acc = jnp.zeros((A_ROPE, tc), F32)
        for h in range(A_HEADS):
            dn_ref[A_NOPE * h:A_NOPE * (h + 1), :] = dk_ref[A_PAD * h:A_PAD * h + A_NOPE, :].astype(BF16)
            acc = acc + dk_ref[A_PAD * h + A_NOPE:A_PAD * h + A_QK, :]
        dr_ref[0:A_ROPE, :] = (acc * ck_ref[0:A_ROPE, :]).astype(BF16)
        dr_ref[A_ROPE:2 * A_ROPE, :] = (acc * ck_ref[A_ROPE:2 * A_ROPE, :]).astype(BF16)
        dr_ref[2 * A_ROPE:LANES, :] = jnp.zeros((LANES - 2 * A_ROPE, tc), BF16)

    col = lambda r: pl.BlockSpec((r, tc), lambda i: (0, i))
    return _pallas(
        body, name=name, grid=(T // tc,), in_specs=[col(A_W), col(2 * A_ROPE)], out_specs=[col(512), col(LANES)],
        out_shape=[jax.ShapeDtypeStruct((512, T), BF16), jax.ShapeDtypeStruct((LANES, T), BF16)],
        compiler_params=_params("parallel"))(dkt, ck_t)


def _flash_fwd(qs, k, vxt, *, name):
    T = qs.shape[0]
    tq, tk = _tile(T, 512), _tile(T, 2048)
    nk = T // tk
    H, P, DV = A_HEADS, A_PAD, A_V

    def body(q_ref, k_ref, v_ref, o_ref, lse_ref, m_sc, acc_sc):
        j = pl.program_id(1)

        @pl.when(j == 0)
        def _():
            m_sc[...] = jnp.full(m_sc.shape, -jnp.inf, F32)
            acc_sc[...] = jnp.zeros(acc_sc.shape, F32)

        def scores(h):
            return _dot(k_ref[:, P * h:P * (h + 1)], q_ref[:, P * h:P * (h + 1)], _NT)

        st_next = scores(0)
        for h in range(H):
            st = st_next
            if h + 1 < H:
                st_next = scores(h + 1)
            m_prev = m_sc[h]
            m_new = jnp.maximum(m_prev, jnp.max(st, axis=0, keepdims=True))
            pt = jnp.exp2(st - m_new).astype(BF16)
            acc_sc[h] = jnp.exp2(m_prev - m_new) * acc_sc[h] + _dot(v_ref[P * h:P * (h + 1), :], pt, _NN)
            m_sc[h] = m_new

        @pl.when(j == nk - 1)
        def _():
            for h in range(H):
                acc = acc_sc[h]
                l = acc[DV:DV + 1, :]
                o_ref[:, DV * h:DV * (h + 1)] = (acc[0:DV, :] / l).T
                lse_ref[h] = m_sc[h] + jnp.log2(l)

    return _pallas(
        body, name=name, grid=(T // tq, nk),
        in_specs=[pl.BlockSpec((tq, A_W), lambda i, j: (i, 0)), pl.BlockSpec((tk, A_W), lambda i, j: (j, 0)),
                  pl.BlockSpec((A_W, tk), lambda i, j: (0, j))],
        out_specs=[pl.BlockSpec((tq, H * DV), lambda i, j: (i, 0)), pl.BlockSpec((H, 1, tq), lambda i, j: (0, 0, i))],
        out_shape=[jax.ShapeDtypeStruct((T, H * DV), F32), jax.ShapeDtypeStruct((H, 1, T), F32)],
        scratch_shapes=[pltpu.VMEM((H, 1, tq), F32), pltpu.VMEM((H, P, tq), F32)],
        compiler_params=_params("parallel", "arbitrary"))(qs, k, vxt)


def _attn_delta(o, do, *, name):
    T = o.shape[0]
    tm = _tile(T, 512, 8)

    def body(o_ref, do_ref, d_ref):
        prod = o_ref[...] * do_ref[...].astype(F32)
        for h in range(A_HEADS):
            d_ref[h] = jnp.sum(prod[:, A_V * h:A_V * (h + 1)], axis=-1, keepdims=True)

    row = pl.BlockSpec((tm, A_HEADS * A_V), lambda i: (i, 0))
    return _pallas(body, name=name, grid=(T // tm,), in_specs=[row, row],
                   out_specs=pl.BlockSpec((A_HEADS, tm, 1), lambda i: (0, i, 0)),
                   out_shape=jax.ShapeDtypeStruct((A_HEADS, T, 1), F32), compiler_params=_params("parallel"))(o, do)


def _flash_bwd(qs, qst, k, kv, do, dot_, lse2, delta, *, tiles=None, name):
    T = qs.shape[0]
    tq, tk = tiles or (_tile(T, 512), _tile(T, 1024))
    nq, nk = T // tq, T // tk
    H, P, DV = A_HEADS, A_PAD, A_V

    def body(q_ref, qt_ref, k_ref, v_ref, do_ref, dot_ref, lse_ref, delta_ref, dq_ref, dkt_ref, dvt_ref, dkt_sc, dvt_sc):
        i = pl.program_id(1)

        @pl.when(i == 0)
        def _():
            dkt_sc[...] = jnp.zeros(dkt_sc.shape, F32)
            dvt_sc[...] = jnp.zeros(dvt_sc.shape, F32)

        for h in range(H):
            s = _dot(q_ref[:, P * h:P * (h + 1)], k_ref[:, P * h:P * (h + 1)], _NT)
            dp = _dot(do_ref[:, DV * h:DV * (h + 1)], v_ref[:, DV * h:DV * (h + 1)], _NT)
            p = jnp.exp2(s - lse_ref[h])
            ds = (p * (dp - delta_ref[h])).astype(BF16)
            pb = p.astype(BF16)
            dq_ref[0, :, P * h:P * (h + 1)] = _dot(ds, k_ref[:, P * h:P * (h + 1)], _NN).astype(BF16)
            dkt_sc[h] += _dot(qt_ref[P * h:P * (h + 1), :], ds, _NN)
            dvt_sc[h] += _dot(dot_ref[DV * h:DV * (h + 1), :], pb, _NN)

        @pl.when(i == nq - 1)
        def _():
            for h in range(H):
                dkt_ref[P * h:P * (h + 1), :] = dkt_sc[h] * LN2
                dvt_ref[DV * h:DV * (h + 1), :] = dvt_sc[h].astype(BF16)

    qrow = lambda w: pl.BlockSpec((tq, w), lambda j, i: (i, 0))
    qcol = lambda r: pl.BlockSpec((r, tq), lambda j, i: (0, i))
    stat = pl.BlockSpec((H, tq, 1), lambda j, i: (0, i, 0))
    return _pallas(
        body, name=name, grid=(nk, nq),
        in_specs=[qrow(A_W), qcol(A_W), pl.BlockSpec((tk, A_W), lambda j, i: (j, 0)), pl.BlockSpec((tk, H * DV), lambda j, i: (j, 1)),
                  qrow(H * DV), qcol(H * DV), stat, stat],
        out_specs=[pl.BlockSpec((1, tq, A_W), lambda j, i: (j, i, 0)), pl.BlockSpec((A_W, tk), lambda j, i: (0, j)),
                   pl.BlockSpec((H * DV, tk), lambda j, i: (0, j))],
        out_shape=[jax.ShapeDtypeStruct((nk, T, A_W), BF16), jax.ShapeDtypeStruct((A_W, T), F32),
                   jax.ShapeDtypeStruct((H * DV, T), BF16)],
        scratch_shapes=[pltpu.VMEM((H, P, tk), F32), pltpu.VMEM((H, DV, tk), F32)],
        compiler_params=_params("parallel", "arbitrary"))(qs, qst, k, kv, do, dot_, lse2, delta)


def _dq_sum(dq_part, cq, sq, *, name):
    n, T, W = dq_part.shape
    tm = _tile(T, 256, 16)

    def body(p_ref, c_ref, s_ref, o_ref):
        acc = p_ref[0].astype(F32)
        for j in range(1, n):
            acc = acc + p_ref[j].astype(F32)
        acc = acc * (A_QK ** -0.5)
        o_ref[:, 0:W] = (acc * jnp.concatenate([c_ref[...]] * A_HEADS, axis=1)).astype(BF16)
        o_ref[:, W:2 * W] = (acc * jnp.concatenate([s_ref[...]] * A_HEADS, axis=1)).astype(BF16)

    row = pl.BlockSpec((tm, A_PAD), lambda i: (i, 0))
    return _pallas(body, name=name, grid=(T // tm,), in_specs=[pl.BlockSpec((n, tm, W), lambda i: (0, i, 0)), row, row],
                   out_specs=pl.BlockSpec((tm, 2 * W), lambda i: (i, 0)), out_shape=jax.ShapeDtypeStruct((T, 2 * W), BF16),
                   compiler_params=_params("parallel"))(dq_part, cq, sq)


HB = 8 * B_CHUNK


def _chunk_masks(reverse):
    r = lax.broadcasted_iota(jnp.int32, (HB, HB), 0)
    c = lax.broadcasted_iota(jnp.int32, (HB, HB), 1)
    same = (r // B_CHUNK) == (c // B_CHUNK)
    incl = same & ((c >= r) if reverse else (c <= r))
    return same, incl


def _mask_mm(mask, x):
    hi = x.astype(BF16)
    lo = (x - hi.astype(F32)).astype(BF16)
    return _dot(mask, hi, _NN) + _dot(mask, lo, _NN)


def _hgrn_gates(q, z, lb, reverse):
    same, incl = _chunk_masks(reverse)
    sg = jax.nn.sigmoid(z)
    f = lb + (1.0 - lb) * sg
    lf = jnp.log(jnp.maximum(f, TINY))
    kk = (1.0 - lb) * jax.nn.sigmoid(-z)
    b = _mask_mm(incl.astype(BF16), lf)
    edge = 0 if reverse else B_CHUNK - 1
    btot = jnp.concatenate([jnp.broadcast_to(b[B_CHUNK * c + edge:B_CHUNK * c + edge + 1, :], (B_CHUNK, b.shape[1]))
                            for c in range(HB // B_CHUNK)], axis=0)
    eb, enb, er, dec = jnp.exp(b), jnp.exp(-b), jnp.exp(btot - b), jnp.exp(btot)
    return dict(same=same, incl=incl, sg=sg, f=f, kk=kk, eb=eb, enb=enb, er=er, dec=dec,
                qd=q * eb, ki=kk * enb, ke=kk * er)


def _hgrn_specs(T, reverse, gate_reverse):
    nb = T // HB
    blk = (lambda i: nb - 1 - i) if reverse else (lambda i: i)
    wide = B_HEADS * B_DK
    return nb, blk, [
        pl.BlockSpec((HB, wide), lambda i: (blk(i), 0)),
        pl.BlockSpec((HB, wide), lambda i: (blk(i), 2 if gate_reverse else 1)),
        pl.BlockSpec((HB, B_HEADS * B_DV), lambda i: (blk(i), 6)),
        pl.BlockSpec((1, wide), lambda i: (0, 0)),
    ]


def _hk(h):
    return slice(B_DK * h, B_DK * (h + 1))


def _hv(h):
    return slice(B_DV * h, B_DV * (h + 1))


def _crows(c):
    return slice(B_CHUNK * c, B_CHUNK * (c + 1))


def _chunk_selectors():
    r = lax.broadcasted_iota(jnp.int32, (HB, 1), 0) // B_CHUNK
    l = lax.broadcasted_iota(jnp.int32, (1, HB), 1) // B_CHUNK
    return [r == c for c in range(8)], [l == c for c in range(8)]


def _hgrn_fwd(zb, lb, *, reverse, name):
    T = zb.shape[0]
    nb, blk, in_specs = _hgrn_specs(T, reverse, reverse)
    order = range(7, -1, -1) if reverse else range(8)
    heads = range(B_HEADS)

    def body(q_ref, z_ref, v_ref, lb_ref, o_ref, st_ref, s_sc):
        @pl.when(pl.program_id(0) == 0)
        def _():
            s_sc[...] = jnp.zeros(s_sc.shape, F32)

        g = _hgrn_gates(q_ref[...], z_ref[...], lb_ref[...], reverse)
        v = v_ref[...].astype(BF16)
        qd, ki, ke = g["qd"].astype(BF16), g["ki"].astype(BF16), g["ke"].astype(BF16)
        dec = g["dec"]
        in_chunk_rows, in_chunk_lanes = _chunk_selectors()
        o_intra, upd = [], []
        for h in heads:
            a = jnp.where(g["incl"], _dot(qd[:, _hk(h)], ki[:, _hk(h)], _NT), 0.0)
            o_intra.append(_dot(a.astype(BF16), v[:, _hv(h)], _NN))
            vt = v[:, _hv(h)].T
            lhs = jnp.concatenate([jnp.where(in_chunk_lanes[c], vt, 0) for c in range(8)], axis=0)
            upd.append(_dot(lhs, ke[:, _hk(h)], _NN))
        st = [s_sc[h] for h in heads]
        snap = [[None] * 8 for _ in heads]
        for c in order:
            for h in heads:
                snap[h][c] = st[h]
                st[h] = st[h] * dec[B_CHUNK * c:B_CHUNK * c + 1, _hk(h)] + upd[h][B_DV * c:B_DV * (c + 1), :]
        for h in heads:
            s_sc[h] = st[h]
            for c in range(8):
                st_ref[h, c] = snap[h][c]
            qd_big = jnp.concatenate([jnp.where(in_chunk_rows[c], qd[:, _hk(h)], 0) for c in range(8)], axis=1)
            states = jnp.concatenate([snap[h][c].astype(BF16) for c in range(8)], axis=1)
            o_ref[h] = o_intra[h] + _dot(qd_big, states, _NT)

    return _pallas(
        body, name=name, grid=(nb,), in_specs=in_specs,
        out_specs=[pl.BlockSpec((B_HEADS, HB, B_DV), lambda i: (0, blk(i), 0)),
                   pl.BlockSpec((B_HEADS, 8, B_DV, B_DK), lambda i: (0, blk(i), 0, 0))],
        out_shape=[jax.ShapeDtypeStruct((B_HEADS, T, B_DV), F32),
                   jax.ShapeDtypeStruct((B_HEADS, T // B_CHUNK, B_DV, B_DK), F32)],
        scratch_shapes=[pltpu.VMEM((B_HEADS, B_DV, B_DK), F32)],
        compiler_params=_params("arbitrary"))(zb, zb, zb, lb)


def _hgrn_bwd(zb, lb, do, states, *, reverse, name):
    T = zb.shape[0]
    nb, blk, in_specs = _hgrn_specs(T, not reverse, reverse)
    order = range(8) if reverse else range(7, -1, -1)
    heads = range(B_HEADS)

    def body(q_ref, z_ref, v_ref, lb_ref, do_ref, st_ref, dq_ref, dz_ref, dv_ref, dlb_ref, ds_sc):
        @pl.when(pl.program_id(0) == 0)
        def _():
            ds_sc[...] = jnp.zeros(ds_sc.shape, F32)
            dlb_ref[...] = jnp.zeros(dlb_ref.shape, F32)

        lb = lb_ref[...]
        g = _hgrn_gates(q_ref[...], z_ref[...], lb, reverse)
        v = v_ref[...].astype(BF16)
        qd, ki, ke = g["qd"].astype(BF16), g["ki"].astype(BF16), g["ke"].astype(BF16)
        dec = g["dec"]
        dout = [do_ref[h].astype(BF16) for h in heads]
        in_chunk_rows, in_chunk_lanes = _chunk_selectors()
        _, incl_t = _chunk_masks(not reverse)
        rows_of = lambda x: jnp.concatenate([jnp.where(in_chunk_rows[c], x, 0) for c in range(8)], axis=1)
        dv_i, dqd_h, dki, upd = [], [], [], []
        for h in heads:
            qd_h, ki_h, v_h = qd[:, _hk(h)], ki[:, _hk(h)], v[:, _hv(h)]
            da = jnp.where(g["incl"], _dot(dout[h], v_h, _NT), 0.0).astype(BF16)
            at = jnp.where(incl_t, _dot(ki_h, qd_h, _NT), 0.0).astype(BF16)
            dat = jnp.where(incl_t, _dot(v_h, dout[h], _NT), 0.0).astype(BF16)
            dv_i.append(_dot(at, dout[h], _NN))
            dki.append(_dot(dat, qd_h, _NN))
            dot_t = dout[h].T
            lhs = jnp.concatenate([jnp.where(in_chunk_lanes[c], dot_t, 0) for c in range(8)], axis=0)
            upd.append(_dot(lhs, qd_h, _NN))
            saved = jnp.concatenate([st_ref[h, c].astype(BF16) for c in range(8)], axis=0)
            dqd_h.append(_dot(da, ki_h, _NN) + _dot(rows_of(dout[h]), saved, _NN))
        dst = [ds_sc[h] for h in heads]
        used = [[None] * 8 for _ in heads]
        for c in order:
            for h in heads:
                used[h][c] = dst[h]
                dst[h] = dst[h] * dec[B_CHUNK * c:B_CHUNK * c + 1, _hk(h)] + upd[h][B_DV * c:B_DV * (c + 1), :]
        dke_h, ddec_h = [], []
        for h in heads:
            ds_sc[h] = dst[h]
            used16 = [used[h][c].astype(BF16) for c in range(8)]
            dv_ref[h] = dv_i[h] + _dot(rows_of(ke[:, _hk(h)]), jnp.concatenate(used16, axis=1), _NT)
            dke_h.append(_dot(rows_of(v[:, _hv(h)]), jnp.concatenate(used16, axis=0), _NN))
            ddec_p = []
            for c in range(8):
                tot = jnp.sum(used[h][c] * st_ref[h, c], axis=0, keepdims=True) * dec[B_CHUNK * c:B_CHUNK * c + 1, _hk(h)]
                ddec_p.append(jnp.broadcast_to(tot, (B_CHUNK, B_DK)))
            ddec_h.append(jnp.concatenate(ddec_p, axis=0))
        dqd = jnp.concatenate(dqd_h, axis=1)
        dki = jnp.concatenate(dki, axis=1)
        dke = jnp.concatenate(dke_h, axis=1)
        db = dqd * g["qd"] - dki * g["ki"] - dke * g["ke"]
        masks = jnp.concatenate([incl_t.astype(BF16), g["same"].astype(BF16)], axis=1)
        dlf = _mask_mm(masks, jnp.concatenate([db, dke * g["ke"]], axis=0)) + jnp.concatenate(ddec_h, axis=1)
        dk = dki * g["enb"] + dke * g["er"]
        u = jnp.where(g["f"] > TINY, dlf / g["f"], 0.0) - dk
        sg = g["sg"]
        dq_ref[...] = dqd * g["eb"]
        dz_ref[...] = u * (1.0 - lb) * sg * (1.0 - sg)
        dlb_ref[...] += jnp.sum(u * (1.0 - sg), axis=0, keepdims=True)

    wide = pl.BlockSpec((HB, B_HEADS * B_DK), lambda i: (blk(i), 0))
    hm = pl.BlockSpec((B_HEADS, HB, B_DV), lambda i: (0, blk(i), 0))
    return _pallas(
        body, name=name, grid=(nb,),
        in_specs=in_specs + [hm, pl.BlockSpec((B_HEADS, 8, B_DV, B_DK), lambda i: (0, blk(i), 0, 0))],
        out_specs=[wide, wide, hm, pl.BlockSpec((1, B_HEADS * B_DK), lambda i: (0, 0))],
        out_shape=[jax.ShapeDtypeStruct((T, B_HEADS * B_DK), F32), jax.ShapeDtypeStruct((T, B_HEADS * B_DK), F32),
                   jax.ShapeDtypeStruct((B_HEADS, T, B_DV), F32), jax.ShapeDtypeStruct((1, B_HEADS * B_DK), F32)],
        scratch_shapes=[pltpu.VMEM((B_HEADS, B_DV, B_DK), F32)],
        compiler_params=_params("arbitrary"))(zb, zb, zb, lb, do, states)


def _hgrn_out(of, ob, zb, gout, *, name):
    T = zb.shape[0]
    tm = _tile(T, 512, 8)

    def body(of_ref, ob_ref, g_ref, gout_ref, y_ref):
        for h in range(B_HEADS):
            o = of_ref[h] + ob_ref[h]
            r = lax.rsqrt(jnp.mean(o * o, axis=-1, keepdims=True) + EPS)
            gh = g_ref[:, B_DV * h:B_DV * (h + 1)]
            y_ref[:, B_DV * h:B_DV * (h + 1)] = (o * r * gout_ref[...] * (gh * jax.nn.sigmoid(gh))).astype(BF16)

    hm = pl.BlockSpec((B_HEADS, tm, B_DV), lambda i: (0, i, 0))
    return _pallas(
        body, name=name, grid=(T // tm,),
        in_specs=[hm, hm, pl.BlockSpec((tm, 512), lambda i: (i, 7)), pl.BlockSpec((1, B_DV), lambda i: (0, 0))],
        out_specs=pl.BlockSpec((tm, 512), lambda i: (i, 0)),
        out_shape=jax.ShapeDtypeStruct((T, 512), BF16), compiler_params=_params("parallel"))(of, ob, zb, gout.reshape(1, B_DV))


def _hgrn_out_bwd(of, ob, zb, gout, dy, *, name):
    T = zb.shape[0]
    tm = _tile(T, 512, 8)

    def body(of_ref, ob_ref, g_ref, gout_ref, dy_ref, do_ref, dg_ref, dgo_ref):
        gout_v = gout_ref[...]
        acc = jnp.zeros((1, B_DV), F32)
        for h in range(B_HEADS):
            o = of_ref[h] + ob_ref[h]
            r = lax.rsqrt(jnp.mean(o * o, axis=-1, keepdims=True) + EPS)
            oh = o * r
            gh = g_ref[:, B_DV * h:B_DV * (h + 1)]
            sg = jax.nn.sigmoid(gh)
            dyh = dy_ref[:, B_DV * h:B_DV * (h + 1)].astype(F32)
            dn = dyh * (gh * sg)
            dg_ref[:, B_DV * h:B_DV * (h + 1)] = dyh * (oh * gout_v) * (sg * (1.0 + gh * (1.0 - sg)))
            dxh = dn * gout_v
            do_ref[h] = r * (dxh - oh * jnp.mean(dxh * oh, axis=-1, keepdims=True))
            acc = acc + jnp.sum(dn * oh, axis=0, keepdims=True)

        @pl.when(pl.program_id(0) == 0)
        def _():
            dgo_ref[...] = acc

        @pl.when(pl.program_id(0) > 0)
        def _():
            dgo_ref[...] += acc

    hm = pl.BlockSpec((B_HEADS, tm, B_DV), lambda i: (0, i, 0))
    row = pl.BlockSpec((tm, 512), lambda i: (i, 0))
    return _pallas(
        body, name=name, grid=(T // tm,),
        in_specs=[hm, hm, pl.BlockSpec((tm, 512), lambda i: (i, 7)), pl.BlockSpec((1, B_DV), lambda i: (0, 0)), row],
        out_specs=[hm, row, pl.BlockSpec((1, B_DV), lambda i: (0, 0))],
        out_shape=[jax.ShapeDtypeStruct((B_HEADS, T, B_DV), F32), jax.ShapeDtypeStruct((T, 512), F32),
                   jax.ShapeDtypeStruct((1, B_DV), F32)],
        compiler_params=_params("arbitrary"))(of, ob, zb, gout.reshape(1, B_DV), dy)


def _dzb_assemble(dq_f, dq_b, dzf, dzb_, dv_f, dv_b, dgate, *, name):
    T = dq_f.shape[0]
    tm = _tile(T, 256, 8)

    def body(qf, qb, zf, zr, vf, vr, dg, o_ref):
        o_ref[:, 0:1024] = (qf[...] + qb[...]).astype(BF16)
        o_ref[:, 1024:2048] = zf[...].astype(BF16)
        o_ref[:, 2048:3072] = zr[...].astype(BF16)
        for h in range(B_HEADS):
            o_ref[:, 3072 + B_DV * h:3072 + B_DV * (h + 1)] = (vf[h] + vr[h]).astype(BF16)
        o_ref[:, 3584:4096] = dg[...].astype(BF16)

    wide = pl.BlockSpec((tm, 1024), lambda i: (i, 0))
    hm = pl.BlockSpec((B_HEADS, tm, B_DV), lambda i: (0, i, 0))
    return _pallas(
        body, name=name, grid=(T // tm,),
        in_specs=[wide, wide, wide, wide, hm, hm, pl.BlockSpec((tm, 512), lambda i: (i, 0))],
        out_specs=pl.BlockSpec((tm, 4096), lambda i: (i, 0)),
        out_shape=jax.ShapeDtypeStruct((T, 4096), BF16), compiler_params=_params("parallel"))(dq_f, dq_b, dzf, dzb_, dv_f, dv_b, dgate)


C_SPAN = 3 * C_BLOCK
C_G = C_HEADS // C_KV_HEADS


def _t5_bucket(rel):
    nb = REL_BUCKETS // 2
    max_exact = nb // 2
    ret = (rel > 0).astype(jnp.int32) * nb
    n = jnp.abs(rel)
    large = max_exact + (jnp.log(jnp.maximum(n, 1).astype(F32) / max_exact)
                         / math.log(REL_MAX_DIST / max_exact) * (nb - max_exact)).astype(jnp.int32)
    large = jnp.minimum(large, nb - 1)
    return ret + jnp.where(n < max_exact, n, large)


def _swa_buckets():
    rel = jnp.arange(C_SPAN)[None, :] - C_BLOCK - jnp.arange(C_BLOCK)[:, None]
    return _t5_bucket(rel)


def _swa_specs(T):
    nb = T // C_BLOCK
    return nb, [
        pl.BlockSpec((C_BLOCK, 512), lambda n: (n, 0)),
        pl.BlockSpec((C_BLOCK, LANES), lambda n: (jnp.maximum(n - 1, 0), 4)),
        pl.BlockSpec((C_BLOCK, LANES), lambda n: (n, 4)),
        pl.BlockSpec((C_BLOCK, LANES), lambda n: (jnp.minimum(n + 1, nb - 1), 4)),
        pl.BlockSpec((C_BLOCK, LANES), lambda n: (jnp.maximum(n - 1, 0), 5)),
        pl.BlockSpec((C_BLOCK, LANES), lambda n: (n, 5)),
        pl.BlockSpec((C_BLOCK, LANES), lambda n: (jnp.minimum(n + 1, nb - 1), 5)),
        pl.BlockSpec((C_HEADS, C_BLOCK, C_SPAN), lambda n: (0, 0, 0)),
        pl.BlockSpec(memory_space=pltpu.SMEM),
    ]


def _swa_valid(n, T):
    qi = lax.broadcasted_iota(jnp.int32, (C_BLOCK, C_SPAN), 0)
    si = lax.broadcasted_iota(jnp.int32, (C_BLOCK, C_SPAN), 1)
    rel = si - C_BLOCK - qi
    kpos = (n - 1) * C_BLOCK + si
    return (jnp.abs(rel) <= C_WINDOW) & (kpos >= 0) & (kpos < T)


def _swa_softmax(raw, bias, valid, sink):
    s = raw * (C_DH ** -0.5) + bias
    s = jnp.where(valid, s, MASK_VALUE)
    m = jnp.maximum(jnp.max(s, axis=-1, keepdims=True), sink)
    e = jnp.exp(s - m)
    den = jnp.sum(e, axis=-1, keepdims=True) + jnp.exp(sink - m)
    return e / den, jnp.exp(sink - m) / den


def _swa_fwd(zc, bias, sink, *, name):
    T = zc.shape[0]
    nb, in_specs = _swa_specs(T)

    def body(q_ref, kp, kc, kn, vp, vc, vn, bias_ref, sink_ref, y_ref):
        n = pl.program_id(0)
        kcat = jnp.concatenate([kp[...], kc[...], kn[...]], axis=0)
        vcat = jnp.concatenate([vp[...], vc[...], vn[...]], axis=0)
        valid = _swa_valid(n, T)
        heads = range(C_HEADS)
        kvs = [slice(C_DH * (h // C_G), C_DH * (h // C_G + 1)) for h in heads]
        scores = [_dot(q_ref[:, C_DH * h:C_DH * (h + 1)], kcat[:, kvs[h]], _NT) for h in heads]
        probs = [_swa_softmax(scores[h], bias_ref[h], valid, sink_ref[h])[0].astype(BF16) for h in heads]
        for h in heads:
            y_ref[:, C_DH * h:C_DH * (h + 1)] = _dot(probs[h], vcat[:, kvs[h]], _NN).astype(BF16)

    return _pallas(
        body, name=name, grid=(nb,), in_specs=in_specs, out_specs=pl.BlockSpec((C_BLOCK, 512), lambda n: (n, 0)),
        out_shape=jax.ShapeDtypeStruct((T, 512), BF16), compiler_params=_params("parallel"))(zc, zc, zc, zc, zc, zc, zc, bias, sink)


def _swa_bwd(zc, bias, sink, dy, *, name):
    T = zc.shape[0]
    nb, in_specs = _swa_specs(T)
    scale = C_DH ** -0.5

    def body(q_ref, kp, kc, kn, vp, vc, vn, bias_ref, sink_ref, dy_ref, dq_ref, dkc_ref, dvc_ref, dbias_ref, dsink_ref):
        n = pl.program_id(0)

        @pl.when(n == 0)
        def _():
            dbias_ref[...] = jnp.zeros(dbias_ref.shape, F32)
            dsink_ref[...] = jnp.zeros(dsink_ref.shape, F32)

        kcat = jnp.concatenate([kp[...], kc[...], kn[...]], axis=0)
        vcat = jnp.concatenate([vp[...], vc[...], vn[...]], axis=0)
        valid = _swa_valid(n, T)
        heads = range(C_HEADS)
        kvs = [slice(C_DH * (h // C_G), C_DH * (h // C_G + 1)) for h in heads]
        qs = [q_ref[:, C_DH * h:C_DH * (h + 1)] for h in heads]
        dos = [dy_ref[:, C_DH * h:C_DH * (h + 1)].astype(BF16) for h in heads]
        scores = [_dot(qs[h], kcat[:, kvs[h]], _NT) for h in heads]
        dps = [_dot(dos[h], vcat[:, kvs[h]], _NT) for h in heads]
        pbs, dsbs = [], []
        for h in heads:
            p, p_sink = _swa_softmax(scores[h], bias_ref[h], valid, sink_ref[h])
            rowdot = jnp.sum(p * dps[h], axis=-1, keepdims=True)
            ds = p * (dps[h] - rowdot)
            dbias_ref[h] += ds
            tot = jnp.sum(jnp.sum(-p_sink * rowdot, axis=0, keepdims=True), axis=1, keepdims=True)
            dsink_ref[h:h + 1, :] += jnp.broadcast_to(tot, (1, LANES))
            pbs.append(p.astype(BF16))
            dsbs.append((ds * scale).astype(BF16))
        for h in heads:
            dq_ref[:, C_DH * h:C_DH * (h + 1)] = _dot(dsbs[h], kcat[:, kvs[h]], _NN).astype(BF16)
        dks = [_dot(dsbs[h], qs[h], _TN) for h in heads]
        dvs = [_dot(pbs[h], dos[h], _TN) for h in heads]
        for kv in range(C_KV_HEADS):
            group = range(kv * C_G, (kv + 1) * C_G)
            dkc_ref[0, :, C_DH * kv:C_DH * (kv + 1)] = sum(dks[h] for h in group)
            dvc_ref[0, :, C_DH * kv:C_DH * (kv + 1)] = sum(dvs[h] for h in group)

    part = pl.BlockSpec((1, C_SPAN, LANES), lambda n: (n, 0, 0))
    dq, dkc, dvc, dbias, dsink = _pallas(
        body, name=name, grid=(nb,), in_specs=in_specs + [pl.BlockSpec((C_BLOCK, 512), lambda n: (n, 0))],
        out_specs=[pl.BlockSpec((C_BLOCK, 512), lambda n: (n, 0)), part, part,
                   pl.BlockSpec((C_HEADS, C_BLOCK, C_SPAN), lambda n: (0, 0, 0)), pl.BlockSpec((C_HEADS, LANES), lambda n: (0, 0))],
        out_shape=[jax.ShapeDtypeStruct((T, 512), BF16), jax.ShapeDtypeStruct((nb, C_SPAN, LANES), F32),
                   jax.ShapeDtypeStruct((nb, C_SPAN, LANES), F32), jax.ShapeDtypeStruct((C_HEADS, C_BLOCK, C_SPAN), F32),
                   jax.ShapeDtypeStruct((C_HEADS, LANES), F32)],
        compiler_params=_params("arbitrary"))(zc, zc, zc, zc, zc, zc, zc, bias, sink, dy)

    def combine(dq_ref, kp, kc, kn, vp, vc, vn, o_ref):
        n = pl.program_id(0)
        lo = (n > 0).astype(F32)
        hi = (n < nb - 1).astype(F32)
        o_ref[:, 0:512] = dq_ref[...]
        o_ref[:, 512:640] = (kp[0] * lo + kc[0] + kn[0] * hi).astype(BF16)
        o_ref[:, 640:768] = (vp[0] * lo + vc[0] + vn[0] * hi).astype(BF16)

    prev = pl.BlockSpec((1, C_BLOCK, LANES), lambda n: (jnp.maximum(n - 1, 0), 2, 0))
    cur = pl.BlockSpec((1, C_BLOCK, LANES), lambda n: (n, 1, 0))
    nxt = pl.BlockSpec((1, C_BLOCK, LANES), lambda n: (jnp.minimum(n + 1, nb - 1), 0, 0))
    dzc = _pallas(
        combine, name=name + "_combine", grid=(nb,),
        in_specs=[pl.BlockSpec((C_BLOCK, 512), lambda n: (n, 0)), prev, cur, nxt, prev, cur, nxt],
        out_specs=pl.BlockSpec((C_BLOCK, 768), lambda n: (n, 0)),
        out_shape=jax.ShapeDtypeStruct((T, 768), BF16), compiler_params=_params("parallel"))(dq, dkc, dkc, dkc, dvc, dvc, dvc)
    return dzc, dbias, dsink


def _merge_tiles(T):
    return _tile(T, 512, 8), 512


def _merge_fwd(ya, yb, yc, wa, wb, wc, zg, *, name):
    T = ya.shape[0]
    tm, tn = _merge_tiles(T)
    nd = D_MODEL // tn

    def body(ya_ref, yb_ref, yc_ref, wa_ref, wb_ref, wc_ref, ga_ref, gb_ref, gc_ref, o_ref):
        acc = jax.nn.sigmoid(ga_ref[...]) * _dot(ya_ref[...].astype(BF16), wa_ref[...], _NN)
        acc += jax.nn.sigmoid(gb_ref[...]) * _dot(yb_ref[...].astype(BF16), wb_ref[...], _NN)
        acc += jax.nn.sigmoid(gc_ref[...]) * _dot(yc_ref[...].astype(BF16), wc_ref[...], _NN)
        o_ref[...] = acc.astype(BF16)

    y = pl.BlockSpec((tm, 512), lambda i, j: (i, 0))
    w = pl.BlockSpec((512, tn), lambda i, j: (0, j))
    gate = lambda b: pl.BlockSpec((tm, tn), lambda i, j: (i, b * nd + j))
    return _pallas(
        body, name=name, grid=(T // tm, nd), in_specs=[y, y, y, w, w, w, gate(0), gate(1), gate(2)],
        out_specs=pl.BlockSpec((tm, tn), lambda i, j: (i, j)),
        out_shape=jax.ShapeDtypeStruct((T, D_MODEL), BF16),
        compiler_params=_params("parallel", "parallel"))(ya, yb, yc, wa, wb, wc, zg, zg, zg)


def _merge_bwd(ya, yb, yc, wa, wb, wc, zg, dm, *, name):
    T = ya.shape[0]
    tm, tn = _merge_tiles(T)
    nd = D_MODEL // tn

    def body(ya_ref, yb_ref, yc_ref, wa_ref, wb_ref, wc_ref, ga_ref, gb_ref, gc_ref, dm_ref, *outs):
        dmv = dm_ref[...].astype(F32)
        for y_ref, w_ref, g_ref, du_ref, dg_ref in zip((ya_ref, yb_ref, yc_ref), (wa_ref, wb_ref, wc_ref),
                                                       (ga_ref, gb_ref, gc_ref), outs[:3], outs[3:]):
            u = _dot(y_ref[...].astype(BF16), w_ref[...], _NN)
            sg = jax.nn.sigmoid(g_ref[...])
            du_ref[...] = (dmv * sg).astype(BF16)
            dg_ref[...] = (dmv * u * sg * (1.0 - sg)).astype(BF16)

    y = pl.BlockSpec((tm, 512), lambda i, j: (i, 0))
    w = pl.BlockSpec((512, tn), lambda i, j: (0, j))
    gate = lambda b: pl.BlockSpec((tm, tn), lambda i, j: (i, b * nd + j))
    t = pl.BlockSpec((tm, tn), lambda i, j: (i, j))
    return _pallas(
        body, name=name, grid=(T // tm, nd), in_specs=[y, y, y, w, w, w, gate(0), gate(1), gate(2), t],
        out_specs=[t] * 6, out_shape=[jax.ShapeDtypeStruct((T, D_MODEL), BF16)] * 6,
        compiler_params=_params("parallel", "parallel"))(ya, yb, yc, wa, wb, wc, zg, zg, zg, dm)


def _cross_fwd(q, kvm, *, name):
    T = q.shape[0]
    M = kvm.shape[0]
    tm = _tile(T, 512, 8)
    scale = X_DH ** -0.5

    def body(q_ref, k_ref, v_ref, o_ref):
        for h in range(X_HEADS):
            cs = slice(X_DH * h, X_DH * (h + 1))
            s = _dot(q_ref[:, cs], k_ref[:, cs], _NT) * scale
            e = jnp.exp(s - jnp.max(s, axis=-1, keepdims=True))
            p = e / jnp.sum(e, axis=-1, keepdims=True)
            o_ref[:, cs] = _dot(p.astype(BF16), v_ref[:, cs], _NN).astype(BF16)

    row = pl.BlockSpec((tm, D_MODEL), lambda i: (i, 0))
    return _pallas(
        body, name=name, grid=(T // tm,),
        in_specs=[row, pl.BlockSpec((M, D_MODEL), lambda i: (0, 0)), pl.BlockSpec((M, D_MODEL), lambda i: (0, 1))],
        out_specs=row, out_shape=jax.ShapeDtypeStruct((T, D_MODEL), BF16), compiler_params=_params("parallel"))(q, kvm, kvm)


def _cross_bwd(q, kvm, do, *, name):
    T = q.shape[0]
    M = kvm.shape[0]
    tm = _tile(T, 512, 8)
    scale = X_DH ** -0.5

    def body(q_ref, k_ref, v_ref, do_ref, dq_ref, dkv_ref):
        @pl.when(pl.program_id(0) == 0)
        def _():
            dkv_ref[...] = jnp.zeros(dkv_ref.shape, F32)

        for h in range(X_HEADS):
            cs = slice(X_DH * h, X_DH * (h + 1))
            vs = slice(D_MODEL + X_DH * h, D_MODEL + X_DH * (h + 1))
            qh, kh, doh = q_ref[:, cs], k_ref[:, cs], do_ref[:, cs]
            s = _dot(qh, kh, _NT) * scale
            e = jnp.exp(s - jnp.max(s, axis=-1, keepdims=True))
            p = e / jnp.sum(e, axis=-1, keepdims=True)
            dp = _dot(doh, v_ref[:, cs], _NT)
            ds = (p * (dp - jnp.sum(p * dp, axis=-1, keepdims=True)) * scale).astype(BF16)
            dq_ref[:, cs] = _dot(ds, kh, _NN).astype(BF16)
            dkv_ref[:, cs] += _dot(ds, qh, _TN)
            dkv_ref[:, vs] += _dot(p.astype(BF16), doh, _TN)

    row = pl.BlockSpec((tm, D_MODEL), lambda i: (i, 0))
    return _pallas(
        body, name=name, grid=(T // tm,),
        in_specs=[row, pl.BlockSpec((M, D_MODEL), lambda i: (0, 0)), pl.BlockSpec((M, D_MODEL), lambda i: (0, 1)), row],
        out_specs=[row, pl.BlockSpec((M, 2 * D_MODEL), lambda i: (0, 0))],
        out_shape=[jax.ShapeDtypeStruct((T, D_MODEL), BF16), jax.ShapeDtypeStruct((M, 2 * D_MODEL), F32)],
        compiler_params=_params("arbitrary"))(q, kvm, kvm, do)


def _ffn_up(h, w1, w3, *, name):
    T = h.shape[0]
    tm = _tile(T, 512, 8)
    tn = D_FF // 2

    def body(h_ref, w1_ref, w3_ref, a_ref, b_ref, act_ref):
        hv = h_ref[...]
        a = _dot(hv, w1_ref[...], _NN)
        b = _dot(hv, w3_ref[...], _NN)
        a_ref[...] = a
        b_ref[...] = b
        act_ref[...] = (a * jax.nn.sigmoid(a) * b).astype(BF16)

    w = pl.BlockSpec((D_MODEL, tn), lambda i, j: (0, j))
    t = pl.BlockSpec((tm, tn), lambda i, j: (i, j))
    return _pallas(
        body, name=name, grid=(T // tm, D_FF // tn), in_specs=[pl.BlockSpec((tm, D_MODEL), lambda i, j: (i, 0)), w, w],
        out_specs=[t, t, t],
        out_shape=[jax.ShapeDtypeStruct((T, D_FF), F32), jax.ShapeDtypeStruct((T, D_FF), F32), jax.ShapeDtypeStruct((T, D_FF), BF16)],
        compiler_params=_params("parallel", "parallel"))(h, w1, w3)


def _ffn_dact(dx, w2, a, b, *, name):
    T = dx.shape[0]
    tm = _tile(T, 512, 8)
    tn = D_FF // 2

    def body(dx_ref, w2_ref, a_ref, b_ref, da_ref, db_ref):
        dact = _dot(dx_ref[...].astype(BF16), w2_ref[...], _NT)
        av = a_ref[...]
        sg = jax.nn.sigmoid(av)
        da_ref[...] = (dact * b_ref[...] * (sg * (1.0 + av * (1.0 - sg)))).astype(BF16)
        db_ref[...] = (dact * (av * sg)).astype(BF16)

    t = pl.BlockSpec((tm, tn), lambda i, j: (i, j))
    return _pallas(
        body, name=name, grid=(T // tm, D_FF // tn),
        in_specs=[pl.BlockSpec((tm, D_MODEL), lambda i, j: (i, 0)), pl.BlockSpec((tn, D_MODEL), lambda i, j: (j, 0)), t, t],
        out_specs=[t, t], out_shape=[jax.ShapeDtypeStruct((T, D_FF), BF16)] * 2,
        compiler_params=_params("parallel", "parallel"))(dx, w2, a, b)


def _loss_head(x, g, target, *, name):
    T, D = x.shape
    tm = _tile(T, 512, 8)

    def body(x_ref, g_ref, t_ref, loss_ref, dx_ref, dg_ref):
        xv = x_ref[...]
        r = lax.rsqrt(jnp.mean(xv * xv, axis=-1, keepdims=True) + EPS)
        xh = xv * r
        gv = g_ref[...]
        err = xh * gv - t_ref[...]
        dy = err * (1.0 / D)
        dxh = dy * gv
        dx_ref[...] = r * (dxh - xh * jnp.mean(dxh * xh, axis=-1, keepdims=True))
        lpart = 0.5 * jnp.sum(jnp.mean(err * err, axis=-1, keepdims=True), axis=0, keepdims=True)
        gpart = jnp.sum(dy * xh, axis=0, keepdims=True)

        @pl.when(pl.program_id(0) == 0)
        def _():
            loss_ref[...] = jnp.broadcast_to(lpart, (1, LANES))
            dg_ref[...] = gpart

        @pl.when(pl.program_id(0) > 0)
        def _():
            loss_ref[...] += jnp.broadcast_to(lpart, (1, LANES))
            dg_ref[...] += gpart

    row = pl.BlockSpec((tm, D), lambda i: (i, 0))
    vec = pl.BlockSpec((1, D), lambda i: (0, 0))
    return _pallas(
        body, name=name, grid=(T // tm,), in_specs=[row, vec, row],
        out_specs=[pl.BlockSpec((1, LANES), lambda i: (0, 0)), row, vec],
        out_shape=[jax.ShapeDtypeStruct((1, LANES), F32), jax.ShapeDtypeStruct((T, D), F32), jax.ShapeDtypeStruct((1, D), F32)],
        compiler_params=_params("arbitrary"))(x, g.reshape(1, D), target)


IN_CQ, IN_CKV, IN_KR, IN_B, IN_C, IN_G, IN_END = 0, 384, 640, 672, 4768, 5536, 8608
WEIGHT_NAMES = ("w_in", "g_mix", "a_gq", "a_gkv", "a_wuq", "a_wukv", "b_lb", "b_gout", "c_sink", "rel_bias",
                "w_br_a", "w_br_b", "w_br_c", "w_out", "g_x", "g_mem", "x_wq", "x_wkv", "x_wo", "g_ffn",
                "f_w1", "f_w3", "f_w2", "g_final")


def _lower_bounds(b_lb):
    sm = jax.nn.softmax(b_lb.astype(F32), axis=1)
    return jnp.cumsum(sm, axis=1) - sm[:, :1]


def _layer_weights(w, l):
    bf = lambda a: a.astype(BF16)
    w_in = bf(w["w_in"][l])
    kr = w_in[:, IN_KR:IN_B]
    wa = jnp.concatenate([w_in[:, IN_CQ:IN_CKV], kr, _rope_swap_cols(kr), jnp.zeros((D_MODEL, 64), BF16),
                          w_in[:, IN_CKV:IN_KR]], axis=1)
    wuq = bf(w["a_wuq"][l]).reshape(A_Q_RANK, A_HEADS, A_QK)
    zeros = lambda n: jnp.zeros((A_Q_RANK, A_HEADS, n), BF16)
    wuq_pad = jnp.concatenate([wuq, zeros(A_PAD - A_QK)], axis=-1)
    wuq_sw = jnp.concatenate([zeros(A_NOPE), _rope_swap_cols(wuq[..., A_NOPE:]), zeros(A_PAD - A_QK)], axis=-1)
    wq2 = jnp.concatenate([wuq_pad.reshape(A_Q_RANK, -1), wuq_sw.reshape(A_Q_RANK, -1)], axis=1)
    wukv = bf(w["a_wukv"][l]).reshape(A_KV_RANK, A_HEADS, A_NOPE + A_V)
    wkv = jnp.concatenate([wukv[..., :A_NOPE].reshape(A_KV_RANK, -1), wukv[..., A_NOPE:].reshape(A_KV_RANK, -1)], axis=1)
    return dict(wa=wa, wb=w_in[:, IN_B:IN_C], wc=w_in[:, IN_C:IN_G], wg=w_in[:, IN_G:IN_END], wq2=wq2, wkv=wkv,
                w_br_a=bf(w["w_br_a"][l]), w_br_b=bf(w["w_br_b"][l]), w_br_c=bf(w["w_br_c"][l]), w_out=bf(w["w_out"][l]),
                x_wq=bf(w["x_wq"][l]), x_wkv=bf(w["x_wkv"][l]), x_wo=bf(w["x_wo"][l]),
                f_w1=bf(w["f_w1"][l]), f_w3=bf(w["f_w3"][l]), f_w2=bf(w["f_w2"][l]))


def _layer_fwd(l, x, mem, w, lw, lower, bias, tabs):
    n = lambda s: f"l{l}_{s}"
    cq_t, sq_t, ck, _ = tabs
    s = dict(x=x)
    s["h0"] = h0 = _rms(x, w["g_mix"][l], name=n("rms_mix"))
    s["za"] = za = _mm(h0, lw["wa"], name=n("in_a"))
    s["zb"] = zb = _mm(h0, lw["wb"], name=n("in_b"))
    s["zc"] = zc = _mm(h0, lw["wc"], out_dtype=BF16, name=n("in_c"))
    s["zg"] = zg = _mm(h0, lw["wg"], name=n("in_g"))
    s["cqn"] = cqn = _rms(za, w["a_gq"][l], col=0, width=A_Q_RANK, name=n("rms_cq"))
    s["ckvn"] = ckvn = _rms(za, w["a_gkv"][l], col=512, width=A_KV_RANK, name=n("rms_ckv"))
    s["q"], s["qt"] = q, _ = _qrope(_mm(cqn, lw["wq2"], name=n("uq")), cq_t, sq_t, name=n("qrope"))
    s["kv"] = kv = _mm(ckvn, lw["wkv"], out_dtype=BF16, name=n("ukv"))
    s["k"], vxt = k, _ = _kprep(kv, za, ck, name=n("kprep"))
    s["ya"], s["lse"] = ya, _ = _flash_fwd(q, k, vxt, name=n("mla"))
    lb_f, lb_b = lower[0, l].reshape(1, -1), lower[1, l].reshape(1, -1)
    s["of"], s["stf"] = of, _ = _hgrn_fwd(zb, lb_f, reverse=False, name=n("hgrn_f"))
    s["ob"], s["stb"] = ob, _ = _hgrn_fwd(zb, lb_b, reverse=True, name=n("hgrn_b"))
    s["yb"] = yb = _hgrn_out(of, ob, zb, w["b_gout"][l], name=n("hgrn_out"))
    s["yc"] = yc = _swa_fwd(zc, bias, w["c_sink"][l], name=n("swa"))
    s["merged"] = merged = _merge_fwd(ya, yb, yc, lw["w_br_a"], lw["w_br_b"], lw["w_br_c"], zg, name=n("merge"))
    s["x1"] = x1 = _mm(merged, lw["w_out"], add=x, name=n("out"))
    s["h1"] = h1 = _rms(x1, w["g_x"][l], name=n("rms_x"))
    s["qx"] = qx = _mm(h1, lw["x_wq"], out_dtype=BF16, name=n("xq"))
    s["memn"] = memn = _rms(mem, w["g_mem"][l], name=n("rms_mem"))
    s["kvm"] = kvm = _mm(memn, lw["x_wkv"], out_dtype=BF16, name=n("xkv"))
    s["ox"] = ox = _cross_fwd(qx, kvm, name=n("cross"))
    s["x2"] = x2 = _mm(ox, lw["x_wo"], add=x1, name=n("xo"))
    s["h2"] = h2 = _rms(x2, w["g_ffn"][l], name=n("rms_ffn"))
    s["a"], s["b"], s["act"] = a, b, act = _ffn_up(h2, lw["f_w1"], lw["f_w3"], name=n("ffn_up"))
    x3 = _mm(act, lw["f_w2"], add=x2, name=n("ffn_down"))
    return x3, s


def _layer_bwd(l, dx3, mem, w, lw, lower, bias, tabs, s):
    n = lambda t: f"l{l}_b_{t}"
    cq_t, sq_t, _, ck_t = tabs
    g = {}
    da, db = _ffn_dact(dx3, lw["f_w2"], s["a"], s["b"], name=n("ffn_dact"))
    g["f_w2"] = _mm(s["act"], dx3, mode="tn", name=n("dw2"))
    dh2 = _mm(db, lw["f_w3"], mode="nt", add=_mm(da, lw["f_w1"], mode="nt", name=n("dh2a")), name=n("dh2b"))
    g["f_w1"] = _mm(s["h2"], da, mode="tn", name=n("dw1"))
    g["f_w3"] = _mm(s["h2"], db, mode="tn", name=n("dw3"))
    dx2, g["g_ffn"] = _rms_bwd(s["x2"], w["g_ffn"][l], dh2, res=dx3, name=n("rms_ffn"))
    dox = _mm(dx2, lw["x_wo"], mode="nt", out_dtype=BF16, name=n("dox"))
    g["x_wo"] = _mm(s["ox"], dx2, mode="tn", name=n("dwo"))
    dqx, dkvm = _cross_bwd(s["qx"], s["kvm"], dox, name=n("cross"))
    g["x_wq"] = _mm(s["h1"], dqx, mode="tn", name=n("dwq"))
    dh1 = _mm(dqx, lw["x_wq"], mode="nt", name=n("dh1"))
    g["x_wkv"] = _mm(s["memn"], dkvm, mode="tn", name=n("dwkv"))
    dmemn = _mm(dkvm, lw["x_wkv"], mode="nt", name=n("dmemn"))
    _, g["g_mem"] = _rms_bwd(mem, w["g_mem"][l], dmemn, name=n("rms_mem"))
    dx1, g["g_x"] = _rms_bwd(s["x1"], w["g_x"][l], dh1, res=dx2, name=n("rms_x"))
    dmerged = _mm(dx1, lw["w_out"], mode="nt", name=n("dmerged"))
    g["w_out"] = _mm(s["merged"], dx1, mode="tn", name=n("dwout"))
    dua, dub, duc, dga, dgb, dgc = _merge_bwd(s["ya"], s["yb"], s["yc"], lw["w_br_a"], lw["w_br_b"], lw["w_br_c"],
                                              s["zg"], dmerged, name=n("merge"))
    dya = _mm(dua, lw["w_br_a"], mode="nt", out_dtype=BF16, name=n("dya"))
    dyb = _mm(dub, lw["w_br_b"], mode="nt", name=n("dyb"))
    dyc = _mm(duc, lw["w_br_c"], mode="nt", out_dtype=BF16, name=n("dyc"))
    g["w_br_a"] = _mm(s["ya"], dua, mode="tn", name=n("dwbra"))
    g["w_br_b"] = _mm(s["yb"], dub, mode="tn", name=n("dwbrb"))
    g["w_br_c"] = _mm(s["yc"], duc, mode="tn", name=n("dwbrc"))
    dzc, dbias, dsink = _swa_bwd(s["zc"], bias, w["c_sink"][l], dyc, name=n("swa"))
    g["c_sink"] = dsink[:, 0]
    g["bias"] = dbias
    lb_f, lb_b = lower[0, l].reshape(1, -1), lower[1, l].reshape(1, -1)
    do_, dgate, dgout = _hgrn_out_bwd(s["of"], s["ob"], s["zb"], w["b_gout"][l], dyb, name=n("hgrn_out"))
    g["b_gout"] = dgout[0]
    dq_f, dzf, dv_f, dlb_f = _hgrn_bwd(s["zb"], lb_f, do_, s["stf"], reverse=False, name=n("hgrn_f"))
    dq_b, dzr, dv_b, dlb_b = _hgrn_bwd(s["zb"], lb_b, do_, s["stb"], reverse=True, name=n("hgrn_b"))
    g["lower"] = jnp.concatenate([dlb_f, dlb_b], axis=0)
    dzb = _dzb_assemble(dq_f, dq_b, dzf, dzr, dv_f, dv_b, dgate, name=n("dzb"))
    delta = _attn_delta(s["ya"], dya, name=n("mla_delta"))
    dq_part, dkt, dvt = _flash_bwd(s["q"], s["qt"], s["k"], s["kv"], dya, dya.T, s["lse"].reshape(A_HEADS, -1, 1), delta, name=n("mla"))
    dq2 = _dq_sum(dq_part, cq_t, sq_t, name=n("mla_dq"))
    dcqn = _mm(dq2, lw["wq2"], mode="nt", name=n("dcqn"))
    dwq2 = _mm(s["cqn"], dq2, mode="tn", name=n("dwq2")).reshape(A_Q_RANK, 2, A_HEADS, A_PAD)
    dknt, dzkrt = _kprep_bwd(dkt, ck_t, name=n("kprep"))
    wkv = lw["wkv"]
    dckvn = _mm(dvt, wkv[:, 512:].T, mode="tn", add=_mm(dknt, wkv[:, :512].T, mode="tn", name=n("dckvn_k")), name=n("dckvn_v"))
    dwkn = _mm(dknt, s["ckvn"], name=n("dwkn")).T
    dwv = _mm(dvt, s["ckvn"], name=n("dwv")).T
    dzcq, dgq = _rms_bwd(s["za"], w["a_gq"][l], dcqn, col=0, width=A_Q_RANK, out_dtype=BF16, name=n("rms_cq"))
    dzckv, dgkv = _rms_bwd(s["za"], w["a_gkv"][l], dckvn, col=512, width=A_KV_RANK, out_dtype=BF16, name=n("rms_ckv"))
    g["a_gq"], g["a_gkv"] = dgq[0], dgkv[0]
    sw = jnp.concatenate([jnp.zeros((A_Q_RANK, A_HEADS, A_NOPE), F32), _rope_unswap_cols(dwq2[:, 1, :, A_NOPE:A_QK])], axis=-1)
    g["a_wuq"] = (dwq2[:, 0, :, :A_QK] + sw).reshape(A_Q_RANK, -1)
    g["a_wukv"] = jnp.concatenate([dwkn.reshape(A_KV_RANK, A_HEADS, A_NOPE), dwv.reshape(A_KV_RANK, A_HEADS, A_V)], axis=-1).reshape(A_KV_RANK, -1)
    wa = lw["wa"]
    pieces = [(dzcq, wa[:, 0:384]), (dzckv, wa[:, 512:768]), (dzb, lw["wb"]), (dzc, lw["wc"]),
              (dga, lw["wg"][:, 0:1024]), (dgb, lw["wg"][:, 1024:2048]), (dgc, lw["wg"][:, 2048:3072])]
    dh0 = _mm(dzkrt, wa[:, 384:512].T, mode="tn", name=n("dh0_kr"))
    dwkr = _mm(dzkrt, s["h0"], name=n("dwin_kr")).T
    dwkr = dwkr[:, 0:A_ROPE] + _rope_unswap_cols(dwkr[:, A_ROPE:2 * A_ROPE])
    dws = []
    for i, (dz, wp) in enumerate(pieces):
        dh0 = _mm(dz, wp, mode="nt", add=dh0, name=n(f"dh0_{i}"))
        dws.append(_mm(s["h0"], dz, mode="tn", name=n(f"dwin_{i}")))
    g["w_in"] = jnp.concatenate([dws[0], dws[1], dwkr] + dws[2:], axis=1)
    dx, g["g_mix"] = _rms_bwd(s["x"], w["g_mix"][l], dh0, res=dx1, name=n("rms_mix"))
    return dx, g


def _local_step(x, mem, target, w):
    T = x.shape[0]
    tabs = _rope_tables(T)
    lower, lower_vjp = jax.vjp(_lower_bounds, w["b_lb"])
    buckets = _swa_buckets()
    onehot = (buckets.reshape(-1)[:, None] == jnp.arange(REL_BUCKETS)[None, :]).astype(F32)
    bias = jnp.dot(w["rel_bias"].astype(F32).T, onehot.T, precision=lax.Precision.HIGHEST).reshape(C_HEADS, C_BLOCK, C_SPAN)
    lws, saved = [], []
    for l in range(DEPTH):
        lws.append(_layer_weights(w, l))
        x, s = _layer_fwd(l, x, mem, w, lws[l], lower, bias, tabs)
        saved.append(s)
    loss, dx, dg_final = _loss_head(x, w["g_final"], target, name="loss_head")
    layer_grads = [None] * DEPTH
    for l in reversed(range(DEPTH)):
        dx, layer_grads[l] = _layer_bwd(l, dx, mem, w, lws[l], lower, bias, tabs, saved[l])
        saved[l] = None
    grads = {}
    for name in WEIGHT_NAMES:
        if name in layer_grads[0]:
            grads[name] = jnp.stack([layer_grads[l][name].reshape(w[name].shape[1:]) for l in range(DEPTH)])
    grads["g_final"] = dg_final[0]
    dlower = jnp.stack([layer_grads[l]["lower"] for l in range(DEPTH)], axis=1)
    grads["b_lb"] = lower_vjp(dlower)[0]
    dbias = layer_grads[0]["bias"] + layer_grads[1]["bias"]
    grads["rel_bias"] = jnp.dot(onehot.T, dbias.reshape(C_HEADS, -1).T, precision=lax.Precision.HIGHEST)
    return loss, dx, grads


N_CHIPS = 4
PACK_COLS = 1024
PACK_ALIGN = 32 * PACK_COLS
SHARDED = (("w_in", 2), ("a_wuq", 2), ("a_wukv", 2), ("b_lb", 2), ("w_br_a", 2), ("w_br_b", 2), ("w_br_c", 2), ("w_out", 1),
           ("x_wq", 1), ("x_wkv", 2), ("x_wo", 1), ("f_w1", 2), ("f_w3", 2), ("f_w2", 1))
REPLICATED = ("g_mix", "a_gq", "a_gkv", "b_gout", "c_sink", "rel_bias", "g_x", "g_mem", "g_ffn", "g_final")
MESH_IDS = pl.DeviceIdType.MESH
ANY_SPEC = pl.BlockSpec(memory_space=pl.ANY)


def _pack_pieces(arrs, cols, align):
    pieces = [a.reshape(-1, cols) for a in arrs]
    pad = (-sum(p.size for p in pieces)) % align
    return pieces + ([jnp.zeros((pad // cols, cols), pieces[0].dtype)] if pad else [])


def _pack(arrs, cols, align):
    return jnp.concatenate(_pack_pieces(arrs, cols, align), axis=0)


def _pack_small(arrs):
    flat = jnp.concatenate([a.reshape(-1) for a in arrs])
    return jnp.pad(flat, (0, (-flat.shape[0]) % (8 * LANES))).reshape(-1, LANES)


def _unpack(buf, shapes):
    cols = buf.shape[-1]
    buf = buf.reshape(-1, cols)
    by_rows = all(math.prod(shp) % cols == 0 for shp in shapes)
    flat = None if by_rows else buf.reshape(-1)
    out, start = [], 0
    for shp in shapes:
        size = math.prod(shp)
        piece = buf[start // cols:(start + size) // cols] if by_rows else flat[start:start + size]
        out.append(piece.reshape(shp))
        start += size
    return out


def _chip_peers():
    x, y, c = lax.axis_index("x"), lax.axis_index("y"), lax.axis_index("c")
    return x, y, c, [(1 - x, y), (x, 1 - y), (1 - x, 1 - y)]


def _chip_gather(src, chip, *, name):
    _, R, C = src.shape

    def body(src_ref, out_ref, send_sems, recv_sems, pass_send_sems, pass_recv_sems):
        x, y, c, chips = _chip_peers()
        me = 2 * x + y
        sibling = (x, y, 1 - c)

        def over_ici(j, slot):
            px, py = chips[j]
            return pltpu.make_async_remote_copy(src_ref=src_ref.at[c], dst_ref=out_ref.at[slot, c], send_sem=send_sems.at[j],
                                                recv_sem=recv_sems.at[j], device_id=(px, py, c), device_id_type=MESH_IDS)

        def pass_on(j, half):
            px, py = chips[j]
            piece = out_ref.at[2 * px + py, half]
            return pltpu.make_async_remote_copy(src_ref=piece, dst_ref=piece, send_sem=pass_send_sems.at[j],
                                                recv_sem=pass_recv_sems.at[j], device_id=sibling, device_id_type=MESH_IDS)

        sends = [over_ici(j, me) for j in range(3)]
        for cp in sends:
            cp.start()
        passed = []
        for j, (px, py) in enumerate(chips):
            over_ici(j, 2 * px + py).wait_recv()
            passed.append(pass_on(j, c))
            passed[j].start()
        for j in range(3):
            pass_on(j, 1 - c).wait_recv()
        for cp in sends + passed:
            cp.wait_send()

    gathered = _pallas(
        body, name=name, in_specs=[ANY_SPEC], out_specs=ANY_SPEC, out_shape=jax.ShapeDtypeStruct((N_CHIPS, 2, R, C), src.dtype),
        scratch_shapes=[pltpu.SemaphoreType.DMA((3,))] * 4,
        compiler_params=pltpu.CompilerParams(has_side_effects=True))(src)

    tr = _tile(R, 512, 16)

    def place(chip_ref, gathered_ref, own_ref, out_ref):
        out_ref[0, 0] = own_ref[0]

    grid_spec = pltpu.PrefetchScalarGridSpec(
        num_scalar_prefetch=1, grid=(2, R // tr),
        in_specs=[ANY_SPEC, pl.BlockSpec((1, tr, C), lambda h, i, chip_ref: (h, i, 0))],
        out_specs=pl.BlockSpec((1, 1, tr, C), lambda h, i, chip_ref: (chip_ref[0], h, i, 0)))
    return _pallas(place, name=name + "_own", grid_spec=grid_spec, out_shape=jax.ShapeDtypeStruct(gathered.shape, gathered.dtype),
                   input_output_aliases={1: 0}, compiler_params=_params("arbitrary", "arbitrary"))(chip, gathered, src)


def _chip_scatter(src, *, name):
    _, R, C = src.shape

    def body(src_ref, out_ref, send_sems, recv_sems):
        x, y, c, chips = _chip_peers()
        me = 2 * x + y

        def copy(j, seg):
            px, py = chips[j]
            return pltpu.make_async_remote_copy(src_ref=src_ref.at[seg], dst_ref=out_ref.at[j], send_sem=send_sems.at[j],
                                                recv_sem=recv_sems.at[j], device_id=(px, py, c), device_id_type=MESH_IDS)

        sends = [copy(j, 2 * px + py) for j, (px, py) in enumerate(chips)]
        for cp in sends:
            cp.start()
        for j in range(3):
            copy(j, me).wait_recv()
        for cp in sends:
            cp.wait_send()

    return _pallas(
        body, name=name, in_specs=[ANY_SPEC], out_specs=ANY_SPEC, out_shape=jax.ShapeDtypeStruct((3, R, C), src.dtype),
        scratch_shapes=[pltpu.SemaphoreType.DMA((3,)), pltpu.SemaphoreType.DMA((3,))],
        compiler_params=pltpu.CompilerParams(has_side_effects=True))(src)


PAIR_CHUNKS = 4


def _pair_swap(src, *, halves, name):
    R, C = src.shape[-2:]
    n = N_CHIPS if halves else 1
    rc = R // PAIR_CHUNKS
    assert rc * PAIR_CHUNKS == R and rc % 16 == 0, R

    def body(src_ref, out_ref, send_sems, recv_sems):
        x, y, c = lax.axis_index("x"), lax.axis_index("y"), lax.axis_index("c")
        copies = []
        for k in range(n):
            for r in range(PAIR_CHUNKS):
                rows = pl.ds(r * rc, rc)
                s = src_ref.at[k, 1 - c, rows] if halves else src_ref.at[rows]
                d = out_ref.at[k, rows] if halves else out_ref.at[rows]
                i = k * PAIR_CHUNKS + r
                copies.append(pltpu.make_async_remote_copy(src_ref=s, dst_ref=d, send_sem=send_sems.at[i], recv_sem=recv_sems.at[i],
                                                           device_id=(x, y, 1 - c), device_id_type=MESH_IDS))
        for cp in copies:
            cp.start()
        for cp in copies:
            cp.wait_recv()
        for cp in copies:
            cp.wait_send()

    shape = (N_CHIPS, R, C) if halves else (R, C)
    return _pallas(
        body, name=name, in_specs=[ANY_SPEC], out_specs=ANY_SPEC, out_shape=jax.ShapeDtypeStruct(shape, src.dtype),
        scratch_shapes=[pltpu.SemaphoreType.DMA((n * PAIR_CHUNKS,)), pltpu.SemaphoreType.DMA((n * PAIR_CHUNKS,))],
        compiler_params=pltpu.CompilerParams(has_side_effects=True))(src)


def _pair_add(g4, got, c, *, name):
    _, _, R, C = g4.shape
    tr = _tile(R, 512, 16)

    def body(c_ref, mine_ref, got_ref, o_ref, ob_ref):
        s = mine_ref[0, 0] + got_ref[0].astype(F32)
        o_ref[0] = s
        ob_ref[0] = s.astype(BF16)

    blk = pl.BlockSpec((1, tr, C), lambda k, i, c_ref: (k, i, 0))
    grid_spec = pltpu.PrefetchScalarGridSpec(
        num_scalar_prefetch=1, grid=(N_CHIPS, R // tr),
        in_specs=[pl.BlockSpec((1, 1, tr, C), lambda k, i, c_ref: (k, c_ref[0], i, 0)), blk], out_specs=[blk, blk])
    return _pallas(body, name=name, grid_spec=grid_spec,
                   out_shape=[jax.ShapeDtypeStruct((N_CHIPS, R, C), F32), jax.ShapeDtypeStruct((N_CHIPS, R, C), BF16)],
                   compiler_params=_params("parallel", "parallel"))(c, g4, got)


def _chip_sum(pair_sum, landed, me, *, name):
    _, R, C = pair_sum.shape
    tr = _tile(R, 512, 16)

    def body(me_ref, own_ref, landed_ref, o_ref):
        acc = own_ref[0]
        for j in range(3):
            acc = acc + landed_ref[j].astype(F32)
        o_ref[...] = acc

    grid_spec = pltpu.PrefetchScalarGridSpec(
        num_scalar_prefetch=1, grid=(R // tr,),
        in_specs=[pl.BlockSpec((1, tr, C), lambda i, me_ref: (me_ref[0], i, 0)), pl.BlockSpec((3, tr, C), lambda i, me_ref: (0, i, 0))],
        out_specs=pl.BlockSpec((tr, C), lambda i, me_ref: (i, 0)))
    return _pallas(body, name=name, grid_spec=grid_spec, out_shape=jax.ShapeDtypeStruct((R, C), F32),
                   compiler_params=_params("parallel"))(me, pair_sum, landed)


def _join_halves(mine, got, c, *, name):
    R, C = mine.shape
    tr = _tile(R, 512, 16)

    def body(c_ref, mine_ref, got_ref, o_ref):
        use_mine = pl.program_id(0) == c_ref[0]
        o_ref[0] = jnp.where(use_mine, mine_ref[...], got_ref[...])

    blk = pl.BlockSpec((tr, C), lambda h, i, c_ref: (i, 0))
    grid_spec = pltpu.PrefetchScalarGridSpec(num_scalar_prefetch=1, grid=(2, R // tr), in_specs=[blk, blk],
                                             out_specs=pl.BlockSpec((1, tr, C), lambda h, i, c_ref: (h, i, 0)))
    return _pallas(body, name=name, grid_spec=grid_spec, out_shape=jax.ShapeDtypeStruct((2, R, C), mine.dtype),
                   compiler_params=_params("parallel", "parallel"))(c, mine, got).reshape(2 * R, C)


def _gather8(s, *, name):
    R, C = s.shape

    def body(s_ref, out_ref, send_sems, recv_sems):
        x, y, c = lax.axis_index("x"), lax.axis_index("y"), lax.axis_index("c")
        me = 4 * x + 2 * y + c
        flips = [(dx, dy, dc) for dx in (0, 1) for dy in (0, 1) for dc in (0, 1)][1:]
        out_ref[me] = s_ref[...]

        def copy(j, slot):
            dx, dy, dc = flips[j]
            return pltpu.make_async_remote_copy(src_ref=s_ref, dst_ref=out_ref.at[slot], send_sem=send_sems.at[j],
                                                recv_sem=recv_sems.at[j], device_id=(x ^ dx, y ^ dy, c ^ dc), device_id_type=MESH_IDS)

        sends = [copy(j, me) for j in range(7)]
        for cp in sends:
            cp.start()
        for j, (dx, dy, dc) in enumerate(flips):
            copy(j, 4 * (x ^ dx) + 2 * (y ^ dy) + (c ^ dc)).wait_recv()
        for cp in sends:
            cp.wait_send()

    vmem = pl.BlockSpec(memory_space=pltpu.VMEM)
    return _pallas(
        body, name=name, in_specs=[vmem], out_specs=vmem, out_shape=jax.ShapeDtypeStruct((8, R, C), s.dtype),
        scratch_shapes=[pltpu.SemaphoreType.DMA((7,)), pltpu.SemaphoreType.DMA((7,))],
        compiler_params=pltpu.CompilerParams(has_side_effects=True))(s)


def _sum_slots(a, *, name):
    n, R, C = a.shape
    tr = _tile(R, 512, 8)

    def body(a_ref, o_ref):
        acc = a_ref[0]
        for k in range(1, n):
            acc = acc + a_ref[k]
        o_ref[...] = acc

    return _pallas(body, name=name, grid=(R // tr,), in_specs=[pl.BlockSpec((n, tr, C), lambda i: (0, i, 0))],
                   out_specs=pl.BlockSpec((tr, C), lambda i: (i, 0)), out_shape=jax.ShapeDtypeStruct((R, C), a.dtype),
                   compiler_params=_params("parallel"))(a)


def _adamw(w, g, m, v, *, name):
    R, C = w.shape
    tr = _tile(R, max(8, (1 << 18) // C // 8 * 8), 8)
    c1 = 1.0 / (1.0 - ADAM_B1 ** ADAM_STEP)
    c2 = 1.0 / (1.0 - ADAM_B2 ** ADAM_STEP)

    def body(w_ref, g_ref, m_ref, v_ref, d_ref, nm_ref, nv_ref):
        gv = g_ref[...]
        nm = ADAM_B1 * m_ref[...] + (1.0 - ADAM_B1) * gv
        nv = ADAM_B2 * v_ref[...] + (1.0 - ADAM_B2) * (gv * gv)
        d_ref[...] = -ADAM_LR * ((nm * c1) / (jnp.sqrt(nv * c2) + ADAM_EPS) + ADAM_WD * w_ref[...])
        nm_ref[...] = nm
        nv_ref[...] = nv

    blk = pl.BlockSpec((tr, C), lambda i: (i, 0))
    shape = jax.ShapeDtypeStruct((R, C), F32)
    return _pallas(body, name=name, grid=(R // tr,), in_specs=[blk] * 4, out_specs=[blk] * 3, out_shape=[shape] * 3,
                   compiler_params=_params("parallel"))(w, g, m, v)


def kernel(x, mem, w_in, g_mix, a_gq, a_gkv, a_wuq, a_wukv, b_lb, b_gout, c_sink, rel_bias, w_br_a, w_br_b, w_br_c, w_out, g_x, g_mem, x_wq, x_wkv, x_wo, g_ffn, f_w1, f_w3, f_w2, g_final, loss_target, m_w_in, m_g_mix, m_a_gq, m_a_gkv, m_a_wuq, m_a_wukv, m_b_lb, m_b_gout, m_c_sink, m_rel_bias, m_w_br_a, m_w_br_b, m_w_br_c, m_w_out, m_g_x, m_g_mem, m_x_wq, m_x_wkv, m_x_wo, m_g_ffn, m_f_w1, m_f_w3, m_f_w2, m_g_final, v_w_in, v_g_mix, v_a_gq, v_a_gkv, v_a_wuq, v_a_wukv, v_b_lb, v_b_gout, v_c_sink, v_rel_bias, v_w_br_a, v_w_br_b, v_w_br_c, v_w_out, v_g_x, v_g_mem, v_x_wq, v_x_wkv, v_x_wo, v_g_ffn, v_f_w1, v_f_w3, v_f_w2, v_g_final):
    ws = dict(zip(WEIGHT_NAMES, (w_in, g_mix, a_gq, a_gkv, a_wuq, a_wukv, b_lb, b_gout, c_sink, rel_bias, w_br_a, w_br_b, w_br_c,
                                 w_out, g_x, g_mem, x_wq, x_wkv, x_wo, g_ffn, f_w1, f_w3, f_w2, g_final)))
    ms = dict(zip(WEIGHT_NAMES, (m_w_in, m_g_mix, m_a_gq, m_a_gkv, m_a_wuq, m_a_wukv, m_b_lb, m_b_gout, m_c_sink, m_rel_bias,
                                 m_w_br_a, m_w_br_b, m_w_br_c, m_w_out, m_g_x, m_g_mem, m_x_wq, m_x_wkv, m_x_wo, m_g_ffn,
                                 m_f_w1, m_f_w3, m_f_w2, m_g_final)))
    vs = dict(zip(WEIGHT_NAMES, (v_w_in, v_g_mix, v_a_gq, v_a_gkv, v_a_wuq, v_a_wukv, v_b_lb, v_b_gout, v_c_sink, v_rel_bias,
                                 v_w_br_a, v_w_br_b, v_w_br_c, v_w_out, v_g_x, v_g_mem, v_x_wq, v_x_wkv, v_x_wo, v_g_ffn,
                                 v_f_w1, v_f_w3, v_f_w2, v_g_final)))
    sharded = [n for n, _ in SHARDED]
    axis_of = dict(SHARDED)

    def wire(n):
        return lax.bitcast_convert_type(ws[n], BF16) if n == "b_lb" else ws[n].astype(BF16)

    core = lax.axis_index("c").astype(jnp.int32).reshape(1)
    chip = (2 * lax.axis_index("x") + lax.axis_index("y")).astype(jnp.int32).reshape(1)
    wire_shapes = [wire(n).shape for n in sharded]
    packed = _pack([wire(n) for n in sharded], PACK_COLS, PACK_ALIGN)
    gathered = _chip_gather(packed.reshape(2, packed.shape[0] // 2, PACK_COLS), chip, name="gather_weights")
    per_chip = [_unpack(gathered[k], wire_shapes) for k in range(N_CHIPS)]
    full = dict(ws)
    for i, n in enumerate(sharded):
        parts = [per_chip[k][i] for k in range(N_CHIPS)]
        if n == "b_lb":
            parts = [lax.bitcast_convert_type(p, F32) for p in parts]
        full[n] = jnp.concatenate(parts, axis=axis_of[n])

    loss, grad_x, grads = _local_step(x[0], mem[0], loss_target[0], full)

    pieces = []
    for k in range(N_CHIPS):
        pieces += _pack_pieces([jnp.split(grads[n], N_CHIPS, axis=axis_of[n])[k] for n in sharded], PACK_COLS, PACK_ALIGN)
    g4 = jnp.concatenate(pieces, axis=0).reshape(N_CHIPS, 2, -1, PACK_COLS)
    got = _pair_swap(g4, halves=True, name="reduce_pair_swap")
    pair_sum, pair_sum_wire = _pair_add(g4, got, core, name="reduce_pair_add")
    landed = _chip_scatter(pair_sum_wire, name="reduce_chip_scatter")
    mine = _chip_sum(pair_sum, landed, chip, name="reduce_chip_sum")
    g_shard = _join_halves(mine, _pair_swap(mine, halves=False, name="reduce_pair_join"), core, name="reduce_join_halves")

    small = _pack_small([grads[n] for n in REPLICATED] + [loss[0, 0:1]])
    small_sum = _sum_slots(_gather8(small, name="gather_small"), name="sum_small")
    small_grads = _unpack(small_sum, [ws[n].shape for n in REPLICATED] + [(1,)])
    loss_total = small_grads.pop()[0]

    shard_shapes = [ws[n].shape for n in sharded]
    out = {}
    for n, gr in zip(sharded, _unpack(g_shard, shard_shapes)):
        flat2 = lambda a: a.reshape(-1, a.shape[-1])
        d, nm, nv = _adamw(flat2(ws[n]), flat2(gr), flat2(ms[n]), flat2(vs[n]), name="adamw_" + n)
        out[n] = (gr, d.reshape(gr.shape), nm.reshape(gr.shape), nv.reshape(gr.shape))
    pk_s = lambda d: _pack_small([d[n] for n in REPLICATED])
    rep_shapes = [ws[n].shape for n in REPLICATED]
    gs_flat = _pack_small(small_grads)
    ds_flat, ms_flat, vs_flat = _adamw(pk_s(ws), gs_flat, pk_s(ms), pk_s(vs), name="adamw_replicated")
    for n, gr, d, nm, nv in zip(REPLICATED, small_grads, _unpack(ds_flat, rep_shapes), _unpack(ms_flat, rep_shapes),
                                _unpack(vs_flat, rep_shapes)):
        out[n] = (gr, d, nm, nv)
    return (loss_total, grad_x[None], *[out[n][0] for n in WEIGHT_NAMES], *[out[n][1] for n in WEIGHT_NAMES],
            *[out[n][2] for n in WEIGHT_NAMES], *[out[n][3] for n in WEIGHT_NAMES])
```

```python
import functools
import math

import jax
import jax.numpy as jnp
from jax import lax
from jax.experimental import pallas as pl
from jax.experimental.pallas import tpu as pltpu

F32 = jnp.float32
BF16 = jnp.bfloat16

D_MODEL = 1024
DEPTH = 2
EPS = 1e-6
MASK_VALUE = -1e30
TINY = 1e-30
A_HEADS, A_NOPE, A_ROPE, A_V = 8, 64, 32, 64
A_QK = A_NOPE + A_ROPE
A_Q_RANK, A_KV_RANK = 384, 256
ROPE_THETA = 10000.0
B_HEADS, B_DK, B_DV, B_CHUNK = 8, 128, 64, 16
C_HEADS, C_KV_HEADS, C_DH, C_WINDOW, C_BLOCK = 8, 2, 64, 128, 128
REL_BUCKETS, REL_MAX_DIST = 32, 128
X_HEADS, X_DH = 4, 256
D_FF = 2816
ADAM_LR, ADAM_B1, ADAM_B2, ADAM_EPS, ADAM_WD, ADAM_STEP = 0.001, 0.9, 0.999, 1e-08, 0.01, 10

LANES = 128
VMEM_LIMIT = 56 * 1024 * 1024


def _pallas(body, **kw):
    return pl.pallas_call(body, **kw)


def _params(*sem):
    return pltpu.CompilerParams(dimension_semantics=sem, vmem_limit_bytes=VMEM_LIMIT)


def _tile(n, pref, unit=LANES):
    if n <= pref:
        return n
    t = (pref // unit) * unit
    while t > unit and n % t:
        t -= unit
    assert n % t == 0, (n, pref, unit)
    return t


def _dot(a, b, dims):
    return lax.dot_general(a, b, (dims, ((), ())), preferred_element_type=F32)


_NN = ((1,), (0,))
_NT = ((1,), (1,))
_TN = ((0,), (0,))


def _mm_tiles(M, N, K, mode):
    tm = _tile(M, 1408 if M % 1408 == 0 else (1024 if mode != "tn" and M >= 2048 else 512), LANES if mode == "tn" else 16)
    tn = _tile(N, 1408 if N % 1408 == 0 else 1024, 256 if N % 256 == 0 and N % 1408 else LANES)
    tk = K if K <= 2816 else _tile(K, 1024)
    return tm, tn, tk


def _mm(a, b, *, mode="nn", add=None, out_dtype=F32, tiles=None, name):
    if mode == "nn":
        (M, K), (K2, N) = a.shape, b.shape
    elif mode == "nt":
        (M, K), (N, K2) = a.shape, b.shape
    else:
        (K, M), (K2, N) = a.shape, b.shape
    assert K == K2, (a.shape, b.shape, mode)
    tm, tn, tk = tiles or _mm_tiles(M, N, K, mode)
    nk = K // tk
    dims = {"nn": _NN, "nt": _NT, "tn": _TN}[mode]
    a_spec = pl.BlockSpec((tk, tm), lambda i, j, k: (k, i)) if mode == "tn" else pl.BlockSpec((tm, tk), lambda i, j, k: (i, k))
    b_spec = pl.BlockSpec((tn, tk), lambda i, j, k: (j, k)) if mode == "nt" else pl.BlockSpec((tk, tn), lambda i, j, k: (k, j))
    o_spec = pl.BlockSpec((tm, tn), lambda i, j, k: (i, j))
    has_add = add is not None

    def body(*refs):
        if has_add:
            a_ref, b_ref, add_ref, o_ref, acc_ref = refs
        else:
            a_ref, b_ref, o_ref, acc_ref = refs
        k = pl.program_id(2)
        part = _dot(a_ref[...].astype(BF16), b_ref[...].astype(BF16), dims)

        @pl.when(k == 0)
        def _():
            acc_ref[...] = part

        @pl.when(k > 0)
        def _():
            acc_ref[...] += part

        @pl.when(k == nk - 1)
        def _():
            r = acc_ref[...]
            if has_add:
                r = r + add_ref[...].astype(F32)
            o_ref[...] = r.astype(out_dtype)

    ins = [a, b] + ([add] if has_add else [])
    in_specs = [a_spec, b_spec] + ([o_spec] if has_add else [])
    return _pallas(
        body, name=name, grid=(M // tm, N // tn, nk), in_specs=in_specs, out_specs=o_spec,
        out_shape=jax.ShapeDtypeStruct((M, N), out_dtype), scratch_shapes=[pltpu.VMEM((tm, tn), F32)],
        compiler_params=_params("parallel", "parallel", "arbitrary"),
    )(*ins)


def _rms(x, g, *, col=0, width=None, out_dtype=BF16, name):
    T = x.shape[0]
    width = x.shape[1] if width is None else width
    assert col % width == 0
    tm = _tile(T, 512, 8)
    cb = col // width

    def body(x_ref, g_ref, o_ref):
        xv = x_ref[...].astype(F32)
        r = lax.rsqrt(jnp.mean(xv * xv, axis=-1, keepdims=True) + EPS)
        o_ref[...] = (xv * r * g_ref[...]).astype(out_dtype)

    return _pallas(
        body, name=name, grid=(T // tm,),
        in_specs=[pl.BlockSpec((tm, width), lambda i: (i, cb)), pl.BlockSpec((1, width), lambda i: (0, 0))],
        out_specs=pl.BlockSpec((tm, width), lambda i: (i, 0)),
        out_shape=jax.ShapeDtypeStruct((T, width), out_dtype), compiler_params=_params("parallel"),
    )(x, g.reshape(1, width))


def _rms_bwd(x, g, dy, *, res=None, col=0, width=None, out_dtype=F32, name):
    T = x.shape[0]
    width = x.shape[1] if width is None else width
    assert col % width == 0
    tm = _tile(T, 512, 8)
    cb = col // width
    has_res = res is not None

    def body(*refs):
        if has_res:
            x_ref, g_ref, dy_ref, res_ref, dx_ref, dg_ref = refs
        else:
            x_ref, g_ref, dy_ref, dx_ref, dg_ref = refs
        xv = x_ref[...].astype(F32)
        r = lax.rsqrt(jnp.mean(xv * xv, axis=-1, keepdims=True) + EPS)
        xh = xv * r
        dyv = dy_ref[...].astype(F32)
        dxh = dyv * g_ref[...]
        dx = r * (dxh - xh * jnp.mean(dxh * xh, axis=-1, keepdims=True))
        if has_res:
            dx = dx + res_ref[...].astype(F32)
        dx_ref[...] = dx.astype(out_dtype)
        part = jnp.sum(dyv * xh, axis=0, keepdims=True)

        @pl.when(pl.program_id(0) == 0)
        def _():
            dg_ref[...] = part

        @pl.when(pl.program_id(0) > 0)
        def _():
            dg_ref[...] += part

    row = pl.BlockSpec((tm, width), lambda i: (i, 0))
    ins = [x, g.reshape(1, width), dy] + ([res] if has_res else [])
    in_specs = [pl.BlockSpec((tm, width), lambda i: (i, cb)), pl.BlockSpec((1, width), lambda i: (0, 0)), row] + ([row] if has_res else [])
    return _pallas(
        body, name=name, grid=(T // tm,), in_specs=in_specs,
        out_specs=[row, pl.BlockSpec((1, width), lambda i: (0, 0))],
        out_shape=[jax.ShapeDtypeStruct((T, width), out_dtype), jax.ShapeDtypeStruct((1, width), F32)],
        compiler_params=_params("arbitrary"),
    )(*ins)


def _rope_tables(T):
    half = A_ROPE // 2
    inv = ROPE_THETA ** (-jnp.arange(half, dtype=F32) / half)
    ang = jnp.arange(T, dtype=jnp.int32).astype(F32)[:, None] * inv[None, :]
    c32 = jnp.concatenate([jnp.cos(ang), jnp.cos(ang)], axis=-1)
    s32 = jnp.concatenate([jnp.sin(ang), jnp.sin(ang)], axis=-1)
    pad = A_PAD - A_QK
    cq = jnp.concatenate([jnp.ones((T, A_NOPE), F32), c32, jnp.ones((T, pad), F32)], axis=-1)
    sq = jnp.concatenate([jnp.zeros((T, A_NOPE), F32), s32, jnp.zeros((T, pad), F32)], axis=-1)
    ck = jnp.concatenate([c32, s32, jnp.zeros((T, LANES - 2 * A_ROPE), F32)], axis=-1)
    ck_t = jnp.concatenate([c32, s32], axis=-1).T
    return cq, sq, ck, ck_t


def _rope_swap_cols(w):
    half = A_ROPE // 2
    return jnp.concatenate([-w[..., half:], w[..., :half]], axis=-1)


def _rope_unswap_cols(g):
    half = A_ROPE // 2
    return jnp.concatenate([g[..., half:], -g[..., :half]], axis=-1)


A_PAD = LANES
A_W = A_HEADS * A_PAD
LOG2E = 1.4426950408889634
LN2 = 0.6931471805599453
Q_SCALE = A_QK ** -0.5 * LOG2E


def _qrope(q2, cq, sq, *, name):
    T = q2.shape[0]
    W = A_W
    tm = _tile(T, 512)

    def body(a_ref, b_ref, c_ref, s_ref, o_ref, ot_ref):
        c = jnp.concatenate([c_ref[...]] * A_HEADS, axis=1)
        s = jnp.concatenate([s_ref[...]] * A_HEADS, axis=1)
        q = (a_ref[...] * c + b_ref[...] * s) * Q_SCALE
        o_ref[...] = q.astype(BF16)
        ot_ref[...] = q.T.astype(BF16)

    blk = lambda j: pl.BlockSpec((tm, W), lambda i: (i, j))
    tab = pl.BlockSpec((tm, A_PAD), lambda i: (i, 0))
    return _pallas(body, name=name, grid=(T // tm,), in_specs=[blk(0), blk(1), tab, tab],
                   out_specs=[blk(0), pl.BlockSpec((W, tm), lambda i: (0, i))],
                   out_shape=[jax.ShapeDtypeStruct((T, W), BF16), jax.ShapeDtypeStruct((W, T), BF16)],
                   compiler_params=_params("parallel"))(q2, q2, cq, sq)


def _kprep(kv, za, ck, *, name):
    T = kv.shape[0]
    tm = _tile(T, 512)

    def body(kv_ref, kr_ref, ck_ref, k_ref, vxt_ref):
        t = kr_ref[...] * ck_ref[...]
        krope = (t[:, 0:A_ROPE] + t[:, A_ROPE:2 * A_ROPE]).astype(BF16)
        one = (lax.broadcasted_iota(jnp.int32, (A_PAD - A_V, tm), 0) == 0).astype(BF16)
        for h in range(A_HEADS):
            k_ref[:, A_PAD * h:A_PAD * h + A_NOPE] = kv_ref[:, A_NOPE * h:A_NOPE * (h + 1)]
            k_ref[:, A_PAD * h + A_NOPE:A_PAD * h + A_QK] = krope
            k_ref[:, A_PAD * h + A_QK:A_PAD * (h + 1)] = jnp.zeros((tm, A_PAD - A_QK), BF16)
            vxt_ref[A_PAD * h + A_V:A_PAD * (h + 1), :] = one
        vt = kv_ref[:, 512:1024].astype(F32).T.astype(BF16)
        for h in range(A_HEADS):
            vxt_ref[A_PAD * h:A_PAD * h + A_V, :] = vt[A_V * h:A_V * (h + 1), :]

    wide = pl.BlockSpec((tm, A_W), lambda i: (i, 0))
    return _pallas(
        body, name=name, grid=(T // tm,),
        in_specs=[wide, pl.BlockSpec((tm, LANES), lambda i: (i, 3)), pl.BlockSpec((tm, LANES), lambda i: (i, 0))],
        out_specs=[wide, pl.BlockSpec((A_W, tm), lambda i: (0, i))],
        out_shape=[jax.ShapeDtypeStruct((T, A_W), BF16), jax.ShapeDtypeStruct((A_W, T), BF16)],
        compiler_params=_params("parallel"))(kv, za, ck)


def _kprep_bwd(dkt, ck_t, *, name):
    T = dkt.shape[1]
    tc = _tile(T, 512)

    def body(dk_ref, ck_ref, dn_ref, dr_ref):
        acc = jnp.zeros((A_ROPE, tc), F32)
        for h in range(A_HEADS):
            dn_ref[A_NOPE * h:A_NOPE * (h + 1), :] = dk_ref[A_PAD * h:A_PAD * h + A_NOPE, :].astype(BF16)
            acc = acc + dk_ref[A_PAD * h + A_NOPE:A_PAD * h + A_QK, :]
        dr_ref[0:A_ROPE, :] = (acc * ck_ref[0:A_ROPE, :]).astype(BF16)
        dr_ref[A_ROPE:2 * A_ROPE, :] = (acc * ck_ref[A_ROPE:2 * A_ROPE, :]).astype(BF16)
        dr_ref[2 * A_ROPE:LANES, :] = jnp.zeros((LANES - 2 * A_ROPE, tc), BF16)

    col = lambda r: pl.BlockSpec((r, tc), lambda i: (0, i))
    return _pallas(
        body, name=name, grid=(T // tc,), in_specs=[col(A_W), col(2 * A_ROPE)], out_specs=[col(512), col(LANES)],
        out_shape=[jax.ShapeDtypeStruct((512, T), BF16), jax.ShapeDtypeStruct((LANES, T), BF16)],
        compiler_params=_params("parallel"))(dkt, ck_t)


def _flash_fwd(qs, k, vxt, *, name):
    T = qs.shape[0]
    tq, tk = _tile(T, 512), _tile(T, 2048)
    nk = T // tk
    H, P, DV = A_HEADS, A_PAD, A_V

    def body(q_ref, k_ref, v_ref, o_ref, lse_ref, m_sc, acc_sc):
        j = pl.program_id(1)

        @pl.when(j == 0)
        def _():
            m_sc[...] = jnp.full(m_sc.shape, -jnp.inf, F32)
            acc_sc[...] = jnp.zeros(acc_sc.shape, F32)

        def scores(h):
            return _dot(k_ref[:, P * h:P * (h + 1)], q_ref[:, P * h:P * (h + 1)], _NT)

        st_next = scores(0)
        for h in range(H):
            st = st_next
            if h + 1 < H:
                st_next = scores(h + 1)
            m_prev = m_sc[h]
            m_new = jnp.maximum(m_prev, jnp.max(st, axis=0, keepdims=True))
            pt = jnp.exp2(st - m_new).astype(BF16)
            acc_sc[h] = jnp.exp2(m_prev - m_new) * acc_sc[h] + _dot(v_ref[P * h:P * (h + 1), :], pt, _NN)
            m_sc[h] = m_new

        @pl.when(j == nk - 1)
        def _():
            for h in range(H):
                acc = acc_sc[h]
                l = acc[DV:DV + 1, :]
                o_ref[:, DV * h:DV * (h + 1)] = (acc[0:DV, :] / l).T
                lse_ref[h] = m_sc[h] + jnp.log2(l)

    return _pallas(
        body, name=name, grid=(T // tq, nk),
        in_specs=[pl.BlockSpec((tq, A_W), lambda i, j: (i, 0)), pl.BlockSpec((tk, A_W), lambda i, j: (j, 0)),
                  pl.BlockSpec((A_W, tk), lambda i, j: (0, j))],
        out_specs=[pl.BlockSpec((tq, H * DV), lambda i, j: (i, 0)), pl.BlockSpec((H, 1, tq), lambda i, j: (0, 0, i))],
        out_shape=[jax.ShapeDtypeStruct((T, H * DV), F32), jax.ShapeDtypeStruct((H, 1, T), F32)],
        scratch_shapes=[pltpu.VMEM((H, 1, tq), F32), pltpu.VMEM((H, P, tq), F32)],
        compiler_params=_params("parallel", "arbitrary"))(qs, k, vxt)


def _attn_delta(o, do, *, name):
    T = o.shape[0]
    tm = _tile(T, 512, 8)

    def body(o_ref, do_ref, d_ref):
        prod = o_ref[...] * do_ref[...].astype(F32)
        for h in range(A_HEADS):
            d_ref[h] = jnp.sum(prod[:, A_V * h:A_V * (h + 1)], axis=-1, keepdims=True)

    row = pl.BlockSpec((tm, A_HEADS * A_V), lambda i: (i, 0))
    return _pallas(body, name=name, grid=(T // tm,), in_specs=[row, row],
                   out_specs=pl.BlockSpec((A_HEADS, tm, 1), lambda i: (0, i, 0)),
                   out_shape=jax.ShapeDtypeStruct((A_HEADS, T, 1), F32), compiler_params=_params("parallel"))(o, do)


def _flash_bwd(qs, qst, k, kv, do, dot_, lse2, delta, *, tiles=None, name):
    T = qs.shape[0]
    tq, tk = tiles or (_tile(T, 512), _tile(T, 1024))
    nq, nk = T // tq, T // tk
    H, P, DV = A_HEADS, A_PAD, A_V

    def body(q_ref, qt_ref, k_ref, v_ref, do_ref, dot_ref, lse_ref, delta_ref, dq_ref, dkt_ref, dvt_ref, dkt_sc, dvt_sc):
        i = pl.program_id(1)

        @pl.when(i == 0)
        def _():
            dkt_sc[...] = jnp.zeros(dkt_sc.shape, F32)
            dvt_sc[...] = jnp.zeros(dvt_sc.shape, F32)

        for h in range(H):
            s = _dot(q_ref[:, P * h:P * (h + 1)], k_ref[:, P * h:P * (h + 1)], _NT)
            dp = _dot(do_ref[:, DV * h:DV * (h + 1)], v_ref[:, DV * h:DV * (h + 1)], _NT)
            p = jnp.exp2(s - lse_ref[h])
            ds = (p * (dp - delta_ref[h])).astype(BF16)
            pb = p.astype(BF16)
            dq_ref[0, :, P * h:P * (h + 1)] = _dot(ds, k_ref[:, P * h:P * (h + 1)], _NN).astype(BF16)
            dkt_sc[h] += _dot(qt_ref[P * h:P * (h + 1), :], ds, _NN)
            dvt_sc[h] += _dot(dot_ref[DV * h:DV * (h + 1), :], pb, _NN)

        @pl.when(i == nq - 1)
        def _():
            for h in range(H):
                dkt_ref[P * h:P * (h + 1), :] = dkt_sc[h] * LN2
                dvt_ref[DV * h:DV * (h + 1), :] = dvt_sc[h].astype(BF16)

    qrow = lambda w: pl.BlockSpec((tq, w), lambda j, i: (i, 0))
    qcol = lambda r: pl.BlockSpec((r, tq), lambda j, i: (0, i))
    stat = pl.BlockSpec((H, tq, 1), lambda j, i: (0, i, 0))
    return _pallas(
        body, name=name, grid=(nk, nq),
        in_specs=[qrow(A_W), qcol(A_W), pl.BlockSpec((tk, A_W), lambda j, i: (j, 0)), pl.BlockSpec((tk, H * DV), lambda j, i: (j, 1)),
                  qrow(H * DV), qcol(H * DV), stat, stat],
        out_specs=[pl.BlockSpec((1, tq, A_W), lambda j, i: (j, i, 0)), pl.BlockSpec((A_W, tk), lambda j, i: (0, j)),
                   pl.BlockSpec((H * DV, tk), lambda j, i: (0, j))],
        out_shape=[jax.ShapeDtypeStruct((nk, T, A_W), BF16), jax.ShapeDtypeStruct((A_W, T), F32),
                   jax.ShapeDtypeStruct((H * DV, T), BF16)],
        scratch_shapes=[pltpu.VMEM((H, P, tk), F32), pltpu.VMEM((H, DV, tk), F32)],
        compiler_params=_params("parallel", "arbitrary"))(qs, qst, k, kv, do, dot_, lse2, delta)


def _dq_sum(dq_part, cq, sq, *, name):
    n, T, W = dq_part.shape
    tm = _tile(T, 256, 16)

    def body(p_ref, c_ref, s_ref, o_ref):
        acc = p_ref[0].astype(F32)
        for j in range(1, n):
            acc = acc + p_ref[j].astype(F32)
        acc = acc * (A_QK ** -0.5)
        o_ref[:, 0:W] = (acc * jnp.concatenate([c_ref[...]] * A_HEADS, axis=1)).astype(BF16)
        o_ref[:, W:2 * W] = (acc * jnp.concatenate([s_ref[...]] * A_HEADS, axis=1)).astype(BF16)

    row = pl.BlockSpec((tm, A_PAD), lambda i: (i, 0))
    return _pallas(body, name=name, grid=(T // tm,), in_specs=[pl.BlockSpec((n, tm, W), lambda i: (0, i, 0)), row, row],
                   out_specs=pl.BlockSpec((tm, 2 * W), lambda i: (i, 0)), out_shape=jax.ShapeDtypeStruct((T, 2 * W), BF16),
                   compiler_params=_params("parallel"))(dq_part, cq, sq)


HB = 8 * B_CHUNK


def _chunk_masks(reverse):
    r = lax.broadcasted_iota(jnp.int32, (HB, HB), 0)
    c = lax.broadcasted_iota(jnp.int32, (HB, HB), 1)
    same = (r // B_CHUNK) == (c // B_CHUNK)
    incl = same & ((c >= r) if reverse else (c <= r))
    return same, incl


def _mask_mm(mask, x):
    hi = x.astype(BF16)
    lo = (x - hi.astype(F32)).astype(BF16)
    return _dot(mask, hi, _NN) + _dot(mask, lo, _NN)


def _hgrn_gates(q, z, lb, reverse):
    same, incl = _chunk_masks(reverse)
    sg = jax.nn.sigmoid(z)
    f = lb + (1.0 - lb) * sg
    lf = jnp.log(jnp.maximum(f, TINY))
    kk = (1.0 - lb) * jax.nn.sigmoid(-z)
    b = _mask_mm(incl.astype(BF16), lf)
    edge = 0 if reverse else B_CHUNK - 1
    btot = jnp.concatenate([jnp.broadcast_to(b[B_CHUNK * c + edge:B_CHUNK * c + edge + 1, :], (B_CHUNK, b.shape[1]))
                            for c in range(HB // B_CHUNK)], axis=0)
    eb, enb, er, dec = jnp.exp(b), jnp.exp(-b), jnp.exp(btot - b), jnp.exp(btot)
    return dict(same=same, incl=incl, sg=sg, f=f, kk=kk, eb=eb, enb=enb, er=er, dec=dec,
                qd=q * eb, ki=kk * enb, ke=kk * er)


def _hgrn_specs(T, reverse, gate_reverse):
    nb = T // HB
    blk = (lambda i: nb - 1 - i) if reverse else (lambda i: i)
    wide = B_HEADS * B_DK
    return nb, blk, [
        pl.BlockSpec((HB, wide), lambda i: (blk(i), 0)),
        pl.BlockSpec((HB, wide), lambda i: (blk(i), 2 if gate_reverse else 1)),
        pl.BlockSpec((HB, B_HEADS * B_DV), lambda i: (blk(i), 6)),
        pl.BlockSpec((1, wide), lambda i: (0, 0)),
    ]


def _hk(h):
    return slice(B_DK * h, B_DK * (h + 1))


def _hv(h):
    return slice(B_DV * h, B_DV * (h + 1))


def _crows(c):
    return slice(B_CHUNK * c, B_CHUNK * (c + 1))


def _chunk_selectors():
    r = lax.broadcasted_iota(jnp.int32, (HB, 1), 0) // B_CHUNK
    l = lax.broadcasted_iota(jnp.int32, (1, HB), 1) // B_CHUNK
    return [r == c for c in range(8)], [l == c for c in range(8)]


def _hgrn_fwd(zb, lb, *, reverse, name):
    T = zb.shape[0]
    nb, blk, in_specs = _hgrn_specs(T, reverse, reverse)
    order = range(7, -1, -1) if reverse else range(8)
    heads = range(B_HEADS)

    def body(q_ref, z_ref, v_ref, lb_ref, o_ref, st_ref, s_sc):
        @pl.when(pl.program_id(0) == 0)
        def _():
            s_sc[...] = jnp.zeros(s_sc.shape, F32)

        g = _hgrn_gates(q_ref[...], z_ref[...], lb_ref[...], reverse)
        v = v_ref[...].astype(BF16)
        qd, ki, ke = g["qd"].astype(BF16), g["ki"].astype(BF16), g["ke"].astype(BF16)
        dec = g["dec"]
        in_chunk_rows, in_chunk_lanes = _chunk_selectors()
        o_intra, upd = [], []
        for h in heads:
            a = jnp.where(g["incl"], _dot(qd[:, _hk(h)], ki[:, _hk(h)], _NT), 0.0)
            o_intra.append(_dot(a.astype(BF16), v[:, _hv(h)], _NN))
            vt = v[:, _hv(h)].T
            lhs = jnp.concatenate([jnp.where(in_chunk_lanes[c], vt, 0) for c in range(8)], axis=0)
            upd.append(_dot(lhs, ke[:, _hk(h)], _NN))
        st = [s_sc[h] for h in heads]
        snap = [[None] * 8 for _ in heads]
        for c in order:
            for h in heads:
                snap[h][c] = st[h]
                st[h] = st[h] * dec[B_CHUNK * c:B_CHUNK * c + 1, _hk(h)] + upd[h][B_DV * c:B_DV * (c + 1), :]
        for h in heads:
            s_sc[h] = st[h]
            for c in range(8):
                st_ref[h, c] = snap[h][c]
            qd_big = jnp.concatenate([jnp.where(in_chunk_rows[c], qd[:, _hk(h)], 0) for c in range(8)], axis=1)
            states = jnp.concatenate([snap[h][c].astype(BF16) for c in range(8)], axis=1)
            o_ref[h] = o_intra[h] + _dot(qd_big, states, _NT)

    return _pallas(
        body, name=name, grid=(nb,), in_specs=in_specs,
        out_specs=[pl.BlockSpec((B_HEADS, HB, B_DV), lambda i: (0, blk(i), 0)),
                   pl.BlockSpec((B_HEADS, 8, B_DV, B_DK), lambda i: (0, blk(i), 0, 0))],
        out_shape=[jax.ShapeDtypeStruct((B_HEADS, T, B_DV), F32),
                   jax.ShapeDtypeStruct((B_HEADS, T // B_CHUNK, B_DV, B_DK), F32)],
        scratch_shapes=[pltpu.VMEM((B_HEADS, B_DV, B_DK), F32)],
        compiler_params=_params("arbitrary"))(zb, zb, zb, lb)


def _hgrn_bwd(zb, lb, do, states, *, reverse, name):
    T = zb.shape[0]
    nb, blk, in_specs = _hgrn_specs(T, not reverse, reverse)
    order = range(8) if reverse else range(7, -1, -1)
    heads = range(B_HEADS)

    def body(q_ref, z_ref, v_ref, lb_ref, do_ref, st_ref, dq_ref, dz_ref, dv_ref, dlb_ref, ds_sc):
        @pl.when(pl.program_id(0) == 0)
        def _():
            ds_sc[...] = jnp.zeros(ds_sc.shape, F32)
            dlb_ref[...] = jnp.zeros(dlb_ref.shape, F32)

        lb = lb_ref[...]
        g = _hgrn_gates(q_ref[...], z_ref[...], lb, reverse)
        v = v_ref[...].astype(BF16)
        qd, ki, ke = g["qd"].astype(BF16), g["ki"].astype(BF16), g["ke"].astype(BF16)
        dec = g["dec"]
        dout = [do_ref[h].astype(BF16) for h in heads]
        in_chunk_rows, in_chunk_lanes = _chunk_selectors()
        _, incl_t = _chunk_masks(not reverse)
        rows_of = lambda x: jnp.concatenate([jnp.where(in_chunk_rows[c], x, 0) for c in range(8)], axis=1)
        dv_i, dqd_h, dki, upd = [], [], [], []
        for h in heads:
            qd_h, ki_h, v_h = qd[:, _hk(h)], ki[:, _hk(h)], v[:, _hv(h)]
            da = jnp.where(g["incl"], _dot(dout[h], v_h, _NT), 0.0).astype(BF16)
            at = jnp.where(incl_t, _dot(ki_h, qd_h, _NT), 0.0).astype(BF16)
            dat = jnp.where(incl_t, _dot(v_h, dout[h], _NT), 0.0).astype(BF16)
            dv_i.append(_dot(at, dout[h], _NN))
            dki.append(_dot(dat, qd_h, _NN))
            dot_t = dout[h].T
            lhs = jnp.concatenate([jnp.where(in_chunk_lanes[c], dot_t, 0) for c in range(8)], axis=0)
            upd.append(_dot(lhs, qd_h, _NN))
            saved = jnp.concatenate([st_ref[h, c].astype(BF16) for c in range(8)], axis=0)
            dqd_h.append(_dot(da, ki_h, _NN) + _dot(rows_of(dout[h]), saved, _NN))
        dst = [ds_sc[h] for h in heads]
        used = [[None] * 8 for _ in heads]
        for c in order:
            for h in heads:
                used[h][c] = dst[h]
                dst[h] = dst[h] * dec[B_CHUNK * c:B_CHUNK * c + 1, _hk(h)] + upd[h][B_DV * c:B_DV * (c + 1), :]
        dke_h, ddec_h = [], []
        for h in heads:
            ds_sc[h] = dst[h]
            used16 = [used[h][c].astype(BF16) for c in range(8)]
            dv_ref[h] = dv_i[h] + _dot(rows_of(ke[:, _hk(h)]), jnp.concatenate(used16, axis=1), _NT)
            dke_h.append(_dot(rows_of(v[:, _hv(h)]), jnp.concatenate(used16, axis=0), _NN))
            ddec_p = []
            for c in range(8):
                tot = jnp.sum(used[h][c] * st_ref[h, c], axis=0, keepdims=True) * dec[B_CHUNK * c:B_CHUNK * c + 1, _hk(h)]
                ddec_p.append(jnp.broadcast_to(tot, (B_CHUNK, B_DK)))
            ddec_h.append(jnp.concatenate(ddec_p, axis=0))
        dqd = jnp.concatenate(dqd_h, axis=1)
        dki = jnp.concatenate(dki, axis=1)
        dke = jnp.concatenate(dke_h, axis=1)
        db = dqd * g["qd"] - dki * g["ki"] - dke * g["ke"]
        masks = jnp.concatenate([incl_t.astype(BF16), g["same"].astype(BF16)], axis=1)
        dlf = _mask_mm(masks, jnp.concatenate([db, dke * g["ke"]], axis=0)) + jnp.concatenate(ddec_h, axis=1)
        dk = dki * g["enb"] + dke * g["er"]
        u = jnp.where(g["f"] > TINY, dlf / g["f"], 0.0) - dk
        sg = g["sg"]
        dq_ref[...] = dqd * g["eb"]
        dz_ref[...] = u * (1.0 - lb) * sg * (1.0 - sg)
        dlb_ref[...] += jnp.sum(u * (1.0 - sg), axis=0, keepdims=True)

    wide = pl.BlockSpec((HB, B_HEADS * B_DK), lambda i: (blk(i), 0))
    hm = pl.BlockSpec((B_HEADS, HB, B_DV), lambda i: (0, blk(i), 0))
    return _pallas(
        body, name=name, grid=(nb,),
        in_specs=in_specs + [hm, pl.BlockSpec((B_HEADS, 8, B_DV, B_DK), lambda i: (0, blk(i), 0, 0))],
        out_specs=[wide, wide, hm, pl.BlockSpec((1, B_HEADS * B_DK), lambda i: (0, 0))],
        out_shape=[jax.ShapeDtypeStruct((T, B_HEADS * B_DK), F32), jax.ShapeDtypeStruct((T, B_HEADS * B_DK), F32),
                   jax.ShapeDtypeStruct((B_HEADS, T, B_DV), F32), jax.ShapeDtypeStruct((1, B_HEADS * B_DK), F32)],
        scratch_shapes=[pltpu.VMEM((B_HEADS, B_DV, B_DK), F32)],
        compiler_params=_params("arbitrary"))(zb, zb, zb, lb, do, states)


def _hgrn_out(of, ob, zb, gout, *, name):
    T = zb.shape[0]
    tm = _tile(T, 512, 8)

    def body(of_ref, ob_ref, g_ref, gout_ref, y_ref):
        for h in range(B_HEADS):
            o = of_ref[h] + ob_ref[h]
            r = lax.rsqrt(jnp.mean(o * o, axis=-1, keepdims=True) + EPS)
            gh = g_ref[:, B_DV * h:B_DV * (h + 1)]
            y_ref[:, B_DV * h:B_DV * (h + 1)] = (o * r * gout_ref[...] * (gh * jax.nn.sigmoid(gh))).astype(BF16)

    hm = pl.BlockSpec((B_HEADS, tm, B_DV), lambda i: (0, i, 0))
    return _pallas(
        body, name=name, grid=(T // tm,),
        in_specs=[hm, hm, pl.BlockSpec((tm, 512), lambda i: (i, 7)), pl.BlockSpec((1, B_DV), lambda i: (0, 0))],
        out_specs=pl.BlockSpec((tm, 512), lambda i: (i, 0)),
        out_shape=jax.ShapeDtypeStruct((T, 512), BF16), compiler_params=_params("parallel"))(of, ob, zb, gout.reshape(1, B_DV))


def _hgrn_out_bwd(of, ob, zb, gout, dy, *, name):
    T = zb.shape[0]
    tm = _tile(T, 512, 8)

    def body(of_ref, ob_ref, g_ref, gout_ref, dy_ref, do_ref, dg_ref, dgo_ref):
        gout_v = gout_ref[...]
        acc = jnp.zeros((1, B_DV), F32)
        for h in range(B_HEADS):
            o = of_ref[h] + ob_ref[h]
            r = lax.rsqrt(jnp.mean(o * o, axis=-1, keepdims=True) + EPS)
            oh = o * r
            gh = g_ref[:, B_DV * h:B_DV * (h + 1)]
            sg = jax.nn.sigmoid(gh)
            dyh = dy_ref[:, B_DV * h:B_DV * (h + 1)].astype(F32)
            dn = dyh * (gh * sg)
            dg_ref[:, B_DV * h:B_DV * (h + 1)] = dyh * (oh * gout_v) * (sg * (1.0 + gh * (1.0 - sg)))
            dxh = dn * gout_v
            do_ref[h] = r * (dxh - oh * jnp.mean(dxh * oh, axis=-1, keepdims=True))
            acc = acc + jnp.sum(dn * oh, axis=0, keepdims=True)

        @pl.when(pl.program_id(0) == 0)
        def _():
            dgo_ref[...] = acc

        @pl.when(pl.program_id(0) > 0)
        def _():
            dgo_ref[...] += acc

    hm = pl.BlockSpec((B_HEADS, tm, B_DV), lambda i: (0, i, 0))
    row = pl.BlockSpec((tm, 512), lambda i: (i, 0))
    return _pallas(
        body, name=name, grid=(T // tm,),
        in_specs=[hm, hm, pl.BlockSpec((tm, 512), lambda i: (i, 7)), pl.BlockSpec((1, B_DV), lambda i: (0, 0)), row],
        out_specs=[hm, row, pl.BlockSpec((1, B_DV), lambda i: (0, 0))],
        out_shape=[jax.ShapeDtypeStruct((B_HEADS, T, B_DV), F32), jax.ShapeDtypeStruct((T, 512), F32),
                   jax.ShapeDtypeStruct((1, B_DV), F32)],
        compiler_params=_params("arbitrary"))(of, ob, zb, gout.reshape(1, B_DV), dy)


def _dzb_assemble(dq_f, dq_b, dzf, dzb_, dv_f, dv_b, dgate, *, name):
    T = dq_f.shape[0]
    tm = _tile(T, 256, 8)

    def body(qf, qb, zf, zr, vf, vr, dg, o_ref):
        o_ref[:, 0:1024] = (qf[...] + qb[...]).astype(BF16)
        o_ref[:, 1024:2048] = zf[...].astype(BF16)
        o_ref[:, 2048:3072] = zr[...].astype(BF16)
        for h in range(B_HEADS):
            o_ref[:, 3072 + B_DV * h:3072 + B_DV * (h + 1)] = (vf[h] + vr[h]).astype(BF16)
        o_ref[:, 3584:4096] = dg[...].astype(BF16)

    wide = pl.BlockSpec((tm, 1024), lambda i: (i, 0))
    hm = pl.BlockSpec((B_HEADS, tm, B_DV), lambda i: (0, i, 0))
    return _pallas(
        body, name=name, grid=(T // tm,),
        in_specs=[wide, wide, wide, wide, hm, hm, pl.BlockSpec((tm, 512), lambda i: (i, 0))],
        out_specs=pl.BlockSpec((tm, 4096), lambda i: (i, 0)),
        out_shape=jax.ShapeDtypeStruct((T, 4096), BF16), compiler_params=_params("parallel"))(dq_f, dq_b, dzf, dzb_, dv_f, dv_b, dgate)


C_SPAN = 3 * C_BLOCK
C_G = C_HEADS // C_KV_HEADS


def _t5_bucket(rel):
    nb = REL_BUCKETS // 2
    max_exact = nb // 2
    ret = (rel > 0).astype(jnp.int32) * nb
    n = jnp.abs(rel)
    large = max_exact + (jnp.log(jnp.maximum(n, 1).astype(F32) / max_exact)
                         / math.log(REL_MAX_DIST / max_exact) * (nb - max_exact)).astype(jnp.int32)
    large = jnp.minimum(large, nb - 1)
    return ret + jnp.where(n < max_exact, n, large)


def _swa_buckets():
    rel = jnp.arange(C_SPAN)[None, :] - C_BLOCK - jnp.arange(C_BLOCK)[:, None]
    return _t5_bucket(rel)


def _swa_specs(T):
    nb = T // C_BLOCK
    return nb, [
        pl.BlockSpec((C_BLOCK, 512), lambda n: (n, 0)),
        pl.BlockSpec((C_BLOCK, LANES), lambda n: (jnp.maximum(n - 1, 0), 4)),
        pl.BlockSpec((C_BLOCK, LANES), lambda n: (n, 4)),
        pl.BlockSpec((C_BLOCK, LANES), lambda n: (jnp.minimum(n + 1, nb - 1), 4)),
        pl.BlockSpec((C_BLOCK, LANES), lambda n: (jnp.maximum(n - 1, 0), 5)),
        pl.BlockSpec((C_BLOCK, LANES), lambda n: (n, 5)),
        pl.BlockSpec((C_BLOCK, LANES), lambda n: (jnp.minimum(n + 1, nb - 1), 5)),
        pl.BlockSpec((C_HEADS, C_BLOCK, C_SPAN), lambda n: (0, 0, 0)),
        pl.BlockSpec(memory_space=pltpu.SMEM),
    ]


def _swa_valid(n, T):
    qi = lax.broadcasted_iota(jnp.int32, (C_BLOCK, C_SPAN), 0)
    si = lax.broadcasted_iota(jnp.int32, (C_BLOCK, C_SPAN), 1)
    rel = si - C_BLOCK - qi
    kpos = (n - 1) * C_BLOCK + si
    return (jnp.abs(rel) <= C_WINDOW) & (kpos >= 0) & (kpos < T)


def _swa_softmax(raw, bias, valid, sink):
    s = raw * (C_DH ** -0.5) + bias
    s = jnp.where(valid, s, MASK_VALUE)
    m = jnp.maximum(jnp.max(s, axis=-1, keepdims=True), sink)
    e = jnp.exp(s - m)
    den = jnp.sum(e, axis=-1, keepdims=True) + jnp.exp(sink - m)
    return e / den, jnp.exp(sink - m) / den


def _swa_fwd(zc, bias, sink, *, name):
    T = zc.shape[0]
    nb, in_specs = _swa_specs(T)

    def body(q_ref, kp, kc, kn, vp, vc, vn, bias_ref, sink_ref, y_ref):
        n = pl.program_id(0)
        kcat = jnp.concatenate([kp[...], kc[...], kn[...]], axis=0)
        vcat = jnp.concatenate([vp[...], vc[...], vn[...]], axis=0)
        valid = _swa_valid(n, T)
        heads = range(C_HEADS)
        kvs = [slice(C_DH * (h // C_G), C_DH * (h // C_G + 1)) for h in heads]
        scores = [_dot(q_ref[:, C_DH * h:C_DH * (h + 1)], kcat[:, kvs[h]], _NT) for h in heads]
        probs = [_swa_softmax(scores[h], bias_ref[h], valid, sink_ref[h])[0].astype(BF16) for h in heads]
        for h in heads:
            y_ref[:, C_DH * h:C_DH * (h + 1)] = _dot(probs[h], vcat[:, kvs[h]], _NN).astype(BF16)

    return _pallas(
        body, name=name, grid=(nb,), in_specs=in_specs, out_specs=pl.BlockSpec((C_BLOCK, 512), lambda n: (n, 0)),
        out_shape=jax.ShapeDtypeStruct((T, 512), BF16), compiler_params=_params("parallel"))(zc, zc, zc, zc, zc, zc, zc, bias, sink)


def _swa_bwd(zc, bias, sink, dy, *, name):
    T = zc.shape[0]
    nb, in_specs = _swa_specs(T)
    scale = C_DH ** -0.5

    def body(q_ref, kp, kc, kn, vp, vc, vn, bias_ref, sink_ref, dy_ref, dq_ref, dkc_ref, dvc_ref, dbias_ref, dsink_ref):
        n = pl.program_id(0)

        @pl.when(n == 0)
        def _():
            dbias_ref[...] = jnp.zeros(dbias_ref.shape, F32)
            dsink_ref[...] = jnp.zeros(dsink_ref.shape, F32)

        kcat = jnp.concatenate([kp[...], kc[...], kn[...]], axis=0)
        vcat = jnp.concatenate([vp[...], vc[...], vn[...]], axis=0)
        valid = _swa_valid(n, T)
        heads = range(C_HEADS)
        kvs = [slice(C_DH * (h // C_G), C_DH * (h // C_G + 1)) for h in heads]
        qs = [q_ref[:, C_DH * h:C_DH * (h + 1)] for h in heads]
        dos = [dy_ref[:, C_DH * h:C_DH * (h + 1)].astype(BF16) for h in heads]
        scores = [_dot(qs[h], kcat[:, kvs[h]], _NT) for h in heads]
        dps = [_dot(dos[h], vcat[:, kvs[h]], _NT) for h in heads]
        pbs, dsbs = [], []
        for h in heads:
            p, p_sink = _swa_softmax(scores[h], bias_ref[h], valid, sink_ref[h])
            rowdot = jnp.sum(p * dps[h], axis=-1, keepdims=True)
            ds = p * (dps[h] - rowdot)
            dbias_ref[h] += ds
            tot = jnp.sum(jnp.sum(-p_sink * rowdot, axis=0, keepdims=True), axis=1, keepdims=True)
            dsink_ref[h:h + 1, :] += jnp.broadcast_to(tot, (1, LANES))
            pbs.append(p.astype(BF16))
            dsbs.append((ds * scale).astype(BF16))
        for h in heads:
            dq_ref[:, C_DH * h:C_DH * (h + 1)] = _dot(dsbs[h], kcat[:, kvs[h]], _NN).astype(BF16)
        dks = [_dot(dsbs[h], qs[h], _TN) for h in heads]
        dvs = [_dot(pbs[h], dos[h], _TN) for h in heads]
        for kv in range(C_KV_HEADS):
            group = range(kv * C_G, (kv + 1) * C_G)
            dkc_ref[0, :, C_DH * kv:C_DH * (kv + 1)] = sum(dks[h] for h in group)
            dvc_ref[0, :, C_DH * kv:C_DH * (kv + 1)] = sum(dvs[h] for h in group)

    part = pl.BlockSpec((1, C_SPAN, LANES), lambda n: (n, 0, 0))
    dq, dkc, dvc, dbias, dsink = _pallas(
        body, name=name, grid=(nb,), in_specs=in_specs + [pl.BlockSpec((C_BLOCK, 512), lambda n: (n, 0))],
        out_specs=[pl.BlockSpec((C_BLOCK, 512), lambda n: (n, 0)), part, part,
                   pl.BlockSpec((C_HEADS, C_BLOCK, C_SPAN), lambda n: (0, 0, 0)), pl.BlockSpec((C_HEADS, LANES), lambda n: (0, 0))],
        out_shape=[jax.ShapeDtypeStruct((T, 512), BF16), jax.ShapeDtypeStruct((nb, C_SPAN, LANES), F32),
                   jax.ShapeDtypeStruct((nb, C_SPAN, LANES), F32), jax.ShapeDtypeStruct((C_HEADS, C_BLOCK, C_SPAN), F32),
                   jax.ShapeDtypeStruct((C_HEADS, LANES), F32)],
        compiler_params=_params("arbitrary"))(zc, zc, zc, zc, zc, zc, zc, bias, sink, dy)

    def combine(dq_ref, kp, kc, kn, vp, vc, vn, o_ref):
        n = pl.program_id(0)
        lo = (n > 0).astype(F32)
        hi = (n < nb - 1).astype(F32)
        o_ref[:, 0:512] = dq_ref[...]
        o_ref[:, 512:640] = (kp[0] * lo + kc[0] + kn[0] * hi).astype(BF16)
        o_ref[:, 640:768] = (vp[0] * lo + vc[0] + vn[0] * hi).astype(BF16)

    prev = pl.BlockSpec((1, C_BLOCK, LANES), lambda n: (jnp.maximum(n - 1, 0), 2, 0))
    cur = pl.BlockSpec((1, C_BLOCK, LANES), lambda n: (n, 1, 0))
    nxt = pl.BlockSpec((1, C_BLOCK, LANES), lambda n: (jnp.minimum(n + 1, nb - 1), 0, 0))
    dzc = _pallas(
        combine, name=name + "_combine", grid=(nb,),
        in_specs=[pl.BlockSpec((C_BLOCK, 512), lambda n: (n, 0)), prev, cur, nxt, prev, cur, nxt],
        out_specs=pl.BlockSpec((C_BLOCK, 768), lambda n: (n, 0)),
        out_shape=jax.ShapeDtypeStruct((T, 768), BF16), compiler_params=_params("parallel"))(dq, dkc, dkc, dkc, dvc, dvc, dvc)
    return dzc, dbias, dsink


def _merge_tiles(T):
    return _tile(T, 512, 8), 512


def _merge_fwd(ya, yb, yc, wa, wb, wc, zg, *, name):
    T = ya.shape[0]
    tm, tn = _merge_tiles(T)
    nd = D_MODEL // tn

    def body(ya_ref, yb_ref, yc_ref, wa_ref, wb_ref, wc_ref, ga_ref, gb_ref, gc_ref, o_ref):
        acc = jax.nn.sigmoid(ga_ref[...].astype(F32)) * _dot(ya_ref[...].astype(BF16), wa_ref[...], _NN)
        acc += jax.nn.sigmoid(gb_ref[...].astype(F32)) * _dot(yb_ref[...].astype(BF16), wb_ref[...], _NN)
        acc += jax.nn.sigmoid(gc_ref[...].astype(F32)) * _dot(yc_ref[...].astype(BF16), wc_ref[...], _NN)
        o_ref[...] = acc.astype(BF16)

    y = pl.BlockSpec((tm, 512), lambda i, j: (i, 0))
    w = pl.BlockSpec((512, tn), lambda i, j: (0, j))
    gate = lambda b: pl.BlockSpec((tm, tn), lambda i, j: (i, b * nd + j))
    return _pallas(
        body, name=name, grid=(T // tm, nd), in_specs=[y, y, y, w, w, w, gate(0), gate(1), gate(2)],
        out_specs=pl.BlockSpec((tm, tn), lambda i, j: (i, j)),
        out_shape=jax.ShapeDtypeStruct((T, D_MODEL), BF16),
        compiler_params=_params("parallel", "parallel"))(ya, yb, yc, wa, wb, wc, zg, zg, zg)


def _merge_bwd(ya, yb, yc, wa, wb, wc, zg, dm, *, name):
    T = ya.shape[0]
    tm, tn = _merge_tiles(T)
    nd = D_MODEL // tn

    def body(ya_ref, yb_ref, yc_ref, wa_ref, wb_ref, wc_ref, ga_ref, gb_ref, gc_ref, dm_ref, *outs):
        dmv = dm_ref[...].astype(F32)
        for y_ref, w_ref, g_ref, du_ref, dg_ref in zip((ya_ref, yb_ref, yc_ref), (wa_ref, wb_ref, wc_ref),
                                                       (ga_ref, gb_ref, gc_ref), outs[:3], outs[3:]):
            u = _dot(y_ref[...].astype(BF16), w_ref[...], _NN)
            sg = jax.nn.sigmoid(g_ref[...].astype(F32))
            du_ref[...] = (dmv * sg).astype(BF16)
            dg_ref[...] = (dmv * u * sg * (1.0 - sg)).astype(BF16)

    y = pl.BlockSpec((tm, 512), lambda i, j: (i, 0))
    w = pl.BlockSpec((512, tn), lambda i, j: (0, j))
    gate = lambda b: pl.BlockSpec((tm, tn), lambda i, j: (i, b * nd + j))
    t = pl.BlockSpec((tm, tn), lambda i, j: (i, j))
    return _pallas(
        body, name=name, grid=(T // tm, nd), in_specs=[y, y, y, w, w, w, gate(0), gate(1), gate(2), t],
        out_specs=[t] * 6, out_shape=[jax.ShapeDtypeStruct((T, D_MODEL), BF16)] * 6,
        compiler_params=_params("parallel", "parallel"))(ya, yb, yc, wa, wb, wc, zg, zg, zg, dm)


def _cross_fwd(q, kvm, *, name):
    T = q.shape[0]
    M = kvm.shape[0]
    tm = _tile(T, 512, 8)
    scale = X_DH ** -0.5

    def body(q_ref, k_ref, v_ref, o_ref):
        for h in range(X_HEADS):
            cs = slice(X_DH * h, X_DH * (h + 1))
            s = _dot(q_ref[:, cs], k_ref[:, cs], _NT) * scale
            e = jnp.exp(s - jnp.max(s, axis=-1, keepdims=True))
            p = e / jnp.sum(e, axis=-1, keepdims=True)
            o_ref[:, cs] = _dot(p.astype(BF16), v_ref[:, cs], _NN).astype(BF16)

    row = pl.BlockSpec((tm, D_MODEL), lambda i: (i, 0))
    return _pallas(
        body, name=name, grid=(T // tm,),
        in_specs=[row, pl.BlockSpec((M, D_MODEL), lambda i: (0, 0)), pl.BlockSpec((M, D_MODEL), lambda i: (0, 1))],
        out_specs=row, out_shape=jax.ShapeDtypeStruct((T, D_MODEL), BF16), compiler_params=_params("parallel"))(q, kvm, kvm)


def _cross_bwd(q, kvm, do, *, name):
    T = q.shape[0]
    M = kvm.shape[0]
    tm = _tile(T, 512, 8)
    scale = X_DH ** -0.5

    def body(q_ref, k_ref, v_ref, do_ref, dq_ref, dkv_ref):
        @pl.when(pl.program_id(0) == 0)
        def _():
            dkv_ref[...] = jnp.zeros(dkv_ref.shape, F32)

        for h in range(X_HEADS):
            cs = slice(X_DH * h, X_DH * (h + 1))
            vs = slice(D_MODEL + X_DH * h, D_MODEL + X_DH * (h + 1))
            qh, kh, doh = q_ref[:, cs], k_ref[:, cs], do_ref[:, cs]
            s = _dot(qh, kh, _NT) * scale
            e = jnp.exp(s - jnp.max(s, axis=-1, keepdims=True))
            p = e / jnp.sum(e, axis=-1, keepdims=True)
            dp = _dot(doh, v_ref[:, cs], _NT)
            ds = (p * (dp - jnp.sum(p * dp, axis=-1, keepdims=True)) * scale).astype(BF16)
            dq_ref[:, cs] = _dot(ds, kh, _NN).astype(BF16)
            dkv_ref[:, cs] += _dot(ds, qh, _TN)
            dkv_ref[:, vs] += _dot(p.astype(BF16), doh, _TN)

    row = pl.BlockSpec((tm, D_MODEL), lambda i: (i, 0))
    return _pallas(
        body, name=name, grid=(T // tm,),
        in_specs=[row, pl.BlockSpec((M, D_MODEL), lambda i: (0, 0)), pl.BlockSpec((M, D_MODEL), lambda i: (0, 1)), row],
        out_specs=[row, pl.BlockSpec((M, 2 * D_MODEL), lambda i: (0, 0))],
        out_shape=[jax.ShapeDtypeStruct((T, D_MODEL), BF16), jax.ShapeDtypeStruct((M, 2 * D_MODEL), F32)],
        compiler_params=_params("arbitrary"))(q, kvm, kvm, do)


def _ffn_up(h, w1, w3, *, name):
    T = h.shape[0]
    tm = _tile(T, 512, 8)
    tn = D_FF // 2

    def body(h_ref, w1_ref, w3_ref, a_ref, b_ref, act_ref):
        hv = h_ref[...]
        a = _dot(hv, w1_ref[...], _NN)
        b = _dot(hv, w3_ref[...], _NN)
        a_ref[...] = a.astype(BF16)
        b_ref[...] = b.astype(BF16)
        act_ref[...] = (a * jax.nn.sigmoid(a) * b).astype(BF16)

    w = pl.BlockSpec((D_MODEL, tn), lambda i, j: (0, j))
    t = pl.BlockSpec((tm, tn), lambda i, j: (i, j))
    return _pallas(
        body, name=name, grid=(T // tm, D_FF // tn), in_specs=[pl.BlockSpec((tm, D_MODEL), lambda i, j: (i, 0)), w, w],
        out_specs=[t, t, t],
        out_shape=[jax.ShapeDtypeStruct((T, D_FF), BF16)] * 3,
        compiler_params=_params("parallel", "parallel"))(h, w1, w3)


def _ffn_dact(dx, w2, a, b, *, name):
    T = dx.shape[0]
    tm = _tile(T, 512, 8)
    tn = D_FF // 2

    def body(dx_ref, w2_ref, a_ref, b_ref, da_ref, db_ref):
        dact = _dot(dx_ref[...].astype(BF16), w2_ref[...], _NT)
        av = a_ref[...].astype(F32)
        sg = jax.nn.sigmoid(av)
        da_ref[...] = (dact * b_ref[...].astype(F32) * (sg * (1.0 + av * (1.0 - sg)))).astype(BF16)
        db_ref[...] = (dact * (av * sg)).astype(BF16)

    t = pl.BlockSpec((tm, tn), lambda i, j: (i, j))
    return _pallas(
        body, name=name, grid=(T // tm, D_FF // tn),
        in_specs=[pl.BlockSpec((tm, D_MODEL), lambda i, j: (i, 0)), pl.BlockSpec((tn, D_MODEL), lambda i, j: (j, 0)), t, t],
        out_specs=[t, t], out_shape=[jax.ShapeDtypeStruct((T, D_FF), BF16)] * 2,
        compiler_params=_params("parallel", "parallel"))(dx, w2, a, b)


def _loss_head(x, g, target, *, name):
    T, D = x.shape
    tm = _tile(T, 512, 8)

    def body(x_ref, g_ref, t_ref, loss_ref, dx_ref, dg_ref):
        xv = x_ref[...]
        r = lax.rsqrt(jnp.mean(xv * xv, axis=-1, keepdims=True) + EPS)
        xh = xv * r
        gv = g_ref[...]
        err = xh * gv - t_ref[...]
        dy = err * (1.0 / D)
        dxh = dy * gv
        dx_ref[...] = r * (dxh - xh * jnp.mean(dxh * xh, axis=-1, keepdims=True))
        lpart = 0.5 * jnp.sum(jnp.mean(err * err, axis=-1, keepdims=True), axis=0, keepdims=True)
        gpart = jnp.sum(dy * xh, axis=0, keepdims=True)

        @pl.when(pl.program_id(0) == 0)
        def _():
            loss_ref[...] = jnp.broadcast_to(lpart, (1, LANES))
            dg_ref[...] = gpart

        @pl.when(pl.program_id(0) > 0)
        def _():
            loss_ref[...] += jnp.broadcast_to(lpart, (1, LANES))
            dg_ref[...] += gpart

    row = pl.BlockSpec((tm, D), lambda i: (i, 0))
    vec = pl.BlockSpec((1, D), lambda i: (0, 0))
    return _pallas(
        body, name=name, grid=(T // tm,), in_specs=[row, vec, row],
        out_specs=[pl.BlockSpec((1, LANES), lambda i: (0, 0)), row, vec],
        out_shape=[jax.ShapeDtypeStruct((1, LANES), F32), jax.ShapeDtypeStruct((T, D), F32), jax.ShapeDtypeStruct((1, D), F32)],
        compiler_params=_params("arbitrary"))(x, g.reshape(1, D), target)


IN_CQ, IN_CKV, IN_KR, IN_B, IN_C, IN_G, IN_END = 0, 384, 640, 672, 4768, 5536, 8608
WEIGHT_NAMES = ("w_in", "g_mix", "a_gq", "a_gkv", "a_wuq", "a_wukv", "b_lb", "b_gout", "c_sink", "rel_bias",
                "w_br_a", "w_br_b", "w_br_c", "w_out", "g_x", "g_mem", "x_wq", "x_wkv", "x_wo", "g_ffn",
                "f_w1", "f_w3", "f_w2", "g_final")


def _lower_bounds(b_lb):
    sm = jax.nn.softmax(b_lb.astype(F32), axis=1)
    return jnp.cumsum(sm, axis=1) - sm[:, :1]


def _layer_weights(w, l):
    bf = lambda a: a.astype(BF16)
    w_in = bf(w["w_in"][l])
    kr = w_in[:, IN_KR:IN_B]
    wa = jnp.concatenate([w_in[:, IN_CQ:IN_CKV], kr, _rope_swap_cols(kr), jnp.zeros((D_MODEL, 64), BF16),
                          w_in[:, IN_CKV:IN_KR]], axis=1)
    wuq = bf(w["a_wuq"][l]).reshape(A_Q_RANK, A_HEADS, A_QK)
    zeros = lambda n: jnp.zeros((A_Q_RANK, A_HEADS, n), BF16)
    wuq_pad = jnp.concatenate([wuq, zeros(A_PAD - A_QK)], axis=-1)
    wuq_sw = jnp.concatenate([zeros(A_NOPE), _rope_swap_cols(wuq[..., A_NOPE:]), zeros(A_PAD - A_QK)], axis=-1)
    wq2 = jnp.concatenate([wuq_pad.reshape(A_Q_RANK, -1), wuq_sw.reshape(A_Q_RANK, -1)], axis=1)
    wukv = bf(w["a_wukv"][l]).reshape(A_KV_RANK, A_HEADS, A_NOPE + A_V)
    wkv = jnp.concatenate([wukv[..., :A_NOPE].reshape(A_KV_RANK, -1), wukv[..., A_NOPE:].reshape(A_KV_RANK, -1)], axis=1)
    return dict(wa=wa, wb=w_in[:, IN_B:IN_C], wc=w_in[:, IN_C:IN_G], wg=w_in[:, IN_G:IN_END], wq2=wq2, wkv=wkv,
                w_br_a=bf(w["w_br_a"][l]), w_br_b=bf(w["w_br_b"][l]), w_br_c=bf(w["w_br_c"][l]), w_out=bf(w["w_out"][l]),
                x_wq=bf(w["x_wq"][l]), x_wkv=bf(w["x_wkv"][l]), x_wo=bf(w["x_wo"][l]),
                f_w1=bf(w["f_w1"][l]), f_w3=bf(w["f_w3"][l]), f_w2=bf(w["f_w2"][l]))


def _layer_fwd(l, x, mem, w, lw, lower, bias, tabs):
    n = lambda s: f"l{l}_{s}"
    cq_t, sq_t, ck, _ = tabs
    s = dict(x=x)
    s["h0"] = h0 = _rms(x, w["g_mix"][l], name=n("rms_mix"))
    s["za"] = za = _mm(h0, lw["wa"], name=n("in_a"))
    s["zb"] = zb = _mm(h0, lw["wb"], name=n("in_b"))
    s["zc"] = zc = _mm(h0, lw["wc"], out_dtype=BF16, name=n("in_c"))
    s["zg"] = zg = _mm(h0, lw["wg"], out_dtype=BF16, name=n("in_g"))
    s["cqn"] = cqn = _rms(za, w["a_gq"][l], col=0, width=A_Q_RANK, name=n("rms_cq"))
    s["ckvn"] = ckvn = _rms(za, w["a_gkv"][l], col=512, width=A_KV_RANK, name=n("rms_ckv"))
    s["q"], s["qt"] = q, _ = _qrope(_mm(cqn, lw["wq2"], name=n("uq")), cq_t, sq_t, name=n("qrope"))
    s["kv"] = kv = _mm(ckvn, lw["wkv"], out_dtype=BF16, name=n("ukv"))
    s["k"], vxt = k, _ = _kprep(kv, za, ck, name=n("kprep"))
    s["ya"], s["lse"] = ya, _ = _flash_fwd(q, k, vxt, name=n("mla"))
    lb_f, lb_b = lower[0, l].reshape(1, -1), lower[1, l].reshape(1, -1)
    s["of"], s["stf"] = of, _ = _hgrn_fwd(zb, lb_f, reverse=False, name=n("hgrn_f"))
    s["ob"], s["stb"] = ob, _ = _hgrn_fwd(zb, lb_b, reverse=True, name=n("hgrn_b"))
    s["yb"] = yb = _hgrn_out(of, ob, zb, w["b_gout"][l], name=n("hgrn_out"))
    s["yc"] = yc = _swa_fwd(zc, bias, w["c_sink"][l], name=n("swa"))
    s["merged"] = merged = _merge_fwd(ya, yb, yc, lw["w_br_a"], lw["w_br_b"], lw["w_br_c"], zg, name=n("merge"))
    s["x1"] = x1 = _mm(merged, lw["w_out"], add=x, name=n("out"))
    s["h1"] = h1 = _rms(x1, w["g_x"][l], name=n("rms_x"))
    s["qx"] = qx = _mm(h1, lw["x_wq"], out_dtype=BF16, name=n("xq"))
    s["memn"] = memn = _rms(mem, w["g_mem"][l], name=n("rms_mem"))
    s["kvm"] = kvm = _mm(memn, lw["x_wkv"], out_dtype=BF16, name=n("xkv"))
    s["ox"] = ox = _cross_fwd(qx, kvm, name=n("cross"))
    s["x2"] = x2 = _mm(ox, lw["x_wo"], add=x1, name=n("xo"))
    s["h2"] = h2 = _rms(x2, w["g_ffn"][l], name=n("rms_ffn"))
    s["a"], s["b"], s["act"] = a, b, act = _ffn_up(h2, lw["f_w1"], lw["f_w3"], name=n("ffn_up"))
    x3 = _mm(act, lw["f_w2"], add=x2, name=n("ffn_down"))
    return x3, s


def _layer_bwd(l, dx3, mem, w, lw, lower, bias, tabs, s):
    n = lambda t: f"l{l}_b_{t}"
    cq_t, sq_t, _, ck_t = tabs
    g = {}
    da, db = _ffn_dact(dx3, lw["f_w2"], s["a"], s["b"], name=n("ffn_dact"))
    g["f_w2"] = _mm(s["act"], dx3, mode="tn", name=n("dw2"))
    dh2 = _mm(db, lw["f_w3"], mode="nt", add=_mm(da, lw["f_w1"], mode="nt", name=n("dh2a")), name=n("dh2b"))
    g["f_w1"] = _mm(s["h2"], da, mode="tn", name=n("dw1"))
    g["f_w3"] = _mm(s["h2"], db, mode="tn", name=n("dw3"))
    dx2, g["g_ffn"] = _rms_bwd(s["x2"], w["g_ffn"][l], dh2, res=dx3, name=n("rms_ffn"))
    dox = _mm(dx2, lw["x_wo"], mode="nt", out_dtype=BF16, name=n("dox"))
    g["x_wo"] = _mm(s["ox"], dx2, mode="tn", name=n("dwo"))
    dqx, dkvm = _cross_bwd(s["qx"], s["kvm"], dox, name=n("cross"))
    g["x_wq"] = _mm(s["h1"], dqx, mode="tn", name=n("dwq"))
    dh1 = _mm(dqx, lw["x_wq"], mode="nt", name=n("dh1"))
    g["x_wkv"] = _mm(s["memn"], dkvm, mode="tn", name=n("dwkv"))
    dmemn = _mm(dkvm, lw["x_wkv"], mode="nt", name=n("dmemn"))
    _, g["g_mem"] = _rms_bwd(mem, w["g_mem"][l], dmemn, name=n("rms_mem"))
    dx1, g["g_x"] = _rms_bwd(s["x1"], w["g_x"][l], dh1, res=dx2, name=n("rms_x"))
    dmerged = _mm(dx1, lw["w_out"], mode="nt", name=n("dmerged"))
    g["w_out"] = _mm(s["merged"], dx1, mode="tn", name=n("dwout"))
    dua, dub, duc, dga, dgb, dgc = _merge_bwd(s["ya"], s["yb"], s["yc"], lw["w_br_a"], lw["w_br_b"], lw["w_br_c"],
                                              s["zg"], dmerged, name=n("merge"))
    dya = _mm(dua, lw["w_br_a"], mode="nt", out_dtype=BF16, name=n("dya"))
    dyb = _mm(dub, lw["w_br_b"], mode="nt", name=n("dyb"))
    dyc = _mm(duc, lw["w_br_c"], mode="nt", out_dtype=BF16, name=n("dyc"))
    g["w_br_a"] = _mm(s["ya"], dua, mode="tn", name=n("dwbra"))
    g["w_br_b"] = _mm(s["yb"], dub, mode="tn", name=n("dwbrb"))
    g["w_br_c"] = _mm(s["yc"], duc, mode="tn", name=n("dwbrc"))
    dzc, dbias, dsink = _swa_bwd(s["zc"], bias, w["c_sink"][l], dyc, name=n("swa"))
    g["c_sink"] = dsink[:, 0]
    g["bias"] = dbias
    lb_f, lb_b = lower[0, l].reshape(1, -1), lower[1, l].reshape(1, -1)
    do_, dgate, dgout = _hgrn_out_bwd(s["of"], s["ob"], s["zb"], w["b_gout"][l], dyb, name=n("hgrn_out"))
    g["b_gout"] = dgout[0]
    dq_f, dzf, dv_f, dlb_f = _hgrn_bwd(s["zb"], lb_f, do_, s["stf"], reverse=False, name=n("hgrn_f"))
    dq_b, dzr, dv_b, dlb_b = _hgrn_bwd(s["zb"], lb_b, do_, s["stb"], reverse=True, name=n("hgrn_b"))
    g["lower"] = jnp.concatenate([dlb_f, dlb_b], axis=0)
    dzb = _dzb_assemble(dq_f, dq_b, dzf, dzr, dv_f, dv_b, dgate, name=n("dzb"))
    delta = _attn_delta(s["ya"], dya, name=n("mla_delta"))
    dq_part, dkt, dvt = _flash_bwd(s["q"], s["qt"], s["k"], s["kv"], dya, dya.T, s["lse"].reshape(A_HEADS, -1, 1), delta, name=n("mla"))
    dq2 = _dq_sum(dq_part, cq_t, sq_t, name=n("mla_dq"))
    dcqn = _mm(dq2, lw["wq2"], mode="nt", name=n("dcqn"))
    dwq2 = _mm(s["cqn"], dq2, mode="tn", name=n("dwq2")).reshape(A_Q_RANK, 2, A_HEADS, A_PAD)
    dknt, dzkrt = _kprep_bwd(dkt, ck_t, name=n("kprep"))
    wkv = lw["wkv"]
    dckvn = _mm(dvt, wkv[:, 512:].T, mode="tn", add=_mm(dknt, wkv[:, :512].T, mode="tn", name=n("dckvn_k")), name=n("dckvn_v"))
    dwkn = _mm(dknt, s["ckvn"], name=n("dwkn")).T
    dwv = _mm(dvt, s["ckvn"], name=n("dwv")).T
    dzcq, dgq = _rms_bwd(s["za"], w["a_gq"][l], dcqn, col=0, width=A_Q_RANK, out_dtype=BF16, name=n("rms_cq"))
    dzckv, dgkv = _rms_bwd(s["za"], w["a_gkv"][l], dckvn, col=512, width=A_KV_RANK, out_dtype=BF16, name=n("rms_ckv"))
    g["a_gq"], g["a_gkv"] = dgq[0], dgkv[0]
    sw = jnp.concatenate([jnp.zeros((A_Q_RANK, A_HEADS, A_NOPE), F32), _rope_unswap_cols(dwq2[:, 1, :, A_NOPE:A_QK])], axis=-1)
    g["a_wuq"] = (dwq2[:, 0, :, :A_QK] + sw).reshape(A_Q_RANK, -1)
    g["a_wukv"] = jnp.concatenate([dwkn.reshape(A_KV_RANK, A_HEADS, A_NOPE), dwv.reshape(A_KV_RANK, A_HEADS, A_V)], axis=-1).reshape(A_KV_RANK, -1)
    wa = lw["wa"]
    pieces = [(dzcq, wa[:, 0:384]), (dzckv, wa[:, 512:768]), (dzb, lw["wb"]), (dzc, lw["wc"]),
              (dga, lw["wg"][:, 0:1024]), (dgb, lw["wg"][:, 1024:2048]), (dgc, lw["wg"][:, 2048:3072])]
    dh0 = _mm(dzkrt, wa[:, 384:512].T, mode="tn", name=n("dh0_kr"))
    dwkr = _mm(dzkrt, s["h0"], name=n("dwin_kr")).T
    dwkr = dwkr[:, 0:A_ROPE] + _rope_unswap_cols(dwkr[:, A_ROPE:2 * A_ROPE])
    dws = []
    for i, (dz, wp) in enumerate(pieces):
        dh0 = _mm(dz, wp, mode="nt", add=dh0, name=n(f"dh0_{i}"))
        dws.append(_mm(s["h0"], dz, mode="tn", name=n(f"dwin_{i}")))
    g["w_in"] = jnp.concatenate([dws[0], dws[1], dwkr] + dws[2:], axis=1)
    dx, g["g_mix"] = _rms_bwd(s["x"], w["g_mix"][l], dh0, res=dx1, name=n("rms_mix"))
    return dx, g


def _local_step(x, mem, target, w):
    T = x.shape[0]
    tabs = _rope_tables(T)
    lower, lower_vjp = jax.vjp(_lower_bounds, w["b_lb"])
    buckets = _swa_buckets()
    onehot = (buckets.reshape(-1)[:, None] == jnp.arange(REL_BUCKETS)[None, :]).astype(F32)
    bias = jnp.dot(w["rel_bias"].astype(F32).T, onehot.T, precision=lax.Precision.HIGHEST).reshape(C_HEADS, C_BLOCK, C_SPAN)
    lws, saved = [], []
    for l in range(DEPTH):
        lws.append(_layer_weights(w, l))
        x, s = _layer_fwd(l, x, mem, w, lws[l], lower, bias, tabs)
        saved.append(s)
    loss, dx, dg_final = _loss_head(x, w["g_final"], target, name="loss_head")
    layer_grads = [None] * DEPTH
    for l in reversed(range(DEPTH)):
        dx, layer_grads[l] = _layer_bwd(l, dx, mem, w, lws[l], lower, bias, tabs, saved[l])
        saved[l] = None
    grads = {}
    for name in WEIGHT_NAMES:
        if name in layer_grads[0]:
            grads[name] = jnp.stack([layer_grads[l][name].reshape(w[name].shape[1:]) for l in range(DEPTH)])
    grads["g_final"] = dg_final[0]
    dlower = jnp.stack([layer_grads[l]["lower"] for l in range(DEPTH)], axis=1)
    grads["b_lb"] = lower_vjp(dlower)[0]
    dbias = layer_grads[0]["bias"] + layer_grads[1]["bias"]
    grads["rel_bias"] = jnp.dot(onehot.T, dbias.reshape(C_HEADS, -1).T, precision=lax.Precision.HIGHEST)
    return loss, dx, grads


N_CHIPS = 4
PACK_COLS = 1024
PACK_ALIGN = 32 * PACK_COLS
SHARDED = (("w_in", 2), ("a_wuq", 2), ("a_wukv", 2), ("b_lb", 2), ("w_br_a", 2), ("w_br_b", 2), ("w_br_c", 2), ("w_out", 1),
           ("x_wq", 1), ("x_wkv", 2), ("x_wo", 1), ("f_w1", 2), ("f_w3", 2), ("f_w2", 1))
REPLICATED = ("g_mix", "a_gq", "a_gkv", "b_gout", "c_sink", "rel_bias", "g_x", "g_mem", "g_ffn", "g_final")
MESH_IDS = pl.DeviceIdType.MESH
ANY_SPEC = pl.BlockSpec(memory_space=pl.ANY)


def _pack_pieces(arrs, cols, align):
    pieces = [a.reshape(-1, cols) for a in arrs]
    pad = (-sum(p.size for p in pieces)) % align
    return pieces + ([jnp.zeros((pad // cols, cols), pieces[0].dtype)] if pad else [])


def _pack(arrs, cols, align):
    return jnp.concatenate(_pack_pieces(arrs, cols, align), axis=0)


def _pack_small(arrs):
    flat = jnp.concatenate([a.reshape(-1) for a in arrs])
    return jnp.pad(flat, (0, (-flat.shape[0]) % (8 * LANES))).reshape(-1, LANES)


def _unpack(buf, shapes):
    cols = buf.shape[-1]
    buf = buf.reshape(-1, cols)
    by_rows = all(math.prod(shp) % cols == 0 for shp in shapes)
    flat = None if by_rows else buf.reshape(-1)
    out, start = [], 0
    for shp in shapes:
        size = math.prod(shp)
        piece = buf[start // cols:(start + size) // cols] if by_rows else flat[start:start + size]
        out.append(piece.reshape(shp))
        start += size
    return out


def _chip_peers():
    x, y, c = lax.axis_index("x"), lax.axis_index("y"), lax.axis_index("c")
    return x, y, c, [(1 - x, y), (x, 1 - y), (1 - x, 1 - y)]


def _chip_gather(src, chip, *, name):
    _, R, C = src.shape

    def body(src_ref, out_ref, send_sems, recv_sems, pass_send_sems, pass_recv_sems):
        x, y, c, chips = _chip_peers()
        me = 2 * x + y
        sibling = (x, y, 1 - c)

        def over_ici(j, slot):
            px, py = chips[j]
            return pltpu.make_async_remote_copy(src_ref=src_ref.at[c], dst_ref=out_ref.at[slot, c], send_sem=send_sems.at[j],
                                                recv_sem=recv_sems.at[j], device_id=(px, py, c), device_id_type=MESH_IDS)

        def pass_on(j, half):
            px, py = chips[j]
            piece = out_ref.at[2 * px + py, half]
            return pltpu.make_async_remote_copy(src_ref=piece, dst_ref=piece, send_sem=pass_send_sems.at[j],
                                                recv_sem=pass_recv_sems.at[j], device_id=sibling, device_id_type=MESH_IDS)

        sends = [over_ici(j, me) for j in range(3)]
        for cp in sends:
            cp.start()
        passed = []
        for j, (px, py) in enumerate(chips):
            over_ici(j, 2 * px + py).wait_recv()
            passed.append(pass_on(j, c))
            passed[j].start()
        for j in range(3):
            pass_on(j, 1 - c).wait_recv()
        for cp in sends + passed:
            cp.wait_send()

    gathered = _pallas(
        body, name=name, in_specs=[ANY_SPEC], out_specs=ANY_SPEC, out_shape=jax.ShapeDtypeStruct((N_CHIPS, 2, R, C), src.dtype),
        scratch_shapes=[pltpu.SemaphoreType.DMA((3,))] * 4,
        compiler_params=pltpu.CompilerParams(has_side_effects=True))(src)

    tr = _tile(R, 512, 16)

    def place(chip_ref, gathered_ref, own_ref, out_ref):
        out_ref[0, 0] = own_ref[0]

    grid_spec = pltpu.PrefetchScalarGridSpec(
        num_scalar_prefetch=1, grid=(2, R // tr),
        in_specs=[ANY_SPEC, pl.BlockSpec((1, tr, C), lambda h, i, chip_ref: (h, i, 0))],
        out_specs=pl.BlockSpec((1, 1, tr, C), lambda h, i, chip_ref: (chip_ref[0], h, i, 0)))
    return _pallas(place, name=name + "_own", grid_spec=grid_spec, out_shape=jax.ShapeDtypeStruct(gathered.shape, gathered.dtype),
                   input_output_aliases={1: 0}, compiler_params=_params("arbitrary", "arbitrary"))(chip, gathered, src)


def _chip_scatter(src, *, name):
    _, R, C = src.shape

    def body(src_ref, out_ref, send_sems, recv_sems):
        x, y, c, chips = _chip_peers()
        me = 2 * x + y

        def copy(j, seg):
            px, py = chips[j]
            return pltpu.make_async_remote_copy(src_ref=src_ref.at[seg], dst_ref=out_ref.at[j], send_sem=send_sems.at[j],
                                                recv_sem=recv_sems.at[j], device_id=(px, py, c), device_id_type=MESH_IDS)

        sends = [copy(j, 2 * px + py) for j, (px, py) in enumerate(chips)]
        for cp in sends:
            cp.start()
        for j in range(3):
            copy(j, me).wait_recv()
        for cp in sends:
            cp.wait_send()

    return _pallas(
        body, name=name, in_specs=[ANY_SPEC], out_specs=ANY_SPEC, out_shape=jax.ShapeDtypeStruct((3, R, C), src.dtype),
        scratch_shapes=[pltpu.SemaphoreType.DMA((3,)), pltpu.SemaphoreType.DMA((3,))],
        compiler_params=pltpu.CompilerParams(has_side_effects=True))(src)


PAIR_CHUNKS = 4


def _pair_swap(src, *, halves, name):
    R, C = src.shape[-2:]
    n = N_CHIPS if halves else 1
    rc = R // PAIR_CHUNKS
    assert rc * PAIR_CHUNKS == R and rc % 16 == 0, R

    def body(src_ref, out_ref, send_sems, recv_sems):
        x, y, c = lax.axis_index("x"), lax.axis_index("y"), lax.axis_index("c")
        copies = []
        for k in range(n):
            for r in range(PAIR_CHUNKS):
                rows = pl.ds(r * rc, rc)
                s = src_ref.at[k, 1 - c, rows] if halves else src_ref.at[rows]
                d = out_ref.at[k, rows] if halves else out_ref.at[rows]
                i = k * PAIR_CHUNKS + r
                copies.append(pltpu.make_async_remote_copy(src_ref=s, dst_ref=d, send_sem=send_sems.at[i], recv_sem=recv_sems.at[i],
                                                           device_id=(x, y, 1 - c), device_id_type=MESH_IDS))
        for cp in copies:
            cp.start()
        for cp in copies:
            cp.wait_recv()
        for cp in copies:
            cp.wait_send()

    shape = (N_CHIPS, R, C) if halves else (R, C)
    return _pallas(
        body, name=name, in_specs=[ANY_SPEC], out_specs=ANY_SPEC, out_shape=jax.ShapeDtypeStruct(shape, src.dtype),
        scratch_shapes=[pltpu.SemaphoreType.DMA((n * PAIR_CHUNKS,)), pltpu.SemaphoreType.DMA((n * PAIR_CHUNKS,))],
        compiler_params=pltpu.CompilerParams(has_side_effects=True))(src)


def _pair_add(g4, got, c, *, name):
    _, _, R, C = g4.shape
    tr = _tile(R, 512, 16)

    def body(c_ref, mine_ref, got_ref, o_ref, ob_ref):
        s = mine_ref[0, 0] + got_ref[0].astype(F32)
        o_ref[0] = s
        ob_ref[0] = s.astype(BF16)

    blk = pl.BlockSpec((1, tr, C), lambda k, i, c_ref: (k, i, 0))
    grid_spec = pltpu.PrefetchScalarGridSpec(
        num_scalar_prefetch=1, grid=(N_CHIPS, R // tr),
        in_specs=[pl.BlockSpec((1, 1, tr, C), lambda k, i, c_ref: (k, c_ref[0], i, 0)), blk], out_specs=[blk, blk])
    return _pallas(body, name=name, grid_spec=grid_spec,
                   out_shape=[jax.ShapeDtypeStruct((N_CHIPS, R, C), F32), jax.ShapeDtypeStruct((N_CHIPS, R, C), BF16)],
                   compiler_params=_params("parallel", "parallel"))(c, g4, got)


def _chip_sum(pair_sum, landed, me, *, name):
    _, R, C = pair_sum.shape
    tr = _tile(R, 512, 16)

    def body(me_ref, own_ref, landed_ref, o_ref):
        acc = own_ref[0]
        for j in range(3):
            acc = acc + landed_ref[j].astype(F32)
        o_ref[...] = acc

    grid_spec = pltpu.PrefetchScalarGridSpec(
        num_scalar_prefetch=1, grid=(R // tr,),
        in_specs=[pl.BlockSpec((1, tr, C), lambda i, me_ref: (me_ref[0], i, 0)), pl.BlockSpec((3, tr, C), lambda i, me_ref: (0, i, 0))],
        out_specs=pl.BlockSpec((tr, C), lambda i, me_ref: (i, 0)))
    return _pallas(body, name=name, grid_spec=grid_spec, out_shape=jax.ShapeDtypeStruct((R, C), F32),
                   compiler_params=_params("parallel"))(me, pair_sum, landed)


def _join_halves(mine, got, c, *, name):
    R, C = mine.shape
    tr = _tile(R, 512, 16)

    def body(c_ref, mine_ref, got_ref, o_ref):
        use_mine = pl.program_id(0) == c_ref[0]
        o_ref[0] = jnp.where(use_mine, mine_ref[...], got_ref[...])

    blk = pl.BlockSpec((tr, C), lambda h, i, c_ref: (i, 0))
    grid_spec = pltpu.PrefetchScalarGridSpec(num_scalar_prefetch=1, grid=(2, R // tr), in_specs=[blk, blk],
                                             out_specs=pl.BlockSpec((1, tr, C), lambda h, i, c_ref: (h, i, 0)))
    return _pallas(body, name=name, grid_spec=grid_spec, out_shape=jax.ShapeDtypeStruct((2, R, C), mine.dtype),
                   compiler_params=_params("parallel", "parallel"))(c, mine, got).reshape(2 * R, C)


def _gather8(s, *, name):
    R, C = s.shape

    def body(s_ref, out_ref, send_sems, recv_sems):
        x, y, c = lax.axis_index("x"), lax.axis_index("y"), lax.axis_index("c")
        me = 4 * x + 2 * y + c
        flips = [(dx, dy, dc) for dx in (0, 1) for dy in (0, 1) for dc in (0, 1)][1:]
        out_ref[me] = s_ref[...]

        def copy(j, slot):
            dx, dy, dc = flips[j]
            return pltpu.make_async_remote_copy(src_ref=s_ref, dst_ref=out_ref.at[slot], send_sem=send_sems.at[j],
                                                recv_sem=recv_sems.at[j], device_id=(x ^ dx, y ^ dy, c ^ dc), device_id_type=MESH_IDS)

        sends = [copy(j, me) for j in range(7)]
        for cp in sends:
            cp.start()
        for j, (dx, dy, dc) in enumerate(flips):
            copy(j, 4 * (x ^ dx) + 2 * (y ^ dy) + (c ^ dc)).wait_recv()
        for cp in sends:
            cp.wait_send()

    vmem = pl.BlockSpec(memory_space=pltpu.VMEM)
    return _pallas(
        body, name=name, in_specs=[vmem], out_specs=vmem, out_shape=jax.ShapeDtypeStruct((8, R, C), s.dtype),
        scratch_shapes=[pltpu.SemaphoreType.DMA((7,)), pltpu.SemaphoreType.DMA((7,))],
        compiler_params=pltpu.CompilerParams(has_side_effects=True))(s)


def _sum_slots(a, *, name):
    n, R, C = a.shape
    tr = _tile(R, 512, 8)

    def body(a_ref, o_ref):
        acc = a_ref[0]
        for k in range(1, n):
            acc = acc + a_ref[k]
        o_ref[...] = acc

    return _pallas(body, name=name, grid=(R // tr,), in_specs=[pl.BlockSpec((n, tr, C), lambda i: (0, i, 0))],
                   out_specs=pl.BlockSpec((tr, C), lambda i: (i, 0)), out_shape=jax.ShapeDtypeStruct((R, C), a.dtype),
                   compiler_params=_params("parallel"))(a)


def _adamw(w, g, m, v, *, name):
    R, C = w.shape
    tr = _tile(R, max(8, (1 << 18) // C // 8 * 8), 8)
    c1 = 1.0 / (1.0 - ADAM_B1 ** ADAM_STEP)
    c2 = 1.0 / (1.0 - ADAM_B2 ** ADAM_STEP)

    def body(w_ref, g_ref, m_ref, v_ref, d_ref, nm_ref, nv_ref):
        gv = g_ref[...]
        nm = ADAM_B1 * m_ref[...] + (1.0 - ADAM_B1) * gv
        nv = ADAM_B2 * v_ref[...] + (1.0 - ADAM_B2) * (gv * gv)
        d_ref[...] = -ADAM_LR * ((nm * c1) / (jnp.sqrt(nv * c2) + ADAM_EPS) + ADAM_WD * w_ref[...])
        nm_ref[...] = nm
        nv_ref[...] = nv

    blk = pl.BlockSpec((tr, C), lambda i: (i, 0))
    shape = jax.ShapeDtypeStruct((R, C), F32)
    return _pallas(body, name=name, grid=(R // tr,), in_specs=[blk] * 4, out_specs=[blk] * 3, out_shape=[shape] * 3,
                   compiler_params=_params("parallel"))(w, g, m, v)


def kernel(x, mem, w_in, g_mix, a_gq, a_gkv, a_wuq, a_wukv, b_lb, b_gout, c_sink, rel_bias, w_br_a, w_br_b, w_br_c, w_out, g_x, g_mem, x_wq, x_wkv, x_wo, g_ffn, f_w1, f_w3, f_w2, g_final, loss_target, m_w_in, m_g_mix, m_a_gq, m_a_gkv, m_a_wuq, m_a_wukv, m_b_lb, m_b_gout, m_c_sink, m_rel_bias, m_w_br_a, m_w_br_b, m_w_br_c, m_w_out, m_g_x, m_g_mem, m_x_wq, m_x_wkv, m_x_wo, m_g_ffn, m_f_w1, m_f_w3, m_f_w2, m_g_final, v_w_in, v_g_mix, v_a_gq, v_a_gkv, v_a_wuq, v_a_wukv, v_b_lb, v_b_gout, v_c_sink, v_rel_bias, v_w_br_a, v_w_br_b, v_w_br_c, v_w_out, v_g_x, v_g_mem, v_x_wq, v_x_wkv, v_x_wo, v_g_ffn, v_f_w1, v_f_w3, v_f_w2, v_g_final):
    ws = dict(zip(WEIGHT_NAMES, (w_in, g_mix, a_gq, a_gkv, a_wuq, a_wukv, b_lb, b_gout, c_sink, rel_bias, w_br_a, w_br_b, w_br_c,
                                 w_out, g_x, g_mem, x_wq, x_wkv, x_wo, g_ffn, f_w1, f_w3, f_w2, g_final)))
    ms = dict(zip(WEIGHT_NAMES, (m_w_in, m_g_mix, m_a_gq, m_a_gkv, m_a_wuq, m_a_wukv, m_b_lb, m_b_gout, m_c_sink, m_rel_bias,
                                 m_w_br_a, m_w_br_b, m_w_br_c, m_w_out, m_g_x, m_g_mem, m_x_wq, m_x_wkv, m_x_wo, m_g_ffn,
                                 m_f_w1, m_f_w3, m_f_w2, m_g_final)))
    vs = dict(zip(WEIGHT_NAMES, (v_w_in, v_g_mix, v_a_gq, v_a_gkv, v_a_wuq, v_a_wukv, v_b_lb, v_b_gout, v_c_sink, v_rel_bias,
                                 v_w_br_a, v_w_br_b, v_w_br_c, v_w_out, v_g_x, v_g_mem, v_x_wq, v_x_wkv, v_x_wo, v_g_ffn,
                                 v_f_w1, v_f_w3, v_f_w2, v_g_final)))
    sharded = [n for n, _ in SHARDED]
    axis_of = dict(SHARDED)

    def wire(n):
        return lax.bitcast_convert_type(ws[n], BF16) if n == "b_lb" else ws[n].astype(BF16)

    core = lax.axis_index("c").astype(jnp.int32).reshape(1)
    chip = (2 * lax.axis_index("x") + lax.axis_index("y")).astype(jnp.int32).reshape(1)
    wire_shapes = [wire(n).shape for n in sharded]
    packed = _pack([wire(n) for n in sharded], PACK_COLS, PACK_ALIGN)
    gathered = _chip_gather(packed.reshape(2, packed.shape[0] // 2, PACK_COLS), chip, name="gather_weights")
    per_chip = [_unpack(gathered[k], wire_shapes) for k in range(N_CHIPS)]
    full = dict(ws)
    for i, n in enumerate(sharded):
        parts = [per_chip[k][i] for k in range(N_CHIPS)]
        if n == "b_lb":
            parts = [lax.bitcast_convert_type(p, F32) for p in parts]
        full[n] = jnp.concatenate(parts, axis=axis_of[n])

    loss, grad_x, grads = _local_step(x[0], mem[0], loss_target[0], full)

    pieces = []
    for k in range(N_CHIPS):
        pieces += _pack_pieces([jnp.split(grads[n], N_CHIPS, axis=axis_of[n])[k] for n in sharded], PACK_COLS, PACK_ALIGN)
    g4 = jnp.concatenate(pieces, axis=0).reshape(N_CHIPS, 2, -1, PACK_COLS)
    got = _pair_swap(g4, halves=True, name="reduce_pair_swap")
    pair_sum, pair_sum_wire = _pair_add(g4, got, core, name="reduce_pair_add")
    landed = _chip_scatter(pair_sum_wire, name="reduce_chip_scatter")
    mine = _chip_sum(pair_sum, landed, chip, name="reduce_chip_sum")
    g_shard = _join_halves(mine, _pair_swap(mine, halves=False, name="reduce_pair_join"), core, name="reduce_join_halves")

    small = _pack_small([grads[n] for n in REPLICATED] + [loss[0, 0:1]])
    small_sum = _sum_slots(_gather8(small, name="gather_small"), name="sum_small")
    small_grads = _unpack(small_sum, [ws[n].shape for n in REPLICATED] + [(1,)])
    loss_total = small_grads.pop()[0]

    shard_shapes = [ws[n].shape for n in sharded]
    out = {}
    for n, gr in zip(sharded, _unpack(g_shard, shard_shapes)):
        flat2 = lambda a: a.reshape(-1, a.shape[-1])
        d, nm, nv = _adamw(flat2(ws[n]), flat2(gr), flat2(ms[n]), flat2(vs[n]), name="adamw_" + n)
        out[n] = (gr, d.reshape(gr.shape), nm.reshape(gr.shape), nv.reshape(gr.shape))
    pk_s = lambda d: _pack_small([d[n] for n in REPLICATED])
    rep_shapes = [ws[n].shape for n in REPLICATED]
    gs_flat = _pack_small(small_grads)
    ds_flat, ms_flat, vs_flat = _adamw(pk_s(ws), gs_flat, pk_s(ms), pk_s(vs), name="adamw_replicated")
    for n, gr, d, nm, nv in zip(REPLICATED, small_grads, _unpack(ds_flat, rep_shapes), _unpack(ms_flat, rep_shapes),
                                _unpack(vs_flat, rep_shapes)):
        out[n] = (gr, d, nm, nv)
    return (loss_total, grad_x[None], *[out[n][0] for n in WEIGHT_NAMES], *[out[n][1] for n in WEIGHT_NAMES],
            *[out[n][2] for n in WEIGHT_NAMES], *[out[n][3] for n in WEIGHT_NAMES])
```

```python
import math

import jax
import jax.numpy as jnp
from jax import lax
from jax.experimental import pallas as pl
from jax.experimental.pallas import tpu as pltpu

F32 = jnp.float32
BF16 = jnp.bfloat16

D_MODEL = 1024
DEPTH = 2
EPS = 1e-6
MASK_VALUE = -1e30
TINY = 1e-30
A_HEADS, A_NOPE, A_ROPE, A_V = 8, 64, 32, 64
A_QK = A_NOPE + A_ROPE
A_Q_RANK, A_KV_RANK = 384, 256
ROPE_THETA = 10000.0
B_HEADS, B_DK, B_DV, B_CHUNK = 8, 128, 64, 16
C_HEADS, C_KV_HEADS, C_DH, C_WINDOW, C_BLOCK = 8, 2, 64, 128, 128
REL_BUCKETS, REL_MAX_DIST = 32, 128
X_HEADS, X_DH = 4, 256
D_FF = 2816
ADAM_LR, ADAM_B1, ADAM_B2, ADAM_EPS, ADAM_WD, ADAM_STEP = 0.001, 0.9, 0.999, 1e-08, 0.01, 10

LANES = 128
VMEM_LIMIT = 56 * 1024 * 1024


def _pallas(body, **kw):
    return pl.pallas_call(body, **kw)


def _params(*sem):
    return pltpu.CompilerParams(dimension_semantics=sem, vmem_limit_bytes=VMEM_LIMIT)


def _tile(n, pref, unit=LANES):
    if n <= pref:
        return n
    t = (pref // unit) * unit
    while t > unit and n % t:
        t -= unit
    assert n % t == 0, (n, pref, unit)
    return t


def _dot(a, b, dims):
    return lax.dot_general(a, b, (dims, ((), ())), preferred_element_type=F32)


_NN = ((1,), (0,))
_NT = ((1,), (1,))
_TN = ((0,), (0,))


def _mm_tiles(M, N, K, mode):
    tm = _tile(M, 1408 if M % 1408 == 0 else (1024 if mode != "tn" and M >= 2048 else 512), LANES if mode == "tn" else 16)
    tn = _tile(N, 1408 if N % 1408 == 0 else 1024, 256 if N % 256 == 0 and N % 1408 else LANES)
    tk = K if K <= 2816 else _tile(K, 1024)
    return tm, tn, tk


def _mm(a, b, *, mode="nn", add=None, out_dtype=F32, tiles=None, name):
    if mode == "nn":
        (M, K), (K2, N) = a.shape, b.shape
    elif mode == "nt":
        (M, K), (N, K2) = a.shape, b.shape
    else:
        (K, M), (K2, N) = a.shape, b.shape
    assert K == K2, (a.shape, b.shape, mode)
    tm, tn, tk = tiles or _mm_tiles(M, N, K, mode)
    nk = K // tk
    dims = {"nn": _NN, "nt": _NT, "tn": _TN}[mode]
    a_spec = pl.BlockSpec((tk, tm), lambda i, j, k: (k, i)) if mode == "tn" else pl.BlockSpec((tm, tk), lambda i, j, k: (i, k))
    b_spec = pl.BlockSpec((tn, tk), lambda i, j, k: (j, k)) if mode == "nt" else pl.BlockSpec((tk, tn), lambda i, j, k: (k, j))
    o_spec = pl.BlockSpec((tm, tn), lambda i, j, k: (i, j))
    has_add = add is not None

    def body(*refs):
        if has_add:
            a_ref, b_ref, add_ref, o_ref, acc_ref = refs
        else:
            a_ref, b_ref, o_ref, acc_ref = refs
        k = pl.program_id(2)
        part = _dot(a_ref[...].astype(BF16), b_ref[...].astype(BF16), dims)

        @pl.when(k == 0)
        def _():
            acc_ref[...] = part

        @pl.when(k > 0)
        def _():
            acc_ref[...] += part

        @pl.when(k == nk - 1)
        def _():
            r = acc_ref[...]
            if has_add:
                r = r + add_ref[...].astype(F32)
            o_ref[...] = r.astype(out_dtype)

    ins = [a, b] + ([add] if has_add else [])
    in_specs = [a_spec, b_spec] + ([o_spec] if has_add else [])
    return _pallas(
        body, name=name, grid=(M // tm, N // tn, nk), in_specs=in_specs, out_specs=o_spec,
        out_shape=jax.ShapeDtypeStruct((M, N), out_dtype), scratch_shapes=[pltpu.VMEM((tm, tn), F32)],
        compiler_params=_params("parallel", "parallel", "arbitrary"),
    )(*ins)


def _rms(x, g, *, col=0, width=None, out_dtype=BF16, name):
    T = x.shape[0]
    width = x.shape[1] if width is None else width
    assert col % width == 0
    tm = _tile(T, 512, 8)
    cb = col // width

    def body(x_ref, g_ref, o_ref):
        xv = x_ref[...].astype(F32)
        r = lax.rsqrt(jnp.mean(xv * xv, axis=-1, keepdims=True) + EPS)
        o_ref[...] = (xv * r * g_ref[...]).astype(out_dtype)

    return _pallas(
        body, name=name, grid=(T // tm,),
        in_specs=[pl.BlockSpec((tm, width), lambda i: (i, cb)), pl.BlockSpec((1, width), lambda i: (0, 0))],
        out_specs=pl.BlockSpec((tm, width), lambda i: (i, 0)),
        out_shape=jax.ShapeDtypeStruct((T, width), out_dtype), compiler_params=_params("parallel"),
    )(x, g.reshape(1, width))


def _rms_bwd(x, g, dy, *, res=None, col=0, width=None, out_dtype=F32, name):
    T = x.shape[0]
    width = x.shape[1] if width is None else width
    assert col % width == 0
    tm = _tile(T, 512, 8)
    cb = col // width
    has_res = res is not None

    def body(*refs):
        if has_res:
            x_ref, g_ref, dy_ref, res_ref, dx_ref, dg_ref = refs
        else:
            x_ref, g_ref, dy_ref, dx_ref, dg_ref = refs
        xv = x_ref[...].astype(F32)
        r = lax.rsqrt(jnp.mean(xv * xv, axis=-1, keepdims=True) + EPS)
        xh = xv * r
        dyv = dy_ref[...].astype(F32)
        dxh = dyv * g_ref[...]
        dx = r * (dxh - xh * jnp.mean(dxh * xh, axis=-1, keepdims=True))
        if has_res:
            dx = dx + res_ref[...].astype(F32)
        dx_ref[...] = dx.astype(out_dtype)
        part = jnp.sum(dyv * xh, axis=0, keepdims=True)

        @pl.when(pl.program_id(0) == 0)
        def _():
            dg_ref[...] = part

        @pl.when(pl.program_id(0) > 0)
        def _():
            dg_ref[...] += part

    row = pl.BlockSpec((tm, width), lambda i: (i, 0))
    ins = [x, g.reshape(1, width), dy] + ([res] if has_res else [])
    in_specs = [pl.BlockSpec((tm, width), lambda i: (i, cb)), pl.BlockSpec((1, width), lambda i: (0, 0)), row] + ([row] if has_res else [])
    return _pallas(
        body, name=name, grid=(T // tm,), in_specs=in_specs,
        out_specs=[row, pl.BlockSpec((1, width), lambda i: (0, 0))],
        out_shape=[jax.ShapeDtypeStruct((T, width), out_dtype), jax.ShapeDtypeStruct((1, width), F32)],
        compiler_params=_params("arbitrary"),
    )(*ins)


def _rope_tables(T):
    half = A_ROPE // 2
    inv = ROPE_THETA ** (-jnp.arange(half, dtype=F32) / half)
    ang = jnp.arange(T, dtype=jnp.int32).astype(F32)[:, None] * inv[None, :]
    c32 = jnp.concatenate([jnp.cos(ang), jnp.cos(ang)], axis=-1)
    s32 = jnp.concatenate([jnp.sin(ang), jnp.sin(ang)], axis=-1)
    pad = A_PAD - A_QK
    cq = jnp.concatenate([jnp.ones((T, A_NOPE), F32), c32, jnp.ones((T, pad), F32)], axis=-1)
    sq = jnp.concatenate([jnp.zeros((T, A_NOPE), F32), s32, jnp.zeros((T, pad), F32)], axis=-1)
    ck = jnp.concatenate([c32, s32, jnp.zeros((T, LANES - 2 * A_ROPE), F32)], axis=-1)
    ck_t = jnp.concatenate([c32, s32], axis=-1).T
    return cq, sq, ck, ck_t


def _rope_swap_cols(w):
    half = A_ROPE // 2
    return jnp.concatenate([-w[..., half:], w[..., :half]], axis=-1)


def _rope_unswap_cols(g):
    half = A_ROPE // 2
    return jnp.concatenate([g[..., half:], -g[..., :half]], axis=-1)


A_PAD = LANES
A_W = A_HEADS * A_PAD
LOG2E = 1.4426950408889634
LN2 = 0.6931471805599453
Q_SCALE = A_QK ** -0.5 * LOG2E


def _qrope(q2, cq, sq, *, name):
    T = q2.shape[0]
    W = A_W
    tm = _tile(T, 512)

    def body(a_ref, b_ref, c_ref, s_ref, o_ref, ot_ref):
        c = jnp.concatenate([c_ref[...]] * A_HEADS, axis=1)
        s = jnp.concatenate([s_ref[...]] * A_HEADS, axis=1)
        q = (a_ref[...] * c + b_ref[...] * s) * Q_SCALE
        o_ref[...] = q.astype(BF16)
        ot_ref[...] = q.T.astype(BF16)

    blk = lambda j: pl.BlockSpec((tm, W), lambda i: (i, j))
    tab = pl.BlockSpec((tm, A_PAD), lambda i: (i, 0))
    return _pallas(body, name=name, grid=(T // tm,), in_specs=[blk(0), blk(1), tab, tab],
                   out_specs=[blk(0), pl.BlockSpec((W, tm), lambda i: (0, i))],
                   out_shape=[jax.ShapeDtypeStruct((T, W), BF16), jax.ShapeDtypeStruct((W, T), BF16)],
                   compiler_params=_params("parallel"))(q2, q2, cq, sq)


def _kprep(kv, za, ck, *, name):
    T = kv.shape[0]
    tm = _tile(T, 512)

    def body(kv_ref, kr_ref, ck_ref, k_ref, vxt_ref):
        t = kr_ref[...] * ck_ref[...]
        krope = (t[:, 0:A_ROPE] + t[:, A_ROPE:2 * A_ROPE]).astype(BF16)
        one = (lax.broadcasted_iota(jnp.int32, (A_PAD - A_V, tm), 0) == 0).astype(BF16)
        for h in range(A_HEADS):
            k_ref[:, A_PAD * h:A_PAD * h + A_NOPE] = kv_ref[:, A_NOPE * h:A_NOPE * (h + 1)]
            k_ref[:, A_PAD * h + A_NOPE:A_PAD * h + A_QK] = krope
            k_ref[:, A_PAD * h + A_QK:A_PAD * (h + 1)] = jnp.zeros((tm, A_PAD - A_QK), BF16)
            vxt_ref[A_PAD * h + A_V:A_PAD * (h + 1), :] = one
        vt = kv_ref[:, 512:1024].astype(F32).T.astype(BF16)
        for h in range(A_HEADS):
            vxt_ref[A_PAD * h:A_PAD * h + A_V, :] = vt[A_V * h:A_V * (h + 1), :]

    wide = pl.BlockSpec((tm, A_W), lambda i: (i, 0))
    return _pallas(
        body, name=name, grid=(T // tm,),
        in_specs=[wide, pl.BlockSpec((tm, LANES), lambda i: (i, 3)), pl.BlockSpec((tm, LANES), lambda i: (i, 0))],
        out_specs=[wide, pl.BlockSpec((A_W, tm), lambda i: (0, i))],
        out_shape=[jax.ShapeDtypeStruct((T, A_W), BF16), jax.ShapeDtypeStruct((A_W, T), BF16)],
        compiler_params=_params("parallel"))(kv, za, ck)


def _kprep_bwd(dkt, ck_t, *, name):
    T = dkt.shape[1]
    tc = _tile(T, 512)

    def body(dk_ref, ck_ref, dn_ref, dr_ref):
        acc = jnp.zeros((A_ROPE, tc), F32)
        for h in range(A_HEADS):
            dn_ref[A_NOPE * h:A_NOPE * (h + 1), :] = dk_ref[A_PAD * h:A_PAD * h + A_NOPE, :].astype(BF16)
            acc = acc + dk_ref[A_PAD * h + A_NOPE:A_PAD * h + A_QK, :]
        dr_ref[0:A_ROPE, :] = (acc * ck_ref[0:A_ROPE, :]).astype(BF16)
        dr_ref[A_ROPE:2 * A_ROPE, :] = (acc * ck_ref[A_ROPE:2 * A_ROPE, :]).astype(BF16)
        dr_ref[2 * A_ROPE:LANES, :] = jnp.zeros((LANES - 2 * A_ROPE, tc), BF16)

    col = lambda r: pl.BlockSpec((r, tc), lambda i: (0, i))
    return _pallas(
        body, name=name, grid=(T // tc,), in_specs=[col(A_W), col(2 * A_ROPE)], out_specs=[col(512), col(LANES)],
        out_shape=[jax.ShapeDtypeStruct((512, T), BF16), jax.ShapeDtypeStruct((LANES, T), BF16)],
        compiler_params=_params("parallel"))(dkt, ck_t)


def _flash_fwd(qs, k, vxt, *, name):
    T = qs.shape[0]
    tq, tk = _tile(T, 512), _tile(T, 2048)
    nk = T // tk
    H, P, DV = A_HEADS, A_PAD, A_V

    def body(q_ref, k_ref, v_ref, o_ref, lse_ref, m_sc, acc_sc):
        j = pl.program_id(1)

        @pl.when(j == 0)
        def _():
            m_sc[...] = jnp.full(m_sc.shape, -jnp.inf, F32)
            acc_sc[...] = jnp.zeros(acc_sc.shape, F32)

        def scores(h):
            return _dot(k_ref[:, P * h:P * (h + 1)], q_ref[:, P * h:P * (h + 1)], _NT)

        st_next = scores(0)
        for h in range(H):
            st = st_next
            if h + 1 < H:
                st_next = scores(h + 1)
            m_prev = m_sc[h]
            m_new = jnp.maximum(m_prev, jnp.max(st, axis=0, keepdims=True))
            pt = jnp.exp2(st - m_new).astype(BF16)
            acc_sc[h] = jnp.exp2(m_prev - m_new) * acc_sc[h] + _dot(v_ref[P * h:P * (h + 1), :], pt, _NN)
            m_sc[h] = m_new

        @pl.when(j == nk - 1)
        def _():
            for h in range(H):
                acc = acc_sc[h]
                l = acc[DV:DV + 1, :]
                o_ref[:, DV * h:DV * (h + 1)] = (acc[0:DV, :] / l).T
                lse_ref[h] = m_sc[h] + jnp.log2(l)

    return _pallas(
        body, name=name, grid=(T // tq, nk),
        in_specs=[pl.BlockSpec((tq, A_W), lambda i, j: (i, 0)), pl.BlockSpec((tk, A_W), lambda i, j: (j, 0)),
                  pl.BlockSpec((A_W, tk), lambda i, j: (0, j))],
        out_specs=[pl.BlockSpec((tq, H * DV), lambda i, j: (i, 0)), pl.BlockSpec((H, 1, tq), lambda i, j: (0, 0, i))],
        out_shape=[jax.ShapeDtypeStruct((T, H * DV), F32), jax.ShapeDtypeStruct((H, 1, T), F32)],
        scratch_shapes=[pltpu.VMEM((H, 1, tq), F32), pltpu.VMEM((H, P, tq), F32)],
        compiler_params=_params("parallel", "arbitrary"))(qs, k, vxt)


def _attn_delta(o, do, *, name):
    T = o.shape[0]
    tm = _tile(T, 512, 8)

    def body(o_ref, do_ref, d_ref):
        prod = o_ref[...] * do_ref[...].astype(F32)
        for h in range(A_HEADS):
            d_ref[h] = jnp.sum(prod[:, A_V * h:A_V * (h + 1)], axis=-1, keepdims=True)

    row = pl.BlockSpec((tm, A_HEADS * A_V), lambda i: (i, 0))
    return _pallas(body, name=name, grid=(T // tm,), in_specs=[row, row],
                   out_specs=pl.BlockSpec((A_HEADS, tm, 1), lambda i: (0, i, 0)),
                   out_shape=jax.ShapeDtypeStruct((A_HEADS, T, 1), F32), compiler_params=_params("parallel"))(o, do)


def _flash_bwd(qs, qst, k, kv, do, dot_, lse2, delta, *, tiles=None, name):
    T = qs.shape[0]
    tq, tk = tiles or (_tile(T, 512), _tile(T, 1024))
    nq, nk = T // tq, T // tk
    H, P, DV = A_HEADS, A_PAD, A_V

    def body(q_ref, qt_ref, k_ref, v_ref, do_ref, dot_ref, lse_ref, delta_ref, dq_ref, dkt_ref, dvt_ref, dkt_sc, dvt_sc):
        i = pl.program_id(1)

        @pl.when(i == 0)
        def _():
            dkt_sc[...] = jnp.zeros(dkt_sc.shape, F32)
            dvt_sc[...] = jnp.zeros(dvt_sc.shape, F32)

        for h in range(H):
            s = _dot(q_ref[:, P * h:P * (h + 1)], k_ref[:, P * h:P * (h + 1)], _NT)
            dp = _dot(do_ref[:, DV * h:DV * (h + 1)], v_ref[:, DV * h:DV * (h + 1)], _NT)
            p = jnp.exp2(s - lse_ref[h])
            ds = (p * (dp - delta_ref[h])).astype(BF16)
            pb = p.astype(BF16)
            dq_ref[0, :, P * h:P * (h + 1)] = _dot(ds, k_ref[:, P * h:P * (h + 1)], _NN).astype(BF16)
            dkt_sc[h] += _dot(qt_ref[P * h:P * (h + 1), :], ds, _NN)
            dvt_sc[h] += _dot(dot_ref[DV * h:DV * (h + 1), :], pb, _NN)

        @pl.when(i == nq - 1)
        def _():
            for h in range(H):
                dkt_ref[P * h:P * (h + 1), :] = dkt_sc[h] * LN2
                dvt_ref[DV * h:DV * (h + 1), :] = dvt_sc[h].astype(BF16)

    qrow = lambda w: pl.BlockSpec((tq, w), lambda j, i: (i, 0))
    qcol = lambda r: pl.BlockSpec((r, tq), lambda j, i: (0, i))
    stat = pl.BlockSpec((H, tq, 1), lambda j, i: (0, i, 0))
    return _pallas(
        body, name=name, grid=(nk, nq),
        in_specs=[qrow(A_W), qcol(A_W), pl.BlockSpec((tk, A_W), lambda j, i: (j, 0)), pl.BlockSpec((tk, H * DV), lambda j, i: (j, 1)),
                  qrow(H * DV), qcol(H * DV), stat, stat],
        out_specs=[pl.BlockSpec((1, tq, A_W), lambda j, i: (j, i, 0)), pl.BlockSpec((A_W, tk), lambda j, i: (0, j)),
                   pl.BlockSpec((H * DV, tk), lambda j, i: (0, j))],
        out_shape=[jax.ShapeDtypeStruct((nk, T, A_W), BF16), jax.ShapeDtypeStruct((A_W, T), F32),
                   jax.ShapeDtypeStruct((H * DV, T), BF16)],
        scratch_shapes=[pltpu.VMEM((H, P, tk), F32), pltpu.VMEM((H, DV, tk), F32)],
        compiler_params=_params("parallel", "arbitrary"))(qs, qst, k, kv, do, dot_, lse2, delta)


def _dq_sum(dq_part, cq, sq, *, name):
    n, T, W = dq_part.shape
    tm = _tile(T, 256, 16)

    def body(p_ref, c_ref, s_ref, o_ref):
        acc = p_ref[0].astype(F32)
        for j in range(1, n):
            acc = acc + p_ref[j].astype(F32)
        acc = acc * (A_QK ** -0.5)
        o_ref[:, 0:W] = (acc * jnp.concatenate([c_ref[...]] * A_HEADS, axis=1)).astype(BF16)
        o_ref[:, W:2 * W] = (acc * jnp.concatenate([s_ref[...]] * A_HEADS, axis=1)).astype(BF16)

    row = pl.BlockSpec((tm, A_PAD), lambda i: (i, 0))
    return _pallas(body, name=name, grid=(T // tm,), in_specs=[pl.BlockSpec((n, tm, W), lambda i: (0, i, 0)), row, row],
                   out_specs=pl.BlockSpec((tm, 2 * W), lambda i: (i, 0)), out_shape=jax.ShapeDtypeStruct((T, 2 * W), BF16),
                   compiler_params=_params("parallel"))(dq_part, cq, sq)


HB = 8 * B_CHUNK


def _chunk_masks(reverse):
    r = lax.broadcasted_iota(jnp.int32, (HB, HB), 0)
    c = lax.broadcasted_iota(jnp.int32, (HB, HB), 1)
    same = (r // B_CHUNK) == (c // B_CHUNK)
    incl = same & ((c >= r) if reverse else (c <= r))
    return same, incl


def _mask_mm(mask, x):
    hi = x.astype(BF16)
    lo = (x - hi.astype(F32)).astype(BF16)
    return _dot(mask, hi, _NN) + _dot(mask, lo, _NN)


def _hgrn_gates(q, z, lb, reverse):
    same, incl = _chunk_masks(reverse)
    sg = jax.nn.sigmoid(z)
    f = lb + (1.0 - lb) * sg
    lf = jnp.log(jnp.maximum(f, TINY))
    kk = (1.0 - lb) * jax.nn.sigmoid(-z)
    b = _mask_mm(incl.astype(BF16), lf)
    edge = 0 if reverse else B_CHUNK - 1
    btot = jnp.concatenate([jnp.broadcast_to(b[B_CHUNK * c + edge:B_CHUNK * c + edge + 1, :], (B_CHUNK, b.shape[1]))
                            for c in range(HB // B_CHUNK)], axis=0)
    eb, enb, er, dec = jnp.exp(b), jnp.exp(-b), jnp.exp(btot - b), jnp.exp(btot)
    return dict(same=same, incl=incl, sg=sg, f=f, kk=kk, eb=eb, enb=enb, er=er, dec=dec,
                qd=q * eb, ki=kk * enb, ke=kk * er)


def _hgrn_specs(T, reverse, gate_reverse):
    nb = T // HB
    blk = (lambda i: nb - 1 - i) if reverse else (lambda i: i)
    wide = B_HEADS * B_DK
    return nb, blk, [
        pl.BlockSpec((HB, wide), lambda i: (blk(i), 0)),
        pl.BlockSpec((HB, wide), lambda i: (blk(i), 2 if gate_reverse else 1)),
        pl.BlockSpec((HB, B_HEADS * B_DV), lambda i: (blk(i), 6)),
        pl.BlockSpec((1, wide), lambda i: (0, 0)),
    ]


def _hk(h):
    return slice(B_DK * h, B_DK * (h + 1))


def _hv(h):
    return slice(B_DV * h, B_DV * (h + 1))


def _crows(c):
    return slice(B_CHUNK * c, B_CHUNK * (c + 1))


def _chunk_selectors():
    r = lax.broadcasted_iota(jnp.int32, (HB, 1), 0) // B_CHUNK
    l = lax.broadcasted_iota(jnp.int32, (1, HB), 1) // B_CHUNK
    return [r == c for c in range(8)], [l == c for c in range(8)]


def _hgrn_fwd(zb, lb, *, reverse, name):
    T = zb.shape[0]
    nb, blk, in_specs = _hgrn_specs(T, reverse, reverse)
    order = range(7, -1, -1) if reverse else range(8)
    heads = range(B_HEADS)

    def body(q_ref, z_ref, v_ref, lb_ref, o_ref, st_ref, s_sc):
        @pl.when(pl.program_id(0) == 0)
        def _():
            s_sc[...] = jnp.zeros(s_sc.shape, F32)

        g = _hgrn_gates(q_ref[...], z_ref[...], lb_ref[...], reverse)
        v = v_ref[...].astype(BF16)
        qd, ki, ke = g["qd"].astype(BF16), g["ki"].astype(BF16), g["ke"].astype(BF16)
        dec = g["dec"]
        in_chunk_rows, in_chunk_lanes = _chunk_selectors()
        o_intra, upd = [], []
        for h in heads:
            a = jnp.where(g["incl"], _dot(qd[:, _hk(h)], ki[:, _hk(h)], _NT), 0.0)
            o_intra.append(_dot(a.astype(BF16), v[:, _hv(h)], _NN))
            vt = v[:, _hv(h)].T
            lhs = jnp.concatenate([jnp.where(in_chunk_lanes[c], vt, 0) for c in range(8)], axis=0)
            upd.append(_dot(lhs, ke[:, _hk(h)], _NN))
        st = [s_sc[h] for h in heads]
        snap = [[None] * 8 for _ in heads]
        for c in order:
            for h in heads:
                snap[h][c] = st[h]
                st[h] = st[h] * dec[B_CHUNK * c:B_CHUNK * c + 1, _hk(h)] + upd[h][B_DV * c:B_DV * (c + 1), :]
        for h in heads:
            s_sc[h] = st[h]
            for c in range(8):
                st_ref[h, c] = snap[h][c]
            qd_big = jnp.concatenate([jnp.where(in_chunk_rows[c], qd[:, _hk(h)], 0) for c in range(8)], axis=1)
            states = jnp.concatenate([snap[h][c].astype(BF16) for c in range(8)], axis=1)
            o_ref[h] = o_intra[h] + _dot(qd_big, states, _NT)

    return _pallas(
        body, name=name, grid=(nb,), in_specs=in_specs,
        out_specs=[pl.BlockSpec((B_HEADS, HB, B_DV), lambda i: (0, blk(i), 0)),
                   pl.BlockSpec((B_HEADS, 8, B_DV, B_DK), lambda i: (0, blk(i), 0, 0))],
        out_shape=[jax.ShapeDtypeStruct((B_HEADS, T, B_DV), F32),
                   jax.ShapeDtypeStruct((B_HEADS, T // B_CHUNK, B_DV, B_DK), F32)],
        scratch_shapes=[pltpu.VMEM((B_HEADS, B_DV, B_DK), F32)],
        compiler_params=_params("arbitrary"))(zb, zb, zb, lb)


def _hgrn_bwd(zb, lb, do, states, *, reverse, name):
    T = zb.shape[0]
    nb, blk, in_specs = _hgrn_specs(T, not reverse, reverse)
    order = range(8) if reverse else range(7, -1, -1)
    heads = range(B_HEADS)

    def body(q_ref, z_ref, v_ref, lb_ref, do_ref, st_ref, dq_ref, dz_ref, dv_ref, dlb_ref, ds_sc):
        @pl.when(pl.program_id(0) == 0)
        def _():
            ds_sc[...] = jnp.zeros(ds_sc.shape, F32)
            dlb_ref[...] = jnp.zeros(dlb_ref.shape, F32)

        lb = lb_ref[...]
        g = _hgrn_gates(q_ref[...], z_ref[...], lb, reverse)
        v = v_ref[...].astype(BF16)
        qd, ki, ke = g["qd"].astype(BF16), g["ki"].astype(BF16), g["ke"].astype(BF16)
        dec = g["dec"]
        dout = [do_ref[h].astype(BF16) for h in heads]
        in_chunk_rows, in_chunk_lanes = _chunk_selectors()
        _, incl_t = _chunk_masks(not reverse)
        rows_of = lambda x: jnp.concatenate([jnp.where(in_chunk_rows[c], x, 0) for c in range(8)], axis=1)
        dv_i, dqd_h, dki, upd = [], [], [], []
        for h in heads:
            qd_h, ki_h, v_h = qd[:, _hk(h)], ki[:, _hk(h)], v[:, _hv(h)]
            da = jnp.where(g["incl"], _dot(dout[h], v_h, _NT), 0.0).astype(BF16)
            at = jnp.where(incl_t, _dot(ki_h, qd_h, _NT), 0.0).astype(BF16)
            dat = jnp.where(incl_t, _dot(v_h, dout[h], _NT), 0.0).astype(BF16)
            dv_i.append(_dot(at, dout[h], _NN))
            dki.append(_dot(dat, qd_h, _NN))
            dot_t = dout[h].T
            lhs = jnp.concatenate([jnp.where(in_chunk_lanes[c], dot_t, 0) for c in range(8)], axis=0)
            upd.append(_dot(lhs, qd_h, _NN))
            saved = jnp.concatenate([st_ref[h, c].astype(BF16) for c in range(8)], axis=0)
            dqd_h.append(_dot(da, ki_h, _NN) + _dot(rows_of(dout[h]), saved, _NN))
        dst = [ds_sc[h] for h in heads]
        used = [[None] * 8 for _ in heads]
        for c in order:
            for h in heads:
                used[h][c] = dst[h]
                dst[h] = dst[h] * dec[B_CHUNK * c:B_CHUNK * c + 1, _hk(h)] + upd[h][B_DV * c:B_DV * (c + 1), :]
        dke_h, ddec_h = [], []
        for h in heads:
            ds_sc[h] = dst[h]
            used16 = [used[h][c].astype(BF16) for c in range(8)]
            dv_ref[h] = dv_i[h] + _dot(rows_of(ke[:, _hk(h)]), jnp.concatenate(used16, axis=1), _NT)
            dke_h.append(_dot(rows_of(v[:, _hv(h)]), jnp.concatenate(used16, axis=0), _NN))
            ddec_p = []
            for c in range(8):
                tot = jnp.sum(used[h][c] * st_ref[h, c], axis=0, keepdims=True) * dec[B_CHUNK * c:B_CHUNK * c + 1, _hk(h)]
                ddec_p.append(jnp.broadcast_to(tot, (B_CHUNK, B_DK)))
            ddec_h.append(jnp.concatenate(ddec_p, axis=0))
        dqd = jnp.concatenate(dqd_h, axis=1)
        dki = jnp.concatenate(dki, axis=1)
        dke = jnp.concatenate(dke_h, axis=1)
        db = dqd * g["qd"] - dki * g["ki"] - dke * g["ke"]
        masks = jnp.concatenate([incl_t.astype(BF16), g["same"].astype(BF16)], axis=1)
        dlf = _mask_mm(masks, jnp.concatenate([db, dke * g["ke"]], axis=0)) + jnp.concatenate(ddec_h, axis=1)
        dk = dki * g["enb"] + dke * g["er"]
        u = jnp.where(g["f"] > TINY, dlf / g["f"], 0.0) - dk
        sg = g["sg"]
        dq_ref[...] = dqd * g["eb"]
        dz_ref[...] = u * (1.0 - lb) * sg * (1.0 - sg)
        dlb_ref[...] += jnp.sum(u * (1.0 - sg), axis=0, keepdims=True)

    wide = pl.BlockSpec((HB, B_HEADS * B_DK), lambda i: (blk(i), 0))
    hm = pl.BlockSpec((B_HEADS, HB, B_DV), lambda i: (0, blk(i), 0))
    return _pallas(
        body, name=name, grid=(nb,),
        in_specs=in_specs + [hm, pl.BlockSpec((B_HEADS, 8, B_DV, B_DK), lambda i: (0, blk(i), 0, 0))],
        out_specs=[wide, wide, hm, pl.BlockSpec((1, B_HEADS * B_DK), lambda i: (0, 0))],
        out_shape=[jax.ShapeDtypeStruct((T, B_HEADS * B_DK), F32), jax.ShapeDtypeStruct((T, B_HEADS * B_DK), F32),
                   jax.ShapeDtypeStruct((B_HEADS, T, B_DV), F32), jax.ShapeDtypeStruct((1, B_HEADS * B_DK), F32)],
        scratch_shapes=[pltpu.VMEM((B_HEADS, B_DV, B_DK), F32)],
        compiler_params=_params("arbitrary"))(zb, zb, zb, lb, do, states)


def _hgrn_out(of, ob, zb, gout, *, name):
    T = zb.shape[0]
    tm = _tile(T, 512, 8)

    def body(of_ref, ob_ref, g_ref, gout_ref, y_ref):
        for h in range(B_HEADS):
            o = of_ref[h] + ob_ref[h]
            r = lax.rsqrt(jnp.mean(o * o, axis=-1, keepdims=True) + EPS)
            gh = g_ref[:, B_DV * h:B_DV * (h + 1)]
            y_ref[:, B_DV * h:B_DV * (h + 1)] = (o * r * gout_ref[...] * (gh * jax.nn.sigmoid(gh))).astype(BF16)

    hm = pl.BlockSpec((B_HEADS, tm, B_DV), lambda i: (0, i, 0))
    return _pallas(
        body, name=name, grid=(T // tm,),
        in_specs=[hm, hm, pl.BlockSpec((tm, 512), lambda i: (i, 7)), pl.BlockSpec((1, B_DV), lambda i: (0, 0))],
        out_specs=pl.BlockSpec((tm, 512), lambda i: (i, 0)),
        out_shape=jax.ShapeDtypeStruct((T, 512), BF16), compiler_params=_params("parallel"))(of, ob, zb, gout.reshape(1, B_DV))


def _hgrn_out_bwd(of, ob, zb, gout, dy, *, name):
    T = zb.shape[0]
    tm = _tile(T, 512, 8)

    def body(of_ref, ob_ref, g_ref, gout_ref, dy_ref, do_ref, dg_ref, dgo_ref):
        gout_v = gout_ref[...]
        acc = jnp.zeros((1, B_DV), F32)
        for h in range(B_HEADS):
            o = of_ref[h] + ob_ref[h]
            r = lax.rsqrt(jnp.mean(o * o, axis=-1, keepdims=True) + EPS)
            oh = o * r
            gh = g_ref[:, B_DV * h:B_DV * (h + 1)]
            sg = jax.nn.sigmoid(gh)
            dyh = dy_ref[:, B_DV * h:B_DV * (h + 1)].astype(F32)
            dn = dyh * (gh * sg)
            dg_ref[:, B_DV * h:B_DV * (h + 1)] = dyh * (oh * gout_v) * (sg * (1.0 + gh * (1.0 - sg)))
            dxh = dn * gout_v
            do_ref[h] = r * (dxh - oh * jnp.mean(dxh * oh, axis=-1, keepdims=True))
            acc = acc + jnp.sum(dn * oh, axis=0, keepdims=True)

        @pl.when(pl.program_id(0) == 0)
        def _():
            dgo_ref[...] = acc

        @pl.when(pl.program_id(0) > 0)
        def _():
            dgo_ref[...] += acc

    hm = pl.BlockSpec((B_HEADS, tm, B_DV), lambda i: (0, i, 0))
    row = pl.BlockSpec((tm, 512), lambda i: (i, 0))
    return _pallas(
        body, name=name, grid=(T // tm,),
        in_specs=[hm, hm, pl.BlockSpec((tm, 512), lambda i: (i, 7)), pl.BlockSpec((1, B_DV), lambda i: (0, 0)), row],
        out_specs=[hm, row, pl.BlockSpec((1, B_DV), lambda i: (0, 0))],
        out_shape=[jax.ShapeDtypeStruct((B_HEADS, T, B_DV), F32), jax.ShapeDtypeStruct((T, 512), F32),
                   jax.ShapeDtypeStruct((1, B_DV), F32)],
        compiler_params=_params("arbitrary"))(of, ob, zb, gout.reshape(1, B_DV), dy)


def _dzb_assemble(dq_f, dq_b, dzf, dzb_, dv_f, dv_b, dgate, *, name):
    T = dq_f.shape[0]
    tm = _tile(T, 256, 8)

    def body(qf, qb, zf, zr, vf, vr, dg, o_ref):
        o_ref[:, 0:1024] = (qf[...] + qb[...]).astype(BF16)
        o_ref[:, 1024:2048] = zf[...].astype(BF16)
        o_ref[:, 2048:3072] = zr[...].astype(BF16)
        for h in range(B_HEADS):
            o_ref[:, 3072 + B_DV * h:3072 + B_DV * (h + 1)] = (vf[h] + vr[h]).astype(BF16)
        o_ref[:, 3584:4096] = dg[...].astype(BF16)

    wide = pl.BlockSpec((tm, 1024), lambda i: (i, 0))
    hm = pl.BlockSpec((B_HEADS, tm, B_DV), lambda i: (0, i, 0))
    return _pallas(
        body, name=name, grid=(T // tm,),
        in_specs=[wide, wide, wide, wide, hm, hm, pl.BlockSpec((tm, 512), lambda i: (i, 0))],
        out_specs=pl.BlockSpec((tm, 4096), lambda i: (i, 0)),
        out_shape=jax.ShapeDtypeStruct((T, 4096), BF16), compiler_params=_params("parallel"))(dq_f, dq_b, dzf, dzb_, dv_f, dv_b, dgate)


C_SPAN = 3 * C_BLOCK
C_G = C_HEADS // C_KV_HEADS


def _t5_bucket(rel):
    nb = REL_BUCKETS // 2
    max_exact = nb // 2
    ret = (rel > 0).astype(jnp.int32) * nb
    n = jnp.abs(rel)
    large = max_exact + (jnp.log(jnp.maximum(n, 1).astype(F32) / max_exact)
                         / math.log(REL_MAX_DIST / max_exact) * (nb - max_exact)).astype(jnp.int32)
    large = jnp.minimum(large, nb - 1)
    return ret + jnp.where(n < max_exact, n, large)


def _swa_buckets():
    rel = jnp.arange(C_SPAN)[None, :] - C_BLOCK - jnp.arange(C_BLOCK)[:, None]
    return _t5_bucket(rel)


def _swa_specs(T):
    nb = T // C_BLOCK
    return nb, [
        pl.BlockSpec((C_BLOCK, 512), lambda n: (n, 0)),
        pl.BlockSpec((C_BLOCK, LANES), lambda n: (jnp.maximum(n - 1, 0), 4)),
        pl.BlockSpec((C_BLOCK, LANES), lambda n: (n, 4)),
        pl.BlockSpec((C_BLOCK, LANES), lambda n: (jnp.minimum(n + 1, nb - 1), 4)),
        pl.BlockSpec((C_BLOCK, LANES), lambda n: (jnp.maximum(n - 1, 0), 5)),
        pl.BlockSpec((C_BLOCK, LANES), lambda n: (n, 5)),
        pl.BlockSpec((C_BLOCK, LANES), lambda n: (jnp.minimum(n + 1, nb - 1), 5)),
        pl.BlockSpec((C_HEADS, C_BLOCK, C_SPAN), lambda n: (0, 0, 0)),
        pl.BlockSpec(memory_space=pltpu.SMEM),
    ]


def _swa_valid(n, T):
    qi = lax.broadcasted_iota(jnp.int32, (C_BLOCK, C_SPAN), 0)
    si = lax.broadcasted_iota(jnp.int32, (C_BLOCK, C_SPAN), 1)
    rel = si - C_BLOCK - qi
    kpos = (n - 1) * C_BLOCK + si
    return (jnp.abs(rel) <= C_WINDOW) & (kpos >= 0) & (kpos < T)


def _swa_softmax(raw, bias, valid, sink):
    s = raw * (C_DH ** -0.5) + bias
    s = jnp.where(valid, s, MASK_VALUE)
    m = jnp.maximum(jnp.max(s, axis=-1, keepdims=True), sink)
    e = jnp.exp(s - m)
    den = jnp.sum(e, axis=-1, keepdims=True) + jnp.exp(sink - m)
    return e / den, jnp.exp(sink - m) / den


def _swa_fwd(zc, bias, sink, *, name):
    T = zc.shape[0]
    nb, in_specs = _swa_specs(T)

    def body(q_ref, kp, kc, kn, vp, vc, vn, bias_ref, sink_ref, y_ref):
        n = pl.program_id(0)
        kcat = jnp.concatenate([kp[...], kc[...], kn[...]], axis=0)
        vcat = jnp.concatenate([vp[...], vc[...], vn[...]], axis=0)
        valid = _swa_valid(n, T)
        heads = range(C_HEADS)
        kvs = [slice(C_DH * (h // C_G), C_DH * (h // C_G + 1)) for h in heads]
        scores = [_dot(q_ref[:, C_DH * h:C_DH * (h + 1)], kcat[:, kvs[h]], _NT) for h in heads]
        probs = [_swa_softmax(scores[h], bias_ref[h], valid, sink_ref[h])[0].astype(BF16) for h in heads]
        for h in heads:
            y_ref[:, C_DH * h:C_DH * (h + 1)] = _dot(probs[h], vcat[:, kvs[h]], _NN).astype(BF16)

    return _pallas(
        body, name=name, grid=(nb,), in_specs=in_specs, out_specs=pl.BlockSpec((C_BLOCK, 512), lambda n: (n, 0)),
        out_shape=jax.ShapeDtypeStruct((T, 512), BF16), compiler_params=_params("parallel"))(zc, zc, zc, zc, zc, zc, zc, bias, sink)


def _swa_bwd(zc, bias, sink, dy, *, name):
    T = zc.shape[0]
    nb, in_specs = _swa_specs(T)
    scale = C_DH ** -0.5

    def body(q_ref, kp, kc, kn, vp, vc, vn, bias_ref, sink_ref, dy_ref, dq_ref, dkc_ref, dvc_ref, dbias_ref, dsink_ref):
        n = pl.program_id(0)

        @pl.when(n == 0)
        def _():
            dbias_ref[...] = jnp.zeros(dbias_ref.shape, F32)
            dsink_ref[...] = jnp.zeros(dsink_ref.shape, F32)

        kcat = jnp.concatenate([kp[...], kc[...], kn[...]], axis=0)
        vcat = jnp.concatenate([vp[...], vc[...], vn[...]], axis=0)
        valid = _swa_valid(n, T)
        heads = range(C_HEADS)
        kvs = [slice(C_DH * (h // C_G), C_DH * (h // C_G + 1)) for h in heads]
        qs = [q_ref[:, C_DH * h:C_DH * (h + 1)] for h in heads]
        dos = [dy_ref[:, C_DH * h:C_DH * (h + 1)].astype(BF16) for h in heads]
        scores = [_dot(qs[h], kcat[:, kvs[h]], _NT) for h in heads]
        dps = [_dot(dos[h], vcat[:, kvs[h]], _NT) for h in heads]
        pbs, dsbs = [], []
        for h in heads:
            p, p_sink = _swa_softmax(scores[h], bias_ref[h], valid, sink_ref[h])
            rowdot = jnp.sum(p * dps[h], axis=-1, keepdims=True)
            ds = p * (dps[h] - rowdot)
            dbias_ref[h] += ds
            tot = jnp.sum(jnp.sum(-p_sink * rowdot, axis=0, keepdims=True), axis=1, keepdims=True)
            dsink_ref[h:h + 1, :] += jnp.broadcast_to(tot, (1, LANES))
            pbs.append(p.astype(BF16))
            dsbs.append((ds * scale).astype(BF16))
        for h in heads:
            dq_ref[:, C_DH * h:C_DH * (h + 1)] = _dot(dsbs[h], kcat[:, kvs[h]], _NN).astype(BF16)
        dks = [_dot(dsbs[h], qs[h], _TN) for h in heads]
        dvs = [_dot(pbs[h], dos[h], _TN) for h in heads]
        for kv in range(C_KV_HEADS):
            group = range(kv * C_G, (kv + 1) * C_G)
            dkc_ref[0, :, C_DH * kv:C_DH * (kv + 1)] = sum(dks[h] for h in group)
            dvc_ref[0, :, C_DH * kv:C_DH * (kv + 1)] = sum(dvs[h] for h in group)

    part = pl.BlockSpec((1, C_SPAN, LANES), lambda n: (n, 0, 0))
    dq, dkc, dvc, dbias, dsink = _pallas(
        body, name=name, grid=(nb,), in_specs=in_specs + [pl.BlockSpec((C_BLOCK, 512), lambda n: (n, 0))],
        out_specs=[pl.BlockSpec((C_BLOCK, 512), lambda n: (n, 0)), part, part,
                   pl.BlockSpec((C_HEADS, C_BLOCK, C_SPAN), lambda n: (0, 0, 0)), pl.BlockSpec((C_HEADS, LANES), lambda n: (0, 0))],
        out_shape=[jax.ShapeDtypeStruct((T, 512), BF16), jax.ShapeDtypeStruct((nb, C_SPAN, LANES), F32),
                   jax.ShapeDtypeStruct((nb, C_SPAN, LANES), F32), jax.ShapeDtypeStruct((C_HEADS, C_BLOCK, C_SPAN), F32),
                   jax.ShapeDtypeStruct((C_HEADS, LANES), F32)],
        compiler_params=_params("arbitrary"))(zc, zc, zc, zc, zc, zc, zc, bias, sink, dy)

    def combine(dq_ref, kp, kc, kn, vp, vc, vn, o_ref):
        n = pl.program_id(0)
        lo = (n > 0).astype(F32)
        hi = (n < nb - 1).astype(F32)
        o_ref[:, 0:512] = dq_ref[...]
        o_ref[:, 512:640] = (kp[0] * lo + kc[0] + kn[0] * hi).astype(BF16)
        o_ref[:, 640:768] = (vp[0] * lo + vc[0] + vn[0] * hi).astype(BF16)

    prev = pl.BlockSpec((1, C_BLOCK, LANES), lambda n: (jnp.maximum(n - 1, 0), 2, 0))
    cur = pl.BlockSpec((1, C_BLOCK, LANES), lambda n: (n, 1, 0))
    nxt = pl.BlockSpec((1, C_BLOCK, LANES), lambda n: (jnp.minimum(n + 1, nb - 1), 0, 0))
    dzc = _pallas(
        combine, name=name + "_combine", grid=(nb,),
        in_specs=[pl.BlockSpec((C_BLOCK, 512), lambda n: (n, 0)), prev, cur, nxt, prev, cur, nxt],
        out_specs=pl.BlockSpec((C_BLOCK, 768), lambda n: (n, 0)),
        out_shape=jax.ShapeDtypeStruct((T, 768), BF16), compiler_params=_params("parallel"))(dq, dkc, dkc, dkc, dvc, dvc, dvc)
    return dzc, dbias, dsink


def _merge_tiles(T):
    return _tile(T, 512, 8), 512


def _merge_fwd(ya, yb, yc, wa, wb, wc, zg, *, name):
    T = ya.shape[0]
    tm, tn = _merge_tiles(T)
    nd = D_MODEL // tn

    def body(ya_ref, yb_ref, yc_ref, wa_ref, wb_ref, wc_ref, ga_ref, gb_ref, gc_ref, o_ref):
        acc = jax.nn.sigmoid(ga_ref[...].astype(F32)) * _dot(ya_ref[...].astype(BF16), wa_ref[...], _NN)
        acc += jax.nn.sigmoid(gb_ref[...].astype(F32)) * _dot(yb_ref[...].astype(BF16), wb_ref[...], _NN)
        acc += jax.nn.sigmoid(gc_ref[...].astype(F32)) * _dot(yc_ref[...].astype(BF16), wc_ref[...], _NN)
        o_ref[...] = acc.astype(BF16)

    y = pl.BlockSpec((tm, 512), lambda i, j: (i, 0))
    w = pl.BlockSpec((512, tn), lambda i, j: (0, j))
    gate = lambda b: pl.BlockSpec((tm, tn), lambda i, j: (i, b * nd + j))
    return _pallas(
        body, name=name, grid=(T // tm, nd), in_specs=[y, y, y, w, w, w, gate(0), gate(1), gate(2)],
        out_specs=pl.BlockSpec((tm, tn), lambda i, j: (i, j)),
        out_shape=jax.ShapeDtypeStruct((T, D_MODEL), BF16),
        compiler_params=_params("parallel", "parallel"))(ya, yb, yc, wa, wb, wc, zg, zg, zg)


def _merge_bwd(ya, yb, yc, wa, wb, wc, zg, dm, *, name):
    T = ya.shape[0]
    tm, tn = _merge_tiles(T)
    nd = D_MODEL // tn

    def body(ya_ref, yb_ref, yc_ref, wa_ref, wb_ref, wc_ref, ga_ref, gb_ref, gc_ref, dm_ref, *outs):
        dmv = dm_ref[...].astype(F32)
        for y_ref, w_ref, g_ref, du_ref, dg_ref in zip((ya_ref, yb_ref, yc_ref), (wa_ref, wb_ref, wc_ref),
                                                       (ga_ref, gb_ref, gc_ref), outs[:3], outs[3:]):
            u = _dot(y_ref[...].astype(BF16), w_ref[...], _NN)
            sg = jax.nn.sigmoid(g_ref[...].astype(F32))
            du_ref[...] = (dmv * sg).astype(BF16)
            dg_ref[...] = (dmv * u * sg * (1.0 - sg)).astype(BF16)

    y = pl.BlockSpec((tm, 512), lambda i, j: (i, 0))
    w = pl.BlockSpec((512, tn), lambda i, j: (0, j))
    gate = lambda b: pl.BlockSpec((tm, tn), lambda i, j: (i, b * nd + j))
    t = pl.BlockSpec((tm, tn), lambda i, j: (i, j))
    return _pallas(
        body, name=name, grid=(T // tm, nd), in_specs=[y, y, y, w, w, w, gate(0), gate(1), gate(2), t],
        out_specs=[t] * 6, out_shape=[jax.ShapeDtypeStruct((T, D_MODEL), BF16)] * 6,
        compiler_params=_params("parallel", "parallel"))(ya, yb, yc, wa, wb, wc, zg, zg, zg, dm)


def _cross_fwd(q, kvm, *, name):
    T = q.shape[0]
    M = kvm.shape[0]
    tm = _tile(T, 512, 8)
    scale = X_DH ** -0.5

    def body(q_ref, k_ref, v_ref, o_ref):
        for h in range(X_HEADS):
            cs = slice(X_DH * h, X_DH * (h + 1))
            s = _dot(q_ref[:, cs], k_ref[:, cs], _NT) * scale
            e = jnp.exp(s - jnp.max(s, axis=-1, keepdims=True))
            p = e / jnp.sum(e, axis=-1, keepdims=True)
            o_ref[:, cs] = _dot(p.astype(BF16), v_ref[:, cs], _NN).astype(BF16)

    row = pl.BlockSpec((tm, D_MODEL), lambda i: (i, 0))
    return _pallas(
        body, name=name, grid=(T // tm,),
        in_specs=[row, pl.BlockSpec((M, D_MODEL), lambda i: (0, 0)), pl.BlockSpec((M, D_MODEL), lambda i: (0, 1))],
        out_specs=row, out_shape=jax.ShapeDtypeStruct((T, D_MODEL), BF16), compiler_params=_params("parallel"))(q, kvm, kvm)


def _cross_bwd(q, kvm, do, *, name):
    T = q.shape[0]
    M = kvm.shape[0]
    tm = _tile(T, 512, 8)
    scale = X_DH ** -0.5

    def body(q_ref, k_ref, v_ref, do_ref, dq_ref, dkv_ref):
        @pl.when(pl.program_id(0) == 0)
        def _():
            dkv_ref[...] = jnp.zeros(dkv_ref.shape, F32)

        for h in range(X_HEADS):
            cs = slice(X_DH * h, X_DH * (h + 1))
            vs = slice(D_MODEL + X_DH * h, D_MODEL + X_DH * (h + 1))
            qh, kh, doh = q_ref[:, cs], k_ref[:, cs], do_ref[:, cs]
            s = _dot(qh, kh, _NT) * scale
            e = jnp.exp(s - jnp.max(s, axis=-1, keepdims=True))
            p = e / jnp.sum(e, axis=-1, keepdims=True)
            dp = _dot(doh, v_ref[:, cs], _NT)
            ds = (p * (dp - jnp.sum(p * dp, axis=-1, keepdims=True)) * scale).astype(BF16)
            dq_ref[:, cs] = _dot(ds, kh, _NN).astype(BF16)
            dkv_ref[:, cs] += _dot(ds, qh, _TN)
            dkv_ref[:, vs] += _dot(p.astype(BF16), doh, _TN)

    row = pl.BlockSpec((tm, D_MODEL), lambda i: (i, 0))
    return _pallas(
        body, name=name, grid=(T // tm,),
        in_specs=[row, pl.BlockSpec((M, D_MODEL), lambda i: (0, 0)), pl.BlockSpec((M, D_MODEL), lambda i: (0, 1)), row],
        out_specs=[row, pl.BlockSpec((M, 2 * D_MODEL), lambda i: (0, 0))],
        out_shape=[jax.ShapeDtypeStruct((T, D_MODEL), BF16), jax.ShapeDtypeStruct((M, 2 * D_MODEL), F32)],
        compiler_params=_params("arbitrary"))(q, kvm, kvm, do)


def _ffn_up(h, w1, w3, *, name):
    T = h.shape[0]
    tm = _tile(T, 512, 8)
    tn = D_FF // 2

    def body(h_ref, w1_ref, w3_ref, a_ref, b_ref, act_ref):
        hv = h_ref[...]
        a = _dot(hv, w1_ref[...], _NN)
        b = _dot(hv, w3_ref[...], _NN)
        a_ref[...] = a.astype(BF16)
        b_ref[...] = b.astype(BF16)
        act_ref[...] = (a * jax.nn.sigmoid(a) * b).astype(BF16)

    w = pl.BlockSpec((D_MODEL, tn), lambda i, j: (0, j))
    t = pl.BlockSpec((tm, tn), lambda i, j: (i, j))
    return _pallas(
        body, name=name, grid=(T // tm, D_FF // tn), in_specs=[pl.BlockSpec((tm, D_MODEL), lambda i, j: (i, 0)), w, w],
        out_specs=[t, t, t],
        out_shape=[jax.ShapeDtypeStruct((T, D_FF), BF16)] * 3,
        compiler_params=_params("parallel", "parallel"))(h, w1, w3)


def _ffn_dact(dx, w2, a, b, *, name):
    T = dx.shape[0]
    tm = _tile(T, 512, 8)
    tn = D_FF // 2

    def body(dx_ref, w2_ref, a_ref, b_ref, da_ref, db_ref):
        dact = _dot(dx_ref[...].astype(BF16), w2_ref[...], _NT)
        av = a_ref[...].astype(F32)
        sg = jax.nn.sigmoid(av)
        da_ref[...] = (dact * b_ref[...].astype(F32) * (sg * (1.0 + av * (1.0 - sg)))).astype(BF16)
        db_ref[...] = (dact * (av * sg)).astype(BF16)

    t = pl.BlockSpec((tm, tn), lambda i, j: (i, j))
    return _pallas(
        body, name=name, grid=(T // tm, D_FF // tn),
        in_specs=[pl.BlockSpec((tm, D_MODEL), lambda i, j: (i, 0)), pl.BlockSpec((tn, D_MODEL), lambda i, j: (j, 0)), t, t],
        out_specs=[t, t], out_shape=[jax.ShapeDtypeStruct((T, D_FF), BF16)] * 2,
        compiler_params=_params("parallel", "parallel"))(dx, w2, a, b)


def _loss_head(x, g, target, *, name):
    T, D = x.shape
    tm = _tile(T, 512, 8)

    def body(x_ref, g_ref, t_ref, loss_ref, dx_ref, dg_ref):
        xv = x_ref[...]
        r = lax.rsqrt(jnp.mean(xv * xv, axis=-1, keepdims=True) + EPS)
        xh = xv * r
        gv = g_ref[...]
        err = xh * gv - t_ref[...]
        dy = err * (1.0 / D)
        dxh = dy * gv
        dx_ref[...] = r * (dxh - xh * jnp.mean(dxh * xh, axis=-1, keepdims=True))
        lpart = 0.5 * jnp.sum(jnp.mean(err * err, axis=-1, keepdims=True), axis=0, keepdims=True)
        gpart = jnp.sum(dy * xh, axis=0, keepdims=True)

        @pl.when(pl.program_id(0) == 0)
        def _():
            loss_ref[...] = jnp.broadcast_to(lpart, (1, LANES))
            dg_ref[...] = gpart

        @pl.when(pl.program_id(0) > 0)
        def _():
            loss_ref[...] += jnp.broadcast_to(lpart, (1, LANES))
            dg_ref[...] += gpart

    row = pl.BlockSpec((tm, D), lambda i: (i, 0))
    vec = pl.BlockSpec((1, D), lambda i: (0, 0))
    return _pallas(
        body, name=name, grid=(T // tm,), in_specs=[row, vec, row],
        out_specs=[pl.BlockSpec((1, LANES), lambda i: (0, 0)), row, vec],
        out_shape=[jax.ShapeDtypeStruct((1, LANES), F32), jax.ShapeDtypeStruct((T, D), F32), jax.ShapeDtypeStruct((1, D), F32)],
        compiler_params=_params("arbitrary"))(x, g.reshape(1, D), target)


IN_CQ, IN_CKV, IN_KR, IN_B, IN_C, IN_G, IN_END = 0, 384, 640, 672, 4768, 5536, 8608
WEIGHT_NAMES = ("w_in", "g_mix", "a_gq", "a_gkv", "a_wuq", "a_wukv", "b_lb", "b_gout", "c_sink", "rel_bias",
                "w_br_a", "w_br_b", "w_br_c", "w_out", "g_x", "g_mem", "x_wq", "x_wkv", "x_wo", "g_ffn",
                "f_w1", "f_w3", "f_w2", "g_final")


def _lower_bounds(b_lb):
    sm = jax.nn.softmax(b_lb.astype(F32), axis=1)
    return jnp.cumsum(sm, axis=1) - sm[:, :1]


def _layer_weights(w, l):
    bf = lambda a: a.astype(BF16)
    w_in = bf(w["w_in"][l])
    kr = w_in[:, IN_KR:IN_B]
    wa = jnp.concatenate([w_in[:, IN_CQ:IN_CKV], kr, _rope_swap_cols(kr), jnp.zeros((D_MODEL, 64), BF16),
                          w_in[:, IN_CKV:IN_KR]], axis=1)
    wuq = bf(w["a_wuq"][l]).reshape(A_Q_RANK, A_HEADS, A_QK)
    zeros = lambda n: jnp.zeros((A_Q_RANK, A_HEADS, n), BF16)
    wuq_pad = jnp.concatenate([wuq, zeros(A_PAD - A_QK)], axis=-1)
    wuq_sw = jnp.concatenate([zeros(A_NOPE), _rope_swap_cols(wuq[..., A_NOPE:]), zeros(A_PAD - A_QK)], axis=-1)
    wq2 = jnp.concatenate([wuq_pad.reshape(A_Q_RANK, -1), wuq_sw.reshape(A_Q_RANK, -1)], axis=1)
    wukv = bf(w["a_wukv"][l]).reshape(A_KV_RANK, A_HEADS, A_NOPE + A_V)
    wkv = jnp.concatenate([wukv[..., :A_NOPE].reshape(A_KV_RANK, -1), wukv[..., A_NOPE:].reshape(A_KV_RANK, -1)], axis=1)
    return dict(wa=wa, wb=w_in[:, IN_B:IN_C], wc=w_in[:, IN_C:IN_G], wg=w_in[:, IN_G:IN_END], wq2=wq2, wkv=wkv,
                w_br_a=bf(w["w_br_a"][l]), w_br_b=bf(w["w_br_b"][l]), w_br_c=bf(w["w_br_c"][l]), w_out=bf(w["w_out"][l]),
                x_wq=bf(w["x_wq"][l]), x_wkv=bf(w["x_wkv"][l]), x_wo=bf(w["x_wo"][l]),
                f_w1=bf(w["f_w1"][l]), f_w3=bf(w["f_w3"][l]), f_w2=bf(w["f_w2"][l]))


def _layer_fwd(l, x, mem, w, lw, lower, bias, tabs):
    n = lambda s: f"l{l}_{s}"
    cq_t, sq_t, ck, _ = tabs
    s = dict(x=x)
    s["h0"] = h0 = _rms(x, w["g_mix"][l], name=n("rms_mix"))
    s["za"] = za = _mm(h0, lw["wa"], name=n("in_a"))
    s["zb"] = zb = _mm(h0, lw["wb"], name=n("in_b"))
    s["zc"] = zc = _mm(h0, lw["wc"], out_dtype=BF16, name=n("in_c"))
    s["zg"] = zg = _mm(h0, lw["wg"], out_dtype=BF16, name=n("in_g"))
    s["cqn"] = cqn = _rms(za, w["a_gq"][l], col=0, width=A_Q_RANK, name=n("rms_cq"))
    s["ckvn"] = ckvn = _rms(za, w["a_gkv"][l], col=512, width=A_KV_RANK, name=n("rms_ckv"))
    s["q"], s["qt"] = q, _ = _qrope(_mm(cqn, lw["wq2"], name=n("uq")), cq_t, sq_t, name=n("qrope"))
    s["kv"] = kv = _mm(ckvn, lw["wkv"], out_dtype=BF16, name=n("ukv"))
    s["k"], vxt = k, _ = _kprep(kv, za, ck, name=n("kprep"))
    s["ya"], s["lse"] = ya, _ = _flash_fwd(q, k, vxt, name=n("mla"))
    lb_f, lb_b = lower[0, l].reshape(1, -1), lower[1, l].reshape(1, -1)
    s["of"], s["stf"] = of, _ = _hgrn_fwd(zb, lb_f, reverse=False, name=n("hgrn_f"))
    s["ob"], s["stb"] = ob, _ = _hgrn_fwd(zb, lb_b, reverse=True, name=n("hgrn_b"))
    s["yb"] = yb = _hgrn_out(of, ob, zb, w["b_gout"][l], name=n("hgrn_out"))
    s["yc"] = yc = _swa_fwd(zc, bias, w["c_sink"][l], name=n("swa"))
    s["merged"] = merged = _merge_fwd(ya, yb, yc, lw["w_br_a"], lw["w_br_b"], lw["w_br_c"], zg, name=n("merge"))
    s["x1"] = x1 = _mm(merged, lw["w_out"], add=x, name=n("out"))
    s["h1"] = h1 = _rms(x1, w["g_x"][l], name=n("rms_x"))
    s["qx"] = qx = _mm(h1, lw["x_wq"], out_dtype=BF16, name=n("xq"))
    s["memn"] = memn = _rms(mem, w["g_mem"][l], name=n("rms_mem"))
    s["kvm"] = kvm = _mm(memn, lw["x_wkv"], out_dtype=BF16, name=n("xkv"))
    s["ox"] = ox = _cross_fwd(qx, kvm, name=n("cross"))
    s["x2"] = x2 = _mm(ox, lw["x_wo"], add=x1, name=n("xo"))
    s["h2"] = h2 = _rms(x2, w["g_ffn"][l], name=n("rms_ffn"))
    s["a"], s["b"], s["act"] = a, b, act = _ffn_up(h2, lw["f_w1"], lw["f_w3"], name=n("ffn_up"))
    x3 = _mm(act, lw["f_w2"], add=x2, name=n("ffn_down"))
    return x3, s


def _layer_bwd(l, dx3, mem, w, lw, lower, bias, tabs, s):
    n = lambda t: f"l{l}_b_{t}"
    cq_t, sq_t, _, ck_t = tabs
    g = {}
    da, db = _ffn_dact(dx3, lw["f_w2"], s["a"], s["b"], name=n("ffn_dact"))
    g["f_w2"] = _mm(s["act"], dx3, mode="tn", name=n("dw2"))
    dh2 = _mm(db, lw["f_w3"], mode="nt", add=_mm(da, lw["f_w1"], mode="nt", name=n("dh2a")), name=n("dh2b"))
    g["f_w1"] = _mm(s["h2"], da, mode="tn", name=n("dw1"))
    g["f_w3"] = _mm(s["h2"], db, mode="tn", name=n("dw3"))
    dx2, g["g_ffn"] = _rms_bwd(s["x2"], w["g_ffn"][l], dh2, res=dx3, name=n("rms_ffn"))
    dox = _mm(dx2, lw["x_wo"], mode="nt", out_dtype=BF16, name=n("dox"))
    g["x_wo"] = _mm(s["ox"], dx2, mode="tn", name=n("dwo"))
    dqx, dkvm = _cross_bwd(s["qx"], s["kvm"], dox, name=n("cross"))
    g["x_wq"] = _mm(s["h1"], dqx, mode="tn", name=n("dwq"))
    dh1 = _mm(dqx, lw["x_wq"], mode="nt", name=n("dh1"))
    g["x_wkv"] = _mm(s["memn"], dkvm, mode="tn", name=n("dwkv"))
    dmemn = _mm(dkvm, lw["x_wkv"], mode="nt", name=n("dmemn"))
    _, g["g_mem"] = _rms_bwd(mem, w["g_mem"][l], dmemn, name=n("rms_mem"))
    dx1, g["g_x"] = _rms_bwd(s["x1"], w["g_x"][l], dh1, res=dx2, name=n("rms_x"))
    dmerged = _mm(dx1, lw["w_out"], mode="nt", name=n("dmerged"))
    g["w_out"] = _mm(s["merged"], dx1, mode="tn", name=n("dwout"))
    dua, dub, duc, dga, dgb, dgc = _merge_bwd(s["ya"], s["yb"], s["yc"], lw["w_br_a"], lw["w_br_b"], lw["w_br_c"],
                                              s["zg"], dmerged, name=n("merge"))
    dya = _mm(dua, lw["w_br_a"], mode="nt", out_dtype=BF16, name=n("dya"))
    dyb = _mm(dub, lw["w_br_b"], mode="nt", name=n("dyb"))
    dyc = _mm(duc, lw["w_br_c"], mode="nt", out_dtype=BF16, name=n("dyc"))
    g["w_br_a"] = _mm(s["ya"], dua, mode="tn", name=n("dwbra"))
    g["w_br_b"] = _mm(s["yb"], dub, mode="tn", name=n("dwbrb"))
    g["w_br_c"] = _mm(s["yc"], duc, mode="tn", name=n("dwbrc"))
    dzc, dbias, dsink = _swa_bwd(s["zc"], bias, w["c_sink"][l], dyc, name=n("swa"))
    g["c_sink"] = dsink[:, 0]
    g["bias"] = dbias
    lb_f, lb_b = lower[0, l].reshape(1, -1), lower[1, l].reshape(1, -1)
    do_, dgate, dgout = _hgrn_out_bwd(s["of"], s["ob"], s["zb"], w["b_gout"][l], dyb, name=n("hgrn_out"))
    g["b_gout"] = dgout[0]
    dq_f, dzf, dv_f, dlb_f = _hgrn_bwd(s["zb"], lb_f, do_, s["stf"], reverse=False, name=n("hgrn_f"))
    dq_b, dzr, dv_b, dlb_b = _hgrn_bwd(s["zb"], lb_b, do_, s["stb"], reverse=True, name=n("hgrn_b"))
    g["lower"] = jnp.concatenate([dlb_f, dlb_b], axis=0)
    dzb = _dzb_assemble(dq_f, dq_b, dzf, dzr, dv_f, dv_b, dgate, name=n("dzb"))
    delta = _attn_delta(s["ya"], dya, name=n("mla_delta"))
    dq_part, dkt, dvt = _flash_bwd(s["q"], s["qt"], s["k"], s["kv"], dya, dya.T, s["lse"].reshape(A_HEADS, -1, 1), delta, name=n("mla"))
    dq2 = _dq_sum(dq_part, cq_t, sq_t, name=n("mla_dq"))
    dcqn = _mm(dq2, lw["wq2"], mode="nt", name=n("dcqn"))
    dwq2 = _mm(s["cqn"], dq2, mode="tn", name=n("dwq2")).reshape(A_Q_RANK, 2, A_HEADS, A_PAD)
    dknt, dzkrt = _kprep_bwd(dkt, ck_t, name=n("kprep"))
    wkv = lw["wkv"]
    dckvn = _mm(dvt, wkv[:, 512:].T, mode="tn", add=_mm(dknt, wkv[:, :512].T, mode="tn", name=n("dckvn_k")), name=n("dckvn_v"))
    dwkn = _mm(dknt, s["ckvn"], name=n("dwkn")).T
    dwv = _mm(dvt, s["ckvn"], name=n("dwv")).T
    dzcq, dgq = _rms_bwd(s["za"], w["a_gq"][l], dcqn, col=0, width=A_Q_RANK, out_dtype=BF16, name=n("rms_cq"))
    dzckv, dgkv = _rms_bwd(s["za"], w["a_gkv"][l], dckvn, col=512, width=A_KV_RANK, out_dtype=BF16, name=n("rms_ckv"))
    g["a_gq"], g["a_gkv"] = dgq[0], dgkv[0]
    sw = jnp.concatenate([jnp.zeros((A_Q_RANK, A_HEADS, A_NOPE), F32), _rope_unswap_cols(dwq2[:, 1, :, A_NOPE:A_QK])], axis=-1)
    g["a_wuq"] = (dwq2[:, 0, :, :A_QK] + sw).reshape(A_Q_RANK, -1)
    g["a_wukv"] = jnp.concatenate([dwkn.reshape(A_KV_RANK, A_HEADS, A_NOPE), dwv.reshape(A_KV_RANK, A_HEADS, A_V)], axis=-1).reshape(A_KV_RANK, -1)
    wa = lw["wa"]
    pieces = [(dzcq, wa[:, 0:384]), (dzckv, wa[:, 512:768]), (dzb, lw["wb"]), (dzc, lw["wc"]),
              (dga, lw["wg"][:, 0:1024]), (dgb, lw["wg"][:, 1024:2048]), (dgc, lw["wg"][:, 2048:3072])]
    dh0 = _mm(dzkrt, wa[:, 384:512].T, mode="tn", name=n("dh0_kr"))
    dwkr = _mm(dzkrt, s["h0"], name=n("dwin_kr")).T
    dwkr = dwkr[:, 0:A_ROPE] + _rope_unswap_cols(dwkr[:, A_ROPE:2 * A_ROPE])
    dws = []
    for i, (dz, wp) in enumerate(pieces):
        dh0 = _mm(dz, wp, mode="nt", add=dh0, name=n(f"dh0_{i}"))
        dws.append(_mm(s["h0"], dz, mode="tn", name=n(f"dwin_{i}")))
    g["w_in"] = jnp.concatenate([dws[0], dws[1], dwkr] + dws[2:], axis=1)
    dx, g["g_mix"] = _rms_bwd(s["x"], w["g_mix"][l], dh0, res=dx1, name=n("rms_mix"))
    return dx, g


def _local_step(x, mem, target, w):
    T = x.shape[0]
    tabs = _rope_tables(T)
    lower, lower_vjp = jax.vjp(_lower_bounds, w["b_lb"])
    buckets = _swa_buckets()
    onehot = (buckets.reshape(-1)[:, None] == jnp.arange(REL_BUCKETS)[None, :]).astype(F32)
    bias = jnp.dot(w["rel_bias"].astype(F32).T, onehot.T, precision=lax.Precision.HIGHEST).reshape(C_HEADS, C_BLOCK, C_SPAN)
    lws, saved = [], []
    for l in range(DEPTH):
        lws.append(_layer_weights(w, l))
        x, s = _layer_fwd(l, x, mem, w, lws[l], lower, bias, tabs)
        saved.append(s)
    loss, dx, dg_final = _loss_head(x, w["g_final"], target, name="loss_head")
    layer_grads = [None] * DEPTH
    for l in reversed(range(DEPTH)):
        dx, layer_grads[l] = _layer_bwd(l, dx, mem, w, lws[l], lower, bias, tabs, saved[l])
        saved[l] = None
    grads = {}
    for name in WEIGHT_NAMES:
        if name in layer_grads[0]:
            per_layer = [layer_grads[l][name].reshape(w[name].shape[1:]) for l in range(DEPTH)]
            grads[name] = per_layer if per_layer[0].ndim == 2 else jnp.stack(per_layer)
    grads["g_final"] = dg_final[0]
    dlower = jnp.stack([layer_grads[l]["lower"] for l in range(DEPTH)], axis=1)
    grads["b_lb"] = lower_vjp(dlower)[0]
    dbias = layer_grads[0]["bias"] + layer_grads[1]["bias"]
    grads["rel_bias"] = jnp.dot(onehot.T, dbias.reshape(C_HEADS, -1).T, precision=lax.Precision.HIGHEST)
    return loss, dx, grads


N_CHIPS = 4
PACK_COLS = 1024
PACK_ALIGN = 32 * PACK_COLS
SHARDED = (("w_in", 2), ("a_wuq", 2), ("a_wukv", 2), ("b_lb", 2), ("w_br_a", 2), ("w_br_b", 2), ("w_br_c", 2), ("w_out", 1),
           ("x_wq", 1), ("x_wkv", 2), ("x_wo", 1), ("f_w1", 2), ("f_w3", 2), ("f_w2", 1))
REPLICATED = ("g_mix", "a_gq", "a_gkv", "b_gout", "c_sink", "rel_bias", "g_x", "g_mem", "g_ffn", "g_final")
MESH_IDS = pl.DeviceIdType.MESH
ANY_SPEC = pl.BlockSpec(memory_space=pl.ANY)


def _pack_pieces(arrs, cols, align):
    pieces = [a.reshape(-1, cols) for a in arrs]
    pad = (-sum(p.size for p in pieces)) % align
    return pieces + ([jnp.zeros((pad // cols, cols), pieces[0].dtype)] if pad else [])


def _pack(arrs, cols, align):
    return jnp.concatenate(_pack_pieces(arrs, cols, align), axis=0)


def _pack_small(arrs):
    flat = jnp.concatenate([a.reshape(-1) for a in arrs])
    return jnp.pad(flat, (0, (-flat.shape[0]) % (8 * LANES))).reshape(-1, LANES)


def _unpack(buf, shapes):
    cols = buf.shape[-1]
    buf = buf.reshape(-1, cols)
    by_rows = all(math.prod(shp) % cols == 0 for shp in shapes)
    flat = None if by_rows else buf.reshape(-1)
    out, start = [], 0
    for shp in shapes:
        size = math.prod(shp)
        piece = buf[start // cols:(start + size) // cols] if by_rows else flat[start:start + size]
        out.append(piece.reshape(shp))
        start += size
    return out


def _chip_peers():
    x, y, c = lax.axis_index("x"), lax.axis_index("y"), lax.axis_index("c")
    return x, y, c, [(1 - x, y), (x, 1 - y), (1 - x, 1 - y)]


def _chip_gather(src, chip, *, name):
    _, R, C = src.shape

    def body(src_ref, out_ref, send_sems, recv_sems, pass_send_sems, pass_recv_sems):
        x, y, c, chips = _chip_peers()
        me = 2 * x + y
        sibling = (x, y, 1 - c)

        def over_ici(j, slot):
            px, py = chips[j]
            return pltpu.make_async_remote_copy(src_ref=src_ref.at[c], dst_ref=out_ref.at[slot, c], send_sem=send_sems.at[j],
                                                recv_sem=recv_sems.at[j], device_id=(px, py, c), device_id_type=MESH_IDS)

        def pass_on(j, half):
            px, py = chips[j]
            piece = out_ref.at[2 * px + py, half]
            return pltpu.make_async_remote_copy(src_ref=piece, dst_ref=piece, send_sem=pass_send_sems.at[j],
                                                recv_sem=pass_recv_sems.at[j], device_id=sibling, device_id_type=MESH_IDS)

        sends = [over_ici(j, me) for j in range(3)]
        for cp in sends:
            cp.start()
        passed = []
        for j, (px, py) in enumerate(chips):
            over_ici(j, 2 * px + py).wait_recv()
            passed.append(pass_on(j, c))
            passed[j].start()
        for j in range(3):
            pass_on(j, 1 - c).wait_recv()
        for cp in sends + passed:
            cp.wait_send()

    gathered = _pallas(
        body, name=name, in_specs=[ANY_SPEC], out_specs=ANY_SPEC, out_shape=jax.ShapeDtypeStruct((N_CHIPS, 2, R, C), src.dtype),
        scratch_shapes=[pltpu.SemaphoreType.DMA((3,))] * 4,
        compiler_params=pltpu.CompilerParams(has_side_effects=True))(src)

    tr = _tile(R, 512, 16)

    def place(chip_ref, gathered_ref, own_ref, out_ref):
        out_ref[0, 0] = own_ref[0]

    grid_spec = pltpu.PrefetchScalarGridSpec(
        num_scalar_prefetch=1, grid=(2, R // tr),
        in_specs=[ANY_SPEC, pl.BlockSpec((1, tr, C), lambda h, i, chip_ref: (h, i, 0))],
        out_specs=pl.BlockSpec((1, 1, tr, C), lambda h, i, chip_ref: (chip_ref[0], h, i, 0)))
    return _pallas(place, name=name + "_own", grid_spec=grid_spec, out_shape=jax.ShapeDtypeStruct(gathered.shape, gathered.dtype),
                   input_output_aliases={1: 0}, compiler_params=_params("arbitrary", "arbitrary"))(chip, gathered, src)


def _chip_scatter(src, *, name):
    _, R, C = src.shape

    def body(src_ref, out_ref, send_sems, recv_sems):
        x, y, c, chips = _chip_peers()
        me = 2 * x + y

        def copy(j, seg):
            px, py = chips[j]
            return pltpu.make_async_remote_copy(src_ref=src_ref.at[seg], dst_ref=out_ref.at[j], send_sem=send_sems.at[j],
                                                recv_sem=recv_sems.at[j], device_id=(px, py, c), device_id_type=MESH_IDS)

        sends = [copy(j, 2 * px + py) for j, (px, py) in enumerate(chips)]
        for cp in sends:
            cp.start()
        for j in range(3):
            copy(j, me).wait_recv()
        for cp in sends:
            cp.wait_send()

    return _pallas(
        body, name=name, in_specs=[ANY_SPEC], out_specs=ANY_SPEC, out_shape=jax.ShapeDtypeStruct((3, R, C), src.dtype),
        scratch_shapes=[pltpu.SemaphoreType.DMA((3,)), pltpu.SemaphoreType.DMA((3,))],
        compiler_params=pltpu.CompilerParams(has_side_effects=True))(src)


PAIR_CHUNKS = 4


def _pair_swap(src, *, halves, name):
    R, C = src.shape[-2:]
    n = N_CHIPS if halves else 1
    rc = R // PAIR_CHUNKS
    assert rc * PAIR_CHUNKS == R and rc % 16 == 0, R

    def body(src_ref, out_ref, send_sems, recv_sems):
        x, y, c = lax.axis_index("x"), lax.axis_index("y"), lax.axis_index("c")
        copies = []
        for k in range(n):
            for r in range(PAIR_CHUNKS):
                rows = pl.ds(r * rc, rc)
                s = src_ref.at[k, 1 - c, rows] if halves else src_ref.at[rows]
                d = out_ref.at[k, rows] if halves else out_ref.at[rows]
                i = k * PAIR_CHUNKS + r
                copies.append(pltpu.make_async_remote_copy(src_ref=s, dst_ref=d, send_sem=send_sems.at[i], recv_sem=recv_sems.at[i],
                                                           device_id=(x, y, 1 - c), device_id_type=MESH_IDS))
        for cp in copies:
            cp.start()
        for cp in copies:
            cp.wait_recv()
        for cp in copies:
            cp.wait_send()

    shape = (N_CHIPS, R, C) if halves else (R, C)
    return _pallas(
        body, name=name, in_specs=[ANY_SPEC], out_specs=ANY_SPEC, out_shape=jax.ShapeDtypeStruct(shape, src.dtype),
        scratch_shapes=[pltpu.SemaphoreType.DMA((n * PAIR_CHUNKS,)), pltpu.SemaphoreType.DMA((n * PAIR_CHUNKS,))],
        compiler_params=pltpu.CompilerParams(has_side_effects=True))(src)


def _pair_add(g4, got, c, *, name):
    _, _, R, C = g4.shape
    tr = _tile(R, 512, 16)

    def body(c_ref, mine_ref, got_ref, o_ref, ob_ref):
        s = mine_ref[0, 0] + got_ref[0].astype(F32)
        o_ref[0] = s
        ob_ref[0] = s.astype(BF16)

    blk = pl.BlockSpec((1, tr, C), lambda k, i, c_ref: (k, i, 0))
    grid_spec = pltpu.PrefetchScalarGridSpec(
        num_scalar_prefetch=1, grid=(N_CHIPS, R // tr),
        in_specs=[pl.BlockSpec((1, 1, tr, C), lambda k, i, c_ref: (k, c_ref[0], i, 0)), blk], out_specs=[blk, blk])
    return _pallas(body, name=name, grid_spec=grid_spec,
                   out_shape=[jax.ShapeDtypeStruct((N_CHIPS, R, C), F32), jax.ShapeDtypeStruct((N_CHIPS, R, C), BF16)],
                   compiler_params=_params("parallel", "parallel"))(c, g4, got)


def _chip_sum(pair_sum, landed, me, *, name):
    _, R, C = pair_sum.shape
    tr = _tile(R, 512, 16)

    def body(me_ref, own_ref, landed_ref, o_ref):
        acc = own_ref[0]
        for j in range(3):
            acc = acc + landed_ref[j].astype(F32)
        o_ref[...] = acc

    grid_spec = pltpu.PrefetchScalarGridSpec(
        num_scalar_prefetch=1, grid=(R // tr,),
        in_specs=[pl.BlockSpec((1, tr, C), lambda i, me_ref: (me_ref[0], i, 0)), pl.BlockSpec((3, tr, C), lambda i, me_ref: (0, i, 0))],
        out_specs=pl.BlockSpec((tr, C), lambda i, me_ref: (i, 0)))
    return _pallas(body, name=name, grid_spec=grid_spec, out_shape=jax.ShapeDtypeStruct((R, C), F32),
                   compiler_params=_params("parallel"))(me, pair_sum, landed)


def _join_halves(mine, got, c, *, name):
    R, C = mine.shape
    tr = _tile(R, 512, 16)

    def body(c_ref, mine_ref, got_ref, o_ref):
        use_mine = pl.program_id(0) == c_ref[0]
        o_ref[0] = jnp.where(use_mine, mine_ref[...], got_ref[...])

    blk = pl.BlockSpec((tr, C), lambda h, i, c_ref: (i, 0))
    grid_spec = pltpu.PrefetchScalarGridSpec(num_scalar_prefetch=1, grid=(2, R // tr), in_specs=[blk, blk],
                                             out_specs=pl.BlockSpec((1, tr, C), lambda h, i, c_ref: (h, i, 0)))
    return _pallas(body, name=name, grid_spec=grid_spec, out_shape=jax.ShapeDtypeStruct((2, R, C), mine.dtype),
                   compiler_params=_params("parallel", "parallel"))(c, mine, got).reshape(2 * R, C)


def _gather8(s, *, name):
    R, C = s.shape

    def body(s_ref, out_ref, send_sems, recv_sems):
        x, y, c = lax.axis_index("x"), lax.axis_index("y"), lax.axis_index("c")
        me = 4 * x + 2 * y + c
        flips = [(dx, dy, dc) for dx in (0, 1) for dy in (0, 1) for dc in (0, 1)][1:]
        out_ref[me] = s_ref[...]

        def copy(j, slot):
            dx, dy, dc = flips[j]
            return pltpu.make_async_remote_copy(src_ref=s_ref, dst_ref=out_ref.at[slot], send_sem=send_sems.at[j],
                                                recv_sem=recv_sems.at[j], device_id=(x ^ dx, y ^ dy, c ^ dc), device_id_type=MESH_IDS)

        sends = [copy(j, me) for j in range(7)]
        for cp in sends:
            cp.start()
        for j, (dx, dy, dc) in enumerate(flips):
            copy(j, 4 * (x ^ dx) + 2 * (y ^ dy) + (c ^ dc)).wait_recv()
        for cp in sends:
            cp.wait_send()

    vmem = pl.BlockSpec(memory_space=pltpu.VMEM)
    return _pallas(
        body, name=name, in_specs=[vmem], out_specs=vmem, out_shape=jax.ShapeDtypeStruct((8, R, C), s.dtype),
        scratch_shapes=[pltpu.SemaphoreType.DMA((7,)), pltpu.SemaphoreType.DMA((7,))],
        compiler_params=pltpu.CompilerParams(has_side_effects=True))(s)


def _sum_slots(a, *, name):
    n, R, C = a.shape
    tr = _tile(R, 512, 8)

    def body(a_ref, o_ref):
        acc = a_ref[0]
        for k in range(1, n):
            acc = acc + a_ref[k]
        o_ref[...] = acc

    return _pallas(body, name=name, grid=(R // tr,), in_specs=[pl.BlockSpec((n, tr, C), lambda i: (0, i, 0))],
                   out_specs=pl.BlockSpec((tr, C), lambda i: (i, 0)), out_shape=jax.ShapeDtypeStruct((R, C), a.dtype),
                   compiler_params=_params("parallel"))(a)


def _adamw(w, g, m, v, *, name):
    R, C = w.shape
    tr = _tile(R, max(8, (1 << 18) // C // 8 * 8), 8)
    c1 = 1.0 / (1.0 - ADAM_B1 ** ADAM_STEP)
    c2 = 1.0 / (1.0 - ADAM_B2 ** ADAM_STEP)

    def body(w_ref, g_ref, m_ref, v_ref, d_ref, nm_ref, nv_ref):
        gv = g_ref[...]
        nm = ADAM_B1 * m_ref[...] + (1.0 - ADAM_B1) * gv
        nv = ADAM_B2 * v_ref[...] + (1.0 - ADAM_B2) * (gv * gv)
        d_ref[...] = -ADAM_LR * ((nm * c1) / (jnp.sqrt(nv * c2) + ADAM_EPS) + ADAM_WD * w_ref[...])
        nm_ref[...] = nm
        nv_ref[...] = nv

    blk = pl.BlockSpec((tr, C), lambda i: (i, 0))
    shape = jax.ShapeDtypeStruct((R, C), F32)
    return _pallas(body, name=name, grid=(R // tr,), in_specs=[blk] * 4, out_specs=[blk] * 3, out_shape=[shape] * 3,
                   compiler_params=_params("parallel"))(w, g, m, v)


def kernel(x, mem, w_in, g_mix, a_gq, a_gkv, a_wuq, a_wukv, b_lb, b_gout, c_sink, rel_bias, w_br_a, w_br_b, w_br_c, w_out, g_x, g_mem, x_wq, x_wkv, x_wo, g_ffn, f_w1, f_w3, f_w2, g_final, loss_target, m_w_in, m_g_mix, m_a_gq, m_a_gkv, m_a_wuq, m_a_wukv, m_b_lb, m_b_gout, m_c_sink, m_rel_bias, m_w_br_a, m_w_br_b, m_w_br_c, m_w_out, m_g_x, m_g_mem, m_x_wq, m_x_wkv, m_x_wo, m_g_ffn, m_f_w1, m_f_w3, m_f_w2, m_g_final, v_w_in, v_g_mix, v_a_gq, v_a_gkv, v_a_wuq, v_a_wukv, v_b_lb, v_b_gout, v_c_sink, v_rel_bias, v_w_br_a, v_w_br_b, v_w_br_c, v_w_out, v_g_x, v_g_mem, v_x_wq, v_x_wkv, v_x_wo, v_g_ffn, v_f_w1, v_f_w3, v_f_w2, v_g_final):
    ws = dict(zip(WEIGHT_NAMES, (w_in, g_mix, a_gq, a_gkv, a_wuq, a_wukv, b_lb, b_gout, c_sink, rel_bias, w_br_a, w_br_b, w_br_c,
                                 w_out, g_x, g_mem, x_wq, x_wkv, x_wo, g_ffn, f_w1, f_w3, f_w2, g_final)))
    ms = dict(zip(WEIGHT_NAMES, (m_w_in, m_g_mix, m_a_gq, m_a_gkv, m_a_wuq, m_a_wukv, m_b_lb, m_b_gout, m_c_sink, m_rel_bias,
                                 m_w_br_a, m_w_br_b, m_w_br_c, m_w_out, m_g_x, m_g_mem, m_x_wq, m_x_wkv, m_x_wo, m_g_ffn,
                                 m_f_w1, m_f_w3, m_f_w2, m_g_final)))
    vs = dict(zip(WEIGHT_NAMES, (v_w_in, v_g_mix, v_a_gq, v_a_gkv, v_a_wuq, v_a_wukv, v_b_lb, v_b_gout, v_c_sink, v_rel_bias,
                                 v_w_br_a, v_w_br_b, v_w_br_c, v_w_out, v_g_x, v_g_mem, v_x_wq, v_x_wkv, v_x_wo, v_g_ffn,
                                 v_f_w1, v_f_w3, v_f_w2, v_g_final)))
    sharded = [n for n, _ in SHARDED]
    axis_of = dict(SHARDED)

    def wire(n):
        return lax.bitcast_convert_type(ws[n], BF16) if n == "b_lb" else ws[n].astype(BF16)

    core = lax.axis_index("c").astype(jnp.int32).reshape(1)
    chip = (2 * lax.axis_index("x") + lax.axis_index("y")).astype(jnp.int32).reshape(1)
    wire_shapes = [wire(n).shape for n in sharded]
    packed = _pack([wire(n) for n in sharded], PACK_COLS, PACK_ALIGN)
    gathered = _chip_gather(packed.reshape(2, packed.shape[0] // 2, PACK_COLS), chip, name="gather_weights")
    per_chip = [_unpack(gathered[k], wire_shapes) for k in range(N_CHIPS)]
    full = dict(ws)
    for i, n in enumerate(sharded):
        parts = [per_chip[k][i] for k in range(N_CHIPS)]
        if n == "b_lb":
            parts = [lax.bitcast_convert_type(p, F32) for p in parts]
        full[n] = jnp.concatenate(parts, axis=axis_of[n])

    loss, grad_x, grads = _local_step(x[0], mem[0], loss_target[0], full)

    def shard_of(n, k):
        g = grads[n]
        if isinstance(g, list):
            return [jnp.split(gl, N_CHIPS, axis=axis_of[n] - 1)[k] for gl in g]
        return [jnp.split(g, N_CHIPS, axis=axis_of[n])[k]]

    pieces = []
    for k in range(N_CHIPS):
        pieces += _pack_pieces([p for n in sharded for p in shard_of(n, k)], PACK_COLS, PACK_ALIGN)
    g4 = jnp.concatenate(pieces, axis=0).reshape(N_CHIPS, 2, -1, PACK_COLS)
    got = _pair_swap(g4, halves=True, name="reduce_pair_swap")
    pair_sum, pair_sum_wire = _pair_add(g4, got, core, name="reduce_pair_add")
    landed = _chip_scatter(pair_sum_wire, name="reduce_chip_scatter")
    mine = _chip_sum(pair_sum, landed, chip, name="reduce_chip_sum")
    g_shard = _join_halves(mine, _pair_swap(mine, halves=False, name="reduce_pair_join"), core, name="reduce_join_halves")

    small = _pack_small([grads[n] for n in REPLICATED] + [loss[0, 0:1]])
    small_sum = _sum_slots(_gather8(small, name="gather_small"), name="sum_small")
    small_grads = _unpack(small_sum, [ws[n].shape for n in REPLICATED] + [(1,)])
    loss_total = small_grads.pop()[0]

    shard_shapes = [ws[n].shape for n in sharded]
    out = {}
    for n, gr in zip(sharded, _unpack(g_shard, shard_shapes)):
        flat2 = lambda a: a.reshape(-1, a.shape[-1])
        d, nm, nv = _adamw(flat2(ws[n]), flat2(gr), flat2(ms[n]), flat2(vs[n]), name="adamw_" + n)
        out[n] = (gr, d.reshape(gr.shape), nm.reshape(gr.shape), nv.reshape(gr.shape))
    pk_s = lambda d: _pack_small([d[n] for n in REPLICATED])
    rep_shapes = [ws[n].shape for n in REPLICATED]
    gs_flat = _pack_small(small_grads)
    ds_flat, ms_flat, vs_flat = _adamw(pk_s(ws), gs_flat, pk_s(ms), pk_s(vs), name="adamw_replicated")
    for n, gr, d, nm, nv in zip(REPLICATED, small_grads, _unpack(ds_flat, rep_shapes), _unpack(ms_flat, rep_shapes),
                                _unpack(vs_flat, rep_shapes)):
        out[n] = (gr, d, nm, nv)
    return (loss_total, grad_x[None], *[out[n][0] for n in WEIGHT_NAMES], *[out[n][1] for n in WEIGHT_NAMES],
            *[out[n][2] for n in WEIGHT_NAMES], *[out[n][3] for n in WEIGHT_NAMES])
```

```python
import math

import jax
import jax.numpy as jnp
from jax import lax
from jax.experimental import pallas as pl
from jax.experimental.pallas import tpu as pltpu

F32 = jnp.float32
BF16 = jnp.bfloat16

D_MODEL = 1024
DEPTH = 2
EPS = 1e-6
MASK_VALUE = -1e30
TINY = 1e-30
A_HEADS, A_NOPE, A_ROPE, A_V = 8, 64, 32, 64
A_QK = A_NOPE + A_ROPE
A_Q_RANK, A_KV_RANK = 384, 256
ROPE_THETA = 10000.0
B_HEADS, B_DK, B_DV, B_CHUNK = 8, 128, 64, 16
C_HEADS, C_KV_HEADS, C_DH, C_WINDOW, C_BLOCK = 8, 2, 64, 128, 128
REL_BUCKETS, REL_MAX_DIST = 32, 128
X_HEADS, X_DH = 4, 256
D_FF = 2816
ADAM_LR, ADAM_B1, ADAM_B2, ADAM_EPS, ADAM_WD, ADAM_STEP = 0.001, 0.9, 0.999, 1e-08, 0.01, 10

LANES = 128
VMEM_LIMIT = 56 * 1024 * 1024


def _pallas(body, **kw):
    return pl.pallas_call(body, **kw)


def _params(*sem):
    return pltpu.CompilerParams(dimension_semantics=sem, vmem_limit_bytes=VMEM_LIMIT)


def _tile(n, pref, unit=LANES):
    if n <= pref:
        return n
    t = (pref // unit) * unit
    while t > unit and n % t:
        t -= unit
    assert n % t == 0, (n, pref, unit)
    return t


def _dot(a, b, dims):
    return lax.dot_general(a, b, (dims, ((), ())), preferred_element_type=F32)


_NN = ((1,), (0,))
_NT = ((1,), (1,))
_TN = ((0,), (0,))


def _mm_tiles(M, N, K, mode):
    half_ff = D_FF // 2
    tm = _tile(M, half_ff if M % half_ff == 0 else (1024 if mode != "tn" and M >= 2048 else 512), LANES if mode == "tn" else 16)
    tn = _tile(N, half_ff if N % half_ff == 0 else 1024, 256 if N % 256 == 0 and N % half_ff else LANES)
    tk = K if K <= D_FF else _tile(K, 1024)
    return tm, tn, tk


def _mm(a, b, *, mode="nn", add=None, out_dtype=F32, tiles=None, name):
    if mode == "nn":
        (M, K), (K2, N) = a.shape, b.shape
    elif mode == "nt":
        (M, K), (N, K2) = a.shape, b.shape
    else:
        (K, M), (K2, N) = a.shape, b.shape
    assert K == K2, (a.shape, b.shape, mode)
    tm, tn, tk = tiles or _mm_tiles(M, N, K, mode)
    nk = K // tk
    dims = {"nn": _NN, "nt": _NT, "tn": _TN}[mode]
    a_spec = pl.BlockSpec((tk, tm), lambda i, j, k: (k, i)) if mode == "tn" else pl.BlockSpec((tm, tk), lambda i, j, k: (i, k))
    b_spec = pl.BlockSpec((tn, tk), lambda i, j, k: (j, k)) if mode == "nt" else pl.BlockSpec((tk, tn), lambda i, j, k: (k, j))
    o_spec = pl.BlockSpec((tm, tn), lambda i, j, k: (i, j))
    has_add = add is not None

    def body(*refs):
        if has_add:
            a_ref, b_ref, add_ref, o_ref, acc_ref = refs
        else:
            a_ref, b_ref, o_ref, acc_ref = refs
        k = pl.program_id(2)
        part = _dot(a_ref[...].astype(BF16), b_ref[...].astype(BF16), dims)

        @pl.when(k == 0)
        def _():
            acc_ref[...] = part

        @pl.when(k > 0)
        def _():
            acc_ref[...] += part

        @pl.when(k == nk - 1)
        def _():
            r = acc_ref[...]
            if has_add:
                r = r + add_ref[...].astype(F32)
            o_ref[...] = r.astype(out_dtype)

    ins = [a, b] + ([add] if has_add else [])
    in_specs = [a_spec, b_spec] + ([o_spec] if has_add else [])
    return _pallas(
        body, name=name, grid=(M // tm, N // tn, nk), in_specs=in_specs, out_specs=o_spec,
        out_shape=jax.ShapeDtypeStruct((M, N), out_dtype), scratch_shapes=[pltpu.VMEM((tm, tn), F32)],
        compiler_params=_params("parallel", "parallel", "arbitrary"),
    )(*ins)


def _rms(x, g, *, col=0, width=None, out_dtype=BF16, name):
    T = x.shape[0]
    width = x.shape[1] if width is None else width
    assert col % width == 0
    tm = _tile(T, 512, 8)
    cb = col // width

    def body(x_ref, g_ref, o_ref):
        xv = x_ref[...].astype(F32)
        r = lax.rsqrt(jnp.mean(xv * xv, axis=-1, keepdims=True) + EPS)
        o_ref[...] = (xv * r * g_ref[...]).astype(out_dtype)

    return _pallas(
        body, name=name, grid=(T // tm,),
        in_specs=[pl.BlockSpec((tm, width), lambda i: (i, cb)), pl.BlockSpec((1, width), lambda i: (0, 0))],
        out_specs=pl.BlockSpec((tm, width), lambda i: (i, 0)),
        out_shape=jax.ShapeDtypeStruct((T, width), out_dtype), compiler_params=_params("parallel"),
    )(x, g.reshape(1, width))


def _rms_bwd(x, g, dy, *, res=None, col=0, width=None, out_dtype=F32, name):
    T = x.shape[0]
    width = x.shape[1] if width is None else width
    assert col % width == 0
    tm = _tile(T, 512, 8)
    cb = col // width
    has_res = res is not None

    def body(*refs):
        if has_res:
            x_ref, g_ref, dy_ref, res_ref, dx_ref, dg_ref = refs
        else:
            x_ref, g_ref, dy_ref, dx_ref, dg_ref = refs
        xv = x_ref[...].astype(F32)
        r = lax.rsqrt(jnp.mean(xv * xv, axis=-1, keepdims=True) + EPS)
        xh = xv * r
        dyv = dy_ref[...].astype(F32)
        dxh = dyv * g_ref[...]
        dx = r * (dxh - xh * jnp.mean(dxh * xh, axis=-1, keepdims=True))
        if has_res:
            dx = dx + res_ref[...].astype(F32)
        dx_ref[...] = dx.astype(out_dtype)
        part = jnp.sum(dyv * xh, axis=0, keepdims=True)

        @pl.when(pl.program_id(0) == 0)
        def _():
            dg_ref[...] = part

        @pl.when(pl.program_id(0) > 0)
        def _():
            dg_ref[...] += part

    row = pl.BlockSpec((tm, width), lambda i: (i, 0))
    ins = [x, g.reshape(1, width), dy] + ([res] if has_res else [])
    in_specs = [pl.BlockSpec((tm, width), lambda i: (i, cb)), pl.BlockSpec((1, width), lambda i: (0, 0)), row] + ([row] if has_res else [])
    return _pallas(
        body, name=name, grid=(T // tm,), in_specs=in_specs,
        out_specs=[row, pl.BlockSpec((1, width), lambda i: (0, 0))],
        out_shape=[jax.ShapeDtypeStruct((T, width), out_dtype), jax.ShapeDtypeStruct((1, width), F32)],
        compiler_params=_params("arbitrary"),
    )(*ins)


def _rope_tables(T):
    half = A_ROPE // 2
    inv = ROPE_THETA ** (-jnp.arange(half, dtype=F32) / half)
    ang = jnp.arange(T, dtype=jnp.int32).astype(F32)[:, None] * inv[None, :]
    c32 = jnp.concatenate([jnp.cos(ang), jnp.cos(ang)], axis=-1)
    s32 = jnp.concatenate([jnp.sin(ang), jnp.sin(ang)], axis=-1)
    pad = A_PAD - A_QK
    cq = jnp.concatenate([jnp.ones((T, A_NOPE), F32), c32, jnp.ones((T, pad), F32)], axis=-1)
    sq = jnp.concatenate([jnp.zeros((T, A_NOPE), F32), s32, jnp.zeros((T, pad), F32)], axis=-1)
    ck = jnp.concatenate([c32, s32, jnp.zeros((T, LANES - 2 * A_ROPE), F32)], axis=-1)
    ck_t = jnp.concatenate([c32, s32], axis=-1).T
    return cq, sq, ck, ck_t


def _rope_swap_cols(w):
    half = A_ROPE // 2
    return jnp.concatenate([-w[..., half:], w[..., :half]], axis=-1)


def _rope_unswap_cols(g):
    half = A_ROPE // 2
    return jnp.concatenate([g[..., half:], -g[..., :half]], axis=-1)


A_PAD = LANES
A_W = A_HEADS * A_PAD
LOG2E = 1.4426950408889634
LN2 = 0.6931471805599453
Q_SCALE = A_QK ** -0.5 * LOG2E


def _qrope(cqn, wq2, cq, sq, *, name):
    T, R = cqn.shape
    W = A_W
    tm = _tile(T, 512)

    def body(x_ref, w_ref, c_ref, s_ref, o_ref, ot_ref):
        q2 = _dot(x_ref[...], w_ref[...], _NN)
        c = jnp.concatenate([c_ref[...]] * A_HEADS, axis=1)
        s = jnp.concatenate([s_ref[...]] * A_HEADS, axis=1)
        q = (q2[:, 0:W] * c + q2[:, W:2 * W] * s) * Q_SCALE
        o_ref[...] = q.astype(BF16)
        ot_ref[...] = q.T.astype(BF16)

    row = pl.BlockSpec((tm, W), lambda i: (i, 0))
    tab = pl.BlockSpec((tm, A_PAD), lambda i: (i, 0))
    return _pallas(body, name=name, grid=(T // tm,),
                   in_specs=[pl.BlockSpec((tm, R), lambda i: (i, 0)), pl.BlockSpec((R, 2 * W), lambda i: (0, 0)), tab, tab],
                   out_specs=[row, pl.BlockSpec((W, tm), lambda i: (0, i))],
                   out_shape=[jax.ShapeDtypeStruct((T, W), BF16), jax.ShapeDtypeStruct((W, T), BF16)],
                   compiler_params=_params("parallel"))(cqn, wq2, cq, sq)


def _kprep(kv, za, ck, *, name):
    T = kv.shape[0]
    tm = _tile(T, 512)

    def body(kv_ref, kr_ref, ck_ref, k_ref, vxt_ref):
        t = kr_ref[...] * ck_ref[...]
        krope = (t[:, 0:A_ROPE] + t[:, A_ROPE:2 * A_ROPE]).astype(BF16)
        one = (lax.broadcasted_iota(jnp.int32, (A_PAD - A_V, tm), 0) == 0).astype(BF16)
        for h in range(A_HEADS):
            k_ref[:, A_PAD * h:A_PAD * h + A_NOPE] = kv_ref[:, A_NOPE * h:A_NOPE * (h + 1)]
            k_ref[:, A_PAD * h + A_NOPE:A_PAD * h + A_QK] = krope
            k_ref[:, A_PAD * h + A_QK:A_PAD * (h + 1)] = jnp.zeros((tm, A_PAD - A_QK), BF16)
            vxt_ref[A_PAD * h + A_V:A_PAD * (h + 1), :] = one
        vt = kv_ref[:, 512:1024].astype(F32).T.astype(BF16)
        for h in range(A_HEADS):
            vxt_ref[A_PAD * h:A_PAD * h + A_V, :] = vt[A_V * h:A_V * (h + 1), :]

    wide = pl.BlockSpec((tm, A_W), lambda i: (i, 0))
    return _pallas(
        body, name=name, grid=(T // tm,),
        in_specs=[wide, pl.BlockSpec((tm, LANES), lambda i: (i, 3)), pl.BlockSpec((tm, LANES), lambda i: (i, 0))],
        out_specs=[wide, pl.BlockSpec((A_W, tm), lambda i: (0, i))],
        out_shape=[jax.ShapeDtypeStruct((T, A_W), BF16), jax.ShapeDtypeStruct((A_W, T), BF16)],
        compiler_params=_params("parallel"))(kv, za, ck)


def _kprep_bwd(dkt, ck_t, *, name):
    T = dkt.shape[1]
    tc = _tile(T, 512)

    def body(dk_ref, ck_ref, dn_ref, dr_ref):
        acc = jnp.zeros((A_ROPE, tc), F32)
        for h in range(A_HEADS):
            dn_ref[A_NOPE * h:A_NOPE * (h + 1), :] = dk_ref[A_PAD * h:A_PAD * h + A_NOPE, :].astype(BF16)
            acc = acc + dk_ref[A_PAD * h + A_NOPE:A_PAD * h + A_QK, :]
        dr_ref[0:A_ROPE, :] = (acc * ck_ref[0:A_ROPE, :]).astype(BF16)
        dr_ref[A_ROPE:2 * A_ROPE, :] = (acc * ck_ref[A_ROPE:2 * A_ROPE, :]).astype(BF16)
        dr_ref[2 * A_ROPE:LANES, :] = jnp.zeros((LANES - 2 * A_ROPE, tc), BF16)

    col = lambda r: pl.BlockSpec((r, tc), lambda i: (0, i))
    return _pallas(
        body, name=name, grid=(T // tc,), in_specs=[col(A_W), col(2 * A_ROPE)], out_specs=[col(512), col(LANES)],
        out_shape=[jax.ShapeDtypeStruct((512, T), BF16), jax.ShapeDtypeStruct((LANES, T), BF16)],
        compiler_params=_params("parallel"))(dkt, ck_t)


def _flash_fwd(qs, k, vxt, *, name):
    T = qs.shape[0]
    tq, tk = _tile(T, 512), _tile(T, 2048)
    nk = T // tk
    H, P, DV = A_HEADS, A_PAD, A_V

    def body(q_ref, k_ref, v_ref, o_ref, lse_ref, m_sc, acc_sc):
        j = pl.program_id(1)

        @pl.when(j == 0)
        def _():
            m_sc[...] = jnp.full(m_sc.shape, -jnp.inf, F32)
            acc_sc[...] = jnp.zeros(acc_sc.shape, F32)

        def scores(h):
            return _dot(k_ref[:, P * h:P * (h + 1)], q_ref[:, P * h:P * (h + 1)], _NT)

        st_next = scores(0)
        for h in range(H):
            st = st_next
            if h + 1 < H:
                st_next = scores(h + 1)
            m_prev = m_sc[h]
            m_new = jnp.maximum(m_prev, jnp.max(st, axis=0, keepdims=True))
            pt = jnp.exp2(st - m_new).astype(BF16)
            acc_sc[h] = jnp.exp2(m_prev - m_new) * acc_sc[h] + _dot(v_ref[P * h:P * (h + 1), :], pt, _NN)
            m_sc[h] = m_new

        @pl.when(j == nk - 1)
        def _():
            for h in range(H):
                acc = acc_sc[h]
                l = acc[DV:DV + 1, :]
                o_ref[:, DV * h:DV * (h + 1)] = (acc[0:DV, :] / l).T
                lse_ref[h] = m_sc[h] + jnp.log2(l)

    return _pallas(
        body, name=name, grid=(T // tq, nk),
        in_specs=[pl.BlockSpec((tq, A_W), lambda i, j: (i, 0)), pl.BlockSpec((tk, A_W), lambda i, j: (j, 0)),
                  pl.BlockSpec((A_W, tk), lambda i, j: (0, j))],
        out_specs=[pl.BlockSpec((tq, H * DV), lambda i, j: (i, 0)), pl.BlockSpec((H, 1, tq), lambda i, j: (0, 0, i))],
        out_shape=[jax.ShapeDtypeStruct((T, H * DV), F32), jax.ShapeDtypeStruct((H, 1, T), F32)],
        scratch_shapes=[pltpu.VMEM((H, 1, tq), F32), pltpu.VMEM((H, P, tq), F32)],
        compiler_params=_params("parallel", "arbitrary"))(qs, k, vxt)


def _attn_delta(o, do, *, name):
    T = o.shape[0]
    tm = _tile(T, 512, 8)

    def body(o_ref, do_ref, d_ref):
        prod = o_ref[...] * do_ref[...].astype(F32)
        for h in range(A_HEADS):
            d_ref[h] = jnp.sum(prod[:, A_V * h:A_V * (h + 1)], axis=-1, keepdims=True)

    row = pl.BlockSpec((tm, A_HEADS * A_V), lambda i: (i, 0))
    return _pallas(body, name=name, grid=(T // tm,), in_specs=[row, row],
                   out_specs=pl.BlockSpec((A_HEADS, tm, 1), lambda i: (0, i, 0)),
                   out_shape=jax.ShapeDtypeStruct((A_HEADS, T, 1), F32), compiler_params=_params("parallel"))(o, do)


def _flash_bwd(qs, qst, k, kv, do, dot_, lse2, delta, *, tiles=None, name):
    T = qs.shape[0]
    tq, tk = tiles or (_tile(T, 512), _tile(T, 1024))
    nq, nk = T // tq, T // tk
    H, P, DV = A_HEADS, A_PAD, A_V

    def body(q_ref, qt_ref, k_ref, v_ref, do_ref, dot_ref, lse_ref, delta_ref, dq_ref, dkt_ref, dvt_ref, dkt_sc, dvt_sc):
        i = pl.program_id(1)

        @pl.when(i == 0)
        def _():
            dkt_sc[...] = jnp.zeros(dkt_sc.shape, F32)
            dvt_sc[...] = jnp.zeros(dvt_sc.shape, F32)

        for h in range(H):
            s = _dot(q_ref[:, P * h:P * (h + 1)], k_ref[:, P * h:P * (h + 1)], _NT)
            dp = _dot(do_ref[:, DV * h:DV * (h + 1)], v_ref[:, DV * h:DV * (h + 1)], _NT)
            p = jnp.exp2(s - lse_ref[h])
            ds = (p * (dp - delta_ref[h])).astype(BF16)
            pb = p.astype(BF16)
            dq_ref[0, :, P * h:P * (h + 1)] = _dot(ds, k_ref[:, P * h:P * (h + 1)], _NN).astype(BF16)
            dkt_sc[h] += _dot(qt_ref[P * h:P * (h + 1), :], ds, _NN)
            dvt_sc[h] += _dot(dot_ref[DV * h:DV * (h + 1), :], pb, _NN)

        @pl.when(i == nq - 1)
        def _():
            for h in range(H):
                dkt_ref[P * h:P * (h + 1), :] = dkt_sc[h] * LN2
                dvt_ref[DV * h:DV * (h + 1), :] = dvt_sc[h].astype(BF16)

    qrow = lambda w: pl.BlockSpec((tq, w), lambda j, i: (i, 0))
    qcol = lambda r: pl.BlockSpec((r, tq), lambda j, i: (0, i))
    stat = pl.BlockSpec((H, tq, 1), lambda j, i: (0, i, 0))
    return _pallas(
        body, name=name, grid=(nk, nq),
        in_specs=[qrow(A_W), qcol(A_W), pl.BlockSpec((tk, A_W), lambda j, i: (j, 0)), pl.BlockSpec((tk, H * DV), lambda j, i: (j, 1)),
                  qrow(H * DV), qcol(H * DV), stat, stat],
        out_specs=[pl.BlockSpec((1, tq, A_W), lambda j, i: (j, i, 0)), pl.BlockSpec((A_W, tk), lambda j, i: (0, j)),
                   pl.BlockSpec((H * DV, tk), lambda j, i: (0, j))],
        out_shape=[jax.ShapeDtypeStruct((nk, T, A_W), BF16), jax.ShapeDtypeStruct((A_W, T), F32),
                   jax.ShapeDtypeStruct((H * DV, T), BF16)],
        scratch_shapes=[pltpu.VMEM((H, P, tk), F32), pltpu.VMEM((H, DV, tk), F32)],
        compiler_params=_params("parallel", "arbitrary"))(qs, qst, k, kv, do, dot_, lse2, delta)


def _dq_sum(dq_part, cq, sq, *, name):
    n, T, W = dq_part.shape
    tm = _tile(T, 256, 16)

    def body(p_ref, c_ref, s_ref, o_ref):
        acc = p_ref[0].astype(F32)
        for j in range(1, n):
            acc = acc + p_ref[j].astype(F32)
        acc = acc * (A_QK ** -0.5)
        o_ref[:, 0:W] = (acc * jnp.concatenate([c_ref[...]] * A_HEADS, axis=1)).astype(BF16)
        o_ref[:, W:2 * W] = (acc * jnp.concatenate([s_ref[...]] * A_HEADS, axis=1)).astype(BF16)

    row = pl.BlockSpec((tm, A_PAD), lambda i: (i, 0))
    return _pallas(body, name=name, grid=(T // tm,), in_specs=[pl.BlockSpec((n, tm, W), lambda i: (0, i, 0)), row, row],
                   out_specs=pl.BlockSpec((tm, 2 * W), lambda i: (i, 0)), out_shape=jax.ShapeDtypeStruct((T, 2 * W), BF16),
                   compiler_params=_params("parallel"))(dq_part, cq, sq)


HB = 8 * B_CHUNK


def _chunk_masks(reverse):
    r = lax.broadcasted_iota(jnp.int32, (HB, HB), 0)
    c = lax.broadcasted_iota(jnp.int32, (HB, HB), 1)
    same = (r // B_CHUNK) == (c // B_CHUNK)
    incl = same & ((c >= r) if reverse else (c <= r))
    return same, incl


def _mask_mm(mask, x):
    hi = x.astype(BF16)
    lo = (x - hi.astype(F32)).astype(BF16)
    return _dot(mask, hi, _NN) + _dot(mask, lo, _NN)


def _hgrn_gates(q, z, lb, reverse):
    same, incl = _chunk_masks(reverse)
    sg = jax.nn.sigmoid(z)
    f = lb + (1.0 - lb) * sg
    lf = jnp.log(jnp.maximum(f, TINY))
    kk = (1.0 - lb) * jax.nn.sigmoid(-z)
    b = _mask_mm(incl.astype(BF16), lf)
    edge = 0 if reverse else B_CHUNK - 1
    btot = jnp.concatenate([jnp.broadcast_to(b[B_CHUNK * c + edge:B_CHUNK * c + edge + 1, :], (B_CHUNK, b.shape[1]))
                            for c in range(HB // B_CHUNK)], axis=0)
    eb, enb, er, dec = jnp.exp(b), jnp.exp(-b), jnp.exp(btot - b), jnp.exp(btot)
    return dict(same=same, incl=incl, sg=sg, f=f, kk=kk, eb=eb, enb=enb, er=er, dec=dec,
                qd=q * eb, ki=kk * enb, ke=kk * er)


def _hgrn_specs(T, reverse, gate_reverse):
    nb = T // HB
    blk = (lambda i: nb - 1 - i) if reverse else (lambda i: i)
    wide = B_HEADS * B_DK
    return nb, blk, [
        pl.BlockSpec((HB, wide), lambda i: (blk(i), 0)),
        pl.BlockSpec((HB, wide), lambda i: (blk(i), 2 if gate_reverse else 1)),
        pl.BlockSpec((HB, B_HEADS * B_DV), lambda i: (blk(i), 6)),
        pl.BlockSpec((1, wide), lambda i: (0, 0)),
    ]


def _hk(h):
    return slice(B_DK * h, B_DK * (h + 1))


def _hv(h):
    return slice(B_DV * h, B_DV * (h + 1))


def _crows(c):
    return slice(B_CHUNK * c, B_CHUNK * (c + 1))


def _chunk_selectors():
    r = lax.broadcasted_iota(jnp.int32, (HB, 1), 0) // B_CHUNK
    l = lax.broadcasted_iota(jnp.int32, (1, HB), 1) // B_CHUNK
    return [r == c for c in range(8)], [l == c for c in range(8)]


def _hgrn_fwd(zb, lb, *, reverse, name):
    T = zb.shape[0]
    nb, blk, in_specs = _hgrn_specs(T, reverse, reverse)
    order = range(7, -1, -1) if reverse else range(8)
    heads = range(B_HEADS)

    def body(q_ref, z_ref, v_ref, lb_ref, o_ref, st_ref, s_sc):
        @pl.when(pl.program_id(0) == 0)
        def _():
            s_sc[...] = jnp.zeros(s_sc.shape, F32)

        g = _hgrn_gates(q_ref[...], z_ref[...], lb_ref[...], reverse)
        v = v_ref[...].astype(BF16)
        qd, ki, ke = g["qd"].astype(BF16), g["ki"].astype(BF16), g["ke"].astype(BF16)
        dec = g["dec"]
        in_chunk_rows, in_chunk_lanes = _chunk_selectors()
        o_intra, upd = [], []
        for h in heads:
            a = jnp.where(g["incl"], _dot(qd[:, _hk(h)], ki[:, _hk(h)], _NT), 0.0)
            o_intra.append(_dot(a.astype(BF16), v[:, _hv(h)], _NN))
            vt = v[:, _hv(h)].T
            lhs = jnp.concatenate([jnp.where(in_chunk_lanes[c], vt, 0) for c in range(8)], axis=0)
            upd.append(_dot(lhs, ke[:, _hk(h)], _NN))
        st = [s_sc[h] for h in heads]
        snap = [[None] * 8 for _ in heads]
        for c in order:
            for h in heads:
                snap[h][c] = st[h]
                st[h] = st[h] * dec[B_CHUNK * c:B_CHUNK * c + 1, _hk(h)] + upd[h][B_DV * c:B_DV * (c + 1), :]
        for h in heads:
            s_sc[h] = st[h]
            for c in range(8):
                st_ref[h, c] = snap[h][c]
            qd_big = jnp.concatenate([jnp.where(in_chunk_rows[c], qd[:, _hk(h)], 0) for c in range(8)], axis=1)
            states = jnp.concatenate([snap[h][c].astype(BF16) for c in range(8)], axis=1)
            o_ref[h] = o_intra[h] + _dot(qd_big, states, _NT)

    return _pallas(
        body, name=name, grid=(nb,), in_specs=in_specs,
        out_specs=[pl.BlockSpec((B_HEADS, HB, B_DV), lambda i: (0, blk(i), 0)),
                   pl.BlockSpec((B_HEADS, 8, B_DV, B_DK), lambda i: (0, blk(i), 0, 0))],
        out_shape=[jax.ShapeDtypeStruct((B_HEADS, T, B_DV), F32),
                   jax.ShapeDtypeStruct((B_HEADS, T // B_CHUNK, B_DV, B_DK), F32)],
        scratch_shapes=[pltpu.VMEM((B_HEADS, B_DV, B_DK), F32)],
        compiler_params=_params("arbitrary"))(zb, zb, zb, lb)


def _hgrn_bwd(zb, lb, do, states, *, reverse, name):
    T = zb.shape[0]
    nb, blk, in_specs = _hgrn_specs(T, not reverse, reverse)
    order = range(8) if reverse else range(7, -1, -1)
    heads = range(B_HEADS)

    def body(q_ref, z_ref, v_ref, lb_ref, do_ref, st_ref, dq_ref, dz_ref, dv_ref, dlb_ref, ds_sc):
        @pl.when(pl.program_id(0) == 0)
        def _():
            ds_sc[...] = jnp.zeros(ds_sc.shape, F32)
            dlb_ref[...] = jnp.zeros(dlb_ref.shape, F32)

        lb = lb_ref[...]
        g = _hgrn_gates(q_ref[...], z_ref[...], lb, reverse)
        v = v_ref[...].astype(BF16)
        qd, ki, ke = g["qd"].astype(BF16), g["ki"].astype(BF16), g["ke"].astype(BF16)
        dec = g["dec"]
        dout = [do_ref[h].astype(BF16) for h in heads]
        in_chunk_rows, in_chunk_lanes = _chunk_selectors()
        _, incl_t = _chunk_masks(not reverse)
        rows_of = lambda x: jnp.concatenate([jnp.where(in_chunk_rows[c], x, 0) for c in range(8)], axis=1)
        dv_i, dqd_h, dki, upd = [], [], [], []
        for h in heads:
            qd_h, ki_h, v_h = qd[:, _hk(h)], ki[:, _hk(h)], v[:, _hv(h)]
            da = jnp.where(g["incl"], _dot(dout[h], v_h, _NT), 0.0).astype(BF16)
            at = jnp.where(incl_t, _dot(ki_h, qd_h, _NT), 0.0).astype(BF16)
            dat = jnp.where(incl_t, _dot(v_h, dout[h], _NT), 0.0).astype(BF16)
            dv_i.append(_dot(at, dout[h], _NN))
            dki.append(_dot(dat, qd_h, _NN))
            dot_t = dout[h].T
            lhs = jnp.concatenate([jnp.where(in_chunk_lanes[c], dot_t, 0) for c in range(8)], axis=0)
            upd.append(_dot(lhs, qd_h, _NN))
            saved = jnp.concatenate([st_ref[h, c].astype(BF16) for c in range(8)], axis=0)
            dqd_h.append(_dot(da, ki_h, _NN) + _dot(rows_of(dout[h]), saved, _NN))
        dst = [ds_sc[h] for h in heads]
        used = [[None] * 8 for _ in heads]
        for c in order:
            for h in heads:
                used[h][c] = dst[h]
                dst[h] = dst[h] * dec[B_CHUNK * c:B_CHUNK * c + 1, _hk(h)] + upd[h][B_DV * c:B_DV * (c + 1), :]
        dke_h, ddec_h = [], []
        for h in heads:
            ds_sc[h] = dst[h]
            used16 = [used[h][c].astype(BF16) for c in range(8)]
            dv_ref[h] = dv_i[h] + _dot(rows_of(ke[:, _hk(h)]), jnp.concatenate(used16, axis=1), _NT)
            dke_h.append(_dot(rows_of(v[:, _hv(h)]), jnp.concatenate(used16, axis=0), _NN))
            ddec_p = []
            for c in range(8):
                tot = jnp.sum(used[h][c] * st_ref[h, c], axis=0, keepdims=True) * dec[B_CHUNK * c:B_CHUNK * c + 1, _hk(h)]
                ddec_p.append(jnp.broadcast_to(tot, (B_CHUNK, B_DK)))
            ddec_h.append(jnp.concatenate(ddec_p, axis=0))
        dqd = jnp.concatenate(dqd_h, axis=1)
        dki = jnp.concatenate(dki, axis=1)
        dke = jnp.concatenate(dke_h, axis=1)
        db = dqd * g["qd"] - dki * g["ki"] - dke * g["ke"]
        masks = jnp.concatenate([incl_t.astype(BF16), g["same"].astype(BF16)], axis=1)
        dlf = _mask_mm(masks, jnp.concatenate([db, dke * g["ke"]], axis=0)) + jnp.concatenate(ddec_h, axis=1)
        dk = dki * g["enb"] + dke * g["er"]
        u = jnp.where(g["f"] > TINY, dlf / g["f"], 0.0) - dk
        sg = g["sg"]
        dq_ref[...] = dqd * g["eb"]
        dz_ref[...] = u * (1.0 - lb) * sg * (1.0 - sg)
        dlb_ref[...] += jnp.sum(u * (1.0 - sg), axis=0, keepdims=True)

    wide = pl.BlockSpec((HB, B_HEADS * B_DK), lambda i: (blk(i), 0))
    hm = pl.BlockSpec((B_HEADS, HB, B_DV), lambda i: (0, blk(i), 0))
    return _pallas(
        body, name=name, grid=(nb,),
        in_specs=in_specs + [hm, pl.BlockSpec((B_HEADS, 8, B_DV, B_DK), lambda i: (0, blk(i), 0, 0))],
        out_specs=[wide, wide, hm, pl.BlockSpec((1, B_HEADS * B_DK), lambda i: (0, 0))],
        out_shape=[jax.ShapeDtypeStruct((T, B_HEADS * B_DK), F32), jax.ShapeDtypeStruct((T, B_HEADS * B_DK), F32),
                   jax.ShapeDtypeStruct((B_HEADS, T, B_DV), F32), jax.ShapeDtypeStruct((1, B_HEADS * B_DK), F32)],
        scratch_shapes=[pltpu.VMEM((B_HEADS, B_DV, B_DK), F32)],
        compiler_params=_params("arbitrary"))(zb, zb, zb, lb, do, states)


def _hgrn_out(of, ob, zb, gout, *, name):
    T = zb.shape[0]
    tm = _tile(T, 512, 8)

    def body(of_ref, ob_ref, g_ref, gout_ref, y_ref):
        for h in range(B_HEADS):
            o = of_ref[h] + ob_ref[h]
            r = lax.rsqrt(jnp.mean(o * o, axis=-1, keepdims=True) + EPS)
            gh = g_ref[:, B_DV * h:B_DV * (h + 1)]
            y_ref[:, B_DV * h:B_DV * (h + 1)] = (o * r * gout_ref[...] * (gh * jax.nn.sigmoid(gh))).astype(BF16)

    hm = pl.BlockSpec((B_HEADS, tm, B_DV), lambda i: (0, i, 0))
    return _pallas(
        body, name=name, grid=(T // tm,),
        in_specs=[hm, hm, pl.BlockSpec((tm, 512), lambda i: (i, 7)), pl.BlockSpec((1, B_DV), lambda i: (0, 0))],
        out_specs=pl.BlockSpec((tm, 512), lambda i: (i, 0)),
        out_shape=jax.ShapeDtypeStruct((T, 512), BF16), compiler_params=_params("parallel"))(of, ob, zb, gout.reshape(1, B_DV))


def _hgrn_out_bwd(of, ob, zb, gout, dy, *, name):
    T = zb.shape[0]
    tm = _tile(T, 512, 8)

    def body(of_ref, ob_ref, g_ref, gout_ref, dy_ref, do_ref, dg_ref, dgo_ref):
        gout_v = gout_ref[...]
        acc = jnp.zeros((1, B_DV), F32)
        for h in range(B_HEADS):
            o = of_ref[h] + ob_ref[h]
            r = lax.rsqrt(jnp.mean(o * o, axis=-1, keepdims=True) + EPS)
            oh = o * r
            gh = g_ref[:, B_DV * h:B_DV * (h + 1)]
            sg = jax.nn.sigmoid(gh)
            dyh = dy_ref[:, B_DV * h:B_DV * (h + 1)].astype(F32)
            dn = dyh * (gh * sg)
            dg_ref[:, B_DV * h:B_DV * (h + 1)] = dyh * (oh * gout_v) * (sg * (1.0 + gh * (1.0 - sg)))
            dxh = dn * gout_v
            do_ref[h] = r * (dxh - oh * jnp.mean(dxh * oh, axis=-1, keepdims=True))
            acc = acc + jnp.sum(dn * oh, axis=0, keepdims=True)

        @pl.when(pl.program_id(0) == 0)
        def _():
            dgo_ref[...] = acc

        @pl.when(pl.program_id(0) > 0)
        def _():
            dgo_ref[...] += acc

    hm = pl.BlockSpec((B_HEADS, tm, B_DV), lambda i: (0, i, 0))
    row = pl.BlockSpec((tm, 512), lambda i: (i, 0))
    return _pallas(
        body, name=name, grid=(T // tm,),
        in_specs=[hm, hm, pl.BlockSpec((tm, 512), lambda i: (i, 7)), pl.BlockSpec((1, B_DV), lambda i: (0, 0)), row],
        out_specs=[hm, row, pl.BlockSpec((1, B_DV), lambda i: (0, 0))],
        out_shape=[jax.ShapeDtypeStruct((B_HEADS, T, B_DV), F32), jax.ShapeDtypeStruct((T, 512), F32),
                   jax.ShapeDtypeStruct((1, B_DV), F32)],
        compiler_params=_params("arbitrary"))(of, ob, zb, gout.reshape(1, B_DV), dy)


def _dzb_assemble(dq_f, dq_b, dzf, dzb_, dv_f, dv_b, dgate, *, name):
    T = dq_f.shape[0]
    tm = _tile(T, 256, 8)

    def body(qf, qb, zf, zr, vf, vr, dg, o_ref):
        o_ref[:, 0:1024] = (qf[...] + qb[...]).astype(BF16)
        o_ref[:, 1024:2048] = zf[...].astype(BF16)
        o_ref[:, 2048:3072] = zr[...].astype(BF16)
        for h in range(B_HEADS):
            o_ref[:, 3072 + B_DV * h:3072 + B_DV * (h + 1)] = (vf[h] + vr[h]).astype(BF16)
        o_ref[:, 3584:4096] = dg[...].astype(BF16)

    wide = pl.BlockSpec((tm, 1024), lambda i: (i, 0))
    hm = pl.BlockSpec((B_HEADS, tm, B_DV), lambda i: (0, i, 0))
    return _pallas(
        body, name=name, grid=(T // tm,),
        in_specs=[wide, wide, wide, wide, hm, hm, pl.BlockSpec((tm, 512), lambda i: (i, 0))],
        out_specs=pl.BlockSpec((tm, 4096), lambda i: (i, 0)),
        out_shape=jax.ShapeDtypeStruct((T, 4096), BF16), compiler_params=_params("parallel"))(dq_f, dq_b, dzf, dzb_, dv_f, dv_b, dgate)


C_SPAN = 3 * C_BLOCK
C_G = C_HEADS // C_KV_HEADS


def _t5_bucket(rel):
    nb = REL_BUCKETS // 2
    max_exact = nb // 2
    ret = (rel > 0).astype(jnp.int32) * nb
    n = jnp.abs(rel)
    large = max_exact + (jnp.log(jnp.maximum(n, 1).astype(F32) / max_exact)
                         / math.log(REL_MAX_DIST / max_exact) * (nb - max_exact)).astype(jnp.int32)
    large = jnp.minimum(large, nb - 1)
    return ret + jnp.where(n < max_exact, n, large)


def _swa_buckets():
    rel = jnp.arange(C_SPAN)[None, :] - C_BLOCK - jnp.arange(C_BLOCK)[:, None]
    return _t5_bucket(rel)


def _swa_specs(T):
    nb = T // C_BLOCK
    return nb, [
        pl.BlockSpec((C_BLOCK, 512), lambda n: (n, 0)),
        pl.BlockSpec((C_BLOCK, LANES), lambda n: (jnp.maximum(n - 1, 0), 4)),
        pl.BlockSpec((C_BLOCK, LANES), lambda n: (n, 4)),
        pl.BlockSpec((C_BLOCK, LANES), lambda n: (jnp.minimum(n + 1, nb - 1), 4)),
        pl.BlockSpec((C_BLOCK, LANES), lambda n: (jnp.maximum(n - 1, 0), 5)),
        pl.BlockSpec((C_BLOCK, LANES), lambda n: (n, 5)),
        pl.BlockSpec((C_BLOCK, LANES), lambda n: (jnp.minimum(n + 1, nb - 1), 5)),
        pl.BlockSpec((C_HEADS, C_BLOCK, C_SPAN), lambda n: (0, 0, 0)),
        pl.BlockSpec(memory_space=pltpu.SMEM),
    ]


def _swa_valid(n, T):
    qi = lax.broadcasted_iota(jnp.int32, (C_BLOCK, C_SPAN), 0)
    si = lax.broadcasted_iota(jnp.int32, (C_BLOCK, C_SPAN), 1)
    rel = si - C_BLOCK - qi
    kpos = (n - 1) * C_BLOCK + si
    return (jnp.abs(rel) <= C_WINDOW) & (kpos >= 0) & (kpos < T)


def _swa_softmax(raw, bias, valid, sink):
    s = raw * (C_DH ** -0.5) + bias
    s = jnp.where(valid, s, MASK_VALUE)
    m = jnp.maximum(jnp.max(s, axis=-1, keepdims=True), sink)
    e = jnp.exp(s - m)
    den = jnp.sum(e, axis=-1, keepdims=True) + jnp.exp(sink - m)
    return e / den, jnp.exp(sink - m) / den


def _swa_fwd(zc, bias, sink, *, name):
    T = zc.shape[0]
    nb, in_specs = _swa_specs(T)

    def body(q_ref, kp, kc, kn, vp, vc, vn, bias_ref, sink_ref, y_ref):
        n = pl.program_id(0)
        kcat = jnp.concatenate([kp[...], kc[...], kn[...]], axis=0)
        vcat = jnp.concatenate([vp[...], vc[...], vn[...]], axis=0)
        valid = _swa_valid(n, T)
        heads = range(C_HEADS)
        kvs = [slice(C_DH * (h // C_G), C_DH * (h // C_G + 1)) for h in heads]
        scores = [_dot(q_ref[:, C_DH * h:C_DH * (h + 1)], kcat[:, kvs[h]], _NT) for h in heads]
        probs = [_swa_softmax(scores[h], bias_ref[h], valid, sink_ref[h])[0].astype(BF16) for h in heads]
        for h in heads:
            y_ref[:, C_DH * h:C_DH * (h + 1)] = _dot(probs[h], vcat[:, kvs[h]], _NN).astype(BF16)

    return _pallas(
        body, name=name, grid=(nb,), in_specs=in_specs, out_specs=pl.BlockSpec((C_BLOCK, 512), lambda n: (n, 0)),
        out_shape=jax.ShapeDtypeStruct((T, 512), BF16), compiler_params=_params("parallel"))(zc, zc, zc, zc, zc, zc, zc, bias, sink)


def _swa_bwd(zc, bias, sink, dy, *, name):
    T = zc.shape[0]
    nb, in_specs = _swa_specs(T)
    scale = C_DH ** -0.5

    def body(q_ref, kp, kc, kn, vp, vc, vn, bias_ref, sink_ref, dy_ref, dq_ref, dkc_ref, dvc_ref, dbias_ref, dsink_ref):
        n = pl.program_id(0)

        @pl.when(n == 0)
        def _():
            dbias_ref[...] = jnp.zeros(dbias_ref.shape, F32)
            dsink_ref[...] = jnp.zeros(dsink_ref.shape, F32)

        kcat = jnp.concatenate([kp[...], kc[...], kn[...]], axis=0)
        vcat = jnp.concatenate([vp[...], vc[...], vn[...]], axis=0)
        valid = _swa_valid(n, T)
        heads = range(C_HEADS)
        kvs = [slice(C_DH * (h // C_G), C_DH * (h // C_G + 1)) for h in heads]
        qs = [q_ref[:, C_DH * h:C_DH * (h + 1)] for h in heads]
        dos = [dy_ref[:, C_DH * h:C_DH * (h + 1)].astype(BF16) for h in heads]
        scores = [_dot(qs[h], kcat[:, kvs[h]], _NT) for h in heads]
        dps = [_dot(dos[h], vcat[:, kvs[h]], _NT) for h in heads]
        pbs, dsbs = [], []
        for h in heads:
            p, p_sink = _swa_softmax(scores[h], bias_ref[h], valid, sink_ref[h])
            rowdot = jnp.sum(p * dps[h], axis=-1, keepdims=True)
            ds = p * (dps[h] - rowdot)
            dbias_ref[h] += ds
            tot = jnp.sum(jnp.sum(-p_sink * rowdot, axis=0, keepdims=True), axis=1, keepdims=True)
            dsink_ref[h:h + 1, :] += jnp.broadcast_to(tot, (1, LANES))
            pbs.append(p.astype(BF16))
            dsbs.append((ds * scale).astype(BF16))
        for h in heads:
            dq_ref[:, C_DH * h:C_DH * (h + 1)] = _dot(dsbs[h], kcat[:, kvs[h]], _NN).astype(BF16)
        dks = [_dot(dsbs[h], qs[h], _TN) for h in heads]
        dvs = [_dot(pbs[h], dos[h], _TN) for h in heads]
        for kv in range(C_KV_HEADS):
            group = range(kv * C_G, (kv + 1) * C_G)
            dkc_ref[0, :, C_DH * kv:C_DH * (kv + 1)] = sum(dks[h] for h in group)
            dvc_ref[0, :, C_DH * kv:C_DH * (kv + 1)] = sum(dvs[h] for h in group)

    part = pl.BlockSpec((1, C_SPAN, LANES), lambda n: (n, 0, 0))
    dq, dkc, dvc, dbias, dsink = _pallas(
        body, name=name, grid=(nb,), in_specs=in_specs + [pl.BlockSpec((C_BLOCK, 512), lambda n: (n, 0))],
        out_specs=[pl.BlockSpec((C_BLOCK, 512), lambda n: (n, 0)), part, part,
                   pl.BlockSpec((C_HEADS, C_BLOCK, C_SPAN), lambda n: (0, 0, 0)), pl.BlockSpec((C_HEADS, LANES), lambda n: (0, 0))],
        out_shape=[jax.ShapeDtypeStruct((T, 512), BF16), jax.ShapeDtypeStruct((nb, C_SPAN, LANES), F32),
                   jax.ShapeDtypeStruct((nb, C_SPAN, LANES), F32), jax.ShapeDtypeStruct((C_HEADS, C_BLOCK, C_SPAN), F32),
                   jax.ShapeDtypeStruct((C_HEADS, LANES), F32)],
        compiler_params=_params("arbitrary"))(zc, zc, zc, zc, zc, zc, zc, bias, sink, dy)

    def combine(dq_ref, kp, kc, kn, vp, vc, vn, o_ref):
        n = pl.program_id(0)
        lo = (n > 0).astype(F32)
        hi = (n < nb - 1).astype(F32)
        o_ref[:, 0:512] = dq_ref[...]
        o_ref[:, 512:640] = (kp[0] * lo + kc[0] + kn[0] * hi).astype(BF16)
        o_ref[:, 640:768] = (vp[0] * lo + vc[0] + vn[0] * hi).astype(BF16)

    prev = pl.BlockSpec((1, C_BLOCK, LANES), lambda n: (jnp.maximum(n - 1, 0), 2, 0))
    cur = pl.BlockSpec((1, C_BLOCK, LANES), lambda n: (n, 1, 0))
    nxt = pl.BlockSpec((1, C_BLOCK, LANES), lambda n: (jnp.minimum(n + 1, nb - 1), 0, 0))
    dzc = _pallas(
        combine, name=name + "_combine", grid=(nb,),
        in_specs=[pl.BlockSpec((C_BLOCK, 512), lambda n: (n, 0)), prev, cur, nxt, prev, cur, nxt],
        out_specs=pl.BlockSpec((C_BLOCK, 768), lambda n: (n, 0)),
        out_shape=jax.ShapeDtypeStruct((T, 768), BF16), compiler_params=_params("parallel"))(dq, dkc, dkc, dkc, dvc, dvc, dvc)
    return dzc, dbias, dsink


def _merge_tiles(T):
    return _tile(T, 512, 8), 512


def _merge_fwd(ya, yb, yc, wa, wb, wc, zg, *, name):
    T = ya.shape[0]
    tm, tn = _merge_tiles(T)
    nd = D_MODEL // tn

    def body(ya_ref, yb_ref, yc_ref, wa_ref, wb_ref, wc_ref, ga_ref, gb_ref, gc_ref, o_ref):
        acc = jax.nn.sigmoid(ga_ref[...].astype(F32)) * _dot(ya_ref[...].astype(BF16), wa_ref[...], _NN)
        acc += jax.nn.sigmoid(gb_ref[...].astype(F32)) * _dot(yb_ref[...].astype(BF16), wb_ref[...], _NN)
        acc += jax.nn.sigmoid(gc_ref[...].astype(F32)) * _dot(yc_ref[...].astype(BF16), wc_ref[...], _NN)
        o_ref[...] = acc.astype(BF16)

    y = pl.BlockSpec((tm, 512), lambda i, j: (i, 0))
    w = pl.BlockSpec((512, tn), lambda i, j: (0, j))
    gate = lambda b: pl.BlockSpec((tm, tn), lambda i, j: (i, b * nd + j))
    return _pallas(
        body, name=name, grid=(T // tm, nd), in_specs=[y, y, y, w, w, w, gate(0), gate(1), gate(2)],
        out_specs=pl.BlockSpec((tm, tn), lambda i, j: (i, j)),
        out_shape=jax.ShapeDtypeStruct((T, D_MODEL), BF16),
        compiler_params=_params("parallel", "parallel"))(ya, yb, yc, wa, wb, wc, zg, zg, zg)


def _merge_bwd(ya, yb, yc, wa, wb, wc, zg, dm, *, name):
    T = ya.shape[0]
    tm, tn = _merge_tiles(T)
    nd = D_MODEL // tn

    def body(ya_ref, yb_ref, yc_ref, wa_ref, wb_ref, wc_ref, ga_ref, gb_ref, gc_ref, dm_ref, *outs):
        dmv = dm_ref[...].astype(F32)
        for y_ref, w_ref, g_ref, du_ref, dg_ref in zip((ya_ref, yb_ref, yc_ref), (wa_ref, wb_ref, wc_ref),
                                                       (ga_ref, gb_ref, gc_ref), outs[:3], outs[3:]):
            u = _dot(y_ref[...].astype(BF16), w_ref[...], _NN)
            sg = jax.nn.sigmoid(g_ref[...].astype(F32))
            du_ref[...] = (dmv * sg).astype(BF16)
            dg_ref[...] = (dmv * u * sg * (1.0 - sg)).astype(BF16)

    y = pl.BlockSpec((tm, 512), lambda i, j: (i, 0))
    w = pl.BlockSpec((512, tn), lambda i, j: (0, j))
    gate = lambda b: pl.BlockSpec((tm, tn), lambda i, j: (i, b * nd + j))
    t = pl.BlockSpec((tm, tn), lambda i, j: (i, j))
    return _pallas(
        body, name=name, grid=(T // tm, nd), in_specs=[y, y, y, w, w, w, gate(0), gate(1), gate(2), t],
        out_specs=[t] * 6, out_shape=[jax.ShapeDtypeStruct((T, D_MODEL), BF16)] * 6,
        compiler_params=_params("parallel", "parallel"))(ya, yb, yc, wa, wb, wc, zg, zg, zg, dm)


def _cross_fwd(q, kvm, *, name):
    T = q.shape[0]
    M = kvm.shape[0]
    tm = _tile(T, 512, 8)
    scale = X_DH ** -0.5

    def body(q_ref, k_ref, v_ref, o_ref):
        for h in range(X_HEADS):
            cs = slice(X_DH * h, X_DH * (h + 1))
            s = _dot(q_ref[:, cs], k_ref[:, cs], _NT) * scale
            e = jnp.exp(s - jnp.max(s, axis=-1, keepdims=True))
            p = e / jnp.sum(e, axis=-1, keepdims=True)
            o_ref[:, cs] = _dot(p.astype(BF16), v_ref[:, cs], _NN).astype(BF16)

    row = pl.BlockSpec((tm, D_MODEL), lambda i: (i, 0))
    return _pallas(
        body, name=name, grid=(T // tm,),
        in_specs=[row, pl.BlockSpec((M, D_MODEL), lambda i: (0, 0)), pl.BlockSpec((M, D_MODEL), lambda i: (0, 1))],
        out_specs=row, out_shape=jax.ShapeDtypeStruct((T, D_MODEL), BF16), compiler_params=_params("parallel"))(q, kvm, kvm)


def _cross_bwd(q, kvm, do, *, name):
    T = q.shape[0]
    M = kvm.shape[0]
    tm = _tile(T, 512, 8)
    scale = X_DH ** -0.5

    def body(q_ref, k_ref, v_ref, do_ref, dq_ref, dkv_ref):
        @pl.when(pl.program_id(0) == 0)
        def _():
            dkv_ref[...] = jnp.zeros(dkv_ref.shape, F32)

        for h in range(X_HEADS):
            cs = slice(X_DH * h, X_DH * (h + 1))
            vs = slice(D_MODEL + X_DH * h, D_MODEL + X_DH * (h + 1))
            qh, kh, doh = q_ref[:, cs], k_ref[:, cs], do_ref[:, cs]
            s = _dot(qh, kh, _NT) * scale
            e = jnp.exp(s - jnp.max(s, axis=-1, keepdims=True))
            p = e / jnp.sum(e, axis=-1, keepdims=True)
            dp = _dot(doh, v_ref[:, cs], _NT)
            ds = (p * (dp - jnp.sum(p * dp, axis=-1, keepdims=True)) * scale).astype(BF16)
            dq_ref[:, cs] = _dot(ds, kh, _NN).astype(BF16)
            dkv_ref[:, cs] += _dot(ds, qh, _TN)
            dkv_ref[:, vs] += _dot(p.astype(BF16), doh, _TN)

    row = pl.BlockSpec((tm, D_MODEL), lambda i: (i, 0))
    return _pallas(
        body, name=name, grid=(T // tm,),
        in_specs=[row, pl.BlockSpec((M, D_MODEL), lambda i: (0, 0)), pl.BlockSpec((M, D_MODEL), lambda i: (0, 1)), row],
        out_specs=[row, pl.BlockSpec((M, 2 * D_MODEL), lambda i: (0, 0))],
        out_shape=[jax.ShapeDtypeStruct((T, D_MODEL), BF16), jax.ShapeDtypeStruct((M, 2 * D_MODEL), F32)],
        compiler_params=_params("arbitrary"))(q, kvm, kvm, do)


def _ffn_up(h, w1, w3, *, name):
    T = h.shape[0]
    tm = _tile(T, 512, 8)
    tn = D_FF // 2

    def body(h_ref, w1_ref, w3_ref, a_ref, b_ref, act_ref):
        hv = h_ref[...]
        a = _dot(hv, w1_ref[...], _NN)
        b = _dot(hv, w3_ref[...], _NN)
        a_ref[...] = a.astype(BF16)
        b_ref[...] = b.astype(BF16)
        act_ref[...] = (a * jax.nn.sigmoid(a) * b).astype(BF16)

    w = pl.BlockSpec((D_MODEL, tn), lambda i, j: (0, j))
    t = pl.BlockSpec((tm, tn), lambda i, j: (i, j))
    return _pallas(
        body, name=name, grid=(T // tm, D_FF // tn), in_specs=[pl.BlockSpec((tm, D_MODEL), lambda i, j: (i, 0)), w, w],
        out_specs=[t, t, t],
        out_shape=[jax.ShapeDtypeStruct((T, D_FF), BF16)] * 3,
        compiler_params=_params("parallel", "parallel"))(h, w1, w3)


def _ffn_dact(dx, w2, a, b, *, name):
    T = dx.shape[0]
    tm = _tile(T, 512, 8)
    tn = D_FF // 2

    def body(dx_ref, w2_ref, a_ref, b_ref, da_ref, db_ref):
        dact = _dot(dx_ref[...].astype(BF16), w2_ref[...], _NT)
        av = a_ref[...].astype(F32)
        sg = jax.nn.sigmoid(av)
        da_ref[...] = (dact * b_ref[...].astype(F32) * (sg * (1.0 + av * (1.0 - sg)))).astype(BF16)
        db_ref[...] = (dact * (av * sg)).astype(BF16)

    t = pl.BlockSpec((tm, tn), lambda i, j: (i, j))
    return _pallas(
        body, name=name, grid=(T // tm, D_FF // tn),
        in_specs=[pl.BlockSpec((tm, D_MODEL), lambda i, j: (i, 0)), pl.BlockSpec((tn, D_MODEL), lambda i, j: (j, 0)), t, t],
        out_specs=[t, t], out_shape=[jax.ShapeDtypeStruct((T, D_FF), BF16)] * 2,
        compiler_params=_params("parallel", "parallel"))(dx, w2, a, b)


def _loss_head(x, g, target, *, name):
    T, D = x.shape
    tm = _tile(T, 512, 8)

    def body(x_ref, g_ref, t_ref, loss_ref, dx_ref, dg_ref):
        xv = x_ref[...]
        r = lax.rsqrt(jnp.mean(xv * xv, axis=-1, keepdims=True) + EPS)
        xh = xv * r
        gv = g_ref[...]
        err = xh * gv - t_ref[...]
        dy = err * (1.0 / D)
        dxh = dy * gv
        dx_ref[...] = r * (dxh - xh * jnp.mean(dxh * xh, axis=-1, keepdims=True))
        lpart = 0.5 * jnp.sum(jnp.mean(err * err, axis=-1, keepdims=True), axis=0, keepdims=True)
        gpart = jnp.sum(dy * xh, axis=0, keepdims=True)

        @pl.when(pl.program_id(0) == 0)
        def _():
            loss_ref[...] = jnp.broadcast_to(lpart, (1, LANES))
            dg_ref[...] = gpart

        @pl.when(pl.program_id(0) > 0)
        def _():
            loss_ref[...] += jnp.broadcast_to(lpart, (1, LANES))
            dg_ref[...] += gpart

    row = pl.BlockSpec((tm, D), lambda i: (i, 0))
    vec = pl.BlockSpec((1, D), lambda i: (0, 0))
    return _pallas(
        body, name=name, grid=(T // tm,), in_specs=[row, vec, row],
        out_specs=[pl.BlockSpec((1, LANES), lambda i: (0, 0)), row, vec],
        out_shape=[jax.ShapeDtypeStruct((1, LANES), F32), jax.ShapeDtypeStruct((T, D), F32), jax.ShapeDtypeStruct((1, D), F32)],
        compiler_params=_params("arbitrary"))(x, g.reshape(1, D), target)


IN_CQ, IN_CKV, IN_KR, IN_B, IN_C, IN_G, IN_END = 0, 384, 640, 672, 4768, 5536, 8608
WEIGHT_NAMES = ("w_in", "g_mix", "a_gq", "a_gkv", "a_wuq", "a_wukv", "b_lb", "b_gout", "c_sink", "rel_bias",
                "w_br_a", "w_br_b", "w_br_c", "w_out", "g_x", "g_mem", "x_wq", "x_wkv", "x_wo", "g_ffn",
                "f_w1", "f_w3", "f_w2", "g_final")


def _lower_bounds(b_lb):
    sm = jax.nn.softmax(b_lb.astype(F32), axis=1)
    return jnp.cumsum(sm, axis=1) - sm[:, :1]


def _layer_weights(w, l):
    bf = lambda a: a.astype(BF16)
    w_in = bf(w["w_in"][l])
    kr = w_in[:, IN_KR:IN_B]
    wa = jnp.concatenate([w_in[:, IN_CQ:IN_CKV], kr, _rope_swap_cols(kr), jnp.zeros((D_MODEL, 64), BF16),
                          w_in[:, IN_CKV:IN_KR]], axis=1)
    wuq = bf(w["a_wuq"][l]).reshape(A_Q_RANK, A_HEADS, A_QK)
    zeros = lambda n: jnp.zeros((A_Q_RANK, A_HEADS, n), BF16)
    wuq_pad = jnp.concatenate([wuq, zeros(A_PAD - A_QK)], axis=-1)
    wuq_sw = jnp.concatenate([zeros(A_NOPE), _rope_swap_cols(wuq[..., A_NOPE:]), zeros(A_PAD - A_QK)], axis=-1)
    wq2 = jnp.concatenate([wuq_pad.reshape(A_Q_RANK, -1), wuq_sw.reshape(A_Q_RANK, -1)], axis=1)
    wukv = bf(w["a_wukv"][l]).reshape(A_KV_RANK, A_HEADS, A_NOPE + A_V)
    wkv = jnp.concatenate([wukv[..., :A_NOPE].reshape(A_KV_RANK, -1), wukv[..., A_NOPE:].reshape(A_KV_RANK, -1)], axis=1)
    return dict(wa=wa, wb=w_in[:, IN_B:IN_C], wc=w_in[:, IN_C:IN_G], wg=w_in[:, IN_G:IN_END], wq2=wq2, wkv=wkv,
                w_br_a=bf(w["w_br_a"][l]), w_br_b=bf(w["w_br_b"][l]), w_br_c=bf(w["w_br_c"][l]), w_out=bf(w["w_out"][l]),
                x_wq=bf(w["x_wq"][l]), x_wkv=bf(w["x_wkv"][l]), x_wo=bf(w["x_wo"][l]),
                f_w1=bf(w["f_w1"][l]), f_w3=bf(w["f_w3"][l]), f_w2=bf(w["f_w2"][l]))


def _layer_fwd(l, x, mem, w, lw, lower, bias, tabs):
    n = lambda s: f"l{l}_{s}"
    cq_t, sq_t, ck, _ = tabs
    s = dict(x=x)
    s["h0"] = h0 = _rms(x, w["g_mix"][l], name=n("rms_mix"))
    s["za"] = za = _mm(h0, lw["wa"], name=n("in_a"))
    s["zb"] = zb = _mm(h0, lw["wb"], name=n("in_b"))
    s["zc"] = zc = _mm(h0, lw["wc"], out_dtype=BF16, name=n("in_c"))
    s["zg"] = zg = _mm(h0, lw["wg"], out_dtype=BF16, name=n("in_g"))
    s["cqn"] = cqn = _rms(za, w["a_gq"][l], col=0, width=A_Q_RANK, name=n("rms_cq"))
    s["ckvn"] = ckvn = _rms(za, w["a_gkv"][l], col=512, width=A_KV_RANK, name=n("rms_ckv"))
    s["q"], s["qt"] = q, _ = _qrope(cqn, lw["wq2"], cq_t, sq_t, name=n("uq_rope"))
    s["kv"] = kv = _mm(ckvn, lw["wkv"], out_dtype=BF16, name=n("ukv"))
    s["k"], vxt = k, _ = _kprep(kv, za, ck, name=n("kprep"))
    s["ya"], s["lse"] = ya, _ = _flash_fwd(q, k, vxt, name=n("mla"))
    lb_f, lb_b = lower[0, l].reshape(1, -1), lower[1, l].reshape(1, -1)
    s["of"], s["stf"] = of, _ = _hgrn_fwd(zb, lb_f, reverse=False, name=n("hgrn_f"))
    s["ob"], s["stb"] = ob, _ = _hgrn_fwd(zb, lb_b, reverse=True, name=n("hgrn_b"))
    s["yb"] = yb = _hgrn_out(of, ob, zb, w["b_gout"][l], name=n("hgrn_out"))
    s["yc"] = yc = _swa_fwd(zc, bias, w["c_sink"][l], name=n("swa"))
    s["merged"] = merged = _merge_fwd(ya, yb, yc, lw["w_br_a"], lw["w_br_b"], lw["w_br_c"], zg, name=n("merge"))
    s["x1"] = x1 = _mm(merged, lw["w_out"], add=x, name=n("out"))
    s["h1"] = h1 = _rms(x1, w["g_x"][l], name=n("rms_x"))
    s["qx"] = qx = _mm(h1, lw["x_wq"], out_dtype=BF16, name=n("xq"))
    s["memn"] = memn = _rms(mem, w["g_mem"][l], name=n("rms_mem"))
    s["kvm"] = kvm = _mm(memn, lw["x_wkv"], out_dtype=BF16, name=n("xkv"))
    s["ox"] = ox = _cross_fwd(qx, kvm, name=n("cross"))
    s["x2"] = x2 = _mm(ox, lw["x_wo"], add=x1, name=n("xo"))
    s["h2"] = h2 = _rms(x2, w["g_ffn"][l], name=n("rms_ffn"))
    s["a"], s["b"], s["act"] = a, b, act = _ffn_up(h2, lw["f_w1"], lw["f_w3"], name=n("ffn_up"))
    x3 = _mm(act, lw["f_w2"], add=x2, name=n("ffn_down"))
    return x3, s


def _layer_bwd(l, dx3, mem, w, lw, lower, bias, tabs, s):
    n = lambda t: f"l{l}_b_{t}"
    cq_t, sq_t, _, ck_t = tabs
    g = {}
    da, db = _ffn_dact(dx3, lw["f_w2"], s["a"], s["b"], name=n("ffn_dact"))
    g["f_w2"] = _mm(s["act"], dx3, mode="tn", name=n("dw2"))
    dh2 = _mm(db, lw["f_w3"], mode="nt", add=_mm(da, lw["f_w1"], mode="nt", name=n("dh2a")), name=n("dh2b"))
    g["f_w1"] = _mm(s["h2"], da, mode="tn", name=n("dw1"))
    g["f_w3"] = _mm(s["h2"], db, mode="tn", name=n("dw3"))
    dx2, g["g_ffn"] = _rms_bwd(s["x2"], w["g_ffn"][l], dh2, res=dx3, name=n("rms_ffn"))
    dox = _mm(dx2, lw["x_wo"], mode="nt", out_dtype=BF16, name=n("dox"))
    g["x_wo"] = _mm(s["ox"], dx2, mode="tn", name=n("dwo"))
    dqx, dkvm = _cross_bwd(s["qx"], s["kvm"], dox, name=n("cross"))
    g["x_wq"] = _mm(s["h1"], dqx, mode="tn", name=n("dwq"))
    dh1 = _mm(dqx, lw["x_wq"], mode="nt", name=n("dh1"))
    g["x_wkv"] = _mm(s["memn"], dkvm, mode="tn", name=n("dwkv"))
    dmemn = _mm(dkvm, lw["x_wkv"], mode="nt", name=n("dmemn"))
    _, g["g_mem"] = _rms_bwd(mem, w["g_mem"][l], dmemn, name=n("rms_mem"))
    dx1, g["g_x"] = _rms_bwd(s["x1"], w["g_x"][l], dh1, res=dx2, name=n("rms_x"))
    dmerged = _mm(dx1, lw["w_out"], mode="nt", name=n("dmerged"))
    g["w_out"] = _mm(s["merged"], dx1, mode="tn", name=n("dwout"))
    dua, dub, duc, dga, dgb, dgc = _merge_bwd(s["ya"], s["yb"], s["yc"], lw["w_br_a"], lw["w_br_b"], lw["w_br_c"],
                                              s["zg"], dmerged, name=n("merge"))
    dya = _mm(dua, lw["w_br_a"], mode="nt", out_dtype=BF16, name=n("dya"))
    dyb = _mm(dub, lw["w_br_b"], mode="nt", name=n("dyb"))
    dyc = _mm(duc, lw["w_br_c"], mode="nt", out_dtype=BF16, name=n("dyc"))
    g["w_br_a"] = _mm(s["ya"], dua, mode="tn", name=n("dwbra"))
    g["w_br_b"] = _mm(s["yb"], dub, mode="tn", name=n("dwbrb"))
    g["w_br_c"] = _mm(s["yc"], duc, mode="tn", name=n("dwbrc"))
    dzc, dbias, dsink = _swa_bwd(s["zc"], bias, w["c_sink"][l], dyc, name=n("swa"))
    g["c_sink"] = dsink[:, 0]
    g["bias"] = dbias
    lb_f, lb_b = lower[0, l].reshape(1, -1), lower[1, l].reshape(1, -1)
    do_, dgate, dgout = _hgrn_out_bwd(s["of"], s["ob"], s["zb"], w["b_gout"][l], dyb, name=n("hgrn_out"))
    g["b_gout"] = dgout[0]
    dq_f, dzf, dv_f, dlb_f = _hgrn_bwd(s["zb"], lb_f, do_, s["stf"], reverse=False, name=n("hgrn_f"))
    dq_b, dzr, dv_b, dlb_b = _hgrn_bwd(s["zb"], lb_b, do_, s["stb"], reverse=True, name=n("hgrn_b"))
    g["lower"] = jnp.concatenate([dlb_f, dlb_b], axis=0)
    dzb = _dzb_assemble(dq_f, dq_b, dzf, dzr, dv_f, dv_b, dgate, name=n("dzb"))
    delta = _attn_delta(s["ya"], dya, name=n("mla_delta"))
    dq_part, dkt, dvt = _flash_bwd(s["q"], s["qt"], s["k"], s["kv"], dya, dya.T, s["lse"].reshape(A_HEADS, -1, 1), delta, name=n("mla"))
    dq2 = _dq_sum(dq_part, cq_t, sq_t, name=n("mla_dq"))
    dcqn = _mm(dq2, lw["wq2"], mode="nt", name=n("dcqn"))
    dwq2 = _mm(s["cqn"], dq2, mode="tn", name=n("dwq2")).reshape(A_Q_RANK, 2, A_HEADS, A_PAD)
    dknt, dzkrt = _kprep_bwd(dkt, ck_t, name=n("kprep"))
    wkv = lw["wkv"]
    dckvn = _mm(dvt, wkv[:, 512:].T, mode="tn", add=_mm(dknt, wkv[:, :512].T, mode="tn", name=n("dckvn_k")), name=n("dckvn_v"))
    dwkn = _mm(dknt, s["ckvn"], name=n("dwkn")).T
    dwv = _mm(dvt, s["ckvn"], name=n("dwv")).T
    dzcq, dgq = _rms_bwd(s["za"], w["a_gq"][l], dcqn, col=0, width=A_Q_RANK, out_dtype=BF16, name=n("rms_cq"))
    dzckv, dgkv = _rms_bwd(s["za"], w["a_gkv"][l], dckvn, col=512, width=A_KV_RANK, out_dtype=BF16, name=n("rms_ckv"))
    g["a_gq"], g["a_gkv"] = dgq[0], dgkv[0]
    sw = jnp.concatenate([jnp.zeros((A_Q_RANK, A_HEADS, A_NOPE), F32), _rope_unswap_cols(dwq2[:, 1, :, A_NOPE:A_QK])], axis=-1)
    g["a_wuq"] = (dwq2[:, 0, :, :A_QK] + sw).reshape(A_Q_RANK, -1)
    g["a_wukv"] = jnp.concatenate([dwkn.reshape(A_KV_RANK, A_HEADS, A_NOPE), dwv.reshape(A_KV_RANK, A_HEADS, A_V)], axis=-1).reshape(A_KV_RANK, -1)
    wa = lw["wa"]
    pieces = [(dzcq, wa[:, 0:384]), (dzckv, wa[:, 512:768]), (dzb, lw["wb"]), (dzc, lw["wc"]),
              (dga, lw["wg"][:, 0:1024]), (dgb, lw["wg"][:, 1024:2048]), (dgc, lw["wg"][:, 2048:3072])]
    dh0 = _mm(dzkrt, wa[:, 384:512].T, mode="tn", name=n("dh0_kr"))
    dwkr = _mm(dzkrt, s["h0"], name=n("dwin_kr")).T
    dwkr = dwkr[:, 0:A_ROPE] + _rope_unswap_cols(dwkr[:, A_ROPE:2 * A_ROPE])
    dws = []
    for i, (dz, wp) in enumerate(pieces):
        dh0 = _mm(dz, wp, mode="nt", add=dh0, name=n(f"dh0_{i}"))
        dws.append(_mm(s["h0"], dz, mode="tn", name=n(f"dwin_{i}")))
    g["w_in"] = jnp.concatenate([dws[0], dws[1], dwkr] + dws[2:], axis=1)
    dx, g["g_mix"] = _rms_bwd(s["x"], w["g_mix"][l], dh0, res=dx1, name=n("rms_mix"))
    return dx, g


def _local_step(x, mem, target, w):
    T = x.shape[0]
    tabs = _rope_tables(T)
    lower, lower_vjp = jax.vjp(_lower_bounds, w["b_lb"])
    buckets = _swa_buckets()
    onehot = (buckets.reshape(-1)[:, None] == jnp.arange(REL_BUCKETS)[None, :]).astype(F32)
    bias = jnp.dot(w["rel_bias"].astype(F32).T, onehot.T, precision=lax.Precision.HIGHEST).reshape(C_HEADS, C_BLOCK, C_SPAN)
    lws, saved = [], []
    for l in range(DEPTH):
        lws.append(_layer_weights(w, l))
        x, s = _layer_fwd(l, x, mem, w, lws[l], lower, bias, tabs)
        saved.append(s)
    loss, dx, dg_final = _loss_head(x, w["g_final"], target, name="loss_head")
    layer_grads = [None] * DEPTH
    for l in reversed(range(DEPTH)):
        dx, layer_grads[l] = _layer_bwd(l, dx, mem, w, lws[l], lower, bias, tabs, saved[l])
        saved[l] = None
    grads = {}
    for name in WEIGHT_NAMES:
        if name in layer_grads[0]:
            grads[name] = jnp.stack([layer_grads[l][name].reshape(w[name].shape[1:]) for l in range(DEPTH)])
    grads["g_final"] = dg_final[0]
    dlower = jnp.stack([layer_grads[l]["lower"] for l in range(DEPTH)], axis=1)
    grads["b_lb"] = lower_vjp(dlower)[0]
    dbias = layer_grads[0]["bias"] + layer_grads[1]["bias"]
    grads["rel_bias"] = jnp.dot(onehot.T, dbias.reshape(C_HEADS, -1).T, precision=lax.Precision.HIGHEST)
    return loss, dx, grads


N_CHIPS = 4
PACK_COLS = 1024
PACK_ALIGN = 32 * PACK_COLS
SHARDED = (("w_in", 2), ("a_wuq", 2), ("a_wukv", 2), ("b_lb", 2), ("w_br_a", 2), ("w_br_b", 2), ("w_br_c", 2), ("w_out", 1),
           ("x_wq", 1), ("x_wkv", 2), ("x_wo", 1), ("f_w1", 2), ("f_w3", 2), ("f_w2", 1))
REPLICATED = ("g_mix", "a_gq", "a_gkv", "b_gout", "c_sink", "rel_bias", "g_x", "g_mem", "g_ffn", "g_final")
MESH_IDS = pl.DeviceIdType.MESH
ANY_SPEC = pl.BlockSpec(memory_space=pl.ANY)


def _pack_pieces(arrs, cols, align):
    pieces = [a.reshape(-1, cols) for a in arrs]
    pad = (-sum(p.size for p in pieces)) % align
    return pieces + ([jnp.zeros((pad // cols, cols), pieces[0].dtype)] if pad else [])


def _pack(arrs, cols, align):
    return jnp.concatenate(_pack_pieces(arrs, cols, align), axis=0)


def _pack_small(arrs):
    flat = jnp.concatenate([a.reshape(-1) for a in arrs])
    return jnp.pad(flat, (0, (-flat.shape[0]) % (8 * LANES))).reshape(-1, LANES)


def _unpack(buf, shapes):
    cols = buf.shape[-1]
    buf = buf.reshape(-1, cols)
    by_rows = all(math.prod(shp) % cols == 0 for shp in shapes)
    flat = None if by_rows else buf.reshape(-1)
    out, start = [], 0
    for shp in shapes:
        size = math.prod(shp)
        piece = buf[start // cols:(start + size) // cols] if by_rows else flat[start:start + size]
        out.append(piece.reshape(shp))
        start += size
    return out


def _chip_peers():
    x, y, c = lax.axis_index("x"), lax.axis_index("y"), lax.axis_index("c")
    return x, y, c, [(1 - x, y), (x, 1 - y), (1 - x, 1 - y)]


def _chip_gather(src, chip, *, name):
    _, R, C = src.shape

    def body(src_ref, out_ref, send_sems, recv_sems, pass_send_sems, pass_recv_sems):
        x, y, c, chips = _chip_peers()
        me = 2 * x + y
        sibling = (x, y, 1 - c)

        def over_ici(j, slot):
            px, py = chips[j]
            return pltpu.make_async_remote_copy(src_ref=src_ref.at[c], dst_ref=out_ref.at[slot, c], send_sem=send_sems.at[j],
                                                recv_sem=recv_sems.at[j], device_id=(px, py, c), device_id_type=MESH_IDS)

        def pass_on(j, half):
            px, py = chips[j]
            piece = out_ref.at[2 * px + py, half]
            return pltpu.make_async_remote_copy(src_ref=piece, dst_ref=piece, send_sem=pass_send_sems.at[j],
                                                recv_sem=pass_recv_sems.at[j], device_id=sibling, device_id_type=MESH_IDS)

        sends = [over_ici(j, me) for j in range(3)]
        for cp in sends:
            cp.start()
        passed = []
        for j, (px, py) in enumerate(chips):
            over_ici(j, 2 * px + py).wait_recv()
            passed.append(pass_on(j, c))
            passed[j].start()
        for j in range(3):
            pass_on(j, 1 - c).wait_recv()
        for cp in sends + passed:
            cp.wait_send()

    gathered = _pallas(
        body, name=name, in_specs=[ANY_SPEC], out_specs=ANY_SPEC, out_shape=jax.ShapeDtypeStruct((N_CHIPS, 2, R, C), src.dtype),
        scratch_shapes=[pltpu.SemaphoreType.DMA((3,))] * 4,
        compiler_params=pltpu.CompilerParams(has_side_effects=True))(src)

    tr = _tile(R, 512, 16)

    def place(chip_ref, gathered_ref, own_ref, out_ref):
        out_ref[0, 0] = own_ref[0]

    grid_spec = pltpu.PrefetchScalarGridSpec(
        num_scalar_prefetch=1, grid=(2, R // tr),
        in_specs=[ANY_SPEC, pl.BlockSpec((1, tr, C), lambda h, i, chip_ref: (h, i, 0))],
        out_specs=pl.BlockSpec((1, 1, tr, C), lambda h, i, chip_ref: (chip_ref[0], h, i, 0)))
    return _pallas(place, name=name + "_own", grid_spec=grid_spec, out_shape=jax.ShapeDtypeStruct(gathered.shape, gathered.dtype),
                   input_output_aliases={1: 0}, compiler_params=_params("arbitrary", "arbitrary"))(chip, gathered, src)


def _chip_scatter(src, *, name):
    _, R, C = src.shape

    def body(src_ref, out_ref, send_sems, recv_sems):
        x, y, c, chips = _chip_peers()
        me = 2 * x + y

        def copy(j, seg):
            px, py = chips[j]
            return pltpu.make_async_remote_copy(src_ref=src_ref.at[seg], dst_ref=out_ref.at[j], send_sem=send_sems.at[j],
                                                recv_sem=recv_sems.at[j], device_id=(px, py, c), device_id_type=MESH_IDS)

        sends = [copy(j, 2 * px + py) for j, (px, py) in enumerate(chips)]
        for cp in sends:
            cp.start()
        for j in range(3):
            copy(j, me).wait_recv()
        for cp in sends:
            cp.wait_send()

    return _pallas(
        body, name=name, in_specs=[ANY_SPEC], out_specs=ANY_SPEC, out_shape=jax.ShapeDtypeStruct((3, R, C), src.dtype),
        scratch_shapes=[pltpu.SemaphoreType.DMA((3,)), pltpu.SemaphoreType.DMA((3,))],
        compiler_params=pltpu.CompilerParams(has_side_effects=True))(src)


PAIR_CHUNKS = 4


def _pair_swap(src, *, halves, name):
    R, C = src.shape[-2:]
    n = N_CHIPS if halves else 1
    rc = R // PAIR_CHUNKS
    assert rc * PAIR_CHUNKS == R and rc % 16 == 0, R

    def body(src_ref, out_ref, send_sems, recv_sems):
        x, y, c = lax.axis_index("x"), lax.axis_index("y"), lax.axis_index("c")
        copies = []
        for k in range(n):
            for r in range(PAIR_CHUNKS):
                rows = pl.ds(r * rc, rc)
                s = src_ref.at[k, 1 - c, rows] if halves else src_ref.at[rows]
                d = out_ref.at[k, rows] if halves else out_ref.at[rows]
                i = k * PAIR_CHUNKS + r
                copies.append(pltpu.make_async_remote_copy(src_ref=s, dst_ref=d, send_sem=send_sems.at[i], recv_sem=recv_sems.at[i],
                                                           device_id=(x, y, 1 - c), device_id_type=MESH_IDS))
        for cp in copies:
            cp.start()
        for cp in copies:
            cp.wait_recv()
        for cp in copies:
            cp.wait_send()

    shape = (N_CHIPS, R, C) if halves else (R, C)
    return _pallas(
        body, name=name, in_specs=[ANY_SPEC], out_specs=ANY_SPEC, out_shape=jax.ShapeDtypeStruct(shape, src.dtype),
        scratch_shapes=[pltpu.SemaphoreType.DMA((n * PAIR_CHUNKS,)), pltpu.SemaphoreType.DMA((n * PAIR_CHUNKS,))],
        compiler_params=pltpu.CompilerParams(has_side_effects=True))(src)


def _pair_add(g4, got, c, *, name):
    _, _, R, C = g4.shape
    tr = _tile(R, 512, 16)

    def body(c_ref, mine_ref, got_ref, o_ref, ob_ref):
        s = mine_ref[0, 0] + got_ref[0].astype(F32)
        o_ref[0] = s
        ob_ref[0] = s.astype(BF16)

    blk = pl.BlockSpec((1, tr, C), lambda k, i, c_ref: (k, i, 0))
    grid_spec = pltpu.PrefetchScalarGridSpec(
        num_scalar_prefetch=1, grid=(N_CHIPS, R // tr),
        in_specs=[pl.BlockSpec((1, 1, tr, C), lambda k, i, c_ref: (k, c_ref[0], i, 0)), blk], out_specs=[blk, blk])
    return _pallas(body, name=name, grid_spec=grid_spec,
                   out_shape=[jax.ShapeDtypeStruct((N_CHIPS, R, C), F32), jax.ShapeDtypeStruct((N_CHIPS, R, C), BF16)],
                   compiler_params=_params("parallel", "parallel"))(c, g4, got)


def _chip_sum(pair_sum, landed, me, *, name):
    _, R, C = pair_sum.shape
    tr = _tile(R, 512, 16)

    def body(me_ref, own_ref, landed_ref, o_ref):
        acc = own_ref[0]
        for j in range(3):
            acc = acc + landed_ref[j].astype(F32)
        o_ref[...] = acc

    grid_spec = pltpu.PrefetchScalarGridSpec(
        num_scalar_prefetch=1, grid=(R // tr,),
        in_specs=[pl.BlockSpec((1, tr, C), lambda i, me_ref: (me_ref[0], i, 0)), pl.BlockSpec((3, tr, C), lambda i, me_ref: (0, i, 0))],
        out_specs=pl.BlockSpec((tr, C), lambda i, me_ref: (i, 0)))
    return _pallas(body, name=name, grid_spec=grid_spec, out_shape=jax.ShapeDtypeStruct((R, C), F32),
                   compiler_params=_params("parallel"))(me, pair_sum, landed)


def _join_halves(mine, got, c, *, name):
    R, C = mine.shape
    tr = _tile(R, 512, 16)

    def body(c_ref, mine_ref, got_ref, o_ref):
        use_mine = pl.program_id(0) == c_ref[0]
        o_ref[0] = jnp.where(use_mine, mine_ref[...], got_ref[...])

    blk = pl.BlockSpec((tr, C), lambda h, i, c_ref: (i, 0))
    grid_spec = pltpu.PrefetchScalarGridSpec(num_scalar_prefetch=1, grid=(2, R // tr), in_specs=[blk, blk],
                                             out_specs=pl.BlockSpec((1, tr, C), lambda h, i, c_ref: (h, i, 0)))
    return _pallas(body, name=name, grid_spec=grid_spec, out_shape=jax.ShapeDtypeStruct((2, R, C), mine.dtype),
                   compiler_params=_params("parallel", "parallel"))(c, mine, got).reshape(2 * R, C)


def _gather8(s, *, name):
    R, C = s.shape

    def body(s_ref, out_ref, send_sems, recv_sems):
        x, y, c = lax.axis_index("x"), lax.axis_index("y"), lax.axis_index("c")
        me = 4 * x + 2 * y + c
        flips = [(dx, dy, dc) for dx in (0, 1) for dy in (0, 1) for dc in (0, 1)][1:]
        out_ref[me] = s_ref[...]

        def copy(j, slot):
            dx, dy, dc = flips[j]
            return pltpu.make_async_remote_copy(src_ref=s_ref, dst_ref=out_ref.at[slot], send_sem=send_sems.at[j],
                                                recv_sem=recv_sems.at[j], device_id=(x ^ dx, y ^ dy, c ^ dc), device_id_type=MESH_IDS)

        sends = [copy(j, me) for j in range(7)]
        for cp in sends:
            cp.start()
        for j, (dx, dy, dc) in enumerate(flips):
            copy(j, 4 * (x ^ dx) + 2 * (y ^ dy) + (c ^ dc)).wait_recv()
        for cp in sends:
            cp.wait_send()

    vmem = pl.BlockSpec(memory_space=pltpu.VMEM)
    return _pallas(
        body, name=name, in_specs=[vmem], out_specs=vmem, out_shape=jax.ShapeDtypeStruct((8, R, C), s.dtype),
        scratch_shapes=[pltpu.SemaphoreType.DMA((7,)), pltpu.SemaphoreType.DMA((7,))],
        compiler_params=pltpu.CompilerParams(has_side_effects=True))(s)


def _sum_slots(a, *, name):
    n, R, C = a.shape
    tr = _tile(R, 512, 8)

    def body(a_ref, o_ref):
        acc = a_ref[0]
        for k in range(1, n):
            acc = acc + a_ref[k]
        o_ref[...] = acc

    return _pallas(body, name=name, grid=(R // tr,), in_specs=[pl.BlockSpec((n, tr, C), lambda i: (0, i, 0))],
                   out_specs=pl.BlockSpec((tr, C), lambda i: (i, 0)), out_shape=jax.ShapeDtypeStruct((R, C), a.dtype),
                   compiler_params=_params("parallel"))(a)


def _adamw(w, g, m, v, *, name):
    R, C = w.shape
    tr = _tile(R, max(8, (1 << 18) // C // 8 * 8), 8)
    c1 = 1.0 / (1.0 - ADAM_B1 ** ADAM_STEP)
    c2 = 1.0 / (1.0 - ADAM_B2 ** ADAM_STEP)

    def body(w_ref, g_ref, m_ref, v_ref, d_ref, nm_ref, nv_ref):
        gv = g_ref[...]
        nm = ADAM_B1 * m_ref[...] + (1.0 - ADAM_B1) * gv
        nv = ADAM_B2 * v_ref[...] + (1.0 - ADAM_B2) * (gv * gv)
        d_ref[...] = -ADAM_LR * ((nm * c1) / (jnp.sqrt(nv * c2) + ADAM_EPS) + ADAM_WD * w_ref[...])
        nm_ref[...] = nm
        nv_ref[...] = nv

    blk = pl.BlockSpec((tr, C), lambda i: (i, 0))
    shape = jax.ShapeDtypeStruct((R, C), F32)
    return _pallas(body, name=name, grid=(R // tr,), in_specs=[blk] * 4, out_specs=[blk] * 3, out_shape=[shape] * 3,
                   compiler_params=_params("parallel"))(w, g, m, v)


def kernel(x, mem, w_in, g_mix, a_gq, a_gkv, a_wuq, a_wukv, b_lb, b_gout, c_sink, rel_bias, w_br_a, w_br_b, w_br_c, w_out, g_x, g_mem, x_wq, x_wkv, x_wo, g_ffn, f_w1, f_w3, f_w2, g_final, loss_target, m_w_in, m_g_mix, m_a_gq, m_a_gkv, m_a_wuq, m_a_wukv, m_b_lb, m_b_gout, m_c_sink, m_rel_bias, m_w_br_a, m_w_br_b, m_w_br_c, m_w_out, m_g_x, m_g_mem, m_x_wq, m_x_wkv, m_x_wo, m_g_ffn, m_f_w1, m_f_w3, m_f_w2, m_g_final, v_w_in, v_g_mix, v_a_gq, v_a_gkv, v_a_wuq, v_a_wukv, v_b_lb, v_b_gout, v_c_sink, v_rel_bias, v_w_br_a, v_w_br_b, v_w_br_c, v_w_out, v_g_x, v_g_mem, v_x_wq, v_x_wkv, v_x_wo, v_g_ffn, v_f_w1, v_f_w3, v_f_w2, v_g_final):
    ws = dict(zip(WEIGHT_NAMES, (w_in, g_mix, a_gq, a_gkv, a_wuq, a_wukv, b_lb, b_gout, c_sink, rel_bias, w_br_a, w_br_b, w_br_c,
                                 w_out, g_x, g_mem, x_wq, x_wkv, x_wo, g_ffn, f_w1, f_w3, f_w2, g_final)))
    ms = dict(zip(WEIGHT_NAMES, (m_w_in, m_g_mix, m_a_gq, m_a_gkv, m_a_wuq, m_a_wukv, m_b_lb, m_b_gout, m_c_sink, m_rel_bias,
                                 m_w_br_a, m_w_br_b, m_w_br_c, m_w_out, m_g_x, m_g_mem, m_x_wq, m_x_wkv, m_x_wo, m_g_ffn,
                                 m_f_w1, m_f_w3, m_f_w2, m_g_final)))
    vs = dict(zip(WEIGHT_NAMES, (v_w_in, v_g_mix, v_a_gq, v_a_gkv, v_a_wuq, v_a_wukv, v_b_lb, v_b_gout, v_c_sink, v_rel_bias,
                                 v_w_br_a, v_w_br_b, v_w_br_c, v_w_out, v_g_x, v_g_mem, v_x_wq, v_x_wkv, v_x_wo, v_g_ffn,
                                 v_f_w1, v_f_w3, v_f_w2, v_g_final)))
    sharded = [n for n, _ in SHARDED]
    axis_of = dict(SHARDED)

    def wire(n):
        return lax.bitcast_convert_type(ws[n], BF16) if n == "b_lb" else ws[n].astype(BF16)

    core = lax.axis_index("c").astype(jnp.int32).reshape(1)
    chip = (2 * lax.axis_index("x") + lax.axis_index("y")).astype(jnp.int32).reshape(1)
    wire_shapes = [wire(n).shape for n in sharded]
    packed = _pack([wire(n) for n in sharded], PACK_COLS, PACK_ALIGN)
    gathered = _chip_gather(packed.reshape(2, packed.shape[0] // 2, PACK_COLS), chip, name="gather_weights")
    per_chip = [_unpack(gathered[k], wire_shapes) for k in range(N_CHIPS)]
    full = dict(ws)
    for i, n in enumerate(sharded):
        parts = [per_chip[k][i] for k in range(N_CHIPS)]
        if n == "b_lb":
            parts = [lax.bitcast_convert_type(p, F32) for p in parts]
        full[n] = jnp.concatenate(parts, axis=axis_of[n])

    loss, grad_x, grads = _local_step(x[0], mem[0], loss_target[0], full)

    pieces = []
    for k in range(N_CHIPS):
        pieces += _pack_pieces([jnp.split(grads[n], N_CHIPS, axis=axis_of[n])[k] for n in sharded], PACK_COLS, PACK_ALIGN)
    g4 = jnp.concatenate(pieces, axis=0).reshape(N_CHIPS, 2, -1, PACK_COLS)
    got = _pair_swap(g4, halves=True, name="reduce_pair_swap")
    pair_sum, pair_sum_wire = _pair_add(g4, got, core, name="reduce_pair_add")
    landed = _chip_scatter(pair_sum_wire, name="reduce_chip_scatter")
    mine = _chip_sum(pair_sum, landed, chip, name="reduce_chip_sum")
    g_shard = _join_halves(mine, _pair_swap(mine, halves=False, name="reduce_pair_join"), core, name="reduce_join_halves")

    small = _pack_small([grads[n] for n in REPLICATED] + [loss[0, 0:1]])
    small_sum = _sum_slots(_gather8(small, name="gather_small"), name="sum_small")
    small_grads = _unpack(small_sum, [ws[n].shape for n in REPLICATED] + [(1,)])
    loss_total = small_grads.pop()[0]

    shard_shapes = [ws[n].shape for n in sharded]
    out = {}
    for n, gr in zip(sharded, _unpack(g_shard, shard_shapes)):
        flat2 = lambda a: a.reshape(-1, a.shape[-1])
        d, nm, nv = _adamw(flat2(ws[n]), flat2(gr), flat2(ms[n]), flat2(vs[n]), name="adamw_" + n)
        out[n] = (gr, d.reshape(gr.shape), nm.reshape(gr.shape), nv.reshape(gr.shape))
    pk_s = lambda d: _pack_small([d[n] for n in REPLICATED])
    rep_shapes = [ws[n].shape for n in REPLICATED]
    gs_flat = _pack_small(small_grads)
    ds_flat, ms_flat, vs_flat = _adamw(pk_s(ws), gs_flat, pk_s(ms), pk_s(vs), name="adamw_replicated")
    for n, gr, d, nm, nv in zip(REPLICATED, small_grads, _unpack(ds_flat, rep_shapes), _unpack(ms_flat, rep_shapes),
                                _unpack(vs_flat, rep_shapes)):
        out[n] = (gr, d, nm, nv)
    return (loss_total, grad_x[None], *[out[n][0] for n in WEIGHT_NAMES], *[out[n][1] for n in WEIGHT_NAMES],
            *[out[n][2] for n in WEIGHT_NAMES], *[out[n][3] for n in WEIGHT_NAMES])
```

```python
import math

import jax
import jax.numpy as jnp
from jax import lax
from jax.experimental import pallas as pl
from jax.experimental.pallas import tpu as pltpu

F32 = jnp.float32
BF16 = jnp.bfloat16

D_MODEL = 1024
DEPTH = 2
EPS = 1e-6
MASK_VALUE = -1e30
TINY = 1e-30
A_HEADS, A_NOPE, A_ROPE, A_V = 8, 64, 32, 64
A_QK = A_NOPE + A_ROPE
A_Q_RANK, A_KV_RANK = 384, 256
ROPE_THETA = 10000.0
B_HEADS, B_DK, B_DV, B_CHUNK = 8, 128, 64, 16
C_HEADS, C_KV_HEADS, C_DH, C_WINDOW, C_BLOCK = 8, 2, 64, 128, 128
REL_BUCKETS, REL_MAX_DIST = 32, 128
X_HEADS, X_DH = 4, 256
D_FF = 2816
ADAM_LR, ADAM_B1, ADAM_B2, ADAM_EPS, ADAM_WD, ADAM_STEP = 0.001, 0.9, 0.999, 1e-08, 0.01, 10

LANES = 128
VMEM_LIMIT = 56 * 1024 * 1024
VMEM_LIMIT_WIDE = 62 * 1024 * 1024


def _pallas(body, **kw):
    return pl.pallas_call(body, **kw)


def _params(*sem):
    return pltpu.CompilerParams(dimension_semantics=sem, vmem_limit_bytes=VMEM_LIMIT)


def _tile(n, pref, unit=LANES):
    if n <= pref:
        return n
    t = (pref // unit) * unit
    while t > unit and n % t:
        t -= unit
    assert n % t == 0, (n, pref, unit)
    return t


def _dot(a, b, dims):
    return lax.dot_general(a, b, (dims, ((), ())), preferred_element_type=F32)


_NN = ((1,), (0,))
_NT = ((1,), (1,))
_TN = ((0,), (0,))


def _mm_tiles(M, N, K, mode):
    half_ff = D_FF // 2
    tm = _tile(M, half_ff if M % half_ff == 0 else (1024 if mode != "tn" and M >= 2048 else 512), LANES if mode == "tn" else 16)
    tn = _tile(N, half_ff if N % half_ff == 0 else 1024, 256 if N % 256 == 0 and N % half_ff else LANES)
    tk = K if K <= D_FF else _tile(K, 1024)
    return tm, tn, tk


def _mm(a, b, *, mode="nn", add=None, out_dtype=F32, tiles=None, name):
    if mode == "nn":
        (M, K), (K2, N) = a.shape, b.shape
    elif mode == "nt":
        (M, K), (N, K2) = a.shape, b.shape
    else:
        (K, M), (K2, N) = a.shape, b.shape
    assert K == K2, (a.shape, b.shape, mode)
    tm, tn, tk = tiles or _mm_tiles(M, N, K, mode)
    nk = K // tk
    dims = {"nn": _NN, "nt": _NT, "tn": _TN}[mode]
    a_spec = pl.BlockSpec((tk, tm), lambda i, j, k: (k, i)) if mode == "tn" else pl.BlockSpec((tm, tk), lambda i, j, k: (i, k))
    b_spec = pl.BlockSpec((tn, tk), lambda i, j, k: (j, k)) if mode == "nt" else pl.BlockSpec((tk, tn), lambda i, j, k: (k, j))
    o_spec = pl.BlockSpec((tm, tn), lambda i, j, k: (i, j))
    has_add = add is not None

    def body(*refs):
        if has_add:
            a_ref, b_ref, add_ref, o_ref, acc_ref = refs
        else:
            a_ref, b_ref, o_ref, acc_ref = refs
        k = pl.program_id(2)
        part = _dot(a_ref[...].astype(BF16), b_ref[...].astype(BF16), dims)

        @pl.when(k == 0)
        def _():
            acc_ref[...] = part

        @pl.when(k > 0)
        def _():
            acc_ref[...] += part

        @pl.when(k == nk - 1)
        def _():
            r = acc_ref[...]
            if has_add:
                r = r + add_ref[...].astype(F32)
            o_ref[...] = r.astype(out_dtype)

    ins = [a, b] + ([add] if has_add else [])
    in_specs = [a_spec, b_spec] + ([o_spec] if has_add else [])
    return _pallas(
        body, name=name, grid=(M // tm, N // tn, nk), in_specs=in_specs, out_specs=o_spec,
        out_shape=jax.ShapeDtypeStruct((M, N), out_dtype), scratch_shapes=[pltpu.VMEM((tm, tn), F32)],
        compiler_params=_params("parallel", "parallel", "arbitrary"),
    )(*ins)


def _rms(x, g, *, col=0, width=None, out_dtype=BF16, name):
    T = x.shape[0]
    width = x.shape[1] if width is None else width
    assert col % width == 0
    tm = _tile(T, 512, 8)
    cb = col // width

    def body(x_ref, g_ref, o_ref):
        xv = x_ref[...].astype(F32)
        r = lax.rsqrt(jnp.mean(xv * xv, axis=-1, keepdims=True) + EPS)
        o_ref[...] = (xv * r * g_ref[...]).astype(out_dtype)

    return _pallas(
        body, name=name, grid=(T // tm,),
        in_specs=[pl.BlockSpec((tm, width), lambda i: (i, cb)), pl.BlockSpec((1, width), lambda i: (0, 0))],
        out_specs=pl.BlockSpec((tm, width), lambda i: (i, 0)),
        out_shape=jax.ShapeDtypeStruct((T, width), out_dtype), compiler_params=_params("parallel"),
    )(x, g.reshape(1, width))


def _rms_bwd(x, g, dy, *, res=None, col=0, width=None, out_dtype=F32, name):
    T = x.shape[0]
    width = x.shape[1] if width is None else width
    assert col % width == 0
    tm = _tile(T, 512, 8)
    cb = col // width
    has_res = res is not None

    def body(*refs):
        if has_res:
            x_ref, g_ref, dy_ref, res_ref, dx_ref, dg_ref = refs
        else:
            x_ref, g_ref, dy_ref, dx_ref, dg_ref = refs
        xv = x_ref[...].astype(F32)
        r = lax.rsqrt(jnp.mean(xv * xv, axis=-1, keepdims=True) + EPS)
        xh = xv * r
        dyv = dy_ref[...].astype(F32)
        dxh = dyv * g_ref[...]
        dx = r * (dxh - xh * jnp.mean(dxh * xh, axis=-1, keepdims=True))
        if has_res:
            dx = dx + res_ref[...].astype(F32)
        dx_ref[...] = dx.astype(out_dtype)
        part = jnp.sum(dyv * xh, axis=0, keepdims=True)

        @pl.when(pl.program_id(0) == 0)
        def _():
            dg_ref[...] = part

        @pl.when(pl.program_id(0) > 0)
        def _():
            dg_ref[...] += part

    row = pl.BlockSpec((tm, width), lambda i: (i, 0))
    ins = [x, g.reshape(1, width), dy] + ([res] if has_res else [])
    in_specs = [pl.BlockSpec((tm, width), lambda i: (i, cb)), pl.BlockSpec((1, width), lambda i: (0, 0)), row] + ([row] if has_res else [])
    return _pallas(
        body, name=name, grid=(T // tm,), in_specs=in_specs,
        out_specs=[row, pl.BlockSpec((1, width), lambda i: (0, 0))],
        out_shape=[jax.ShapeDtypeStruct((T, width), out_dtype), jax.ShapeDtypeStruct((1, width), F32)],
        compiler_params=_params("arbitrary"),
    )(*ins)


def _rope_tables(T):
    half = A_ROPE // 2
    inv = ROPE_THETA ** (-jnp.arange(half, dtype=F32) / half)
    ang = jnp.arange(T, dtype=jnp.int32).astype(F32)[:, None] * inv[None, :]
    c32 = jnp.concatenate([jnp.cos(ang), jnp.cos(ang)], axis=-1)
    s32 = jnp.concatenate([jnp.sin(ang), jnp.sin(ang)], axis=-1)
    pad = A_PAD - A_QK
    cq = jnp.concatenate([jnp.ones((T, A_NOPE), F32), c32, jnp.ones((T, pad), F32)], axis=-1)
    sq = jnp.concatenate([jnp.zeros((T, A_NOPE), F32), s32, jnp.zeros((T, pad), F32)], axis=-1)
    ck = jnp.concatenate([c32, s32, jnp.zeros((T, LANES - 2 * A_ROPE), F32)], axis=-1)
    ck_t = jnp.concatenate([c32, s32], axis=-1).T
    return cq, sq, ck, ck_t


def _rope_swap_cols(w):
    half = A_ROPE // 2
    return jnp.concatenate([-w[..., half:], w[..., :half]], axis=-1)


def _rope_unswap_cols(g):
    half = A_ROPE // 2
    return jnp.concatenate([g[..., half:], -g[..., :half]], axis=-1)


A_PAD = LANES
A_W = A_HEADS * A_PAD
LOG2E = 1.4426950408889634
LN2 = 0.6931471805599453
Q_SCALE = A_QK ** -0.5 * LOG2E


def _qrope(cqn, wq2, cq, sq, *, name):
    T, R = cqn.shape
    W = A_W
    tm = _tile(T, 512)

    def body(x_ref, w_ref, c_ref, s_ref, o_ref, ot_ref):
        q2 = _dot(x_ref[...], w_ref[...], _NN)
        c = jnp.concatenate([c_ref[...]] * A_HEADS, axis=1)
        s = jnp.concatenate([s_ref[...]] * A_HEADS, axis=1)
        q = (q2[:, 0:W] * c + q2[:, W:2 * W] * s) * Q_SCALE
        o_ref[...] = q.astype(BF16)
        ot_ref[...] = q.T.astype(BF16)

    row = pl.BlockSpec((tm, W), lambda i: (i, 0))
    tab = pl.BlockSpec((tm, A_PAD), lambda i: (i, 0))
    return _pallas(body, name=name, grid=(T // tm,),
                   in_specs=[pl.BlockSpec((tm, R), lambda i: (i, 0)), pl.BlockSpec((R, 2 * W), lambda i: (0, 0)), tab, tab],
                   out_specs=[row, pl.BlockSpec((W, tm), lambda i: (0, i))],
                   out_shape=[jax.ShapeDtypeStruct((T, W), BF16), jax.ShapeDtypeStruct((W, T), BF16)],
                   compiler_params=_params("parallel"))(cqn, wq2, cq, sq)


def _kprep(kv, za, ck, *, name):
    T = kv.shape[0]
    tm = _tile(T, 512)

    def body(kv_ref, kr_ref, ck_ref, k_ref, vxt_ref):
        t = kr_ref[...] * ck_ref[...]
        krope = (t[:, 0:A_ROPE] + t[:, A_ROPE:2 * A_ROPE]).astype(BF16)
        one = (lax.broadcasted_iota(jnp.int32, (A_PAD - A_V, tm), 0) == 0).astype(BF16)
        for h in range(A_HEADS):
            k_ref[:, A_PAD * h:A_PAD * h + A_NOPE] = kv_ref[:, A_NOPE * h:A_NOPE * (h + 1)]
            k_ref[:, A_PAD * h + A_NOPE:A_PAD * h + A_QK] = krope
            k_ref[:, A_PAD * h + A_QK:A_PAD * (h + 1)] = jnp.zeros((tm, A_PAD - A_QK), BF16)
            vxt_ref[A_PAD * h + A_V:A_PAD * (h + 1), :] = one
        vt = kv_ref[:, 512:1024].astype(F32).T.astype(BF16)
        for h in range(A_HEADS):
            vxt_ref[A_PAD * h:A_PAD * h + A_V, :] = vt[A_V * h:A_V * (h + 1), :]

    wide = pl.BlockSpec((tm, A_W), lambda i: (i, 0))
    return _pallas(
        body, name=name, grid=(T // tm,),
        in_specs=[wide, pl.BlockSpec((tm, LANES), lambda i: (i, 3)), pl.BlockSpec((tm, LANES), lambda i: (i, 0))],
        out_specs=[wide, pl.BlockSpec((A_W, tm), lambda i: (0, i))],
        out_shape=[jax.ShapeDtypeStruct((T, A_W), BF16), jax.ShapeDtypeStruct((A_W, T), BF16)],
        compiler_params=_params("parallel"))(kv, za, ck)


def _kprep_bwd(dkt, ck_t, *, name):
    T = dkt.shape[1]
    tc = _tile(T, 512)

    def body(dk_ref, ck_ref, dn_ref, dr_ref):
        acc = jnp.zeros((A_ROPE, tc), F32)
        for h in range(A_HEADS):
            dn_ref[A_NOPE * h:A_NOPE * (h + 1), :] = dk_ref[A_PAD * h:A_PAD * h + A_NOPE, :].astype(BF16)
            acc = acc + dk_ref[A_PAD * h + A_NOPE:A_PAD * h + A_QK, :]
        dr_ref[0:A_ROPE, :] = (acc * ck_ref[0:A_ROPE, :]).astype(BF16)
        dr_ref[A_ROPE:2 * A_ROPE, :] = (acc * ck_ref[A_ROPE:2 * A_ROPE, :]).astype(BF16)
        dr_ref[2 * A_ROPE:LANES, :] = jnp.zeros((LANES - 2 * A_ROPE, tc), BF16)

    col = lambda r: pl.BlockSpec((r, tc), lambda i: (0, i))
    return _pallas(
        body, name=name, grid=(T // tc,), in_specs=[col(A_W), col(2 * A_ROPE)], out_specs=[col(512), col(LANES)],
        out_shape=[jax.ShapeDtypeStruct((512, T), BF16), jax.ShapeDtypeStruct((LANES, T), BF16)],
        compiler_params=_params("parallel"))(dkt, ck_t)


def _flash_fwd(qs, k, vxt, *, name):
    T = qs.shape[0]
    tq, tk = _tile(T, 512), _tile(T, 2048)
    nk = T // tk
    H, P, DV = A_HEADS, A_PAD, A_V

    def body(q_ref, k_ref, v_ref, o_ref, lse_ref, m_sc, acc_sc):
        j = pl.program_id(1)

        @pl.when(j == 0)
        def _():
            m_sc[...] = jnp.full(m_sc.shape, -jnp.inf, F32)
            acc_sc[...] = jnp.zeros(acc_sc.shape, F32)

        def scores(h):
            return _dot(k_ref[:, P * h:P * (h + 1)], q_ref[:, P * h:P * (h + 1)], _NT)

        st_next = scores(0)
        for h in range(H):
            st = st_next
            if h + 1 < H:
                st_next = scores(h + 1)
            m_prev = m_sc[h]
            m_new = jnp.maximum(m_prev, jnp.max(st, axis=0, keepdims=True))
            pt = jnp.exp2(st - m_new).astype(BF16)
            acc_sc[h] = jnp.exp2(m_prev - m_new) * acc_sc[h] + _dot(v_ref[P * h:P * (h + 1), :], pt, _NN)
            m_sc[h] = m_new

        @pl.when(j == nk - 1)
        def _():
            for h in range(H):
                acc = acc_sc[h]
                l = acc[DV:DV + 1, :]
                o_ref[:, DV * h:DV * (h + 1)] = (acc[0:DV, :] / l).T
                lse_ref[h] = m_sc[h] + jnp.log2(l)

    return _pallas(
        body, name=name, grid=(T // tq, nk),
        in_specs=[pl.BlockSpec((tq, A_W), lambda i, j: (i, 0)), pl.BlockSpec((tk, A_W), lambda i, j: (j, 0)),
                  pl.BlockSpec((A_W, tk), lambda i, j: (0, j))],
        out_specs=[pl.BlockSpec((tq, H * DV), lambda i, j: (i, 0)), pl.BlockSpec((H, 1, tq), lambda i, j: (0, 0, i))],
        out_shape=[jax.ShapeDtypeStruct((T, H * DV), F32), jax.ShapeDtypeStruct((H, 1, T), F32)],
        scratch_shapes=[pltpu.VMEM((H, 1, tq), F32), pltpu.VMEM((H, P, tq), F32)],
        compiler_params=_params("parallel", "arbitrary"))(qs, k, vxt)


def _attn_delta(o, do, *, name):
    T = o.shape[0]
    tm = _tile(T, 512, 8)

    def body(o_ref, do_ref, d_ref):
        prod = o_ref[...] * do_ref[...].astype(F32)
        for h in range(A_HEADS):
            d_ref[h] = jnp.sum(prod[:, A_V * h:A_V * (h + 1)], axis=-1, keepdims=True)

    row = pl.BlockSpec((tm, A_HEADS * A_V), lambda i: (i, 0))
    return _pallas(body, name=name, grid=(T // tm,), in_specs=[row, row],
                   out_specs=pl.BlockSpec((A_HEADS, tm, 1), lambda i: (0, i, 0)),
                   out_shape=jax.ShapeDtypeStruct((A_HEADS, T, 1), F32), compiler_params=_params("parallel"))(o, do)


def _flash_bwd(qs, qst, k, kv, do, dot_, lse2, delta, *, tiles=None, name):
    T = qs.shape[0]
    tq, tk = tiles or (_tile(T, 1024), _tile(T, 1024))
    nq, nk = T // tq, T // tk
    H, P, DV = A_HEADS, A_PAD, A_V

    def body(q_ref, qt_ref, k_ref, v_ref, do_ref, dot_ref, lse_ref, delta_ref, dq_ref, dkt_ref, dvt_ref):
        i = pl.program_id(1)

        @pl.when(i == 0)
        def _():
            dkt_ref[...] = jnp.zeros(dkt_ref.shape, F32)
            dvt_ref[...] = jnp.zeros(dvt_ref.shape, F32)

        for h in range(H):
            s = _dot(q_ref[:, P * h:P * (h + 1)], k_ref[:, P * h:P * (h + 1)], _NT)
            dp = _dot(do_ref[:, DV * h:DV * (h + 1)], v_ref[:, DV * h:DV * (h + 1)], _NT)
            p = jnp.exp2(s - lse_ref[h])
            ds = (p * (dp - delta_ref[h])).astype(BF16)
            pb = p.astype(BF16)
            dq_ref[0, :, P * h:P * (h + 1)] = _dot(ds, k_ref[:, P * h:P * (h + 1)], _NN).astype(BF16)
            dkt_ref[P * h:P * (h + 1), :] += _dot(qt_ref[P * h:P * (h + 1), :], ds, _NN)
            dvt_ref[DV * h:DV * (h + 1), :] += _dot(dot_ref[DV * h:DV * (h + 1), :], pb, _NN)

        @pl.when(i == nq - 1)
        def _():
            dkt_ref[...] = dkt_ref[...] * LN2

    qrow = lambda w: pl.BlockSpec((tq, w), lambda j, i: (i, 0))
    qcol = lambda r: pl.BlockSpec((r, tq), lambda j, i: (0, i))
    stat = pl.BlockSpec((H, tq, 1), lambda j, i: (0, i, 0))
    acc = lambda r: pl.BlockSpec((r, tk), lambda j, i: (0, j), pipeline_mode=pl.Buffered(1))
    return _pallas(
        body, name=name, grid=(nk, nq),
        in_specs=[qrow(A_W), qcol(A_W), pl.BlockSpec((tk, A_W), lambda j, i: (j, 0)), pl.BlockSpec((tk, H * DV), lambda j, i: (j, 1)),
                  qrow(H * DV), qcol(H * DV), stat, stat],
        out_specs=[pl.BlockSpec((1, tq, A_W), lambda j, i: (j, i, 0)), acc(A_W), acc(H * DV)],
        out_shape=[jax.ShapeDtypeStruct((nk, T, A_W), BF16), jax.ShapeDtypeStruct((A_W, T), F32),
                   jax.ShapeDtypeStruct((H * DV, T), F32)],
        compiler_params=pltpu.CompilerParams(dimension_semantics=("parallel", "arbitrary"), vmem_limit_bytes=VMEM_LIMIT_WIDE),
    )(qs, qst, k, kv, do, dot_, lse2, delta)


def _dq_sum(dq_part, cq, sq, *, name):
    n, T, W = dq_part.shape
    tm = _tile(T, 256, 16)

    def body(p_ref, c_ref, s_ref, o_ref):
        acc = p_ref[0].astype(F32)
        for j in range(1, n):
            acc = acc + p_ref[j].astype(F32)
        acc = acc * (A_QK ** -0.5)
        o_ref[:, 0:W] = (acc * jnp.concatenate([c_ref[...]] * A_HEADS, axis=1)).astype(BF16)
        o_ref[:, W:2 * W] = (acc * jnp.concatenate([s_ref[...]] * A_HEADS, axis=1)).astype(BF16)

    row = pl.BlockSpec((tm, A_PAD), lambda i: (i, 0))
    return _pallas(body, name=name, grid=(T // tm,), in_specs=[pl.BlockSpec((n, tm, W), lambda i: (0, i, 0)), row, row],
                   out_specs=pl.BlockSpec((tm, 2 * W), lambda i: (i, 0)), out_shape=jax.ShapeDtypeStruct((T, 2 * W), BF16),
                   compiler_params=_params("parallel"))(dq_part, cq, sq)


HB = 8 * B_CHUNK


def _chunk_masks(reverse):
    r = lax.broadcasted_iota(jnp.int32, (HB, HB), 0)
    c = lax.broadcasted_iota(jnp.int32, (HB, HB), 1)
    same = (r // B_CHUNK) == (c // B_CHUNK)
    incl = same & ((c >= r) if reverse else (c <= r))
    return same, incl


def _mask_mm(mask, x):
    hi = x.astype(BF16)
    lo = (x - hi.astype(F32)).astype(BF16)
    return _dot(mask, hi, _NN) + _dot(mask, lo, _NN)


def _hgrn_gates(q, z, lb, reverse):
    same, incl = _chunk_masks(reverse)
    sg = jax.nn.sigmoid(z)
    f = lb + (1.0 - lb) * sg
    lf = jnp.log(jnp.maximum(f, TINY))
    kk = (1.0 - lb) * jax.nn.sigmoid(-z)
    b = _mask_mm(incl.astype(BF16), lf)
    edge = 0 if reverse else B_CHUNK - 1
    btot = jnp.concatenate([jnp.broadcast_to(b[B_CHUNK * c + edge:B_CHUNK * c + edge + 1, :], (B_CHUNK, b.shape[1]))
                            for c in range(HB // B_CHUNK)], axis=0)
    eb, enb, er, dec = jnp.exp(b), jnp.exp(-b), jnp.exp(btot - b), jnp.exp(btot)
    return dict(same=same, incl=incl, sg=sg, f=f, kk=kk, eb=eb, enb=enb, er=er, dec=dec,
                qd=q * eb, ki=kk * enb, ke=kk * er)


def _hgrn_specs(T, reverse, gate_reverse):
    nb = T // HB
    blk = (lambda i: nb - 1 - i) if reverse else (lambda i: i)
    wide = B_HEADS * B_DK
    return nb, blk, [
        pl.BlockSpec((HB, wide), lambda i: (blk(i), 0)),
        pl.BlockSpec((HB, wide), lambda i: (blk(i), 2 if gate_reverse else 1)),
        pl.BlockSpec((HB, B_HEADS * B_DV), lambda i: (blk(i), 6)),
        pl.BlockSpec((1, wide), lambda i: (0, 0)),
    ]


def _hk(h):
    return slice(B_DK * h, B_DK * (h + 1))


def _hv(h):
    return slice(B_DV * h, B_DV * (h + 1))


def _crows(c):
    return slice(B_CHUNK * c, B_CHUNK * (c + 1))


def _chunk_selectors():
    r = lax.broadcasted_iota(jnp.int32, (HB, 1), 0) // B_CHUNK
    l = lax.broadcasted_iota(jnp.int32, (1, HB), 1) // B_CHUNK
    return [r == c for c in range(8)], [l == c for c in range(8)]


def _hgrn_fwd(zb, lb, *, reverse, name):
    T = zb.shape[0]
    nb, blk, in_specs = _hgrn_specs(T, reverse, reverse)
    order = range(7, -1, -1) if reverse else range(8)
    heads = range(B_HEADS)

    def body(q_ref, z_ref, v_ref, lb_ref, o_ref, st_ref, s_sc):
        @pl.when(pl.program_id(0) == 0)
        def _():
            s_sc[...] = jnp.zeros(s_sc.shape, F32)

        g = _hgrn_gates(q_ref[...], z_ref[...], lb_ref[...], reverse)
        v = v_ref[...].astype(BF16)
        qd, ki, ke = g["qd"].astype(BF16), g["ki"].astype(BF16), g["ke"].astype(BF16)
        dec = g["dec"]
        in_chunk_rows, in_chunk_lanes = _chunk_selectors()
        o_intra, upd = [], []
        for h in heads:
            a = jnp.where(g["incl"], _dot(qd[:, _hk(h)], ki[:, _hk(h)], _NT), 0.0)
            o_intra.append(_dot(a.astype(BF16), v[:, _hv(h)], _NN))
            vt = v[:, _hv(h)].T
            lhs = jnp.concatenate([jnp.where(in_chunk_lanes[c], vt, 0) for c in range(8)], axis=0)
            upd.append(_dot(lhs, ke[:, _hk(h)], _NN))
        st = [s_sc[h] for h in heads]
        snap = [[None] * 8 for _ in heads]
        for c in order:
            for h in heads:
                snap[h][c] = st[h]
                st[h] = st[h] * dec[B_CHUNK * c:B_CHUNK * c + 1, _hk(h)] + upd[h][B_DV * c:B_DV * (c + 1), :]
        for h in heads:
            s_sc[h] = st[h]
            for c in range(8):
                st_ref[h, c] = snap[h][c]
            qd_big = jnp.concatenate([jnp.where(in_chunk_rows[c], qd[:, _hk(h)], 0) for c in range(8)], axis=1)
            states = jnp.concatenate([snap[h][c].astype(BF16) for c in range(8)], axis=1)
            o_ref[h] = o_intra[h] + _dot(qd_big, states, _NT)

    return _pallas(
        body, name=name, grid=(nb,), in_specs=in_specs,
        out_specs=[pl.BlockSpec((B_HEADS, HB, B_DV), lambda i: (0, blk(i), 0)),
                   pl.BlockSpec((B_HEADS, 8, B_DV, B_DK), lambda i: (0, blk(i), 0, 0))],
        out_shape=[jax.ShapeDtypeStruct((B_HEADS, T, B_DV), F32),
                   jax.ShapeDtypeStruct((B_HEADS, T // B_CHUNK, B_DV, B_DK), F32)],
        scratch_shapes=[pltpu.VMEM((B_HEADS, B_DV, B_DK), F32)],
        compiler_params=_params("arbitrary"))(zb, zb, zb, lb)


def _hgrn_bwd(zb, lb, do, states, *, reverse, name):
    T = zb.shape[0]
    nb, blk, in_specs = _hgrn_specs(T, not reverse, reverse)
    order = range(8) if reverse else range(7, -1, -1)
    heads = range(B_HEADS)

    def body(q_ref, z_ref, v_ref, lb_ref, do_ref, st_ref, dq_ref, dz_ref, dv_ref, dlb_ref, ds_sc):
        @pl.when(pl.program_id(0) == 0)
        def _():
            ds_sc[...] = jnp.zeros(ds_sc.shape, F32)
            dlb_ref[...] = jnp.zeros(dlb_ref.shape, F32)

        lb = lb_ref[...]
        g = _hgrn_gates(q_ref[...], z_ref[...], lb, reverse)
        v = v_ref[...].astype(BF16)
        qd, ki, ke = g["qd"].astype(BF16), g["ki"].astype(BF16), g["ke"].astype(BF16)
        dec = g["dec"]
        dout = [do_ref[h].astype(BF16) for h in heads]
        in_chunk_rows, in_chunk_lanes = _chunk_selectors()
        _, incl_t = _chunk_masks(not reverse)
        rows_of = lambda x: jnp.concatenate([jnp.where(in_chunk_rows[c], x, 0) for c in range(8)], axis=1)
        dv_i, dqd_h, dki, upd = [], [], [], []
        for h in heads:
            qd_h, ki_h, v_h = qd[:, _hk(h)], ki[:, _hk(h)], v[:, _hv(h)]
            da = jnp.where(g["incl"], _dot(dout[h], v_h, _NT), 0.0).astype(BF16)
            at = jnp.where(incl_t, _dot(ki_h, qd_h, _NT), 0.0).astype(BF16)
            dat = jnp.where(incl_t, _dot(v_h, dout[h], _NT), 0.0).astype(BF16)
            dv_i.append(_dot(at, dout[h], _NN))
            dki.append(_dot(dat, qd_h, _NN))
            dot_t = dout[h].T
            lhs = jnp.concatenate([jnp.where(in_chunk_lanes[c], dot_t, 0) for c in range(8)], axis=0)
            upd.append(_dot(lhs, qd_h, _NN))
            saved = jnp.concatenate([st_ref[h, c].astype(BF16) for c in range(8)], axis=0)
            dqd_h.append(_dot(da, ki_h, _NN) + _dot(rows_of(dout[h]), saved, _NN))
        dst = [ds_sc[h] for h in heads]
        used = [[None] * 8 for _ in heads]
        for c in order:
            for h in heads:
                used[h][c] = dst[h]
                dst[h] = dst[h] * dec[B_CHUNK * c:B_CHUNK * c + 1, _hk(h)] + upd[h][B_DV * c:B_DV * (c + 1), :]
        dke_h, ddec_h = [], []
        for h in heads:
            ds_sc[h] = dst[h]
            used16 = [used[h][c].astype(BF16) for c in range(8)]
            dv_ref[h] = dv_i[h] + _dot(rows_of(ke[:, _hk(h)]), jnp.concatenate(used16, axis=1), _NT)
            dke_h.append(_dot(rows_of(v[:, _hv(h)]), jnp.concatenate(used16, axis=0), _NN))
            ddec_p = []
            for c in range(8):
                tot = jnp.sum(used[h][c] * st_ref[h, c], axis=0, keepdims=True) * dec[B_CHUNK * c:B_CHUNK * c + 1, _hk(h)]
                ddec_p.append(jnp.broadcast_to(tot, (B_CHUNK, B_DK)))
            ddec_h.append(jnp.concatenate(ddec_p, axis=0))
        dqd = jnp.concatenate(dqd_h, axis=1)
        dki = jnp.concatenate(dki, axis=1)
        dke = jnp.concatenate(dke_h, axis=1)
        db = dqd * g["qd"] - dki * g["ki"] - dke * g["ke"]
        masks = jnp.concatenate([incl_t.astype(BF16), g["same"].astype(BF16)], axis=1)
        dlf = _mask_mm(masks, jnp.concatenate([db, dke * g["ke"]], axis=0)) + jnp.concatenate(ddec_h, axis=1)
        dk = dki * g["enb"] + dke * g["er"]
        u = jnp.where(g["f"] > TINY, dlf / g["f"], 0.0) - dk
        sg = g["sg"]
        dq_ref[...] = dqd * g["eb"]
        dz_ref[...] = u * (1.0 - lb) * sg * (1.0 - sg)
        dlb_ref[...] += jnp.sum(u * (1.0 - sg), axis=0, keepdims=True)

    wide = pl.BlockSpec((HB, B_HEADS * B_DK), lambda i: (blk(i), 0))
    hm = pl.BlockSpec((B_HEADS, HB, B_DV), lambda i: (0, blk(i), 0))
    return _pallas(
        body, name=name, grid=(nb,),
        in_specs=in_specs + [hm, pl.BlockSpec((B_HEADS, 8, B_DV, B_DK), lambda i: (0, blk(i), 0, 0))],
        out_specs=[wide, wide, hm, pl.BlockSpec((1, B_HEADS * B_DK), lambda i: (0, 0))],
        out_shape=[jax.ShapeDtypeStruct((T, B_HEADS * B_DK), F32), jax.ShapeDtypeStruct((T, B_HEADS * B_DK), F32),
                   jax.ShapeDtypeStruct((B_HEADS, T, B_DV), F32), jax.ShapeDtypeStruct((1, B_HEADS * B_DK), F32)],
        scratch_shapes=[pltpu.VMEM((B_HEADS, B_DV, B_DK), F32)],
        compiler_params=_params("arbitrary"))(zb, zb, zb, lb, do, states)


def _hgrn_out(of, ob, zb, gout, *, name):
    T = zb.shape[0]
    tm = _tile(T, 512, 8)

    def body(of_ref, ob_ref, g_ref, gout_ref, y_ref):
        for h in range(B_HEADS):
            o = of_ref[h] + ob_ref[h]
            r = lax.rsqrt(jnp.mean(o * o, axis=-1, keepdims=True) + EPS)
            gh = g_ref[:, B_DV * h:B_DV * (h + 1)]
            y_ref[:, B_DV * h:B_DV * (h + 1)] = (o * r * gout_ref[...] * (gh * jax.nn.sigmoid(gh))).astype(BF16)

    hm = pl.BlockSpec((B_HEADS, tm, B_DV), lambda i: (0, i, 0))
    return _pallas(
        body, name=name, grid=(T // tm,),
        in_specs=[hm, hm, pl.BlockSpec((tm, 512), lambda i: (i, 7)), pl.BlockSpec((1, B_DV), lambda i: (0, 0))],
        out_specs=pl.BlockSpec((tm, 512), lambda i: (i, 0)),
        out_shape=jax.ShapeDtypeStruct((T, 512), BF16), compiler_params=_params("parallel"))(of, ob, zb, gout.reshape(1, B_DV))


def _hgrn_out_bwd(of, ob, zb, gout, dy, *, name):
    T = zb.shape[0]
    tm = _tile(T, 512, 8)

    def body(of_ref, ob_ref, g_ref, gout_ref, dy_ref, do_ref, dg_ref, dgo_ref):
        gout_v = gout_ref[...]
        acc = jnp.zeros((1, B_DV), F32)
        for h in range(B_HEADS):
            o = of_ref[h] + ob_ref[h]
            r = lax.rsqrt(jnp.mean(o * o, axis=-1, keepdims=True) + EPS)
            oh = o * r
            gh = g_ref[:, B_DV * h:B_DV * (h + 1)]
            sg = jax.nn.sigmoid(gh)
            dyh = dy_ref[:, B_DV * h:B_DV * (h + 1)].astype(F32)
            dn = dyh * (gh * sg)
            dg_ref[:, B_DV * h:B_DV * (h + 1)] = dyh * (oh * gout_v) * (sg * (1.0 + gh * (1.0 - sg)))
            dxh = dn * gout_v
            do_ref[h] = r * (dxh - oh * jnp.mean(dxh * oh, axis=-1, keepdims=True))
            acc = acc + jnp.sum(dn * oh, axis=0, keepdims=True)

        @pl.when(pl.program_id(0) == 0)
        def _():
            dgo_ref[...] = acc

        @pl.when(pl.program_id(0) > 0)
        def _():
            dgo_ref[...] += acc

    hm = pl.BlockSpec((B_HEADS, tm, B_DV), lambda i: (0, i, 0))
    row = pl.BlockSpec((tm, 512), lambda i: (i, 0))
    return _pallas(
        body, name=name, grid=(T // tm,),
        in_specs=[hm, hm, pl.BlockSpec((tm, 512), lambda i: (i, 7)), pl.BlockSpec((1, B_DV), lambda i: (0, 0)), row],
        out_specs=[hm, row, pl.BlockSpec((1, B_DV), lambda i: (0, 0))],
        out_shape=[jax.ShapeDtypeStruct((B_HEADS, T, B_DV), F32), jax.ShapeDtypeStruct((T, 512), F32),
                   jax.ShapeDtypeStruct((1, B_DV), F32)],
        compiler_params=_params("arbitrary"))(of, ob, zb, gout.reshape(1, B_DV), dy)


def _dzb_assemble(dq_f, dq_b, dzf, dzb_, dv_f, dv_b, dgate, *, name):
    T = dq_f.shape[0]
    tm = _tile(T, 256, 8)

    def body(qf, qb, zf, zr, vf, vr, dg, o_ref):
        o_ref[:, 0:1024] = (qf[...] + qb[...]).astype(BF16)
        o_ref[:, 1024:2048] = zf[...].astype(BF16)
        o_ref[:, 2048:3072] = zr[...].astype(BF16)
        for h in range(B_HEADS):
            o_ref[:, 3072 + B_DV * h:3072 + B_DV * (h + 1)] = (vf[h] + vr[h]).astype(BF16)
        o_ref[:, 3584:4096] = dg[...].astype(BF16)

    wide = pl.BlockSpec((tm, 1024), lambda i: (i, 0))
    hm = pl.BlockSpec((B_HEADS, tm, B_DV), lambda i: (0, i, 0))
    return _pallas(
        body, name=name, grid=(T // tm,),
        in_specs=[wide, wide, wide, wide, hm, hm, pl.BlockSpec((tm, 512), lambda i: (i, 0))],
        out_specs=pl.BlockSpec((tm, 4096), lambda i: (i, 0)),
        out_shape=jax.ShapeDtypeStruct((T, 4096), BF16), compiler_params=_params("parallel"))(dq_f, dq_b, dzf, dzb_, dv_f, dv_b, dgate)


C_SPAN = 3 * C_BLOCK
C_G = C_HEADS // C_KV_HEADS


def _t5_bucket(rel):
    nb = REL_BUCKETS // 2
    max_exact = nb // 2
    ret = (rel > 0).astype(jnp.int32) * nb
    n = jnp.abs(rel)
    large = max_exact + (jnp.log(jnp.maximum(n, 1).astype(F32) / max_exact)
                         / math.log(REL_MAX_DIST / max_exact) * (nb - max_exact)).astype(jnp.int32)
    large = jnp.minimum(large, nb - 1)
    return ret + jnp.where(n < max_exact, n, large)


def _swa_buckets():
    rel = jnp.arange(C_SPAN)[None, :] - C_BLOCK - jnp.arange(C_BLOCK)[:, None]
    return _t5_bucket(rel)


def _swa_specs(T):
    nb = T // C_BLOCK
    return nb, [
        pl.BlockSpec((C_BLOCK, 512), lambda n: (n, 0)),
        pl.BlockSpec((C_BLOCK, LANES), lambda n: (jnp.maximum(n - 1, 0), 4)),
        pl.BlockSpec((C_BLOCK, LANES), lambda n: (n, 4)),
        pl.BlockSpec((C_BLOCK, LANES), lambda n: (jnp.minimum(n + 1, nb - 1), 4)),
        pl.BlockSpec((C_BLOCK, LANES), lambda n: (jnp.maximum(n - 1, 0), 5)),
        pl.BlockSpec((C_BLOCK, LANES), lambda n: (n, 5)),
        pl.BlockSpec((C_BLOCK, LANES), lambda n: (jnp.minimum(n + 1, nb - 1), 5)),
        pl.BlockSpec((C_HEADS, C_BLOCK, C_SPAN), lambda n: (0, 0, 0)),
        pl.BlockSpec(memory_space=pltpu.SMEM),
    ]


def _swa_valid(n, T):
    qi = lax.broadcasted_iota(jnp.int32, (C_BLOCK, C_SPAN), 0)
    si = lax.broadcasted_iota(jnp.int32, (C_BLOCK, C_SPAN), 1)
    rel = si - C_BLOCK - qi
    kpos = (n - 1) * C_BLOCK + si
    return (jnp.abs(rel) <= C_WINDOW) & (kpos >= 0) & (kpos < T)


def _swa_softmax(raw, bias, valid, sink):
    s = raw * (C_DH ** -0.5) + bias
    s = jnp.where(valid, s, MASK_VALUE)
    m = jnp.maximum(jnp.max(s, axis=-1, keepdims=True), sink)
    e = jnp.exp(s - m)
    den = jnp.sum(e, axis=-1, keepdims=True) + jnp.exp(sink - m)
    return e / den, jnp.exp(sink - m) / den


def _swa_fwd(zc, bias, sink, *, name):
    T = zc.shape[0]
    nb, in_specs = _swa_specs(T)

    def body(q_ref, kp, kc, kn, vp, vc, vn, bias_ref, sink_ref, y_ref):
        n = pl.program_id(0)
        kcat = jnp.concatenate([kp[...], kc[...], kn[...]], axis=0)
        vcat = jnp.concatenate([vp[...], vc[...], vn[...]], axis=0)
        valid = _swa_valid(n, T)
        heads = range(C_HEADS)
        kvs = [slice(C_DH * (h // C_G), C_DH * (h // C_G + 1)) for h in heads]
        scores = [_dot(q_ref[:, C_DH * h:C_DH * (h + 1)], kcat[:, kvs[h]], _NT) for h in heads]
        probs = [_swa_softmax(scores[h], bias_ref[h], valid, sink_ref[h])[0].astype(BF16) for h in heads]
        for h in heads:
            y_ref[:, C_DH * h:C_DH * (h + 1)] = _dot(probs[h], vcat[:, kvs[h]], _NN).astype(BF16)

    return _pallas(
        body, name=name, grid=(nb,), in_specs=in_specs, out_specs=pl.BlockSpec((C_BLOCK, 512), lambda n: (n, 0)),
        out_shape=jax.ShapeDtypeStruct((T, 512), BF16), compiler_params=_params("parallel"))(zc, zc, zc, zc, zc, zc, zc, bias, sink)


def _swa_bwd(zc, bias, sink, dy, *, name):
    T = zc.shape[0]
    nb, in_specs = _swa_specs(T)
    scale = C_DH ** -0.5

    def body(q_ref, kp, kc, kn, vp, vc, vn, bias_ref, sink_ref, dy_ref, dq_ref, dkc_ref, dvc_ref, dbias_ref, dsink_ref):
        n = pl.program_id(0)

        @pl.when(n == 0)
        def _():
            dbias_ref[...] = jnp.zeros(dbias_ref.shape, F32)
            dsink_ref[...] = jnp.zeros(dsink_ref.shape, F32)

        kcat = jnp.concatenate([kp[...], kc[...], kn[...]], axis=0)
        vcat = jnp.concatenate([vp[...], vc[...], vn[...]], axis=0)
        valid = _swa_valid(n, T)
        heads = range(C_HEADS)
        kvs = [slice(C_DH * (h // C_G), C_DH * (h // C_G + 1)) for h in heads]
        qs = [q_ref[:, C_DH * h:C_DH * (h + 1)] for h in heads]
        dos = [dy_ref[:, C_DH * h:C_DH * (h + 1)].astype(BF16) for h in heads]
        scores = [_dot(qs[h], kcat[:, kvs[h]], _NT) for h in heads]
        dps = [_dot(dos[h], vcat[:, kvs[h]], _NT) for h in heads]
        pbs, dsbs = [], []
        for h in heads:
            p, p_sink = _swa_softmax(scores[h], bias_ref[h], valid, sink_ref[h])
            rowdot = jnp.sum(p * dps[h], axis=-1, keepdims=True)
            ds = p * (dps[h] - rowdot)
            dbias_ref[h] += ds
            tot = jnp.sum(jnp.sum(-p_sink * rowdot, axis=0, keepdims=True), axis=1, keepdims=True)
            dsink_ref[h:h + 1, :] += jnp.broadcast_to(tot, (1, LANES))
            pbs.append(p.astype(BF16))
            dsbs.append((ds * scale).astype(BF16))
        for h in heads:
            dq_ref[:, C_DH * h:C_DH * (h + 1)] = _dot(dsbs[h], kcat[:, kvs[h]], _NN).astype(BF16)
        dks = [_dot(dsbs[h], qs[h], _TN) for h in heads]
        dvs = [_dot(pbs[h], dos[h], _TN) for h in heads]
        for kv in range(C_KV_HEADS):
            group = range(kv * C_G, (kv + 1) * C_G)
            dkc_ref[0, :, C_DH * kv:C_DH * (kv + 1)] = sum(dks[h] for h in group)
            dvc_ref[0, :, C_DH * kv:C_DH * (kv + 1)] = sum(dvs[h] for h in group)

    part = pl.BlockSpec((1, C_SPAN, LANES), lambda n: (n, 0, 0))
    dq, dkc, dvc, dbias, dsink = _pallas(
        body, name=name, grid=(nb,), in_specs=in_specs + [pl.BlockSpec((C_BLOCK, 512), lambda n: (n, 0))],
        out_specs=[pl.BlockSpec((C_BLOCK, 512), lambda n: (n, 0)), part, part,
                   pl.BlockSpec((C_HEADS, C_BLOCK, C_SPAN), lambda n: (0, 0, 0)), pl.BlockSpec((C_HEADS, LANES), lambda n: (0, 0))],
        out_shape=[jax.ShapeDtypeStruct((T, 512), BF16), jax.ShapeDtypeStruct((nb, C_SPAN, LANES), F32),
                   jax.ShapeDtypeStruct((nb, C_SPAN, LANES), F32), jax.ShapeDtypeStruct((C_HEADS, C_BLOCK, C_SPAN), F32),
                   jax.ShapeDtypeStruct((C_HEADS, LANES), F32)],
        compiler_params=_params("arbitrary"))(zc, zc, zc, zc, zc, zc, zc, bias, sink, dy)

    def combine(dq_ref, kp, kc, kn, vp, vc, vn, o_ref):
        n = pl.program_id(0)
        lo = (n > 0).astype(F32)
        hi = (n < nb - 1).astype(F32)
        o_ref[:, 0:512] = dq_ref[...]
        o_ref[:, 512:640] = (kp[0] * lo + kc[0] + kn[0] * hi).astype(BF16)
        o_ref[:, 640:768] = (vp[0] * lo + vc[0] + vn[0] * hi).astype(BF16)

    prev = pl.BlockSpec((1, C_BLOCK, LANES), lambda n: (jnp.maximum(n - 1, 0), 2, 0))
    cur = pl.BlockSpec((1, C_BLOCK, LANES), lambda n: (n, 1, 0))
    nxt = pl.BlockSpec((1, C_BLOCK, LANES), lambda n: (jnp.minimum(n + 1, nb - 1), 0, 0))
    dzc = _pallas(
        combine, name=name + "_combine", grid=(nb,),
        in_specs=[pl.BlockSpec((C_BLOCK, 512), lambda n: (n, 0)), prev, cur, nxt, prev, cur, nxt],
        out_specs=pl.BlockSpec((C_BLOCK, 768), lambda n: (n, 0)),
        out_shape=jax.ShapeDtypeStruct((T, 768), BF16), compiler_params=_params("parallel"))(dq, dkc, dkc, dkc, dvc, dvc, dvc)
    return dzc, dbias, dsink


def _merge_tiles(T):
    return _tile(T, 512, 8), 512


def _merge_fwd(ya, yb, yc, wa, wb, wc, zg, *, name):
    T = ya.shape[0]
    tm, tn = _merge_tiles(T)
    nd = D_MODEL // tn

    def body(ya_ref, yb_ref, yc_ref, wa_ref, wb_ref, wc_ref, ga_ref, gb_ref, gc_ref, o_ref):
        acc = jax.nn.sigmoid(ga_ref[...].astype(F32)) * _dot(ya_ref[...].astype(BF16), wa_ref[...], _NN)
        acc += jax.nn.sigmoid(gb_ref[...].astype(F32)) * _dot(yb_ref[...].astype(BF16), wb_ref[...], _NN)
        acc += jax.nn.sigmoid(gc_ref[...].astype(F32)) * _dot(yc_ref[...].astype(BF16), wc_ref[...], _NN)
        o_ref[...] = acc.astype(BF16)

    y = pl.BlockSpec((tm, 512), lambda i, j: (i, 0))
    w = pl.BlockSpec((512, tn), lambda i, j: (0, j))
    gate = lambda b: pl.BlockSpec((tm, tn), lambda i, j: (i, b * nd + j))
    return _pallas(
        body, name=name, grid=(T // tm, nd), in_specs=[y, y, y, w, w, w, gate(0), gate(1), gate(2)],
        out_specs=pl.BlockSpec((tm, tn), lambda i, j: (i, j)),
        out_shape=jax.ShapeDtypeStruct((T, D_MODEL), BF16),
        compiler_params=_params("parallel", "parallel"))(ya, yb, yc, wa, wb, wc, zg, zg, zg)


def _merge_bwd(ya, yb, yc, wa, wb, wc, zg, dm, *, name):
    T = ya.shape[0]
    tm, tn = _merge_tiles(T)
    nd = D_MODEL // tn

    def body(ya_ref, yb_ref, yc_ref, wa_ref, wb_ref, wc_ref, ga_ref, gb_ref, gc_ref, dm_ref, *outs):
        dmv = dm_ref[...].astype(F32)
        for y_ref, w_ref, g_ref, du_ref, dg_ref in zip((ya_ref, yb_ref, yc_ref), (wa_ref, wb_ref, wc_ref),
                                                       (ga_ref, gb_ref, gc_ref), outs[:3], outs[3:]):
            u = _dot(y_ref[...].astype(BF16), w_ref[...], _NN)
            sg = jax.nn.sigmoid(g_ref[...].astype(F32))
            du_ref[...] = (dmv * sg).astype(BF16)
            dg_ref[...] = (dmv * u * sg * (1.0 - sg)).astype(BF16)

    y = pl.BlockSpec((tm, 512), lambda i, j: (i, 0))
    w = pl.BlockSpec((512, tn), lambda i, j: (0, j))
    gate = lambda b: pl.BlockSpec((tm, tn), lambda i, j: (i, b * nd + j))
    t = pl.BlockSpec((tm, tn), lambda i, j: (i, j))
    return _pallas(
        body, name=name, grid=(T // tm, nd), in_specs=[y, y, y, w, w, w, gate(0), gate(1), gate(2), t],
        out_specs=[t] * 6, out_shape=[jax.ShapeDtypeStruct((T, D_MODEL), BF16)] * 6,
        compiler_params=_params("parallel", "parallel"))(ya, yb, yc, wa, wb, wc, zg, zg, zg, dm)


def _cross_fwd(q, kvm, *, name):
    T = q.shape[0]
    M = kvm.shape[0]
    tm = _tile(T, 512, 8)
    scale = X_DH ** -0.5

    def body(q_ref, k_ref, v_ref, o_ref):
        for h in range(X_HEADS):
            cs = slice(X_DH * h, X_DH * (h + 1))
            s = _dot(q_ref[:, cs], k_ref[:, cs], _NT) * scale
            e = jnp.exp(s - jnp.max(s, axis=-1, keepdims=True))
            p = e / jnp.sum(e, axis=-1, keepdims=True)
            o_ref[:, cs] = _dot(p.astype(BF16), v_ref[:, cs], _NN).astype(BF16)

    row = pl.BlockSpec((tm, D_MODEL), lambda i: (i, 0))
    return _pallas(
        body, name=name, grid=(T // tm,),
        in_specs=[row, pl.BlockSpec((M, D_MODEL), lambda i: (0, 0)), pl.BlockSpec((M, D_MODEL), lambda i: (0, 1))],
        out_specs=row, out_shape=jax.ShapeDtypeStruct((T, D_MODEL), BF16), compiler_params=_params("parallel"))(q, kvm, kvm)


def _cross_bwd(q, kvm, do, *, name):
    T = q.shape[0]
    M = kvm.shape[0]
    tm = _tile(T, 512, 8)
    scale = X_DH ** -0.5

    def body(q_ref, k_ref, v_ref, do_ref, dq_ref, dkv_ref):
        @pl.when(pl.program_id(0) == 0)
        def _():
            dkv_ref[...] = jnp.zeros(dkv_ref.shape, F32)

        for h in range(X_HEADS):
            cs = slice(X_DH * h, X_DH * (h + 1))
            vs = slice(D_MODEL + X_DH * h, D_MODEL + X_DH * (h + 1))
            qh, kh, doh = q_ref[:, cs], k_ref[:, cs], do_ref[:, cs]
            s = _dot(qh, kh, _NT) * scale
            e = jnp.exp(s - jnp.max(s, axis=-1, keepdims=True))
            p = e / jnp.sum(e, axis=-1, keepdims=True)
            dp = _dot(doh, v_ref[:, cs], _NT)
            ds = (p * (dp - jnp.sum(p * dp, axis=-1, keepdims=True)) * scale).astype(BF16)
            dq_ref[:, cs] = _dot(ds, kh, _NN).astype(BF16)
            dkv_ref[:, cs] += _dot(ds, qh, _TN)
            dkv_ref[:, vs] += _dot(p.astype(BF16), doh, _TN)

    row = pl.BlockSpec((tm, D_MODEL), lambda i: (i, 0))
    return _pallas(
        body, name=name, grid=(T // tm,),
        in_specs=[row, pl.BlockSpec((M, D_MODEL), lambda i: (0, 0)), pl.BlockSpec((M, D_MODEL), lambda i: (0, 1)), row],
        out_specs=[row, pl.BlockSpec((M, 2 * D_MODEL), lambda i: (0, 0))],
        out_shape=[jax.ShapeDtypeStruct((T, D_MODEL), BF16), jax.ShapeDtypeStruct((M, 2 * D_MODEL), F32)],
        compiler_params=_params("arbitrary"))(q, kvm, kvm, do)


def _ffn_up(h, w1, w3, *, name):
    T = h.shape[0]
    tm = _tile(T, 512, 8)
    tn = D_FF // 2

    def body(h_ref, w1_ref, w3_ref, a_ref, b_ref, act_ref):
        hv = h_ref[...]
        a = _dot(hv, w1_ref[...], _NN)
        b = _dot(hv, w3_ref[...], _NN)
        a_ref[...] = a.astype(BF16)
        b_ref[...] = b.astype(BF16)
        act_ref[...] = (a * jax.nn.sigmoid(a) * b).astype(BF16)

    w = pl.BlockSpec((D_MODEL, tn), lambda i, j: (0, j))
    t = pl.BlockSpec((tm, tn), lambda i, j: (i, j))
    return _pallas(
        body, name=name, grid=(T // tm, D_FF // tn), in_specs=[pl.BlockSpec((tm, D_MODEL), lambda i, j: (i, 0)), w, w],
        out_specs=[t, t, t],
        out_shape=[jax.ShapeDtypeStruct((T, D_FF), BF16)] * 3,
        compiler_params=_params("parallel", "parallel"))(h, w1, w3)


def _ffn_dact(dx, w2, a, b, *, name):
    T = dx.shape[0]
    tm = _tile(T, 512, 8)
    tn = D_FF // 2

    def body(dx_ref, w2_ref, a_ref, b_ref, da_ref, db_ref):
        dact = _dot(dx_ref[...].astype(BF16), w2_ref[...], _NT)
        av = a_ref[...].astype(F32)
        sg = jax.nn.sigmoid(av)
        da_ref[...] = (dact * b_ref[...].astype(F32) * (sg * (1.0 + av * (1.0 - sg)))).astype(BF16)
        db_ref[...] = (dact * (av * sg)).astype(BF16)

    t = pl.BlockSpec((tm, tn), lambda i, j: (i, j))
    return _pallas(
        body, name=name, grid=(T // tm, D_FF // tn),
        in_specs=[pl.BlockSpec((tm, D_MODEL), lambda i, j: (i, 0)), pl.BlockSpec((tn, D_MODEL), lambda i, j: (j, 0)), t, t],
        out_specs=[t, t], out_shape=[jax.ShapeDtypeStruct((T, D_FF), BF16)] * 2,
        compiler_params=_params("parallel", "parallel"))(dx, w2, a, b)


def _loss_head(x, g, target, *, name):
    T, D = x.shape
    tm = _tile(T, 512, 8)

    def body(x_ref, g_ref, t_ref, loss_ref, dx_ref, dg_ref):
        xv = x_ref[...]
        r = lax.rsqrt(jnp.mean(xv * xv, axis=-1, keepdims=True) + EPS)
        xh = xv * r
        gv = g_ref[...]
        err = xh * gv - t_ref[...]
        dy = err * (1.0 / D)
        dxh = dy * gv
        dx_ref[...] = r * (dxh - xh * jnp.mean(dxh * xh, axis=-1, keepdims=True))
        lpart = 0.5 * jnp.sum(jnp.mean(err * err, axis=-1, keepdims=True), axis=0, keepdims=True)
        gpart = jnp.sum(dy * xh, axis=0, keepdims=True)

        @pl.when(pl.program_id(0) == 0)
        def _():
            loss_ref[...] = jnp.broadcast_to(lpart, (1, LANES))
            dg_ref[...] = gpart

        @pl.when(pl.program_id(0) > 0)
        def _():
            loss_ref[...] += jnp.broadcast_to(lpart, (1, LANES))
            dg_ref[...] += gpart

    row = pl.BlockSpec((tm, D), lambda i: (i, 0))
    vec = pl.BlockSpec((1, D), lambda i: (0, 0))
    return _pallas(
        body, name=name, grid=(T // tm,), in_specs=[row, vec, row],
        out_specs=[pl.BlockSpec((1, LANES), lambda i: (0, 0)), row, vec],
        out_shape=[jax.ShapeDtypeStruct((1, LANES), F32), jax.ShapeDtypeStruct((T, D), F32), jax.ShapeDtypeStruct((1, D), F32)],
        compiler_params=_params("arbitrary"))(x, g.reshape(1, D), target)


IN_CQ, IN_CKV, IN_KR, IN_B, IN_C, IN_G, IN_END = 0, 384, 640, 672, 4768, 5536, 8608
WEIGHT_NAMES = ("w_in", "g_mix", "a_gq", "a_gkv", "a_wuq", "a_wukv", "b_lb", "b_gout", "c_sink", "rel_bias",
                "w_br_a", "w_br_b", "w_br_c", "w_out", "g_x", "g_mem", "x_wq", "x_wkv", "x_wo", "g_ffn",
                "f_w1", "f_w3", "f_w2", "g_final")


def _lower_bounds(b_lb):
    sm = jax.nn.softmax(b_lb.astype(F32), axis=1)
    return jnp.cumsum(sm, axis=1) - sm[:, :1]


def _layer_weights(w, l):
    bf = lambda a: a.astype(BF16)
    w_in = bf(w["w_in"][l])
    kr = w_in[:, IN_KR:IN_B]
    wa = jnp.concatenate([w_in[:, IN_CQ:IN_CKV], kr, _rope_swap_cols(kr), jnp.zeros((D_MODEL, 64), BF16),
                          w_in[:, IN_CKV:IN_KR]], axis=1)
    wuq = bf(w["a_wuq"][l]).reshape(A_Q_RANK, A_HEADS, A_QK)
    zeros = lambda n: jnp.zeros((A_Q_RANK, A_HEADS, n), BF16)
    wuq_pad = jnp.concatenate([wuq, zeros(A_PAD - A_QK)], axis=-1)
    wuq_sw = jnp.concatenate([zeros(A_NOPE), _rope_swap_cols(wuq[..., A_NOPE:]), zeros(A_PAD - A_QK)], axis=-1)
    wq2 = jnp.concatenate([wuq_pad.reshape(A_Q_RANK, -1), wuq_sw.reshape(A_Q_RANK, -1)], axis=1)
    wukv = bf(w["a_wukv"][l]).reshape(A_KV_RANK, A_HEADS, A_NOPE + A_V)
    wkv = jnp.concatenate([wukv[..., :A_NOPE].reshape(A_KV_RANK, -1), wukv[..., A_NOPE:].reshape(A_KV_RANK, -1)], axis=1)
    return dict(wa=wa, wb=w_in[:, IN_B:IN_C], wc=w_in[:, IN_C:IN_G], wg=w_in[:, IN_G:IN_END], wq2=wq2, wkv=wkv,
                w_br_a=bf(w["w_br_a"][l]), w_br_b=bf(w["w_br_b"][l]), w_br_c=bf(w["w_br_c"][l]), w_out=bf(w["w_out"][l]),
                x_wq=bf(w["x_wq"][l]), x_wkv=bf(w["x_wkv"][l]), x_wo=bf(w["x_wo"][l]),
                f_w1=bf(w["f_w1"][l]), f_w3=bf(w["f_w3"][l]), f_w2=bf(w["f_w2"][l]))


def _layer_fwd(l, x, mem, w, lw, lower, bias, tabs):
    n = lambda s: f"l{l}_{s}"
    cq_t, sq_t, ck, _ = tabs
    s = dict(x=x)
    s["h0"] = h0 = _rms(x, w["g_mix"][l], name=n("rms_mix"))
    s["za"] = za = _mm(h0, lw["wa"], name=n("in_a"))
    s["zb"] = zb = _mm(h0, lw["wb"], name=n("in_b"))
    s["zc"] = zc = _mm(h0, lw["wc"], out_dtype=BF16, name=n("in_c"))
    s["zg"] = zg = _mm(h0, lw["wg"], out_dtype=BF16, name=n("in_g"))
    s["cqn"] = cqn = _rms(za, w["a_gq"][l], col=0, width=A_Q_RANK, name=n("rms_cq"))
    s["ckvn"] = ckvn = _rms(za, w["a_gkv"][l], col=512, width=A_KV_RANK, name=n("rms_ckv"))
    s["q"], s["qt"] = q, _ = _qrope(cqn, lw["wq2"], cq_t, sq_t, name=n("uq_rope"))
    s["kv"] = kv = _mm(ckvn, lw["wkv"], out_dtype=BF16, name=n("ukv"))
    s["k"], vxt = k, _ = _kprep(kv, za, ck, name=n("kprep"))
    s["ya"], s["lse"] = ya, _ = _flash_fwd(q, k, vxt, name=n("mla"))
    lb_f, lb_b = lower[0, l].reshape(1, -1), lower[1, l].reshape(1, -1)
    s["of"], s["stf"] = of, _ = _hgrn_fwd(zb, lb_f, reverse=False, name=n("hgrn_f"))
    s["ob"], s["stb"] = ob, _ = _hgrn_fwd(zb, lb_b, reverse=True, name=n("hgrn_b"))
    s["yb"] = yb = _hgrn_out(of, ob, zb, w["b_gout"][l], name=n("hgrn_out"))
    s["yc"] = yc = _swa_fwd(zc, bias, w["c_sink"][l], name=n("swa"))
    s["merged"] = merged = _merge_fwd(ya, yb, yc, lw["w_br_a"], lw["w_br_b"], lw["w_br_c"], zg, name=n("merge"))
    s["x1"] = x1 = _mm(merged, lw["w_out"], add=x, name=n("out"))
    s["h1"] = h1 = _rms(x1, w["g_x"][l], name=n("rms_x"))
    s["qx"] = qx = _mm(h1, lw["x_wq"], out_dtype=BF16, name=n("xq"))
    s["memn"] = memn = _rms(mem, w["g_mem"][l], name=n("rms_mem"))
    s["kvm"] = kvm = _mm(memn, lw["x_wkv"], out_dtype=BF16, name=n("xkv"))
    s["ox"] = ox = _cross_fwd(qx, kvm, name=n("cross"))
    s["x2"] = x2 = _mm(ox, lw["x_wo"], add=x1, name=n("xo"))
    s["h2"] = h2 = _rms(x2, w["g_ffn"][l], name=n("rms_ffn"))
    s["a"], s["b"], s["act"] = a, b, act = _ffn_up(h2, lw["f_w1"], lw["f_w3"], name=n("ffn_up"))
    x3 = _mm(act, lw["f_w2"], add=x2, name=n("ffn_down"))
    return x3, s


def _layer_bwd(l, dx3, mem, w, lw, lower, bias, tabs, s):
    n = lambda t: f"l{l}_b_{t}"
    cq_t, sq_t, _, ck_t = tabs
    g = {}
    da, db = _ffn_dact(dx3, lw["f_w2"], s["a"], s["b"], name=n("ffn_dact"))
    g["f_w2"] = _mm(s["act"], dx3, mode="tn", name=n("dw2"))
    dh2 = _mm(db, lw["f_w3"], mode="nt", add=_mm(da, lw["f_w1"], mode="nt", name=n("dh2a")), name=n("dh2b"))
    g["f_w1"] = _mm(s["h2"], da, mode="tn", name=n("dw1"))
    g["f_w3"] = _mm(s["h2"], db, mode="tn", name=n("dw3"))
    dx2, g["g_ffn"] = _rms_bwd(s["x2"], w["g_ffn"][l], dh2, res=dx3, name=n("rms_ffn"))
    dox = _mm(dx2, lw["x_wo"], mode="nt", out_dtype=BF16, name=n("dox"))
    g["x_wo"] = _mm(s["ox"], dx2, mode="tn", name=n("dwo"))
    dqx, dkvm = _cross_bwd(s["qx"], s["kvm"], dox, name=n("cross"))
    g["x_wq"] = _mm(s["h1"], dqx, mode="tn", name=n("dwq"))
    dh1 = _mm(dqx, lw["x_wq"], mode="nt", name=n("dh1"))
    g["x_wkv"] = _mm(s["memn"], dkvm, mode="tn", name=n("dwkv"))
    dmemn = _mm(dkvm, lw["x_wkv"], mode="nt", name=n("dmemn"))
    _, g["g_mem"] = _rms_bwd(mem, w["g_mem"][l], dmemn, name=n("rms_mem"))
    dx1, g["g_x"] = _rms_bwd(s["x1"], w["g_x"][l], dh1, res=dx2, name=n("rms_x"))
    dmerged = _mm(dx1, lw["w_out"], mode="nt", name=n("dmerged"))
    g["w_out"] = _mm(s["merged"], dx1, mode="tn", name=n("dwout"))
    dua, dub, duc, dga, dgb, dgc = _merge_bwd(s["ya"], s["yb"], s["yc"], lw["w_br_a"], lw["w_br_b"], lw["w_br_c"],
                                              s["zg"], dmerged, name=n("merge"))
    dya = _mm(dua, lw["w_br_a"], mode="nt", out_dtype=BF16, name=n("dya"))
    dyb = _mm(dub, lw["w_br_b"], mode="nt", name=n("dyb"))
    dyc = _mm(duc, lw["w_br_c"], mode="nt", out_dtype=BF16, name=n("dyc"))
    g["w_br_a"] = _mm(s["ya"], dua, mode="tn", name=n("dwbra"))
    g["w_br_b"] = _mm(s["yb"], dub, mode="tn", name=n("dwbrb"))
    g["w_br_c"] = _mm(s["yc"], duc, mode="tn", name=n("dwbrc"))
    dzc, dbias, dsink = _swa_bwd(s["zc"], bias, w["c_sink"][l], dyc, name=n("swa"))
    g["c_sink"] = dsink[:, 0]
    g["bias"] = dbias
    lb_f, lb_b = lower[0, l].reshape(1, -1), lower[1, l].reshape(1, -1)
    do_, dgate, dgout = _hgrn_out_bwd(s["of"], s["ob"], s["zb"], w["b_gout"][l], dyb, name=n("hgrn_out"))
    g["b_gout"] = dgout[0]
    dq_f, dzf, dv_f, dlb_f = _hgrn_bwd(s["zb"], lb_f, do_, s["stf"], reverse=False, name=n("hgrn_f"))
    dq_b, dzr, dv_b, dlb_b = _hgrn_bwd(s["zb"], lb_b, do_, s["stb"], reverse=True, name=n("hgrn_b"))
    g["lower"] = jnp.concatenate([dlb_f, dlb_b], axis=0)
    dzb = _dzb_assemble(dq_f, dq_b, dzf, dzr, dv_f, dv_b, dgate, name=n("dzb"))
    delta = _attn_delta(s["ya"], dya, name=n("mla_delta"))
    dq_part, dkt, dvt = _flash_bwd(s["q"], s["qt"], s["k"], s["kv"], dya, dya.T, s["lse"].reshape(A_HEADS, -1, 1), delta, name=n("mla"))
    dq2 = _dq_sum(dq_part, cq_t, sq_t, name=n("mla_dq"))
    dcqn = _mm(dq2, lw["wq2"], mode="nt", name=n("dcqn"))
    dwq2 = _mm(s["cqn"], dq2, mode="tn", name=n("dwq2")).reshape(A_Q_RANK, 2, A_HEADS, A_PAD)
    dknt, dzkrt = _kprep_bwd(dkt, ck_t, name=n("kprep"))
    wkv = lw["wkv"]
    dckvn = _mm(dvt, wkv[:, 512:].T, mode="tn", add=_mm(dknt, wkv[:, :512].T, mode="tn", name=n("dckvn_k")), name=n("dckvn_v"))
    dwkn = _mm(dknt, s["ckvn"], name=n("dwkn")).T
    dwv = _mm(dvt, s["ckvn"], name=n("dwv")).T
    dzcq, dgq = _rms_bwd(s["za"], w["a_gq"][l], dcqn, col=0, width=A_Q_RANK, out_dtype=BF16, name=n("rms_cq"))
    dzckv, dgkv = _rms_bwd(s["za"], w["a_gkv"][l], dckvn, col=512, width=A_KV_RANK, out_dtype=BF16, name=n("rms_ckv"))
    g["a_gq"], g["a_gkv"] = dgq[0], dgkv[0]
    sw = jnp.concatenate([jnp.zeros((A_Q_RANK, A_HEADS, A_NOPE), F32), _rope_unswap_cols(dwq2[:, 1, :, A_NOPE:A_QK])], axis=-1)
    g["a_wuq"] = (dwq2[:, 0, :, :A_QK] + sw).reshape(A_Q_RANK, -1)
    g["a_wukv"] = jnp.concatenate([dwkn.reshape(A_KV_RANK, A_HEADS, A_NOPE), dwv.reshape(A_KV_RANK, A_HEADS, A_V)], axis=-1).reshape(A_KV_RANK, -1)
    wa = lw["wa"]
    pieces = [(dzcq, wa[:, 0:384]), (dzckv, wa[:, 512:768]), (dzb, lw["wb"]), (dzc, lw["wc"]),
              (dga, lw["wg"][:, 0:1024]), (dgb, lw["wg"][:, 1024:2048]), (dgc, lw["wg"][:, 2048:3072])]
    dh0 = _mm(dzkrt, wa[:, 384:512].T, mode="tn", name=n("dh0_kr"))
    dwkr = _mm(dzkrt, s["h0"], name=n("dwin_kr")).T
    dwkr = dwkr[:, 0:A_ROPE] + _rope_unswap_cols(dwkr[:, A_ROPE:2 * A_ROPE])
    dws = []
    for i, (dz, wp) in enumerate(pieces):
        dh0 = _mm(dz, wp, mode="nt", add=dh0, name=n(f"dh0_{i}"))
        dws.append(_mm(s["h0"], dz, mode="tn", name=n(f"dwin_{i}")))
    g["w_in"] = jnp.concatenate([dws[0], dws[1], dwkr] + dws[2:], axis=1)
    dx, g["g_mix"] = _rms_bwd(s["x"], w["g_mix"][l], dh0, res=dx1, name=n("rms_mix"))
    return dx, g


def _local_step(x, mem, target, w):
    T = x.shape[0]
    tabs = _rope_tables(T)
    lower, lower_vjp = jax.vjp(_lower_bounds, w["b_lb"])
    buckets = _swa_buckets()
    onehot = (buckets.reshape(-1)[:, None] == jnp.arange(REL_BUCKETS)[None, :]).astype(F32)
    bias = jnp.dot(w["rel_bias"].astype(F32).T, onehot.T, precision=lax.Precision.HIGHEST).reshape(C_HEADS, C_BLOCK, C_SPAN)
    lws, saved = [], []
    for l in range(DEPTH):
        lws.append(_layer_weights(w, l))
        x, s = _layer_fwd(l, x, mem, w, lws[l], lower, bias, tabs)
        saved.append(s)
    loss, dx, dg_final = _loss_head(x, w["g_final"], target, name="loss_head")
    layer_grads = [None] * DEPTH
    for l in reversed(range(DEPTH)):
        dx, layer_grads[l] = _layer_bwd(l, dx, mem, w, lws[l], lower, bias, tabs, saved[l])
        saved[l] = None
    grads = {}
    for name in WEIGHT_NAMES:
        if name in layer_grads[0]:
            grads[name] = jnp.stack([layer_grads[l][name].reshape(w[name].shape[1:]) for l in range(DEPTH)])
    grads["g_final"] = dg_final[0]
    dlower = jnp.stack([layer_grads[l]["lower"] for l in range(DEPTH)], axis=1)
    grads["b_lb"] = lower_vjp(dlower)[0]
    dbias = layer_grads[0]["bias"] + layer_grads[1]["bias"]
    grads["rel_bias"] = jnp.dot(onehot.T, dbias.reshape(C_HEADS, -1).T, precision=lax.Precision.HIGHEST)
    return loss, dx, grads


N_CHIPS = 4
PACK_COLS = 1024
PACK_ALIGN = 32 * PACK_COLS
SHARDED = (("w_in", 2), ("a_wuq", 2), ("a_wukv", 2), ("b_lb", 2), ("w_br_a", 2), ("w_br_b", 2), ("w_br_c", 2), ("w_out", 1),
           ("x_wq", 1), ("x_wkv", 2), ("x_wo", 1), ("f_w1", 2), ("f_w3", 2), ("f_w2", 1))
REPLICATED = ("g_mix", "a_gq", "a_gkv", "b_gout", "c_sink", "rel_bias", "g_x", "g_mem", "g_ffn", "g_final")
MESH_IDS = pl.DeviceIdType.MESH
ANY_SPEC = pl.BlockSpec(memory_space=pl.ANY)


def _pack_pieces(arrs, cols, align):
    pieces = [a.reshape(-1, cols) for a in arrs]
    pad = (-sum(p.size for p in pieces)) % align
    return pieces + ([jnp.zeros((pad // cols, cols), pieces[0].dtype)] if pad else [])


def _pack(arrs, cols, align):
    return jnp.concatenate(_pack_pieces(arrs, cols, align), axis=0)


def _pack_small(arrs):
    flat = jnp.concatenate([a.reshape(-1) for a in arrs])
    return jnp.pad(flat, (0, (-flat.shape[0]) % (8 * LANES))).reshape(-1, LANES)


def _unpack(buf, shapes):
    cols = buf.shape[-1]
    buf = buf.reshape(-1, cols)
    by_rows = all(math.prod(shp) % cols == 0 for shp in shapes)
    flat = None if by_rows else buf.reshape(-1)
    out, start = [], 0
    for shp in shapes:
        size = math.prod(shp)
        piece = buf[start // cols:(start + size) // cols] if by_rows else flat[start:start + size]
        out.append(piece.reshape(shp))
        start += size
    return out


def _chip_peers():
    x, y, c = lax.axis_index("x"), lax.axis_index("y"), lax.axis_index("c")
    return x, y, c, [(1 - x, y), (x, 1 - y), (1 - x, 1 - y)]


def _chip_gather(src, chip, *, name):
    _, R, C = src.shape

    def body(src_ref, out_ref, send_sems, recv_sems, pass_send_sems, pass_recv_sems):
        x, y, c, chips = _chip_peers()
        me = 2 * x + y
        sibling = (x, y, 1 - c)

        def over_ici(j, slot):
            px, py = chips[j]
            return pltpu.make_async_remote_copy(src_ref=src_ref.at[c], dst_ref=out_ref.at[slot, c], send_sem=send_sems.at[j],
                                                recv_sem=recv_sems.at[j], device_id=(px, py, c), device_id_type=MESH_IDS)

        def pass_on(j, half):
            px, py = chips[j]
            piece = out_ref.at[2 * px + py, half]
            return pltpu.make_async_remote_copy(src_ref=piece, dst_ref=piece, send_sem=pass_send_sems.at[j],
                                                recv_sem=pass_recv_sems.at[j], device_id=sibling, device_id_type=MESH_IDS)

        sends = [over_ici(j, me) for j in range(3)]
        for cp in sends:
            cp.start()
        passed = []
        for j, (px, py) in enumerate(chips):
            over_ici(j, 2 * px + py).wait_recv()
            passed.append(pass_on(j, c))
            passed[j].start()
        for j in range(3):
            pass_on(j, 1 - c).wait_recv()
        for cp in sends + passed:
            cp.wait_send()

    gathered = _pallas(
        body, name=name, in_specs=[ANY_SPEC], out_specs=ANY_SPEC, out_shape=jax.ShapeDtypeStruct((N_CHIPS, 2, R, C), src.dtype),
        scratch_shapes=[pltpu.SemaphoreType.DMA((3,))] * 4,
        compiler_params=pltpu.CompilerParams(has_side_effects=True))(src)

    tr = _tile(R, 512, 16)

    def place(chip_ref, gathered_ref, own_ref, out_ref):
        out_ref[0, 0] = own_ref[0]

    grid_spec = pltpu.PrefetchScalarGridSpec(
        num_scalar_prefetch=1, grid=(2, R // tr),
        in_specs=[ANY_SPEC, pl.BlockSpec((1, tr, C), lambda h, i, chip_ref: (h, i, 0))],
        out_specs=pl.BlockSpec((1, 1, tr, C), lambda h, i, chip_ref: (chip_ref[0], h, i, 0)))
    return _pallas(place, name=name + "_own", grid_spec=grid_spec, out_shape=jax.ShapeDtypeStruct(gathered.shape, gathered.dtype),
                   input_output_aliases={1: 0}, compiler_params=_params("arbitrary", "arbitrary"))(chip, gathered, src)


def _chip_scatter(src, *, name):
    _, R, C = src.shape

    def body(src_ref, out_ref, send_sems, recv_sems):
        x, y, c, chips = _chip_peers()
        me = 2 * x + y

        def copy(j, seg):
            px, py = chips[j]
            return pltpu.make_async_remote_copy(src_ref=src_ref.at[seg], dst_ref=out_ref.at[j], send_sem=send_sems.at[j],
                                                recv_sem=recv_sems.at[j], device_id=(px, py, c), device_id_type=MESH_IDS)

        sends = [copy(j, 2 * px + py) for j, (px, py) in enumerate(chips)]
        for cp in sends:
            cp.start()
        for j in range(3):
            copy(j, me).wait_recv()
        for cp in sends:
            cp.wait_send()

    return _pallas(
        body, name=name, in_specs=[ANY_SPEC], out_specs=ANY_SPEC, out_shape=jax.ShapeDtypeStruct((3, R, C), src.dtype),
        scratch_shapes=[pltpu.SemaphoreType.DMA((3,)), pltpu.SemaphoreType.DMA((3,))],
        compiler_params=pltpu.CompilerParams(has_side_effects=True))(src)


PAIR_CHUNKS = 4


def _pair_swap(src, *, halves, name):
    R, C = src.shape[-2:]
    n = N_CHIPS if halves else 1
    rc = R // PAIR_CHUNKS
    assert rc * PAIR_CHUNKS == R and rc % 16 == 0, R

    def body(src_ref, out_ref, send_sems, recv_sems):
        x, y, c = lax.axis_index("x"), lax.axis_index("y"), lax.axis_index("c")
        copies = []
        for k in range(n):
            for r in range(PAIR_CHUNKS):
                rows = pl.ds(r * rc, rc)
                s = src_ref.at[k, 1 - c, rows] if halves else src_ref.at[rows]
                d = out_ref.at[k, rows] if halves else out_ref.at[rows]
                i = k * PAIR_CHUNKS + r
                copies.append(pltpu.make_async_remote_copy(src_ref=s, dst_ref=d, send_sem=send_sems.at[i], recv_sem=recv_sems.at[i],
                                                           device_id=(x, y, 1 - c), device_id_type=MESH_IDS))
        for cp in copies:
            cp.start()
        for cp in copies:
            cp.wait_recv()
        for cp in copies:
            cp.wait_send()

    shape = (N_CHIPS, R, C) if halves else (R, C)
    return _pallas(
        body, name=name, in_specs=[ANY_SPEC], out_specs=ANY_SPEC, out_shape=jax.ShapeDtypeStruct(shape, src.dtype),
        scratch_shapes=[pltpu.SemaphoreType.DMA((n * PAIR_CHUNKS,)), pltpu.SemaphoreType.DMA((n * PAIR_CHUNKS,))],
        compiler_params=pltpu.CompilerParams(has_side_effects=True))(src)


def _pair_add(g4, got, c, *, name):
    _, _, R, C = g4.shape
    tr = _tile(R, 512, 16)

    def body(c_ref, mine_ref, got_ref, o_ref, ob_ref):
        s = mine_ref[0, 0] + got_ref[0].astype(F32)
        o_ref[0] = s
        ob_ref[0] = s.astype(BF16)

    blk = pl.BlockSpec((1, tr, C), lambda k, i, c_ref: (k, i, 0))
    grid_spec = pltpu.PrefetchScalarGridSpec(
        num_scalar_prefetch=1, grid=(N_CHIPS, R // tr),
        in_specs=[pl.BlockSpec((1, 1, tr, C), lambda k, i, c_ref: (k, c_ref[0], i, 0)), blk], out_specs=[blk, blk])
    return _pallas(body, name=name, grid_spec=grid_spec,
                   out_shape=[jax.ShapeDtypeStruct((N_CHIPS, R, C), F32), jax.ShapeDtypeStruct((N_CHIPS, R, C), BF16)],
                   compiler_params=_params("parallel", "parallel"))(c, g4, got)


def _chip_sum(pair_sum, landed, me, *, name):
    _, R, C = pair_sum.shape
    tr = _tile(R, 512, 16)

    def body(me_ref, own_ref, landed_ref, o_ref):
        acc = own_ref[0]
        for j in range(3):
            acc = acc + landed_ref[j].astype(F32)
        o_ref[...] = acc

    grid_spec = pltpu.PrefetchScalarGridSpec(
        num_scalar_prefetch=1, grid=(R // tr,),
        in_specs=[pl.BlockSpec((1, tr, C), lambda i, me_ref: (me_ref[0], i, 0)), pl.BlockSpec((3, tr, C), lambda i, me_ref: (0, i, 0))],
        out_specs=pl.BlockSpec((tr, C), lambda i, me_ref: (i, 0)))
    return _pallas(body, name=name, grid_spec=grid_spec, out_shape=jax.ShapeDtypeStruct((R, C), F32),
                   compiler_params=_params("parallel"))(me, pair_sum, landed)


def _join_halves(mine, got, c, *, name):
    R, C = mine.shape
    tr = _tile(R, 512, 16)

    def body(c_ref, mine_ref, got_ref, o_ref):
        use_mine = pl.program_id(0) == c_ref[0]
        o_ref[0] = jnp.where(use_mine, mine_ref[...], got_ref[...])

    blk = pl.BlockSpec((tr, C), lambda h, i, c_ref: (i, 0))
    grid_spec = pltpu.PrefetchScalarGridSpec(num_scalar_prefetch=1, grid=(2, R // tr), in_specs=[blk, blk],
                                             out_specs=pl.BlockSpec((1, tr, C), lambda h, i, c_ref: (h, i, 0)))
    return _pallas(body, name=name, grid_spec=grid_spec, out_shape=jax.ShapeDtypeStruct((2, R, C), mine.dtype),
                   compiler_params=_params("parallel", "parallel"))(c, mine, got).reshape(2 * R, C)


def _gather8(s, *, name):
    R, C = s.shape

    def body(s_ref, out_ref, send_sems, recv_sems):
        x, y, c = lax.axis_index("x"), lax.axis_index("y"), lax.axis_index("c")
        me = 4 * x + 2 * y + c
        flips = [(dx, dy, dc) for dx in (0, 1) for dy in (0, 1) for dc in (0, 1)][1:]
        out_ref[me] = s_ref[...]

        def copy(j, slot):
            dx, dy, dc = flips[j]
            return pltpu.make_async_remote_copy(src_ref=s_ref, dst_ref=out_ref.at[slot], send_sem=send_sems.at[j],
                                                recv_sem=recv_sems.at[j], device_id=(x ^ dx, y ^ dy, c ^ dc), device_id_type=MESH_IDS)

        sends = [copy(j, me) for j in range(7)]
        for cp in sends:
            cp.start()
        for j, (dx, dy, dc) in enumerate(flips):
            copy(j, 4 * (x ^ dx) + 2 * (y ^ dy) + (c ^ dc)).wait_recv()
        for cp in sends:
            cp.wait_send()

    vmem = pl.BlockSpec(memory_space=pltpu.VMEM)
    return _pallas(
        body, name=name, in_specs=[vmem], out_specs=vmem, out_shape=jax.ShapeDtypeStruct((8, R, C), s.dtype),
        scratch_shapes=[pltpu.SemaphoreType.DMA((7,)), pltpu.SemaphoreType.DMA((7,))],
        compiler_params=pltpu.CompilerParams(has_side_effects=True))(s)


def _sum_slots(a, *, name):
    n, R, C = a.shape
    tr = _tile(R, 512, 8)

    def body(a_ref, o_ref):
        acc = a_ref[0]
        for k in range(1, n):
            acc = acc + a_ref[k]
        o_ref[...] = acc

    return _pallas(body, name=name, grid=(R // tr,), in_specs=[pl.BlockSpec((n, tr, C), lambda i: (0, i, 0))],
                   out_specs=pl.BlockSpec((tr, C), lambda i: (i, 0)), out_shape=jax.ShapeDtypeStruct((R, C), a.dtype),
                   compiler_params=_params("parallel"))(a)


def _adamw(w, g, m, v, *, name):
    R, C = w.shape
    tr = _tile(R, max(8, (1 << 18) // C // 8 * 8), 8)
    c1 = 1.0 / (1.0 - ADAM_B1 ** ADAM_STEP)
    c2 = 1.0 / (1.0 - ADAM_B2 ** ADAM_STEP)

    def body(w_ref, g_ref, m_ref, v_ref, d_ref, nm_ref, nv_ref):
        gv = g_ref[...]
        nm = ADAM_B1 * m_ref[...] + (1.0 - ADAM_B1) * gv
        nv = ADAM_B2 * v_ref[...] + (1.0 - ADAM_B2) * (gv * gv)
        d_ref[...] = -ADAM_LR * ((nm * c1) / (jnp.sqrt(nv * c2) + ADAM_EPS) + ADAM_WD * w_ref[...])
        nm_ref[...] = nm
        nv_ref[...] = nv

    blk = pl.BlockSpec((tr, C), lambda i: (i, 0))
    shape = jax.ShapeDtypeStruct((R, C), F32)
    return _pallas(body, name=name, grid=(R // tr,), in_specs=[blk] * 4, out_specs=[blk] * 3, out_shape=[shape] * 3,
                   compiler_params=_params("parallel"))(w, g, m, v)


def kernel(x, mem, w_in, g_mix, a_gq, a_gkv, a_wuq, a_wukv, b_lb, b_gout, c_sink, rel_bias, w_br_a, w_br_b, w_br_c, w_out, g_x, g_mem, x_wq, x_wkv, x_wo, g_ffn, f_w1, f_w3, f_w2, g_final, loss_target, m_w_in, m_g_mix, m_a_gq, m_a_gkv, m_a_wuq, m_a_wukv, m_b_lb, m_b_gout, m_c_sink, m_rel_bias, m_w_br_a, m_w_br_b, m_w_br_c, m_w_out, m_g_x, m_g_mem, m_x_wq, m_x_wkv, m_x_wo, m_g_ffn, m_f_w1, m_f_w3, m_f_w2, m_g_final, v_w_in, v_g_mix, v_a_gq, v_a_gkv, v_a_wuq, v_a_wukv, v_b_lb, v_b_gout, v_c_sink, v_rel_bias, v_w_br_a, v_w_br_b, v_w_br_c, v_w_out, v_g_x, v_g_mem, v_x_wq, v_x_wkv, v_x_wo, v_g_ffn, v_f_w1, v_f_w3, v_f_w2, v_g_final):
    ws = dict(zip(WEIGHT_NAMES, (w_in, g_mix, a_gq, a_gkv, a_wuq, a_wukv, b_lb, b_gout, c_sink, rel_bias, w_br_a, w_br_b, w_br_c,
                                 w_out, g_x, g_mem, x_wq, x_wkv, x_wo, g_ffn, f_w1, f_w3, f_w2, g_final)))
    ms = dict(zip(WEIGHT_NAMES, (m_w_in, m_g_mix, m_a_gq, m_a_gkv, m_a_wuq, m_a_wukv, m_b_lb, m_b_gout, m_c_sink, m_rel_bias,
                                 m_w_br_a, m_w_br_b, m_w_br_c, m_w_out, m_g_x, m_g_mem, m_x_wq, m_x_wkv, m_x_wo, m_g_ffn,
                                 m_f_w1, m_f_w3, m_f_w2, m_g_final)))
    vs = dict(zip(WEIGHT_NAMES, (v_w_in, v_g_mix, v_a_gq, v_a_gkv, v_a_wuq, v_a_wukv, v_b_lb, v_b_gout, v_c_sink, v_rel_bias,
                                 v_w_br_a, v_w_br_b, v_w_br_c, v_w_out, v_g_x, v_g_mem, v_x_wq, v_x_wkv, v_x_wo, v_g_ffn,
                                 v_f_w1, v_f_w3, v_f_w2, v_g_final)))
    sharded = [n for n, _ in SHARDED]
    axis_of = dict(SHARDED)

    def wire(n):
        return lax.bitcast_convert_type(ws[n], BF16) if n == "b_lb" else ws[n].astype(BF16)

    core = lax.axis_index("c").astype(jnp.int32).reshape(1)
    chip = (2 * lax.axis_index("x") + lax.axis_index("y")).astype(jnp.int32).reshape(1)
    wire_shapes = [wire(n).shape for n in sharded]
    packed = _pack([wire(n) for n in sharded], PACK_COLS, PACK_ALIGN)
    gathered = _chip_gather(packed.reshape(2, packed.shape[0] // 2, PACK_COLS), chip, name="gather_weights")
    per_chip = [_unpack(gathered[k], wire_shapes) for k in range(N_CHIPS)]
    full = dict(ws)
    for i, n in enumerate(sharded):
        parts = [per_chip[k][i] for k in range(N_CHIPS)]
        if n == "b_lb":
            parts = [lax.bitcast_convert_type(p, F32) for p in parts]
        full[n] = jnp.concatenate(parts, axis=axis_of[n])

    loss, grad_x, grads = _local_step(x[0], mem[0], loss_target[0], full)

    pieces = []
    for k in range(N_CHIPS):
        pieces += _pack_pieces([jnp.split(grads[n], N_CHIPS, axis=axis_of[n])[k] for n in sharded], PACK_COLS, PACK_ALIGN)
    g4 = jnp.concatenate(pieces, axis=0).reshape(N_CHIPS, 2, -1, PACK_COLS)
    got = _pair_swap(g4, halves=True, name="reduce_pair_swap")
    pair_sum, pair_sum_wire = _pair_add(g4, got, core, name="reduce_pair_add")
    landed = _chip_scatter(pair_sum_wire, name="reduce_chip_scatter")
    mine = _chip_sum(pair_sum, landed, chip, name="reduce_chip_sum")
    g_shard = _join_halves(mine, _pair_swap(mine, halves=False, name="reduce_pair_join"), core, name="reduce_join_halves")

    small = _pack_small([grads[n] for n in REPLICATED] + [loss[0, 0:1]])
    small_sum = _sum_slots(_gather8(small, name="gather_small"), name="sum_small")
    small_grads = _unpack(small_sum, [ws[n].shape for n in REPLICATED] + [(1,)])
    loss_total = small_grads.pop()[0]

    shard_shapes = [ws[n].shape for n in sharded]
    out = {}
    for n, gr in zip(sharded, _unpack(g_shard, shard_shapes)):
        flat2 = lambda a: a.reshape(-1, a.shape[-1])
        d, nm, nv = _adamw(flat2(ws[n]), flat2(gr), flat2(ms[n]), flat2(vs[n]), name="adamw_" + n)
        out[n] = (gr, d.reshape(gr.shape), nm.reshape(gr.shape), nv.reshape(gr.shape))
    pk_s = lambda d: _pack_small([d[n] for n in REPLICATED])
    rep_shapes = [ws[n].shape for n in REPLICATED]
    gs_flat = _pack_small(small_grads)
    ds_flat, ms_flat, vs_flat = _adamw(pk_s(ws), gs_flat, pk_s(ms), pk_s(vs), name="adamw_replicated")
    for n, gr, d, nm, nv in zip(REPLICATED, small_grads, _unpack(ds_flat, rep_shapes), _unpack(ms_flat, rep_shapes),
                                _unpack(vs_flat, rep_shapes)):
        out[n] = (gr, d, nm, nv)
    return (loss_total, grad_x[None], *[out[n][0] for n in WEIGHT_NAMES], *[out[n][1] for n in WEIGHT_NAMES],
            *[out[n][2] for n in WEIGHT_NAMES], *[out[n][3] for n in WEIGHT_NAMES])
```

```python
import math

import jax
import jax.numpy as jnp
from jax import lax
from jax.experimental import pallas as pl
from jax.experimental.pallas import tpu as pltpu

F32 = jnp.float32
BF16 = jnp.bfloat16

D_MODEL = 1024
DEPTH = 2
EPS = 1e-6
MASK_VALUE = -1e30
TINY = 1e-30
A_HEADS, A_NOPE, A_ROPE, A_V = 8, 64, 32, 64
A_QK = A_NOPE + A_ROPE
A_Q_RANK, A_KV_RANK = 384, 256
ROPE_THETA = 10000.0
B_HEADS, B_DK, B_DV, B_CHUNK = 8, 128, 64, 16
C_HEADS, C_KV_HEADS, C_DH, C_WINDOW, C_BLOCK = 8, 2, 64, 128, 128
REL_BUCKETS, REL_MAX_DIST = 32, 128
X_HEADS, X_DH = 4, 256
D_FF = 2816
ADAM_LR, ADAM_B1, ADAM_B2, ADAM_EPS, ADAM_WD, ADAM_STEP = 0.001, 0.9, 0.999, 1e-08, 0.01, 10

LANES = 128
VMEM_LIMIT = 56 * 1024 * 1024
VMEM_LIMIT_WIDE = 62 * 1024 * 1024


def _pallas(body, **kw):
    return pl.pallas_call(body, **kw)


def _params(*sem):
    return pltpu.CompilerParams(dimension_semantics=sem, vmem_limit_bytes=VMEM_LIMIT)


def _tile(n, pref, unit=LANES):
    if n <= pref:
        return n
    t = (pref // unit) * unit
    while t > unit and n % t:
        t -= unit
    assert n % t == 0, (n, pref, unit)
    return t


def _dot(a, b, dims):
    return lax.dot_general(a, b, (dims, ((), ())), preferred_element_type=F32)


_NN = ((1,), (0,))
_NT = ((1,), (1,))
_TN = ((0,), (0,))


def _mm_tiles(M, N, K, mode):
    half_ff = D_FF // 2
    tm = _tile(M, half_ff if M % half_ff == 0 else (1024 if mode != "tn" and M >= 2048 else 512), LANES if mode == "tn" else 16)
    tn = _tile(N, half_ff if N % half_ff == 0 else 1024, 256 if N % 256 == 0 and N % half_ff else LANES)
    tk = K if K <= D_FF else _tile(K, 1024)
    return tm, tn, tk


def _mm(a, b, *, mode="nn", add=None, out_dtype=F32, tiles=None, name):
    if mode == "nn":
        (M, K), (K2, N) = a.shape, b.shape
    elif mode == "nt":
        (M, K), (N, K2) = a.shape, b.shape
    else:
        (K, M), (K2, N) = a.shape, b.shape
    assert K == K2, (a.shape, b.shape, mode)
    tm, tn, tk = tiles or _mm_tiles(M, N, K, mode)
    nk = K // tk
    dims = {"nn": _NN, "nt": _NT, "tn": _TN}[mode]
    a_spec = pl.BlockSpec((tk, tm), lambda i, j, k: (k, i)) if mode == "tn" else pl.BlockSpec((tm, tk), lambda i, j, k: (i, k))
    b_spec = pl.BlockSpec((tn, tk), lambda i, j, k: (j, k)) if mode == "nt" else pl.BlockSpec((tk, tn), lambda i, j, k: (k, j))
    o_spec = pl.BlockSpec((tm, tn), lambda i, j, k: (i, j))
    has_add = add is not None

    def body(*refs):
        if has_add:
            a_ref, b_ref, add_ref, o_ref, acc_ref = refs
        else:
            a_ref, b_ref, o_ref, acc_ref = refs
        k = pl.program_id(2)
        part = _dot(a_ref[...].astype(BF16), b_ref[...].astype(BF16), dims)

        @pl.when(k == 0)
        def _():
            acc_ref[...] = part

        @pl.when(k > 0)
        def _():
            acc_ref[...] += part

        @pl.when(k == nk - 1)
        def _():
            r = acc_ref[...]
            if has_add:
                r = r + add_ref[...].astype(F32)
            o_ref[...] = r.astype(out_dtype)

    ins = [a, b] + ([add] if has_add else [])
    in_specs = [a_spec, b_spec] + ([o_spec] if has_add else [])
    return _pallas(
        body, name=name, grid=(M // tm, N // tn, nk), in_specs=in_specs, out_specs=o_spec,
        out_shape=jax.ShapeDtypeStruct((M, N), out_dtype), scratch_shapes=[pltpu.VMEM((tm, tn), F32)],
        compiler_params=_params("parallel", "parallel", "arbitrary"),
    )(*ins)


def _rms(x, g, *, col=0, width=None, out_dtype=BF16, name):
    T = x.shape[0]
    width = x.shape[1] if width is None else width
    assert col % width == 0
    tm = _tile(T, 512, 8)
    cb = col // width

    def body(x_ref, g_ref, o_ref):
        xv = x_ref[...].astype(F32)
        r = lax.rsqrt(jnp.mean(xv * xv, axis=-1, keepdims=True) + EPS)
        o_ref[...] = (xv * r * g_ref[...]).astype(out_dtype)

    return _pallas(
        body, name=name, grid=(T // tm,),
        in_specs=[pl.BlockSpec((tm, width), lambda i: (i, cb)), pl.BlockSpec((1, width), lambda i: (0, 0))],
        out_specs=pl.BlockSpec((tm, width), lambda i: (i, 0)),
        out_shape=jax.ShapeDtypeStruct((T, width), out_dtype), compiler_params=_params("parallel"),
    )(x, g.reshape(1, width))


def _rms_bwd(x, g, dy, *, res=None, col=0, width=None, out_dtype=F32, name):
    T = x.shape[0]
    width = x.shape[1] if width is None else width
    assert col % width == 0
    tm = _tile(T, 512, 8)
    cb = col // width
    has_res = res is not None

    def body(*refs):
        if has_res:
            x_ref, g_ref, dy_ref, res_ref, dx_ref, dg_ref = refs
        else:
            x_ref, g_ref, dy_ref, dx_ref, dg_ref = refs
        xv = x_ref[...].astype(F32)
        r = lax.rsqrt(jnp.mean(xv * xv, axis=-1, keepdims=True) + EPS)
        xh = xv * r
        dyv = dy_ref[...].astype(F32)
        dxh = dyv * g_ref[...]
        dx = r * (dxh - xh * jnp.mean(dxh * xh, axis=-1, keepdims=True))
        if has_res:
            dx = dx + res_ref[...].astype(F32)
        dx_ref[...] = dx.astype(out_dtype)
        part = jnp.sum(dyv * xh, axis=0, keepdims=True)

        @pl.when(pl.program_id(0) == 0)
        def _():
            dg_ref[...] = part

        @pl.when(pl.program_id(0) > 0)
        def _():
            dg_ref[...] += part

    row = pl.BlockSpec((tm, width), lambda i: (i, 0))
    ins = [x, g.reshape(1, width), dy] + ([res] if has_res else [])
    in_specs = [pl.BlockSpec((tm, width), lambda i: (i, cb)), pl.BlockSpec((1, width), lambda i: (0, 0)), row] + ([row] if has_res else [])
    return _pallas(
        body, name=name, grid=(T // tm,), in_specs=in_specs,
        out_specs=[row, pl.BlockSpec((1, width), lambda i: (0, 0))],
        out_shape=[jax.ShapeDtypeStruct((T, width), out_dtype), jax.ShapeDtypeStruct((1, width), F32)],
        compiler_params=_params("arbitrary"),
    )(*ins)


def _rope_tables(T):
    half = A_ROPE // 2
    inv = ROPE_THETA ** (-jnp.arange(half, dtype=F32) / half)
    ang = jnp.arange(T, dtype=jnp.int32).astype(F32)[:, None] * inv[None, :]
    c32 = jnp.concatenate([jnp.cos(ang), jnp.cos(ang)], axis=-1)
    s32 = jnp.concatenate([jnp.sin(ang), jnp.sin(ang)], axis=-1)
    pad = A_PAD - A_QK
    cq = jnp.concatenate([jnp.ones((T, A_NOPE), F32), c32, jnp.ones((T, pad), F32)], axis=-1)
    sq = jnp.concatenate([jnp.zeros((T, A_NOPE), F32), s32, jnp.zeros((T, pad), F32)], axis=-1)
    ck = jnp.concatenate([c32, s32, jnp.zeros((T, LANES - 2 * A_ROPE), F32)], axis=-1)
    ck_t = jnp.concatenate([c32, s32], axis=-1).T
    return cq, sq, ck, ck_t


def _rope_swap_cols(w):
    half = A_ROPE // 2
    return jnp.concatenate([-w[..., half:], w[..., :half]], axis=-1)


def _rope_unswap_cols(g):
    half = A_ROPE // 2
    return jnp.concatenate([g[..., half:], -g[..., :half]], axis=-1)


A_PAD = LANES
A_W = A_HEADS * A_PAD
LOG2E = 1.4426950408889634
LN2 = 0.6931471805599453
Q_SCALE = A_QK ** -0.5 * LOG2E


def _qrope(cqn, wq2, cq, sq, *, name):
    T, R = cqn.shape
    W = A_W
    tm = _tile(T, 512)

    def body(x_ref, w_ref, c_ref, s_ref, o_ref, ot_ref):
        q2 = _dot(x_ref[...], w_ref[...], _NN)
        c = jnp.concatenate([c_ref[...]] * A_HEADS, axis=1)
        s = jnp.concatenate([s_ref[...]] * A_HEADS, axis=1)
        q = (q2[:, 0:W] * c + q2[:, W:2 * W] * s) * Q_SCALE
        o_ref[...] = q.astype(BF16)
        ot_ref[...] = q.T.astype(BF16)

    row = pl.BlockSpec((tm, W), lambda i: (i, 0))
    tab = pl.BlockSpec((tm, A_PAD), lambda i: (i, 0))
    return _pallas(body, name=name, grid=(T // tm,),
                   in_specs=[pl.BlockSpec((tm, R), lambda i: (i, 0)), pl.BlockSpec((R, 2 * W), lambda i: (0, 0)), tab, tab],
                   out_specs=[row, pl.BlockSpec((W, tm), lambda i: (0, i))],
                   out_shape=[jax.ShapeDtypeStruct((T, W), BF16), jax.ShapeDtypeStruct((W, T), BF16)],
                   compiler_params=_params("parallel"))(cqn, wq2, cq, sq)


def _kprep(kv, za, ck, *, name):
    T = kv.shape[0]
    tm = _tile(T, 512)

    def body(kv_ref, kr_ref, ck_ref, k_ref, vxt_ref):
        t = kr_ref[...] * ck_ref[...]
        krope = (t[:, 0:A_ROPE] + t[:, A_ROPE:2 * A_ROPE]).astype(BF16)
        one = (lax.broadcasted_iota(jnp.int32, (A_PAD - A_V, tm), 0) == 0).astype(BF16)
        for h in range(A_HEADS):
            k_ref[:, A_PAD * h:A_PAD * h + A_NOPE] = kv_ref[:, A_NOPE * h:A_NOPE * (h + 1)]
            k_ref[:, A_PAD * h + A_NOPE:A_PAD * h + A_QK] = krope
            k_ref[:, A_PAD * h + A_QK:A_PAD * (h + 1)] = jnp.zeros((tm, A_PAD - A_QK), BF16)
            vxt_ref[A_PAD * h + A_V:A_PAD * (h + 1), :] = one
        vt = kv_ref[:, 512:1024].astype(F32).T.astype(BF16)
        for h in range(A_HEADS):
            vxt_ref[A_PAD * h:A_PAD * h + A_V, :] = vt[A_V * h:A_V * (h + 1), :]

    wide = pl.BlockSpec((tm, A_W), lambda i: (i, 0))
    return _pallas(
        body, name=name, grid=(T // tm,),
        in_specs=[wide, pl.BlockSpec((tm, LANES), lambda i: (i, 3)), pl.BlockSpec((tm, LANES), lambda i: (i, 0))],
        out_specs=[wide, pl.BlockSpec((A_W, tm), lambda i: (0, i))],
        out_shape=[jax.ShapeDtypeStruct((T, A_W), BF16), jax.ShapeDtypeStruct((A_W, T), BF16)],
        compiler_params=_params("parallel"))(kv, za, ck)


def _kprep_bwd(dkt, ck_t, *, name):
    T = dkt.shape[1]
    tc = _tile(T, 512)

    def body(dk_ref, ck_ref, dn_ref, dr_ref):
        acc = jnp.zeros((A_ROPE, tc), F32)
        for h in range(A_HEADS):
            dn_ref[A_NOPE * h:A_NOPE * (h + 1), :] = dk_ref[A_PAD * h:A_PAD * h + A_NOPE, :].astype(BF16)
            acc = acc + dk_ref[A_PAD * h + A_NOPE:A_PAD * h + A_QK, :]
        dr_ref[0:A_ROPE, :] = (acc * ck_ref[0:A_ROPE, :]).astype(BF16)
        dr_ref[A_ROPE:2 * A_ROPE, :] = (acc * ck_ref[A_ROPE:2 * A_ROPE, :]).astype(BF16)
        dr_ref[2 * A_ROPE:LANES, :] = jnp.zeros((LANES - 2 * A_ROPE, tc), BF16)

    col = lambda r: pl.BlockSpec((r, tc), lambda i: (0, i))
    return _pallas(
        body, name=name, grid=(T // tc,), in_specs=[col(A_W), col(2 * A_ROPE)], out_specs=[col(512), col(LANES)],
        out_shape=[jax.ShapeDtypeStruct((512, T), BF16), jax.ShapeDtypeStruct((LANES, T), BF16)],
        compiler_params=_params("parallel"))(dkt, ck_t)


def _flash_fwd(qs, k, vxt, *, name):
    T = qs.shape[0]
    tq, tk = _tile(T, 512), _tile(T, 4096)
    nk = T // tk
    H, P, DV = A_HEADS, A_PAD, A_V

    def body(q_ref, k_ref, v_ref, o_ref, lse_ref, m_sc, acc_sc):
        j = pl.program_id(1)

        @pl.when(j == 0)
        def _():
            m_sc[...] = jnp.full(m_sc.shape, -jnp.inf, F32)
            acc_sc[...] = jnp.zeros(acc_sc.shape, F32)

        def scores(h):
            return _dot(k_ref[:, P * h:P * (h + 1)], q_ref[:, P * h:P * (h + 1)], _NT)

        st_next = scores(0)
        for h in range(H):
            st = st_next
            if h + 1 < H:
                st_next = scores(h + 1)
            m_prev = m_sc[h]
            m_new = jnp.maximum(m_prev, jnp.max(st, axis=0, keepdims=True))
            pt = jnp.exp2(st - m_new).astype(BF16)
            acc_sc[h] = jnp.exp2(m_prev - m_new) * acc_sc[h] + _dot(v_ref[P * h:P * (h + 1), :], pt, _NN)
            m_sc[h] = m_new

        @pl.when(j == nk - 1)
        def _():
            for h in range(H):
                acc = acc_sc[h]
                l = acc[DV:DV + 1, :]
                o_ref[:, DV * h:DV * (h + 1)] = (acc[0:DV, :] / l).T
                lse_ref[h] = m_sc[h] + jnp.log2(l)

    return _pallas(
        body, name=name, grid=(T // tq, nk),
        in_specs=[pl.BlockSpec((tq, A_W), lambda i, j: (i, 0)), pl.BlockSpec((tk, A_W), lambda i, j: (j, 0)),
                  pl.BlockSpec((A_W, tk), lambda i, j: (0, j))],
        out_specs=[pl.BlockSpec((tq, H * DV), lambda i, j: (i, 0)), pl.BlockSpec((H, 1, tq), lambda i, j: (0, 0, i))],
        out_shape=[jax.ShapeDtypeStruct((T, H * DV), F32), jax.ShapeDtypeStruct((H, 1, T), F32)],
        scratch_shapes=[pltpu.VMEM((H, 1, tq), F32), pltpu.VMEM((H, P, tq), F32)],
        compiler_params=pltpu.CompilerParams(dimension_semantics=("parallel", "arbitrary"), vmem_limit_bytes=VMEM_LIMIT_WIDE),
    )(qs, k, vxt)


def _attn_delta(o, do, *, name):
    T = o.shape[0]
    tm = _tile(T, 512, 8)

    def body(o_ref, do_ref, d_ref):
        prod = o_ref[...] * do_ref[...].astype(F32)
        for h in range(A_HEADS):
            d_ref[h] = jnp.sum(prod[:, A_V * h:A_V * (h + 1)], axis=-1, keepdims=True)

    row = pl.BlockSpec((tm, A_HEADS * A_V), lambda i: (i, 0))
    return _pallas(body, name=name, grid=(T // tm,), in_specs=[row, row],
                   out_specs=pl.BlockSpec((A_HEADS, tm, 1), lambda i: (0, i, 0)),
                   out_shape=jax.ShapeDtypeStruct((A_HEADS, T, 1), F32), compiler_params=_params("parallel"))(o, do)


def _flash_bwd(qs, qst, k, kv, do, dot_, lse2, delta, *, tiles=None, name):
    T = qs.shape[0]
    tq, tk = tiles or (_tile(T, 1024), _tile(T, 1024))
    nq, nk = T // tq, T // tk
    H, P, DV = A_HEADS, A_PAD, A_V

    def body(q_ref, qt_ref, k_ref, v_ref, do_ref, dot_ref, lse_ref, delta_ref, dq_ref, dkt_ref, dvt_ref):
        i = pl.program_id(1)

        @pl.when(i == 0)
        def _():
            dkt_ref[...] = jnp.zeros(dkt_ref.shape, F32)
            dvt_ref[...] = jnp.zeros(dvt_ref.shape, F32)

        for h in range(H):
            s = _dot(q_ref[:, P * h:P * (h + 1)], k_ref[:, P * h:P * (h + 1)], _NT)
            dp = _dot(do_ref[:, DV * h:DV * (h + 1)], v_ref[:, DV * h:DV * (h + 1)], _NT)
            p = jnp.exp2(s - lse_ref[h])
            ds = (p * (dp - delta_ref[h])).astype(BF16)
            pb = p.astype(BF16)
            dq_ref[0, :, P * h:P * (h + 1)] = _dot(ds, k_ref[:, P * h:P * (h + 1)], _NN).astype(BF16)
            dkt_ref[P * h:P * (h + 1), :] += _dot(qt_ref[P * h:P * (h + 1), :], ds, _NN)
            dvt_ref[DV * h:DV * (h + 1), :] += _dot(dot_ref[DV * h:DV * (h + 1), :], pb, _NN)

        @pl.when(i == nq - 1)
        def _():
            dkt_ref[...] = dkt_ref[...] * LN2

    qrow = lambda w: pl.BlockSpec((tq, w), lambda j, i: (i, 0))
    qcol = lambda r: pl.BlockSpec((r, tq), lambda j, i: (0, i))
    stat = pl.BlockSpec((H, tq, 1), lambda j, i: (0, i, 0))
    acc = lambda r: pl.BlockSpec((r, tk), lambda j, i: (0, j), pipeline_mode=pl.Buffered(1))
    return _pallas(
        body, name=name, grid=(nk, nq),
        in_specs=[qrow(A_W), qcol(A_W), pl.BlockSpec((tk, A_W), lambda j, i: (j, 0)), pl.BlockSpec((tk, H * DV), lambda j, i: (j, 1)),
                  qrow(H * DV), qcol(H * DV), stat, stat],
        out_specs=[pl.BlockSpec((1, tq, A_W), lambda j, i: (j, i, 0)), acc(A_W), acc(H * DV)],
        out_shape=[jax.ShapeDtypeStruct((nk, T, A_W), BF16), jax.ShapeDtypeStruct((A_W, T), F32),
                   jax.ShapeDtypeStruct((H * DV, T), F32)],
        compiler_params=pltpu.CompilerParams(dimension_semantics=("parallel", "arbitrary"), vmem_limit_bytes=VMEM_LIMIT_WIDE),
    )(qs, qst, k, kv, do, dot_, lse2, delta)


def _dq_sum(dq_part, cq, sq, *, name):
    n, T, W = dq_part.shape
    tm = _tile(T, 256, 16)

    def body(p_ref, c_ref, s_ref, o_ref):
        acc = p_ref[0].astype(F32)
        for j in range(1, n):
            acc = acc + p_ref[j].astype(F32)
        acc = acc * (A_QK ** -0.5)
        o_ref[:, 0:W] = (acc * jnp.concatenate([c_ref[...]] * A_HEADS, axis=1)).astype(BF16)
        o_ref[:, W:2 * W] = (acc * jnp.concatenate([s_ref[...]] * A_HEADS, axis=1)).astype(BF16)

    row = pl.BlockSpec((tm, A_PAD), lambda i: (i, 0))
    return _pallas(body, name=name, grid=(T // tm,), in_specs=[pl.BlockSpec((n, tm, W), lambda i: (0, i, 0)), row, row],
                   out_specs=pl.BlockSpec((tm, 2 * W), lambda i: (i, 0)), out_shape=jax.ShapeDtypeStruct((T, 2 * W), BF16),
                   compiler_params=_params("parallel"))(dq_part, cq, sq)


HB = 8 * B_CHUNK


def _chunk_masks(reverse):
    r = lax.broadcasted_iota(jnp.int32, (HB, HB), 0)
    c = lax.broadcasted_iota(jnp.int32, (HB, HB), 1)
    same = (r // B_CHUNK) == (c // B_CHUNK)
    incl = same & ((c >= r) if reverse else (c <= r))
    return same, incl


def _mask_mm(mask, x):
    hi = x.astype(BF16)
    lo = (x - hi.astype(F32)).astype(BF16)
    return _dot(mask, hi, _NN) + _dot(mask, lo, _NN)


def _hgrn_gates(q, z, lb, reverse):
    same, incl = _chunk_masks(reverse)
    sg = jax.nn.sigmoid(z)
    f = lb + (1.0 - lb) * sg
    lf = jnp.log(jnp.maximum(f, TINY))
    kk = (1.0 - lb) * jax.nn.sigmoid(-z)
    b = _mask_mm(incl.astype(BF16), lf)
    edge = 0 if reverse else B_CHUNK - 1
    btot = jnp.concatenate([jnp.broadcast_to(b[B_CHUNK * c + edge:B_CHUNK * c + edge + 1, :], (B_CHUNK, b.shape[1]))
                            for c in range(HB // B_CHUNK)], axis=0)
    eb, enb, er, dec = jnp.exp(b), jnp.exp(-b), jnp.exp(btot - b), jnp.exp(btot)
    return dict(same=same, incl=incl, sg=sg, f=f, kk=kk, eb=eb, enb=enb, er=er, dec=dec,
                qd=q * eb, ki=kk * enb, ke=kk * er)


def _hgrn_specs(T, reverse, gate_reverse):
    nb = T // HB
    blk = (lambda i: nb - 1 - i) if reverse else (lambda i: i)
    wide = B_HEADS * B_DK
    return nb, blk, [
        pl.BlockSpec((HB, wide), lambda i: (blk(i), 0)),
        pl.BlockSpec((HB, wide), lambda i: (blk(i), 2 if gate_reverse else 1)),
        pl.BlockSpec((HB, B_HEADS * B_DV), lambda i: (blk(i), 6)),
        pl.BlockSpec((1, wide), lambda i: (0, 0)),
    ]


def _hk(h):
    return slice(B_DK * h, B_DK * (h + 1))


def _hv(h):
    return slice(B_DV * h, B_DV * (h + 1))


def _crows(c):
    return slice(B_CHUNK * c, B_CHUNK * (c + 1))


def _chunk_selectors():
    r = lax.broadcasted_iota(jnp.int32, (HB, 1), 0) // B_CHUNK
    l = lax.broadcasted_iota(jnp.int32, (1, HB), 1) // B_CHUNK
    return [r == c for c in range(8)], [l == c for c in range(8)]


def _hgrn_fwd(zb, lb, *, reverse, name):
    T = zb.shape[0]
    nb, blk, in_specs = _hgrn_specs(T, reverse, reverse)
    order = range(7, -1, -1) if reverse else range(8)
    heads = range(B_HEADS)

    def body(q_ref, z_ref, v_ref, lb_ref, o_ref, st_ref, s_sc):
        @pl.when(pl.program_id(0) == 0)
        def _():
            s_sc[...] = jnp.zeros(s_sc.shape, F32)

        g = _hgrn_gates(q_ref[...], z_ref[...], lb_ref[...], reverse)
        v = v_ref[...].astype(BF16)
        qd, ki, ke = g["qd"].astype(BF16), g["ki"].astype(BF16), g["ke"].astype(BF16)
        dec = g["dec"]
        in_chunk_rows, in_chunk_lanes = _chunk_selectors()
        o_intra, upd = [], []
        for h in heads:
            a = jnp.where(g["incl"], _dot(qd[:, _hk(h)], ki[:, _hk(h)], _NT), 0.0)
            o_intra.append(_dot(a.astype(BF16), v[:, _hv(h)], _NN))
            vt = v[:, _hv(h)].T
            lhs = jnp.concatenate([jnp.where(in_chunk_lanes[c], vt, 0) for c in range(8)], axis=0)
            upd.append(_dot(lhs, ke[:, _hk(h)], _NN))
        st = [s_sc[h] for h in heads]
        snap = [[None] * 8 for _ in heads]
        for c in order:
            for h in heads:
                snap[h][c] = st[h]
                st[h] = st[h] * dec[B_CHUNK * c:B_CHUNK * c + 1, _hk(h)] + upd[h][B_DV * c:B_DV * (c + 1), :]
        for h in heads:
            s_sc[h] = st[h]
            for c in range(8):
                st_ref[h, c] = snap[h][c]
            qd_big = jnp.concatenate([jnp.where(in_chunk_rows[c], qd[:, _hk(h)], 0) for c in range(8)], axis=1)
            states = jnp.concatenate([snap[h][c].astype(BF16) for c in range(8)], axis=1)
            o_ref[h] = o_intra[h] + _dot(qd_big, states, _NT)

    return _pallas(
        body, name=name, grid=(nb,), in_specs=in_specs,
        out_specs=[pl.BlockSpec((B_HEADS, HB, B_DV), lambda i: (0, blk(i), 0)),
                   pl.BlockSpec((B_HEADS, 8, B_DV, B_DK), lambda i: (0, blk(i), 0, 0))],
        out_shape=[jax.ShapeDtypeStruct((B_HEADS, T, B_DV), F32),
                   jax.ShapeDtypeStruct((B_HEADS, T // B_CHUNK, B_DV, B_DK), F32)],
        scratch_shapes=[pltpu.VMEM((B_HEADS, B_DV, B_DK), F32)],
        compiler_params=_params("arbitrary"))(zb, zb, zb, lb)


def _hgrn_bwd(zb, lb, do, states, *, reverse, name):
    T = zb.shape[0]
    nb, blk, in_specs = _hgrn_specs(T, not reverse, reverse)
    order = range(8) if reverse else range(7, -1, -1)
    heads = range(B_HEADS)

    def body(q_ref, z_ref, v_ref, lb_ref, do_ref, st_ref, dq_ref, dz_ref, dv_ref, dlb_ref, ds_sc):
        @pl.when(pl.program_id(0) == 0)
        def _():
            ds_sc[...] = jnp.zeros(ds_sc.shape, F32)
            dlb_ref[...] = jnp.zeros(dlb_ref.shape, F32)

        lb = lb_ref[...]
        g = _hgrn_gates(q_ref[...], z_ref[...], lb, reverse)
        v = v_ref[...].astype(BF16)
        qd, ki, ke = g["qd"].astype(BF16), g["ki"].astype(BF16), g["ke"].astype(BF16)
        dec = g["dec"]
        dout = [do_ref[h].astype(BF16) for h in heads]
        in_chunk_rows, in_chunk_lanes = _chunk_selectors()
        _, incl_t = _chunk_masks(not reverse)
        rows_of = lambda x: jnp.concatenate([jnp.where(in_chunk_rows[c], x, 0) for c in range(8)], axis=1)
        dv_i, dqd_h, dki, upd = [], [], [], []
        for h in heads:
            qd_h, ki_h, v_h = qd[:, _hk(h)], ki[:, _hk(h)], v[:, _hv(h)]
            da = jnp.where(g["incl"], _dot(dout[h], v_h, _NT), 0.0).astype(BF16)
            at = jnp.where(incl_t, _dot(ki_h, qd_h, _NT), 0.0).astype(BF16)
            dat = jnp.where(incl_t, _dot(v_h, dout[h], _NT), 0.0).astype(BF16)
            dv_i.append(_dot(at, dout[h], _NN))
            dki.append(_dot(dat, qd_h, _NN))
            dot_t = dout[h].T
            lhs = jnp.concatenate([jnp.where(in_chunk_lanes[c], dot_t, 0) for c in range(8)], axis=0)
            upd.append(_dot(lhs, qd_h, _NN))
            saved = jnp.concatenate([st_ref[h, c].astype(BF16) for c in range(8)], axis=0)
            dqd_h.append(_dot(da, ki_h, _NN) + _dot(rows_of(dout[h]), saved, _NN))
        dst = [ds_sc[h] for h in heads]
        used = [[None] * 8 for _ in heads]
        for c in order:
            for h in heads:
                used[h][c] = dst[h]
                dst[h] = dst[h] * dec[B_CHUNK * c:B_CHUNK * c + 1, _hk(h)] + upd[h][B_DV * c:B_DV * (c + 1), :]
        dke_h, ddec_h = [], []
        for h in heads:
            ds_sc[h] = dst[h]
            used16 = [used[h][c].astype(BF16) for c in range(8)]
            dv_ref[h] = dv_i[h] + _dot(rows_of(ke[:, _hk(h)]), jnp.concatenate(used16, axis=1), _NT)
            dke_h.append(_dot(rows_of(v[:, _hv(h)]), jnp.concatenate(used16, axis=0), _NN))
            ddec_p = []
            for c in range(8):
                tot = jnp.sum(used[h][c] * st_ref[h, c], axis=0, keepdims=True) * dec[B_CHUNK * c:B_CHUNK * c + 1, _hk(h)]
                ddec_p.append(jnp.broadcast_to(tot, (B_CHUNK, B_DK)))
            ddec_h.append(jnp.concatenate(ddec_p, axis=0))
        dqd = jnp.concatenate(dqd_h, axis=1)
        dki = jnp.concatenate(dki, axis=1)
        dke = jnp.concatenate(dke_h, axis=1)
        db = dqd * g["qd"] - dki * g["ki"] - dke * g["ke"]
        masks = jnp.concatenate([incl_t.astype(BF16), g["same"].astype(BF16)], axis=1)
        dlf = _mask_mm(masks, jnp.concatenate([db, dke * g["ke"]], axis=0)) + jnp.concatenate(ddec_h, axis=1)
        dk = dki * g["enb"] + dke * g["er"]
        u = jnp.where(g["f"] > TINY, dlf / g["f"], 0.0) - dk
        sg = g["sg"]
        dq_ref[...] = dqd * g["eb"]
        dz_ref[...] = u * (1.0 - lb) * sg * (1.0 - sg)
        dlb_ref[...] += jnp.sum(u * (1.0 - sg), axis=0, keepdims=True)

    wide = pl.BlockSpec((HB, B_HEADS * B_DK), lambda i: (blk(i), 0))
    hm = pl.BlockSpec((B_HEADS, HB, B_DV), lambda i: (0, blk(i), 0))
    return _pallas(
        body, name=name, grid=(nb,),
        in_specs=in_specs + [hm, pl.BlockSpec((B_HEADS, 8, B_DV, B_DK), lambda i: (0, blk(i), 0, 0))],
        out_specs=[wide, wide, hm, pl.BlockSpec((1, B_HEADS * B_DK), lambda i: (0, 0))],
        out_shape=[jax.ShapeDtypeStruct((T, B_HEADS * B_DK), F32), jax.ShapeDtypeStruct((T, B_HEADS * B_DK), F32),
                   jax.ShapeDtypeStruct((B_HEADS, T, B_DV), F32), jax.ShapeDtypeStruct((1, B_HEADS * B_DK), F32)],
        scratch_shapes=[pltpu.VMEM((B_HEADS, B_DV, B_DK), F32)],
        compiler_params=_params("arbitrary"))(zb, zb, zb, lb, do, states)


def _hgrn_out(of, ob, zb, gout, *, name):
    T = zb.shape[0]
    tm = _tile(T, 512, 8)

    def body(of_ref, ob_ref, g_ref, gout_ref, y_ref):
        for h in range(B_HEADS):
            o = of_ref[h] + ob_ref[h]
            r = lax.rsqrt(jnp.mean(o * o, axis=-1, keepdims=True) + EPS)
            gh = g_ref[:, B_DV * h:B_DV * (h + 1)]
            y_ref[:, B_DV * h:B_DV * (h + 1)] = (o * r * gout_ref[...] * (gh * jax.nn.sigmoid(gh))).astype(BF16)

    hm = pl.BlockSpec((B_HEADS, tm, B_DV), lambda i: (0, i, 0))
    return _pallas(
        body, name=name, grid=(T // tm,),
        in_specs=[hm, hm, pl.BlockSpec((tm, 512), lambda i: (i, 7)), pl.BlockSpec((1, B_DV), lambda i: (0, 0))],
        out_specs=pl.BlockSpec((tm, 512), lambda i: (i, 0)),
        out_shape=jax.ShapeDtypeStruct((T, 512), BF16), compiler_params=_params("parallel"))(of, ob, zb, gout.reshape(1, B_DV))


def _hgrn_out_bwd(of, ob, zb, gout, dy, *, name):
    T = zb.shape[0]
    tm = _tile(T, 512, 8)

    def body(of_ref, ob_ref, g_ref, gout_ref, dy_ref, do_ref, dg_ref, dgo_ref):
        gout_v = gout_ref[...]
        acc = jnp.zeros((1, B_DV), F32)
        for h in range(B_HEADS):
            o = of_ref[h] + ob_ref[h]
            r = lax.rsqrt(jnp.mean(o * o, axis=-1, keepdims=True) + EPS)
            oh = o * r
            gh = g_ref[:, B_DV * h:B_DV * (h + 1)]
            sg = jax.nn.sigmoid(gh)
            dyh = dy_ref[:, B_DV * h:B_DV * (h + 1)].astype(F32)
            dn = dyh * (gh * sg)
            dg_ref[:, B_DV * h:B_DV * (h + 1)] = dyh * (oh * gout_v) * (sg * (1.0 + gh * (1.0 - sg)))
            dxh = dn * gout_v
            do_ref[h] = r * (dxh - oh * jnp.mean(dxh * oh, axis=-1, keepdims=True))
            acc = acc + jnp.sum(dn * oh, axis=0, keepdims=True)

        @pl.when(pl.program_id(0) == 0)
        def _():
            dgo_ref[...] = acc

        @pl.when(pl.program_id(0) > 0)
        def _():
            dgo_ref[...] += acc

    hm = pl.BlockSpec((B_HEADS, tm, B_DV), lambda i: (0, i, 0))
    row = pl.BlockSpec((tm, 512), lambda i: (i, 0))
    return _pallas(
        body, name=name, grid=(T // tm,),
        in_specs=[hm, hm, pl.BlockSpec((tm, 512), lambda i: (i, 7)), pl.BlockSpec((1, B_DV), lambda i: (0, 0)), row],
        out_specs=[hm, row, pl.BlockSpec((1, B_DV), lambda i: (0, 0))],
        out_shape=[jax.ShapeDtypeStruct((B_HEADS, T, B_DV), F32), jax.ShapeDtypeStruct((T, 512), F32),
                   jax.ShapeDtypeStruct((1, B_DV), F32)],
        compiler_params=_params("arbitrary"))(of, ob, zb, gout.reshape(1, B_DV), dy)


def _dzb_assemble(dq_f, dq_b, dzf, dzb_, dv_f, dv_b, dgate, *, name):
    T = dq_f.shape[0]
    tm = _tile(T, 256, 8)

    def body(qf, qb, zf, zr, vf, vr, dg, o_ref):
        o_ref[:, 0:1024] = (qf[...] + qb[...]).astype(BF16)
        o_ref[:, 1024:2048] = zf[...].astype(BF16)
        o_ref[:, 2048:3072] = zr[...].astype(BF16)
        for h in range(B_HEADS):
            o_ref[:, 3072 + B_DV * h:3072 + B_DV * (h + 1)] = (vf[h] + vr[h]).astype(BF16)
        o_ref[:, 3584:4096] = dg[...].astype(BF16)

    wide = pl.BlockSpec((tm, 1024), lambda i: (i, 0))
    hm = pl.BlockSpec((B_HEADS, tm, B_DV), lambda i: (0, i, 0))
    return _pallas(
        body, name=name, grid=(T // tm,),
        in_specs=[wide, wide, wide, wide, hm, hm, pl.BlockSpec((tm, 512), lambda i: (i, 0))],
        out_specs=pl.BlockSpec((tm, 4096), lambda i: (i, 0)),
        out_shape=jax.ShapeDtypeStruct((T, 4096), BF16), compiler_params=_params("parallel"))(dq_f, dq_b, dzf, dzb_, dv_f, dv_b, dgate)


C_SPAN = 3 * C_BLOCK
C_G = C_HEADS // C_KV_HEADS


def _t5_bucket(rel):
    nb = REL_BUCKETS // 2
    max_exact = nb // 2
    ret = (rel > 0).astype(jnp.int32) * nb
    n = jnp.abs(rel)
    large = max_exact + (jnp.log(jnp.maximum(n, 1).astype(F32) / max_exact)
                         / math.log(REL_MAX_DIST / max_exact) * (nb - max_exact)).astype(jnp.int32)
    large = jnp.minimum(large, nb - 1)
    return ret + jnp.where(n < max_exact, n, large)


def _swa_buckets():
    rel = jnp.arange(C_SPAN)[None, :] - C_BLOCK - jnp.arange(C_BLOCK)[:, None]
    return _t5_bucket(rel)


def _swa_specs(T):
    nb = T // C_BLOCK
    return nb, [
        pl.BlockSpec((C_BLOCK, 512), lambda n: (n, 0)),
        pl.BlockSpec((C_BLOCK, LANES), lambda n: (jnp.maximum(n - 1, 0), 4)),
        pl.BlockSpec((C_BLOCK, LANES), lambda n: (n, 4)),
        pl.BlockSpec((C_BLOCK, LANES), lambda n: (jnp.minimum(n + 1, nb - 1), 4)),
        pl.BlockSpec((C_BLOCK, LANES), lambda n: (jnp.maximum(n - 1, 0), 5)),
        pl.BlockSpec((C_BLOCK, LANES), lambda n: (n, 5)),
        pl.BlockSpec((C_BLOCK, LANES), lambda n: (jnp.minimum(n + 1, nb - 1), 5)),
        pl.BlockSpec((C_HEADS, C_BLOCK, C_SPAN), lambda n: (0, 0, 0)),
        pl.BlockSpec(memory_space=pltpu.SMEM),
    ]


def _swa_valid(n, T):
    qi = lax.broadcasted_iota(jnp.int32, (C_BLOCK, C_SPAN), 0)
    si = lax.broadcasted_iota(jnp.int32, (C_BLOCK, C_SPAN), 1)
    rel = si - C_BLOCK - qi
    kpos = (n - 1) * C_BLOCK + si
    return (jnp.abs(rel) <= C_WINDOW) & (kpos >= 0) & (kpos < T)


def _swa_softmax(raw, bias, valid, sink):
    s = raw * (C_DH ** -0.5) + bias
    s = jnp.where(valid, s, MASK_VALUE)
    m = jnp.maximum(jnp.max(s, axis=-1, keepdims=True), sink)
    e = jnp.exp(s - m)
    den = jnp.sum(e, axis=-1, keepdims=True) + jnp.exp(sink - m)
    return e / den, jnp.exp(sink - m) / den


def _swa_fwd(zc, bias, sink, *, name):
    T = zc.shape[0]
    nb, in_specs = _swa_specs(T)

    def body(q_ref, kp, kc, kn, vp, vc, vn, bias_ref, sink_ref, y_ref):
        n = pl.program_id(0)
        kcat = jnp.concatenate([kp[...], kc[...], kn[...]], axis=0)
        vcat = jnp.concatenate([vp[...], vc[...], vn[...]], axis=0)
        valid = _swa_valid(n, T)
        heads = range(C_HEADS)
        kvs = [slice(C_DH * (h // C_G), C_DH * (h // C_G + 1)) for h in heads]
        scores = [_dot(q_ref[:, C_DH * h:C_DH * (h + 1)], kcat[:, kvs[h]], _NT) for h in heads]
        probs = [_swa_softmax(scores[h], bias_ref[h], valid, sink_ref[h])[0].astype(BF16) for h in heads]
        for h in heads:
            y_ref[:, C_DH * h:C_DH * (h + 1)] = _dot(probs[h], vcat[:, kvs[h]], _NN).astype(BF16)

    return _pallas(
        body, name=name, grid=(nb,), in_specs=in_specs, out_specs=pl.BlockSpec((C_BLOCK, 512), lambda n: (n, 0)),
        out_shape=jax.ShapeDtypeStruct((T, 512), BF16), compiler_params=_params("parallel"))(zc, zc, zc, zc, zc, zc, zc, bias, sink)


def _swa_bwd(zc, bias, sink, dy, *, name):
    T = zc.shape[0]
    nb, in_specs = _swa_specs(T)
    scale = C_DH ** -0.5

    def body(q_ref, kp, kc, kn, vp, vc, vn, bias_ref, sink_ref, dy_ref, dq_ref, dkc_ref, dvc_ref, dbias_ref, dsink_ref):
        n = pl.program_id(0)

        @pl.when(n == 0)
        def _():
            dbias_ref[...] = jnp.zeros(dbias_ref.shape, F32)
            dsink_ref[...] = jnp.zeros(dsink_ref.shape, F32)

        kcat = jnp.concatenate([kp[...], kc[...], kn[...]], axis=0)
        vcat = jnp.concatenate([vp[...], vc[...], vn[...]], axis=0)
        valid = _swa_valid(n, T)
        heads = range(C_HEADS)
        kvs = [slice(C_DH * (h // C_G), C_DH * (h // C_G + 1)) for h in heads]
        qs = [q_ref[:, C_DH * h:C_DH * (h + 1)] for h in heads]
        dos = [dy_ref[:, C_DH * h:C_DH * (h + 1)].astype(BF16) for h in heads]
        scores = [_dot(qs[h], kcat[:, kvs[h]], _NT) for h in heads]
        dps = [_dot(dos[h], vcat[:, kvs[h]], _NT) for h in heads]
        pbs, dsbs = [], []
        for h in heads:
            p, p_sink = _swa_softmax(scores[h], bias_ref[h], valid, sink_ref[h])
            rowdot = jnp.sum(p * dps[h], axis=-1, keepdims=True)
            ds = p * (dps[h] - rowdot)
            dbias_ref[h] += ds
            tot = jnp.sum(jnp.sum(-p_sink * rowdot, axis=0, keepdims=True), axis=1, keepdims=True)
            dsink_ref[h:h + 1, :] += jnp.broadcast_to(tot, (1, LANES))
            pbs.append(p.astype(BF16))
            dsbs.append((ds * scale).astype(BF16))
        for h in heads:
            dq_ref[:, C_DH * h:C_DH * (h + 1)] = _dot(dsbs[h], kcat[:, kvs[h]], _NN).astype(BF16)
        dks = [_dot(dsbs[h], qs[h], _TN) for h in heads]
        dvs = [_dot(pbs[h], dos[h], _TN) for h in heads]
        for kv in range(C_KV_HEADS):
            group = range(kv * C_G, (kv + 1) * C_G)
            dkc_ref[0, :, C_DH * kv:C_DH * (kv + 1)] = sum(dks[h] for h in group)
            dvc_ref[0, :, C_DH * kv:C_DH * (kv + 1)] = sum(dvs[h] for h in group)

    part = pl.BlockSpec((1, C_SPAN, LANES), lambda n: (n, 0, 0))
    dq, dkc, dvc, dbias, dsink = _pallas(
        body, name=name, grid=(nb,), in_specs=in_specs + [pl.BlockSpec((C_BLOCK, 512), lambda n: (n, 0))],
        out_specs=[pl.BlockSpec((C_BLOCK, 512), lambda n: (n, 0)), part, part,
                   pl.BlockSpec((C_HEADS, C_BLOCK, C_SPAN), lambda n: (0, 0, 0)), pl.BlockSpec((C_HEADS, LANES), lambda n: (0, 0))],
        out_shape=[jax.ShapeDtypeStruct((T, 512), BF16), jax.ShapeDtypeStruct((nb, C_SPAN, LANES), F32),
                   jax.ShapeDtypeStruct((nb, C_SPAN, LANES), F32), jax.ShapeDtypeStruct((C_HEADS, C_BLOCK, C_SPAN), F32),
                   jax.ShapeDtypeStruct((C_HEADS, LANES), F32)],
        compiler_params=_params("arbitrary"))(zc, zc, zc, zc, zc, zc, zc, bias, sink, dy)

    def combine(dq_ref, kp, kc, kn, vp, vc, vn, o_ref):
        n = pl.program_id(0)
        lo = (n > 0).astype(F32)
        hi = (n < nb - 1).astype(F32)
        o_ref[:, 0:512] = dq_ref[...]
        o_ref[:, 512:640] = (kp[0] * lo + kc[0] + kn[0] * hi).astype(BF16)
        o_ref[:, 640:768] = (vp[0] * lo + vc[0] + vn[0] * hi).astype(BF16)

    prev = pl.BlockSpec((1, C_BLOCK, LANES), lambda n: (jnp.maximum(n - 1, 0), 2, 0))
    cur = pl.BlockSpec((1, C_BLOCK, LANES), lambda n: (n, 1, 0))
    nxt = pl.BlockSpec((1, C_BLOCK, LANES), lambda n: (jnp.minimum(n + 1, nb - 1), 0, 0))
    dzc = _pallas(
        combine, name=name + "_combine", grid=(nb,),
        in_specs=[pl.BlockSpec((C_BLOCK, 512), lambda n: (n, 0)), prev, cur, nxt, prev, cur, nxt],
        out_specs=pl.BlockSpec((C_BLOCK, 768), lambda n: (n, 0)),
        out_shape=jax.ShapeDtypeStruct((T, 768), BF16), compiler_params=_params("parallel"))(dq, dkc, dkc, dkc, dvc, dvc, dvc)
    return dzc, dbias, dsink


def _merge_tiles(T):
    return _tile(T, 512, 8), 512


def _merge_fwd(ya, yb, yc, wa, wb, wc, zg, *, name):
    T = ya.shape[0]
    tm, tn = _merge_tiles(T)
    nd = D_MODEL // tn

    def body(ya_ref, yb_ref, yc_ref, wa_ref, wb_ref, wc_ref, ga_ref, gb_ref, gc_ref, o_ref):
        acc = jax.nn.sigmoid(ga_ref[...].astype(F32)) * _dot(ya_ref[...].astype(BF16), wa_ref[...], _NN)
        acc += jax.nn.sigmoid(gb_ref[...].astype(F32)) * _dot(yb_ref[...].astype(BF16), wb_ref[...], _NN)
        acc += jax.nn.sigmoid(gc_ref[...].astype(F32)) * _dot(yc_ref[...].astype(BF16), wc_ref[...], _NN)
        o_ref[...] = acc.astype(BF16)

    y = pl.BlockSpec((tm, 512), lambda i, j: (i, 0))
    w = pl.BlockSpec((512, tn), lambda i, j: (0, j))
    gate = lambda b: pl.BlockSpec((tm, tn), lambda i, j: (i, b * nd + j))
    return _pallas(
        body, name=name, grid=(T // tm, nd), in_specs=[y, y, y, w, w, w, gate(0), gate(1), gate(2)],
        out_specs=pl.BlockSpec((tm, tn), lambda i, j: (i, j)),
        out_shape=jax.ShapeDtypeStruct((T, D_MODEL), BF16),
        compiler_params=_params("parallel", "parallel"))(ya, yb, yc, wa, wb, wc, zg, zg, zg)


def _merge_bwd(ya, yb, yc, wa, wb, wc, zg, dm, *, name):
    T = ya.shape[0]
    tm, tn = _merge_tiles(T)
    nd = D_MODEL // tn

    def body(ya_ref, yb_ref, yc_ref, wa_ref, wb_ref, wc_ref, ga_ref, gb_ref, gc_ref, dm_ref, *outs):
        dmv = dm_ref[...].astype(F32)
        for y_ref, w_ref, g_ref, du_ref, dg_ref in zip((ya_ref, yb_ref, yc_ref), (wa_ref, wb_ref, wc_ref),
                                                       (ga_ref, gb_ref, gc_ref), outs[:3], outs[3:]):
            u = _dot(y_ref[...].astype(BF16), w_ref[...], _NN)
            sg = jax.nn.sigmoid(g_ref[...].astype(F32))
            du_ref[...] = (dmv * sg).astype(BF16)
            dg_ref[...] = (dmv * u * sg * (1.0 - sg)).astype(BF16)

    y = pl.BlockSpec((tm, 512), lambda i, j: (i, 0))
    w = pl.BlockSpec((512, tn), lambda i, j: (0, j))
    gate = lambda b: pl.BlockSpec((tm, tn), lambda i, j: (i, b * nd + j))
    t = pl.BlockSpec((tm, tn), lambda i, j: (i, j))
    return _pallas(
        body, name=name, grid=(T // tm, nd), in_specs=[y, y, y, w, w, w, gate(0), gate(1), gate(2), t],
        out_specs=[t] * 6, out_shape=[jax.ShapeDtypeStruct((T, D_MODEL), BF16)] * 6,
        compiler_params=_params("parallel", "parallel"))(ya, yb, yc, wa, wb, wc, zg, zg, zg, dm)


def _cross_fwd(q, kvm, *, name):
    T = q.shape[0]
    M = kvm.shape[0]
    tm = _tile(T, 512, 8)
    scale = X_DH ** -0.5

    def body(q_ref, k_ref, v_ref, o_ref):
        for h in range(X_HEADS):
            cs = slice(X_DH * h, X_DH * (h + 1))
            s = _dot(q_ref[:, cs], k_ref[:, cs], _NT) * scale
            e = jnp.exp(s - jnp.max(s, axis=-1, keepdims=True))
            p = e / jnp.sum(e, axis=-1, keepdims=True)
            o_ref[:, cs] = _dot(p.astype(BF16), v_ref[:, cs], _NN).astype(BF16)

    row = pl.BlockSpec((tm, D_MODEL), lambda i: (i, 0))
    return _pallas(
        body, name=name, grid=(T // tm,),
        in_specs=[row, pl.BlockSpec((M, D_MODEL), lambda i: (0, 0)), pl.BlockSpec((M, D_MODEL), lambda i: (0, 1))],
        out_specs=row, out_shape=jax.ShapeDtypeStruct((T, D_MODEL), BF16), compiler_params=_params("parallel"))(q, kvm, kvm)


def _cross_bwd(q, kvm, do, *, name):
    T = q.shape[0]
    M = kvm.shape[0]
    tm = _tile(T, 512, 8)
    scale = X_DH ** -0.5

    def body(q_ref, k_ref, v_ref, do_ref, dq_ref, dkv_ref):
        @pl.when(pl.program_id(0) == 0)
        def _():
            dkv_ref[...] = jnp.zeros(dkv_ref.shape, F32)

        for h in range(X_HEADS):
            cs = slice(X_DH * h, X_DH * (h + 1))
            vs = slice(D_MODEL + X_DH * h, D_MODEL + X_DH * (h + 1))
            qh, kh, doh = q_ref[:, cs], k_ref[:, cs], do_ref[:, cs]
            s = _dot(qh, kh, _NT) * scale
            e = jnp.exp(s - jnp.max(s, axis=-1, keepdims=True))
            p = e / jnp.sum(e, axis=-1, keepdims=True)
            dp = _dot(doh, v_ref[:, cs], _NT)
            ds = (p * (dp - jnp.sum(p * dp, axis=-1, keepdims=True)) * scale).astype(BF16)
            dq_ref[:, cs] = _dot(ds, kh, _NN).astype(BF16)
            dkv_ref[:, cs] += _dot(ds, qh, _TN)
            dkv_ref[:, vs] += _dot(p.astype(BF16), doh, _TN)

    row = pl.BlockSpec((tm, D_MODEL), lambda i: (i, 0))
    return _pallas(
        body, name=name, grid=(T // tm,),
        in_specs=[row, pl.BlockSpec((M, D_MODEL), lambda i: (0, 0)), pl.BlockSpec((M, D_MODEL), lambda i: (0, 1)), row],
        out_specs=[row, pl.BlockSpec((M, 2 * D_MODEL), lambda i: (0, 0))],
        out_shape=[jax.ShapeDtypeStruct((T, D_MODEL), BF16), jax.ShapeDtypeStruct((M, 2 * D_MODEL), F32)],
        compiler_params=_params("arbitrary"))(q, kvm, kvm, do)


def _ffn_up(h, w1, w3, *, name):
    T = h.shape[0]
    tm = _tile(T, 512, 8)
    tn = D_FF // 2

    def body(h_ref, w1_ref, w3_ref, a_ref, b_ref, act_ref):
        hv = h_ref[...]
        a = _dot(hv, w1_ref[...], _NN)
        b = _dot(hv, w3_ref[...], _NN)
        a_ref[...] = a.astype(BF16)
        b_ref[...] = b.astype(BF16)
        act_ref[...] = (a * jax.nn.sigmoid(a) * b).astype(BF16)

    w = pl.BlockSpec((D_MODEL, tn), lambda i, j: (0, j))
    t = pl.BlockSpec((tm, tn), lambda i, j: (i, j))
    return _pallas(
        body, name=name, grid=(T // tm, D_FF // tn), in_specs=[pl.BlockSpec((tm, D_MODEL), lambda i, j: (i, 0)), w, w],
        out_specs=[t, t, t],
        out_shape=[jax.ShapeDtypeStruct((T, D_FF), BF16)] * 3,
        compiler_params=_params("parallel", "parallel"))(h, w1, w3)


def _ffn_dact(dx, w2, a, b, *, name):
    T = dx.shape[0]
    tm = _tile(T, 512, 8)
    tn = D_FF // 2

    def body(dx_ref, w2_ref, a_ref, b_ref, da_ref, db_ref):
        dact = _dot(dx_ref[...].astype(BF16), w2_ref[...], _NT)
        av = a_ref[...].astype(F32)
        sg = jax.nn.sigmoid(av)
        da_ref[...] = (dact * b_ref[...].astype(F32) * (sg * (1.0 + av * (1.0 - sg)))).astype(BF16)
        db_ref[...] = (dact * (av * sg)).astype(BF16)

    t = pl.BlockSpec((tm, tn), lambda i, j: (i, j))
    return _pallas(
        body, name=name, grid=(T // tm, D_FF // tn),
        in_specs=[pl.BlockSpec((tm, D_MODEL), lambda i, j: (i, 0)), pl.BlockSpec((tn, D_MODEL), lambda i, j: (j, 0)), t, t],
        out_specs=[t, t], out_shape=[jax.ShapeDtypeStruct((T, D_FF), BF16)] * 2,
        compiler_params=_params("parallel", "parallel"))(dx, w2, a, b)


def _loss_head(x, g, target, *, name):
    T, D = x.shape
    tm = _tile(T, 512, 8)

    def body(x_ref, g_ref, t_ref, loss_ref, dx_ref, dg_ref):
        xv = x_ref[...]
        r = lax.rsqrt(jnp.mean(xv * xv, axis=-1, keepdims=True) + EPS)
        xh = xv * r
        gv = g_ref[...]
        err = xh * gv - t_ref[...]
        dy = err * (1.0 / D)
        dxh = dy * gv
        dx_ref[...] = r * (dxh - xh * jnp.mean(dxh * xh, axis=-1, keepdims=True))
        lpart = 0.5 * jnp.sum(jnp.mean(err * err, axis=-1, keepdims=True), axis=0, keepdims=True)
        gpart = jnp.sum(dy * xh, axis=0, keepdims=True)

        @pl.when(pl.program_id(0) == 0)
        def _():
            loss_ref[...] = jnp.broadcast_to(lpart, (1, LANES))
            dg_ref[...] = gpart

        @pl.when(pl.program_id(0) > 0)
        def _():
            loss_ref[...] += jnp.broadcast_to(lpart, (1, LANES))
            dg_ref[...] += gpart

    row = pl.BlockSpec((tm, D), lambda i: (i, 0))
    vec = pl.BlockSpec((1, D), lambda i: (0, 0))
    return _pallas(
        body, name=name, grid=(T // tm,), in_specs=[row, vec, row],
        out_specs=[pl.BlockSpec((1, LANES), lambda i: (0, 0)), row, vec],
        out_shape=[jax.ShapeDtypeStruct((1, LANES), F32), jax.ShapeDtypeStruct((T, D), F32), jax.ShapeDtypeStruct((1, D), F32)],
        compiler_params=_params("arbitrary"))(x, g.reshape(1, D), target)


IN_CQ, IN_CKV, IN_KR, IN_B, IN_C, IN_G, IN_END = 0, 384, 640, 672, 4768, 5536, 8608
WEIGHT_NAMES = ("w_in", "g_mix", "a_gq", "a_gkv", "a_wuq", "a_wukv", "b_lb", "b_gout", "c_sink", "rel_bias",
                "w_br_a", "w_br_b", "w_br_c", "w_out", "g_x", "g_mem", "x_wq", "x_wkv", "x_wo", "g_ffn",
                "f_w1", "f_w3", "f_w2", "g_final")


def _lower_bounds(b_lb):
    sm = jax.nn.softmax(b_lb.astype(F32), axis=1)
    return jnp.cumsum(sm, axis=1) - sm[:, :1]


def _layer_weights(w, l):
    bf = lambda a: a.astype(BF16)
    w_in = bf(w["w_in"][l])
    kr = w_in[:, IN_KR:IN_B]
    wa = jnp.concatenate([w_in[:, IN_CQ:IN_CKV], kr, _rope_swap_cols(kr), jnp.zeros((D_MODEL, 64), BF16),
                          w_in[:, IN_CKV:IN_KR]], axis=1)
    wuq = bf(w["a_wuq"][l]).reshape(A_Q_RANK, A_HEADS, A_QK)
    zeros = lambda n: jnp.zeros((A_Q_RANK, A_HEADS, n), BF16)
    wuq_pad = jnp.concatenate([wuq, zeros(A_PAD - A_QK)], axis=-1)
    wuq_sw = jnp.concatenate([zeros(A_NOPE), _rope_swap_cols(wuq[..., A_NOPE:]), zeros(A_PAD - A_QK)], axis=-1)
    wq2 = jnp.concatenate([wuq_pad.reshape(A_Q_RANK, -1), wuq_sw.reshape(A_Q_RANK, -1)], axis=1)
    wukv = bf(w["a_wukv"][l]).reshape(A_KV_RANK, A_HEADS, A_NOPE + A_V)
    wkv = jnp.concatenate([wukv[..., :A_NOPE].reshape(A_KV_RANK, -1), wukv[..., A_NOPE:].reshape(A_KV_RANK, -1)], axis=1)
    return dict(wa=wa, wb=w_in[:, IN_B:IN_C], wc=w_in[:, IN_C:IN_G], wg=w_in[:, IN_G:IN_END], wq2=wq2, wkv=wkv,
                w_br_a=bf(w["w_br_a"][l]), w_br_b=bf(w["w_br_b"][l]), w_br_c=bf(w["w_br_c"][l]), w_out=bf(w["w_out"][l]),
                x_wq=bf(w["x_wq"][l]), x_wkv=bf(w["x_wkv"][l]), x_wo=bf(w["x_wo"][l]),
                f_w1=bf(w["f_w1"][l]), f_w3=bf(w["f_w3"][l]), f_w2=bf(w["f_w2"][l]))


def _layer_fwd(l, x, mem, w, lw, lower, bias, tabs):
    n = lambda s: f"l{l}_{s}"
    cq_t, sq_t, ck, _ = tabs
    s = dict(x=x)
    s["h0"] = h0 = _rms(x, w["g_mix"][l], name=n("rms_mix"))
    s["za"] = za = _mm(h0, lw["wa"], name=n("in_a"))
    s["zb"] = zb = _mm(h0, lw["wb"], name=n("in_b"))
    s["zc"] = zc = _mm(h0, lw["wc"], out_dtype=BF16, name=n("in_c"))
    s["zg"] = zg = _mm(h0, lw["wg"], out_dtype=BF16, name=n("in_g"))
    s["cqn"] = cqn = _rms(za, w["a_gq"][l], col=0, width=A_Q_RANK, name=n("rms_cq"))
    s["ckvn"] = ckvn = _rms(za, w["a_gkv"][l], col=512, width=A_KV_RANK, name=n("rms_ckv"))
    s["q"], s["qt"] = q, _ = _qrope(cqn, lw["wq2"], cq_t, sq_t, name=n("uq_rope"))
    s["kv"] = kv = _mm(ckvn, lw["wkv"], out_dtype=BF16, name=n("ukv"))
    s["k"], vxt = k, _ = _kprep(kv, za, ck, name=n("kprep"))
    s["ya"], s["lse"] = ya, _ = _flash_fwd(q, k, vxt, name=n("mla"))
    lb_f, lb_b = lower[0, l].reshape(1, -1), lower[1, l].reshape(1, -1)
    s["of"], s["stf"] = of, _ = _hgrn_fwd(zb, lb_f, reverse=False, name=n("hgrn_f"))
    s["ob"], s["stb"] = ob, _ = _hgrn_fwd(zb, lb_b, reverse=True, name=n("hgrn_b"))
    s["yb"] = yb = _hgrn_out(of, ob, zb, w["b_gout"][l], name=n("hgrn_out"))
    s["yc"] = yc = _swa_fwd(zc, bias, w["c_sink"][l], name=n("swa"))
    s["merged"] = merged = _merge_fwd(ya, yb, yc, lw["w_br_a"], lw["w_br_b"], lw["w_br_c"], zg, name=n("merge"))
    s["x1"] = x1 = _mm(merged, lw["w_out"], add=x, name=n("out"))
    s["h1"] = h1 = _rms(x1, w["g_x"][l], name=n("rms_x"))
    s["qx"] = qx = _mm(h1, lw["x_wq"], out_dtype=BF16, name=n("xq"))
    s["memn"] = memn = _rms(mem, w["g_mem"][l], name=n("rms_mem"))
    s["kvm"] = kvm = _mm(memn, lw["x_wkv"], out_dtype=BF16, name=n("xkv"))
    s["ox"] = ox = _cross_fwd(qx, kvm, name=n("cross"))
    s["x2"] = x2 = _mm(ox, lw["x_wo"], add=x1, name=n("xo"))
    s["h2"] = h2 = _rms(x2, w["g_ffn"][l], name=n("rms_ffn"))
    s["a"], s["b"], s["act"] = a, b, act = _ffn_up(h2, lw["f_w1"], lw["f_w3"], name=n("ffn_up"))
    x3 = _mm(act, lw["f_w2"], add=x2, name=n("ffn_down"))
    return x3, s


def _layer_bwd(l, dx3, mem, w, lw, lower, bias, tabs, s):
    n = lambda t: f"l{l}_b_{t}"
    cq_t, sq_t, _, ck_t = tabs
    g = {}
    da, db = _ffn_dact(dx3, lw["f_w2"], s["a"], s["b"], name=n("ffn_dact"))
    g["f_w2"] = _mm(s["act"], dx3, mode="tn", name=n("dw2"))
    dh2 = _mm(db, lw["f_w3"], mode="nt", add=_mm(da, lw["f_w1"], mode="nt", name=n("dh2a")), name=n("dh2b"))
    g["f_w1"] = _mm(s["h2"], da, mode="tn", name=n("dw1"))
    g["f_w3"] = _mm(s["h2"], db, mode="tn", name=n("dw3"))
    dx2, g["g_ffn"] = _rms_bwd(s["x2"], w["g_ffn"][l], dh2, res=dx3, name=n("rms_ffn"))
    dox = _mm(dx2, lw["x_wo"], mode="nt", out_dtype=BF16, name=n("dox"))
    g["x_wo"] = _mm(s["ox"], dx2, mode="tn", name=n("dwo"))
    dqx, dkvm = _cross_bwd(s["qx"], s["kvm"], dox, name=n("cross"))
    g["x_wq"] = _mm(s["h1"], dqx, mode="tn", name=n("dwq"))
    dh1 = _mm(dqx, lw["x_wq"], mode="nt", name=n("dh1"))
    g["x_wkv"] = _mm(s["memn"], dkvm, mode="tn", name=n("dwkv"))
    dmemn = _mm(dkvm, lw["x_wkv"], mode="nt", name=n("dmemn"))
    _, g["g_mem"] = _rms_bwd(mem, w["g_mem"][l], dmemn, name=n("rms_mem"))
    dx1, g["g_x"] = _rms_bwd(s["x1"], w["g_x"][l], dh1, res=dx2, name=n("rms_x"))
    dmerged = _mm(dx1, lw["w_out"], mode="nt", name=n("dmerged"))
    g["w_out"] = _mm(s["merged"], dx1, mode="tn", name=n("dwout"))
    dua, dub, duc, dga, dgb, dgc = _merge_bwd(s["ya"], s["yb"], s["yc"], lw["w_br_a"], lw["w_br_b"], lw["w_br_c"],
                                              s["zg"], dmerged, name=n("merge"))
    dya = _mm(dua, lw["w_br_a"], mode="nt", out_dtype=BF16, name=n("dya"))
    dyb = _mm(dub, lw["w_br_b"], mode="nt", name=n("dyb"))
    dyc = _mm(duc, lw["w_br_c"], mode="nt", out_dtype=BF16, name=n("dyc"))
    g["w_br_a"] = _mm(s["ya"], dua, mode="tn", name=n("dwbra"))
    g["w_br_b"] = _mm(s["yb"], dub, mode="tn", name=n("dwbrb"))
    g["w_br_c"] = _mm(s["yc"], duc, mode="tn", name=n("dwbrc"))
    dzc, dbias, dsink = _swa_bwd(s["zc"], bias, w["c_sink"][l], dyc, name=n("swa"))
    g["c_sink"] = dsink[:, 0]
    g["bias"] = dbias
    lb_f, lb_b = lower[0, l].reshape(1, -1), lower[1, l].reshape(1, -1)
    do_, dgate, dgout = _hgrn_out_bwd(s["of"], s["ob"], s["zb"], w["b_gout"][l], dyb, name=n("hgrn_out"))
    g["b_gout"] = dgout[0]
    dq_f, dzf, dv_f, dlb_f = _hgrn_bwd(s["zb"], lb_f, do_, s["stf"], reverse=False, name=n("hgrn_f"))
    dq_b, dzr, dv_b, dlb_b = _hgrn_bwd(s["zb"], lb_b, do_, s["stb"], reverse=True, name=n("hgrn_b"))
    g["lower"] = jnp.concatenate([dlb_f, dlb_b], axis=0)
    dzb = _dzb_assemble(dq_f, dq_b, dzf, dzr, dv_f, dv_b, dgate, name=n("dzb"))
    delta = _attn_delta(s["ya"], dya, name=n("mla_delta"))
    dq_part, dkt, dvt = _flash_bwd(s["q"], s["qt"], s["k"], s["kv"], dya, dya.T, s["lse"].reshape(A_HEADS, -1, 1), delta, name=n("mla"))
    dq2 = _dq_sum(dq_part, cq_t, sq_t, name=n("mla_dq"))
    dcqn = _mm(dq2, lw["wq2"], mode="nt", name=n("dcqn"))
    dwq2 = _mm(s["cqn"], dq2, mode="tn", name=n("dwq2")).reshape(A_Q_RANK, 2, A_HEADS, A_PAD)
    dknt, dzkrt = _kprep_bwd(dkt, ck_t, name=n("kprep"))
    wkv = lw["wkv"]
    dckvn = _mm(dvt, wkv[:, 512:].T, mode="tn", add=_mm(dknt, wkv[:, :512].T, mode="tn", name=n("dckvn_k")), name=n("dckvn_v"))
    dwkn = _mm(dknt, s["ckvn"], name=n("dwkn")).T
    dwv = _mm(dvt, s["ckvn"], name=n("dwv")).T
    dzcq, dgq = _rms_bwd(s["za"], w["a_gq"][l], dcqn, col=0, width=A_Q_RANK, out_dtype=BF16, name=n("rms_cq"))
    dzckv, dgkv = _rms_bwd(s["za"], w["a_gkv"][l], dckvn, col=512, width=A_KV_RANK, out_dtype=BF16, name=n("rms_ckv"))
    g["a_gq"], g["a_gkv"] = dgq[0], dgkv[0]
    sw = jnp.concatenate([jnp.zeros((A_Q_RANK, A_HEADS, A_NOPE), F32), _rope_unswap_cols(dwq2[:, 1, :, A_NOPE:A_QK])], axis=-1)
    g["a_wuq"] = (dwq2[:, 0, :, :A_QK] + sw).reshape(A_Q_RANK, -1)
    g["a_wukv"] = jnp.concatenate([dwkn.reshape(A_KV_RANK, A_HEADS, A_NOPE), dwv.reshape(A_KV_RANK, A_HEADS, A_V)], axis=-1).reshape(A_KV_RANK, -1)
    wa = lw["wa"]
    pieces = [(dzcq, wa[:, 0:384]), (dzckv, wa[:, 512:768]), (dzb, lw["wb"]), (dzc, lw["wc"]),
              (dga, lw["wg"][:, 0:1024]), (dgb, lw["wg"][:, 1024:2048]), (dgc, lw["wg"][:, 2048:3072])]
    dh0 = _mm(dzkrt, wa[:, 384:512].T, mode="tn", name=n("dh0_kr"))
    dwkr = _mm(dzkrt, s["h0"], name=n("dwin_kr")).T
    dwkr = dwkr[:, 0:A_ROPE] + _rope_unswap_cols(dwkr[:, A_ROPE:2 * A_ROPE])
    dws = []
    for i, (dz, wp) in enumerate(pieces):
        dh0 = _mm(dz, wp, mode="nt", add=dh0, name=n(f"dh0_{i}"))
        dws.append(_mm(s["h0"], dz, mode="tn", name=n(f"dwin_{i}")))
    g["w_in"] = jnp.concatenate([dws[0], dws[1], dwkr] + dws[2:], axis=1)
    dx, g["g_mix"] = _rms_bwd(s["x"], w["g_mix"][l], dh0, res=dx1, name=n("rms_mix"))
    return dx, g


def _local_step(x, mem, target, w):
    T = x.shape[0]
    tabs = _rope_tables(T)
    lower, lower_vjp = jax.vjp(_lower_bounds, w["b_lb"])
    buckets = _swa_buckets()
    onehot = (buckets.reshape(-1)[:, None] == jnp.arange(REL_BUCKETS)[None, :]).astype(F32)
    bias = jnp.dot(w["rel_bias"].astype(F32).T, onehot.T, precision=lax.Precision.HIGHEST).reshape(C_HEADS, C_BLOCK, C_SPAN)
    lws, saved = [], []
    for l in range(DEPTH):
        lws.append(_layer_weights(w, l))
        x, s = _layer_fwd(l, x, mem, w, lws[l], lower, bias, tabs)
        saved.append(s)
    loss, dx, dg_final = _loss_head(x, w["g_final"], target, name="loss_head")
    layer_grads = [None] * DEPTH
    for l in reversed(range(DEPTH)):
        dx, layer_grads[l] = _layer_bwd(l, dx, mem, w, lws[l], lower, bias, tabs, saved[l])
        saved[l] = None
    grads = {}
    for name in WEIGHT_NAMES:
        if name in layer_grads[0]:
            grads[name] = jnp.stack([layer_grads[l][name].reshape(w[name].shape[1:]) for l in range(DEPTH)])
    grads["g_final"] = dg_final[0]
    dlower = jnp.stack([layer_grads[l]["lower"] for l in range(DEPTH)], axis=1)
    grads["b_lb"] = lower_vjp(dlower)[0]
    dbias = layer_grads[0]["bias"] + layer_grads[1]["bias"]
    grads["rel_bias"] = jnp.dot(onehot.T, dbias.reshape(C_HEADS, -1).T, precision=lax.Precision.HIGHEST)
    return loss, dx, grads


N_CHIPS = 4
PACK_COLS = 1024
PACK_ALIGN = 32 * PACK_COLS
SHARDED = (("w_in", 2), ("a_wuq", 2), ("a_wukv", 2), ("b_lb", 2), ("w_br_a", 2), ("w_br_b", 2), ("w_br_c", 2), ("w_out", 1),
           ("x_wq", 1), ("x_wkv", 2), ("x_wo", 1), ("f_w1", 2), ("f_w3", 2), ("f_w2", 1))
REPLICATED = ("g_mix", "a_gq", "a_gkv", "b_gout", "c_sink", "rel_bias", "g_x", "g_mem", "g_ffn", "g_final")
MESH_IDS = pl.DeviceIdType.MESH
ANY_SPEC = pl.BlockSpec(memory_space=pl.ANY)


def _pack_pieces(arrs, cols, align):
    pieces = [a.reshape(-1, cols) for a in arrs]
    pad = (-sum(p.size for p in pieces)) % align
    return pieces + ([jnp.zeros((pad // cols, cols), pieces[0].dtype)] if pad else [])


def _pack(arrs, cols, align):
    return jnp.concatenate(_pack_pieces(arrs, cols, align), axis=0)


def _pack_small(arrs):
    flat = jnp.concatenate([a.reshape(-1) for a in arrs])
    return jnp.pad(flat, (0, (-flat.shape[0]) % (8 * LANES))).reshape(-1, LANES)


def _unpack(buf, shapes):
    cols = buf.shape[-1]
    buf = buf.reshape(-1, cols)
    by_rows = all(math.prod(shp) % cols == 0 for shp in shapes)
    flat = None if by_rows else buf.reshape(-1)
    out, start = [], 0
    for shp in shapes:
        size = math.prod(shp)
        piece = buf[start // cols:(start + size) // cols] if by_rows else flat[start:start + size]
        out.append(piece.reshape(shp))
        start += size
    return out


def _chip_peers():
    x, y, c = lax.axis_index("x"), lax.axis_index("y"), lax.axis_index("c")
    return x, y, c, [(1 - x, y), (x, 1 - y), (1 - x, 1 - y)]


def _chip_gather(src, chip, *, name):
    _, R, C = src.shape

    def body(src_ref, out_ref, send_sems, recv_sems, pass_send_sems, pass_recv_sems):
        x, y, c, chips = _chip_peers()
        me = 2 * x + y
        sibling = (x, y, 1 - c)

        def over_ici(j, slot):
            px, py = chips[j]
            return pltpu.make_async_remote_copy(src_ref=src_ref.at[c], dst_ref=out_ref.at[slot, c], send_sem=send_sems.at[j],
                                                recv_sem=recv_sems.at[j], device_id=(px, py, c), device_id_type=MESH_IDS)

        def pass_on(j, half):
            px, py = chips[j]
            piece = out_ref.at[2 * px + py, half]
            return pltpu.make_async_remote_copy(src_ref=piece, dst_ref=piece, send_sem=pass_send_sems.at[j],
                                                recv_sem=pass_recv_sems.at[j], device_id=sibling, device_id_type=MESH_IDS)

        sends = [over_ici(j, me) for j in range(3)]
        for cp in sends:
            cp.start()
        passed = []
        for j, (px, py) in enumerate(chips):
            over_ici(j, 2 * px + py).wait_recv()
            passed.append(pass_on(j, c))
            passed[j].start()
        for j in range(3):
            pass_on(j, 1 - c).wait_recv()
        for cp in sends + passed:
            cp.wait_send()

    gathered = _pallas(
        body, name=name, in_specs=[ANY_SPEC], out_specs=ANY_SPEC, out_shape=jax.ShapeDtypeStruct((N_CHIPS, 2, R, C), src.dtype),
        scratch_shapes=[pltpu.SemaphoreType.DMA((3,))] * 4,
        compiler_params=pltpu.CompilerParams(has_side_effects=True))(src)

    tr = _tile(R, 512, 16)

    def place(chip_ref, gathered_ref, own_ref, out_ref):
        out_ref[0, 0] = own_ref[0]

    grid_spec = pltpu.PrefetchScalarGridSpec(
        num_scalar_prefetch=1, grid=(2, R // tr),
        in_specs=[ANY_SPEC, pl.BlockSpec((1, tr, C), lambda h, i, chip_ref: (h, i, 0))],
        out_specs=pl.BlockSpec((1, 1, tr, C), lambda h, i, chip_ref: (chip_ref[0], h, i, 0)))
    return _pallas(place, name=name + "_own", grid_spec=grid_spec, out_shape=jax.ShapeDtypeStruct(gathered.shape, gathered.dtype),
                   input_output_aliases={1: 0}, compiler_params=_params("arbitrary", "arbitrary"))(chip, gathered, src)


def _chip_scatter(src, *, name):
    _, R, C = src.shape

    def body(src_ref, out_ref, send_sems, recv_sems):
        x, y, c, chips = _chip_peers()
        me = 2 * x + y

        def copy(j, seg):
            px, py = chips[j]
            return pltpu.make_async_remote_copy(src_ref=src_ref.at[seg], dst_ref=out_ref.at[j], send_sem=send_sems.at[j],
                                                recv_sem=recv_sems.at[j], device_id=(px, py, c), device_id_type=MESH_IDS)

        sends = [copy(j, 2 * px + py) for j, (px, py) in enumerate(chips)]
        for cp in sends:
            cp.start()
        for j in range(3):
            copy(j, me).wait_recv()
        for cp in sends:
            cp.wait_send()

    return _pallas(
        body, name=name, in_specs=[ANY_SPEC], out_specs=ANY_SPEC, out_shape=jax.ShapeDtypeStruct((3, R, C), src.dtype),
        scratch_shapes=[pltpu.SemaphoreType.DMA((3,)), pltpu.SemaphoreType.DMA((3,))],
        compiler_params=pltpu.CompilerParams(has_side_effects=True))(src)


PAIR_CHUNKS = 4


def _pair_swap(src, *, halves, name):
    R, C = src.shape[-2:]
    n = N_CHIPS if halves else 1
    rc = R // PAIR_CHUNKS
    assert rc * PAIR_CHUNKS == R and rc % 16 == 0, R

    def body(src_ref, out_ref, send_sems, recv_sems):
        x, y, c = lax.axis_index("x"), lax.axis_index("y"), lax.axis_index("c")
        copies = []
        for k in range(n):
            for r in range(PAIR_CHUNKS):
                rows = pl.ds(r * rc, rc)
                s = src_ref.at[k, 1 - c, rows] if halves else src_ref.at[rows]
                d = out_ref.at[k, rows] if halves else out_ref.at[rows]
                i = k * PAIR_CHUNKS + r
                copies.append(pltpu.make_async_remote_copy(src_ref=s, dst_ref=d, send_sem=send_sems.at[i], recv_sem=recv_sems.at[i],
                                                           device_id=(x, y, 1 - c), device_id_type=MESH_IDS))
        for cp in copies:
            cp.start()
        for cp in copies:
            cp.wait_recv()
        for cp in copies:
            cp.wait_send()

    shape = (N_CHIPS, R, C) if halves else (R, C)
    return _pallas(
        body, name=name, in_specs=[ANY_SPEC], out_specs=ANY_SPEC, out_shape=jax.ShapeDtypeStruct(shape, src.dtype),
        scratch_shapes=[pltpu.SemaphoreType.DMA((n * PAIR_CHUNKS,)), pltpu.SemaphoreType.DMA((n * PAIR_CHUNKS,))],
        compiler_params=pltpu.CompilerParams(has_side_effects=True))(src)


def _pair_add(g4, got, c, *, name):
    _, _, R, C = g4.shape
    tr = _tile(R, 512, 16)

    def body(c_ref, mine_ref, got_ref, o_ref, ob_ref):
        s = mine_ref[0, 0] + got_ref[0].astype(F32)
        o_ref[0] = s
        ob_ref[0] = s.astype(BF16)

    blk = pl.BlockSpec((1, tr, C), lambda k, i, c_ref: (k, i, 0))
    grid_spec = pltpu.PrefetchScalarGridSpec(
        num_scalar_prefetch=1, grid=(N_CHIPS, R // tr),
        in_specs=[pl.BlockSpec((1, 1, tr, C), lambda k, i, c_ref: (k, c_ref[0], i, 0)), blk], out_specs=[blk, blk])
    return _pallas(body, name=name, grid_spec=grid_spec,
                   out_shape=[jax.ShapeDtypeStruct((N_CHIPS, R, C), F32), jax.ShapeDtypeStruct((N_CHIPS, R, C), BF16)],
                   compiler_params=_params("parallel", "parallel"))(c, g4, got)


def _chip_sum(pair_sum, landed, me, *, name):
    _, R, C = pair_sum.shape
    tr = _tile(R, 512, 16)

    def body(me_ref, own_ref, landed_ref, o_ref):
        acc = own_ref[0]
        for j in range(3):
            acc = acc + landed_ref[j].astype(F32)
        o_ref[...] = acc

    grid_spec = pltpu.PrefetchScalarGridSpec(
        num_scalar_prefetch=1, grid=(R // tr,),
        in_specs=[pl.BlockSpec((1, tr, C), lambda i, me_ref: (me_ref[0], i, 0)), pl.BlockSpec((3, tr, C), lambda i, me_ref: (0, i, 0))],
        out_specs=pl.BlockSpec((tr, C), lambda i, me_ref: (i, 0)))
    return _pallas(body, name=name, grid_spec=grid_spec, out_shape=jax.ShapeDtypeStruct((R, C), F32),
                   compiler_params=_params("parallel"))(me, pair_sum, landed)


def _join_halves(mine, got, c, *, name):
    R, C = mine.shape
    tr = _tile(R, 512, 16)

    def body(c_ref, mine_ref, got_ref, o_ref):
        use_mine = pl.program_id(0) == c_ref[0]
        o_ref[0] = jnp.where(use_mine, mine_ref[...], got_ref[...])

    blk = pl.BlockSpec((tr, C), lambda h, i, c_ref: (i, 0))
    grid_spec = pltpu.PrefetchScalarGridSpec(num_scalar_prefetch=1, grid=(2, R // tr), in_specs=[blk, blk],
                                             out_specs=pl.BlockSpec((1, tr, C), lambda h, i, c_ref: (h, i, 0)))
    return _pallas(body, name=name, grid_spec=grid_spec, out_shape=jax.ShapeDtypeStruct((2, R, C), mine.dtype),
                   compiler_params=_params("parallel", "parallel"))(c, mine, got).reshape(2 * R, C)


def _gather8(s, *, name):
    R, C = s.shape

    def body(s_ref, out_ref, send_sems, recv_sems):
        x, y, c = lax.axis_index("x"), lax.axis_index("y"), lax.axis_index("c")
        me = 4 * x + 2 * y + c
        flips = [(dx, dy, dc) for dx in (0, 1) for dy in (0, 1) for dc in (0, 1)][1:]
        out_ref[me] = s_ref[...]

        def copy(j, slot):
            dx, dy, dc = flips[j]
            return pltpu.make_async_remote_copy(src_ref=s_ref, dst_ref=out_ref.at[slot], send_sem=send_sems.at[j],
                                                recv_sem=recv_sems.at[j], device_id=(x ^ dx, y ^ dy, c ^ dc), device_id_type=MESH_IDS)

        sends = [copy(j, me) for j in range(7)]
        for cp in sends:
            cp.start()
        for j, (dx, dy, dc) in enumerate(flips):
            copy(j, 4 * (x ^ dx) + 2 * (y ^ dy) + (c ^ dc)).wait_recv()
        for cp in sends:
            cp.wait_send()

    vmem = pl.BlockSpec(memory_space=pltpu.VMEM)
    return _pallas(
        body, name=name, in_specs=[vmem], out_specs=vmem, out_shape=jax.ShapeDtypeStruct((8, R, C), s.dtype),
        scratch_shapes=[pltpu.SemaphoreType.DMA((7,)), pltpu.SemaphoreType.DMA((7,))],
        compiler_params=pltpu.CompilerParams(has_side_effects=True))(s)


def _sum_slots(a, *, name):
    n, R, C = a.shape
    tr = _tile(R, 512, 8)

    def body(a_ref, o_ref):
        acc = a_ref[0]
        for k in range(1, n):
            acc = acc + a_ref[k]
        o_ref[...] = acc

    return _pallas(body, name=name, grid=(R // tr,), in_specs=[pl.BlockSpec((n, tr, C), lambda i: (0, i, 0))],
                   out_specs=pl.BlockSpec((tr, C), lambda i: (i, 0)), out_shape=jax.ShapeDtypeStruct((R, C), a.dtype),
                   compiler_params=_params("parallel"))(a)


def _adamw(w, g, m, v, *, name):
    R, C = w.shape
    tr = _tile(R, max(8, (1 << 18) // C // 8 * 8), 8)
    c1 = 1.0 / (1.0 - ADAM_B1 ** ADAM_STEP)
    c2 = 1.0 / (1.0 - ADAM_B2 ** ADAM_STEP)

    def body(w_ref, g_ref, m_ref, v_ref, d_ref, nm_ref, nv_ref):
        gv = g_ref[...]
        nm = ADAM_B1 * m_ref[...] + (1.0 - ADAM_B1) * gv
        nv = ADAM_B2 * v_ref[...] + (1.0 - ADAM_B2) * (gv * gv)
        d_ref[...] = -ADAM_LR * ((nm * c1) / (jnp.sqrt(nv * c2) + ADAM_EPS) + ADAM_WD * w_ref[...])
        nm_ref[...] = nm
        nv_ref[...] = nv

    blk = pl.BlockSpec((tr, C), lambda i: (i, 0))
    shape = jax.ShapeDtypeStruct((R, C), F32)
    return _pallas(body, name=name, grid=(R // tr,), in_specs=[blk] * 4, out_specs=[blk] * 3, out_shape=[shape] * 3,
                   compiler_params=_params("parallel"))(w, g, m, v)


def kernel(x, mem, w_in, g_mix, a_gq, a_gkv, a_wuq, a_wukv, b_lb, b_gout, c_sink, rel_bias, w_br_a, w_br_b, w_br_c, w_out, g_x, g_mem, x_wq, x_wkv, x_wo, g_ffn, f_w1, f_w3, f_w2, g_final, loss_target, m_w_in, m_g_mix, m_a_gq, m_a_gkv, m_a_wuq, m_a_wukv, m_b_lb, m_b_gout, m_c_sink, m_rel_bias, m_w_br_a, m_w_br_b, m_w_br_c, m_w_out, m_g_x, m_g_mem, m_x_wq, m_x_wkv, m_x_wo, m_g_ffn, m_f_w1, m_f_w3, m_f_w2, m_g_final, v_w_in, v_g_mix, v_a_gq, v_a_gkv, v_a_wuq, v_a_wukv, v_b_lb, v_b_gout, v_c_sink, v_rel_bias, v_w_br_a, v_w_br_b, v_w_br_c, v_w_out, v_g_x, v_g_mem, v_x_wq, v_x_wkv, v_x_wo, v_g_ffn, v_f_w1, v_f_w3, v_f_w2, v_g_final):
    ws = dict(zip(WEIGHT_NAMES, (w_in, g_mix, a_gq, a_gkv, a_wuq, a_wukv, b_lb, b_gout, c_sink, rel_bias, w_br_a, w_br_b, w_br_c,
                                 w_out, g_x, g_mem, x_wq, x_wkv, x_wo, g_ffn, f_w1, f_w3, f_w2, g_final)))
    ms = dict(zip(WEIGHT_NAMES, (m_w_in, m_g_mix, m_a_gq, m_a_gkv, m_a_wuq, m_a_wukv, m_b_lb, m_b_gout, m_c_sink, m_rel_bias,
                                 m_w_br_a, m_w_br_b, m_w_br_c, m_w_out, m_g_x, m_g_mem, m_x_wq, m_x_wkv, m_x_wo, m_g_ffn,
                                 m_f_w1, m_f_w3, m_f_w2, m_g_final)))
    vs = dict(zip(WEIGHT_NAMES, (v_w_in, v_g_mix, v_a_gq, v_a_gkv, v_a_wuq, v_a_wukv, v_b_lb, v_b_gout, v_c_sink, v_rel_bias,
                                 v_w_br_a, v_w_br_b, v_w_br_c, v_w_out, v_g_x, v_g_mem, v_x_wq, v_x_wkv, v_x_wo, v_g_ffn,
                                 v_f_w1, v_f_w3, v_f_w2, v_g_final)))
    sharded = [n for n, _ in SHARDED]
    axis_of = dict(SHARDED)

    def wire(n):
        return lax.bitcast_convert_type(ws[n], BF16) if n == "b_lb" else ws[n].astype(BF16)

    core = lax.axis_index("c").astype(jnp.int32).reshape(1)
    chip = (2 * lax.axis_index("x") + lax.axis_index("y")).astype(jnp.int32).reshape(1)
    wire_shapes = [wire(n).shape for n in sharded]
    packed = _pack([wire(n) for n in sharded], PACK_COLS, PACK_ALIGN)
    gathered = _chip_gather(packed.reshape(2, packed.shape[0] // 2, PACK_COLS), chip, name="gather_weights")
    per_chip = [_unpack(gathered[k], wire_shapes) for k in range(N_CHIPS)]
    full = dict(ws)
    for i, n in enumerate(sharded):
        parts = [per_chip[k][i] for k in range(N_CHIPS)]
        if n == "b_lb":
            parts = [lax.bitcast_convert_type(p, F32) for p in parts]
        full[n] = jnp.concatenate(parts, axis=axis_of[n])

    loss, grad_x, grads = _local_step(x[0], mem[0], loss_target[0], full)

    pieces = []
    for k in range(N_CHIPS):
        pieces += _pack_pieces([jnp.split(grads[n], N_CHIPS, axis=axis_of[n])[k] for n in sharded], PACK_COLS, PACK_ALIGN)
    g4 = jnp.concatenate(pieces, axis=0).reshape(N_CHIPS, 2, -1, PACK_COLS)
    got = _pair_swap(g4, halves=True, name="reduce_pair_swap")
    pair_sum, pair_sum_wire = _pair_add(g4, got, core, name="reduce_pair_add")
    landed = _chip_scatter(pair_sum_wire, name="reduce_chip_scatter")
    mine = _chip_sum(pair_sum, landed, chip, name="reduce_chip_sum")
    g_shard = _join_halves(mine, _pair_swap(mine, halves=False, name="reduce_pair_join"), core, name="reduce_join_halves")

    small = _pack_small([grads[n] for n in REPLICATED] + [loss[0, 0:1]])
    small_sum = _sum_slots(_gather8(small, name="gather_small"), name="sum_small")
    small_grads = _unpack(small_sum, [ws[n].shape for n in REPLICATED] + [(1,)])
    loss_total = small_grads.pop()[0]

    shard_shapes = [ws[n].shape for n in sharded]
    out = {}
    for n, gr in zip(sharded, _unpack(g_shard, shard_shapes)):
        flat2 = lambda a: a.reshape(-1, a.shape[-1])
        d, nm, nv = _adamw(flat2(ws[n]), flat2(gr), flat2(ms[n]), flat2(vs[n]), name="adamw_" + n)
        out[n] = (gr, d.reshape(gr.shape), nm.reshape(gr.shape), nv.reshape(gr.shape))
    pk_s = lambda d: _pack_small([d[n] for n in REPLICATED])
    rep_shapes = [ws[n].shape for n in REPLICATED]
    gs_flat = _pack_small(small_grads)
    ds_flat, ms_flat, vs_flat = _adamw(pk_s(ws), gs_flat, pk_s(ms), pk_s(vs), name="adamw_replicated")
    for n, gr, d, nm, nv in zip(REPLICATED, small_grads, _unpack(ds_flat, rep_shapes), _unpack(ms_flat, rep_shapes),
                                _unpack(vs_flat, rep_shapes)):
        out[n] = (gr, d, nm, nv)
    return (loss_total, grad_x[None], *[out[n][0] for n in WEIGHT_NAMES], *[out[n][1] for n in WEIGHT_NAMES],
            *[out[n][2] for n in WEIGHT_NAMES], *[out[n][3] for n in WEIGHT_NAMES])
```

```python
import math

import jax
import jax.numpy as jnp
from jax import lax
from jax.experimental import pallas as pl
from jax.experimental.pallas import tpu as pltpu

F32 = jnp.float32
BF16 = jnp.bfloat16

D_MODEL = 1024
DEPTH = 2
EPS = 1e-6
MASK_VALUE = -1e30
TINY = 1e-30
A_HEADS, A_NOPE, A_ROPE, A_V = 8, 64, 32, 64
A_QK = A_NOPE + A_ROPE
A_Q_RANK, A_KV_RANK = 384, 256
ROPE_THETA = 10000.0
B_HEADS, B_DK, B_DV, B_CHUNK = 8, 128, 64, 16
C_HEADS, C_KV_HEADS, C_DH, C_WINDOW, C_BLOCK = 8, 2, 64, 128, 128
REL_BUCKETS, REL_MAX_DIST = 32, 128
X_HEADS, X_DH = 4, 256
D_FF = 2816
ADAM_LR, ADAM_B1, ADAM_B2, ADAM_EPS, ADAM_WD, ADAM_STEP = 0.001, 0.9, 0.999, 1e-08, 0.01, 10

LANES = 128
VMEM_LIMIT = 56 * 1024 * 1024
VMEM_LIMIT_WIDE = 62 * 1024 * 1024


def _pallas(body, **kw):
    return pl.pallas_call(body, **kw)


def _params(*sem):
    return pltpu.CompilerParams(dimension_semantics=sem, vmem_limit_bytes=VMEM_LIMIT)


def _tile(n, pref, unit=LANES):
    if n <= pref:
        return n
    t = (pref // unit) * unit
    while t > unit and n % t:
        t -= unit
    assert n % t == 0, (n, pref, unit)
    return t


def _dot(a, b, dims):
    return lax.dot_general(a, b, (dims, ((), ())), preferred_element_type=F32)


_NN = ((1,), (0,))
_NT = ((1,), (1,))
_TN = ((0,), (0,))


def _mm_tiles(M, N, K, mode):
    half_ff = D_FF // 2
    tm = _tile(M, half_ff if M % half_ff == 0 else (1024 if mode != "tn" and M >= 2048 else 512), LANES if mode == "tn" else 16)
    tn = _tile(N, half_ff if N % half_ff == 0 else 1024, 256 if N % 256 == 0 and N % half_ff else LANES)
    tk = K if K <= D_FF else _tile(K, 1024)
    return tm, tn, tk


def _mm(a, b, *, mode="nn", add=None, out_dtype=F32, tiles=None, name):
    if mode == "nn":
        (M, K), (K2, N) = a.shape, b.shape
    elif mode == "nt":
        (M, K), (N, K2) = a.shape, b.shape
    else:
        (K, M), (K2, N) = a.shape, b.shape
    assert K == K2, (a.shape, b.shape, mode)
    tm, tn, tk = tiles or _mm_tiles(M, N, K, mode)
    nk = K // tk
    dims = {"nn": _NN, "nt": _NT, "tn": _TN}[mode]
    a_spec = pl.BlockSpec((tk, tm), lambda i, j, k: (k, i)) if mode == "tn" else pl.BlockSpec((tm, tk), lambda i, j, k: (i, k))
    b_spec = pl.BlockSpec((tn, tk), lambda i, j, k: (j, k)) if mode == "nt" else pl.BlockSpec((tk, tn), lambda i, j, k: (k, j))
    o_spec = pl.BlockSpec((tm, tn), lambda i, j, k: (i, j))
    has_add = add is not None

    def body(*refs):
        if has_add:
            a_ref, b_ref, add_ref, o_ref, acc_ref = refs
        else:
            a_ref, b_ref, o_ref, acc_ref = refs
        k = pl.program_id(2)
        part = _dot(a_ref[...].astype(BF16), b_ref[...].astype(BF16), dims)

        @pl.when(k == 0)
        def _():
            acc_ref[...] = part

        @pl.when(k > 0)
        def _():
            acc_ref[...] += part

        @pl.when(k == nk - 1)
        def _():
            r = acc_ref[...]
            if has_add:
                r = r + add_ref[...].astype(F32)
            o_ref[...] = r.astype(out_dtype)

    ins = [a, b] + ([add] if has_add else [])
    in_specs = [a_spec, b_spec] + ([o_spec] if has_add else [])
    return _pallas(
        body, name=name, grid=(M // tm, N // tn, nk), in_specs=in_specs, out_specs=o_spec,
        out_shape=jax.ShapeDtypeStruct((M, N), out_dtype), scratch_shapes=[pltpu.VMEM((tm, tn), F32)],
        compiler_params=_params("parallel", "parallel", "arbitrary"),
    )(*ins)


def _rms(x, g, *, col=0, width=None, out_dtype=BF16, name):
    T = x.shape[0]
    width = x.shape[1] if width is None else width
    assert col % width == 0
    tm = _tile(T, 512, 8)
    cb = col // width

    def body(x_ref, g_ref, o_ref):
        xv = x_ref[...].astype(F32)
        r = lax.rsqrt(jnp.mean(xv * xv, axis=-1, keepdims=True) + EPS)
        o_ref[...] = (xv * r * g_ref[...]).astype(out_dtype)

    return _pallas(
        body, name=name, grid=(T // tm,),
        in_specs=[pl.BlockSpec((tm, width), lambda i: (i, cb)), pl.BlockSpec((1, width), lambda i: (0, 0))],
        out_specs=pl.BlockSpec((tm, width), lambda i: (i, 0)),
        out_shape=jax.ShapeDtypeStruct((T, width), out_dtype), compiler_params=_params("parallel"),
    )(x, g.reshape(1, width))


def _rms_bwd(x, g, dy, *, res=None, col=0, width=None, out_dtype=F32, name):
    T = x.shape[0]
    width = x.shape[1] if width is None else width
    assert col % width == 0
    tm = _tile(T, 512, 8)
    cb = col // width
    has_res = res is not None

    def body(*refs):
        if has_res:
            x_ref, g_ref, dy_ref, res_ref, dx_ref, dg_ref = refs
        else:
            x_ref, g_ref, dy_ref, dx_ref, dg_ref = refs
        xv = x_ref[...].astype(F32)
        r = lax.rsqrt(jnp.mean(xv * xv, axis=-1, keepdims=True) + EPS)
        xh = xv * r
        dyv = dy_ref[...].astype(F32)
        dxh = dyv * g_ref[...]
        dx = r * (dxh - xh * jnp.mean(dxh * xh, axis=-1, keepdims=True))
        if has_res:
            dx = dx + res_ref[...].astype(F32)
        dx_ref[...] = dx.astype(out_dtype)
        part = jnp.sum(dyv * xh, axis=0, keepdims=True)

        @pl.when(pl.program_id(0) == 0)
        def _():
            dg_ref[...] = part

        @pl.when(pl.program_id(0) > 0)
        def _():
            dg_ref[...] += part

    row = pl.BlockSpec((tm, width), lambda i: (i, 0))
    ins = [x, g.reshape(1, width), dy] + ([res] if has_res else [])
    in_specs = [pl.BlockSpec((tm, width), lambda i: (i, cb)), pl.BlockSpec((1, width), lambda i: (0, 0)), row] + ([row] if has_res else [])
    return _pallas(
        body, name=name, grid=(T // tm,), in_specs=in_specs,
        out_specs=[row, pl.BlockSpec((1, width), lambda i: (0, 0))],
        out_shape=[jax.ShapeDtypeStruct((T, width), out_dtype), jax.ShapeDtypeStruct((1, width), F32)],
        compiler_params=_params("arbitrary"),
    )(*ins)


def _rope_tables(T):
    half = A_ROPE // 2
    inv = ROPE_THETA ** (-jnp.arange(half, dtype=F32) / half)
    ang = jnp.arange(T, dtype=jnp.int32).astype(F32)[:, None] * inv[None, :]
    c32 = jnp.concatenate([jnp.cos(ang), jnp.cos(ang)], axis=-1)
    s32 = jnp.concatenate([jnp.sin(ang), jnp.sin(ang)], axis=-1)
    pad = A_PAD - A_QK
    cq = jnp.concatenate([jnp.ones((T, A_NOPE), F32), c32, jnp.ones((T, pad), F32)], axis=-1)
    sq = jnp.concatenate([jnp.zeros((T, A_NOPE), F32), s32, jnp.zeros((T, pad), F32)], axis=-1)
    ck = jnp.concatenate([c32, s32, jnp.zeros((T, LANES - 2 * A_ROPE), F32)], axis=-1)
    ck_t = jnp.concatenate([c32, s32], axis=-1).T
    return cq, sq, ck, ck_t


def _rope_swap_cols(w):
    half = A_ROPE // 2
    return jnp.concatenate([-w[..., half:], w[..., :half]], axis=-1)


def _rope_unswap_cols(g):
    half = A_ROPE // 2
    return jnp.concatenate([g[..., half:], -g[..., :half]], axis=-1)


A_PAD = LANES
A_W = A_HEADS * A_PAD
LOG2E = 1.4426950408889634
LN2 = 0.6931471805599453
Q_SCALE = A_QK ** -0.5 * LOG2E


def _qrope(cqn, wq2, cq, sq, *, name):
    T, R = cqn.shape
    W = A_W
    tm = _tile(T, 512)

    def body(x_ref, w_ref, c_ref, s_ref, o_ref, ot_ref):
        q2 = _dot(x_ref[...], w_ref[...], _NN)
        c = jnp.concatenate([c_ref[...]] * A_HEADS, axis=1)
        s = jnp.concatenate([s_ref[...]] * A_HEADS, axis=1)
        q = (q2[:, 0:W] * c + q2[:, W:2 * W] * s) * Q_SCALE
        o_ref[...] = q.astype(BF16)
        ot_ref[...] = q.T.astype(BF16)

    row = pl.BlockSpec((tm, W), lambda i: (i, 0))
    tab = pl.BlockSpec((tm, A_PAD), lambda i: (i, 0))
    return _pallas(body, name=name, grid=(T // tm,),
                   in_specs=[pl.BlockSpec((tm, R), lambda i: (i, 0)), pl.BlockSpec((R, 2 * W), lambda i: (0, 0)), tab, tab],
                   out_specs=[row, pl.BlockSpec((W, tm), lambda i: (0, i))],
                   out_shape=[jax.ShapeDtypeStruct((T, W), BF16), jax.ShapeDtypeStruct((W, T), BF16)],
                   compiler_params=_params("parallel"))(cqn, wq2, cq, sq)


def _kprep(kv, za, ck, *, name):
    T = kv.shape[0]
    tm = _tile(T, 512)

    def body(kv_ref, kr_ref, ck_ref, k_ref, vxt_ref):
        t = kr_ref[...] * ck_ref[...]
        krope = (t[:, 0:A_ROPE] + t[:, A_ROPE:2 * A_ROPE]).astype(BF16)
        one = (lax.broadcasted_iota(jnp.int32, (A_PAD - A_V, tm), 0) == 0).astype(BF16)
        for h in range(A_HEADS):
            k_ref[:, A_PAD * h:A_PAD * h + A_NOPE] = kv_ref[:, A_NOPE * h:A_NOPE * (h + 1)]
            k_ref[:, A_PAD * h + A_NOPE:A_PAD * h + A_QK] = krope
            k_ref[:, A_PAD * h + A_QK:A_PAD * (h + 1)] = jnp.zeros((tm, A_PAD - A_QK), BF16)
            vxt_ref[A_PAD * h + A_V:A_PAD * (h + 1), :] = one
        vt = kv_ref[:, 512:1024].astype(F32).T.astype(BF16)
        for h in range(A_HEADS):
            vxt_ref[A_PAD * h:A_PAD * h + A_V, :] = vt[A_V * h:A_V * (h + 1), :]

    wide = pl.BlockSpec((tm, A_W), lambda i: (i, 0))
    return _pallas(
        body, name=name, grid=(T // tm,),
        in_specs=[wide, pl.BlockSpec((tm, LANES), lambda i: (i, 3)), pl.BlockSpec((tm, LANES), lambda i: (i, 0))],
        out_specs=[wide, pl.BlockSpec((A_W, tm), lambda i: (0, i))],
        out_shape=[jax.ShapeDtypeStruct((T, A_W), BF16), jax.ShapeDtypeStruct((A_W, T), BF16)],
        compiler_params=_params("parallel"))(kv, za, ck)


def _kprep_bwd(dkt, ck_t, *, name):
    T = dkt.shape[1]
    tc = _tile(T, 512)

    def body(dk_ref, ck_ref, dn_ref, dr_ref):
        acc = jnp.zeros((A_ROPE, tc), F32)
        for h in range(A_HEADS):
            dn_ref[A_NOPE * h:A_NOPE * (h + 1), :] = dk_ref[A_PAD * h:A_PAD * h + A_NOPE, :].astype(BF16)
            acc = acc + dk_ref[A_PAD * h + A_NOPE:A_PAD * h + A_QK, :]
        dr_ref[0:A_ROPE, :] = (acc * ck_ref[0:A_ROPE, :]).astype(BF16)
        dr_ref[A_ROPE:2 * A_ROPE, :] = (acc * ck_ref[A_ROPE:2 * A_ROPE, :]).astype(BF16)
        dr_ref[2 * A_ROPE:LANES, :] = jnp.zeros((LANES - 2 * A_ROPE, tc), BF16)

    col = lambda r: pl.BlockSpec((r, tc), lambda i: (0, i))
    return _pallas(
        body, name=name, grid=(T // tc,), in_specs=[col(A_W), col(2 * A_ROPE)], out_specs=[col(512), col(LANES)],
        out_shape=[jax.ShapeDtypeStruct((512, T), BF16), jax.ShapeDtypeStruct((LANES, T), BF16)],
        compiler_params=_params("parallel"))(dkt, ck_t)


def _flash_fwd(qs, k, vxt, *, name):
    T = qs.shape[0]
    tq, tk = _tile(T, 512), _tile(T, 4096)
    nk = T // tk
    H, P, DV = A_HEADS, A_PAD, A_V

    def body(q_ref, k_ref, v_ref, o_ref, lse_ref, m_sc, acc_sc):
        j = pl.program_id(1)

        @pl.when(j == 0)
        def _():
            m_sc[...] = jnp.full(m_sc.shape, -jnp.inf, F32)
            acc_sc[...] = jnp.zeros(acc_sc.shape, F32)

        def scores(h):
            return _dot(k_ref[:, P * h:P * (h + 1)], q_ref[:, P * h:P * (h + 1)], _NT)

        st_next = scores(0)
        for h in range(H):
            st = st_next
            if h + 1 < H:
                st_next = scores(h + 1)
            m_prev = m_sc[h]
            m_new = jnp.maximum(m_prev, jnp.max(st, axis=0, keepdims=True))
            pt = jnp.exp2(st - m_new).astype(BF16)
            acc_sc[h] = jnp.exp2(m_prev - m_new) * acc_sc[h] + _dot(v_ref[P * h:P * (h + 1), :], pt, _NN)
            m_sc[h] = m_new

        @pl.when(j == nk - 1)
        def _():
            for h in range(H):
                acc = acc_sc[h]
                l = acc[DV:DV + 1, :]
                o_ref[:, DV * h:DV * (h + 1)] = (acc[0:DV, :] / l).T
                lse_ref[h] = m_sc[h] + jnp.log2(l)

    return _pallas(
        body, name=name, grid=(T // tq, nk),
        in_specs=[pl.BlockSpec((tq, A_W), lambda i, j: (i, 0)), pl.BlockSpec((tk, A_W), lambda i, j: (j, 0)),
                  pl.BlockSpec((A_W, tk), lambda i, j: (0, j))],
        out_specs=[pl.BlockSpec((tq, H * DV), lambda i, j: (i, 0)), pl.BlockSpec((H, 1, tq), lambda i, j: (0, 0, i))],
        out_shape=[jax.ShapeDtypeStruct((T, H * DV), F32), jax.ShapeDtypeStruct((H, 1, T), F32)],
        scratch_shapes=[pltpu.VMEM((H, 1, tq), F32), pltpu.VMEM((H, P, tq), F32)],
        compiler_params=pltpu.CompilerParams(dimension_semantics=("parallel", "arbitrary"), vmem_limit_bytes=VMEM_LIMIT_WIDE),
    )(qs, k, vxt)


def _attn_delta(o, do, *, name):
    T = o.shape[0]
    tm = _tile(T, 512, 8)

    def body(o_ref, do_ref, d_ref):
        prod = o_ref[...] * do_ref[...].astype(F32)
        for h in range(A_HEADS):
            d_ref[h] = jnp.sum(prod[:, A_V * h:A_V * (h + 1)], axis=-1, keepdims=True)

    row = pl.BlockSpec((tm, A_HEADS * A_V), lambda i: (i, 0))
    return _pallas(body, name=name, grid=(T // tm,), in_specs=[row, row],
                   out_specs=pl.BlockSpec((A_HEADS, tm, 1), lambda i: (0, i, 0)),
                   out_shape=jax.ShapeDtypeStruct((A_HEADS, T, 1), F32), compiler_params=_params("parallel"))(o, do)


def _flash_bwd(qs, qst, k, kv, do, dot_, lse2, delta, *, tiles=None, name):
    T = qs.shape[0]
    tq, tk = tiles or (_tile(T, 1024), _tile(T, 1024))
    nq, nk = T // tq, T // tk
    H, P, DV = A_HEADS, A_PAD, A_V

    def body(q_ref, qt_ref, k_ref, v_ref, do_ref, dot_ref, lse_ref, delta_ref, dq_ref, dkt_ref, dvt_ref):
        i = pl.program_id(1)

        @pl.when(i == 0)
        def _():
            dkt_ref[...] = jnp.zeros(dkt_ref.shape, F32)
            dvt_ref[...] = jnp.zeros(dvt_ref.shape, F32)

        for h in range(H):
            s = _dot(q_ref[:, P * h:P * (h + 1)], k_ref[:, P * h:P * (h + 1)], _NT)
            dp = _dot(do_ref[:, DV * h:DV * (h + 1)], v_ref[:, DV * h:DV * (h + 1)], _NT)
            pb = jnp.exp2((s - lse_ref[h]).astype(BF16))
            ds = pb * (dp - delta_ref[h]).astype(BF16)
            dq_ref[0, :, P * h:P * (h + 1)] = _dot(ds, k_ref[:, P * h:P * (h + 1)], _NN).astype(BF16)
            dkt_ref[P * h:P * (h + 1), :] += _dot(qt_ref[P * h:P * (h + 1), :], ds, _NN)
            dvt_ref[DV * h:DV * (h + 1), :] += _dot(dot_ref[DV * h:DV * (h + 1), :], pb, _NN)

        @pl.when(i == nq - 1)
        def _():
            dkt_ref[...] = dkt_ref[...] * LN2

    qrow = lambda w: pl.BlockSpec((tq, w), lambda j, i: (i, 0))
    qcol = lambda r: pl.BlockSpec((r, tq), lambda j, i: (0, i))
    stat = pl.BlockSpec((H, tq, 1), lambda j, i: (0, i, 0))
    acc = lambda r: pl.BlockSpec((r, tk), lambda j, i: (0, j), pipeline_mode=pl.Buffered(1))
    return _pallas(
        body, name=name, grid=(nk, nq),
        in_specs=[qrow(A_W), qcol(A_W), pl.BlockSpec((tk, A_W), lambda j, i: (j, 0)), pl.BlockSpec((tk, H * DV), lambda j, i: (j, 1)),
                  qrow(H * DV), qcol(H * DV), stat, stat],
        out_specs=[pl.BlockSpec((1, tq, A_W), lambda j, i: (j, i, 0)), acc(A_W), acc(H * DV)],
        out_shape=[jax.ShapeDtypeStruct((nk, T, A_W), BF16), jax.ShapeDtypeStruct((A_W, T), F32),
                   jax.ShapeDtypeStruct((H * DV, T), F32)],
        compiler_params=pltpu.CompilerParams(dimension_semantics=("parallel", "arbitrary"), vmem_limit_bytes=VMEM_LIMIT_WIDE),
    )(qs, qst, k, kv, do, dot_, lse2, delta)


def _dq_sum(dq_part, cq, sq, *, name):
    n, T, W = dq_part.shape
    tm = _tile(T, 256, 16)

    def body(p_ref, c_ref, s_ref, o_ref):
        acc = p_ref[0].astype(F32)
        for j in range(1, n):
            acc = acc + p_ref[j].astype(F32)
        acc = acc * (A_QK ** -0.5)
        o_ref[:, 0:W] = (acc * jnp.concatenate([c_ref[...]] * A_HEADS, axis=1)).astype(BF16)
        o_ref[:, W:2 * W] = (acc * jnp.concatenate([s_ref[...]] * A_HEADS, axis=1)).astype(BF16)

    row = pl.BlockSpec((tm, A_PAD), lambda i: (i, 0))
    return _pallas(body, name=name, grid=(T // tm,), in_specs=[pl.BlockSpec((n, tm, W), lambda i: (0, i, 0)), row, row],
                   out_specs=pl.BlockSpec((tm, 2 * W), lambda i: (i, 0)), out_shape=jax.ShapeDtypeStruct((T, 2 * W), BF16),
                   compiler_params=_params("parallel"))(dq_part, cq, sq)


HB = 8 * B_CHUNK


def _chunk_masks(reverse):
    r = lax.broadcasted_iota(jnp.int32, (HB, HB), 0)
    c = lax.broadcasted_iota(jnp.int32, (HB, HB), 1)
    same = (r // B_CHUNK) == (c // B_CHUNK)
    incl = same & ((c >= r) if reverse else (c <= r))
    return same, incl


def _mask_mm(mask, x):
    hi = x.astype(BF16)
    lo = (x - hi.astype(F32)).astype(BF16)
    return _dot(mask, hi, _NN) + _dot(mask, lo, _NN)


def _hgrn_gates(q, z, lb, reverse):
    same, incl = _chunk_masks(reverse)
    sg = jax.nn.sigmoid(z)
    f = lb + (1.0 - lb) * sg
    lf = jnp.log(jnp.maximum(f, TINY))
    kk = (1.0 - lb) * jax.nn.sigmoid(-z)
    b = _mask_mm(incl.astype(BF16), lf)
    edge = 0 if reverse else B_CHUNK - 1
    btot = jnp.concatenate([jnp.broadcast_to(b[B_CHUNK * c + edge:B_CHUNK * c + edge + 1, :], (B_CHUNK, b.shape[1]))
                            for c in range(HB // B_CHUNK)], axis=0)
    eb, enb, er, dec = jnp.exp(b), jnp.exp(-b), jnp.exp(btot - b), jnp.exp(btot)
    return dict(same=same, incl=incl, sg=sg, f=f, kk=kk, eb=eb, enb=enb, er=er, dec=dec,
                qd=q * eb, ki=kk * enb, ke=kk * er)


def _hgrn_specs(T, reverse, gate_reverse):
    nb = T // HB
    blk = (lambda i: nb - 1 - i) if reverse else (lambda i: i)
    wide = B_HEADS * B_DK
    return nb, blk, [
        pl.BlockSpec((HB, wide), lambda i: (blk(i), 0)),
        pl.BlockSpec((HB, wide), lambda i: (blk(i), 2 if gate_reverse else 1)),
        pl.BlockSpec((HB, B_HEADS * B_DV), lambda i: (blk(i), 6)),
        pl.BlockSpec((1, wide), lambda i: (0, 0)),
    ]


def _hk(h):
    return slice(B_DK * h, B_DK * (h + 1))


def _hv(h):
    return slice(B_DV * h, B_DV * (h + 1))


def _crows(c):
    return slice(B_CHUNK * c, B_CHUNK * (c + 1))


def _chunk_selectors():
    r = lax.broadcasted_iota(jnp.int32, (HB, 1), 0) // B_CHUNK
    l = lax.broadcasted_iota(jnp.int32, (1, HB), 1) // B_CHUNK
    return [r == c for c in range(8)], [l == c for c in range(8)]


def _hgrn_fwd(zb, lb, *, reverse, name):
    T = zb.shape[0]
    nb, blk, in_specs = _hgrn_specs(T, reverse, reverse)
    order = range(7, -1, -1) if reverse else range(8)
    heads = range(B_HEADS)

    def body(q_ref, z_ref, v_ref, lb_ref, o_ref, st_ref, s_sc):
        @pl.when(pl.program_id(0) == 0)
        def _():
            s_sc[...] = jnp.zeros(s_sc.shape, F32)

        g = _hgrn_gates(q_ref[...], z_ref[...], lb_ref[...], reverse)
        v = v_ref[...].astype(BF16)
        qd, ki, ke = g["qd"].astype(BF16), g["ki"].astype(BF16), g["ke"].astype(BF16)
        dec = g["dec"]
        in_chunk_rows, in_chunk_lanes = _chunk_selectors()
        o_intra, upd = [], []
        for h in heads:
            a = jnp.where(g["incl"], _dot(qd[:, _hk(h)], ki[:, _hk(h)], _NT), 0.0)
            o_intra.append(_dot(a.astype(BF16), v[:, _hv(h)], _NN))
            vt = v[:, _hv(h)].T
            lhs = jnp.concatenate([jnp.where(in_chunk_lanes[c], vt, 0) for c in range(8)], axis=0)
            upd.append(_dot(lhs, ke[:, _hk(h)], _NN))
        st = [s_sc[h] for h in heads]
        snap = [[None] * 8 for _ in heads]
        for c in order:
            for h in heads:
                snap[h][c] = st[h]
                st[h] = st[h] * dec[B_CHUNK * c:B_CHUNK * c + 1, _hk(h)] + upd[h][B_DV * c:B_DV * (c + 1), :]
        for h in heads:
            s_sc[h] = st[h]
            for c in range(8):
                st_ref[h, c] = snap[h][c]
            qd_big = jnp.concatenate([jnp.where(in_chunk_rows[c], qd[:, _hk(h)], 0) for c in range(8)], axis=1)
            states = jnp.concatenate([snap[h][c].astype(BF16) for c in range(8)], axis=1)
            o_ref[h] = o_intra[h] + _dot(qd_big, states, _NT)

    return _pallas(
        body, name=name, grid=(nb,), in_specs=in_specs,
        out_specs=[pl.BlockSpec((B_HEADS, HB, B_DV), lambda i: (0, blk(i), 0)),
                   pl.BlockSpec((B_HEADS, 8, B_DV, B_DK), lambda i: (0, blk(i), 0, 0))],
        out_shape=[jax.ShapeDtypeStruct((B_HEADS, T, B_DV), F32),
                   jax.ShapeDtypeStruct((B_HEADS, T // B_CHUNK, B_DV, B_DK), F32)],
        scratch_shapes=[pltpu.VMEM((B_HEADS, B_DV, B_DK), F32)],
        compiler_params=_params("arbitrary"))(zb, zb, zb, lb)


def _hgrn_bwd(zb, lb, do, states, *, reverse, name):
    T = zb.shape[0]
    nb, blk, in_specs = _hgrn_specs(T, not reverse, reverse)
    order = range(8) if reverse else range(7, -1, -1)
    heads = range(B_HEADS)

    def body(q_ref, z_ref, v_ref, lb_ref, do_ref, st_ref, dq_ref, dz_ref, dv_ref, dlb_ref, ds_sc):
        @pl.when(pl.program_id(0) == 0)
        def _():
            ds_sc[...] = jnp.zeros(ds_sc.shape, F32)
            dlb_ref[...] = jnp.zeros(dlb_ref.shape, F32)

        lb = lb_ref[...]
        g = _hgrn_gates(q_ref[...], z_ref[...], lb, reverse)
        v = v_ref[...].astype(BF16)
        qd, ki, ke = g["qd"].astype(BF16), g["ki"].astype(BF16), g["ke"].astype(BF16)
        dec = g["dec"]
        dout = [do_ref[h].astype(BF16) for h in heads]
        in_chunk_rows, in_chunk_lanes = _chunk_selectors()
        _, incl_t = _chunk_masks(not reverse)
        rows_of = lambda x: jnp.concatenate([jnp.where(in_chunk_rows[c], x, 0) for c in range(8)], axis=1)
        dv_i, dqd_h, dki, upd = [], [], [], []
        for h in heads:
            qd_h, ki_h, v_h = qd[:, _hk(h)], ki[:, _hk(h)], v[:, _hv(h)]
            da = jnp.where(g["incl"], _dot(dout[h], v_h, _NT), 0.0).astype(BF16)
            at = jnp.where(incl_t, _dot(ki_h, qd_h, _NT), 0.0).astype(BF16)
            dat = jnp.where(incl_t, _dot(v_h, dout[h], _NT), 0.0).astype(BF16)
            dv_i.append(_dot(at, dout[h], _NN))
            dki.append(_dot(dat, qd_h, _NN))
            dot_t = dout[h].T
            lhs = jnp.concatenate([jnp.where(in_chunk_lanes[c], dot_t, 0) for c in range(8)], axis=0)
            upd.append(_dot(lhs, qd_h, _NN))
            saved = jnp.concatenate([st_ref[h, c].astype(BF16) for c in range(8)], axis=0)
            dqd_h.append(_dot(da, ki_h, _NN) + _dot(rows_of(dout[h]), saved, _NN))
        dst = [ds_sc[h] for h in heads]
        used = [[None] * 8 for _ in heads]
        for c in order:
            for h in heads:
                used[h][c] = dst[h]
                dst[h] = dst[h] * dec[B_CHUNK * c:B_CHUNK * c + 1, _hk(h)] + upd[h][B_DV * c:B_DV * (c + 1), :]
        dke_h, ddec_h = [], []
        for h in heads:
            ds_sc[h] = dst[h]
            used16 = [used[h][c].astype(BF16) for c in range(8)]
            dv_ref[h] = dv_i[h] + _dot(rows_of(ke[:, _hk(h)]), jnp.concatenate(used16, axis=1), _NT)
            dke_h.append(_dot(rows_of(v[:, _hv(h)]), jnp.concatenate(used16, axis=0), _NN))
            ddec_p = []
            for c in range(8):
                tot = jnp.sum(used[h][c] * st_ref[h, c], axis=0, keepdims=True) * dec[B_CHUNK * c:B_CHUNK * c + 1, _hk(h)]
                ddec_p.append(jnp.broadcast_to(tot, (B_CHUNK, B_DK)))
            ddec_h.append(jnp.concatenate(ddec_p, axis=0))
        dqd = jnp.concatenate(dqd_h, axis=1)
        dki = jnp.concatenate(dki, axis=1)
        dke = jnp.concatenate(dke_h, axis=1)
        db = dqd * g["qd"] - dki * g["ki"] - dke * g["ke"]
        masks = jnp.concatenate([incl_t.astype(BF16), g["same"].astype(BF16)], axis=1)
        dlf = _mask_mm(masks, jnp.concatenate([db, dke * g["ke"]], axis=0)) + jnp.concatenate(ddec_h, axis=1)
        dk = dki * g["enb"] + dke * g["er"]
        u = jnp.where(g["f"] > TINY, dlf / g["f"], 0.0) - dk
        sg = g["sg"]
        dq_ref[...] = dqd * g["eb"]
        dz_ref[...] = u * (1.0 - lb) * sg * (1.0 - sg)
        dlb_ref[...] += jnp.sum(u * (1.0 - sg), axis=0, keepdims=True)

    wide = pl.BlockSpec((HB, B_HEADS * B_DK), lambda i: (blk(i), 0))
    hm = pl.BlockSpec((B_HEADS, HB, B_DV), lambda i: (0, blk(i), 0))
    return _pallas(
        body, name=name, grid=(nb,),
        in_specs=in_specs + [hm, pl.BlockSpec((B_HEADS, 8, B_DV, B_DK), lambda i: (0, blk(i), 0, 0))],
        out_specs=[wide, wide, hm, pl.BlockSpec((1, B_HEADS * B_DK), lambda i: (0, 0))],
        out_shape=[jax.ShapeDtypeStruct((T, B_HEADS * B_DK), F32), jax.ShapeDtypeStruct((T, B_HEADS * B_DK), F32),
                   jax.ShapeDtypeStruct((B_HEADS, T, B_DV), F32), jax.ShapeDtypeStruct((1, B_HEADS * B_DK), F32)],
        scratch_shapes=[pltpu.VMEM((B_HEADS, B_DV, B_DK), F32)],
        compiler_params=_params("arbitrary"))(zb, zb, zb, lb, do, states)


def _hgrn_out(of, ob, zb, gout, *, name):
    T = zb.shape[0]
    tm = _tile(T, 512, 8)

    def body(of_ref, ob_ref, g_ref, gout_ref, y_ref):
        for h in range(B_HEADS):
            o = of_ref[h] + ob_ref[h]
            r = lax.rsqrt(jnp.mean(o * o, axis=-1, keepdims=True) + EPS)
            gh = g_ref[:, B_DV * h:B_DV * (h + 1)]
            y_ref[:, B_DV * h:B_DV * (h + 1)] = (o * r * gout_ref[...] * (gh * jax.nn.sigmoid(gh))).astype(BF16)

    hm = pl.BlockSpec((B_HEADS, tm, B_DV), lambda i: (0, i, 0))
    return _pallas(
        body, name=name, grid=(T // tm,),
        in_specs=[hm, hm, pl.BlockSpec((tm, 512), lambda i: (i, 7)), pl.BlockSpec((1, B_DV), lambda i: (0, 0))],
        out_specs=pl.BlockSpec((tm, 512), lambda i: (i, 0)),
        out_shape=jax.ShapeDtypeStruct((T, 512), BF16), compiler_params=_params("parallel"))(of, ob, zb, gout.reshape(1, B_DV))


def _hgrn_out_bwd(of, ob, zb, gout, dy, *, name):
    T = zb.shape[0]
    tm = _tile(T, 512, 8)

    def body(of_ref, ob_ref, g_ref, gout_ref, dy_ref, do_ref, dg_ref, dgo_ref):
        gout_v = gout_ref[...]
        acc = jnp.zeros((1, B_DV), F32)
        for h in range(B_HEADS):
            o = of_ref[h] + ob_ref[h]
            r = lax.rsqrt(jnp.mean(o * o, axis=-1, keepdims=True) + EPS)
            oh = o * r
            gh = g_ref[:, B_DV * h:B_DV * (h + 1)]
            sg = jax.nn.sigmoid(gh)
            dyh = dy_ref[:, B_DV * h:B_DV * (h + 1)].astype(F32)
            dn = dyh * (gh * sg)
            dg_ref[:, B_DV * h:B_DV * (h + 1)] = dyh * (oh * gout_v) * (sg * (1.0 + gh * (1.0 - sg)))
            dxh = dn * gout_v
            do_ref[h] = r * (dxh - oh * jnp.mean(dxh * oh, axis=-1, keepdims=True))
            acc = acc + jnp.sum(dn * oh, axis=0, keepdims=True)

        @pl.when(pl.program_id(0) == 0)
        def _():
            dgo_ref[...] = acc

        @pl.when(pl.program_id(0) > 0)
        def _():
            dgo_ref[...] += acc

    hm = pl.BlockSpec((B_HEADS, tm, B_DV), lambda i: (0, i, 0))
    row = pl.BlockSpec((tm, 512), lambda i: (i, 0))
    return _pallas(
        body, name=name, grid=(T // tm,),
        in_specs=[hm, hm, pl.BlockSpec((tm, 512), lambda i: (i, 7)), pl.BlockSpec((1, B_DV), lambda i: (0, 0)), row],
        out_specs=[hm, row, pl.BlockSpec((1, B_DV), lambda i: (0, 0))],
        out_shape=[jax.ShapeDtypeStruct((B_HEADS, T, B_DV), F32), jax.ShapeDtypeStruct((T, 512), F32),
                   jax.ShapeDtypeStruct((1, B_DV), F32)],
        compiler_params=_params("arbitrary"))(of, ob, zb, gout.reshape(1, B_DV), dy)


def _dzb_assemble(dq_f, dq_b, dzf, dzb_, dv_f, dv_b, dgate, *, name):
    T = dq_f.shape[0]
    tm = _tile(T, 256, 8)

    def body(qf, qb, zf, zr, vf, vr, dg, o_ref):
        o_ref[:, 0:1024] = (qf[...] + qb[...]).astype(BF16)
        o_ref[:, 1024:2048] = zf[...].astype(BF16)
        o_ref[:, 2048:3072] = zr[...].astype(BF16)
        for h in range(B_HEADS):
            o_ref[:, 3072 + B_DV * h:3072 + B_DV * (h + 1)] = (vf[h] + vr[h]).astype(BF16)
        o_ref[:, 3584:4096] = dg[...].astype(BF16)

    wide = pl.BlockSpec((tm, 1024), lambda i: (i, 0))
    hm = pl.BlockSpec((B_HEADS, tm, B_DV), lambda i: (0, i, 0))
    return _pallas(
        body, name=name, grid=(T // tm,),
        in_specs=[wide, wide, wide, wide, hm, hm, pl.BlockSpec((tm, 512), lambda i: (i, 0))],
        out_specs=pl.BlockSpec((tm, 4096), lambda i: (i, 0)),
        out_shape=jax.ShapeDtypeStruct((T, 4096), BF16), compiler_params=_params("parallel"))(dq_f, dq_b, dzf, dzb_, dv_f, dv_b, dgate)


C_SPAN = 3 * C_BLOCK
C_G = C_HEADS // C_KV_HEADS


def _t5_bucket(rel):
    nb = REL_BUCKETS // 2
    max_exact = nb // 2
    ret = (rel > 0).astype(jnp.int32) * nb
    n = jnp.abs(rel)
    large = max_exact + (jnp.log(jnp.maximum(n, 1).astype(F32) / max_exact)
                         / math.log(REL_MAX_DIST / max_exact) * (nb - max_exact)).astype(jnp.int32)
    large = jnp.minimum(large, nb - 1)
    return ret + jnp.where(n < max_exact, n, large)


def _swa_buckets():
    rel = jnp.arange(C_SPAN)[None, :] - C_BLOCK - jnp.arange(C_BLOCK)[:, None]
    return _t5_bucket(rel)


def _swa_specs(T):
    nb = T // C_BLOCK
    return nb, [
        pl.BlockSpec((C_BLOCK, 512), lambda n: (n, 0)),
        pl.BlockSpec((C_BLOCK, LANES), lambda n: (jnp.maximum(n - 1, 0), 4)),
        pl.BlockSpec((C_BLOCK, LANES), lambda n: (n, 4)),
        pl.BlockSpec((C_BLOCK, LANES), lambda n: (jnp.minimum(n + 1, nb - 1), 4)),
        pl.BlockSpec((C_BLOCK, LANES), lambda n: (jnp.maximum(n - 1, 0), 5)),
        pl.BlockSpec((C_BLOCK, LANES), lambda n: (n, 5)),
        pl.BlockSpec((C_BLOCK, LANES), lambda n: (jnp.minimum(n + 1, nb - 1), 5)),
        pl.BlockSpec((C_HEADS, C_BLOCK, C_SPAN), lambda n: (0, 0, 0)),
        pl.BlockSpec(memory_space=pltpu.SMEM),
    ]


def _swa_valid(n, T):
    qi = lax.broadcasted_iota(jnp.int32, (C_BLOCK, C_SPAN), 0)
    si = lax.broadcasted_iota(jnp.int32, (C_BLOCK, C_SPAN), 1)
    rel = si - C_BLOCK - qi
    kpos = (n - 1) * C_BLOCK + si
    return (jnp.abs(rel) <= C_WINDOW) & (kpos >= 0) & (kpos < T)


def _swa_softmax(raw, bias, valid, sink):
    s = raw * (C_DH ** -0.5) + bias
    s = jnp.where(valid, s, MASK_VALUE)
    m = jnp.maximum(jnp.max(s, axis=-1, keepdims=True), sink)
    e = jnp.exp(s - m)
    den = jnp.sum(e, axis=-1, keepdims=True) + jnp.exp(sink - m)
    return e / den, jnp.exp(sink - m) / den


def _swa_fwd(zc, bias, sink, *, name):
    T = zc.shape[0]
    nb, in_specs = _swa_specs(T)

    def body(q_ref, kp, kc, kn, vp, vc, vn, bias_ref, sink_ref, y_ref):
        n = pl.program_id(0)
        kcat = jnp.concatenate([kp[...], kc[...], kn[...]], axis=0)
        vcat = jnp.concatenate([vp[...], vc[...], vn[...]], axis=0)
        valid = _swa_valid(n, T)
        heads = range(C_HEADS)
        kvs = [slice(C_DH * (h // C_G), C_DH * (h // C_G + 1)) for h in heads]
        scores = [_dot(q_ref[:, C_DH * h:C_DH * (h + 1)], kcat[:, kvs[h]], _NT) for h in heads]
        probs = [_swa_softmax(scores[h], bias_ref[h], valid, sink_ref[h])[0].astype(BF16) for h in heads]
        for h in heads:
            y_ref[:, C_DH * h:C_DH * (h + 1)] = _dot(probs[h], vcat[:, kvs[h]], _NN).astype(BF16)

    return _pallas(
        body, name=name, grid=(nb,), in_specs=in_specs, out_specs=pl.BlockSpec((C_BLOCK, 512), lambda n: (n, 0)),
        out_shape=jax.ShapeDtypeStruct((T, 512), BF16), compiler_params=_params("parallel"))(zc, zc, zc, zc, zc, zc, zc, bias, sink)


def _swa_bwd(zc, bias, sink, dy, *, name):
    T = zc.shape[0]
    nb, in_specs = _swa_specs(T)
    scale = C_DH ** -0.5

    def body(q_ref, kp, kc, kn, vp, vc, vn, bias_ref, sink_ref, dy_ref, dq_ref, dkc_ref, dvc_ref, dbias_ref, dsink_ref):
        n = pl.program_id(0)

        @pl.when(n == 0)
        def _():
            dbias_ref[...] = jnp.zeros(dbias_ref.shape, F32)
            dsink_ref[...] = jnp.zeros(dsink_ref.shape, F32)

        kcat = jnp.concatenate([kp[...], kc[...], kn[...]], axis=0)
        vcat = jnp.concatenate([vp[...], vc[...], vn[...]], axis=0)
        valid = _swa_valid(n, T)
        heads = range(C_HEADS)
        kvs = [slice(C_DH * (h // C_G), C_DH * (h // C_G + 1)) for h in heads]
        qs = [q_ref[:, C_DH * h:C_DH * (h + 1)] for h in heads]
        dos = [dy_ref[:, C_DH * h:C_DH * (h + 1)].astype(BF16) for h in heads]
        scores = [_dot(qs[h], kcat[:, kvs[h]], _NT) for h in heads]
        dps = [_dot(dos[h], vcat[:, kvs[h]], _NT) for h in heads]
        pbs, dsbs = [], []
        for h in heads:
            p, p_sink = _swa_softmax(scores[h], bias_ref[h], valid, sink_ref[h])
            rowdot = jnp.sum(p * dps[h], axis=-1, keepdims=True)
            ds = p * (dps[h] - rowdot)
            dbias_ref[h] += ds
            tot = jnp.sum(jnp.sum(-p_sink * rowdot, axis=0, keepdims=True), axis=1, keepdims=True)
            dsink_ref[h:h + 1, :] += jnp.broadcast_to(tot, (1, LANES))
            pbs.append(p.astype(BF16))
            dsbs.append((ds * scale).astype(BF16))
        for h in heads:
            dq_ref[:, C_DH * h:C_DH * (h + 1)] = _dot(dsbs[h], kcat[:, kvs[h]], _NN).astype(BF16)
        dks = [_dot(dsbs[h], qs[h], _TN) for h in heads]
        dvs = [_dot(pbs[h], dos[h], _TN) for h in heads]
        for kv in range(C_KV_HEADS):
            group = range(kv * C_G, (kv + 1) * C_G)
            dkc_ref[0, :, C_DH * kv:C_DH * (kv + 1)] = sum(dks[h] for h in group)
            dvc_ref[0, :, C_DH * kv:C_DH * (kv + 1)] = sum(dvs[h] for h in group)

    part = pl.BlockSpec((1, C_SPAN, LANES), lambda n: (n, 0, 0))
    dq, dkc, dvc, dbias, dsink = _pallas(
        body, name=name, grid=(nb,), in_specs=in_specs + [pl.BlockSpec((C_BLOCK, 512), lambda n: (n, 0))],
        out_specs=[pl.BlockSpec((C_BLOCK, 512), lambda n: (n, 0)), part, part,
                   pl.BlockSpec((C_HEADS, C_BLOCK, C_SPAN), lambda n: (0, 0, 0)), pl.BlockSpec((C_HEADS, LANES), lambda n: (0, 0))],
        out_shape=[jax.ShapeDtypeStruct((T, 512), BF16), jax.ShapeDtypeStruct((nb, C_SPAN, LANES), F32),
                   jax.ShapeDtypeStruct((nb, C_SPAN, LANES), F32), jax.ShapeDtypeStruct((C_HEADS, C_BLOCK, C_SPAN), F32),
                   jax.ShapeDtypeStruct((C_HEADS, LANES), F32)],
        compiler_params=_params("arbitrary"))(zc, zc, zc, zc, zc, zc, zc, bias, sink, dy)

    def combine(dq_ref, kp, kc, kn, vp, vc, vn, o_ref):
        n = pl.program_id(0)
        lo = (n > 0).astype(F32)
        hi = (n < nb - 1).astype(F32)
        o_ref[:, 0:512] = dq_ref[...]
        o_ref[:, 512:640] = (kp[0] * lo + kc[0] + kn[0] * hi).astype(BF16)
        o_ref[:, 640:768] = (vp[0] * lo + vc[0] + vn[0] * hi).astype(BF16)

    prev = pl.BlockSpec((1, C_BLOCK, LANES), lambda n: (jnp.maximum(n - 1, 0), 2, 0))
    cur = pl.BlockSpec((1, C_BLOCK, LANES), lambda n: (n, 1, 0))
    nxt = pl.BlockSpec((1, C_BLOCK, LANES), lambda n: (jnp.minimum(n + 1, nb - 1), 0, 0))
    dzc = _pallas(
        combine, name=name + "_combine", grid=(nb,),
        in_specs=[pl.BlockSpec((C_BLOCK, 512), lambda n: (n, 0)), prev, cur, nxt, prev, cur, nxt],
        out_specs=pl.BlockSpec((C_BLOCK, 768), lambda n: (n, 0)),
        out_shape=jax.ShapeDtypeStruct((T, 768), BF16), compiler_params=_params("parallel"))(dq, dkc, dkc, dkc, dvc, dvc, dvc)
    return dzc, dbias, dsink


def _merge_tiles(T):
    return _tile(T, 512, 8), 512


def _merge_fwd(ya, yb, yc, wa, wb, wc, zg, *, name):
    T = ya.shape[0]
    tm, tn = _merge_tiles(T)
    nd = D_MODEL // tn

    def body(ya_ref, yb_ref, yc_ref, wa_ref, wb_ref, wc_ref, ga_ref, gb_ref, gc_ref, o_ref):
        acc = jax.nn.sigmoid(ga_ref[...].astype(F32)) * _dot(ya_ref[...].astype(BF16), wa_ref[...], _NN)
        acc += jax.nn.sigmoid(gb_ref[...].astype(F32)) * _dot(yb_ref[...].astype(BF16), wb_ref[...], _NN)
        acc += jax.nn.sigmoid(gc_ref[...].astype(F32)) * _dot(yc_ref[...].astype(BF16), wc_ref[...], _NN)
        o_ref[...] = acc.astype(BF16)

    y = pl.BlockSpec((tm, 512), lambda i, j: (i, 0))
    w = pl.BlockSpec((512, tn), lambda i, j: (0, j))
    gate = lambda b: pl.BlockSpec((tm, tn), lambda i, j: (i, b * nd + j))
    return _pallas(
        body, name=name, grid=(T // tm, nd), in_specs=[y, y, y, w, w, w, gate(0), gate(1), gate(2)],
        out_specs=pl.BlockSpec((tm, tn), lambda i, j: (i, j)),
        out_shape=jax.ShapeDtypeStruct((T, D_MODEL), BF16),
        compiler_params=_params("parallel", "parallel"))(ya, yb, yc, wa, wb, wc, zg, zg, zg)


def _merge_bwd(ya, yb, yc, wa, wb, wc, zg, dm, *, name):
    T = ya.shape[0]
    tm, tn = _merge_tiles(T)
    nd = D_MODEL // tn

    def body(ya_ref, yb_ref, yc_ref, wa_ref, wb_ref, wc_ref, ga_ref, gb_ref, gc_ref, dm_ref, *outs):
        dmv = dm_ref[...].astype(F32)
        for y_ref, w_ref, g_ref, du_ref, dg_ref in zip((ya_ref, yb_ref, yc_ref), (wa_ref, wb_ref, wc_ref),
                                                       (ga_ref, gb_ref, gc_ref), outs[:3], outs[3:]):
            u = _dot(y_ref[...].astype(BF16), w_ref[...], _NN)
            sg = jax.nn.sigmoid(g_ref[...].astype(F32))
            du_ref[...] = (dmv * sg).astype(BF16)
            dg_ref[...] = (dmv * u * sg * (1.0 - sg)).astype(BF16)

    y = pl.BlockSpec((tm, 512), lambda i, j: (i, 0))
    w = pl.BlockSpec((512, tn), lambda i, j: (0, j))
    gate = lambda b: pl.BlockSpec((tm, tn), lambda i, j: (i, b * nd + j))
    t = pl.BlockSpec((tm, tn), lambda i, j: (i, j))
    return _pallas(
        body, name=name, grid=(T // tm, nd), in_specs=[y, y, y, w, w, w, gate(0), gate(1), gate(2), t],
        out_specs=[t] * 6, out_shape=[jax.ShapeDtypeStruct((T, D_MODEL), BF16)] * 6,
        compiler_params=_params("parallel", "parallel"))(ya, yb, yc, wa, wb, wc, zg, zg, zg, dm)


def _cross_fwd(q, kvm, *, name):
    T = q.shape[0]
    M = kvm.shape[0]
    tm = _tile(T, 512, 8)
    scale = X_DH ** -0.5

    def body(q_ref, k_ref, v_ref, o_ref):
        for h in range(X_HEADS):
            cs = slice(X_DH * h, X_DH * (h + 1))
            s = _dot(q_ref[:, cs], k_ref[:, cs], _NT) * scale
            e = jnp.exp(s - jnp.max(s, axis=-1, keepdims=True))
            p = e / jnp.sum(e, axis=-1, keepdims=True)
            o_ref[:, cs] = _dot(p.astype(BF16), v_ref[:, cs], _NN).astype(BF16)

    row = pl.BlockSpec((tm, D_MODEL), lambda i: (i, 0))
    return _pallas(
        body, name=name, grid=(T // tm,),
        in_specs=[row, pl.BlockSpec((M, D_MODEL), lambda i: (0, 0)), pl.BlockSpec((M, D_MODEL), lambda i: (0, 1))],
        out_specs=row, out_shape=jax.ShapeDtypeStruct((T, D_MODEL), BF16), compiler_params=_params("parallel"))(q, kvm, kvm)


def _cross_bwd(q, kvm, do, *, name):
    T = q.shape[0]
    M = kvm.shape[0]
    tm = _tile(T, 512, 8)
    scale = X_DH ** -0.5

    def body(q_ref, k_ref, v_ref, do_ref, dq_ref, dkv_ref):
        @pl.when(pl.program_id(0) == 0)
        def _():
            dkv_ref[...] = jnp.zeros(dkv_ref.shape, F32)

        for h in range(X_HEADS):
            cs = slice(X_DH * h, X_DH * (h + 1))
            vs = slice(D_MODEL + X_DH * h, D_MODEL + X_DH * (h + 1))
            qh, kh, doh = q_ref[:, cs], k_ref[:, cs], do_ref[:, cs]
            s = _dot(qh, kh, _NT) * scale
            e = jnp.exp(s - jnp.max(s, axis=-1, keepdims=True))
            p = e / jnp.sum(e, axis=-1, keepdims=True)
            dp = _dot(doh, v_ref[:, cs], _NT)
            ds = (p * (dp - jnp.sum(p * dp, axis=-1, keepdims=True)) * scale).astype(BF16)
            dq_ref[:, cs] = _dot(ds, kh, _NN).astype(BF16)
            dkv_ref[:, cs] += _dot(ds, qh, _TN)
            dkv_ref[:, vs] += _dot(p.astype(BF16), doh, _TN)

    row = pl.BlockSpec((tm, D_MODEL), lambda i: (i, 0))
    return _pallas(
        body, name=name, grid=(T // tm,),
        in_specs=[row, pl.BlockSpec((M, D_MODEL), lambda i: (0, 0)), pl.BlockSpec((M, D_MODEL), lambda i: (0, 1)), row],
        out_specs=[row, pl.BlockSpec((M, 2 * D_MODEL), lambda i: (0, 0))],
        out_shape=[jax.ShapeDtypeStruct((T, D_MODEL), BF16), jax.ShapeDtypeStruct((M, 2 * D_MODEL), F32)],
        compiler_params=_params("arbitrary"))(q, kvm, kvm, do)


def _ffn_up(h, w1, w3, *, name):
    T = h.shape[0]
    tm = _tile(T, 512, 8)
    tn = D_FF // 2

    def body(h_ref, w1_ref, w3_ref, a_ref, b_ref, act_ref):
        hv = h_ref[...]
        a = _dot(hv, w1_ref[...], _NN)
        b = _dot(hv, w3_ref[...], _NN)
        a_ref[...] = a.astype(BF16)
        b_ref[...] = b.astype(BF16)
        act_ref[...] = (a * jax.nn.sigmoid(a) * b).astype(BF16)

    w = pl.BlockSpec((D_MODEL, tn), lambda i, j: (0, j))
    t = pl.BlockSpec((tm, tn), lambda i, j: (i, j))
    return _pallas(
        body, name=name, grid=(T // tm, D_FF // tn), in_specs=[pl.BlockSpec((tm, D_MODEL), lambda i, j: (i, 0)), w, w],
        out_specs=[t, t, t],
        out_shape=[jax.ShapeDtypeStruct((T, D_FF), BF16)] * 3,
        compiler_params=_params("parallel", "parallel"))(h, w1, w3)


def _ffn_dact(dx, w2, a, b, *, name):
    T = dx.shape[0]
    tm = _tile(T, 512, 8)
    tn = D_FF // 2

    def body(dx_ref, w2_ref, a_ref, b_ref, da_ref, db_ref):
        dact = _dot(dx_ref[...].astype(BF16), w2_ref[...], _NT)
        av = a_ref[...].astype(F32)
        sg = jax.nn.sigmoid(av)
        da_ref[...] = (dact * b_ref[...].astype(F32) * (sg * (1.0 + av * (1.0 - sg)))).astype(BF16)
        db_ref[...] = (dact * (av * sg)).astype(BF16)

    t = pl.BlockSpec((tm, tn), lambda i, j: (i, j))
    return _pallas(
        body, name=name, grid=(T // tm, D_FF // tn),
        in_specs=[pl.BlockSpec((tm, D_MODEL), lambda i, j: (i, 0)), pl.BlockSpec((tn, D_MODEL), lambda i, j: (j, 0)), t, t],
        out_specs=[t, t], out_shape=[jax.ShapeDtypeStruct((T, D_FF), BF16)] * 2,
        compiler_params=_params("parallel", "parallel"))(dx, w2, a, b)


def _loss_head(x, g, target, *, name):
    T, D = x.shape
    tm = _tile(T, 512, 8)

    def body(x_ref, g_ref, t_ref, loss_ref, dx_ref, dg_ref):
        xv = x_ref[...]
        r = lax.rsqrt(jnp.mean(xv * xv, axis=-1, keepdims=True) + EPS)
        xh = xv * r
        gv = g_ref[...]
        err = xh * gv - t_ref[...]
        dy = err * (1.0 / D)
        dxh = dy * gv
        dx_ref[...] = r * (dxh - xh * jnp.mean(dxh * xh, axis=-1, keepdims=True))
        lpart = 0.5 * jnp.sum(jnp.mean(err * err, axis=-1, keepdims=True), axis=0, keepdims=True)
        gpart = jnp.sum(dy * xh, axis=0, keepdims=True)

        @pl.when(pl.program_id(0) == 0)
        def _():
            loss_ref[...] = jnp.broadcast_to(lpart, (1, LANES))
            dg_ref[...] = gpart

        @pl.when(pl.program_id(0) > 0)
        def _():
            loss_ref[...] += jnp.broadcast_to(lpart, (1, LANES))
            dg_ref[...] += gpart

    row = pl.BlockSpec((tm, D), lambda i: (i, 0))
    vec = pl.BlockSpec((1, D), lambda i: (0, 0))
    return _pallas(
        body, name=name, grid=(T // tm,), in_specs=[row, vec, row],
        out_specs=[pl.BlockSpec((1, LANES), lambda i: (0, 0)), row, vec],
        out_shape=[jax.ShapeDtypeStruct((1, LANES), F32), jax.ShapeDtypeStruct((T, D), F32), jax.ShapeDtypeStruct((1, D), F32)],
        compiler_params=_params("arbitrary"))(x, g.reshape(1, D), target)


IN_CQ, IN_CKV, IN_KR, IN_B, IN_C, IN_G, IN_END = 0, 384, 640, 672, 4768, 5536, 8608
WEIGHT_NAMES = ("w_in", "g_mix", "a_gq", "a_gkv", "a_wuq", "a_wukv", "b_lb", "b_gout", "c_sink", "rel_bias",
                "w_br_a", "w_br_b", "w_br_c", "w_out", "g_x", "g_mem", "x_wq", "x_wkv", "x_wo", "g_ffn",
                "f_w1", "f_w3", "f_w2", "g_final")


def _lower_bounds(b_lb):
    sm = jax.nn.softmax(b_lb.astype(F32), axis=1)
    return jnp.cumsum(sm, axis=1) - sm[:, :1]


def _layer_weights(w, l):
    bf = lambda a: a.astype(BF16)
    w_in = bf(w["w_in"][l])
    kr = w_in[:, IN_KR:IN_B]
    wa = jnp.concatenate([w_in[:, IN_CQ:IN_CKV], kr, _rope_swap_cols(kr), jnp.zeros((D_MODEL, 64), BF16),
                          w_in[:, IN_CKV:IN_KR]], axis=1)
    wuq = bf(w["a_wuq"][l]).reshape(A_Q_RANK, A_HEADS, A_QK)
    zeros = lambda n: jnp.zeros((A_Q_RANK, A_HEADS, n), BF16)
    wuq_pad = jnp.concatenate([wuq, zeros(A_PAD - A_QK)], axis=-1)
    wuq_sw = jnp.concatenate([zeros(A_NOPE), _rope_swap_cols(wuq[..., A_NOPE:]), zeros(A_PAD - A_QK)], axis=-1)
    wq2 = jnp.concatenate([wuq_pad.reshape(A_Q_RANK, -1), wuq_sw.reshape(A_Q_RANK, -1)], axis=1)
    wukv = bf(w["a_wukv"][l]).reshape(A_KV_RANK, A_HEADS, A_NOPE + A_V)
    wkv = jnp.concatenate([wukv[..., :A_NOPE].reshape(A_KV_RANK, -1), wukv[..., A_NOPE:].reshape(A_KV_RANK, -1)], axis=1)
    return dict(wa=wa, wb=w_in[:, IN_B:IN_C], wc=w_in[:, IN_C:IN_G], wg=w_in[:, IN_G:IN_END], wq2=wq2, wkv=wkv,
                w_br_a=bf(w["w_br_a"][l]), w_br_b=bf(w["w_br_b"][l]), w_br_c=bf(w["w_br_c"][l]), w_out=bf(w["w_out"][l]),
                x_wq=bf(w["x_wq"][l]), x_wkv=bf(w["x_wkv"][l]), x_wo=bf(w["x_wo"][l]),
                f_w1=bf(w["f_w1"][l]), f_w3=bf(w["f_w3"][l]), f_w2=bf(w["f_w2"][l]))


def _layer_fwd(l, x, mem, w, lw, lower, bias, tabs):
    n = lambda s: f"l{l}_{s}"
    cq_t, sq_t, ck, _ = tabs
    s = dict(x=x)
    s["h0"] = h0 = _rms(x, w["g_mix"][l], name=n("rms_mix"))
    s["za"] = za = _mm(h0, lw["wa"], name=n("in_a"))
    s["zb"] = zb = _mm(h0, lw["wb"], name=n("in_b"))
    s["zc"] = zc = _mm(h0, lw["wc"], out_dtype=BF16, name=n("in_c"))
    s["zg"] = zg = _mm(h0, lw["wg"], out_dtype=BF16, name=n("in_g"))
    s["cqn"] = cqn = _rms(za, w["a_gq"][l], col=0, width=A_Q_RANK, name=n("rms_cq"))
    s["ckvn"] = ckvn = _rms(za, w["a_gkv"][l], col=512, width=A_KV_RANK, name=n("rms_ckv"))
    s["q"], s["qt"] = q, _ = _qrope(cqn, lw["wq2"], cq_t, sq_t, name=n("uq_rope"))
    s["kv"] = kv = _mm(ckvn, lw["wkv"], out_dtype=BF16, name=n("ukv"))
    s["k"], vxt = k, _ = _kprep(kv, za, ck, name=n("kprep"))
    s["ya"], s["lse"] = ya, _ = _flash_fwd(q, k, vxt, name=n("mla"))
    lb_f, lb_b = lower[0, l].reshape(1, -1), lower[1, l].reshape(1, -1)
    s["of"], s["stf"] = of, _ = _hgrn_fwd(zb, lb_f, reverse=False, name=n("hgrn_f"))
    s["ob"], s["stb"] = ob, _ = _hgrn_fwd(zb, lb_b, reverse=True, name=n("hgrn_b"))
    s["yb"] = yb = _hgrn_out(of, ob, zb, w["b_gout"][l], name=n("hgrn_out"))
    s["yc"] = yc = _swa_fwd(zc, bias, w["c_sink"][l], name=n("swa"))
    s["merged"] = merged = _merge_fwd(ya, yb, yc, lw["w_br_a"], lw["w_br_b"], lw["w_br_c"], zg, name=n("merge"))
    s["x1"] = x1 = _mm(merged, lw["w_out"], add=x, name=n("out"))
    s["h1"] = h1 = _rms(x1, w["g_x"][l], name=n("rms_x"))
    s["qx"] = qx = _mm(h1, lw["x_wq"], out_dtype=BF16, name=n("xq"))
    s["memn"] = memn = _rms(mem, w["g_mem"][l], name=n("rms_mem"))
    s["kvm"] = kvm = _mm(memn, lw["x_wkv"], out_dtype=BF16, name=n("xkv"))
    s["ox"] = ox = _cross_fwd(qx, kvm, name=n("cross"))
    s["x2"] = x2 = _mm(ox, lw["x_wo"], add=x1, name=n("xo"))
    s["h2"] = h2 = _rms(x2, w["g_ffn"][l], name=n("rms_ffn"))
    s["a"], s["b"], s["act"] = a, b, act = _ffn_up(h2, lw["f_w1"], lw["f_w3"], name=n("ffn_up"))
    x3 = _mm(act, lw["f_w2"], add=x2, name=n("ffn_down"))
    return x3, s


def _layer_bwd(l, dx3, mem, w, lw, lower, bias, tabs, s):
    n = lambda t: f"l{l}_b_{t}"
    cq_t, sq_t, _, ck_t = tabs
    g = {}
    da, db = _ffn_dact(dx3, lw["f_w2"], s["a"], s["b"], name=n("ffn_dact"))
    g["f_w2"] = _mm(s["act"], dx3, mode="tn", name=n("dw2"))
    dh2 = _mm(db, lw["f_w3"], mode="nt", add=_mm(da, lw["f_w1"], mode="nt", name=n("dh2a")), name=n("dh2b"))
    g["f_w1"] = _mm(s["h2"], da, mode="tn", name=n("dw1"))
    g["f_w3"] = _mm(s["h2"], db, mode="tn", name=n("dw3"))
    dx2, g["g_ffn"] = _rms_bwd(s["x2"], w["g_ffn"][l], dh2, res=dx3, name=n("rms_ffn"))
    dox = _mm(dx2, lw["x_wo"], mode="nt", out_dtype=BF16, name=n("dox"))
    g["x_wo"] = _mm(s["ox"], dx2, mode="tn", name=n("dwo"))
    dqx, dkvm = _cross_bwd(s["qx"], s["kvm"], dox, name=n("cross"))
    g["x_wq"] = _mm(s["h1"], dqx, mode="tn", name=n("dwq"))
    dh1 = _mm(dqx, lw["x_wq"], mode="nt", name=n("dh1"))
    g["x_wkv"] = _mm(s["memn"], dkvm, mode="tn", name=n("dwkv"))
    dmemn = _mm(dkvm, lw["x_wkv"], mode="nt", name=n("dmemn"))
    _, g["g_mem"] = _rms_bwd(mem, w["g_mem"][l], dmemn, name=n("rms_mem"))
    dx1, g["g_x"] = _rms_bwd(s["x1"], w["g_x"][l], dh1, res=dx2, name=n("rms_x"))
    dmerged = _mm(dx1, lw["w_out"], mode="nt", name=n("dmerged"))
    g["w_out"] = _mm(s["merged"], dx1, mode="tn", name=n("dwout"))
    dua, dub, duc, dga, dgb, dgc = _merge_bwd(s["ya"], s["yb"], s["yc"], lw["w_br_a"], lw["w_br_b"], lw["w_br_c"],
                                              s["zg"], dmerged, name=n("merge"))
    dya = _mm(dua, lw["w_br_a"], mode="nt", out_dtype=BF16, name=n("dya"))
    dyb = _mm(dub, lw["w_br_b"], mode="nt", name=n("dyb"))
    dyc = _mm(duc, lw["w_br_c"], mode="nt", out_dtype=BF16, name=n("dyc"))
    g["w_br_a"] = _mm(s["ya"], dua, mode="tn", name=n("dwbra"))
    g["w_br_b"] = _mm(s["yb"], dub, mode="tn", name=n("dwbrb"))
    g["w_br_c"] = _mm(s["yc"], duc, mode="tn", name=n("dwbrc"))
    dzc, dbias, dsink = _swa_bwd(s["zc"], bias, w["c_sink"][l], dyc, name=n("swa"))
    g["c_sink"] = dsink[:, 0]
    g["bias"] = dbias
    lb_f, lb_b = lower[0, l].reshape(1, -1), lower[1, l].reshape(1, -1)
    do_, dgate, dgout = _hgrn_out_bwd(s["of"], s["ob"], s["zb"], w["b_gout"][l], dyb, name=n("hgrn_out"))
    g["b_gout"] = dgout[0]
    dq_f, dzf, dv_f, dlb_f = _hgrn_bwd(s["zb"], lb_f, do_, s["stf"], reverse=False, name=n("hgrn_f"))
    dq_b, dzr, dv_b, dlb_b = _hgrn_bwd(s["zb"], lb_b, do_, s["stb"], reverse=True, name=n("hgrn_b"))
    g["lower"] = jnp.concatenate([dlb_f, dlb_b], axis=0)
    dzb = _dzb_assemble(dq_f, dq_b, dzf, dzr, dv_f, dv_b, dgate, name=n("dzb"))
    delta = _attn_delta(s["ya"], dya, name=n("mla_delta"))
    dq_part, dkt, dvt = _flash_bwd(s["q"], s["qt"], s["k"], s["kv"], dya, dya.T, s["lse"].reshape(A_HEADS, -1, 1), delta, name=n("mla"))
    dq2 = _dq_sum(dq_part, cq_t, sq_t, name=n("mla_dq"))
    dcqn = _mm(dq2, lw["wq2"], mode="nt", name=n("dcqn"))
    dwq2 = _mm(s["cqn"], dq2, mode="tn", name=n("dwq2")).reshape(A_Q_RANK, 2, A_HEADS, A_PAD)
    dknt, dzkrt = _kprep_bwd(dkt, ck_t, name=n("kprep"))
    wkv = lw["wkv"]
    dckvn = _mm(dvt, wkv[:, 512:].T, mode="tn", add=_mm(dknt, wkv[:, :512].T, mode="tn", name=n("dckvn_k")), name=n("dckvn_v"))
    dwkn = _mm(dknt, s["ckvn"], name=n("dwkn")).T
    dwv = _mm(dvt, s["ckvn"], name=n("dwv")).T
    dzcq, dgq = _rms_bwd(s["za"], w["a_gq"][l], dcqn, col=0, width=A_Q_RANK, out_dtype=BF16, name=n("rms_cq"))
    dzckv, dgkv = _rms_bwd(s["za"], w["a_gkv"][l], dckvn, col=512, width=A_KV_RANK, out_dtype=BF16, name=n("rms_ckv"))
    g["a_gq"], g["a_gkv"] = dgq[0], dgkv[0]
    sw = jnp.concatenate([jnp.zeros((A_Q_RANK, A_HEADS, A_NOPE), F32), _rope_unswap_cols(dwq2[:, 1, :, A_NOPE:A_QK])], axis=-1)
    g["a_wuq"] = (dwq2[:, 0, :, :A_QK] + sw).reshape(A_Q_RANK, -1)
    g["a_wukv"] = jnp.concatenate([dwkn.reshape(A_KV_RANK, A_HEADS, A_NOPE), dwv.reshape(A_KV_RANK, A_HEADS, A_V)], axis=-1).reshape(A_KV_RANK, -1)
    wa = lw["wa"]
    pieces = [(dzcq, wa[:, 0:384]), (dzckv, wa[:, 512:768]), (dzb, lw["wb"]), (dzc, lw["wc"]),
              (dga, lw["wg"][:, 0:1024]), (dgb, lw["wg"][:, 1024:2048]), (dgc, lw["wg"][:, 2048:3072])]
    dh0 = _mm(dzkrt, wa[:, 384:512].T, mode="tn", name=n("dh0_kr"))
    dwkr = _mm(dzkrt, s["h0"], name=n("dwin_kr")).T
    dwkr = dwkr[:, 0:A_ROPE] + _rope_unswap_cols(dwkr[:, A_ROPE:2 * A_ROPE])
    dws = []
    for i, (dz, wp) in enumerate(pieces):
        dh0 = _mm(dz, wp, mode="nt", add=dh0, name=n(f"dh0_{i}"))
        dws.append(_mm(s["h0"], dz, mode="tn", name=n(f"dwin_{i}")))
    g["w_in"] = jnp.concatenate([dws[0], dws[1], dwkr] + dws[2:], axis=1)
    dx, g["g_mix"] = _rms_bwd(s["x"], w["g_mix"][l], dh0, res=dx1, name=n("rms_mix"))
    return dx, g


def _local_step(x, mem, target, w):
    T = x.shape[0]
    tabs = _rope_tables(T)
    lower, lower_vjp = jax.vjp(_lower_bounds, w["b_lb"])
    buckets = _swa_buckets()
    onehot = (buckets.reshape(-1)[:, None] == jnp.arange(REL_BUCKETS)[None, :]).astype(F32)
    bias = jnp.dot(w["rel_bias"].astype(F32).T, onehot.T, precision=lax.Precision.HIGHEST).reshape(C_HEADS, C_BLOCK, C_SPAN)
    lws, saved = [], []
    for l in range(DEPTH):
        lws.append(_layer_weights(w, l))
        x, s = _layer_fwd(l, x, mem, w, lws[l], lower, bias, tabs)
        saved.append(s)
    loss, dx, dg_final = _loss_head(x, w["g_final"], target, name="loss_head")
    layer_grads = [None] * DEPTH
    for l in reversed(range(DEPTH)):
        dx, layer_grads[l] = _layer_bwd(l, dx, mem, w, lws[l], lower, bias, tabs, saved[l])
        saved[l] = None
    grads = {}
    for name in WEIGHT_NAMES:
        if name in layer_grads[0]:
            grads[name] = jnp.stack([layer_grads[l][name].reshape(w[name].shape[1:]) for l in range(DEPTH)])
    grads["g_final"] = dg_final[0]
    dlower = jnp.stack([layer_grads[l]["lower"] for l in range(DEPTH)], axis=1)
    grads["b_lb"] = lower_vjp(dlower)[0]
    dbias = layer_grads[0]["bias"] + layer_grads[1]["bias"]
    grads["rel_bias"] = jnp.dot(onehot.T, dbias.reshape(C_HEADS, -1).T, precision=lax.Precision.HIGHEST)
    return loss, dx, grads


N_CHIPS = 4
PACK_COLS = 1024
PACK_ALIGN = 32 * PACK_COLS
SHARDED = (("w_in", 2), ("a_wuq", 2), ("a_wukv", 2), ("b_lb", 2), ("w_br_a", 2), ("w_br_b", 2), ("w_br_c", 2), ("w_out", 1),
           ("x_wq", 1), ("x_wkv", 2), ("x_wo", 1), ("f_w1", 2), ("f_w3", 2), ("f_w2", 1))
REPLICATED = ("g_mix", "a_gq", "a_gkv", "b_gout", "c_sink", "rel_bias", "g_x", "g_mem", "g_ffn", "g_final")
MESH_IDS = pl.DeviceIdType.MESH
ANY_SPEC = pl.BlockSpec(memory_space=pl.ANY)


def _pack_pieces(arrs, cols, align):
    pieces = [a.reshape(-1, cols) for a in arrs]
    pad = (-sum(p.size for p in pieces)) % align
    return pieces + ([jnp.zeros((pad // cols, cols), pieces[0].dtype)] if pad else [])


def _pack(arrs, cols, align):
    return jnp.concatenate(_pack_pieces(arrs, cols, align), axis=0)


def _pack_small(arrs):
    flat = jnp.concatenate([a.reshape(-1) for a in arrs])
    return jnp.pad(flat, (0, (-flat.shape[0]) % (8 * LANES))).reshape(-1, LANES)


def _unpack(buf, shapes):
    cols = buf.shape[-1]
    buf = buf.reshape(-1, cols)
    by_rows = all(math.prod(shp) % cols == 0 for shp in shapes)
    flat = None if by_rows else buf.reshape(-1)
    out, start = [], 0
    for shp in shapes:
        size = math.prod(shp)
        piece = buf[start // cols:(start + size) // cols] if by_rows else flat[start:start + size]
        out.append(piece.reshape(shp))
        start += size
    return out


def _chip_peers():
    x, y, c = lax.axis_index("x"), lax.axis_index("y"), lax.axis_index("c")
    return x, y, c, [(1 - x, y), (x, 1 - y), (1 - x, 1 - y)]


def _chip_gather(src, chip, *, name):
    _, R, C = src.shape

    def body(src_ref, out_ref, send_sems, recv_sems, pass_send_sems, pass_recv_sems):
        x, y, c, chips = _chip_peers()
        me = 2 * x + y
        sibling = (x, y, 1 - c)

        def over_ici(j, slot):
            px, py = chips[j]
            return pltpu.make_async_remote_copy(src_ref=src_ref.at[c], dst_ref=out_ref.at[slot, c], send_sem=send_sems.at[j],
                                                recv_sem=recv_sems.at[j], device_id=(px, py, c), device_id_type=MESH_IDS)

        def pass_on(j, half):
            px, py = chips[j]
            piece = out_ref.at[2 * px + py, half]
            return pltpu.make_async_remote_copy(src_ref=piece, dst_ref=piece, send_sem=pass_send_sems.at[j],
                                                recv_sem=pass_recv_sems.at[j], device_id=sibling, device_id_type=MESH_IDS)

        sends = [over_ici(j, me) for j in range(3)]
        for cp in sends:
            cp.start()
        passed = []
        for j, (px, py) in enumerate(chips):
            over_ici(j, 2 * px + py).wait_recv()
            passed.append(pass_on(j, c))
            passed[j].start()
        for j in range(3):
            pass_on(j, 1 - c).wait_recv()
        for cp in sends + passed:
            cp.wait_send()

    gathered = _pallas(
        body, name=name, in_specs=[ANY_SPEC], out_specs=ANY_SPEC, out_shape=jax.ShapeDtypeStruct((N_CHIPS, 2, R, C), src.dtype),
        scratch_shapes=[pltpu.SemaphoreType.DMA((3,))] * 4,
        compiler_params=pltpu.CompilerParams(has_side_effects=True))(src)

    tr = _tile(R, 512, 16)

    def place(chip_ref, gathered_ref, own_ref, out_ref):
        out_ref[0, 0] = own_ref[0]

    grid_spec = pltpu.PrefetchScalarGridSpec(
        num_scalar_prefetch=1, grid=(2, R // tr),
        in_specs=[ANY_SPEC, pl.BlockSpec((1, tr, C), lambda h, i, chip_ref: (h, i, 0))],
        out_specs=pl.BlockSpec((1, 1, tr, C), lambda h, i, chip_ref: (chip_ref[0], h, i, 0)))
    return _pallas(place, name=name + "_own", grid_spec=grid_spec, out_shape=jax.ShapeDtypeStruct(gathered.shape, gathered.dtype),
                   input_output_aliases={1: 0}, compiler_params=_params("arbitrary", "arbitrary"))(chip, gathered, src)


def _chip_scatter(src, *, name):
    _, R, C = src.shape

    def body(src_ref, out_ref, send_sems, recv_sems):
        x, y, c, chips = _chip_peers()
        me = 2 * x + y

        def copy(j, seg):
            px, py = chips[j]
            return pltpu.make_async_remote_copy(src_ref=src_ref.at[seg], dst_ref=out_ref.at[j], send_sem=send_sems.at[j],
                                                recv_sem=recv_sems.at[j], device_id=(px, py, c), device_id_type=MESH_IDS)

        sends = [copy(j, 2 * px + py) for j, (px, py) in enumerate(chips)]
        for cp in sends:
            cp.start()
        for j in range(3):
            copy(j, me).wait_recv()
        for cp in sends:
            cp.wait_send()

    return _pallas(
        body, name=name, in_specs=[ANY_SPEC], out_specs=ANY_SPEC, out_shape=jax.ShapeDtypeStruct((3, R, C), src.dtype),
        scratch_shapes=[pltpu.SemaphoreType.DMA((3,)), pltpu.SemaphoreType.DMA((3,))],
        compiler_params=pltpu.CompilerParams(has_side_effects=True))(src)


PAIR_CHUNKS = 4


def _pair_swap(src, *, halves, name):
    R, C = src.shape[-2:]
    n = N_CHIPS if halves else 1
    rc = R // PAIR_CHUNKS
    assert rc * PAIR_CHUNKS == R and rc % 16 == 0, R

    def body(src_ref, out_ref, send_sems, recv_sems):
        x, y, c = lax.axis_index("x"), lax.axis_index("y"), lax.axis_index("c")
        copies = []
        for k in range(n):
            for r in range(PAIR_CHUNKS):
                rows = pl.ds(r * rc, rc)
                s = src_ref.at[k, 1 - c, rows] if halves else src_ref.at[rows]
                d = out_ref.at[k, rows] if halves else out_ref.at[rows]
                i = k * PAIR_CHUNKS + r
                copies.append(pltpu.make_async_remote_copy(src_ref=s, dst_ref=d, send_sem=send_sems.at[i], recv_sem=recv_sems.at[i],
                                                           device_id=(x, y, 1 - c), device_id_type=MESH_IDS))
        for cp in copies:
            cp.start()
        for cp in copies:
            cp.wait_recv()
        for cp in copies:
            cp.wait_send()

    shape = (N_CHIPS, R, C) if halves else (R, C)
    return _pallas(
        body, name=name, in_specs=[ANY_SPEC], out_specs=ANY_SPEC, out_shape=jax.ShapeDtypeStruct(shape, src.dtype),
        scratch_shapes=[pltpu.SemaphoreType.DMA((n * PAIR_CHUNKS,)), pltpu.SemaphoreType.DMA((n * PAIR_CHUNKS,))],
        compiler_params=pltpu.CompilerParams(has_side_effects=True))(src)


def _pair_add(g4, got, c, *, name):
    _, _, R, C = g4.shape
    tr = _tile(R, 512, 16)

    def body(c_ref, mine_ref, got_ref, o_ref, ob_ref):
        s = mine_ref[0, 0] + got_ref[0].astype(F32)
        o_ref[0] = s
        ob_ref[0] = s.astype(BF16)

    blk = pl.BlockSpec((1, tr, C), lambda k, i, c_ref: (k, i, 0))
    grid_spec = pltpu.PrefetchScalarGridSpec(
        num_scalar_prefetch=1, grid=(N_CHIPS, R // tr),
        in_specs=[pl.BlockSpec((1, 1, tr, C), lambda k, i, c_ref: (k, c_ref[0], i, 0)), blk], out_specs=[blk, blk])
    return _pallas(body, name=name, grid_spec=grid_spec,
                   out_shape=[jax.ShapeDtypeStruct((N_CHIPS, R, C), F32), jax.ShapeDtypeStruct((N_CHIPS, R, C), BF16)],
                   compiler_params=_params("parallel", "parallel"))(c, g4, got)


def _chip_sum(pair_sum, landed, me, *, name):
    _, R, C = pair_sum.shape
    tr = _tile(R, 512, 16)

    def body(me_ref, own_ref, landed_ref, o_ref):
        acc = own_ref[0]
        for j in range(3):
            acc = acc + landed_ref[j].astype(F32)
        o_ref[...] = acc

    grid_spec = pltpu.PrefetchScalarGridSpec(
        num_scalar_prefetch=1, grid=(R // tr,),
        in_specs=[pl.BlockSpec((1, tr, C), lambda i, me_ref: (me_ref[0], i, 0)), pl.BlockSpec((3, tr, C), lambda i, me_ref: (0, i, 0))],
        out_specs=pl.BlockSpec((tr, C), lambda i, me_ref: (i, 0)))
    return _pallas(body, name=name, grid_spec=grid_spec, out_shape=jax.ShapeDtypeStruct((R, C), F32),
                   compiler_params=_params("parallel"))(me, pair_sum, landed)


def _join_halves(mine, got, c, *, name):
    R, C = mine.shape
    tr = _tile(R, 512, 16)

    def body(c_ref, mine_ref, got_ref, o_ref):
        use_mine = pl.program_id(0) == c_ref[0]
        o_ref[0] = jnp.where(use_mine, mine_ref[...], got_ref[...])

    blk = pl.BlockSpec((tr, C), lambda h, i, c_ref: (i, 0))
    grid_spec = pltpu.PrefetchScalarGridSpec(num_scalar_prefetch=1, grid=(2, R // tr), in_specs=[blk, blk],
                                             out_specs=pl.BlockSpec((1, tr, C), lambda h, i, c_ref: (h, i, 0)))
    return _pallas(body, name=name, grid_spec=grid_spec, out_shape=jax.ShapeDtypeStruct((2, R, C), mine.dtype),
                   compiler_params=_params("parallel", "parallel"))(c, mine, got).reshape(2 * R, C)


def _gather8(s, *, name):
    R, C = s.shape

    def body(s_ref, out_ref, send_sems, recv_sems):
        x, y, c = lax.axis_index("x"), lax.axis_index("y"), lax.axis_index("c")
        me = 4 * x + 2 * y + c
        flips = [(dx, dy, dc) for dx in (0, 1) for dy in (0, 1) for dc in (0, 1)][1:]
        out_ref[me] = s_ref[...]

        def copy(j, slot):
            dx, dy, dc = flips[j]
            return pltpu.make_async_remote_copy(src_ref=s_ref, dst_ref=out_ref.at[slot], send_sem=send_sems.at[j],
                                                recv_sem=recv_sems.at[j], device_id=(x ^ dx, y ^ dy, c ^ dc), device_id_type=MESH_IDS)

        sends = [copy(j, me) for j in range(7)]
        for cp in sends:
            cp.start()
        for j, (dx, dy, dc) in enumerate(flips):
            copy(j, 4 * (x ^ dx) + 2 * (y ^ dy) + (c ^ dc)).wait_recv()
        for cp in sends:
            cp.wait_send()

    vmem = pl.BlockSpec(memory_space=pltpu.VMEM)
    return _pallas(
        body, name=name, in_specs=[vmem], out_specs=vmem, out_shape=jax.ShapeDtypeStruct((8, R, C), s.dtype),
        scratch_shapes=[pltpu.SemaphoreType.DMA((7,)), pltpu.SemaphoreType.DMA((7,))],
        compiler_params=pltpu.CompilerParams(has_side_effects=True))(s)


def _sum_slots(a, *, name):
    n, R, C = a.shape
    tr = _tile(R, 512, 8)

    def body(a_ref, o_ref):
        acc = a_ref[0]
        for k in range(1, n):
            acc = acc + a_ref[k]
        o_ref[...] = acc

    return _pallas(body, name=name, grid=(R // tr,), in_specs=[pl.BlockSpec((n, tr, C), lambda i: (0, i, 0))],
                   out_specs=pl.BlockSpec((tr, C), lambda i: (i, 0)), out_shape=jax.ShapeDtypeStruct((R, C), a.dtype),
                   compiler_params=_params("parallel"))(a)


def _adamw(w, g, m, v, *, name):
    R, C = w.shape
    tr = _tile(R, max(8, (1 << 18) // C // 8 * 8), 8)
    c1 = 1.0 / (1.0 - ADAM_B1 ** ADAM_STEP)
    c2 = 1.0 / (1.0 - ADAM_B2 ** ADAM_STEP)

    def body(w_ref, g_ref, m_ref, v_ref, d_ref, nm_ref, nv_ref):
        gv = g_ref[...]
        nm = ADAM_B1 * m_ref[...] + (1.0 - ADAM_B1) * gv
        nv = ADAM_B2 * v_ref[...] + (1.0 - ADAM_B2) * (gv * gv)
        d_ref[...] = -ADAM_LR * ((nm * c1) / (jnp.sqrt(nv * c2) + ADAM_EPS) + ADAM_WD * w_ref[...])
        nm_ref[...] = nm
        nv_ref[...] = nv

    blk = pl.BlockSpec((tr, C), lambda i: (i, 0))
    shape = jax.ShapeDtypeStruct((R, C), F32)
    return _pallas(body, name=name, grid=(R // tr,), in_specs=[blk] * 4, out_specs=[blk] * 3, out_shape=[shape] * 3,
                   compiler_params=_params("parallel"))(w, g, m, v)


def kernel(x, mem, w_in, g_mix, a_gq, a_gkv, a_wuq, a_wukv, b_lb, b_gout, c_sink, rel_bias, w_br_a, w_br_b, w_br_c, w_out, g_x, g_mem, x_wq, x_wkv, x_wo, g_ffn, f_w1, f_w3, f_w2, g_final, loss_target, m_w_in, m_g_mix, m_a_gq, m_a_gkv, m_a_wuq, m_a_wukv, m_b_lb, m_b_gout, m_c_sink, m_rel_bias, m_w_br_a, m_w_br_b, m_w_br_c, m_w_out, m_g_x, m_g_mem, m_x_wq, m_x_wkv, m_x_wo, m_g_ffn, m_f_w1, m_f_w3, m_f_w2, m_g_final, v_w_in, v_g_mix, v_a_gq, v_a_gkv, v_a_wuq, v_a_wukv, v_b_lb, v_b_gout, v_c_sink, v_rel_bias, v_w_br_a, v_w_br_b, v_w_br_c, v_w_out, v_g_x, v_g_mem, v_x_wq, v_x_wkv, v_x_wo, v_g_ffn, v_f_w1, v_f_w3, v_f_w2, v_g_final):
    ws = dict(zip(WEIGHT_NAMES, (w_in, g_mix, a_gq, a_gkv, a_wuq, a_wukv, b_lb, b_gout, c_sink, rel_bias, w_br_a, w_br_b, w_br_c,
                                 w_out, g_x, g_mem, x_wq, x_wkv, x_wo, g_ffn, f_w1, f_w3, f_w2, g_final)))
    ms = dict(zip(WEIGHT_NAMES, (m_w_in, m_g_mix, m_a_gq, m_a_gkv, m_a_wuq, m_a_wukv, m_b_lb, m_b_gout, m_c_sink, m_rel_bias,
                                 m_w_br_a, m_w_br_b, m_w_br_c, m_w_out, m_g_x, m_g_mem, m_x_wq, m_x_wkv, m_x_wo, m_g_ffn,
                                 m_f_w1, m_f_w3, m_f_w2, m_g_final)))
    vs = dict(zip(WEIGHT_NAMES, (v_w_in, v_g_mix, v_a_gq, v_a_gkv, v_a_wuq, v_a_wukv, v_b_lb, v_b_gout, v_c_sink, v_rel_bias,
                                 v_w_br_a, v_w_br_b, v_w_br_c, v_w_out, v_g_x, v_g_mem, v_x_wq, v_x_wkv, v_x_wo, v_g_ffn,
                                 v_f_w1, v_f_w3, v_f_w2, v_g_final)))
    sharded = [n for n, _ in SHARDED]
    axis_of = dict(SHARDED)

    def wire(n):
        return lax.bitcast_convert_type(ws[n], BF16) if n == "b_lb" else ws[n].astype(BF16)

    core = lax.axis_index("c").astype(jnp.int32).reshape(1)
    chip = (2 * lax.axis_index("x") + lax.axis_index("y")).astype(jnp.int32).reshape(1)
    wire_shapes = [wire(n).shape for n in sharded]
    packed = _pack([wire(n) for n in sharded], PACK_COLS, PACK_ALIGN)
    gathered = _chip_gather(packed.reshape(2, packed.shape[0] // 2, PACK_COLS), chip, name="gather_weights")
    per_chip = [_unpack(gathered[k], wire_shapes) for k in range(N_CHIPS)]
    full = dict(ws)
    for i, n in enumerate(sharded):
        parts = [per_chip[k][i] for k in range(N_CHIPS)]
        if n == "b_lb":
            parts = [lax.bitcast_convert_type(p, F32) for p in parts]
        full[n] = jnp.concatenate(parts, axis=axis_of[n])

    loss, grad_x, grads = _local_step(x[0], mem[0], loss_target[0], full)

    pieces = []
    for k in range(N_CHIPS):
        pieces += _pack_pieces([jnp.split(grads[n], N_CHIPS, axis=axis_of[n])[k] for n in sharded], PACK_COLS, PACK_ALIGN)
    g4 = jnp.concatenate(pieces, axis=0).reshape(N_CHIPS, 2, -1, PACK_COLS)
    got = _pair_swap(g4, halves=True, name="reduce_pair_swap")
    pair_sum, pair_sum_wire = _pair_add(g4, got, core, name="reduce_pair_add")
    landed = _chip_scatter(pair_sum_wire, name="reduce_chip_scatter")
    mine = _chip_sum(pair_sum, landed, chip, name="reduce_chip_sum")
    g_shard = _join_halves(mine, _pair_swap(mine, halves=False, name="reduce_pair_join"), core, name="reduce_join_halves")

    small = _pack_small([grads[n] for n in REPLICATED] + [loss[0, 0:1]])
    small_sum = _sum_slots(_gather8(small, name="gather_small"), name="sum_small")
    small_grads = _unpack(small_sum, [ws[n].shape for n in REPLICATED] + [(1,)])
    loss_total = small_grads.pop()[0]

    shard_shapes = [ws[n].shape for n in sharded]
    out = {}
    for n, gr in zip(sharded, _unpack(g_shard, shard_shapes)):
        flat2 = lambda a: a.reshape(-1, a.shape[-1])
        d, nm, nv = _adamw(flat2(ws[n]), flat2(gr), flat2(ms[n]), flat2(vs[n]), name="adamw_" + n)
        out[n] = (gr, d.reshape(gr.shape), nm.reshape(gr.shape), nv.reshape(gr.shape))
    pk_s = lambda d: _pack_small([d[n] for n in REPLICATED])
    rep_shapes = [ws[n].shape for n in REPLICATED]
    gs_flat = _pack_small(small_grads)
    ds_flat, ms_flat, vs_flat = _adamw(pk_s(ws), gs_flat, pk_s(ms), pk_s(vs), name="adamw_replicated")
    for n, gr, d, nm, nv in zip(REPLICATED, small_grads, _unpack(ds_flat, rep_shapes), _unpack(ms_flat, rep_shapes),
                                _unpack(vs_flat, rep_shapes)):
        out[n] = (gr, d, nm, nv)
    return (loss_total, grad_x[None], *[out[n][0] for n in WEIGHT_NAMES], *[out[n][1] for n in WEIGHT_NAMES],
            *[out[n][2] for n in WEIGHT_NAMES], *[out[n][3] for n in WEIGHT_NAMES])
```
